```python
import jax, jax.numpy as jnp
from jax import lax
import numpy as np

D_MODEL = 4096
BATCH = 8
SEQ = 4096
DEPTH = 1

CTX_LEN = 256
GRID_W = 64
MIX_WIDTH = D_MODEL
ATTN_WIDTH = MIX_WIDTH // 2
POOL_WIDTH = MIX_WIDTH - ATTN_WIDTH
HEAD_DIM = 128
N_HEADS = ATTN_WIDTH // HEAD_DIM
N_KV_HEADS = max(1, N_HEADS // 4)
GQA_GROUP = N_HEADS // N_KV_HEADS
KV_WIDTH = N_KV_HEADS * HEAD_DIM
ROPE_PAIRS = HEAD_DIM // 4
ROPE_THETA = 10000.0
ATTN_SCALE = HEAD_DIM ** -0.5
Q_BLOCK = 128
POOL_WINDOWS = (2, 4, 8, 16)
N_POOL_GROUPS = len(POOL_WINDOWS)
POOL_GROUP = POOL_WIDTH // N_POOL_GROUPS
EPS = 1e-6
IN_WIDTH = ATTN_WIDTH + 2 * KV_WIDTH + ATTN_WIDTH + POOL_WIDTH + POOL_WIDTH
SPLITS = [ATTN_WIDTH,
          ATTN_WIDTH + KV_WIDTH,
          ATTN_WIDTH + 2 * KV_WIDTH,
          2 * ATTN_WIDTH + 2 * KV_WIDTH,
          2 * ATTN_WIDTH + 2 * KV_WIDTH + POOL_WIDTH]

kernel_name = "hymba_gqa_pool_prefix_dit_layer"


def rms_norm(x, gain):
    xf = x.astype(jnp.float32)
    y = xf * lax.rsqrt(jnp.mean(xf * xf, axis=-1, keepdims=True) + EPS)
    return (y * gain.astype(jnp.float32)).astype(x.dtype)


def axial_rope_tables(n_tokens):
    rows = n_tokens // GRID_W
    row = jnp.repeat(jnp.arange(rows, dtype=jnp.float32), GRID_W)
    col = jnp.tile(jnp.arange(GRID_W, dtype=jnp.float32), rows)
    inv = ROPE_THETA ** (-jnp.arange(ROPE_PAIRS, dtype=jnp.float32) / ROPE_PAIRS)
    ang = jnp.concatenate([row[:, None] * inv, col[:, None] * inv], axis=-1)
    return jnp.cos(ang), jnp.sin(ang)


def apply_axial_rope(x, cos, sin):
    xf = x.astype(jnp.float32)
    c = cos[None, :, None, :]
    s = sin[None, :, None, :]

    def rot(v, cc, ss):
        v1, v2 = jnp.split(v, 2, axis=-1)
        return jnp.concatenate([v1 * cc - v2 * ss, v1 * ss + v2 * cc], axis=-1)

    x_row, x_col = jnp.split(xf, 2, axis=-1)
    out = jnp.concatenate([rot(x_row, c[..., :ROPE_PAIRS], s[..., :ROPE_PAIRS]),
                           rot(x_col, c[..., ROPE_PAIRS:], s[..., ROPE_PAIRS:])], axis=-1)
    return out.astype(x.dtype)


def attention_scores_to_out(q_blk, k_all, v_all):
    s = jnp.einsum('bqkgd,bskd->bkgqs', q_blk, k_all).astype(jnp.float32) * ATTN_SCALE
    p = jax.nn.softmax(s, axis=-1).astype(v_all.dtype)
    return jnp.einsum('bkgqs,bskd->bqkgd', p, v_all)


def latent_attention(q, k_all, v_all):
    b, n = q.shape[0], q.shape[1]
    nb = n // Q_BLOCK
    qb = q.reshape(b, nb, Q_BLOCK, N_KV_HEADS, GQA_GROUP, HEAD_DIM).transpose(1, 0, 2, 3, 4, 5)
    o = lax.map(lambda q_blk: attention_scores_to_out(q_blk, k_all, v_all), qb)
    return o.transpose(1, 0, 2, 3, 4, 5).reshape(b, n, ATTN_WIDTH)


def context_attention(q, k, v):
    b, l = q.shape[0], q.shape[1]
    qg = q.reshape(b, l, N_KV_HEADS, GQA_GROUP, HEAD_DIM)
    return attention_scores_to_out(qg, k, v).reshape(b, l, ATTN_WIDTH)


def multiscale_pool(u, pool_w, pool_scale):
    n = u.shape[1]
    t = jnp.arange(n)
    cs = jnp.pad(jnp.cumsum(u.astype(jnp.float32), axis=1), ((0, 0), (1, 0), (0, 0)))
    outs = []
    for gi, w in enumerate(POOL_WINDOWS):
        half = w // 2
        lo = jnp.clip(t - half, 0, n)
        hi = jnp.clip(t + half, 0, n)
        csg = cs[..., gi * POOL_GROUP:(gi + 1) * POOL_GROUP]
        win = jnp.take(csg, hi, axis=1) - jnp.take(csg, lo, axis=1)
        cnt = (hi - lo).astype(jnp.float32)[None, :, None]
        ug = u[..., gi * POOL_GROUP:(gi + 1) * POOL_GROUP].astype(jnp.float32)
        d = (win / cnt - ug).astype(u.dtype)
        outs.append(jnp.einsum('bnc,cd->bnd', d, pool_w[gi]))
    return jnp.concatenate(outs, axis=-1) * pool_scale


def merge_branches(attn_o, g_attn, u_pool, g_pool, pool_w, pool_scale, w_out):
    pool_o = multiscale_pool(u_pool, pool_w, pool_scale)
    y = jnp.concatenate([attn_o * jax.nn.silu(g_attn), pool_o * jax.nn.silu(g_pool)], axis=-1)
    return y @ w_out


def hybrid_layer(x, x_ctx, c_act, cc_act, w_ada, b_ada, g_pre, g_post, w_in,
                 g_q, g_k, pool_w, pool_scale, w_out, cos, sin, update_ctx):
    b, n = x.shape[0], x.shape[1]
    l = x_ctx.shape[1]
    shift, scale, gate = jnp.split(c_act @ w_ada + b_ada, 3, axis=-1)
    shift_c, scale_c, gate_c = jnp.split(cc_act @ w_ada + b_ada, 3, axis=-1)
    h = rms_norm(x, g_pre) * (1 + scale[:, None, :]) + shift[:, None, :]
    hc = rms_norm(x_ctx, g_pre) * (1 + scale_c) + shift_c

    q, k, v, g_attn, u_pool, g_pool = jnp.split(h @ w_in, SPLITS, axis=-1)
    q = apply_axial_rope(rms_norm(q.reshape(b, n, N_HEADS, HEAD_DIM), g_q), cos, sin)
    k = apply_axial_rope(rms_norm(k.reshape(b, n, N_KV_HEADS, HEAD_DIM), g_k), cos, sin)
    v = v.reshape(b, n, N_KV_HEADS, HEAD_DIM)

    if update_ctx:
        qc, kc, vc, gc_attn, uc_pool, gc_pool = jnp.split(hc @ w_in, SPLITS, axis=-1)
    else:
        kc, vc = jnp.split(hc @ w_in[:, SPLITS[0]:SPLITS[2]], 2, axis=-1)
    kc = rms_norm(kc.reshape(b, l, N_KV_HEADS, HEAD_DIM), g_k)
    vc = vc.reshape(b, l, N_KV_HEADS, HEAD_DIM)

    k_all = jnp.concatenate([kc, k], axis=1)
    v_all = jnp.concatenate([vc, v], axis=1)
    attn_o = latent_attention(q, k_all, v_all)
    out = merge_branches(attn_o, g_attn, u_pool, g_pool, pool_w, pool_scale, w_out)
    x_new = x + gate[:, None, :] * rms_norm(out, g_post)

    if update_ctx:
        qc = rms_norm(qc.reshape(b, l, N_HEADS, HEAD_DIM), g_q)
        attn_c = context_attention(qc, kc, vc)
        out_c = merge_branches(attn_c, gc_attn, uc_pool, gc_pool, pool_w, pool_scale, w_out)
        x_ctx = x_ctx + gate_c * rms_norm(out_c, g_post)
    return x_new, x_ctx


def _fwd_setup_inputs(seed: int = 0) -> dict:
    key = jax.random.key(seed)
    ks = jax.random.split(key, 16)
    f32 = jnp.float32
    x = jax.random.normal(ks[0], (BATCH, SEQ, D_MODEL), f32)
    c = jax.random.normal(ks[1], (BATCH, D_MODEL), f32)
    ctx = jax.random.normal(ks[2], (BATCH, CTX_LEN, D_MODEL), f32)
    c_ctx = jax.random.normal(ks[3], (D_MODEL,), f32)
    w_ada = jax.random.normal(ks[4], (DEPTH, D_MODEL, 3 * D_MODEL), f32) * (0.5 * D_MODEL ** -0.5)
    b_ada = jax.random.normal(ks[5], (DEPTH, 3 * D_MODEL), f32) * 0.01
    norm_pre = 1.0 + 0.01 * jax.random.normal(ks[6], (DEPTH, D_MODEL), f32)
    norm_post = 1.0 + 0.01 * jax.random.normal(ks[7], (DEPTH, D_MODEL), f32)
    w_in = jax.random.normal(ks[8], (DEPTH, D_MODEL, IN_WIDTH), f32) * D_MODEL ** -0.5
    q_norm = 1.0 + 0.01 * jax.random.normal(ks[9], (DEPTH, HEAD_DIM), f32)
    k_norm = 1.0 + 0.01 * jax.random.normal(ks[10], (DEPTH, HEAD_DIM), f32)
    pool_w = jax.random.normal(ks[11], (DEPTH, N_POOL_GROUPS, POOL_GROUP, POOL_GROUP), f32) * POOL_GROUP ** -0.5
    pool_scale = 1.0 + 0.02 * jax.random.normal(ks[12], (DEPTH, POOL_WIDTH), f32)
    w_out = jax.random.normal(ks[13], (DEPTH, MIX_WIDTH, D_MODEL), f32) * MIX_WIDTH ** -0.5
    return {"x": x, "c": c, "ctx": ctx, "c_ctx": c_ctx, "w_ada": w_ada, "b_ada": b_ada,
            "norm_pre": norm_pre, "norm_post": norm_post, "w_in": w_in, "q_norm": q_norm,
            "k_norm": k_norm, "pool_w": pool_w, "pool_scale": pool_scale, "w_out": w_out}


def _fwd_reference(x, c, ctx, c_ctx, w_ada, b_ada, norm_pre, norm_post, w_in, q_norm,
              k_norm, pool_w, pool_scale, w_out):
    cos, sin = axial_rope_tables(x.shape[1])
    c_act = jax.nn.silu(c)
    cc_act = jax.nn.silu(c_ctx)
    x_ctx = ctx
    for layer in range(DEPTH):
        x, x_ctx = hybrid_layer(x, x_ctx, c_act, cc_act, w_ada[layer], b_ada[layer],
                                norm_pre[layer], norm_post[layer], w_in[layer],
                                q_norm[layer], k_norm[layer], pool_w[layer],
                                pool_scale[layer], w_out[layer], cos, sin,
                                update_ctx=(layer < DEPTH - 1))
    return x


import jax as _jax
import jax.numpy as _jnp

TWIN_FORMAT = 'train_step'
FWD_PARAMS = ['x', 'c', 'ctx', 'c_ctx', 'w_ada', 'b_ada', 'norm_pre', 'norm_post', 'w_in', 'q_norm', 'k_norm', 'pool_w', 'pool_scale', 'w_out']
TWIN_WEIGHTS = ['c_ctx', 'w_ada', 'b_ada', 'norm_pre', 'norm_post', 'w_in', 'q_norm', 'k_norm', 'pool_w', 'pool_scale', 'w_out']
TWIN_DIFF_INPUT = 'x'
TWIN_INPUTS = ['x', 'c', 'ctx', 'c_ctx', 'w_ada', 'b_ada', 'norm_pre', 'norm_post', 'w_in', 'q_norm', 'k_norm', 'pool_w', 'pool_scale', 'w_out', 'loss_target', 'm_c_ctx', 'm_w_ada', 'm_b_ada', 'm_norm_pre', 'm_norm_post', 'm_w_in', 'm_q_norm', 'm_k_norm', 'm_pool_w', 'm_pool_scale', 'm_w_out', 'v_c_ctx', 'v_w_ada', 'v_b_ada', 'v_norm_pre', 'v_norm_post', 'v_w_in', 'v_q_norm', 'v_k_norm', 'v_pool_w', 'v_pool_scale', 'v_w_out']
TWIN_OUTPUTS = ['loss', 'grad_x', 'grad_c_ctx', 'grad_w_ada', 'grad_b_ada', 'grad_norm_pre', 'grad_norm_post', 'grad_w_in', 'grad_q_norm', 'grad_k_norm', 'grad_pool_w', 'grad_pool_scale', 'grad_w_out', 'delta_c_ctx', 'delta_w_ada', 'delta_b_ada', 'delta_norm_pre', 'delta_norm_post', 'delta_w_in', 'delta_q_norm', 'delta_k_norm', 'delta_pool_w', 'delta_pool_scale', 'delta_w_out', 'new_m_c_ctx', 'new_m_w_ada', 'new_m_b_ada', 'new_m_norm_pre', 'new_m_norm_post', 'new_m_w_in', 'new_m_q_norm', 'new_m_k_norm', 'new_m_pool_w', 'new_m_pool_scale', 'new_m_w_out', 'new_v_c_ctx', 'new_v_w_ada', 'new_v_b_ada', 'new_v_norm_pre', 'new_v_norm_post', 'new_v_w_in', 'new_v_q_norm', 'new_v_k_norm', 'new_v_pool_w', 'new_v_pool_scale', 'new_v_w_out']
TWIN_LEAF_KINDS = {'loss': 'loss', 'grad_x': 'grad_x', 'grad_c_ctx': 'grad_w', 'grad_w_ada': 'grad_w', 'grad_b_ada': 'grad_w', 'grad_norm_pre': 'grad_w', 'grad_norm_post': 'grad_w', 'grad_w_in': 'grad_w', 'grad_q_norm': 'grad_w', 'grad_k_norm': 'grad_w', 'grad_pool_w': 'grad_w', 'grad_pool_scale': 'grad_w', 'grad_w_out': 'grad_w', 'delta_c_ctx': 'delta_w', 'delta_w_ada': 'delta_w', 'delta_b_ada': 'delta_w', 'delta_norm_pre': 'delta_w', 'delta_norm_post': 'delta_w', 'delta_w_in': 'delta_w', 'delta_q_norm': 'delta_w', 'delta_k_norm': 'delta_w', 'delta_pool_w': 'delta_w', 'delta_pool_scale': 'delta_w', 'delta_w_out': 'delta_w', 'new_m_c_ctx': 'new_m', 'new_m_w_ada': 'new_m', 'new_m_b_ada': 'new_m', 'new_m_norm_pre': 'new_m', 'new_m_norm_post': 'new_m', 'new_m_w_in': 'new_m', 'new_m_q_norm': 'new_m', 'new_m_k_norm': 'new_m', 'new_m_pool_w': 'new_m', 'new_m_pool_scale': 'new_m', 'new_m_w_out': 'new_m', 'new_v_c_ctx': 'new_v', 'new_v_w_ada': 'new_v', 'new_v_b_ada': 'new_v', 'new_v_norm_pre': 'new_v', 'new_v_norm_post': 'new_v', 'new_v_w_in': 'new_v', 'new_v_q_norm': 'new_v', 'new_v_k_norm': 'new_v', 'new_v_pool_w': 'new_v', 'new_v_pool_scale': 'new_v', 'new_v_w_out': 'new_v'}


def _forward(args):
    return _fwd_reference(*[args[k] for k in FWD_PARAMS])


def _output_shape():
    out = _jax.eval_shape(lambda: _forward(_fwd_setup_inputs(0)))
    return out.shape, out.dtype

N_MICROBATCH = 1
ADAM_LR = 0.001
ADAM_B1 = 0.9
ADAM_B2 = 0.999
ADAM_EPS = 1e-08
ADAM_WD = 0.01
ADAM_STEP = 10
PER_EXAMPLE_BATCH_AXIS = {'x': 0, 'c': 0, 'ctx': 0, 'loss_target': 0}
SHARED_INPUTS = []
_WEIGHT_DTYPES = {'c_ctx': _jnp.float32, 'w_ada': _jnp.float32, 'b_ada': _jnp.float32, 'norm_pre': _jnp.float32, 'norm_post': _jnp.float32, 'w_in': _jnp.float32, 'q_norm': _jnp.float32, 'k_norm': _jnp.float32, 'pool_w': _jnp.float32, 'pool_scale': _jnp.float32, 'w_out': _jnp.float32}
MOMENT_SCALE = {'c_ctx': 3.337970e-03, 'w_ada': 4.084248e-01, 'b_ada': 7.991764e-01, 'norm_pre': 2.774930e-02, 'norm_post': 8.899784e-01, 'w_in': 1.944195e-02, 'q_norm': 9.536585e-03, 'k_norm': 8.966244e-03, 'pool_w': 2.711925e-02, 'pool_scale': 3.089116e-02, 'w_out': 2.155584e-02}


def _to_microbatches(a, axis):
    t = _jnp.moveaxis(a, axis, 0)
    t = t.reshape((N_MICROBATCH, t.shape[0] // N_MICROBATCH) + t.shape[1:])
    return _jnp.moveaxis(t, 1, axis + 1)


def setup_inputs(seed: int = 0) -> dict:
    inp = _fwd_setup_inputs(seed)
    key = _jax.random.fold_in(_jax.random.key(seed), 7919)
    shape, _ = _output_shape()
    out = dict(inp)
    out["loss_target"] = _jax.random.normal(_jax.random.fold_in(key, 0), shape, _jnp.float32)
    for i, name in enumerate(TWIN_WEIGHTS):
        w = inp[name].astype(_jnp.float32)
        if MOMENT_SCALE is None:
            s = _jnp.sqrt(_jnp.mean(_jnp.square(w)) + 1e-30)
        else:
            s = MOMENT_SCALE[name]
        km, kv = _jax.random.split(_jax.random.fold_in(key, i + 1))
        out[name] = w
        out["m_" + name] = s * _jax.random.normal(km, w.shape, _jnp.float32)
        out["v_" + name] = (s * s) * _jax.random.uniform(kv, w.shape, _jnp.float32, 0.5, 1.5)
    if N_MICROBATCH > 1:
        for name, axis in PER_EXAMPLE_BATCH_AXIS.items():
            out[name] = _to_microbatches(out[name], axis)
    return {'x': out['x'], 'c': out['c'], 'ctx': out['ctx'], 'c_ctx': out['c_ctx'], 'w_ada': out['w_ada'], 'b_ada': out['b_ada'], 'norm_pre': out['norm_pre'], 'norm_post': out['norm_post'], 'w_in': out['w_in'], 'q_norm': out['q_norm'], 'k_norm': out['k_norm'], 'pool_w': out['pool_w'], 'pool_scale': out['pool_scale'], 'w_out': out['w_out'], 'loss_target': out['loss_target'], 'm_c_ctx': out['m_c_ctx'], 'm_w_ada': out['m_w_ada'], 'm_b_ada': out['m_b_ada'], 'm_norm_pre': out['m_norm_pre'], 'm_norm_post': out['m_norm_post'], 'm_w_in': out['m_w_in'], 'm_q_norm': out['m_q_norm'], 'm_k_norm': out['m_k_norm'], 'm_pool_w': out['m_pool_w'], 'm_pool_scale': out['m_pool_scale'], 'm_w_out': out['m_w_out'], 'v_c_ctx': out['v_c_ctx'], 'v_w_ada': out['v_w_ada'], 'v_b_ada': out['v_b_ada'], 'v_norm_pre': out['v_norm_pre'], 'v_norm_post': out['v_norm_post'], 'v_w_in': out['v_w_in'], 'v_q_norm': out['v_q_norm'], 'v_k_norm': out['v_k_norm'], 'v_pool_w': out['v_pool_w'], 'v_pool_scale': out['v_pool_scale'], 'v_w_out': out['v_w_out']}


def _loss(weights, diff, rest, loss_target):
    with _jax.named_scope("forward"):
        args = {**rest, TWIN_DIFF_INPUT: diff, **{k: w.astype(_WEIGHT_DTYPES[k]) for k, w in weights.items()}}
        y = _forward(args)
    with _jax.named_scope("loss_head"):
        err = _jnp.square(y.astype(_jnp.float32) - loss_target)
        return 0.5 * _jnp.sum(_jnp.mean(err, axis=-1)) if err.ndim else 0.5 * err


def _adamw(w, g, m, v):
    m = ADAM_B1 * m + (1.0 - ADAM_B1) * g
    v = ADAM_B2 * v + (1.0 - ADAM_B2) * _jnp.square(g)
    m_hat = m / (1.0 - ADAM_B1 ** ADAM_STEP)
    v_hat = v / (1.0 - ADAM_B2 ** ADAM_STEP)
    delta = -ADAM_LR * (m_hat / (_jnp.sqrt(v_hat) + ADAM_EPS) + ADAM_WD * w)
    return delta, m, v


def reference(x, c, ctx, c_ctx, w_ada, b_ada, norm_pre, norm_post, w_in, q_norm, k_norm, pool_w, pool_scale, w_out, loss_target, m_c_ctx, m_w_ada, m_b_ada, m_norm_pre, m_norm_post, m_w_in, m_q_norm, m_k_norm, m_pool_w, m_pool_scale, m_w_out, v_c_ctx, v_w_ada, v_b_ada, v_norm_pre, v_norm_post, v_w_in, v_q_norm, v_k_norm, v_pool_w, v_pool_scale, v_w_out):
    given = dict(x=x, c=c, ctx=ctx, c_ctx=c_ctx, w_ada=w_ada, b_ada=b_ada, norm_pre=norm_pre, norm_post=norm_post, w_in=w_in, q_norm=q_norm, k_norm=k_norm, pool_w=pool_w, pool_scale=pool_scale, w_out=w_out, loss_target=loss_target, m_c_ctx=m_c_ctx, m_w_ada=m_w_ada, m_b_ada=m_b_ada, m_norm_pre=m_norm_pre, m_norm_post=m_norm_post, m_w_in=m_w_in, m_q_norm=m_q_norm, m_k_norm=m_k_norm, m_pool_w=m_pool_w, m_pool_scale=m_pool_scale, m_w_out=m_w_out, v_c_ctx=v_c_ctx, v_w_ada=v_w_ada, v_b_ada=v_b_ada, v_norm_pre=v_norm_pre, v_norm_post=v_norm_post, v_w_in=v_w_in, v_q_norm=v_q_norm, v_k_norm=v_k_norm, v_pool_w=v_pool_w, v_pool_scale=v_pool_scale, v_w_out=v_w_out)
    weights = {n: given[n] for n in TWIN_WEIGHTS}
    shared = {n: given[n] for n in SHARED_INPUTS}
    per_example = {n: given[n] for n in ['x', 'c', 'ctx']}
    grad_fn = _jax.value_and_grad(_loss, argnums=(0, 1))

    def one_microbatch(ex, loss_target):
        ex = dict(ex)
        diff = ex.pop(TWIN_DIFF_INPUT)
        return grad_fn(weights, diff, {**shared, **ex}, loss_target)

    if N_MICROBATCH == 1:
        loss, (grad_w, grad_x) = one_microbatch(per_example, given["loss_target"])
    else:
        def body(carry, xs):
            loss_sum, grad_sum = carry
            l_k, (gw_k, gx_k) = one_microbatch(xs[0], xs[1])
            with _jax.named_scope("update"):
                return (loss_sum + l_k, _jax.tree.map(_jnp.add, grad_sum, gw_k)), gx_k

        init = (_jnp.zeros((), _jnp.float32), _jax.tree.map(_jnp.zeros_like, weights))
        (loss, grad_w), grad_x = _jax.lax.scan(body, init, (per_example, given["loss_target"]))
    with _jax.named_scope("update"):
        delta_w, new_m, new_v = {}, {}, {}
        for n in TWIN_WEIGHTS:
            delta_w[n], new_m[n], new_v[n] = _adamw(weights[n], grad_w[n], given["m_" + n], given["v_" + n])
    return (loss, grad_x, *[grad_w[n] for n in TWIN_WEIGHTS], *[delta_w[n] for n in TWIN_WEIGHTS],
            *[new_m[n] for n in TWIN_WEIGHTS], *[new_v[n] for n in TWIN_WEIGHTS])
```

```python
import functools

import jax
import jax.numpy as jnp
from jax import lax
from jax.experimental import pallas as pl
from jax.experimental.pallas import tpu as pltpu

HEAD_DIM = 128
GQA_GROUP = 4
GRID_W = 64
ROPE_PAIRS = HEAD_DIM // 4
ROPE_THETA = 10000.0
ATTN_SCALE = HEAD_DIM ** -0.5
EPS = 1e-6
POOL_WINDOWS = (2, 4, 8, 16)
POOL_HALO = 8
N_DEV = 8
N_CHIPS = 4
ADAM_LR = 0.001
ADAM_B1 = 0.9
ADAM_B2 = 0.999
ADAM_EPS = 1e-08
ADAM_WD = 0.01
ADAM_STEP = 10

LANES = 128
SUBLANES = 8
BF16_ROWS = 16

_MESH = pl.DeviceIdType.MESH
_ANY = pl.BlockSpec(memory_space=pl.ANY)
_VMEM = pl.BlockSpec(memory_space=pltpu.VMEM)
_F32 = jnp.float32
_BF16 = jnp.bfloat16


def _tile(dim, pref, align):
    t = min(pref, dim)
    t -= t % align
    while t >= align:
        if dim % t == 0:
            return t
        t -= align
    return dim


def _position():
    return lax.axis_index("x"), lax.axis_index("y"), lax.axis_index("c")


def _flip(v, bit):
    return 1 - v if bit else v


def _dev_index(x, y, c):
    return 4 * x + 2 * y + c


def _silu(g):
    return g * jax.nn.sigmoid(g)


def _silu_grad(g):
    s = jax.nn.sigmoid(g)
    return s * (1.0 + g * (1.0 - s))


def _adamw(w, g, m, v):
    m = ADAM_B1 * m + (1.0 - ADAM_B1) * g
    v = ADAM_B2 * v + (1.0 - ADAM_B2) * (g * g)
    m_hat = m / (1.0 - ADAM_B1 ** ADAM_STEP)
    v_hat = v / (1.0 - ADAM_B2 ** ADAM_STEP)
    delta = -ADAM_LR * (m_hat / (jnp.sqrt(v_hat) + ADAM_EPS) + ADAM_WD * w)
    return delta, m, v


def _all_gather_small(v, name):
    rows, cols = v.shape

    def body(v_ref, out_ref, send_sems, recv_sems):
        x, y, c = _position()
        me = _dev_index(x, y, c)
        out_ref[me] = v_ref[...]
        peers = [(_flip(x, k & 4), _flip(y, k & 2), _flip(c, k & 1)) for k in range(1, N_DEV)]

        def copy(k, block, to):
            return pltpu.make_async_remote_copy(
                src_ref=v_ref, dst_ref=out_ref.at[block], send_sem=send_sems.at[k], recv_sem=recv_sems.at[k],
                device_id=to, device_id_type=_MESH)

        sends = [copy(k, me, p) for k, p in enumerate(peers)]
        for s in sends:
            s.start()
        for k, p in enumerate(peers):
            copy(k, _dev_index(*p), p).wait_recv()
        for s in sends:
            s.wait_send()

    return pl.pallas_call(
        body, name=name,
        out_shape=jax.ShapeDtypeStruct((N_DEV, rows, cols), v.dtype),
        in_specs=[_VMEM], out_specs=_VMEM,
        scratch_shapes=[pltpu.SemaphoreType.DMA((N_DEV - 1,)), pltpu.SemaphoreType.DMA((N_DEV - 1,))],
    )(v)


def _window(ref, axis, size, j):
    start = pl.multiple_of(j * size, size)
    if axis == 1:
        return ref.at[:, pl.ds(start, size), :]
    return ref.at[:, :, pl.ds(start, size)]


def _gather_weights(shards, axes, name):
    n = len(shards)
    sizes = [s.shape[a] for s, a in zip(shards, axes)]
    out_shapes = [
        jax.ShapeDtypeStruct(tuple(d * N_DEV if i == a else d for i, d in enumerate(s.shape)), s.dtype)
        for s, a in zip(shards, axes)]

    def body(*refs):
        srcs, outs = refs[:n], refs[n:2 * n]
        send_sems, recv_sems, local_sems = refs[2 * n:]
        x, y, c = _position()
        me, sibling = (x, y, c), (x, y, 1 - c)
        chips = [(1 - x, y), (x, 1 - y), (1 - x, 1 - y)]
        firsts, passed, locals_ = [], [], []
        for a in range(n):
            def rows(block, a=a):
                return _window(outs[a], axes[a], sizes[a], _dev_index(*block))

            def copy(k, block, to, src=None, a=a, rows=rows):
                return pltpu.make_async_remote_copy(
                    src_ref=rows(block) if src is None else src, dst_ref=rows(block),
                    send_sem=send_sems.at[7 * a + k], recv_sem=recv_sems.at[7 * a + k],
                    device_id=to, device_id_type=_MESH)

            mine = pltpu.make_async_copy(srcs[a], rows(me), local_sems.at[a])
            mine.start()
            locals_.append(mine)
            first = [copy(0, me, sibling, src=srcs[a])]
            first += [copy(1 + j, me, (*chip, c), src=srcs[a]) for j, chip in enumerate(chips)]
            for cp in first:
                cp.start()
            firsts.append((first, copy))
        for a in range(n):
            first, copy = firsts[a]
            fwd = [copy(4 + j, (*chip, c), sibling) for j, chip in enumerate(chips)]
            for j, chip in enumerate(chips):
                copy(1 + j, (*chip, c), me).wait_recv()
                fwd[j].start()
            passed.append(fwd)
        for a in range(n):
            first, copy = firsts[a]
            copy(0, sibling, me).wait_recv()
            for j, chip in enumerate(chips):
                copy(4 + j, (*chip, 1 - c), me).wait_recv()
            for cp in first + passed[a]:
                cp.wait_send()
            locals_[a].wait()

    return pl.pallas_call(
        body, name=name, out_shape=out_shapes,
        in_specs=[_ANY] * n, out_specs=[_ANY] * n,
        scratch_shapes=[pltpu.SemaphoreType.DMA((7 * n,)), pltpu.SemaphoreType.DMA((7 * n,)),
                        pltpu.SemaphoreType.DMA((n,))],
    )(*shards)


def _exchange_sibling(grads, axes, name):
    n = len(grads)
    sizes = [g.shape[a] // N_DEV for g, a in zip(grads, axes)]
    slab = [tuple(sz if i == a else d for i, d in enumerate(g.shape)) for g, a, sz in zip(grads, axes, sizes)]
    out_shapes = ([jax.ShapeDtypeStruct((N_CHIPS,) + s, g.dtype) for s, g in zip(slab, grads)]
                  + [jax.ShapeDtypeStruct((N_CHIPS,) + s, g.dtype) for s, g in zip(slab, grads)])

    def body(*refs):
        srcs, owns, gots = refs[:n], refs[n:2 * n], refs[2 * n:3 * n]
        send_sems, recv_sems, local_sems = refs[3 * n:]
        x, y, c = _position()
        sibling = (x, y, 1 - c)
        chips = [(x, y), (1 - x, y), (x, 1 - y), (1 - x, 1 - y)]
        sends, locals_ = [], []
        for a in range(n):
            for s, chip in enumerate(chips):
                k = N_CHIPS * a + s
                keep = pltpu.make_async_copy(
                    _window(srcs[a], axes[a], sizes[a], _dev_index(*chip, c)), owns[a].at[s], local_sems.at[k])
                keep.start()
                locals_.append(keep)
                give = pltpu.make_async_remote_copy(
                    src_ref=_window(srcs[a], axes[a], sizes[a], _dev_index(*chip, 1 - c)), dst_ref=gots[a].at[s],
                    send_sem=send_sems.at[k], recv_sem=recv_sems.at[k], device_id=sibling, device_id_type=_MESH)
                give.start()
                sends.append(give)
        for cp in sends:
            cp.wait_recv()
        for cp in sends:
            cp.wait_send()
        for cp in locals_:
            cp.wait()

    return pl.pallas_call(
        body, name=name, out_shape=out_shapes,
        in_specs=[_ANY] * n, out_specs=[_ANY] * (2 * n),
        scratch_shapes=[pltpu.SemaphoreType.DMA((N_CHIPS * n,)), pltpu.SemaphoreType.DMA((N_CHIPS * n,)),
                        pltpu.SemaphoreType.DMA((N_CHIPS * n,))],
    )(*grads)


def _exchange_chips(sums, name):
    n = len(sums)

    def body(*refs):
        srcs, gots = refs[:n], refs[n:2 * n]
        send_sems, recv_sems = refs[2 * n:]
        x, y, c = _position()
        chips = [(1 - x, y), (x, 1 - y), (1 - x, 1 - y)]
        sends = []
        for a in range(n):
            for k, chip in enumerate(chips):
                cp = pltpu.make_async_remote_copy(
                    src_ref=srcs[a].at[k], dst_ref=gots[a].at[k],
                    send_sem=send_sems.at[3 * a + k], recv_sem=recv_sems.at[3 * a + k],
                    device_id=(*chip, c), device_id_type=_MESH)
                cp.start()
                sends.append(cp)
        for cp in sends:
            cp.wait_recv()
        for cp in sends:
            cp.wait_send()

    return pl.pallas_call(
        body, name=name, out_shape=[jax.ShapeDtypeStruct(s.shape, s.dtype) for s in sums],
        in_specs=[_ANY] * n, out_specs=[_ANY] * n,
        scratch_shapes=[pltpu.SemaphoreType.DMA((3 * n,)), pltpu.SemaphoreType.DMA((3 * n,))],
    )(*sums)


def _matmul(a, b, *, ta=False, tb=False, out_dtype=_F32, tm=1024, tn=1024, tk=1024, name):
    kdim, m = a.shape if ta else a.shape[::-1]
    n = b.shape[0] if tb else b.shape[1]
    tm = _tile(m, tm, LANES if ta else BF16_ROWS)
    tn = _tile(n, tn, LANES)
    tk = _tile(kdim, tk, BF16_ROWS if ta else LANES)
    nk = kdim // tk
    dims = (((0 if ta else 1,), (1 if tb else 0,)), ((), ()))

    def body(a_ref, b_ref, o_ref, acc_ref):
        k = pl.program_id(2)

        @pl.when(k == 0)
        def _():
            acc_ref[...] = jnp.zeros_like(acc_ref)

        acc_ref[...] += lax.dot_general(a_ref[...], b_ref[...], dims, preferred_element_type=_F32)

        @pl.when(k == nk - 1)
        def _():
            o_ref[...] = acc_ref[...].astype(out_dtype)

    a_spec = pl.BlockSpec((tk, tm), lambda i, j, k: (k, i)) if ta else pl.BlockSpec((tm, tk), lambda i, j, k: (i, k))
    b_spec = pl.BlockSpec((tn, tk), lambda i, j, k: (j, k)) if tb else pl.BlockSpec((tk, tn), lambda i, j, k: (k, j))
    return pl.pallas_call(
        body, name=name, grid=(m // tm, n // tn, nk),
        in_specs=[a_spec, b_spec], out_specs=pl.BlockSpec((tm, tn), lambda i, j, k: (i, j)),
        out_shape=jax.ShapeDtypeStruct((m, n), out_dtype),
        scratch_shapes=[pltpu.VMEM((tm, tn), _F32)],
        compiler_params=pltpu.CompilerParams(dimension_semantics=("parallel", "parallel", "arbitrary")),
    )(a, b)


def _row_tile(rows, cols):
    return _tile(rows, max(BF16_ROWS, min(512, (1 << 19) // cols)), BF16_ROWS)


def _cast_bf16(a, name):
    r, c = a.shape
    tr = _row_tile(r, c)

    def body(a_ref, o_ref):
        o_ref[...] = a_ref[...].astype(_BF16)

    blk = pl.BlockSpec((tr, c), lambda i: (i, 0))
    return pl.pallas_call(
        body, name=name, grid=(r // tr,), in_specs=[blk], out_specs=blk,
        out_shape=jax.ShapeDtypeStruct((r, c), _BF16),
        compiler_params=pltpu.CompilerParams(dimension_semantics=("parallel",)),
    )(a)


def _pre_add(own, got, name):
    _, r, c = own.shape
    tr = _row_tile(r, c)

    def body(a_ref, b_ref, o_ref):
        o_ref[...] = (a_ref[...] + b_ref[...]).astype(_BF16)

    src = pl.BlockSpec((1, tr, c), lambda s, i: (s + 1, i, 0))
    return pl.pallas_call(
        body, name=name, grid=(N_CHIPS - 1, r // tr), in_specs=[src, src],
        out_specs=pl.BlockSpec((1, tr, c), lambda s, i: (s, i, 0)),
        out_shape=jax.ShapeDtypeStruct((N_CHIPS - 1, r, c), _BF16),
        compiler_params=pltpu.CompilerParams(dimension_semantics=("parallel", "parallel")),
    )(own, got)


def _ada_forward(craw, w_shard, name):
    d, cols = w_shard.shape
    tk = _tile(d, 512, LANES)

    def body(c_ref, w_ref, o_ref):
        @pl.when(pl.program_id(0) == 0)
        def _():
            o_ref[...] = jnp.zeros_like(o_ref)

        o_ref[...] += jnp.dot(_silu(c_ref[...]).astype(_BF16), w_ref[...].astype(_BF16), preferred_element_type=_F32)

    return pl.pallas_call(
        body, name=name, grid=(d // tk,),
        in_specs=[pl.BlockSpec((craw.shape[0], tk), lambda k: (0, k)), pl.BlockSpec((tk, cols), lambda k: (k, 0))],
        out_specs=pl.BlockSpec((craw.shape[0], cols), lambda k: (0, 0)),
        out_shape=jax.ShapeDtypeStruct((craw.shape[0], cols), _F32),
        compiler_params=pltpu.CompilerParams(dimension_semantics=("arbitrary",)),
    )(craw, w_shard)


def _ada_backward(craw, dmod, w, m, v, name):
    d, cols = w.shape
    rows = craw.shape[0]
    tr = _tile(d, 256, LANES)

    def body(c_ref, dm_ref, w_ref, m_ref, v_ref, g_ref, dl_ref, nm_ref, nv_ref, dc_ref):
        act = _silu(c_ref[...]).astype(_BF16)
        dmb = dm_ref[...].astype(_BF16)
        wv = w_ref[...]
        g = lax.dot_general(act, dmb, (((0,), (0,)), ((), ())), preferred_element_type=_F32)
        delta, nm, nv = _adamw(wv, g, m_ref[...], v_ref[...])
        g_ref[...] = g
        dl_ref[...] = delta
        nm_ref[...] = nm
        nv_ref[...] = nv
        dc = lax.dot_general(dmb, wv.astype(_BF16), (((1,), (1,)), ((), ())), preferred_element_type=_F32)
        dc_ref[...] = jnp.broadcast_to(jnp.sum(dc[N_DEV:], axis=0, keepdims=True), dc_ref.shape)

    blk = pl.BlockSpec((tr, cols), lambda i: (i, 0))
    return pl.pallas_call(
        body, name=name, grid=(d // tr,),
        in_specs=[pl.BlockSpec((rows, tr), lambda i: (0, i)), pl.BlockSpec((rows, cols), lambda i: (0, 0)), blk, blk, blk],
        out_specs=[blk, blk, blk, blk, pl.BlockSpec((SUBLANES, tr), lambda i: (0, i))],
        out_shape=[jax.ShapeDtypeStruct((d, cols), _F32)] * 4 + [jax.ShapeDtypeStruct((SUBLANES, d), _F32)],
        compiler_params=pltpu.CompilerParams(dimension_semantics=("parallel",)),
    )(craw, dmod, w, m, v)


def _rms(xf):
    return lax.rsqrt(jnp.mean(xf * xf, axis=-1, keepdims=True) + EPS)


def _prenorm(ctx, x, g_pre, mods, tr, name):
    l, d = ctx.shape
    n = x.shape[0]
    nbc = l // tr

    def body(ctx_ref, x_ref, g_ref, mod_ref, h_ref):
        def emit(src_ref):
            xf = src_ref[...]
            y = (xf * _rms(xf)) * g_ref[...]
            h_ref[...] = (y * (1.0 + mod_ref[0, 0:1, :]) + mod_ref[0, 1:2, :]).astype(_BF16)

        is_ctx = pl.program_id(0) < nbc
        pl.when(is_ctx)(lambda: emit(ctx_ref))
        pl.when(jnp.logical_not(is_ctx))(lambda: emit(x_ref))

    return pl.pallas_call(
        body, name=name, grid=((l + n) // tr,),
        in_specs=[pl.BlockSpec((tr, d), lambda i: (jnp.minimum(i, nbc - 1), 0)),
                  pl.BlockSpec((tr, d), lambda i: (jnp.maximum(i - nbc, 0), 0)),
                  pl.BlockSpec((1, d), lambda i: (0, 0)),
                  pl.BlockSpec((1, SUBLANES, d), lambda i: ((i >= nbc).astype(jnp.int32), 0, 0))],
        out_specs=pl.BlockSpec((tr, d), lambda i: (i, 0)),
        out_shape=jax.ShapeDtypeStruct((l + n, d), _BF16),
        compiler_params=pltpu.CompilerParams(dimension_semantics=("arbitrary",)),
    )(ctx, x, g_pre, mods)


def _prenorm_backward(dh, ctx, x, dxn, g_pre, mods, tr, name):
    l, d = ctx.shape
    n = x.shape[0]
    nbc = l // tr

    def body(dh_ref, ctx_ref, x_ref, dxn_ref, g_ref, mod_ref, gx_ref, dmod_ref, dg_ref):
        i = pl.program_id(0)

        @pl.when(i == 0)
        def _():
            dg_ref[...] = jnp.zeros_like(dg_ref)

        @pl.when(jnp.logical_or(i == 0, i == nbc))
        def _():
            dmod_ref[...] = jnp.zeros_like(dmod_ref)

        def emit(src_ref, latent):
            xf = src_ref[...]
            r = _rms(xf)
            xn = xf * r
            dhv = dh_ref[...]
            one_scale = 1.0 + mod_ref[0, 0:1, :]
            dmod_ref[0, 0:1, :] += jnp.sum(dhv * (xn * g_ref[...]), axis=0, keepdims=True)
            dmod_ref[0, 1:2, :] += jnp.sum(dhv, axis=0, keepdims=True)
            dyg = dhv * one_scale
            dg_ref[0:1, :] += jnp.sum(dyg * xn, axis=0, keepdims=True)
            if latent:
                dn = dyg * g_ref[...]
                gx_ref[...] = dxn_ref[...] + r * (dn - xn * jnp.mean(dn * xn, axis=-1, keepdims=True))

        pl.when(i < nbc)(lambda: emit(ctx_ref, False))
        pl.when(i >= nbc)(lambda: emit(x_ref, True))

    lat = pl.BlockSpec((tr, d), lambda i: (jnp.maximum(i - nbc, 0), 0))
    sel = pl.BlockSpec((1, SUBLANES, d), lambda i: ((i >= nbc).astype(jnp.int32), 0, 0))
    return pl.pallas_call(
        body, name=name, grid=((l + n) // tr,),
        in_specs=[pl.BlockSpec((tr, d), lambda i: (i, 0)),
                  pl.BlockSpec((tr, d), lambda i: (jnp.minimum(i, nbc - 1), 0)),
                  lat, lat, pl.BlockSpec((1, d), lambda i: (0, 0)), sel],
        out_specs=[lat, sel, pl.BlockSpec((SUBLANES, d), lambda i: (0, 0))],
        out_shape=[jax.ShapeDtypeStruct((n, d), _F32), jax.ShapeDtypeStruct((2, SUBLANES, d), _F32),
                   jax.ShapeDtypeStruct((SUBLANES, d), _F32)],
        compiler_params=pltpu.CompilerParams(dimension_semantics=("arbitrary",)),
    )(dh, ctx, x, dxn, g_pre, mods)


def _rope_tables(l, n):
    rows = n // GRID_W
    row = jnp.repeat(jnp.arange(rows, dtype=_F32), GRID_W)
    col = jnp.tile(jnp.arange(GRID_W, dtype=_F32), rows)
    inv = ROPE_THETA ** (-jnp.arange(ROPE_PAIRS, dtype=_F32) / ROPE_PAIRS)
    ang_r, ang_c = row[:, None] * inv, col[:, None] * inv
    cr, sr, cc, sc = jnp.cos(ang_r), jnp.sin(ang_r), jnp.cos(ang_c), jnp.sin(ang_c)
    zero = jnp.zeros_like(sr)
    tc = jnp.concatenate([cr, cr, cc, cc], axis=-1)
    ta = jnp.concatenate([-sr, zero, -sc, zero], axis=-1)
    tb = jnp.concatenate([zero, sr, zero, sc], axis=-1)
    pad = lambda t, fill: jnp.concatenate([jnp.full((l, HEAD_DIM), fill, _F32), t], axis=0)
    return pad(tc, 1.0), pad(ta, 0.0), pad(tb, 0.0)


def _rope(y, tc, ta, tb):
    return y * tc + pltpu.roll(y, HEAD_DIM - ROPE_PAIRS, 1) * ta + pltpu.roll(y, ROPE_PAIRS, 1) * tb


def _rope_transposed(dy, tc, ta, tb):
    return dy * tc + pltpu.roll(dy * ta, ROPE_PAIRS, 1) + pltpu.roll(dy * tb, HEAD_DIM - ROPE_PAIRS, 1)


def _qkv_post(proj, tables, g_q, g_k, heads, kv_heads, tr, name):
    t = proj.shape[0]
    aw, kw = heads * HEAD_DIM, kv_heads * HEAD_DIM
    w3 = aw + 2 * kw

    def body(p_ref, tc_ref, ta_ref, tb_ref, gq_ref, gk_ref, q_ref, k_ref, v_ref):
        tabs = (tc_ref[...], ta_ref[...], tb_ref[...])

        def norm_rope(col, gain):
            xh = p_ref[:, col:col + HEAD_DIM]
            return _rope((xh * _rms(xh)) * gain, *tabs).astype(_BF16)

        for h in range(heads):
            q_ref[h] = norm_rope(h * HEAD_DIM, gq_ref[...])
        for h in range(kv_heads):
            k_ref[h] = norm_rope(aw + h * HEAD_DIM, gk_ref[...])
            v_ref[h] = p_ref[:, aw + kw + h * HEAD_DIM:aw + kw + (h + 1) * HEAD_DIM].astype(_BF16)

    tab = pl.BlockSpec((tr, HEAD_DIM), lambda i: (i, 0))
    gain = pl.BlockSpec((1, HEAD_DIM), lambda i: (0, 0))
    return pl.pallas_call(
        body, name=name, grid=(t // tr,),
        in_specs=[pl.BlockSpec((tr, w3), lambda i: (i, 0)), tab, tab, tab, gain, gain],
        out_specs=[pl.BlockSpec((heads, tr, HEAD_DIM), lambda i: (0, i, 0)),
                   pl.BlockSpec((kv_heads, tr, HEAD_DIM), lambda i: (0, i, 0)),
                   pl.BlockSpec((kv_heads, tr, HEAD_DIM), lambda i: (0, i, 0))],
        out_shape=[jax.ShapeDtypeStruct((heads, t, HEAD_DIM), _BF16),
                   jax.ShapeDtypeStruct((kv_heads, t, HEAD_DIM), _BF16),
                   jax.ShapeDtypeStruct((kv_heads, t, HEAD_DIM), _BF16)],
        compiler_params=pltpu.CompilerParams(dimension_semantics=("parallel",)),
    )(proj, *tables, g_q, g_k)


def _qkv_post_backward(proj, dq, dk, dv, tables, g_q, g_k, l, tr, name):
    t = proj.shape[0]
    heads, kv_heads = dq.shape[0], dk.shape[0]
    aw, kw = heads * HEAD_DIM, kv_heads * HEAD_DIM
    w3 = aw + 2 * kw
    nbc = l // tr

    def body(p_ref, dq_ref, dk_ref, dv_ref, tc_ref, ta_ref, tb_ref, gq_ref, gk_ref, o_ref, dgq_ref, dgk_ref):
        i = pl.program_id(0)

        @pl.when(i == 0)
        def _():
            dgq_ref[...] = jnp.zeros_like(dgq_ref)
            dgk_ref[...] = jnp.zeros_like(dgk_ref)

        tabs = (tc_ref[...], ta_ref[...], tb_ref[...])
        latent = i >= nbc

        def back(col, dout, gain, dg_ref):
            xh = p_ref[:, col:col + HEAD_DIM]
            r = _rms(xh)
            xn = xh * r
            dy = _rope_transposed(dout, *tabs)
            dg_ref[0:1, :] += jnp.sum(dy * xn, axis=0, keepdims=True)
            dn = dy * gain
            o_ref[:, col:col + HEAD_DIM] = (r * (dn - xn * jnp.mean(dn * xn, axis=-1, keepdims=True))).astype(_BF16)

        for h in range(heads):
            back(h * HEAD_DIM, jnp.where(latent, dq_ref[h], 0.0), gq_ref[...], dgq_ref)
        for h in range(kv_heads):
            back(aw + h * HEAD_DIM, dk_ref[h], gk_ref[...], dgk_ref)
            o_ref[:, aw + kw + h * HEAD_DIM:aw + kw + (h + 1) * HEAD_DIM] = dv_ref[h].astype(_BF16)

    tab = pl.BlockSpec((tr, HEAD_DIM), lambda i: (i, 0))
    gain = pl.BlockSpec((1, HEAD_DIM), lambda i: (0, 0))
    acc = pl.BlockSpec((SUBLANES, HEAD_DIM), lambda i: (0, 0))
    return pl.pallas_call(
        body, name=name, grid=(t // tr,),
        in_specs=[pl.BlockSpec((tr, w3), lambda i: (i, 0)),
                  pl.BlockSpec((heads, tr, HEAD_DIM), lambda i: (0, jnp.maximum(i - nbc, 0), 0)),
                  pl.BlockSpec((kv_heads, tr, HEAD_DIM), lambda i: (0, i, 0)),
                  pl.BlockSpec((kv_heads, tr, HEAD_DIM), lambda i: (0, i, 0)),
                  tab, tab, tab, gain, gain],
        out_specs=[pl.BlockSpec((tr, w3), lambda i: (i, 0)), acc, acc],
        out_shape=[jax.ShapeDtypeStruct((t, w3), _BF16), jax.ShapeDtypeStruct((SUBLANES, HEAD_DIM), _F32),
                   jax.ShapeDtypeStruct((SUBLANES, HEAD_DIM), _F32)],
        compiler_params=pltpu.CompilerParams(dimension_semantics=("arbitrary",)),
    )(proj, dq, dk, dv, *tables, g_q, g_k)


def _attention(q, k, v, proj, l, mix, tq, ck, name):
    heads, t, _ = q.shape
    kv_heads = k.shape[0]
    n = t - l
    rows = GQA_GROUP * tq
    gw = GQA_GROUP * HEAD_DIM
    aw = heads * HEAD_DIM
    gate_col = (aw + 2 * kv_heads * HEAD_DIM) // gw
    off = l // tq
    n_chunks = t // ck

    def body(q_ref, k_ref, v_ref, g_ref, o_ref, y_ref, lse_ref, m_s, l_s, acc_s):
        q4 = q_ref[...].reshape(rows, HEAD_DIM)
        m_s[...] = jnp.full_like(m_s, -jnp.inf)
        l_s[...] = jnp.zeros_like(l_s)
        acc_s[...] = jnp.zeros_like(acc_s)

        def step(ci, carry):
            start = pl.multiple_of(ci * ck, ck)
            ks = k_ref[0, pl.ds(start, ck), :]
            vs = v_ref[0, pl.ds(start, ck), :]
            s = lax.dot_general(q4, ks, (((1,), (1,)), ((), ())), preferred_element_type=_F32) * ATTN_SCALE
            m_old = m_s[...]
            m_new = jnp.maximum(m_old, jnp.max(s, axis=-1, keepdims=True))
            alpha = jnp.exp(m_old - m_new)
            p = jnp.exp(s - m_new)
            l_s[...] = alpha * l_s[...] + jnp.sum(p, axis=-1, keepdims=True)
            acc_s[...] = alpha * acc_s[...] + jnp.dot(p.astype(_BF16), vs, preferred_element_type=_F32)
            m_s[...] = m_new
            return carry

        lax.fori_loop(0, n_chunks, step, 0)
        o4 = acc_s[...] / l_s[...]
        lse4 = m_s[...] + jnp.log(l_s[...])
        lane = lax.broadcasted_iota(jnp.int32, (tq, LANES), 1)
        lse_blk = jnp.zeros((tq, LANES), _F32)
        for g in range(GQA_GROUP):
            og = o4[g * tq:(g + 1) * tq]
            cols = slice(g * HEAD_DIM, (g + 1) * HEAD_DIM)
            o_ref[:, cols] = og
            y_ref[:, cols] = (og * _silu(g_ref[:, cols])).astype(_BF16)
            lse_blk = jnp.where(lane == g, lse4[g * tq:(g + 1) * tq], lse_blk)
        lse_ref[0] = lse_blk

    return pl.pallas_call(
        body, name=name, grid=(kv_heads, n // tq),
        in_specs=[pl.BlockSpec((GQA_GROUP, tq, HEAD_DIM), lambda h, i: (h, i + off, 0)),
                  pl.BlockSpec((1, t, HEAD_DIM), lambda h, i: (h, 0, 0)),
                  pl.BlockSpec((1, t, HEAD_DIM), lambda h, i: (h, 0, 0)),
                  pl.BlockSpec((tq, gw), lambda h, i: (i + off, gate_col + h))],
        out_specs=[pl.BlockSpec((tq, gw), lambda h, i: (i, h)),
                   pl.BlockSpec((tq, gw), lambda h, i: (i, h)),
                   pl.BlockSpec((1, tq, LANES), lambda h, i: (h, i, 0))],
        out_shape=[jax.ShapeDtypeStruct((n, aw), _F32), jax.ShapeDtypeStruct((n, mix), _BF16),
                   jax.ShapeDtypeStruct((kv_heads, n, LANES), _F32)],
        scratch_shapes=[pltpu.VMEM((rows, 1), _F32), pltpu.VMEM((rows, 1), _F32), pltpu.VMEM((rows, HEAD_DIM), _F32)],
        compiler_params=pltpu.CompilerParams(dimension_semantics=("parallel", "parallel")),
    )(q, k, v, proj)


def _attention_backward(q, k, v, attn_o, dy, proj, lse, l, tq, ck, name):
    heads, t, _ = q.shape
    kv_heads = k.shape[0]
    n = t - l
    rows = GQA_GROUP * tq
    gw = GQA_GROUP * HEAD_DIM
    aw = heads * HEAD_DIM
    gate_col = (aw + 2 * kv_heads * HEAD_DIM) // gw
    off = l // tq
    n_chunks = t // ck

    def body(q_ref, k_ref, v_ref, o_ref, dy_ref, g_ref, lse_ref, dq_ref, dg_ref, dk_ref, dv_ref, dq_s):
        @pl.when(pl.program_id(1) == 0)
        def _():
            dk_ref[...] = jnp.zeros_like(dk_ref)
            dv_ref[...] = jnp.zeros_like(dv_ref)

        q4 = q_ref[...].reshape(rows, HEAD_DIM)
        do_parts, delta_parts, lse_parts = [], [], []
        lse_blk = lse_ref[0]
        for g in range(GQA_GROUP):
            cols = slice(g * HEAD_DIM, (g + 1) * HEAD_DIM)
            gate, og, dyg = g_ref[:, cols], o_ref[:, cols], dy_ref[:, cols]
            dog = dyg * _silu(gate)
            dg_ref[:, cols] = (dyg * og * _silu_grad(gate)).astype(_BF16)
            do_parts.append(dog)
            delta_parts.append(jnp.sum(dog * og, axis=-1, keepdims=True))
            lse_parts.append(lse_blk[:, g:g + 1])
        do4 = jnp.concatenate(do_parts, axis=0).astype(_BF16)
        delta4 = jnp.concatenate(delta_parts, axis=0)
        lse4 = jnp.concatenate(lse_parts, axis=0)
        dq_s[...] = jnp.zeros_like(dq_s)

        def step(ci, carry):
            start = pl.multiple_of(ci * ck, ck)
            ks = k_ref[0, pl.ds(start, ck), :]
            vs = v_ref[0, pl.ds(start, ck), :]
            s = lax.dot_general(q4, ks, (((1,), (1,)), ((), ())), preferred_element_type=_F32) * ATTN_SCALE
            p = jnp.exp(s - lse4)
            dp = lax.dot_general(do4, vs, (((1,), (1,)), ((), ())), preferred_element_type=_F32)
            ds = (p * (dp - delta4) * ATTN_SCALE).astype(_BF16)
            dq_s[...] += jnp.dot(ds, ks, preferred_element_type=_F32)
            dk_ref[0, pl.ds(start, ck), :] += lax.dot_general(ds, q4, (((0,), (0,)), ((), ())), preferred_element_type=_F32)
            dv_ref[0, pl.ds(start, ck), :] += lax.dot_general(
                p.astype(_BF16), do4, (((0,), (0,)), ((), ())), preferred_element_type=_F32)
            return carry

        lax.fori_loop(0, n_chunks, step, 0)
        dq_ref[...] = dq_s[...].reshape(GQA_GROUP, tq, HEAD_DIM)

    kv_spec = pl.BlockSpec((1, t, HEAD_DIM), lambda h, i: (h, 0, 0))
    tok = pl.BlockSpec((tq, gw), lambda h, i: (i, h))
    return pl.pallas_call(
        body, name=name, grid=(kv_heads, n // tq),
        in_specs=[pl.BlockSpec((GQA_GROUP, tq, HEAD_DIM), lambda h, i: (h, i + off, 0)), kv_spec, kv_spec,
                  tok, tok, pl.BlockSpec((tq, gw), lambda h, i: (i + off, gate_col + h)),
                  pl.BlockSpec((1, tq, LANES), lambda h, i: (h, i, 0))],
        out_specs=[pl.BlockSpec((GQA_GROUP, tq, HEAD_DIM), lambda h, i: (h, i, 0)), tok, kv_spec, kv_spec],
        out_shape=[jax.ShapeDtypeStruct((heads, n, HEAD_DIM), _F32), jax.ShapeDtypeStruct((n, aw), _BF16),
                   jax.ShapeDtypeStruct((kv_heads, t, HEAD_DIM), _F32), jax.ShapeDtypeStruct((kv_heads, t, HEAD_DIM), _F32)],
        scratch_shapes=[pltpu.VMEM((rows, HEAD_DIM), _F32)],
        compiler_params=pltpu.CompilerParams(dimension_semantics=("parallel", "arbitrary")),
    )(q, k, v, attn_o, dy, proj, lse)


def _halo_specs(tp, width, col, row_off, total_rows):
    per = tp // POOL_HALO
    first = row_off // POOL_HALO
    last = total_rows // POOL_HALO - 1
    return [pl.BlockSpec((tp, width), lambda i: (i + row_off // tp, col)),
            pl.BlockSpec((POOL_HALO, width), lambda i: (jnp.maximum(first + i * per - 1, 0), col)),
            pl.BlockSpec((POOL_HALO, width), lambda i: (jnp.minimum(first + (i + 1) * per, last), col))]


def _with_halo(cur, prev, nxt, t0, n):
    tp = cur.shape[0]
    r8 = lax.broadcasted_iota(jnp.int32, (POOL_HALO, 1), 0)
    prev = jnp.where(t0 - POOL_HALO + r8 >= 0, prev, 0.0)
    nxt = jnp.where(t0 + tp + r8 < n, nxt, 0.0)
    return jnp.concatenate([prev, cur, nxt], axis=0)


def _shift_rows(a, s):
    return pltpu.roll(a, s % a.shape[0], 0)


def _window_sum(e, w, mirrored):
    a = e + _shift_rows(e, -1 if mirrored else 1)
    s = 1
    while 2 * s < w:
        a = _shift_rows(a, s) + _shift_rows(a, -s)
        s *= 2
    return a


def _window_count(t, w, n):
    half = w // 2
    return (jnp.minimum(t + half, n) - jnp.maximum(t - half, 0)).astype(_F32)


def _pool_forward(gi, proj, y, pool_w, pool_scale, l, heads, kv_heads, tp, name):
    t = proj.shape[0]
    n = t - l
    pg = pool_w.shape[-1]
    w = POOL_WINDOWS[gi]
    aw, kw = heads * HEAD_DIM, kv_heads * HEAD_DIM
    u_col = (2 * aw + 2 * kw) // pg + gi
    gate_col = (2 * aw + 2 * kw + len(POOL_WINDOWS) * pg) // pg + gi

    def body(u_ref, up_ref, un_ref, g_ref, w_ref, sc_ref, y_in_ref, y_ref, raw_ref, d_ref):
        del y_in_ref
        t0 = pl.program_id(0) * tp
        cur = u_ref[...]
        win = _window_sum(_with_halo(cur, up_ref[...], un_ref[...], t0, n), w, False)[POOL_HALO:POOL_HALO + tp]
        tok = t0 + lax.broadcasted_iota(jnp.int32, (tp, 1), 0)
        d = (win / _window_count(tok, w, n) - cur).astype(_BF16)
        raw = jnp.dot(d, w_ref[0], preferred_element_type=_F32)
        d_ref[...] = d
        raw_ref[...] = raw
        y_ref[...] = ((raw * sc_ref[...]) * _silu(g_ref[...])).astype(_BF16)

    blk = pl.BlockSpec((tp, pg), lambda i: (i, 0))
    return pl.pallas_call(
        body, name=name, grid=(n // tp,),
        in_specs=_halo_specs(tp, pg, u_col, l, t) + [
            pl.BlockSpec((tp, pg), lambda i: (i + l // tp, gate_col)),
            pl.BlockSpec((1, pg, pg), lambda i: (gi, 0, 0)),
            pl.BlockSpec((1, pg), lambda i: (0, gi)), _ANY],
        out_specs=[pl.BlockSpec((tp, pg), lambda i: (i, aw // pg + gi)), blk, blk],
        out_shape=[jax.ShapeDtypeStruct(y.shape, y.dtype), jax.ShapeDtypeStruct((n, pg), _F32),
                   jax.ShapeDtypeStruct((n, pg), _BF16)],
        input_output_aliases={6: 0},
        compiler_params=pltpu.CompilerParams(dimension_semantics=("arbitrary",)),
    )(proj, proj, proj, proj, pool_w, pool_scale, y)


def _pool_backward_gate(gi, dy, proj, raw, pool_w, pool_scale, l, heads, kv_heads, tp, name):
    n, pg = raw.shape
    aw, kw = heads * HEAD_DIM, kv_heads * HEAD_DIM
    gate_col = (2 * aw + 2 * kw + len(POOL_WINDOWS) * pg) // pg + gi

    def body(dy_ref, g_ref, raw_ref, w_ref, sc_ref, dg_ref, dr_ref, dd_ref, ds_ref):
        @pl.when(pl.program_id(0) == 0)
        def _():
            ds_ref[...] = jnp.zeros_like(ds_ref)

        gate, rawv, dyv, scale = g_ref[...], raw_ref[...], dy_ref[...], sc_ref[...]
        dpool = dyv * _silu(gate)
        dg_ref[...] = (dyv * (rawv * scale) * _silu_grad(gate)).astype(_BF16)
        ds_ref[0:1, :] += jnp.sum(dpool * rawv, axis=0, keepdims=True)
        draw = (dpool * scale).astype(_BF16)
        dr_ref[...] = draw
        dd_ref[...] = lax.dot_general(draw, w_ref[0], (((1,), (1,)), ((), ())), preferred_element_type=_F32)

    blk = pl.BlockSpec((tp, pg), lambda i: (i, 0))
    return pl.pallas_call(
        body, name=name, grid=(n // tp,),
        in_specs=[pl.BlockSpec((tp, pg), lambda i: (i, aw // pg + gi)),
                  pl.BlockSpec((tp, pg), lambda i: (i + l // tp, gate_col)), blk,
                  pl.BlockSpec((1, pg, pg), lambda i: (gi, 0, 0)), pl.BlockSpec((1, pg), lambda i: (0, gi))],
        out_specs=[blk, blk, blk, pl.BlockSpec((SUBLANES, pg), lambda i: (0, 0))],
        out_shape=[jax.ShapeDtypeStruct((n, pg), _BF16), jax.ShapeDtypeStruct((n, pg), _BF16),
                   jax.ShapeDtypeStruct((n, pg), _F32), jax.ShapeDtypeStruct((SUBLANES, pg), _F32)],
        compiler_params=pltpu.CompilerParams(dimension_semantics=("arbitrary",)),
    )(dy, proj, raw, pool_w, pool_scale)


def _pool_backward_window(gi, dd, tp, name):
    n, pg = dd.shape
    w = POOL_WINDOWS[gi]

    def body(c_ref, p_ref, n_ref, du_ref):
        t0 = pl.program_id(0) * tp
        cur = c_ref[...]
        e = _with_halo(cur, p_ref[...], n_ref[...], t0, n)
        tok = t0 - POOL_HALO + lax.broadcasted_iota(jnp.int32, (tp + 2 * POOL_HALO, 1), 0)
        e = e / jnp.maximum(_window_count(tok, w, n), 1.0)
        du_ref[...] = (_window_sum(e, w, True)[POOL_HALO:POOL_HALO + tp] - cur).astype(_BF16)

    return pl.pallas_call(
        body, name=name, grid=(n // tp,),
        in_specs=_halo_specs(tp, pg, 0, 0, n), out_specs=pl.BlockSpec((tp, pg), lambda i: (i, 0)),
        out_shape=jax.ShapeDtypeStruct((n, pg), _BF16),
        compiler_params=pltpu.CompilerParams(dimension_semantics=("parallel",)),
    )(dd, dd, dd)


def _post(out, x, target, gate, g_post, tr, name):
    n, d = out.shape

    def body(o_ref, x_ref, t_ref, gate_ref, g_ref, dxn_ref, do_ref, dgate_ref, dg_ref, loss_ref):
        @pl.when(pl.program_id(0) == 0)
        def _():
            dgate_ref[...] = jnp.zeros_like(dgate_ref)
            dg_ref[...] = jnp.zeros_like(dg_ref)
            loss_ref[...] = jnp.zeros_like(loss_ref)

        ov = o_ref[...]
        r = _rms(ov)
        on = ov * r
        normed = on * g_ref[...]
        err = (x_ref[...] + gate_ref[...] * normed) - t_ref[...]
        loss_ref[...] += jnp.sum(err * err)
        dxn = err / d
        dxn_ref[...] = dxn
        dgate_ref[0:1, :] += jnp.sum(dxn * normed, axis=0, keepdims=True)
        dr = dxn * gate_ref[...]
        dg_ref[0:1, :] += jnp.sum(dr * on, axis=0, keepdims=True)
        dn = dr * g_ref[...]
        do_ref[...] = (r * (dn - on * jnp.mean(dn * on, axis=-1, keepdims=True))).astype(_BF16)

    blk = pl.BlockSpec((tr, d), lambda i: (i, 0))
    vec = pl.BlockSpec((1, d), lambda i: (0, 0))
    acc = pl.BlockSpec((SUBLANES, d), lambda i: (0, 0))
    return pl.pallas_call(
        body, name=name, grid=(n // tr,),
        in_specs=[blk, blk, blk, vec, vec],
        out_specs=[blk, blk, acc, acc, pl.BlockSpec((SUBLANES, LANES), lambda i: (0, 0))],
        out_shape=[jax.ShapeDtypeStruct((n, d), _F32), jax.ShapeDtypeStruct((n, d), _BF16),
                   jax.ShapeDtypeStruct((SUBLANES, d), _F32), jax.ShapeDtypeStruct((SUBLANES, d), _F32),
                   jax.ShapeDtypeStruct((SUBLANES, LANES), _F32)],
        compiler_params=pltpu.CompilerParams(dimension_semantics=("arbitrary",)),
    )(out, x, target, gate, g_post)


def _adam_sharded(own, got, far, w, m, v, name):
    r, c = w.shape
    tr = _tile(r, max(BF16_ROWS, min(256, (1 << 18) // c)), BF16_ROWS)

    def body(own_ref, got_ref, far_ref, w_ref, m_ref, v_ref, g_ref, dl_ref, nm_ref, nv_ref):
        g = own_ref[0] + got_ref[0]
        for k in range(N_CHIPS - 1):
            g = g + far_ref[k].astype(_F32)
        delta, nm, nv = _adamw(w_ref[...], g, m_ref[...], v_ref[...])
        g_ref[...] = g
        dl_ref[...] = delta
        nm_ref[...] = nm
        nv_ref[...] = nv

    blk = pl.BlockSpec((tr, c), lambda i: (i, 0))
    return pl.pallas_call(
        body, name=name, grid=(r // tr,),
        in_specs=[pl.BlockSpec((1, tr, c), lambda i: (0, i, 0)), pl.BlockSpec((1, tr, c), lambda i: (0, i, 0)),
                  pl.BlockSpec((3, tr, c), lambda i: (0, i, 0)), blk, blk, blk],
        out_specs=[blk] * 4, out_shape=[jax.ShapeDtypeStruct((r, c), _F32)] * 4,
        compiler_params=pltpu.CompilerParams(dimension_semantics=("parallel",)),
    )(own, got, far, w, m, v)


def _adam_replicated(parts, extra, through_silu, w, m, v, name):
    def body(p_ref, e_ref, s_ref, w_ref, m_ref, v_ref, g_ref, dl_ref, nm_ref, nv_ref):
        total = p_ref[0] + e_ref[0]
        for dev in range(1, N_DEV):
            total = total + (p_ref[dev] + e_ref[dev])
        g = jnp.where(s_ref[...] > 0.5, total * _silu_grad(w_ref[...]), total)
        delta, nm, nv = _adamw(w_ref[...], g, m_ref[...], v_ref[...])
        g_ref[...] = g
        dl_ref[...] = delta
        nm_ref[...] = nm
        nv_ref[...] = nv

    return pl.pallas_call(
        body, name=name, in_specs=[_VMEM] * 6, out_specs=[_VMEM] * 4,
        out_shape=[jax.ShapeDtypeStruct(w.shape, _F32)] * 4,
    )(parts, extra, through_silu, w, m, v)


def _as_rows(vec):
    size = vec.shape[0]
    padded = -(-size // (SUBLANES * LANES)) * SUBLANES * LANES
    return jnp.pad(vec, (0, padded - size)).reshape(padded // LANES, LANES)


def kernel(x, c, ctx, c_ctx, w_ada, b_ada, norm_pre, norm_post, w_in, q_norm, k_norm, pool_w, pool_scale, w_out, loss_target, m_c_ctx, m_w_ada, m_b_ada, m_norm_pre, m_norm_post, m_w_in, m_q_norm, m_k_norm, m_pool_w, m_pool_scale, m_w_out, v_c_ctx, v_w_ada, v_b_ada, v_norm_pre, v_norm_post, v_w_in, v_q_norm, v_k_norm, v_pool_w, v_pool_scale, v_w_out):
    me = _dev_index(*_position())
    x2, ctx2, target = x[0], ctx[0], loss_target[0]
    n, d = x2.shape
    l = ctx2.shape[0]
    t = l + n
    aw = d // 2
    heads = aw // HEAD_DIM
    kv_heads = heads // GQA_GROUP
    kw = kv_heads * HEAD_DIM
    n_groups = len(POOL_WINDOWS)
    pg = (d - aw) // n_groups
    mix = d
    tr = _tile(l, 128, BF16_ROWS)
    tq = _tile(l, 128, BF16_ROWS)
    ck = _tile(t, 256, BF16_ROWS)
    tp = _tile(l, 512, POOL_HALO)

    w_in_b = _cast_bf16(w_in[0], "cast_w_in")
    w_out_b = _cast_bf16(w_out[0], "cast_w_out")
    pool_b = _cast_bf16(pool_w[0].reshape(-1, pg), "cast_pool_w").reshape(pool_w.shape[1:])
    w_in_g, w_out_g, pool_g = _gather_weights([w_in_b[None], w_out_b[None], pool_b], [2, 1, 1], "gather_weights")
    w_in_g, w_out_g = w_in_g[0], w_out_g[0]

    c_all = _all_gather_small(_as_rows(c[0]), "gather_c").reshape(N_DEV, -1)[:, :d]
    craw = jnp.concatenate([c_all, jnp.broadcast_to(c_ctx[None], (N_DEV, d))], axis=0)
    ada = _ada_forward(craw, w_ada[0], "ada_forward")
    ada_all = _all_gather_small(ada, "gather_ada")
    mod_all = ada_all.transpose(1, 0, 2).reshape(ada.shape[0], -1) + b_ada[0]
    mod = lax.dynamic_index_in_dim(mod_all, me, 0, keepdims=False)
    mod_c = mod_all[N_DEV]
    shift, scale, gate = mod[:d], mod[d:2 * d], mod[2 * d:]
    zeros6 = jnp.zeros((SUBLANES - 2, d), _F32)
    mods = jnp.stack([jnp.concatenate([mod_c[None, d:2 * d], mod_c[None, :d], zeros6], axis=0),
                      jnp.concatenate([scale[None], shift[None], zeros6], axis=0)])

    h_all = _prenorm(ctx2, x2, norm_pre, mods, tr, "prenorm")
    proj = _matmul(h_all, w_in_g, tm=1088, name="proj")
    tables = _rope_tables(l, n)
    q, k, v = _qkv_post(proj, tables, q_norm, k_norm, heads, kv_heads, tr, "qkv_post")
    attn_o, y, lse = _attention(q, k, v, proj, l, mix, tq, ck, "attention")
    raws, ds = [], []
    for gi in range(n_groups):
        y, raw, dsave = _pool_forward(gi, proj, y, pool_g, pool_scale, l, heads, kv_heads, tp, f"pool_forward_{gi}")
        raws.append(raw)
        ds.append(dsave)
    out = _matmul(y, w_out_g, name="out_proj")
    dxn, dout, dgate8, dgpost8, loss8 = _post(out, x2, target, gate[None], norm_post, tr, "post")

    dy = _matmul(dout, w_out_g, tb=True, name="d_y")
    gw_out = _matmul(y, dout, ta=True, name="grad_w_out")
    dq, dgate_attn, dk, dv = _attention_backward(q, k, v, attn_o, dy, proj, lse, l, tq, ck, "attention_backward")
    dqkv, dgq8, dgk8 = _qkv_post_backward(proj, dq, dk, dv, tables, q_norm, k_norm, l, tr, "qkv_post_backward")
    dus, dgps, gpw, dps8 = [], [], [], []
    for gi in range(n_groups):
        dgp, draw, dd, dps = _pool_backward_gate(
            gi, dy, proj, raws[gi], pool_g, pool_scale, l, heads, kv_heads, tp, f"pool_backward_gate_{gi}")
        dus.append(_pool_backward_window(gi, dd, tp, f"pool_backward_window_{gi}"))
        dgps.append(dgp)
        dps8.append(dps)
        gpw.append(_matmul(ds[gi], draw, ta=True, name=f"grad_pool_w_{gi}"))
    latent_cols = jnp.concatenate([dgate_attn] + dus + dgps, axis=1)
    dproj = jnp.concatenate([dqkv, jnp.pad(latent_cols, ((l, 0), (0, 0)))], axis=1)
    dh = _matmul(dproj, w_in_g, tb=True, tm=1088, name="d_h")
    gw_in = _matmul(h_all, dproj, ta=True, tk=2176, name="grad_w_in")
    grad_x, dmods, dgpre8 = _prenorm_backward(dh, ctx2, x2, dxn, norm_pre, mods, tr, "prenorm_backward")

    grads3 = [gw_in[None], gw_out[None], jnp.stack(gpw)]
    owns_gots = _exchange_sibling(grads3, [2, 1, 1], "exchange_sibling")
    owns, gots = owns_gots[:3], owns_gots[3:]
    flat = lambda a: a.reshape(a.shape[0], -1, a.shape[-1])
    sums = []
    for own, got, name in zip(owns, gots, ("pre_add_w_in", "pre_add_w_out", "pre_add_pool_w")):
        sums.append(_pre_add(flat(own), flat(got), name).reshape((N_CHIPS - 1,) + own.shape[1:]))
    fars = _exchange_chips(sums, "exchange_chips")
    sharded = []
    for own, got, far, w, m, v_, name in zip(
            owns, gots, fars, (w_in, w_out, pool_w), (m_w_in, m_w_out, m_pool_w), (v_w_in, v_w_out, v_pool_w),
            ("adam_w_in", "adam_w_out", "adam_pool_w")):
        two = lambda a: a.reshape(-1, a.shape[-1])
        res = _adam_sharded(flat(own), flat(got), flat(far), two(w), two(m), two(v_), name)
        sharded.append([r.reshape(w.shape) for r in res])
    (g_w_in, dl_w_in, nm_w_in, nv_w_in), (g_w_out, dl_w_out, nm_w_out, nv_w_out), (g_pw, dl_pw, nm_pw, nv_pw) = sharded

    dmod_lat = jnp.concatenate([dmods[1, 1], dmods[1, 0], dgate8[0]])
    dmod_ctx = jnp.concatenate([dmods[0, 1], dmods[0, 0], jnp.zeros((d,), _F32)])
    small = jnp.concatenate([dmod_lat, dmod_ctx, dgpre8[0], dgpost8[0], dgq8[0], dgk8[0]] + [p[0] for p in dps8]
                            + [loss8[0, :1]])
    gathered = _all_gather_small(_as_rows(small), "gather_small").reshape(N_DEV, -1)
    o = 0
    take = lambda size: (gathered[:, o:o + size], o + size)
    g_mod, o = take(3 * d)
    g_modc, o = take(3 * d)
    g_pre, o = take(d)
    g_post, o = take(d)
    g_q, o = take(HEAD_DIM)
    g_k, o = take(HEAD_DIM)
    g_ps, o = take(n_groups * pg)
    g_loss, o = take(1)
    cols = w_ada.shape[-1]
    mine = lambda a: lax.dynamic_slice_in_dim(a, me * cols, cols, axis=1)
    dmod_rows = jnp.concatenate([mine(g_mod), mine(g_modc)], axis=0)
    g_wada, dl_wada, nm_wada, nv_wada, dcact = _ada_backward(craw, dmod_rows, w_ada[0], m_w_ada[0], v_w_ada[0], "ada_backward")
    dcc = _all_gather_small(_as_rows(dcact[0]), "gather_dcc").reshape(N_DEV, -1)[:, :d]

    sizes = [d, 3 * d, d, d, HEAD_DIM, HEAD_DIM, n_groups * pg]
    pack = lambda parts: jnp.stack([_as_rows(jnp.concatenate([p[dev] for p in parts])) for dev in range(N_DEV)])
    zero = lambda size: jnp.zeros((N_DEV, size), _F32)
    parts = pack([dcc, g_mod, g_pre, g_post, g_q, g_k, g_ps])
    extra = pack([zero(d), g_modc, zero(d), zero(d), zero(HEAD_DIM), zero(HEAD_DIM), zero(n_groups * pg)])
    through_silu = _as_rows(jnp.concatenate([jnp.ones((d,), _F32), jnp.zeros((sum(sizes[1:]),), _F32)]))
    cat = lambda items: _as_rows(jnp.concatenate([a.reshape(-1) for a in items]))
    ws = [c_ctx, b_ada, norm_pre, norm_post, q_norm, k_norm, pool_scale]
    ms = [m_c_ctx, m_b_ada, m_norm_pre, m_norm_post, m_q_norm, m_k_norm, m_pool_scale]
    vs = [v_c_ctx, v_b_ada, v_norm_pre, v_norm_post, v_q_norm, v_k_norm, v_pool_scale]
    rep = _adam_replicated(parts, extra, through_silu, cat(ws), cat(ms), cat(vs), "adam_replicated")

    def split(packed):
        flat_, outs, at = packed.reshape(-1), [], 0
        for w, size in zip(ws, sizes):
            outs.append(flat_[at:at + size].reshape(w.shape))
            at += size
        return outs

    g_rep, dl_rep, nm_rep, nv_rep = [split(r) for r in rep]
    loss_sum = g_loss[0, 0]
    for dev in range(1, N_DEV):
        loss_sum = loss_sum + g_loss[dev, 0]
    loss = (0.5 / d) * loss_sum

    def ordered(rep_list, ada_, w_in_, pw_, w_out_):
        return [rep_list[0], ada_[None], rep_list[1], rep_list[2], rep_list[3], w_in_, rep_list[4], rep_list[5],
                pw_, rep_list[6], w_out_]

    return (loss, grad_x[None],
            *ordered(g_rep, g_wada, g_w_in, g_pw, g_w_out),
            *ordered(dl_rep, dl_wada, dl_w_in, dl_pw, dl_w_out),
            *ordered(nm_rep, nm_wada, nm_w_in, nm_pw, nm_w_out),
            *ordered(nv_rep, nv_wada, nv_w_in, nv_pw, nv_w_out))
```

```python
import functools

import jax
import jax.numpy as jnp
from jax import lax
from jax.experimental import pallas as pl
from jax.experimental.pallas import tpu as pltpu

HEAD_DIM = 128
GQA_GROUP = 4
ATTN_SUB_HEADS = 2
LOG2_E = 1.4426950408889634
GRID_W = 64
ROPE_PAIRS = HEAD_DIM // 4
ROPE_THETA = 10000.0
ATTN_SCALE = HEAD_DIM ** -0.5
EPS = 1e-6
POOL_WINDOWS = (2, 4, 8, 16)
POOL_HALO = 8
N_DEV = 8
N_CHIPS = 4
ADAM_LR = 0.001
ADAM_B1 = 0.9
ADAM_B2 = 0.999
ADAM_EPS = 1e-08
ADAM_WD = 0.01
ADAM_STEP = 10

LANES = 128
SUBLANES = 8
BF16_ROWS = 16

_MESH = pl.DeviceIdType.MESH
_ANY = pl.BlockSpec(memory_space=pl.ANY)
_VMEM = pl.BlockSpec(memory_space=pltpu.VMEM)
_F32 = jnp.float32
_BF16 = jnp.bfloat16


def _tile(dim, pref, align):
    t = min(pref, dim)
    t -= t % align
    while t >= align:
        if dim % t == 0:
            return t
        t -= align
    return dim


def _position():
    return lax.axis_index("x"), lax.axis_index("y"), lax.axis_index("c")


def _flip(v, bit):
    return 1 - v if bit else v


def _dev_index(x, y, c):
    return 4 * x + 2 * y + c


def _silu(g):
    return g * jax.nn.sigmoid(g)


def _silu_grad(g):
    s = jax.nn.sigmoid(g)
    return s * (1.0 + g * (1.0 - s))


def _adamw(w, g, m, v):
    m = ADAM_B1 * m + (1.0 - ADAM_B1) * g
    v = ADAM_B2 * v + (1.0 - ADAM_B2) * (g * g)
    m_hat = m / (1.0 - ADAM_B1 ** ADAM_STEP)
    v_hat = v / (1.0 - ADAM_B2 ** ADAM_STEP)
    delta = -ADAM_LR * (m_hat / (jnp.sqrt(v_hat) + ADAM_EPS) + ADAM_WD * w)
    return delta, m, v


def _all_gather_small(v, name):
    rows, cols = v.shape

    def body(v_ref, out_ref, send_sems, recv_sems):
        x, y, c = _position()
        me = _dev_index(x, y, c)
        out_ref[me] = v_ref[...]
        peers = [(_flip(x, k & 4), _flip(y, k & 2), _flip(c, k & 1)) for k in range(1, N_DEV)]

        def copy(k, block, to):
            return pltpu.make_async_remote_copy(
                src_ref=v_ref, dst_ref=out_ref.at[block], send_sem=send_sems.at[k], recv_sem=recv_sems.at[k],
                device_id=to, device_id_type=_MESH)

        sends = [copy(k, me, p) for k, p in enumerate(peers)]
        for s in sends:
            s.start()
        for k, p in enumerate(peers):
            copy(k, _dev_index(*p), p).wait_recv()
        for s in sends:
            s.wait_send()

    return pl.pallas_call(
        body, name=name,
        out_shape=jax.ShapeDtypeStruct((N_DEV, rows, cols), v.dtype),
        in_specs=[_VMEM], out_specs=_VMEM,
        scratch_shapes=[pltpu.SemaphoreType.DMA((N_DEV - 1,)), pltpu.SemaphoreType.DMA((N_DEV - 1,))],
    )(v)


def _window(ref, axis, size, j):
    start = pl.multiple_of(j * size, size)
    if axis == 0:
        return ref.at[pl.ds(start, size)]
    if axis == 1:
        return ref.at[:, pl.ds(start, size), :]
    return ref.at[:, :, pl.ds(start, size)]


def _gather_weights(shards, axes, name):
    n = len(shards)
    sizes = [s.shape[a] for s, a in zip(shards, axes)]
    out_shapes = [
        jax.ShapeDtypeStruct(tuple(d * N_DEV if i == a else d for i, d in enumerate(s.shape)), s.dtype)
        for s, a in zip(shards, axes)]

    def body(*refs):
        srcs, outs = refs[:n], refs[n:2 * n]
        send_sems, recv_sems, local_sems = refs[2 * n:]
        x, y, c = _position()
        me, sibling = (x, y, c), (x, y, 1 - c)
        chips = [(1 - x, y), (x, 1 - y), (1 - x, 1 - y)]
        firsts, passed, locals_ = [], [], []
        for a in range(n):
            def rows(block, a=a):
                return _window(outs[a], axes[a], sizes[a], _dev_index(*block))

            def copy(k, block, to, src=None, a=a, rows=rows):
                return pltpu.make_async_remote_copy(
                    src_ref=rows(block) if src is None else src, dst_ref=rows(block),
                    send_sem=send_sems.at[7 * a + k], recv_sem=recv_sems.at[7 * a + k],
                    device_id=to, device_id_type=_MESH)

            mine = pltpu.make_async_copy(srcs[a], rows(me), local_sems.at[a])
            mine.start()
            locals_.append(mine)
            first = [copy(0, me, sibling, src=srcs[a])]
            first += [copy(1 + j, me, (*chip, c), src=srcs[a]) for j, chip in enumerate(chips)]
            for cp in first:
                cp.start()
            firsts.append((first, copy))
        for a in range(n):
            first, copy = firsts[a]
            fwd = [copy(4 + j, (*chip, c), sibling) for j, chip in enumerate(chips)]
            for j, chip in enumerate(chips):
                copy(1 + j, (*chip, c), me).wait_recv()
                fwd[j].start()
            passed.append(fwd)
        for a in range(n):
            first, copy = firsts[a]
            copy(0, sibling, me).wait_recv()
            for j, chip in enumerate(chips):
                copy(4 + j, (*chip, 1 - c), me).wait_recv()
            for cp in first + passed[a]:
                cp.wait_send()
            locals_[a].wait()

    return pl.pallas_call(
        body, name=name, out_shape=out_shapes,
        in_specs=[_ANY] * n, out_specs=[_ANY] * n,
        scratch_shapes=[pltpu.SemaphoreType.DMA((7 * n,)), pltpu.SemaphoreType.DMA((7 * n,)),
                        pltpu.SemaphoreType.DMA((n,))],
    )(*shards)


def _exchange_sibling(grads, axes, name):
    n = len(grads)
    sizes = [g.shape[a] // N_DEV for g, a in zip(grads, axes)]
    slab = [tuple(sz if i == a else d for i, d in enumerate(g.shape)) for g, a, sz in zip(grads, axes, sizes)]
    out_shapes = ([jax.ShapeDtypeStruct((N_CHIPS,) + s, g.dtype) for s, g in zip(slab, grads)]
                  + [jax.ShapeDtypeStruct((N_CHIPS,) + s, g.dtype) for s, g in zip(slab, grads)])

    def body(*refs):
        srcs, owns, gots = refs[:n], refs[n:2 * n], refs[2 * n:3 * n]
        send_sems, recv_sems, local_sems = refs[3 * n:]
        x, y, c = _position()
        sibling = (x, y, 1 - c)
        chips = [(x, y), (1 - x, y), (x, 1 - y), (1 - x, 1 - y)]
        sends, locals_ = [], []
        for a in range(n):
            for s, chip in enumerate(chips):
                k = N_CHIPS * a + s
                keep = pltpu.make_async_copy(
                    _window(srcs[a], axes[a], sizes[a], _dev_index(*chip, c)), owns[a].at[s], local_sems.at[k])
                keep.start()
                locals_.append(keep)
                give = pltpu.make_async_remote_copy(
                    src_ref=_window(srcs[a], axes[a], sizes[a], _dev_index(*chip, 1 - c)), dst_ref=gots[a].at[s],
                    send_sem=send_sems.at[k], recv_sem=recv_sems.at[k], device_id=sibling, device_id_type=_MESH)
                give.start()
                sends.append(give)
        for cp in sends:
            cp.wait_recv()
        for cp in sends:
            cp.wait_send()
        for cp in locals_:
            cp.wait()

    return pl.pallas_call(
        body, name=name, out_shape=out_shapes,
        in_specs=[_ANY] * n, out_specs=[_ANY] * (2 * n),
        scratch_shapes=[pltpu.SemaphoreType.DMA((N_CHIPS * n,)), pltpu.SemaphoreType.DMA((N_CHIPS * n,)),
                        pltpu.SemaphoreType.DMA((N_CHIPS * n,))],
    )(*grads)


def _exchange_chips(sums, name):
    n = len(sums)

    def body(*refs):
        srcs, gots = refs[:n], refs[n:2 * n]
        send_sems, recv_sems = refs[2 * n:]
        x, y, c = _position()
        chips = [(1 - x, y), (x, 1 - y), (1 - x, 1 - y)]
        sends = []
        for a in range(n):
            for k, chip in enumerate(chips):
                cp = pltpu.make_async_remote_copy(
                    src_ref=srcs[a].at[k], dst_ref=gots[a].at[k],
                    send_sem=send_sems.at[3 * a + k], recv_sem=recv_sems.at[3 * a + k],
                    device_id=(*chip, c), device_id_type=_MESH)
                cp.start()
                sends.append(cp)
        for cp in sends:
            cp.wait_recv()
        for cp in sends:
            cp.wait_send()

    return pl.pallas_call(
        body, name=name, out_shape=[jax.ShapeDtypeStruct(s.shape, s.dtype) for s in sums],
        in_specs=[_ANY] * n, out_specs=[_ANY] * n,
        scratch_shapes=[pltpu.SemaphoreType.DMA((3 * n,)), pltpu.SemaphoreType.DMA((3 * n,))],
    )(*sums)


def _matmul(a, b, *, ta=False, tb=False, out_dtype=_F32, tm=1024, tn=1024, tk=1024, col_slabs=None, name):
    kdim, m = a.shape if ta else a.shape[::-1]
    n = b.shape[0] if tb else b.shape[1]
    tm = _tile(m, tm, LANES if ta else BF16_ROWS)
    tn = n // col_slabs if col_slabs else _tile(n, tn, LANES)
    tk = _tile(kdim, tk, BF16_ROWS if ta else LANES)
    nk = kdim // tk
    dims = (((0 if ta else 1,), (1 if tb else 0,)), ((), ()))

    def body(a_ref, b_ref, o_ref, acc_ref):
        k = pl.program_id(2)

        @pl.when(k == 0)
        def _():
            acc_ref[...] = jnp.zeros_like(acc_ref)

        acc_ref[...] += lax.dot_general(a_ref[...], b_ref[...], dims, preferred_element_type=_F32)

        @pl.when(k == nk - 1)
        def _():
            o_ref[...] = acc_ref[...].astype(out_dtype).reshape(o_ref.shape)

    a_spec = pl.BlockSpec((tk, tm), lambda i, j, k: (k, i)) if ta else pl.BlockSpec((tm, tk), lambda i, j, k: (i, k))
    b_spec = pl.BlockSpec((tn, tk), lambda i, j, k: (j, k)) if tb else pl.BlockSpec((tk, tn), lambda i, j, k: (k, j))
    if col_slabs:
        out_spec = pl.BlockSpec((1, tm, tn), lambda i, j, k: (j, i, 0))
        out_shape = jax.ShapeDtypeStruct((col_slabs, m, tn), out_dtype)
    else:
        out_spec = pl.BlockSpec((tm, tn), lambda i, j, k: (i, j))
        out_shape = jax.ShapeDtypeStruct((m, n), out_dtype)
    return pl.pallas_call(
        body, name=name, grid=(m // tm, n // tn, nk),
        in_specs=[a_spec, b_spec], out_specs=out_spec, out_shape=out_shape,
        scratch_shapes=[pltpu.VMEM((tm, tn), _F32)],
        compiler_params=pltpu.CompilerParams(dimension_semantics=("parallel", "parallel", "arbitrary")),
    )(a, b)


def _row_tile(rows, cols):
    return _tile(rows, max(BF16_ROWS, min(512, (1 << 19) // cols)), BF16_ROWS)


def _cast_bf16(a, name):
    r, c = a.shape
    tr = _row_tile(r, c)

    def body(a_ref, o_ref):
        o_ref[...] = a_ref[...].astype(_BF16)

    blk = pl.BlockSpec((tr, c), lambda i: (i, 0))
    return pl.pallas_call(
        body, name=name, grid=(r // tr,), in_specs=[blk], out_specs=blk,
        out_shape=jax.ShapeDtypeStruct((r, c), _BF16),
        compiler_params=pltpu.CompilerParams(dimension_semantics=("parallel",)),
    )(a)


def _pre_add(own, got, name):
    _, r, c = own.shape
    tr = _row_tile(r, c)

    def body(a_ref, b_ref, o_ref):
        o_ref[...] = (a_ref[...] + b_ref[...]).astype(_BF16)

    src = pl.BlockSpec((1, tr, c), lambda s, i: (s + 1, i, 0))
    return pl.pallas_call(
        body, name=name, grid=(N_CHIPS - 1, r // tr), in_specs=[src, src],
        out_specs=pl.BlockSpec((1, tr, c), lambda s, i: (s, i, 0)),
        out_shape=jax.ShapeDtypeStruct((N_CHIPS - 1, r, c), _BF16),
        compiler_params=pltpu.CompilerParams(dimension_semantics=("parallel", "parallel")),
    )(own, got)


def _ada_forward(craw, w_shard, name):
    d, cols = w_shard.shape
    tk = _tile(d, 512, LANES)

    def body(c_ref, w_ref, o_ref):
        @pl.when(pl.program_id(0) == 0)
        def _():
            o_ref[...] = jnp.zeros_like(o_ref)

        o_ref[...] += jnp.dot(_silu(c_ref[...]).astype(_BF16), w_ref[...].astype(_BF16), preferred_element_type=_F32)

    return pl.pallas_call(
        body, name=name, grid=(d // tk,),
        in_specs=[pl.BlockSpec((craw.shape[0], tk), lambda k: (0, k)), pl.BlockSpec((tk, cols), lambda k: (k, 0))],
        out_specs=pl.BlockSpec((craw.shape[0], cols), lambda k: (0, 0)),
        out_shape=jax.ShapeDtypeStruct((craw.shape[0], cols), _F32),
        compiler_params=pltpu.CompilerParams(dimension_semantics=("arbitrary",)),
    )(craw, w_shard)


def _ada_backward(craw, dmod, w, m, v, name):
    d, cols = w.shape
    rows = craw.shape[0]
    tr = _tile(d, 256, LANES)

    def body(c_ref, dm_ref, w_ref, m_ref, v_ref, g_ref, dl_ref, nm_ref, nv_ref, dc_ref):
        act = _silu(c_ref[...]).astype(_BF16)
        dmb = dm_ref[...].astype(_BF16)
        wv = w_ref[...]
        g = lax.dot_general(act, dmb, (((0,), (0,)), ((), ())), preferred_element_type=_F32)
        delta, nm, nv = _adamw(wv, g, m_ref[...], v_ref[...])
        g_ref[...] = g
        dl_ref[...] = delta
        nm_ref[...] = nm
        nv_ref[...] = nv
        dc = lax.dot_general(dmb, wv.astype(_BF16), (((1,), (1,)), ((), ())), preferred_element_type=_F32)
        dc_ref[...] = jnp.broadcast_to(jnp.sum(dc[N_DEV:], axis=0, keepdims=True), dc_ref.shape)

    blk = pl.BlockSpec((tr, cols), lambda i: (i, 0))
    return pl.pallas_call(
        body, name=name, grid=(d // tr,),
        in_specs=[pl.BlockSpec((rows, tr), lambda i: (0, i)), pl.BlockSpec((rows, cols), lambda i: (0, 0)), blk, blk, blk],
        out_specs=[blk, blk, blk, blk, pl.BlockSpec((SUBLANES, tr), lambda i: (0, i))],
        out_shape=[jax.ShapeDtypeStruct((d, cols), _F32)] * 4 + [jax.ShapeDtypeStruct((SUBLANES, d), _F32)],
        compiler_params=pltpu.CompilerParams(dimension_semantics=("parallel",)),
    )(craw, dmod, w, m, v)


def _rms(xf):
    return lax.rsqrt(jnp.mean(xf * xf, axis=-1, keepdims=True) + EPS)


def _prenorm(ctx, x, g_pre, mods, tr, name):
    l, d = ctx.shape
    n = x.shape[0]
    nbc = l // tr

    def body(ctx_ref, x_ref, g_ref, mod_ref, h_ref):
        def emit(src_ref):
            xf = src_ref[...]
            y = (xf * _rms(xf)) * g_ref[...]
            h_ref[...] = (y * (1.0 + mod_ref[0, 0:1, :]) + mod_ref[0, 1:2, :]).astype(_BF16)

        is_ctx = pl.program_id(0) < nbc
        pl.when(is_ctx)(lambda: emit(ctx_ref))
        pl.when(jnp.logical_not(is_ctx))(lambda: emit(x_ref))

    return pl.pallas_call(
        body, name=name, grid=((l + n) // tr,),
        in_specs=[pl.BlockSpec((tr, d), lambda i: (jnp.minimum(i, nbc - 1), 0)),
                  pl.BlockSpec((tr, d), lambda i: (jnp.maximum(i - nbc, 0), 0)),
                  pl.BlockSpec((1, d), lambda i: (0, 0)),
                  pl.BlockSpec((1, SUBLANES, d), lambda i: ((i >= nbc).astype(jnp.int32), 0, 0))],
        out_specs=pl.BlockSpec((tr, d), lambda i: (i, 0)),
        out_shape=jax.ShapeDtypeStruct((l + n, d), _BF16),
        compiler_params=pltpu.CompilerParams(dimension_semantics=("arbitrary",)),
    )(ctx, x, g_pre, mods)


def _prenorm_backward(dh, ctx, x, dxn, g_pre, mods, tr, name):
    l, d = ctx.shape
    n = x.shape[0]
    nbc = l // tr

    def body(dh_ref, ctx_ref, x_ref, dxn_ref, g_ref, mod_ref, gx_ref, dmod_ref, dg_ref):
        i = pl.program_id(0)

        @pl.when(i == 0)
        def _():
            dg_ref[...] = jnp.zeros_like(dg_ref)

        @pl.when(jnp.logical_or(i == 0, i == nbc))
        def _():
            dmod_ref[...] = jnp.zeros_like(dmod_ref)

        def emit(src_ref, latent):
            xf = src_ref[...]
            r = _rms(xf)
            xn = xf * r
            dhv = dh_ref[...]
            one_scale = 1.0 + mod_ref[0, 0:1, :]
            dmod_ref[0, 0:1, :] += jnp.sum(dhv * (xn * g_ref[...]), axis=0, keepdims=True)
            dmod_ref[0, 1:2, :] += jnp.sum(dhv, axis=0, keepdims=True)
            dyg = dhv * one_scale
            dg_ref[0:1, :] += jnp.sum(dyg * xn, axis=0, keepdims=True)
            if latent:
                dn = dyg * g_ref[...]
                gx_ref[...] = dxn_ref[...] + r * (dn - xn * jnp.mean(dn * xn, axis=-1, keepdims=True))

        pl.when(i < nbc)(lambda: emit(ctx_ref, False))
        pl.when(i >= nbc)(lambda: emit(x_ref, True))

    lat = pl.BlockSpec((tr, d), lambda i: (jnp.maximum(i - nbc, 0), 0))
    sel = pl.BlockSpec((1, SUBLANES, d), lambda i: ((i >= nbc).astype(jnp.int32), 0, 0))
    return pl.pallas_call(
        body, name=name, grid=((l + n) // tr,),
        in_specs=[pl.BlockSpec((tr, d), lambda i: (i, 0)),
                  pl.BlockSpec((tr, d), lambda i: (jnp.minimum(i, nbc - 1), 0)),
                  lat, lat, pl.BlockSpec((1, d), lambda i: (0, 0)), sel],
        out_specs=[lat, sel, pl.BlockSpec((SUBLANES, d), lambda i: (0, 0))],
        out_shape=[jax.ShapeDtypeStruct((n, d), _F32), jax.ShapeDtypeStruct((2, SUBLANES, d), _F32),
                   jax.ShapeDtypeStruct((SUBLANES, d), _F32)],
        compiler_params=pltpu.CompilerParams(dimension_semantics=("arbitrary",)),
    )(dh, ctx, x, dxn, g_pre, mods)


def _rope_tables(l, n):
    rows = n // GRID_W
    row = jnp.repeat(jnp.arange(rows, dtype=_F32), GRID_W)
    col = jnp.tile(jnp.arange(GRID_W, dtype=_F32), rows)
    inv = ROPE_THETA ** (-jnp.arange(ROPE_PAIRS, dtype=_F32) / ROPE_PAIRS)
    ang_r, ang_c = row[:, None] * inv, col[:, None] * inv
    cr, sr, cc, sc = jnp.cos(ang_r), jnp.sin(ang_r), jnp.cos(ang_c), jnp.sin(ang_c)
    zero = jnp.zeros_like(sr)
    tc = jnp.concatenate([cr, cr, cc, cc], axis=-1)
    ta = jnp.concatenate([-sr, zero, -sc, zero], axis=-1)
    tb = jnp.concatenate([zero, sr, zero, sc], axis=-1)
    pad = lambda t, fill: jnp.concatenate([jnp.full((l, HEAD_DIM), fill, _F32), t], axis=0)
    return pad(tc, 1.0), pad(ta, 0.0), pad(tb, 0.0)


def _rope(y, tc, ta, tb):
    return y * tc + pltpu.roll(y, HEAD_DIM - ROPE_PAIRS, 1) * ta + pltpu.roll(y, ROPE_PAIRS, 1) * tb


def _rope_transposed(dy, tc, ta, tb):
    return dy * tc + pltpu.roll(dy * ta, ROPE_PAIRS, 1) + pltpu.roll(dy * tb, HEAD_DIM - ROPE_PAIRS, 1)


def _qkv_post(proj, tables, g_q, g_k, heads, kv_heads, tr, name):
    t = proj.shape[0]
    aw, kw = heads * HEAD_DIM, kv_heads * HEAD_DIM
    w3 = aw + 2 * kw

    def body(p_ref, tc_ref, ta_ref, tb_ref, gq_ref, gk_ref, q_ref, k_ref, v_ref):
        tabs = (tc_ref[...], ta_ref[...], tb_ref[...])

        def norm_rope(col, gain):
            xh = p_ref[:, col:col + HEAD_DIM]
            return _rope((xh * _rms(xh)) * gain, *tabs).astype(_BF16)

        for h in range(heads):
            q_ref[h] = norm_rope(h * HEAD_DIM, gq_ref[...])
        for h in range(kv_heads):
            k_ref[h] = norm_rope(aw + h * HEAD_DIM, gk_ref[...])
            v_ref[h] = p_ref[:, aw + kw + h * HEAD_DIM:aw + kw + (h + 1) * HEAD_DIM].astype(_BF16)

    tab = pl.BlockSpec((tr, HEAD_DIM), lambda i: (i, 0))
    gain = pl.BlockSpec((1, HEAD_DIM), lambda i: (0, 0))
    return pl.pallas_call(
        body, name=name, grid=(t // tr,),
        in_specs=[pl.BlockSpec((tr, w3), lambda i: (i, 0)), tab, tab, tab, gain, gain],
        out_specs=[pl.BlockSpec((heads, tr, HEAD_DIM), lambda i: (0, i, 0)),
                   pl.BlockSpec((kv_heads, tr, HEAD_DIM), lambda i: (0, i, 0)),
                   pl.BlockSpec((kv_heads, tr, HEAD_DIM), lambda i: (0, i, 0))],
        out_shape=[jax.ShapeDtypeStruct((heads, t, HEAD_DIM), _BF16),
                   jax.ShapeDtypeStruct((kv_heads, t, HEAD_DIM), _BF16),
                   jax.ShapeDtypeStruct((kv_heads, t, HEAD_DIM), _BF16)],
        compiler_params=pltpu.CompilerParams(dimension_semantics=("parallel",)),
    )(proj, *tables, g_q, g_k)


def _qkv_post_backward(proj, dq, dk, dv, tables, g_q, g_k, l, tr, name):
    t = proj.shape[0]
    heads, kv_heads = dq.shape[0], dk.shape[0]
    aw, kw = heads * HEAD_DIM, kv_heads * HEAD_DIM
    w3 = aw + 2 * kw
    nbc = l // tr

    def body(p_ref, dq_ref, dk_ref, dv_ref, tc_ref, ta_ref, tb_ref, gq_ref, gk_ref, o_ref, dgq_ref, dgk_ref):
        i = pl.program_id(0)

        @pl.when(i == 0)
        def _():
            dgq_ref[...] = jnp.zeros_like(dgq_ref)
            dgk_ref[...] = jnp.zeros_like(dgk_ref)

        tabs = (tc_ref[...], ta_ref[...], tb_ref[...])
        latent = i >= nbc

        def back(col, dout, gain, dg_ref):
            xh = p_ref[:, col:col + HEAD_DIM]
            r = _rms(xh)
            xn = xh * r
            dy = _rope_transposed(dout, *tabs)
            dg_ref[0:1, :] += jnp.sum(dy * xn, axis=0, keepdims=True)
            dn = dy * gain
            o_ref[:, col:col + HEAD_DIM] = (r * (dn - xn * jnp.mean(dn * xn, axis=-1, keepdims=True))).astype(_BF16)

        for h in range(heads):
            back(h * HEAD_DIM, jnp.where(latent, dq_ref[h], 0.0), gq_ref[...], dgq_ref)
        for h in range(kv_heads):
            back(aw + h * HEAD_DIM, dk_ref[h], gk_ref[...], dgk_ref)
            o_ref[:, aw + kw + h * HEAD_DIM:aw + kw + (h + 1) * HEAD_DIM] = dv_ref[h].astype(_BF16)

    tab = pl.BlockSpec((tr, HEAD_DIM), lambda i: (i, 0))
    gain = pl.BlockSpec((1, HEAD_DIM), lambda i: (0, 0))
    acc = pl.BlockSpec((SUBLANES, HEAD_DIM), lambda i: (0, 0))
    return pl.pallas_call(
        body, name=name, grid=(t // tr,),
        in_specs=[pl.BlockSpec((tr, w3), lambda i: (i, 0)),
                  pl.BlockSpec((heads, tr, HEAD_DIM), lambda i: (0, jnp.maximum(i - nbc, 0), 0)),
                  pl.BlockSpec((kv_heads, tr, HEAD_DIM), lambda i: (0, i, 0)),
                  pl.BlockSpec((kv_heads, tr, HEAD_DIM), lambda i: (0, i, 0)),
                  tab, tab, tab, gain, gain],
        out_specs=[pl.BlockSpec((tr, w3), lambda i: (i, 0)), acc, acc],
        out_shape=[jax.ShapeDtypeStruct((t, w3), _BF16), jax.ShapeDtypeStruct((SUBLANES, HEAD_DIM), _F32),
                   jax.ShapeDtypeStruct((SUBLANES, HEAD_DIM), _F32)],
        compiler_params=pltpu.CompilerParams(dimension_semantics=("arbitrary",)),
    )(proj, dq, dk, dv, *tables, g_q, g_k)


def _attention(q, k, v, proj, l, mix, tq, name):
    heads, t, _ = q.shape
    kv_heads = k.shape[0]
    n = t - l
    rows = GQA_GROUP * tq
    gw = GQA_GROUP * HEAD_DIM
    aw = heads * HEAD_DIM
    gate_col = (aw + 2 * kv_heads * HEAD_DIM) // gw
    off = l // tq

    def body(q_ref, k_ref, v_ref, g_ref, o_ref, y_ref, lse_ref):
        lane = lax.broadcasted_iota(jnp.int32, (tq, LANES), 1)
        lse_blk = jnp.zeros((tq, LANES), _F32)
        for first in range(0, GQA_GROUP, ATTN_SUB_HEADS):
            qs = q_ref[first:first + ATTN_SUB_HEADS].reshape(ATTN_SUB_HEADS * tq, HEAD_DIM)
            raw = lax.dot_general(qs, k_ref[0], (((1,), (1,)), ((), ())), preferred_element_type=_F32)
            m = jnp.max(raw, axis=-1, keepdims=True)
            p = jnp.exp2((raw - m) * (ATTN_SCALE * LOG2_E))
            denom = jnp.sum(p, axis=-1, keepdims=True)
            os_ = jnp.dot(p.astype(_BF16), v_ref[0], preferred_element_type=_F32) / denom
            lse_s = m * ATTN_SCALE + jnp.log(denom)
            for j in range(ATTN_SUB_HEADS):
                g = first + j
                og = os_[j * tq:(j + 1) * tq]
                cols = slice(g * HEAD_DIM, (g + 1) * HEAD_DIM)
                o_ref[:, cols] = og
                y_ref[:, cols] = (og * _silu(g_ref[:, cols])).astype(_BF16)
                lse_blk = jnp.where(lane == g, lse_s[j * tq:(j + 1) * tq], lse_blk)
        lse_ref[0] = lse_blk

    return pl.pallas_call(
        body, name=name, grid=(kv_heads, n // tq),
        in_specs=[pl.BlockSpec((GQA_GROUP, tq, HEAD_DIM), lambda h, i: (h, i + off, 0)),
                  pl.BlockSpec((1, t, HEAD_DIM), lambda h, i: (h, 0, 0)),
                  pl.BlockSpec((1, t, HEAD_DIM), lambda h, i: (h, 0, 0)),
                  pl.BlockSpec((tq, gw), lambda h, i: (i + off, gate_col + h))],
        out_specs=[pl.BlockSpec((tq, gw), lambda h, i: (i, h)),
                   pl.BlockSpec((tq, gw), lambda h, i: (i, h)),
                   pl.BlockSpec((1, tq, LANES), lambda h, i: (h, i, 0))],
        out_shape=[jax.ShapeDtypeStruct((n, aw), _F32), jax.ShapeDtypeStruct((n, mix), _BF16),
                   jax.ShapeDtypeStruct((kv_heads, n, LANES), _F32)],
        compiler_params=pltpu.CompilerParams(dimension_semantics=("parallel", "parallel")),
    )(q, k, v, proj)


def _attention_backward(q, k, v, attn_o, dy, proj, lse, l, tq, name):
    heads, t, _ = q.shape
    kv_heads = k.shape[0]
    n = t - l
    rows = GQA_GROUP * tq
    gw = GQA_GROUP * HEAD_DIM
    aw = heads * HEAD_DIM
    gate_col = (aw + 2 * kv_heads * HEAD_DIM) // gw
    off = l // tq
    n_parts = 2 if t % (2 * BF16_ROWS) == 0 else 1
    part = t // n_parts

    def body(q_ref, k_ref, v_ref, o_ref, dy_ref, g_ref, lse_ref, dq_ref, dg_ref, dk_ref, dv_ref):
        @pl.when(pl.program_id(1) == 0)
        def _():
            dk_ref[...] = jnp.zeros_like(dk_ref)
            dv_ref[...] = jnp.zeros_like(dv_ref)

        q4 = q_ref[...].reshape(rows, HEAD_DIM)
        do_parts, delta_parts, lse_parts = [], [], []
        lse_blk = lse_ref[0]
        for g in range(GQA_GROUP):
            cols = slice(g * HEAD_DIM, (g + 1) * HEAD_DIM)
            gate, og, dyg = g_ref[:, cols], o_ref[:, cols], dy_ref[:, cols]
            dog = dyg * _silu(gate)
            dg_ref[:, cols] = (dyg * og * _silu_grad(gate)).astype(_BF16)
            do_parts.append(dog)
            delta_parts.append(jnp.sum(dog * og, axis=-1, keepdims=True))
            lse_parts.append(lse_blk[:, g:g + 1])
        do4 = jnp.concatenate(do_parts, axis=0).astype(_BF16)
        delta4 = jnp.concatenate(delta_parts, axis=0)
        lse4 = jnp.concatenate(lse_parts, axis=0)
        dq4 = jnp.zeros((rows, HEAD_DIM), _F32)
        for part_i in range(n_parts):
            keys = slice(part_i * part, (part_i + 1) * part)
            ks, vs = k_ref[0, keys, :], v_ref[0, keys, :]
            s = lax.dot_general(q4, ks, (((1,), (1,)), ((), ())), preferred_element_type=_F32) * ATTN_SCALE
            p = jnp.exp(s - lse4)
            dp = lax.dot_general(do4, vs, (((1,), (1,)), ((), ())), preferred_element_type=_F32)
            ds = (p * (dp - delta4) * ATTN_SCALE).astype(_BF16)
            dq4 = dq4 + jnp.dot(ds, ks, preferred_element_type=_F32)
            dk_ref[0, keys, :] += lax.dot_general(ds, q4, (((0,), (0,)), ((), ())), preferred_element_type=_F32)
            dv_ref[0, keys, :] += lax.dot_general(
                p.astype(_BF16), do4, (((0,), (0,)), ((), ())), preferred_element_type=_F32)
        dq_ref[...] = dq4.reshape(GQA_GROUP, tq, HEAD_DIM)

    kv_spec = pl.BlockSpec((1, t, HEAD_DIM), lambda h, i: (h, 0, 0))
    tok = pl.BlockSpec((tq, gw), lambda h, i: (i, h))
    return pl.pallas_call(
        body, name=name, grid=(kv_heads, n // tq),
        in_specs=[pl.BlockSpec((GQA_GROUP, tq, HEAD_DIM), lambda h, i: (h, i + off, 0)), kv_spec, kv_spec,
                  tok, tok, pl.BlockSpec((tq, gw), lambda h, i: (i + off, gate_col + h)),
                  pl.BlockSpec((1, tq, LANES), lambda h, i: (h, i, 0))],
        out_specs=[pl.BlockSpec((GQA_GROUP, tq, HEAD_DIM), lambda h, i: (h, i, 0)), tok, kv_spec, kv_spec],
        out_shape=[jax.ShapeDtypeStruct((heads, n, HEAD_DIM), _F32), jax.ShapeDtypeStruct((n, aw), _BF16),
                   jax.ShapeDtypeStruct((kv_heads, t, HEAD_DIM), _F32), jax.ShapeDtypeStruct((kv_heads, t, HEAD_DIM), _F32)],
        compiler_params=pltpu.CompilerParams(dimension_semantics=("parallel", "arbitrary")),
    )(q, k, v, attn_o, dy, proj, lse)


def _halo_specs(tp, width, col, row_off, total_rows):
    per = tp // POOL_HALO
    first = row_off // POOL_HALO
    last = total_rows // POOL_HALO - 1
    return [pl.BlockSpec((tp, width), lambda i: (i + row_off // tp, col)),
            pl.BlockSpec((POOL_HALO, width), lambda i: (jnp.maximum(first + i * per - 1, 0), col)),
            pl.BlockSpec((POOL_HALO, width), lambda i: (jnp.minimum(first + (i + 1) * per, last), col))]


def _with_halo(cur, prev, nxt, t0, n):
    tp = cur.shape[0]
    r8 = lax.broadcasted_iota(jnp.int32, (POOL_HALO, 1), 0)
    prev = jnp.where(t0 - POOL_HALO + r8 >= 0, prev, 0.0)
    nxt = jnp.where(t0 + tp + r8 < n, nxt, 0.0)
    return jnp.concatenate([prev, cur, nxt], axis=0)


def _shift_rows(a, s):
    return pltpu.roll(a, s % a.shape[0], 0)


def _window_sum(e, w, mirrored):
    a = e + _shift_rows(e, -1 if mirrored else 1)
    s = 1
    while 2 * s < w:
        a = _shift_rows(a, s) + _shift_rows(a, -s)
        s *= 2
    return a


def _window_count(t, w, n):
    half = w // 2
    return (jnp.minimum(t + half, n) - jnp.maximum(t - half, 0)).astype(_F32)


def _pool_forward(gi, proj, y, pool_w, pool_scale, l, heads, kv_heads, tp, name):
    t = proj.shape[0]
    n = t - l
    pg = pool_w.shape[-1]
    w = POOL_WINDOWS[gi]
    aw, kw = heads * HEAD_DIM, kv_heads * HEAD_DIM
    u_col = (2 * aw + 2 * kw) // pg + gi
    gate_col = (2 * aw + 2 * kw + len(POOL_WINDOWS) * pg) // pg + gi

    def body(u_ref, up_ref, un_ref, g_ref, w_ref, sc_ref, y_in_ref, y_ref, raw_ref, d_ref):
        del y_in_ref
        t0 = pl.program_id(0) * tp
        cur = u_ref[...]
        win = _window_sum(_with_halo(cur, up_ref[...], un_ref[...], t0, n), w, False)[POOL_HALO:POOL_HALO + tp]
        tok = t0 + lax.broadcasted_iota(jnp.int32, (tp, 1), 0)
        d = (win / _window_count(tok, w, n) - cur).astype(_BF16)
        raw = jnp.dot(d, w_ref[0], preferred_element_type=_F32)
        d_ref[...] = d
        raw_ref[...] = raw
        y_ref[...] = ((raw * sc_ref[...]) * _silu(g_ref[...])).astype(_BF16)

    blk = pl.BlockSpec((tp, pg), lambda i: (i, 0))
    return pl.pallas_call(
        body, name=name, grid=(n // tp,),
        in_specs=_halo_specs(tp, pg, u_col, l, t) + [
            pl.BlockSpec((tp, pg), lambda i: (i + l // tp, gate_col)),
            pl.BlockSpec((1, pg, pg), lambda i: (gi, 0, 0)),
            pl.BlockSpec((1, pg), lambda i: (0, gi)), _ANY],
        out_specs=[pl.BlockSpec((tp, pg), lambda i: (i, aw // pg + gi)), blk, blk],
        out_shape=[jax.ShapeDtypeStruct(y.shape, y.dtype), jax.ShapeDtypeStruct((n, pg), _F32),
                   jax.ShapeDtypeStruct((n, pg), _BF16)],
        input_output_aliases={6: 0},
        compiler_params=pltpu.CompilerParams(dimension_semantics=("arbitrary",)),
    )(proj, proj, proj, proj, pool_w, pool_scale, y)


def _pool_backward_gate(gi, dy, proj, raw, pool_w, pool_scale, l, heads, kv_heads, tp, name):
    n, pg = raw.shape
    aw, kw = heads * HEAD_DIM, kv_heads * HEAD_DIM
    gate_col = (2 * aw + 2 * kw + len(POOL_WINDOWS) * pg) // pg + gi

    def body(dy_ref, g_ref, raw_ref, w_ref, sc_ref, dg_ref, dr_ref, dd_ref, ds_ref):
        @pl.when(pl.program_id(0) == 0)
        def _():
            ds_ref[...] = jnp.zeros_like(ds_ref)

        gate, rawv, dyv, scale = g_ref[...], raw_ref[...], dy_ref[...], sc_ref[...]
        dpool = dyv * _silu(gate)
        dg_ref[...] = (dyv * (rawv * scale) * _silu_grad(gate)).astype(_BF16)
        ds_ref[0:1, :] += jnp.sum(dpool * rawv, axis=0, keepdims=True)
        draw = (dpool * scale).astype(_BF16)
        dr_ref[...] = draw
        dd_ref[...] = lax.dot_general(draw, w_ref[0], (((1,), (1,)), ((), ())), preferred_element_type=_F32)

    blk = pl.BlockSpec((tp, pg), lambda i: (i, 0))
    return pl.pallas_call(
        body, name=name, grid=(n // tp,),
        in_specs=[pl.BlockSpec((tp, pg), lambda i: (i, aw // pg + gi)),
                  pl.BlockSpec((tp, pg), lambda i: (i + l // tp, gate_col)), blk,
                  pl.BlockSpec((1, pg, pg), lambda i: (gi, 0, 0)), pl.BlockSpec((1, pg), lambda i: (0, gi))],
        out_specs=[blk, blk, blk, pl.BlockSpec((SUBLANES, pg), lambda i: (0, 0))],
        out_shape=[jax.ShapeDtypeStruct((n, pg), _BF16), jax.ShapeDtypeStruct((n, pg), _BF16),
                   jax.ShapeDtypeStruct((n, pg), _F32), jax.ShapeDtypeStruct((SUBLANES, pg), _F32)],
        compiler_params=pltpu.CompilerParams(dimension_semantics=("arbitrary",)),
    )(dy, proj, raw, pool_w, pool_scale)


def _pool_backward_window(gi, dd, tp, name):
    n, pg = dd.shape
    w = POOL_WINDOWS[gi]

    def body(c_ref, p_ref, n_ref, du_ref):
        t0 = pl.program_id(0) * tp
        cur = c_ref[...]
        e = _with_halo(cur, p_ref[...], n_ref[...], t0, n)
        tok = t0 - POOL_HALO + lax.broadcasted_iota(jnp.int32, (tp + 2 * POOL_HALO, 1), 0)
        e = e / jnp.maximum(_window_count(tok, w, n), 1.0)
        du_ref[...] = (_window_sum(e, w, True)[POOL_HALO:POOL_HALO + tp] - cur).astype(_BF16)

    return pl.pallas_call(
        body, name=name, grid=(n // tp,),
        in_specs=_halo_specs(tp, pg, 0, 0, n), out_specs=pl.BlockSpec((tp, pg), lambda i: (i, 0)),
        out_shape=jax.ShapeDtypeStruct((n, pg), _BF16),
        compiler_params=pltpu.CompilerParams(dimension_semantics=("parallel",)),
    )(dd, dd, dd)


def _post(out, x, target, gate, g_post, tr, name):
    n, d = out.shape

    def body(o_ref, x_ref, t_ref, gate_ref, g_ref, dxn_ref, do_ref, dgate_ref, dg_ref, loss_ref):
        @pl.when(pl.program_id(0) == 0)
        def _():
            dgate_ref[...] = jnp.zeros_like(dgate_ref)
            dg_ref[...] = jnp.zeros_like(dg_ref)
            loss_ref[...] = jnp.zeros_like(loss_ref)

        ov = o_ref[...]
        r = _rms(ov)
        on = ov * r
        normed = on * g_ref[...]
        err = (x_ref[...] + gate_ref[...] * normed) - t_ref[...]
        loss_ref[...] += jnp.sum(err * err)
        dxn = err / d
        dxn_ref[...] = dxn
        dgate_ref[0:1, :] += jnp.sum(dxn * normed, axis=0, keepdims=True)
        dr = dxn * gate_ref[...]
        dg_ref[0:1, :] += jnp.sum(dr * on, axis=0, keepdims=True)
        dn = dr * g_ref[...]
        do_ref[...] = (r * (dn - on * jnp.mean(dn * on, axis=-1, keepdims=True))).astype(_BF16)

    blk = pl.BlockSpec((tr, d), lambda i: (i, 0))
    vec = pl.BlockSpec((1, d), lambda i: (0, 0))
    acc = pl.BlockSpec((SUBLANES, d), lambda i: (0, 0))
    return pl.pallas_call(
        body, name=name, grid=(n // tr,),
        in_specs=[blk, blk, blk, vec, vec],
        out_specs=[blk, blk, acc, acc, pl.BlockSpec((SUBLANES, LANES), lambda i: (0, 0))],
        out_shape=[jax.ShapeDtypeStruct((n, d), _F32), jax.ShapeDtypeStruct((n, d), _BF16),
                   jax.ShapeDtypeStruct((SUBLANES, d), _F32), jax.ShapeDtypeStruct((SUBLANES, d), _F32),
                   jax.ShapeDtypeStruct((SUBLANES, LANES), _F32)],
        compiler_params=pltpu.CompilerParams(dimension_semantics=("arbitrary",)),
    )(out, x, target, gate, g_post)


def _adam_sharded(own, got, far, w, m, v, name):
    r, c = w.shape
    tr = _tile(r, max(BF16_ROWS, min(256, (1 << 18) // c)), BF16_ROWS)

    def body(own_ref, got_ref, far_ref, w_ref, m_ref, v_ref, g_ref, dl_ref, nm_ref, nv_ref):
        g = own_ref[0] + got_ref[0]
        for k in range(N_CHIPS - 1):
            g = g + far_ref[k].astype(_F32)
        delta, nm, nv = _adamw(w_ref[...], g, m_ref[...], v_ref[...])
        g_ref[...] = g
        dl_ref[...] = delta
        nm_ref[...] = nm
        nv_ref[...] = nv

    blk = pl.BlockSpec((tr, c), lambda i: (i, 0))
    return pl.pallas_call(
        body, name=name, grid=(r // tr,),
        in_specs=[pl.BlockSpec((1, tr, c), lambda i: (0, i, 0)), pl.BlockSpec((1, tr, c), lambda i: (0, i, 0)),
                  pl.BlockSpec((3, tr, c), lambda i: (0, i, 0)), blk, blk, blk],
        out_specs=[blk] * 4, out_shape=[jax.ShapeDtypeStruct((r, c), _F32)] * 4,
        compiler_params=pltpu.CompilerParams(dimension_semantics=("parallel",)),
    )(own, got, far, w, m, v)


def _adam_replicated(parts, extra, through_silu, w, m, v, name):
    def body(p_ref, e_ref, s_ref, w_ref, m_ref, v_ref, g_ref, dl_ref, nm_ref, nv_ref):
        total = p_ref[0] + e_ref[0]
        for dev in range(1, N_DEV):
            total = total + (p_ref[dev] + e_ref[dev])
        g = jnp.where(s_ref[...] > 0.5, total * _silu_grad(w_ref[...]), total)
        delta, nm, nv = _adamw(w_ref[...], g, m_ref[...], v_ref[...])
        g_ref[...] = g
        dl_ref[...] = delta
        nm_ref[...] = nm
        nv_ref[...] = nv

    return pl.pallas_call(
        body, name=name, in_specs=[_VMEM] * 6, out_specs=[_VMEM] * 4,
        out_shape=[jax.ShapeDtypeStruct(w.shape, _F32)] * 4,
    )(parts, extra, through_silu, w, m, v)


def _as_rows(vec):
    size = vec.shape[0]
    padded = -(-size // (SUBLANES * LANES)) * SUBLANES * LANES
    return jnp.pad(vec, (0, padded - size)).reshape(padded // LANES, LANES)


def kernel(x, c, ctx, c_ctx, w_ada, b_ada, norm_pre, norm_post, w_in, q_norm, k_norm, pool_w, pool_scale, w_out, loss_target, m_c_ctx, m_w_ada, m_b_ada, m_norm_pre, m_norm_post, m_w_in, m_q_norm, m_k_norm, m_pool_w, m_pool_scale, m_w_out, v_c_ctx, v_w_ada, v_b_ada, v_norm_pre, v_norm_post, v_w_in, v_q_norm, v_k_norm, v_pool_w, v_pool_scale, v_w_out):
    me = _dev_index(*_position())
    x2, ctx2, target = x[0], ctx[0], loss_target[0]
    n, d = x2.shape
    l = ctx2.shape[0]
    t = l + n
    aw = d // 2
    heads = aw // HEAD_DIM
    kv_heads = heads // GQA_GROUP
    kw = kv_heads * HEAD_DIM
    n_groups = len(POOL_WINDOWS)
    pg = (d - aw) // n_groups
    mix = d
    tr = _tile(l, 128, BF16_ROWS)
    tq = _tile(l, 128, BF16_ROWS)
    tp = _tile(l, 512, POOL_HALO)

    w_in_b = _cast_bf16(w_in[0], "cast_w_in")
    w_out_b = _cast_bf16(w_out[0], "cast_w_out")
    pool_b = _cast_bf16(pool_w[0].reshape(-1, pg), "cast_pool_w").reshape(pool_w.shape[1:])
    w_in_g, w_out_g, pool_g = _gather_weights([w_in_b[None], w_out_b[None], pool_b], [2, 1, 1], "gather_weights")
    w_in_g, w_out_g = w_in_g[0], w_out_g[0]

    c_all = _all_gather_small(_as_rows(c[0]), "gather_c").reshape(N_DEV, -1)[:, :d]
    craw = jnp.concatenate([c_all, jnp.broadcast_to(c_ctx[None], (N_DEV, d))], axis=0)
    ada = _ada_forward(craw, w_ada[0], "ada_forward")
    ada_all = _all_gather_small(ada, "gather_ada")
    mod_all = ada_all.transpose(1, 0, 2).reshape(ada.shape[0], -1) + b_ada[0]
    mod = lax.dynamic_index_in_dim(mod_all, me, 0, keepdims=False)
    mod_c = mod_all[N_DEV]
    shift, scale, gate = mod[:d], mod[d:2 * d], mod[2 * d:]
    zeros6 = jnp.zeros((SUBLANES - 2, d), _F32)
    mods = jnp.stack([jnp.concatenate([mod_c[None, d:2 * d], mod_c[None, :d], zeros6], axis=0),
                      jnp.concatenate([scale[None], shift[None], zeros6], axis=0)])

    h_all = _prenorm(ctx2, x2, norm_pre, mods, tr, "prenorm")
    proj = _matmul(h_all, w_in_g, tm=1088, name="proj")
    tables = _rope_tables(l, n)
    q, k, v = _qkv_post(proj, tables, q_norm, k_norm, heads, kv_heads, tr, "qkv_post")
    attn_o, y, lse = _attention(q, k, v, proj, l, mix, tq, "attention")
    raws, ds = [], []
    for gi in range(n_groups):
        y, raw, dsave = _pool_forward(gi, proj, y, pool_g, pool_scale, l, heads, kv_heads, tp, f"pool_forward_{gi}")
        raws.append(raw)
        ds.append(dsave)
    out = _matmul(y, w_out_g, name="out_proj")
    dxn, dout, dgate8, dgpost8, loss8 = _post(out, x2, target, gate[None], norm_post, tr, "post")

    dy = _matmul(dout, w_out_g, tb=True, name="d_y")
    gw_out = _matmul(y, dout, ta=True, name="grad_w_out")
    dq, dgate_attn, dk, dv = _attention_backward(q, k, v, attn_o, dy, proj, lse, l, tq, "attention_backward")
    dqkv, dgq8, dgk8 = _qkv_post_backward(proj, dq, dk, dv, tables, q_norm, k_norm, l, tr, "qkv_post_backward")
    dus, dgps, gpw, dps8 = [], [], [], []
    for gi in range(n_groups):
        dgp, draw, dd, dps = _pool_backward_gate(
            gi, dy, proj, raws[gi], pool_g, pool_scale, l, heads, kv_heads, tp, f"pool_backward_gate_{gi}")
        dus.append(_pool_backward_window(gi, dd, tp, f"pool_backward_window_{gi}"))
        dgps.append(dgp)
        dps8.append(dps)
        gpw.append(_matmul(ds[gi], draw, ta=True, name=f"grad_pool_w_{gi}"))
    latent_cols = jnp.concatenate([dgate_attn] + dus + dgps, axis=1)
    dproj = jnp.concatenate([dqkv, jnp.pad(latent_cols, ((l, 0), (0, 0)))], axis=1)
    dh = _matmul(dproj, w_in_g, tb=True, tm=1088, name="d_h")
    gw_in = _matmul(h_all, dproj, ta=True, tk=2176, col_slabs=N_DEV, name="grad_w_in")
    grad_x, dmods, dgpre8 = _prenorm_backward(dh, ctx2, x2, dxn, norm_pre, mods, tr, "prenorm_backward")

    grads3 = [gw_in, gw_out[None], jnp.stack(gpw)]
    owns_gots = _exchange_sibling(grads3, [0, 1, 1], "exchange_sibling")
    owns, gots = owns_gots[:3], owns_gots[3:]
    flat = lambda a: a.reshape(a.shape[0], -1, a.shape[-1])
    sums = []
    for own, got, name in zip(owns, gots, ("pre_add_w_in", "pre_add_w_out", "pre_add_pool_w")):
        sums.append(_pre_add(flat(own), flat(got), name).reshape((N_CHIPS - 1,) + own.shape[1:]))
    fars = _exchange_chips(sums, "exchange_chips")
    sharded = []
    for own, got, far, w, m, v_, name in zip(
            owns, gots, fars, (w_in, w_out, pool_w), (m_w_in, m_w_out, m_pool_w), (v_w_in, v_w_out, v_pool_w),
            ("adam_w_in", "adam_w_out", "adam_pool_w")):
        two = lambda a: a.reshape(-1, a.shape[-1])
        res = _adam_sharded(flat(own), flat(got), flat(far), two(w), two(m), two(v_), name)
        sharded.append([r.reshape(w.shape) for r in res])
    (g_w_in, dl_w_in, nm_w_in, nv_w_in), (g_w_out, dl_w_out, nm_w_out, nv_w_out), (g_pw, dl_pw, nm_pw, nv_pw) = sharded

    dmod_lat = jnp.concatenate([dmods[1, 1], dmods[1, 0], dgate8[0]])
    dmod_ctx = jnp.concatenate([dmods[0, 1], dmods[0, 0], jnp.zeros((d,), _F32)])
    small = jnp.concatenate([dmod_lat, dmod_ctx, dgpre8[0], dgpost8[0], dgq8[0], dgk8[0]] + [p[0] for p in dps8]
                            + [loss8[0, :1]])
    gathered = _all_gather_small(_as_rows(small), "gather_small").reshape(N_DEV, -1)
    o = 0
    take = lambda size: (gathered[:, o:o + size], o + size)
    g_mod, o = take(3 * d)
    g_modc, o = take(3 * d)
    g_pre, o = take(d)
    g_post, o = take(d)
    g_q, o = take(HEAD_DIM)
    g_k, o = take(HEAD_DIM)
    g_ps, o = take(n_groups * pg)
    g_loss, o = take(1)
    cols = w_ada.shape[-1]
    mine = lambda a: lax.dynamic_slice_in_dim(a, me * cols, cols, axis=1)
    dmod_rows = jnp.concatenate([mine(g_mod), mine(g_modc)], axis=0)
    g_wada, dl_wada, nm_wada, nv_wada, dcact = _ada_backward(craw, dmod_rows, w_ada[0], m_w_ada[0], v_w_ada[0], "ada_backward")
    dcc = _all_gather_small(_as_rows(dcact[0]), "gather_dcc").reshape(N_DEV, -1)[:, :d]

    sizes = [d, 3 * d, d, d, HEAD_DIM, HEAD_DIM, n_groups * pg]
    pack = lambda parts: jnp.stack([_as_rows(jnp.concatenate([p[dev] for p in parts])) for dev in range(N_DEV)])
    zero = lambda size: jnp.zeros((N_DEV, size), _F32)
    parts = pack([dcc, g_mod, g_pre, g_post, g_q, g_k, g_ps])
    extra = pack([zero(d), g_modc, zero(d), zero(d), zero(HEAD_DIM), zero(HEAD_DIM), zero(n_groups * pg)])
    through_silu = _as_rows(jnp.concatenate([jnp.ones((d,), _F32), jnp.zeros((sum(sizes[1:]),), _F32)]))
    cat = lambda items: _as_rows(jnp.concatenate([a.reshape(-1) for a in items]))
    ws = [c_ctx, b_ada, norm_pre, norm_post, q_norm, k_norm, pool_scale]
    ms = [m_c_ctx, m_b_ada, m_norm_pre, m_norm_post, m_q_norm, m_k_norm, m_pool_scale]
    vs = [v_c_ctx, v_b_ada, v_norm_pre, v_norm_post, v_q_norm, v_k_norm, v_pool_scale]
    rep = _adam_replicated(parts, extra, through_silu, cat(ws), cat(ms), cat(vs), "adam_replicated")

    def split(packed):
        flat_, outs, at = packed.reshape(-1), [], 0
        for w, size in zip(ws, sizes):
            outs.append(flat_[at:at + size].reshape(w.shape))
            at += size
        return outs

    g_rep, dl_rep, nm_rep, nv_rep = [split(r) for r in rep]
    loss_sum = g_loss[0, 0]
    for dev in range(1, N_DEV):
        loss_sum = loss_sum + g_loss[dev, 0]
    loss = (0.5 / d) * loss_sum

    def ordered(rep_list, ada_, w_in_, pw_, w_out_):
        return [rep_list[0], ada_[None], rep_list[1], rep_list[2], rep_list[3], w_in_, rep_list[4], rep_list[5],
                pw_, rep_list[6], w_out_]

    return (loss, grad_x[None],
            *ordered(g_rep, g_wada, g_w_in, g_pw, g_w_out),
            *ordered(dl_rep, dl_wada, dl_w_in, dl_pw, dl_w_out),
            *ordered(nm_rep, nm_wada, nm_w_in, nm_pw, nm_w_out),
            *ordered(nv_rep, nv_wada, nv_w_in, nv_pw, nv_w_out))
```

```python
import functools

import jax
import jax.numpy as jnp
from jax import lax
from jax.experimental import pallas as pl
from jax.experimental.pallas import tpu as pltpu

HEAD_DIM = 128
GQA_GROUP = 4
ATTN_SUB_HEADS = 2
LOG2_E = 1.4426950408889634
GRID_W = 64
ROPE_PAIRS = HEAD_DIM // 4
ROPE_THETA = 10000.0
ATTN_SCALE = HEAD_DIM ** -0.5
EPS = 1e-6
POOL_WINDOWS = (2, 4, 8, 16)
POOL_HALO = 8
N_DEV = 8
N_CHIPS = 4
ADAM_LR = 0.001
ADAM_B1 = 0.9
ADAM_B2 = 0.999
ADAM_EPS = 1e-08
ADAM_WD = 0.01
ADAM_STEP = 10

LANES = 128
SUBLANES = 8
BF16_ROWS = 16

_MESH = pl.DeviceIdType.MESH
_ANY = pl.BlockSpec(memory_space=pl.ANY)
_VMEM = pl.BlockSpec(memory_space=pltpu.VMEM)
_F32 = jnp.float32
_BF16 = jnp.bfloat16


def _tile(dim, pref, align):
    t = min(pref, dim)
    t -= t % align
    while t >= align:
        if dim % t == 0:
            return t
        t -= align
    return dim


def _position():
    return lax.axis_index("x"), lax.axis_index("y"), lax.axis_index("c")


def _flip(v, bit):
    return 1 - v if bit else v


def _dev_index(x, y, c):
    return 4 * x + 2 * y + c


def _silu(g):
    return g * jax.nn.sigmoid(g)


def _silu_grad(g):
    s = jax.nn.sigmoid(g)
    return s * (1.0 + g * (1.0 - s))


def _adamw(w, g, m, v):
    m = ADAM_B1 * m + (1.0 - ADAM_B1) * g
    v = ADAM_B2 * v + (1.0 - ADAM_B2) * (g * g)
    m_hat = m / (1.0 - ADAM_B1 ** ADAM_STEP)
    v_hat = v / (1.0 - ADAM_B2 ** ADAM_STEP)
    delta = -ADAM_LR * (m_hat / (jnp.sqrt(v_hat) + ADAM_EPS) + ADAM_WD * w)
    return delta, m, v


def _all_gather_small(v, name):
    rows, cols = v.shape

    def body(v_ref, out_ref, send_sems, recv_sems):
        x, y, c = _position()
        me = _dev_index(x, y, c)
        out_ref[me] = v_ref[...]
        peers = [(_flip(x, k & 4), _flip(y, k & 2), _flip(c, k & 1)) for k in range(1, N_DEV)]

        def copy(k, block, to):
            return pltpu.make_async_remote_copy(
                src_ref=v_ref, dst_ref=out_ref.at[block], send_sem=send_sems.at[k], recv_sem=recv_sems.at[k],
                device_id=to, device_id_type=_MESH)

        sends = [copy(k, me, p) for k, p in enumerate(peers)]
        for s in sends:
            s.start()
        for k, p in enumerate(peers):
            copy(k, _dev_index(*p), p).wait_recv()
        for s in sends:
            s.wait_send()

    return pl.pallas_call(
        body, name=name,
        out_shape=jax.ShapeDtypeStruct((N_DEV, rows, cols), v.dtype),
        in_specs=[_VMEM], out_specs=_VMEM,
        scratch_shapes=[pltpu.SemaphoreType.DMA((N_DEV - 1,)), pltpu.SemaphoreType.DMA((N_DEV - 1,))],
    )(v)


def _window(ref, axis, size, j):
    start = pl.multiple_of(j * size, size)
    if axis == 0:
        return ref.at[pl.ds(start, size)]
    if axis == 1:
        return ref.at[:, pl.ds(start, size), :]
    return ref.at[:, :, pl.ds(start, size)]


def _gather_weights(shards, axes, name):
    n = len(shards)
    sizes = [s.shape[a] for s, a in zip(shards, axes)]
    out_shapes = [
        jax.ShapeDtypeStruct(tuple(d * N_DEV if i == a else d for i, d in enumerate(s.shape)), s.dtype)
        for s, a in zip(shards, axes)]

    def body(*refs):
        srcs, outs = refs[:n], refs[n:2 * n]
        send_sems, recv_sems, local_sems = refs[2 * n:]
        x, y, c = _position()
        me, sibling = (x, y, c), (x, y, 1 - c)
        chips = [(1 - x, y), (x, 1 - y), (1 - x, 1 - y)]
        firsts, passed, locals_ = [], [], []
        for a in range(n):
            def rows(block, a=a):
                return _window(outs[a], axes[a], sizes[a], _dev_index(*block))

            def copy(k, block, to, src=None, a=a, rows=rows):
                return pltpu.make_async_remote_copy(
                    src_ref=rows(block) if src is None else src, dst_ref=rows(block),
                    send_sem=send_sems.at[7 * a + k], recv_sem=recv_sems.at[7 * a + k],
                    device_id=to, device_id_type=_MESH)

            mine = pltpu.make_async_copy(srcs[a], rows(me), local_sems.at[a])
            mine.start()
            locals_.append(mine)
            first = [copy(0, me, sibling, src=srcs[a])]
            first += [copy(1 + j, me, (*chip, c), src=srcs[a]) for j, chip in enumerate(chips)]
            for cp in first:
                cp.start()
            firsts.append((first, copy))
        for a in range(n):
            first, copy = firsts[a]
            fwd = [copy(4 + j, (*chip, c), sibling) for j, chip in enumerate(chips)]
            for j, chip in enumerate(chips):
                copy(1 + j, (*chip, c), me).wait_recv()
                fwd[j].start()
            passed.append(fwd)
        for a in range(n):
            first, copy = firsts[a]
            copy(0, sibling, me).wait_recv()
            for j, chip in enumerate(chips):
                copy(4 + j, (*chip, 1 - c), me).wait_recv()
            for cp in first + passed[a]:
                cp.wait_send()
            locals_[a].wait()

    return pl.pallas_call(
        body, name=name, out_shape=out_shapes,
        in_specs=[_ANY] * n, out_specs=[_ANY] * n,
        scratch_shapes=[pltpu.SemaphoreType.DMA((7 * n,)), pltpu.SemaphoreType.DMA((7 * n,)),
                        pltpu.SemaphoreType.DMA((n,))],
    )(*shards)


def _chip_order(x, y):
    return [(x, y), (1 - x, y), (x, 1 - y), (1 - x, 1 - y)]


def _exchange_sibling(grads, name):
    n = len(grads)
    out_shapes = [jax.ShapeDtypeStruct((N_CHIPS,) + g.shape[1:], g.dtype) for g in grads]

    def body(*refs):
        srcs, gots = refs[:n], refs[n:2 * n]
        send_sems, recv_sems = refs[2 * n:]
        x, y, c = _position()
        sends = []
        for a in range(n):
            for s, chip in enumerate(_chip_order(x, y)):
                k = N_CHIPS * a + s
                give = pltpu.make_async_remote_copy(
                    src_ref=srcs[a].at[_dev_index(*chip, 1 - c)], dst_ref=gots[a].at[s],
                    send_sem=send_sems.at[k], recv_sem=recv_sems.at[k], device_id=(x, y, 1 - c), device_id_type=_MESH)
                give.start()
                sends.append(give)
        for cp in sends:
            cp.wait_recv()
        for cp in sends:
            cp.wait_send()

    return pl.pallas_call(
        body, name=name, out_shape=out_shapes,
        in_specs=[_ANY] * n, out_specs=[_ANY] * n,
        scratch_shapes=[pltpu.SemaphoreType.DMA((N_CHIPS * n,)), pltpu.SemaphoreType.DMA((N_CHIPS * n,))],
    )(*grads)


def _exchange_chips(sums, name):
    n = len(sums)

    def body(*refs):
        srcs, gots = refs[:n], refs[n:2 * n]
        send_sems, recv_sems = refs[2 * n:]
        x, y, c = _position()
        chips = [(1 - x, y), (x, 1 - y), (1 - x, 1 - y)]
        sends = []
        for a in range(n):
            for k, chip in enumerate(chips):
                cp = pltpu.make_async_remote_copy(
                    src_ref=srcs[a].at[k], dst_ref=gots[a].at[k],
                    send_sem=send_sems.at[3 * a + k], recv_sem=recv_sems.at[3 * a + k],
                    device_id=(*chip, c), device_id_type=_MESH)
                cp.start()
                sends.append(cp)
        for cp in sends:
            cp.wait_recv()
        for cp in sends:
            cp.wait_send()

    return pl.pallas_call(
        body, name=name, out_shape=[jax.ShapeDtypeStruct(s.shape, s.dtype) for s in sums],
        in_specs=[_ANY] * n, out_specs=[_ANY] * n,
        scratch_shapes=[pltpu.SemaphoreType.DMA((3 * n,)), pltpu.SemaphoreType.DMA((3 * n,))],
    )(*sums)


def _matmul(a, b, *, ta=False, tb=False, out_dtype=_F32, tm=1024, tn=1024, tk=1024, col_slabs=None, name):
    kdim, m = a.shape if ta else a.shape[::-1]
    n = b.shape[0] if tb else b.shape[1]
    tm = _tile(m, tm, LANES if ta else BF16_ROWS)
    tn = n // col_slabs if col_slabs else _tile(n, tn, LANES)
    tk = _tile(kdim, tk, BF16_ROWS if ta else LANES)
    nk = kdim // tk
    dims = (((0 if ta else 1,), (1 if tb else 0,)), ((), ()))

    def body(a_ref, b_ref, o_ref, acc_ref):
        k = pl.program_id(2)

        @pl.when(k == 0)
        def _():
            acc_ref[...] = jnp.zeros_like(acc_ref)

        acc_ref[...] += lax.dot_general(a_ref[...], b_ref[...], dims, preferred_element_type=_F32)

        @pl.when(k == nk - 1)
        def _():
            o_ref[...] = acc_ref[...].astype(out_dtype).reshape(o_ref.shape)

    a_spec = pl.BlockSpec((tk, tm), lambda i, j, k: (k, i)) if ta else pl.BlockSpec((tm, tk), lambda i, j, k: (i, k))
    b_spec = pl.BlockSpec((tn, tk), lambda i, j, k: (j, k)) if tb else pl.BlockSpec((tk, tn), lambda i, j, k: (k, j))
    if col_slabs:
        out_spec = pl.BlockSpec((1, tm, tn), lambda i, j, k: (j, i, 0))
        out_shape = jax.ShapeDtypeStruct((col_slabs, m, tn), out_dtype)
    else:
        out_spec = pl.BlockSpec((tm, tn), lambda i, j, k: (i, j))
        out_shape = jax.ShapeDtypeStruct((m, n), out_dtype)
    return pl.pallas_call(
        body, name=name, grid=(m // tm, n // tn, nk),
        in_specs=[a_spec, b_spec], out_specs=out_spec, out_shape=out_shape,
        scratch_shapes=[pltpu.VMEM((tm, tn), _F32)],
        compiler_params=pltpu.CompilerParams(dimension_semantics=("parallel", "parallel", "arbitrary")),
    )(a, b)


def _row_tile(rows, cols):
    return _tile(rows, max(BF16_ROWS, min(512, (1 << 19) // cols)), BF16_ROWS)


def _cast_bf16(a, name):
    r, c = a.shape
    tr = _row_tile(r, c)

    def body(a_ref, o_ref):
        o_ref[...] = a_ref[...].astype(_BF16)

    blk = pl.BlockSpec((tr, c), lambda i: (i, 0))
    return pl.pallas_call(
        body, name=name, grid=(r // tr,), in_specs=[blk], out_specs=blk,
        out_shape=jax.ShapeDtypeStruct((r, c), _BF16),
        compiler_params=pltpu.CompilerParams(dimension_semantics=("parallel",)),
    )(a)


def _pre_add(slab_ids, grad, got, name):
    _, r, c = grad.shape
    tr = _row_tile(r, c)

    def body(ids_ref, a_ref, b_ref, o_ref):
        del ids_ref
        o_ref[...] = (a_ref[...] + b_ref[...]).astype(_BF16)

    return pl.pallas_call(
        body, name=name,
        grid_spec=pltpu.PrefetchScalarGridSpec(
            num_scalar_prefetch=1, grid=(N_CHIPS - 1, r // tr),
            in_specs=[pl.BlockSpec((1, tr, c), lambda s, i, ids: (ids[s + 1], i, 0)),
                      pl.BlockSpec((1, tr, c), lambda s, i, ids: (s + 1, i, 0))],
            out_specs=pl.BlockSpec((1, tr, c), lambda s, i, ids: (s, i, 0))),
        out_shape=jax.ShapeDtypeStruct((N_CHIPS - 1, r, c), _BF16),
        compiler_params=pltpu.CompilerParams(dimension_semantics=("parallel", "parallel")),
    )(slab_ids, grad, got)


def _ada_forward(craw, w_shard, name):
    d, cols = w_shard.shape
    tk = _tile(d, 512, LANES)

    def body(c_ref, w_ref, o_ref):
        @pl.when(pl.program_id(0) == 0)
        def _():
            o_ref[...] = jnp.zeros_like(o_ref)

        o_ref[...] += jnp.dot(_silu(c_ref[...]).astype(_BF16), w_ref[...].astype(_BF16), preferred_element_type=_F32)

    return pl.pallas_call(
        body, name=name, grid=(d // tk,),
        in_specs=[pl.BlockSpec((craw.shape[0], tk), lambda k: (0, k)), pl.BlockSpec((tk, cols), lambda k: (k, 0))],
        out_specs=pl.BlockSpec((craw.shape[0], cols), lambda k: (0, 0)),
        out_shape=jax.ShapeDtypeStruct((craw.shape[0], cols), _F32),
        compiler_params=pltpu.CompilerParams(dimension_semantics=("arbitrary",)),
    )(craw, w_shard)


def _ada_backward(craw, dmod, w, m, v, name):
    d, cols = w.shape
    rows = craw.shape[0]
    tr = _tile(d, 256, LANES)

    def body(c_ref, dm_ref, w_ref, m_ref, v_ref, g_ref, dl_ref, nm_ref, nv_ref, dc_ref):
        act = _silu(c_ref[...]).astype(_BF16)
        dmb = dm_ref[...].astype(_BF16)
        wv = w_ref[...]
        g = lax.dot_general(act, dmb, (((0,), (0,)), ((), ())), preferred_element_type=_F32)
        delta, nm, nv = _adamw(wv, g, m_ref[...], v_ref[...])
        g_ref[...] = g
        dl_ref[...] = delta
        nm_ref[...] = nm
        nv_ref[...] = nv
        dc = lax.dot_general(dmb, wv.astype(_BF16), (((1,), (1,)), ((), ())), preferred_element_type=_F32)
        dc_ref[...] = jnp.broadcast_to(jnp.sum(dc[N_DEV:], axis=0, keepdims=True), dc_ref.shape)

    blk = pl.BlockSpec((tr, cols), lambda i: (i, 0))
    return pl.pallas_call(
        body, name=name, grid=(d // tr,),
        in_specs=[pl.BlockSpec((rows, tr), lambda i: (0, i)), pl.BlockSpec((rows, cols), lambda i: (0, 0)), blk, blk, blk],
        out_specs=[blk, blk, blk, blk, pl.BlockSpec((SUBLANES, tr), lambda i: (0, i))],
        out_shape=[jax.ShapeDtypeStruct((d, cols), _F32)] * 4 + [jax.ShapeDtypeStruct((SUBLANES, d), _F32)],
        compiler_params=pltpu.CompilerParams(dimension_semantics=("parallel",)),
    )(craw, dmod, w, m, v)


def _rms(xf):
    return lax.rsqrt(jnp.mean(xf * xf, axis=-1, keepdims=True) + EPS)


def _prenorm(ctx, x, g_pre, mods, tr, name):
    l, d = ctx.shape
    n = x.shape[0]
    nbc = l // tr

    def body(ctx_ref, x_ref, g_ref, mod_ref, h_ref):
        def emit(src_ref):
            xf = src_ref[...]
            y = (xf * _rms(xf)) * g_ref[...]
            h_ref[...] = (y * (1.0 + mod_ref[0, 0:1, :]) + mod_ref[0, 1:2, :]).astype(_BF16)

        is_ctx = pl.program_id(0) < nbc
        pl.when(is_ctx)(lambda: emit(ctx_ref))
        pl.when(jnp.logical_not(is_ctx))(lambda: emit(x_ref))

    return pl.pallas_call(
        body, name=name, grid=((l + n) // tr,),
        in_specs=[pl.BlockSpec((tr, d), lambda i: (jnp.minimum(i, nbc - 1), 0)),
                  pl.BlockSpec((tr, d), lambda i: (jnp.maximum(i - nbc, 0), 0)),
                  pl.BlockSpec((1, d), lambda i: (0, 0)),
                  pl.BlockSpec((1, SUBLANES, d), lambda i: ((i >= nbc).astype(jnp.int32), 0, 0))],
        out_specs=pl.BlockSpec((tr, d), lambda i: (i, 0)),
        out_shape=jax.ShapeDtypeStruct((l + n, d), _BF16),
        compiler_params=pltpu.CompilerParams(dimension_semantics=("arbitrary",)),
    )(ctx, x, g_pre, mods)


def _prenorm_backward(dh, ctx, x, dxn, g_pre, mods, tr, name):
    l, d = ctx.shape
    n = x.shape[0]
    nbc = l // tr

    def body(dh_ref, ctx_ref, x_ref, dxn_ref, g_ref, mod_ref, gx_ref, dmod_ref, dg_ref):
        i = pl.program_id(0)

        @pl.when(i == 0)
        def _():
            dg_ref[...] = jnp.zeros_like(dg_ref)

        @pl.when(jnp.logical_or(i == 0, i == nbc))
        def _():
            dmod_ref[...] = jnp.zeros_like(dmod_ref)

        def emit(src_ref, latent):
            xf = src_ref[...]
            r = _rms(xf)
            xn = xf * r
            dhv = dh_ref[...]
            one_scale = 1.0 + mod_ref[0, 0:1, :]
            dmod_ref[0, 0:1, :] += jnp.sum(dhv * (xn * g_ref[...]), axis=0, keepdims=True)
            dmod_ref[0, 1:2, :] += jnp.sum(dhv, axis=0, keepdims=True)
            dyg = dhv * one_scale
            dg_ref[0:1, :] += jnp.sum(dyg * xn, axis=0, keepdims=True)
            if latent:
                dn = dyg * g_ref[...]
                gx_ref[...] = dxn_ref[...] + r * (dn - xn * jnp.mean(dn * xn, axis=-1, keepdims=True))

        pl.when(i < nbc)(lambda: emit(ctx_ref, False))
        pl.when(i >= nbc)(lambda: emit(x_ref, True))

    lat = pl.BlockSpec((tr, d), lambda i: (jnp.maximum(i - nbc, 0), 0))
    sel = pl.BlockSpec((1, SUBLANES, d), lambda i: ((i >= nbc).astype(jnp.int32), 0, 0))
    return pl.pallas_call(
        body, name=name, grid=((l + n) // tr,),
        in_specs=[pl.BlockSpec((tr, d), lambda i: (i, 0)),
                  pl.BlockSpec((tr, d), lambda i: (jnp.minimum(i, nbc - 1), 0)),
                  lat, lat, pl.BlockSpec((1, d), lambda i: (0, 0)), sel],
        out_specs=[lat, sel, pl.BlockSpec((SUBLANES, d), lambda i: (0, 0))],
        out_shape=[jax.ShapeDtypeStruct((n, d), _F32), jax.ShapeDtypeStruct((2, SUBLANES, d), _F32),
                   jax.ShapeDtypeStruct((SUBLANES, d), _F32)],
        compiler_params=pltpu.CompilerParams(dimension_semantics=("arbitrary",)),
    )(dh, ctx, x, dxn, g_pre, mods)


def _rope_tables(l, n):
    rows = n // GRID_W
    row = jnp.repeat(jnp.arange(rows, dtype=_F32), GRID_W)
    col = jnp.tile(jnp.arange(GRID_W, dtype=_F32), rows)
    inv = ROPE_THETA ** (-jnp.arange(ROPE_PAIRS, dtype=_F32) / ROPE_PAIRS)
    ang_r, ang_c = row[:, None] * inv, col[:, None] * inv
    cr, sr, cc, sc = jnp.cos(ang_r), jnp.sin(ang_r), jnp.cos(ang_c), jnp.sin(ang_c)
    zero = jnp.zeros_like(sr)
    tc = jnp.concatenate([cr, cr, cc, cc], axis=-1)
    ta = jnp.concatenate([-sr, zero, -sc, zero], axis=-1)
    tb = jnp.concatenate([zero, sr, zero, sc], axis=-1)
    pad = lambda t, fill: jnp.concatenate([jnp.full((l, HEAD_DIM), fill, _F32), t], axis=0)
    return pad(tc, 1.0), pad(ta, 0.0), pad(tb, 0.0)


def _rope(y, tc, ta, tb):
    return y * tc + pltpu.roll(y, HEAD_DIM - ROPE_PAIRS, 1) * ta + pltpu.roll(y, ROPE_PAIRS, 1) * tb


def _rope_transposed(dy, tc, ta, tb):
    return dy * tc + pltpu.roll(dy * ta, ROPE_PAIRS, 1) + pltpu.roll(dy * tb, HEAD_DIM - ROPE_PAIRS, 1)


def _qkv_post(proj, tables, g_q, g_k, heads, kv_heads, tr, name):
    t = proj.shape[0]
    aw, kw = heads * HEAD_DIM, kv_heads * HEAD_DIM
    w3 = aw + 2 * kw

    def body(p_ref, tc_ref, ta_ref, tb_ref, gq_ref, gk_ref, q_ref, k_ref, v_ref):
        tabs = (tc_ref[...], ta_ref[...], tb_ref[...])

        def norm_rope(col, gain):
            xh = p_ref[:, col:col + HEAD_DIM]
            return _rope((xh * _rms(xh)) * gain, *tabs).astype(_BF16)

        for h in range(heads):
            q_ref[h] = norm_rope(h * HEAD_DIM, gq_ref[...])
        for h in range(kv_heads):
            k_ref[h] = norm_rope(aw + h * HEAD_DIM, gk_ref[...])
            v_ref[h] = p_ref[:, aw + kw + h * HEAD_DIM:aw + kw + (h + 1) * HEAD_DIM].astype(_BF16)

    tab = pl.BlockSpec((tr, HEAD_DIM), lambda i: (i, 0))
    gain = pl.BlockSpec((1, HEAD_DIM), lambda i: (0, 0))
    return pl.pallas_call(
        body, name=name, grid=(t // tr,),
        in_specs=[pl.BlockSpec((tr, w3), lambda i: (i, 0)), tab, tab, tab, gain, gain],
        out_specs=[pl.BlockSpec((heads, tr, HEAD_DIM), lambda i: (0, i, 0)),
                   pl.BlockSpec((kv_heads, tr, HEAD_DIM), lambda i: (0, i, 0)),
                   pl.BlockSpec((kv_heads, tr, HEAD_DIM), lambda i: (0, i, 0))],
        out_shape=[jax.ShapeDtypeStruct((heads, t, HEAD_DIM), _BF16),
                   jax.ShapeDtypeStruct((kv_heads, t, HEAD_DIM), _BF16),
                   jax.ShapeDtypeStruct((kv_heads, t, HEAD_DIM), _BF16)],
        compiler_params=pltpu.CompilerParams(dimension_semantics=("parallel",)),
    )(proj, *tables, g_q, g_k)


def _qkv_post_backward(proj, dq, dk, dv, tables, g_q, g_k, l, tr, name):
    t = proj.shape[0]
    heads, kv_heads = dq.shape[0], dk.shape[0]
    aw, kw = heads * HEAD_DIM, kv_heads * HEAD_DIM
    w3 = aw + 2 * kw
    nbc = l // tr

    def body(p_ref, dq_ref, dk_ref, dv_ref, tc_ref, ta_ref, tb_ref, gq_ref, gk_ref, o_ref, dgq_ref, dgk_ref):
        i = pl.program_id(0)

        @pl.when(i == 0)
        def _():
            dgq_ref[...] = jnp.zeros_like(dgq_ref)
            dgk_ref[...] = jnp.zeros_like(dgk_ref)

        tabs = (tc_ref[...], ta_ref[...], tb_ref[...])
        latent = i >= nbc

        def back(col, dout, gain, dg_ref):
            xh = p_ref[:, col:col + HEAD_DIM]
            r = _rms(xh)
            xn = xh * r
            dy = _rope_transposed(dout, *tabs)
            dg_ref[0:1, :] += jnp.sum(dy * xn, axis=0, keepdims=True)
            dn = dy * gain
            o_ref[:, col:col + HEAD_DIM] = (r * (dn - xn * jnp.mean(dn * xn, axis=-1, keepdims=True))).astype(_BF16)

        for h in range(heads):
            back(h * HEAD_DIM, jnp.where(latent, dq_ref[h], 0.0), gq_ref[...], dgq_ref)
        for h in range(kv_heads):
            back(aw + h * HEAD_DIM, dk_ref[h], gk_ref[...], dgk_ref)
            o_ref[:, aw + kw + h * HEAD_DIM:aw + kw + (h + 1) * HEAD_DIM] = dv_ref[h].astype(_BF16)

    tab = pl.BlockSpec((tr, HEAD_DIM), lambda i: (i, 0))
    gain = pl.BlockSpec((1, HEAD_DIM), lambda i: (0, 0))
    acc = pl.BlockSpec((SUBLANES, HEAD_DIM), lambda i: (0, 0))
    return pl.pallas_call(
        body, name=name, grid=(t // tr,),
        in_specs=[pl.BlockSpec((tr, w3), lambda i: (i, 0)),
                  pl.BlockSpec((heads, tr, HEAD_DIM), lambda i: (0, jnp.maximum(i - nbc, 0), 0)),
                  pl.BlockSpec((kv_heads, tr, HEAD_DIM), lambda i: (0, i, 0)),
                  pl.BlockSpec((kv_heads, tr, HEAD_DIM), lambda i: (0, i, 0)),
                  tab, tab, tab, gain, gain],
        out_specs=[pl.BlockSpec((tr, w3), lambda i: (i, 0)), acc, acc],
        out_shape=[jax.ShapeDtypeStruct((t, w3), _BF16), jax.ShapeDtypeStruct((SUBLANES, HEAD_DIM), _F32),
                   jax.ShapeDtypeStruct((SUBLANES, HEAD_DIM), _F32)],
        compiler_params=pltpu.CompilerParams(dimension_semantics=("arbitrary",)),
    )(proj, dq, dk, dv, *tables, g_q, g_k)


def _attention(q, k, v, proj, l, mix, tq, name):
    heads, t, _ = q.shape
    kv_heads = k.shape[0]
    n = t - l
    rows = GQA_GROUP * tq
    gw = GQA_GROUP * HEAD_DIM
    aw = heads * HEAD_DIM
    gate_col = (aw + 2 * kv_heads * HEAD_DIM) // gw
    off = l // tq

    def body(q_ref, k_ref, v_ref, g_ref, o_ref, y_ref, lse_ref):
        lane = lax.broadcasted_iota(jnp.int32, (tq, LANES), 1)
        lse_blk = jnp.zeros((tq, LANES), _F32)
        for first in range(0, GQA_GROUP, ATTN_SUB_HEADS):
            qs = q_ref[first:first + ATTN_SUB_HEADS].reshape(ATTN_SUB_HEADS * tq, HEAD_DIM)
            raw = lax.dot_general(qs, k_ref[0], (((1,), (1,)), ((), ())), preferred_element_type=_F32)
            m = jnp.max(raw, axis=-1, keepdims=True)
            p = jnp.exp2((raw - m) * (ATTN_SCALE * LOG2_E))
            denom = jnp.sum(p, axis=-1, keepdims=True)
            os_ = jnp.dot(p.astype(_BF16), v_ref[0], preferred_element_type=_F32) / denom
            lse_s = m * ATTN_SCALE + jnp.log(denom)
            for j in range(ATTN_SUB_HEADS):
                g = first + j
                og = os_[j * tq:(j + 1) * tq]
                cols = slice(g * HEAD_DIM, (g + 1) * HEAD_DIM)
                o_ref[:, cols] = og
                y_ref[:, cols] = (og * _silu(g_ref[:, cols])).astype(_BF16)
                lse_blk = jnp.where(lane == g, lse_s[j * tq:(j + 1) * tq], lse_blk)
        lse_ref[0] = lse_blk

    return pl.pallas_call(
        body, name=name, grid=(kv_heads, n // tq),
        in_specs=[pl.BlockSpec((GQA_GROUP, tq, HEAD_DIM), lambda h, i: (h, i + off, 0)),
                  pl.BlockSpec((1, t, HEAD_DIM), lambda h, i: (h, 0, 0)),
                  pl.BlockSpec((1, t, HEAD_DIM), lambda h, i: (h, 0, 0)),
                  pl.BlockSpec((tq, gw), lambda h, i: (i + off, gate_col + h))],
        out_specs=[pl.BlockSpec((tq, gw), lambda h, i: (i, h)),
                   pl.BlockSpec((tq, gw), lambda h, i: (i, h)),
                   pl.BlockSpec((1, tq, LANES), lambda h, i: (h, i, 0))],
        out_shape=[jax.ShapeDtypeStruct((n, aw), _F32), jax.ShapeDtypeStruct((n, mix), _BF16),
                   jax.ShapeDtypeStruct((kv_heads, n, LANES), _F32)],
        compiler_params=pltpu.CompilerParams(dimension_semantics=("parallel", "parallel")),
    )(q, k, v, proj)


def _attention_backward(q, k, v, attn_o, dy, proj, lse, l, tq, name):
    heads, t, _ = q.shape
    kv_heads = k.shape[0]
    n = t - l
    rows = GQA_GROUP * tq
    gw = GQA_GROUP * HEAD_DIM
    aw = heads * HEAD_DIM
    gate_col = (aw + 2 * kv_heads * HEAD_DIM) // gw
    off = l // tq
    n_parts = 2 if t % (2 * BF16_ROWS) == 0 else 1
    part = t // n_parts

    def body(q_ref, k_ref, v_ref, o_ref, dy_ref, g_ref, lse_ref, dq_ref, dg_ref, dk_ref, dv_ref):
        @pl.when(pl.program_id(1) == 0)
        def _():
            dk_ref[...] = jnp.zeros_like(dk_ref)
            dv_ref[...] = jnp.zeros_like(dv_ref)

        q4 = q_ref[...].reshape(rows, HEAD_DIM)
        do_parts, delta_parts, lse_parts = [], [], []
        lse_blk = lse_ref[0]
        for g in range(GQA_GROUP):
            cols = slice(g * HEAD_DIM, (g + 1) * HEAD_DIM)
            gate, og, dyg = g_ref[:, cols], o_ref[:, cols], dy_ref[:, cols]
            dog = dyg * _silu(gate)
            dg_ref[:, cols] = (dyg * og * _silu_grad(gate)).astype(_BF16)
            do_parts.append(dog)
            delta_parts.append(jnp.sum(dog * og, axis=-1, keepdims=True))
            lse_parts.append(lse_blk[:, g:g + 1])
        do4 = jnp.concatenate(do_parts, axis=0).astype(_BF16)
        delta4 = jnp.concatenate(delta_parts, axis=0)
        lse4 = jnp.concatenate(lse_parts, axis=0)
        dq4 = jnp.zeros((rows, HEAD_DIM), _F32)
        for part_i in range(n_parts):
            keys = slice(part_i * part, (part_i + 1) * part)
            ks, vs = k_ref[0, keys, :], v_ref[0, keys, :]
            s = lax.dot_general(q4, ks, (((1,), (1,)), ((), ())), preferred_element_type=_F32) * ATTN_SCALE
            p = jnp.exp(s - lse4)
            dp = lax.dot_general(do4, vs, (((1,), (1,)), ((), ())), preferred_element_type=_F32)
            ds = (p * (dp - delta4) * ATTN_SCALE).astype(_BF16)
            dq4 = dq4 + jnp.dot(ds, ks, preferred_element_type=_F32)
            dk_ref[0, keys, :] += lax.dot_general(ds, q4, (((0,), (0,)), ((), ())), preferred_element_type=_F32)
            dv_ref[0, keys, :] += lax.dot_general(
                p.astype(_BF16), do4, (((0,), (0,)), ((), ())), preferred_element_type=_F32)
        dq_ref[...] = dq4.reshape(GQA_GROUP, tq, HEAD_DIM)

    kv_spec = pl.BlockSpec((1, t, HEAD_DIM), lambda h, i: (h, 0, 0))
    tok = pl.BlockSpec((tq, gw), lambda h, i: (i, h))
    return pl.pallas_call(
        body, name=name, grid=(kv_heads, n // tq),
        in_specs=[pl.BlockSpec((GQA_GROUP, tq, HEAD_DIM), lambda h, i: (h, i + off, 0)), kv_spec, kv_spec,
                  tok, tok, pl.BlockSpec((tq, gw), lambda h, i: (i + off, gate_col + h)),
                  pl.BlockSpec((1, tq, LANES), lambda h, i: (h, i, 0))],
        out_specs=[pl.BlockSpec((GQA_GROUP, tq, HEAD_DIM), lambda h, i: (h, i, 0)), tok, kv_spec, kv_spec],
        out_shape=[jax.ShapeDtypeStruct((heads, n, HEAD_DIM), _F32), jax.ShapeDtypeStruct((n, aw), _BF16),
                   jax.ShapeDtypeStruct((kv_heads, t, HEAD_DIM), _F32), jax.ShapeDtypeStruct((kv_heads, t, HEAD_DIM), _F32)],
        compiler_params=pltpu.CompilerParams(dimension_semantics=("parallel", "arbitrary")),
    )(q, k, v, attn_o, dy, proj, lse)


def _halo_specs(tp, width, col, row_off, total_rows):
    per = tp // POOL_HALO
    first = row_off // POOL_HALO
    last = total_rows // POOL_HALO - 1
    return [pl.BlockSpec((tp, width), lambda i: (i + row_off // tp, col)),
            pl.BlockSpec((POOL_HALO, width), lambda i: (jnp.maximum(first + i * per - 1, 0), col)),
            pl.BlockSpec((POOL_HALO, width), lambda i: (jnp.minimum(first + (i + 1) * per, last), col))]


def _with_halo(cur, prev, nxt, t0, n):
    tp = cur.shape[0]
    r8 = lax.broadcasted_iota(jnp.int32, (POOL_HALO, 1), 0)
    prev = jnp.where(t0 - POOL_HALO + r8 >= 0, prev, 0.0)
    nxt = jnp.where(t0 + tp + r8 < n, nxt, 0.0)
    return jnp.concatenate([prev, cur, nxt], axis=0)


def _shift_rows(a, s):
    return pltpu.roll(a, s % a.shape[0], 0)


def _window_sum(e, w, mirrored):
    a = e + _shift_rows(e, -1 if mirrored else 1)
    s = 1
    while 2 * s < w:
        a = _shift_rows(a, s) + _shift_rows(a, -s)
        s *= 2
    return a


def _window_count(t, w, n):
    half = w // 2
    return (jnp.minimum(t + half, n) - jnp.maximum(t - half, 0)).astype(_F32)


def _pool_forward(gi, proj, y, pool_w, pool_scale, l, heads, kv_heads, tp, name):
    t = proj.shape[0]
    n = t - l
    pg = pool_w.shape[-1]
    w = POOL_WINDOWS[gi]
    aw, kw = heads * HEAD_DIM, kv_heads * HEAD_DIM
    u_col = (2 * aw + 2 * kw) // pg + gi
    gate_col = (2 * aw + 2 * kw + len(POOL_WINDOWS) * pg) // pg + gi

    def body(u_ref, up_ref, un_ref, g_ref, w_ref, sc_ref, y_in_ref, y_ref, raw_ref, d_ref):
        del y_in_ref
        t0 = pl.program_id(0) * tp
        cur = u_ref[...]
        win = _window_sum(_with_halo(cur, up_ref[...], un_ref[...], t0, n), w, False)[POOL_HALO:POOL_HALO + tp]
        tok = t0 + lax.broadcasted_iota(jnp.int32, (tp, 1), 0)
        d = (win / _window_count(tok, w, n) - cur).astype(_BF16)
        raw = jnp.dot(d, w_ref[0], preferred_element_type=_F32)
        d_ref[...] = d
        raw_ref[...] = raw
        y_ref[...] = ((raw * sc_ref[...]) * _silu(g_ref[...])).astype(_BF16)

    blk = pl.BlockSpec((tp, pg), lambda i: (i, 0))
    return pl.pallas_call(
        body, name=name, grid=(n // tp,),
        in_specs=_halo_specs(tp, pg, u_col, l, t) + [
            pl.BlockSpec((tp, pg), lambda i: (i + l // tp, gate_col)),
            pl.BlockSpec((1, pg, pg), lambda i: (gi, 0, 0)),
            pl.BlockSpec((1, pg), lambda i: (0, gi)), _ANY],
        out_specs=[pl.BlockSpec((tp, pg), lambda i: (i, aw // pg + gi)), blk, blk],
        out_shape=[jax.ShapeDtypeStruct(y.shape, y.dtype), jax.ShapeDtypeStruct((n, pg), _F32),
                   jax.ShapeDtypeStruct((n, pg), _BF16)],
        input_output_aliases={6: 0},
        compiler_params=pltpu.CompilerParams(dimension_semantics=("arbitrary",)),
    )(proj, proj, proj, proj, pool_w, pool_scale, y)


def _pool_backward_gate(gi, dy, proj, raw, pool_w, pool_scale, l, heads, kv_heads, tp, name):
    n, pg = raw.shape
    aw, kw = heads * HEAD_DIM, kv_heads * HEAD_DIM
    gate_col = (2 * aw + 2 * kw + len(POOL_WINDOWS) * pg) // pg + gi

    def body(dy_ref, g_ref, raw_ref, w_ref, sc_ref, dg_ref, dr_ref, dd_ref, ds_ref):
        @pl.when(pl.program_id(0) == 0)
        def _():
            ds_ref[...] = jnp.zeros_like(ds_ref)

        gate, rawv, dyv, scale = g_ref[...], raw_ref[...], dy_ref[...], sc_ref[...]
        dpool = dyv * _silu(gate)
        dg_ref[...] = (dyv * (rawv * scale) * _silu_grad(gate)).astype(_BF16)
        ds_ref[0:1, :] += jnp.sum(dpool * rawv, axis=0, keepdims=True)
        draw = (dpool * scale).astype(_BF16)
        dr_ref[...] = draw
        dd_ref[...] = lax.dot_general(draw, w_ref[0], (((1,), (1,)), ((), ())), preferred_element_type=_F32)

    blk = pl.BlockSpec((tp, pg), lambda i: (i, 0))
    return pl.pallas_call(
        body, name=name, grid=(n // tp,),
        in_specs=[pl.BlockSpec((tp, pg), lambda i: (i, aw // pg + gi)),
                  pl.BlockSpec((tp, pg), lambda i: (i + l // tp, gate_col)), blk,
                  pl.BlockSpec((1, pg, pg), lambda i: (gi, 0, 0)), pl.BlockSpec((1, pg), lambda i: (0, gi))],
        out_specs=[blk, blk, blk, pl.BlockSpec((SUBLANES, pg), lambda i: (0, 0))],
        out_shape=[jax.ShapeDtypeStruct((n, pg), _BF16), jax.ShapeDtypeStruct((n, pg), _BF16),
                   jax.ShapeDtypeStruct((n, pg), _F32), jax.ShapeDtypeStruct((SUBLANES, pg), _F32)],
        compiler_params=pltpu.CompilerParams(dimension_semantics=("arbitrary",)),
    )(dy, proj, raw, pool_w, pool_scale)


def _pool_backward_window(gi, dd, tp, name):
    n, pg = dd.shape
    w = POOL_WINDOWS[gi]

    def body(c_ref, p_ref, n_ref, du_ref):
        t0 = pl.program_id(0) * tp
        cur = c_ref[...]
        e = _with_halo(cur, p_ref[...], n_ref[...], t0, n)
        tok = t0 - POOL_HALO + lax.broadcasted_iota(jnp.int32, (tp + 2 * POOL_HALO, 1), 0)
        e = e / jnp.maximum(_window_count(tok, w, n), 1.0)
        du_ref[...] = (_window_sum(e, w, True)[POOL_HALO:POOL_HALO + tp] - cur).astype(_BF16)

    return pl.pallas_call(
        body, name=name, grid=(n // tp,),
        in_specs=_halo_specs(tp, pg, 0, 0, n), out_specs=pl.BlockSpec((tp, pg), lambda i: (i, 0)),
        out_shape=jax.ShapeDtypeStruct((n, pg), _BF16),
        compiler_params=pltpu.CompilerParams(dimension_semantics=("parallel",)),
    )(dd, dd, dd)


def _post(out, x, target, gate, g_post, tr, name):
    n, d = out.shape

    def body(o_ref, x_ref, t_ref, gate_ref, g_ref, dxn_ref, do_ref, dgate_ref, dg_ref, loss_ref):
        @pl.when(pl.program_id(0) == 0)
        def _():
            dgate_ref[...] = jnp.zeros_like(dgate_ref)
            dg_ref[...] = jnp.zeros_like(dg_ref)
            loss_ref[...] = jnp.zeros_like(loss_ref)

        ov = o_ref[...]
        r = _rms(ov)
        on = ov * r
        normed = on * g_ref[...]
        err = (x_ref[...] + gate_ref[...] * normed) - t_ref[...]
        loss_ref[...] += jnp.sum(err * err)
        dxn = err / d
        dxn_ref[...] = dxn
        dgate_ref[0:1, :] += jnp.sum(dxn * normed, axis=0, keepdims=True)
        dr = dxn * gate_ref[...]
        dg_ref[0:1, :] += jnp.sum(dr * on, axis=0, keepdims=True)
        dn = dr * g_ref[...]
        do_ref[...] = (r * (dn - on * jnp.mean(dn * on, axis=-1, keepdims=True))).astype(_BF16)

    blk = pl.BlockSpec((tr, d), lambda i: (i, 0))
    vec = pl.BlockSpec((1, d), lambda i: (0, 0))
    acc = pl.BlockSpec((SUBLANES, d), lambda i: (0, 0))
    return pl.pallas_call(
        body, name=name, grid=(n // tr,),
        in_specs=[blk, blk, blk, vec, vec],
        out_specs=[blk, blk, acc, acc, pl.BlockSpec((SUBLANES, LANES), lambda i: (0, 0))],
        out_shape=[jax.ShapeDtypeStruct((n, d), _F32), jax.ShapeDtypeStruct((n, d), _BF16),
                   jax.ShapeDtypeStruct((SUBLANES, d), _F32), jax.ShapeDtypeStruct((SUBLANES, d), _F32),
                   jax.ShapeDtypeStruct((SUBLANES, LANES), _F32)],
        compiler_params=pltpu.CompilerParams(dimension_semantics=("arbitrary",)),
    )(out, x, target, gate, g_post)


def _adam_sharded(slab_ids, grad, got, far, w, m, v, name):
    r, c = w.shape
    tr = _tile(r, max(BF16_ROWS, min(256, (1 << 18) // c)), BF16_ROWS)

    def body(ids_ref, own_ref, got_ref, far_ref, w_ref, m_ref, v_ref, g_ref, dl_ref, nm_ref, nv_ref):
        del ids_ref
        g = own_ref[0] + got_ref[0]
        for k in range(N_CHIPS - 1):
            g = g + far_ref[k].astype(_F32)
        delta, nm, nv = _adamw(w_ref[...], g, m_ref[...], v_ref[...])
        g_ref[...] = g
        dl_ref[...] = delta
        nm_ref[...] = nm
        nv_ref[...] = nv

    blk = pl.BlockSpec((tr, c), lambda i, ids: (i, 0))
    return pl.pallas_call(
        body, name=name,
        grid_spec=pltpu.PrefetchScalarGridSpec(
            num_scalar_prefetch=1, grid=(r // tr,),
            in_specs=[pl.BlockSpec((1, tr, c), lambda i, ids: (ids[0], i, 0)),
                      pl.BlockSpec((1, tr, c), lambda i, ids: (0, i, 0)),
                      pl.BlockSpec((N_CHIPS - 1, tr, c), lambda i, ids: (0, i, 0)), blk, blk, blk],
            out_specs=[blk] * 4),
        out_shape=[jax.ShapeDtypeStruct((r, c), _F32)] * 4,
        compiler_params=pltpu.CompilerParams(dimension_semantics=("parallel",)),
    )(slab_ids, grad, got, far, w, m, v)


def _adam_replicated(parts, extra, through_silu, w, m, v, name):
    def body(p_ref, e_ref, s_ref, w_ref, m_ref, v_ref, g_ref, dl_ref, nm_ref, nv_ref):
        total = p_ref[0] + e_ref[0]
        for dev in range(1, N_DEV):
            total = total + (p_ref[dev] + e_ref[dev])
        g = jnp.where(s_ref[...] > 0.5, total * _silu_grad(w_ref[...]), total)
        delta, nm, nv = _adamw(w_ref[...], g, m_ref[...], v_ref[...])
        g_ref[...] = g
        dl_ref[...] = delta
        nm_ref[...] = nm
        nv_ref[...] = nv

    return pl.pallas_call(
        body, name=name, in_specs=[_VMEM] * 6, out_specs=[_VMEM] * 4,
        out_shape=[jax.ShapeDtypeStruct(w.shape, _F32)] * 4,
    )(parts, extra, through_silu, w, m, v)


def _as_rows(vec):
    size = vec.shape[0]
    padded = -(-size // (SUBLANES * LANES)) * SUBLANES * LANES
    return jnp.pad(vec, (0, padded - size)).reshape(padded // LANES, LANES)


def kernel(x, c, ctx, c_ctx, w_ada, b_ada, norm_pre, norm_post, w_in, q_norm, k_norm, pool_w, pool_scale, w_out, loss_target, m_c_ctx, m_w_ada, m_b_ada, m_norm_pre, m_norm_post, m_w_in, m_q_norm, m_k_norm, m_pool_w, m_pool_scale, m_w_out, v_c_ctx, v_w_ada, v_b_ada, v_norm_pre, v_norm_post, v_w_in, v_q_norm, v_k_norm, v_pool_w, v_pool_scale, v_w_out):
    me = _dev_index(*_position())
    x2, ctx2, target = x[0], ctx[0], loss_target[0]
    n, d = x2.shape
    l = ctx2.shape[0]
    t = l + n
    aw = d // 2
    heads = aw // HEAD_DIM
    kv_heads = heads // GQA_GROUP
    kw = kv_heads * HEAD_DIM
    n_groups = len(POOL_WINDOWS)
    pg = (d - aw) // n_groups
    mix = d
    tr = _tile(l, 128, BF16_ROWS)
    tq = _tile(l, 128, BF16_ROWS)
    tp = _tile(l, 512, POOL_HALO)

    w_in_b = _cast_bf16(w_in[0], "cast_w_in")
    w_out_b = _cast_bf16(w_out[0], "cast_w_out")
    pool_b = _cast_bf16(pool_w[0].reshape(-1, pg), "cast_pool_w").reshape(pool_w.shape[1:])
    w_in_g, w_out_g, pool_g = _gather_weights([w_in_b[None], w_out_b[None], pool_b], [2, 1, 1], "gather_weights")
    w_in_g, w_out_g = w_in_g[0], w_out_g[0]

    c_all = _all_gather_small(_as_rows(c[0]), "gather_c").reshape(N_DEV, -1)[:, :d]
    craw = jnp.concatenate([c_all, jnp.broadcast_to(c_ctx[None], (N_DEV, d))], axis=0)
    ada = _ada_forward(craw, w_ada[0], "ada_forward")
    ada_all = _all_gather_small(ada, "gather_ada")
    mod_all = ada_all.transpose(1, 0, 2).reshape(ada.shape[0], -1) + b_ada[0]
    mod = lax.dynamic_index_in_dim(mod_all, me, 0, keepdims=False)
    mod_c = mod_all[N_DEV]
    shift, scale, gate = mod[:d], mod[d:2 * d], mod[2 * d:]
    zeros6 = jnp.zeros((SUBLANES - 2, d), _F32)
    mods = jnp.stack([jnp.concatenate([mod_c[None, d:2 * d], mod_c[None, :d], zeros6], axis=0),
                      jnp.concatenate([scale[None], shift[None], zeros6], axis=0)])

    h_all = _prenorm(ctx2, x2, norm_pre, mods, tr, "prenorm")
    proj = _matmul(h_all, w_in_g, tm=1088, name="proj")
    tables = _rope_tables(l, n)
    q, k, v = _qkv_post(proj, tables, q_norm, k_norm, heads, kv_heads, tr, "qkv_post")
    attn_o, y, lse = _attention(q, k, v, proj, l, mix, tq, "attention")
    raws, ds = [], []
    for gi in range(n_groups):
        y, raw, dsave = _pool_forward(gi, proj, y, pool_g, pool_scale, l, heads, kv_heads, tp, f"pool_forward_{gi}")
        raws.append(raw)
        ds.append(dsave)
    out = _matmul(y, w_out_g, name="out_proj")
    dxn, dout, dgate8, dgpost8, loss8 = _post(out, x2, target, gate[None], norm_post, tr, "post")

    dy = _matmul(dout, w_out_g, tb=True, name="d_y")
    gw_out = _matmul(y, dout, ta=True, name="grad_w_out")
    dq, dgate_attn, dk, dv = _attention_backward(q, k, v, attn_o, dy, proj, lse, l, tq, "attention_backward")
    dqkv, dgq8, dgk8 = _qkv_post_backward(proj, dq, dk, dv, tables, q_norm, k_norm, l, tr, "qkv_post_backward")
    dus, dgps, gpw, dps8 = [], [], [], []
    for gi in range(n_groups):
        dgp, draw, dd, dps = _pool_backward_gate(
            gi, dy, proj, raws[gi], pool_g, pool_scale, l, heads, kv_heads, tp, f"pool_backward_gate_{gi}")
        dus.append(_pool_backward_window(gi, dd, tp, f"pool_backward_window_{gi}"))
        dgps.append(dgp)
        dps8.append(dps)
        gpw.append(_matmul(ds[gi], draw, ta=True, name=f"grad_pool_w_{gi}"))
    latent_cols = jnp.concatenate([dgate_attn] + dus + dgps, axis=1)
    dproj = jnp.concatenate([dqkv, jnp.pad(latent_cols, ((l, 0), (0, 0)))], axis=1)
    dh = _matmul(dproj, w_in_g, tb=True, tm=1088, name="d_h")
    gw_in = _matmul(h_all, dproj, ta=True, tk=2176, col_slabs=N_DEV, name="grad_w_in")
    grad_x, dmods, dgpre8 = _prenorm_backward(dh, ctx2, x2, dxn, norm_pre, mods, tr, "prenorm_backward")

    pr = pool_w.shape[2]
    gpw8 = jnp.stack(gpw).reshape(n_groups, N_DEV, pr, pg).transpose(1, 0, 2, 3).reshape(N_DEV, n_groups * pr, pg)
    grads3 = [gw_in, gw_out.reshape(N_DEV, mix // N_DEV, d), gpw8]
    xi, yi, ci = _position()
    slab_ids = jnp.stack([_dev_index(*chip, ci) for chip in _chip_order(xi, yi)]).astype(jnp.int32)
    gots = _exchange_sibling(grads3, "exchange_sibling")
    sums = [_pre_add(slab_ids, g, got, name)
            for g, got, name in zip(grads3, gots, ("pre_add_w_in", "pre_add_w_out", "pre_add_pool_w"))]
    fars = _exchange_chips(sums, "exchange_chips")
    sharded = []
    for g, got, far, w, m, v_, name in zip(
            grads3, gots, fars, (w_in, w_out, pool_w), (m_w_in, m_w_out, m_pool_w), (v_w_in, v_w_out, v_pool_w),
            ("adam_w_in", "adam_w_out", "adam_pool_w")):
        two = lambda a: a.reshape(-1, a.shape[-1])
        res = _adam_sharded(slab_ids, g, got, far, two(w), two(m), two(v_), name)
        sharded.append([r.reshape(w.shape) for r in res])
    (g_w_in, dl_w_in, nm_w_in, nv_w_in), (g_w_out, dl_w_out, nm_w_out, nv_w_out), (g_pw, dl_pw, nm_pw, nv_pw) = sharded

    dmod_lat = jnp.concatenate([dmods[1, 1], dmods[1, 0], dgate8[0]])
    dmod_ctx = jnp.concatenate([dmods[0, 1], dmods[0, 0], jnp.zeros((d,), _F32)])
    small = jnp.concatenate([dmod_lat, dmod_ctx, dgpre8[0], dgpost8[0], dgq8[0], dgk8[0]] + [p[0] for p in dps8]
                            + [loss8[0, :1]])
    gathered = _all_gather_small(_as_rows(small), "gather_small").reshape(N_DEV, -1)
    o = 0
    take = lambda size: (gathered[:, o:o + size], o + size)
    g_mod, o = take(3 * d)
    g_modc, o = take(3 * d)
    g_pre, o = take(d)
    g_post, o = take(d)
    g_q, o = take(HEAD_DIM)
    g_k, o = take(HEAD_DIM)
    g_ps, o = take(n_groups * pg)
    g_loss, o = take(1)
    cols = w_ada.shape[-1]
    mine = lambda a: lax.dynamic_slice_in_dim(a, me * cols, cols, axis=1)
    dmod_rows = jnp.concatenate([mine(g_mod), mine(g_modc)], axis=0)
    g_wada, dl_wada, nm_wada, nv_wada, dcact = _ada_backward(craw, dmod_rows, w_ada[0], m_w_ada[0], v_w_ada[0], "ada_backward")
    dcc = _all_gather_small(_as_rows(dcact[0]), "gather_dcc").reshape(N_DEV, -1)[:, :d]

    sizes = [d, 3 * d, d, d, HEAD_DIM, HEAD_DIM, n_groups * pg]
    pack = lambda parts: jnp.stack([_as_rows(jnp.concatenate([p[dev] for p in parts])) for dev in range(N_DEV)])
    zero = lambda size: jnp.zeros((N_DEV, size), _F32)
    parts = pack([dcc, g_mod, g_pre, g_post, g_q, g_k, g_ps])
    extra = pack([zero(d), g_modc, zero(d), zero(d), zero(HEAD_DIM), zero(HEAD_DIM), zero(n_groups * pg)])
    through_silu = _as_rows(jnp.concatenate([jnp.ones((d,), _F32), jnp.zeros((sum(sizes[1:]),), _F32)]))
    cat = lambda items: _as_rows(jnp.concatenate([a.reshape(-1) for a in items]))
    ws = [c_ctx, b_ada, norm_pre, norm_post, q_norm, k_norm, pool_scale]
    ms = [m_c_ctx, m_b_ada, m_norm_pre, m_norm_post, m_q_norm, m_k_norm, m_pool_scale]
    vs = [v_c_ctx, v_b_ada, v_norm_pre, v_norm_post, v_q_norm, v_k_norm, v_pool_scale]
    rep = _adam_replicated(parts, extra, through_silu, cat(ws), cat(ms), cat(vs), "adam_replicated")

    def split(packed):
        flat_, outs, at = packed.reshape(-1), [], 0
        for w, size in zip(ws, sizes):
            outs.append(flat_[at:at + size].reshape(w.shape))
            at += size
        return outs

    g_rep, dl_rep, nm_rep, nv_rep = [split(r) for r in rep]
    loss_sum = g_loss[0, 0]
    for dev in range(1, N_DEV):
        loss_sum = loss_sum + g_loss[dev, 0]
    loss = (0.5 / d) * loss_sum

    def ordered(rep_list, ada_, w_in_, pw_, w_out_):
        return [rep_list[0], ada_[None], rep_list[1], rep_list[2], rep_list[3], w_in_, rep_list[4], rep_list[5],
                pw_, rep_list[6], w_out_]

    return (loss, grad_x[None],
            *ordered(g_rep, g_wada, g_w_in, g_pw, g_w_out),
            *ordered(dl_rep, dl_wada, dl_w_in, dl_pw, dl_w_out),
            *ordered(nm_rep, nm_wada, nm_w_in, nm_pw, nm_w_out),
            *ordered(nv_rep, nv_wada, nv_w_in, nv_pw, nv_w_out))
```

```python
import functools

import jax
import jax.numpy as jnp
from jax import lax
from jax.experimental import pallas as pl
from jax.experimental.pallas import tpu as pltpu

HEAD_DIM = 128
GQA_GROUP = 4
ATTN_SUB_HEADS = 2
LOG2_E = 1.4426950408889634
GRID_W = 64
ROPE_PAIRS = HEAD_DIM // 4
ROPE_THETA = 10000.0
ATTN_SCALE = HEAD_DIM ** -0.5
EPS = 1e-6
POOL_WINDOWS = (2, 4, 8, 16)
POOL_HALO = 8
N_DEV = 8
N_CHIPS = 4
ADAM_LR = 0.001
ADAM_B1 = 0.9
ADAM_B2 = 0.999
ADAM_EPS = 1e-08
ADAM_WD = 0.01
ADAM_STEP = 10

LANES = 128
SUBLANES = 8
BF16_ROWS = 16

_MESH = pl.DeviceIdType.MESH
_ANY = pl.BlockSpec(memory_space=pl.ANY)
_VMEM = pl.BlockSpec(memory_space=pltpu.VMEM)
_HBM = pl.BlockSpec(memory_space=pltpu.HBM)
_SEM = pl.BlockSpec(memory_space=pltpu.SEMAPHORE)
_EFFECT = pltpu.SideEffectType.DATAFLOW_SIDE_EFFECTING
_F32 = jnp.float32
_BF16 = jnp.bfloat16


def _tile(dim, pref, align):
    t = min(pref, dim)
    t -= t % align
    while t >= align:
        if dim % t == 0:
            return t
        t -= align
    return dim


def _position():
    return lax.axis_index("x"), lax.axis_index("y"), lax.axis_index("c")


def _flip(v, bit):
    return 1 - v if bit else v


def _dev_index(x, y, c):
    return 4 * x + 2 * y + c


def _silu(g):
    return g * jax.nn.sigmoid(g)


def _silu_grad(g):
    s = jax.nn.sigmoid(g)
    return s * (1.0 + g * (1.0 - s))


def _adamw(w, g, m, v):
    m = ADAM_B1 * m + (1.0 - ADAM_B1) * g
    v = ADAM_B2 * v + (1.0 - ADAM_B2) * (g * g)
    m_hat = m / (1.0 - ADAM_B1 ** ADAM_STEP)
    v_hat = v / (1.0 - ADAM_B2 ** ADAM_STEP)
    delta = -ADAM_LR * (m_hat / (jnp.sqrt(v_hat) + ADAM_EPS) + ADAM_WD * w)
    return delta, m, v


def _all_gather_small(v, name):
    rows, cols = v.shape

    def body(v_ref, out_ref, send_sems, recv_sems):
        x, y, c = _position()
        me = _dev_index(x, y, c)
        out_ref[me] = v_ref[...]
        peers = [(_flip(x, k & 4), _flip(y, k & 2), _flip(c, k & 1)) for k in range(1, N_DEV)]

        def copy(k, block, to):
            return pltpu.make_async_remote_copy(
                src_ref=v_ref, dst_ref=out_ref.at[block], send_sem=send_sems.at[k], recv_sem=recv_sems.at[k],
                device_id=to, device_id_type=_MESH)

        sends = [copy(k, me, p) for k, p in enumerate(peers)]
        for s in sends:
            s.start()
        for k, p in enumerate(peers):
            copy(k, _dev_index(*p), p).wait_recv()
        for s in sends:
            s.wait_send()

    return pl.pallas_call(
        body, name=name,
        out_shape=jax.ShapeDtypeStruct((N_DEV, rows, cols), v.dtype),
        in_specs=[_VMEM], out_specs=_VMEM,
        scratch_shapes=[pltpu.SemaphoreType.DMA((N_DEV - 1,)), pltpu.SemaphoreType.DMA((N_DEV - 1,))],
    )(v)


def _window(ref, axis, size, j):
    start = pl.multiple_of(j * size, size)
    if axis == 0:
        return ref.at[pl.ds(start, size)]
    if axis == 1:
        return ref.at[:, pl.ds(start, size), :]
    return ref.at[:, :, pl.ds(start, size)]


def _gather_weights(shards, axes, name):
    n = len(shards)
    sizes = [s.shape[a] for s, a in zip(shards, axes)]
    out_shapes = [
        jax.ShapeDtypeStruct(tuple(d * N_DEV if i == a else d for i, d in enumerate(s.shape)), s.dtype)
        for s, a in zip(shards, axes)]

    def body(*refs):
        srcs, outs = refs[:n], refs[n:2 * n]
        send_sems, recv_sems, local_sems = refs[2 * n:]
        x, y, c = _position()
        me, sibling = (x, y, c), (x, y, 1 - c)
        chips = [(1 - x, y), (x, 1 - y), (1 - x, 1 - y)]
        firsts, passed, locals_ = [], [], []
        for a in range(n):
            def rows(block, a=a):
                return _window(outs[a], axes[a], sizes[a], _dev_index(*block))

            def copy(k, block, to, src=None, a=a, rows=rows):
                return pltpu.make_async_remote_copy(
                    src_ref=rows(block) if src is None else src, dst_ref=rows(block),
                    send_sem=send_sems.at[7 * a + k], recv_sem=recv_sems.at[7 * a + k],
                    device_id=to, device_id_type=_MESH)

            mine = pltpu.make_async_copy(srcs[a], rows(me), local_sems.at[a])
            mine.start()
            locals_.append(mine)
            first = [copy(0, me, sibling, src=srcs[a])]
            first += [copy(1 + j, me, (*chip, c), src=srcs[a]) for j, chip in enumerate(chips)]
            for cp in first:
                cp.start()
            firsts.append((first, copy))
        for a in range(n):
            first, copy = firsts[a]
            fwd = [copy(4 + j, (*chip, c), sibling) for j, chip in enumerate(chips)]
            for j, chip in enumerate(chips):
                copy(1 + j, (*chip, c), me).wait_recv()
                fwd[j].start()
            passed.append(fwd)
        for a in range(n):
            first, copy = firsts[a]
            copy(0, sibling, me).wait_recv()
            for j, chip in enumerate(chips):
                copy(4 + j, (*chip, 1 - c), me).wait_recv()
            for cp in first + passed[a]:
                cp.wait_send()
            locals_[a].wait()

    return pl.pallas_call(
        body, name=name, out_shape=out_shapes,
        in_specs=[_ANY] * n, out_specs=[_ANY] * n,
        scratch_shapes=[pltpu.SemaphoreType.DMA((7 * n,)), pltpu.SemaphoreType.DMA((7 * n,)),
                        pltpu.SemaphoreType.DMA((n,))],
    )(*shards)


def _chip_order(x, y):
    return [(x, y), (1 - x, y), (x, 1 - y), (1 - x, 1 - y)]


def _exchange_sibling(grads, name):
    n = len(grads)
    out_shapes = [jax.ShapeDtypeStruct((N_CHIPS,) + g.shape[1:], g.dtype) for g in grads]

    def body(*refs):
        srcs, gots = refs[:n], refs[n:2 * n]
        send_sems, recv_sems = refs[2 * n:]
        x, y, c = _position()
        sends = []
        for a in range(n):
            for s, chip in enumerate(_chip_order(x, y)):
                k = N_CHIPS * a + s
                give = pltpu.make_async_remote_copy(
                    src_ref=srcs[a].at[_dev_index(*chip, 1 - c)], dst_ref=gots[a].at[s],
                    send_sem=send_sems.at[k], recv_sem=recv_sems.at[k], device_id=(x, y, 1 - c), device_id_type=_MESH)
                give.start()
                sends.append(give)
        for cp in sends:
            cp.wait_recv()
        for cp in sends:
            cp.wait_send()

    return pl.pallas_call(
        body, name=name, out_shape=out_shapes,
        in_specs=[_ANY] * n, out_specs=[_ANY] * n,
        scratch_shapes=[pltpu.SemaphoreType.DMA((N_CHIPS * n,)), pltpu.SemaphoreType.DMA((N_CHIPS * n,))],
    )(*grads)


def _chip_copies(srcs, lands, send_sems, recv_sems):
    x, y, c = _position()
    return [pltpu.make_async_remote_copy(
        src_ref=srcs[a].at[k], dst_ref=lands[a].at[k],
        send_sem=send_sems.at[(N_CHIPS - 1) * a + k], recv_sem=recv_sems.at[(N_CHIPS - 1) * a + k],
        device_id=(*chip, c), device_id_type=_MESH)
        for a in range(len(srcs)) for k, chip in enumerate(_chip_order(x, y)[1:])]


def _exchange_chips_start(sums, name):
    n = len(sums)
    n_copies = (N_CHIPS - 1) * n

    def body(*refs):
        srcs, lands = refs[:n], refs[n:2 * n]
        send_sems, recv_sems, token = refs[2 * n], refs[2 * n + 1], refs[-1]
        for cp in _chip_copies(srcs, lands, send_sems, recv_sems):
            cp.start()
        token[...] = jnp.zeros_like(token)

    hbm = [pltpu.HBM(s.shape, s.dtype) for s in sums]
    outs = pl.pallas_call(
        body, name=name,
        out_shape=(pltpu.SemaphoreType.DMA((n_copies,)), pltpu.SemaphoreType.DMA((n_copies,)), *hbm, *hbm,
                   jax.ShapeDtypeStruct((SUBLANES, LANES), _F32)),
        in_specs=[_HBM] * (2 * n), out_specs=(_SEM, _SEM, *[_HBM] * (2 * n), _VMEM),
        input_output_aliases={i: 2 + i for i in range(2 * n)},
        compiler_params=pltpu.CompilerParams(has_side_effects=_EFFECT),
    )(*[pltpu.with_memory_space_constraint(s, pltpu.HBM) for s in sums],
      *[pltpu.with_memory_space_constraint(lax.empty(s.shape, s.dtype), pltpu.HBM) for s in sums])
    return outs[0], outs[1], list(outs[2:2 + n]), list(outs[2 + n:2 + 2 * n]), outs[-1]


def _exchange_chips_wait(send_sems, recv_sems, srcs, lands, after, name):
    n = len(srcs)

    def body(*refs):
        for cp in _chip_copies(refs[:n], refs[n:2 * n], refs[2 * n], refs[2 * n + 1]):
            cp.wait_send()
            cp.wait_recv()

    hbm = [pltpu.HBM(s.shape, s.dtype) for s in srcs]
    outs = pl.pallas_call(
        body, name=name, out_shape=(*hbm, *hbm),
        in_specs=[_HBM] * (2 * n) + [_SEM, _SEM, _ANY], out_specs=[_HBM] * (2 * n),
        input_output_aliases={i: i for i in range(2 * n)},
        compiler_params=pltpu.CompilerParams(has_side_effects=_EFFECT),
    )(*srcs, *lands, send_sems, recv_sems, after)
    return list(outs[n:])


def _matmul(a, b, *, ta=False, tb=False, out_dtype=_F32, tm=1024, tn=1024, tk=1024, col_slabs=None, after=None, name):
    kdim, m = a.shape if ta else a.shape[::-1]
    n = b.shape[0] if tb else b.shape[1]
    tm = _tile(m, tm, LANES if ta else BF16_ROWS)
    tn = n // col_slabs if col_slabs else _tile(n, tn, LANES)
    tk = _tile(kdim, tk, BF16_ROWS if ta else LANES)
    nk = kdim // tk
    dims = (((0 if ta else 1,), (1 if tb else 0,)), ((), ()))

    def body(a_ref, b_ref, *rest):
        o_ref, acc_ref = rest[-2:]
        k = pl.program_id(2)

        @pl.when(k == 0)
        def _():
            acc_ref[...] = jnp.zeros_like(acc_ref)

        acc_ref[...] += lax.dot_general(a_ref[...], b_ref[...], dims, preferred_element_type=_F32)

        @pl.when(k == nk - 1)
        def _():
            o_ref[...] = acc_ref[...].astype(out_dtype).reshape(o_ref.shape)

    a_spec = pl.BlockSpec((tk, tm), lambda i, j, k: (k, i)) if ta else pl.BlockSpec((tm, tk), lambda i, j, k: (i, k))
    b_spec = pl.BlockSpec((tn, tk), lambda i, j, k: (j, k)) if tb else pl.BlockSpec((tk, tn), lambda i, j, k: (k, j))
    if col_slabs:
        out_spec = pl.BlockSpec((1, tm, tn), lambda i, j, k: (j, i, 0))
        out_shape = jax.ShapeDtypeStruct((col_slabs, m, tn), out_dtype)
    else:
        out_spec = pl.BlockSpec((tm, tn), lambda i, j, k: (i, j))
        out_shape = jax.ShapeDtypeStruct((m, n), out_dtype)
    extra = [] if after is None else [after]
    return pl.pallas_call(
        body, name=name, grid=(m // tm, n // tn, nk),
        in_specs=[a_spec, b_spec] + [pl.BlockSpec(t.shape, lambda i, j, k: (0, 0)) for t in extra],
        out_specs=out_spec, out_shape=out_shape,
        scratch_shapes=[pltpu.VMEM((tm, tn), _F32)],
        compiler_params=pltpu.CompilerParams(dimension_semantics=("parallel", "parallel", "arbitrary")),
    )(a, b, *extra)


def _row_tile(rows, cols):
    return _tile(rows, max(BF16_ROWS, min(512, (1 << 19) // cols)), BF16_ROWS)


def _cast_bf16(a, name):
    r, c = a.shape
    tr = _row_tile(r, c)

    def body(a_ref, o_ref):
        o_ref[...] = a_ref[...].astype(_BF16)

    blk = pl.BlockSpec((tr, c), lambda i: (i, 0))
    return pl.pallas_call(
        body, name=name, grid=(r // tr,), in_specs=[blk], out_specs=blk,
        out_shape=jax.ShapeDtypeStruct((r, c), _BF16),
        compiler_params=pltpu.CompilerParams(dimension_semantics=("parallel",)),
    )(a)


def _pre_add(slab_ids, grad, got, name):
    _, r, c = grad.shape
    tr = _row_tile(r, c)

    def body(ids_ref, a_ref, b_ref, o_ref):
        del ids_ref
        o_ref[...] = (a_ref[...] + b_ref[...]).astype(_BF16)

    return pl.pallas_call(
        body, name=name,
        grid_spec=pltpu.PrefetchScalarGridSpec(
            num_scalar_prefetch=1, grid=(N_CHIPS - 1, r // tr),
            in_specs=[pl.BlockSpec((1, tr, c), lambda s, i, ids: (ids[s + 1], i, 0)),
                      pl.BlockSpec((1, tr, c), lambda s, i, ids: (s + 1, i, 0))],
            out_specs=pl.BlockSpec((1, tr, c), lambda s, i, ids: (s, i, 0))),
        out_shape=jax.ShapeDtypeStruct((N_CHIPS - 1, r, c), _BF16),
        compiler_params=pltpu.CompilerParams(dimension_semantics=("parallel", "parallel")),
    )(slab_ids, grad, got)


def _ada_forward(craw, w_shard, name):
    d, cols = w_shard.shape
    tk = _tile(d, 512, LANES)

    def body(c_ref, w_ref, o_ref):
        @pl.when(pl.program_id(0) == 0)
        def _():
            o_ref[...] = jnp.zeros_like(o_ref)

        o_ref[...] += jnp.dot(_silu(c_ref[...]).astype(_BF16), w_ref[...].astype(_BF16), preferred_element_type=_F32)

    return pl.pallas_call(
        body, name=name, grid=(d // tk,),
        in_specs=[pl.BlockSpec((craw.shape[0], tk), lambda k: (0, k)), pl.BlockSpec((tk, cols), lambda k: (k, 0))],
        out_specs=pl.BlockSpec((craw.shape[0], cols), lambda k: (0, 0)),
        out_shape=jax.ShapeDtypeStruct((craw.shape[0], cols), _F32),
        compiler_params=pltpu.CompilerParams(dimension_semantics=("arbitrary",)),
    )(craw, w_shard)


def _ada_backward(craw, dmod, w, m, v, name):
    d, cols = w.shape
    rows = craw.shape[0]
    tr = _tile(d, 256, LANES)

    def body(c_ref, dm_ref, w_ref, m_ref, v_ref, g_ref, dl_ref, nm_ref, nv_ref, dc_ref):
        act = _silu(c_ref[...]).astype(_BF16)
        dmb = dm_ref[...].astype(_BF16)
        wv = w_ref[...]
        g = lax.dot_general(act, dmb, (((0,), (0,)), ((), ())), preferred_element_type=_F32)
        delta, nm, nv = _adamw(wv, g, m_ref[...], v_ref[...])
        g_ref[...] = g
        dl_ref[...] = delta
        nm_ref[...] = nm
        nv_ref[...] = nv
        dc = lax.dot_general(dmb, wv.astype(_BF16), (((1,), (1,)), ((), ())), preferred_element_type=_F32)
        dc_ref[...] = jnp.broadcast_to(jnp.sum(dc[N_DEV:], axis=0, keepdims=True), dc_ref.shape)

    blk = pl.BlockSpec((tr, cols), lambda i: (i, 0))
    return pl.pallas_call(
        body, name=name, grid=(d // tr,),
        in_specs=[pl.BlockSpec((rows, tr), lambda i: (0, i)), pl.BlockSpec((rows, cols), lambda i: (0, 0)), blk, blk, blk],
        out_specs=[blk, blk, blk, blk, pl.BlockSpec((SUBLANES, tr), lambda i: (0, i))],
        out_shape=[jax.ShapeDtypeStruct((d, cols), _F32)] * 4 + [jax.ShapeDtypeStruct((SUBLANES, d), _F32)],
        compiler_params=pltpu.CompilerParams(dimension_semantics=("parallel",)),
    )(craw, dmod, w, m, v)


def _rms(xf):
    return lax.rsqrt(jnp.mean(xf * xf, axis=-1, keepdims=True) + EPS)


def _prenorm(ctx, x, g_pre, mods, tr, name):
    l, d = ctx.shape
    n = x.shape[0]
    nbc = l // tr

    def body(ctx_ref, x_ref, g_ref, mod_ref, h_ref):
        def emit(src_ref):
            xf = src_ref[...]
            y = (xf * _rms(xf)) * g_ref[...]
            h_ref[...] = (y * (1.0 + mod_ref[0, 0:1, :]) + mod_ref[0, 1:2, :]).astype(_BF16)

        is_ctx = pl.program_id(0) < nbc
        pl.when(is_ctx)(lambda: emit(ctx_ref))
        pl.when(jnp.logical_not(is_ctx))(lambda: emit(x_ref))

    return pl.pallas_call(
        body, name=name, grid=((l + n) // tr,),
        in_specs=[pl.BlockSpec((tr, d), lambda i: (jnp.minimum(i, nbc - 1), 0)),
                  pl.BlockSpec((tr, d), lambda i: (jnp.maximum(i - nbc, 0), 0)),
                  pl.BlockSpec((1, d), lambda i: (0, 0)),
                  pl.BlockSpec((1, SUBLANES, d), lambda i: ((i >= nbc).astype(jnp.int32), 0, 0))],
        out_specs=pl.BlockSpec((tr, d), lambda i: (i, 0)),
        out_shape=jax.ShapeDtypeStruct((l + n, d), _BF16),
        compiler_params=pltpu.CompilerParams(dimension_semantics=("arbitrary",)),
    )(ctx, x, g_pre, mods)


def _prenorm_backward(dh, ctx, x, dxn, g_pre, mods, tr, name):
    l, d = ctx.shape
    n = x.shape[0]
    nbc = l // tr

    def body(dh_ref, ctx_ref, x_ref, dxn_ref, g_ref, mod_ref, gx_ref, dmod_ref, dg_ref):
        i = pl.program_id(0)

        @pl.when(i == 0)
        def _():
            dg_ref[...] = jnp.zeros_like(dg_ref)

        @pl.when(jnp.logical_or(i == 0, i == nbc))
        def _():
            dmod_ref[...] = jnp.zeros_like(dmod_ref)

        def emit(src_ref, latent):
            xf = src_ref[...]
            r = _rms(xf)
            xn = xf * r
            dhv = dh_ref[...]
            one_scale = 1.0 + mod_ref[0, 0:1, :]
            dmod_ref[0, 0:1, :] += jnp.sum(dhv * (xn * g_ref[...]), axis=0, keepdims=True)
            dmod_ref[0, 1:2, :] += jnp.sum(dhv, axis=0, keepdims=True)
            dyg = dhv * one_scale
            dg_ref[0:1, :] += jnp.sum(dyg * xn, axis=0, keepdims=True)
            if latent:
                dn = dyg * g_ref[...]
                gx_ref[...] = dxn_ref[...] + r * (dn - xn * jnp.mean(dn * xn, axis=-1, keepdims=True))

        pl.when(i < nbc)(lambda: emit(ctx_ref, False))
        pl.when(i >= nbc)(lambda: emit(x_ref, True))

    lat = pl.BlockSpec((tr, d), lambda i: (jnp.maximum(i - nbc, 0), 0))
    sel = pl.BlockSpec((1, SUBLANES, d), lambda i: ((i >= nbc).astype(jnp.int32), 0, 0))
    return pl.pallas_call(
        body, name=name, grid=((l + n) // tr,),
        in_specs=[pl.BlockSpec((tr, d), lambda i: (i, 0)),
                  pl.BlockSpec((tr, d), lambda i: (jnp.minimum(i, nbc - 1), 0)),
                  lat, lat, pl.BlockSpec((1, d), lambda i: (0, 0)), sel],
        out_specs=[lat, sel, pl.BlockSpec((SUBLANES, d), lambda i: (0, 0))],
        out_shape=[jax.ShapeDtypeStruct((n, d), _F32), jax.ShapeDtypeStruct((2, SUBLANES, d), _F32),
                   jax.ShapeDtypeStruct((SUBLANES, d), _F32)],
        compiler_params=pltpu.CompilerParams(dimension_semantics=("arbitrary",)),
    )(dh, ctx, x, dxn, g_pre, mods)


def _rope_tables(l, n):
    rows = n // GRID_W
    row = jnp.repeat(jnp.arange(rows, dtype=_F32), GRID_W)
    col = jnp.tile(jnp.arange(GRID_W, dtype=_F32), rows)
    inv = ROPE_THETA ** (-jnp.arange(ROPE_PAIRS, dtype=_F32) / ROPE_PAIRS)
    ang_r, ang_c = row[:, None] * inv, col[:, None] * inv
    cr, sr, cc, sc = jnp.cos(ang_r), jnp.sin(ang_r), jnp.cos(ang_c), jnp.sin(ang_c)
    zero = jnp.zeros_like(sr)
    tc = jnp.concatenate([cr, cr, cc, cc], axis=-1)
    ta = jnp.concatenate([-sr, zero, -sc, zero], axis=-1)
    tb = jnp.concatenate([zero, sr, zero, sc], axis=-1)
    pad = lambda t, fill: jnp.concatenate([jnp.full((l, HEAD_DIM), fill, _F32), t], axis=0)
    return pad(tc, 1.0), pad(ta, 0.0), pad(tb, 0.0)


def _rope(y, tc, ta, tb):
    return y * tc + pltpu.roll(y, HEAD_DIM - ROPE_PAIRS, 1) * ta + pltpu.roll(y, ROPE_PAIRS, 1) * tb


def _rope_transposed(dy, tc, ta, tb):
    return dy * tc + pltpu.roll(dy * ta, ROPE_PAIRS, 1) + pltpu.roll(dy * tb, HEAD_DIM - ROPE_PAIRS, 1)


def _qkv_post(proj, tables, g_q, g_k, heads, kv_heads, tr, name):
    t = proj.shape[0]
    aw, kw = heads * HEAD_DIM, kv_heads * HEAD_DIM
    w3 = aw + 2 * kw

    def body(p_ref, tc_ref, ta_ref, tb_ref, gq_ref, gk_ref, q_ref, k_ref, v_ref):
        tabs = (tc_ref[...], ta_ref[...], tb_ref[...])

        def norm_rope(col, gain):
            xh = p_ref[:, col:col + HEAD_DIM]
            return _rope((xh * _rms(xh)) * gain, *tabs).astype(_BF16)

        for h in range(heads):
            q_ref[h] = norm_rope(h * HEAD_DIM, gq_ref[...])
        for h in range(kv_heads):
            k_ref[h] = norm_rope(aw + h * HEAD_DIM, gk_ref[...])
            v_ref[h] = p_ref[:, aw + kw + h * HEAD_DIM:aw + kw + (h + 1) * HEAD_DIM].astype(_BF16)

    tab = pl.BlockSpec((tr, HEAD_DIM), lambda i: (i, 0))
    gain = pl.BlockSpec((1, HEAD_DIM), lambda i: (0, 0))
    return pl.pallas_call(
        body, name=name, grid=(t // tr,),
        in_specs=[pl.BlockSpec((tr, w3), lambda i: (i, 0)), tab, tab, tab, gain, gain],
        out_specs=[pl.BlockSpec((heads, tr, HEAD_DIM), lambda i: (0, i, 0)),
                   pl.BlockSpec((kv_heads, tr, HEAD_DIM), lambda i: (0, i, 0)),
                   pl.BlockSpec((kv_heads, tr, HEAD_DIM), lambda i: (0, i, 0))],
        out_shape=[jax.ShapeDtypeStruct((heads, t, HEAD_DIM), _BF16),
                   jax.ShapeDtypeStruct((kv_heads, t, HEAD_DIM), _BF16),
                   jax.ShapeDtypeStruct((kv_heads, t, HEAD_DIM), _BF16)],
        compiler_params=pltpu.CompilerParams(dimension_semantics=("parallel",)),
    )(proj, *tables, g_q, g_k)


def _qkv_post_backward(proj, dq, dk, dv, tables, g_q, g_k, l, tr, name):
    t = proj.shape[0]
    heads, kv_heads = dq.shape[0], dk.shape[0]
    aw, kw = heads * HEAD_DIM, kv_heads * HEAD_DIM
    w3 = aw + 2 * kw
    nbc = l // tr

    def body(p_ref, dq_ref, dk_ref, dv_ref, tc_ref, ta_ref, tb_ref, gq_ref, gk_ref, o_ref, dgq_ref, dgk_ref):
        i = pl.program_id(0)

        @pl.when(i == 0)
        def _():
            dgq_ref[...] = jnp.zeros_like(dgq_ref)
            dgk_ref[...] = jnp.zeros_like(dgk_ref)

        tabs = (tc_ref[...], ta_ref[...], tb_ref[...])
        latent = i >= nbc

        def back(col, dout, gain, dg_ref):
            xh = p_ref[:, col:col + HEAD_DIM]
            r = _rms(xh)
            xn = xh * r
            dy = _rope_transposed(dout, *tabs)
            dg_ref[0:1, :] += jnp.sum(dy * xn, axis=0, keepdims=True)
            dn = dy * gain
            o_ref[:, col:col + HEAD_DIM] = (r * (dn - xn * jnp.mean(dn * xn, axis=-1, keepdims=True))).astype(_BF16)

        for h in range(heads):
            back(h * HEAD_DIM, jnp.where(latent, dq_ref[h], 0.0), gq_ref[...], dgq_ref)
        for h in range(kv_heads):
            back(aw + h * HEAD_DIM, dk_ref[h], gk_ref[...], dgk_ref)
            o_ref[:, aw + kw + h * HEAD_DIM:aw + kw + (h + 1) * HEAD_DIM] = dv_ref[h].astype(_BF16)

    tab = pl.BlockSpec((tr, HEAD_DIM), lambda i: (i, 0))
    gain = pl.BlockSpec((1, HEAD_DIM), lambda i: (0, 0))
    acc = pl.BlockSpec((SUBLANES, HEAD_DIM), lambda i: (0, 0))
    return pl.pallas_call(
        body, name=name, grid=(t // tr,),
        in_specs=[pl.BlockSpec((tr, w3), lambda i: (i, 0)),
                  pl.BlockSpec((heads, tr, HEAD_DIM), lambda i: (0, jnp.maximum(i - nbc, 0), 0)),
                  pl.BlockSpec((kv_heads, tr, HEAD_DIM), lambda i: (0, i, 0)),
                  pl.BlockSpec((kv_heads, tr, HEAD_DIM), lambda i: (0, i, 0)),
                  tab, tab, tab, gain, gain],
        out_specs=[pl.BlockSpec((tr, w3), lambda i: (i, 0)), acc, acc],
        out_shape=[jax.ShapeDtypeStruct((t, w3), _BF16), jax.ShapeDtypeStruct((SUBLANES, HEAD_DIM), _F32),
                   jax.ShapeDtypeStruct((SUBLANES, HEAD_DIM), _F32)],
        compiler_params=pltpu.CompilerParams(dimension_semantics=("arbitrary",)),
    )(proj, dq, dk, dv, *tables, g_q, g_k)


def _attention(q, k, v, proj, l, mix, tq, name):
    heads, t, _ = q.shape
    kv_heads = k.shape[0]
    n = t - l
    rows = GQA_GROUP * tq
    gw = GQA_GROUP * HEAD_DIM
    aw = heads * HEAD_DIM
    gate_col = (aw + 2 * kv_heads * HEAD_DIM) // gw
    off = l // tq

    def body(q_ref, k_ref, v_ref, g_ref, o_ref, y_ref, lse_ref):
        lane = lax.broadcasted_iota(jnp.int32, (tq, LANES), 1)
        lse_blk = jnp.zeros((tq, LANES), _F32)
        for first in range(0, GQA_GROUP, ATTN_SUB_HEADS):
            qs = q_ref[first:first + ATTN_SUB_HEADS].reshape(ATTN_SUB_HEADS * tq, HEAD_DIM)
            raw = lax.dot_general(qs, k_ref[0], (((1,), (1,)), ((), ())), preferred_element_type=_F32)
            m = jnp.max(raw, axis=-1, keepdims=True)
            p = jnp.exp2((raw - m) * (ATTN_SCALE * LOG2_E))
            denom = jnp.sum(p, axis=-1, keepdims=True)
            os_ = jnp.dot(p.astype(_BF16), v_ref[0], preferred_element_type=_F32) / denom
            lse_s = m * ATTN_SCALE + jnp.log(denom)
            for j in range(ATTN_SUB_HEADS):
                g = first + j
                og = os_[j * tq:(j + 1) * tq]
                cols = slice(g * HEAD_DIM, (g + 1) * HEAD_DIM)
                o_ref[:, cols] = og
                y_ref[:, cols] = (og * _silu(g_ref[:, cols])).astype(_BF16)
                lse_blk = jnp.where(lane == g, lse_s[j * tq:(j + 1) * tq], lse_blk)
        lse_ref[0] = lse_blk

    return pl.pallas_call(
        body, name=name, grid=(kv_heads, n // tq),
        in_specs=[pl.BlockSpec((GQA_GROUP, tq, HEAD_DIM), lambda h, i: (h, i + off, 0)),
                  pl.BlockSpec((1, t, HEAD_DIM), lambda h, i: (h, 0, 0)),
                  pl.BlockSpec((1, t, HEAD_DIM), lambda h, i: (h, 0, 0)),
                  pl.BlockSpec((tq, gw), lambda h, i: (i + off, gate_col + h))],
        out_specs=[pl.BlockSpec((tq, gw), lambda h, i: (i, h)),
                   pl.BlockSpec((tq, gw), lambda h, i: (i, h)),
                   pl.BlockSpec((1, tq, LANES), lambda h, i: (h, i, 0))],
        out_shape=[jax.ShapeDtypeStruct((n, aw), _F32), jax.ShapeDtypeStruct((n, mix), _BF16),
                   jax.ShapeDtypeStruct((kv_heads, n, LANES), _F32)],
        compiler_params=pltpu.CompilerParams(dimension_semantics=("parallel", "parallel")),
    )(q, k, v, proj)


def _attention_backward(q, k, v, attn_o, dy, proj, lse, after, l, tq, name):
    heads, t, _ = q.shape
    kv_heads = k.shape[0]
    n = t - l
    rows = GQA_GROUP * tq
    gw = GQA_GROUP * HEAD_DIM
    aw = heads * HEAD_DIM
    gate_col = (aw + 2 * kv_heads * HEAD_DIM) // gw
    off = l // tq
    n_parts = 2 if t % (2 * BF16_ROWS) == 0 else 1
    part = t // n_parts

    def body(q_ref, k_ref, v_ref, o_ref, dy_ref, g_ref, lse_ref, after_ref, dq_ref, dg_ref, dk_ref, dv_ref):
        del after_ref

        @pl.when(pl.program_id(1) == 0)
        def _():
            dk_ref[...] = jnp.zeros_like(dk_ref)
            dv_ref[...] = jnp.zeros_like(dv_ref)

        q4 = q_ref[...].reshape(rows, HEAD_DIM)
        do_parts, delta_parts, lse_parts = [], [], []
        lse_blk = lse_ref[0]
        for g in range(GQA_GROUP):
            cols = slice(g * HEAD_DIM, (g + 1) * HEAD_DIM)
            gate, og, dyg = g_ref[:, cols], o_ref[:, cols], dy_ref[:, cols]
            dog = dyg * _silu(gate)
            dg_ref[:, cols] = (dyg * og * _silu_grad(gate)).astype(_BF16)
            do_parts.append(dog)
            delta_parts.append(jnp.sum(dog * og, axis=-1, keepdims=True))
            lse_parts.append(lse_blk[:, g:g + 1])
        do4 = jnp.concatenate(do_parts, axis=0).astype(_BF16)
        delta4 = jnp.concatenate(delta_parts, axis=0)
        lse4 = jnp.concatenate(lse_parts, axis=0)
        dq4 = jnp.zeros((rows, HEAD_DIM), _F32)
        for part_i in range(n_parts):
            keys = slice(part_i * part, (part_i + 1) * part)
            ks, vs = k_ref[0, keys, :], v_ref[0, keys, :]
            s = lax.dot_general(q4, ks, (((1,), (1,)), ((), ())), preferred_element_type=_F32) * ATTN_SCALE
            p = jnp.exp(s - lse4)
            dp = lax.dot_general(do4, vs, (((1,), (1,)), ((), ())), preferred_element_type=_F32)
            ds = (p * (dp - delta4) * ATTN_SCALE).astype(_BF16)
            dq4 = dq4 + jnp.dot(ds, ks, preferred_element_type=_F32)
            dk_ref[0, keys, :] += lax.dot_general(ds, q4, (((0,), (0,)), ((), ())), preferred_element_type=_F32)
            dv_ref[0, keys, :] += lax.dot_general(
                p.astype(_BF16), do4, (((0,), (0,)), ((), ())), preferred_element_type=_F32)
        dq_ref[...] = dq4.reshape(GQA_GROUP, tq, HEAD_DIM)

    kv_spec = pl.BlockSpec((1, t, HEAD_DIM), lambda h, i: (h, 0, 0))
    tok = pl.BlockSpec((tq, gw), lambda h, i: (i, h))
    return pl.pallas_call(
        body, name=name, grid=(kv_heads, n // tq),
        in_specs=[pl.BlockSpec((GQA_GROUP, tq, HEAD_DIM), lambda h, i: (h, i + off, 0)), kv_spec, kv_spec,
                  tok, tok, pl.BlockSpec((tq, gw), lambda h, i: (i + off, gate_col + h)),
                  pl.BlockSpec((1, tq, LANES), lambda h, i: (h, i, 0)),
                  pl.BlockSpec(after.shape, lambda h, i: (0, 0))],
        out_specs=[pl.BlockSpec((GQA_GROUP, tq, HEAD_DIM), lambda h, i: (h, i, 0)), tok, kv_spec, kv_spec],
        out_shape=[jax.ShapeDtypeStruct((heads, n, HEAD_DIM), _F32), jax.ShapeDtypeStruct((n, aw), _BF16),
                   jax.ShapeDtypeStruct((kv_heads, t, HEAD_DIM), _F32), jax.ShapeDtypeStruct((kv_heads, t, HEAD_DIM), _F32)],
        compiler_params=pltpu.CompilerParams(dimension_semantics=("parallel", "arbitrary")),
    )(q, k, v, attn_o, dy, proj, lse, after)


def _halo_specs(tp, width, col, row_off, total_rows):
    per = tp // POOL_HALO
    first = row_off // POOL_HALO
    last = total_rows // POOL_HALO - 1
    return [pl.BlockSpec((tp, width), lambda i: (i + row_off // tp, col)),
            pl.BlockSpec((POOL_HALO, width), lambda i: (jnp.maximum(first + i * per - 1, 0), col)),
            pl.BlockSpec((POOL_HALO, width), lambda i: (jnp.minimum(first + (i + 1) * per, last), col))]


def _with_halo(cur, prev, nxt, t0, n):
    tp = cur.shape[0]
    r8 = lax.broadcasted_iota(jnp.int32, (POOL_HALO, 1), 0)
    prev = jnp.where(t0 - POOL_HALO + r8 >= 0, prev, 0.0)
    nxt = jnp.where(t0 + tp + r8 < n, nxt, 0.0)
    return jnp.concatenate([prev, cur, nxt], axis=0)


def _shift_rows(a, s):
    return pltpu.roll(a, s % a.shape[0], 0)


def _window_sum(e, w, mirrored):
    a = e + _shift_rows(e, -1 if mirrored else 1)
    s = 1
    while 2 * s < w:
        a = _shift_rows(a, s) + _shift_rows(a, -s)
        s *= 2
    return a


def _window_count(t, w, n):
    half = w // 2
    return (jnp.minimum(t + half, n) - jnp.maximum(t - half, 0)).astype(_F32)


def _pool_forward(gi, proj, y, pool_w, pool_scale, l, heads, kv_heads, tp, name):
    t = proj.shape[0]
    n = t - l
    pg = pool_w.shape[-1]
    w = POOL_WINDOWS[gi]
    aw, kw = heads * HEAD_DIM, kv_heads * HEAD_DIM
    u_col = (2 * aw + 2 * kw) // pg + gi
    gate_col = (2 * aw + 2 * kw + len(POOL_WINDOWS) * pg) // pg + gi

    def body(u_ref, up_ref, un_ref, g_ref, w_ref, sc_ref, y_in_ref, y_ref, raw_ref, d_ref):
        del y_in_ref
        t0 = pl.program_id(0) * tp
        cur = u_ref[...]
        win = _window_sum(_with_halo(cur, up_ref[...], un_ref[...], t0, n), w, False)[POOL_HALO:POOL_HALO + tp]
        tok = t0 + lax.broadcasted_iota(jnp.int32, (tp, 1), 0)
        d = (win / _window_count(tok, w, n) - cur).astype(_BF16)
        raw = jnp.dot(d, w_ref[0], preferred_element_type=_F32)
        d_ref[...] = d
        raw_ref[...] = raw
        y_ref[...] = ((raw * sc_ref[...]) * _silu(g_ref[...])).astype(_BF16)

    blk = pl.BlockSpec((tp, pg), lambda i: (i, 0))
    return pl.pallas_call(
        body, name=name, grid=(n // tp,),
        in_specs=_halo_specs(tp, pg, u_col, l, t) + [
            pl.BlockSpec((tp, pg), lambda i: (i + l // tp, gate_col)),
            pl.BlockSpec((1, pg, pg), lambda i: (gi, 0, 0)),
            pl.BlockSpec((1, pg), lambda i: (0, gi)), _ANY],
        out_specs=[pl.BlockSpec((tp, pg), lambda i: (i, aw // pg + gi)), blk, blk],
        out_shape=[jax.ShapeDtypeStruct(y.shape, y.dtype), jax.ShapeDtypeStruct((n, pg), _F32),
                   jax.ShapeDtypeStruct((n, pg), _BF16)],
        input_output_aliases={6: 0},
        compiler_params=pltpu.CompilerParams(dimension_semantics=("arbitrary",)),
    )(proj, proj, proj, proj, pool_w, pool_scale, y)


def _pool_backward_gate(gi, dy, proj, raw, pool_w, pool_scale, l, heads, kv_heads, tp, name):
    n, pg = raw.shape
    aw, kw = heads * HEAD_DIM, kv_heads * HEAD_DIM
    gate_col = (2 * aw + 2 * kw + len(POOL_WINDOWS) * pg) // pg + gi

    def body(dy_ref, g_ref, raw_ref, w_ref, sc_ref, dg_ref, dr_ref, dd_ref, ds_ref):
        @pl.when(pl.program_id(0) == 0)
        def _():
            ds_ref[...] = jnp.zeros_like(ds_ref)

        gate, rawv, dyv, scale = g_ref[...], raw_ref[...], dy_ref[...], sc_ref[...]
        dpool = dyv * _silu(gate)
        dg_ref[...] = (dyv * (rawv * scale) * _silu_grad(gate)).astype(_BF16)
        ds_ref[0:1, :] += jnp.sum(dpool * rawv, axis=0, keepdims=True)
        draw = (dpool * scale).astype(_BF16)
        dr_ref[...] = draw
        dd_ref[...] = lax.dot_general(draw, w_ref[0], (((1,), (1,)), ((), ())), preferred_element_type=_F32)

    blk = pl.BlockSpec((tp, pg), lambda i: (i, 0))
    return pl.pallas_call(
        body, name=name, grid=(n // tp,),
        in_specs=[pl.BlockSpec((tp, pg), lambda i: (i, aw // pg + gi)),
                  pl.BlockSpec((tp, pg), lambda i: (i + l // tp, gate_col)), blk,
                  pl.BlockSpec((1, pg, pg), lambda i: (gi, 0, 0)), pl.BlockSpec((1, pg), lambda i: (0, gi))],
        out_specs=[blk, blk, blk, pl.BlockSpec((SUBLANES, pg), lambda i: (0, 0))],
        out_shape=[jax.ShapeDtypeStruct((n, pg), _BF16), jax.ShapeDtypeStruct((n, pg), _BF16),
                   jax.ShapeDtypeStruct((n, pg), _F32), jax.ShapeDtypeStruct((SUBLANES, pg), _F32)],
        compiler_params=pltpu.CompilerParams(dimension_semantics=("arbitrary",)),
    )(dy, proj, raw, pool_w, pool_scale)


def _pool_backward_window(gi, dd, tp, name):
    n, pg = dd.shape
    w = POOL_WINDOWS[gi]

    def body(c_ref, p_ref, n_ref, du_ref):
        t0 = pl.program_id(0) * tp
        cur = c_ref[...]
        e = _with_halo(cur, p_ref[...], n_ref[...], t0, n)
        tok = t0 - POOL_HALO + lax.broadcasted_iota(jnp.int32, (tp + 2 * POOL_HALO, 1), 0)
        e = e / jnp.maximum(_window_count(tok, w, n), 1.0)
        du_ref[...] = (_window_sum(e, w, True)[POOL_HALO:POOL_HALO + tp] - cur).astype(_BF16)

    return pl.pallas_call(
        body, name=name, grid=(n // tp,),
        in_specs=_halo_specs(tp, pg, 0, 0, n), out_specs=pl.BlockSpec((tp, pg), lambda i: (i, 0)),
        out_shape=jax.ShapeDtypeStruct((n, pg), _BF16),
        compiler_params=pltpu.CompilerParams(dimension_semantics=("parallel",)),
    )(dd, dd, dd)


def _post(out, x, target, gate, g_post, tr, name):
    n, d = out.shape

    def body(o_ref, x_ref, t_ref, gate_ref, g_ref, dxn_ref, do_ref, dgate_ref, dg_ref, loss_ref):
        @pl.when(pl.program_id(0) == 0)
        def _():
            dgate_ref[...] = jnp.zeros_like(dgate_ref)
            dg_ref[...] = jnp.zeros_like(dg_ref)
            loss_ref[...] = jnp.zeros_like(loss_ref)

        ov = o_ref[...]
        r = _rms(ov)
        on = ov * r
        normed = on * g_ref[...]
        err = (x_ref[...] + gate_ref[...] * normed) - t_ref[...]
        loss_ref[...] += jnp.sum(err * err)
        dxn = err / d
        dxn_ref[...] = dxn
        dgate_ref[0:1, :] += jnp.sum(dxn * normed, axis=0, keepdims=True)
        dr = dxn * gate_ref[...]
        dg_ref[0:1, :] += jnp.sum(dr * on, axis=0, keepdims=True)
        dn = dr * g_ref[...]
        do_ref[...] = (r * (dn - on * jnp.mean(dn * on, axis=-1, keepdims=True))).astype(_BF16)

    blk = pl.BlockSpec((tr, d), lambda i: (i, 0))
    vec = pl.BlockSpec((1, d), lambda i: (0, 0))
    acc = pl.BlockSpec((SUBLANES, d), lambda i: (0, 0))
    return pl.pallas_call(
        body, name=name, grid=(n // tr,),
        in_specs=[blk, blk, blk, vec, vec],
        out_specs=[blk, blk, acc, acc, pl.BlockSpec((SUBLANES, LANES), lambda i: (0, 0))],
        out_shape=[jax.ShapeDtypeStruct((n, d), _F32), jax.ShapeDtypeStruct((n, d), _BF16),
                   jax.ShapeDtypeStruct((SUBLANES, d), _F32), jax.ShapeDtypeStruct((SUBLANES, d), _F32),
                   jax.ShapeDtypeStruct((SUBLANES, LANES), _F32)],
        compiler_params=pltpu.CompilerParams(dimension_semantics=("arbitrary",)),
    )(out, x, target, gate, g_post)


def _adam_sharded(slab_ids, grad, got, far, w, m, v, name):
    r, c = w.shape
    tr = _tile(r, max(BF16_ROWS, min(256, (1 << 18) // c)), BF16_ROWS)

    def body(ids_ref, own_ref, got_ref, far_ref, w_ref, m_ref, v_ref, g_ref, dl_ref, nm_ref, nv_ref):
        del ids_ref
        g = own_ref[0] + got_ref[0]
        for k in range(N_CHIPS - 1):
            g = g + far_ref[k].astype(_F32)
        delta, nm, nv = _adamw(w_ref[...], g, m_ref[...], v_ref[...])
        g_ref[...] = g
        dl_ref[...] = delta
        nm_ref[...] = nm
        nv_ref[...] = nv

    blk = pl.BlockSpec((tr, c), lambda i, ids: (i, 0))
    return pl.pallas_call(
        body, name=name,
        grid_spec=pltpu.PrefetchScalarGridSpec(
            num_scalar_prefetch=1, grid=(r // tr,),
            in_specs=[pl.BlockSpec((1, tr, c), lambda i, ids: (ids[0], i, 0)),
                      pl.BlockSpec((1, tr, c), lambda i, ids: (0, i, 0)),
                      pl.BlockSpec((N_CHIPS - 1, tr, c), lambda i, ids: (0, i, 0)), blk, blk, blk],
            out_specs=[blk] * 4),
        out_shape=[jax.ShapeDtypeStruct((r, c), _F32)] * 4,
        compiler_params=pltpu.CompilerParams(dimension_semantics=("parallel",)),
    )(slab_ids, grad, got, far, w, m, v)


def _adam_replicated(parts, extra, through_silu, w, m, v, name):
    def body(p_ref, e_ref, s_ref, w_ref, m_ref, v_ref, g_ref, dl_ref, nm_ref, nv_ref):
        total = p_ref[0] + e_ref[0]
        for dev in range(1, N_DEV):
            total = total + (p_ref[dev] + e_ref[dev])
        g = jnp.where(s_ref[...] > 0.5, total * _silu_grad(w_ref[...]), total)
        delta, nm, nv = _adamw(w_ref[...], g, m_ref[...], v_ref[...])
        g_ref[...] = g
        dl_ref[...] = delta
        nm_ref[...] = nm
        nv_ref[...] = nv

    return pl.pallas_call(
        body, name=name, in_specs=[_VMEM] * 6, out_specs=[_VMEM] * 4,
        out_shape=[jax.ShapeDtypeStruct(w.shape, _F32)] * 4,
    )(parts, extra, through_silu, w, m, v)


def _as_rows(vec):
    size = vec.shape[0]
    padded = -(-size // (SUBLANES * LANES)) * SUBLANES * LANES
    return jnp.pad(vec, (0, padded - size)).reshape(padded // LANES, LANES)


def kernel(x, c, ctx, c_ctx, w_ada, b_ada, norm_pre, norm_post, w_in, q_norm, k_norm, pool_w, pool_scale, w_out, loss_target, m_c_ctx, m_w_ada, m_b_ada, m_norm_pre, m_norm_post, m_w_in, m_q_norm, m_k_norm, m_pool_w, m_pool_scale, m_w_out, v_c_ctx, v_w_ada, v_b_ada, v_norm_pre, v_norm_post, v_w_in, v_q_norm, v_k_norm, v_pool_w, v_pool_scale, v_w_out):
    me = _dev_index(*_position())
    x2, ctx2, target = x[0], ctx[0], loss_target[0]
    n, d = x2.shape
    l = ctx2.shape[0]
    t = l + n
    aw = d // 2
    heads = aw // HEAD_DIM
    kv_heads = heads // GQA_GROUP
    kw = kv_heads * HEAD_DIM
    n_groups = len(POOL_WINDOWS)
    pg = (d - aw) // n_groups
    mix = d
    tr = _tile(l, 128, BF16_ROWS)
    tq = _tile(l, 128, BF16_ROWS)
    tp = _tile(l, 512, POOL_HALO)

    w_in_b = _cast_bf16(w_in[0], "cast_w_in")
    w_out_b = _cast_bf16(w_out[0], "cast_w_out")
    pool_b = _cast_bf16(pool_w[0].reshape(-1, pg), "cast_pool_w").reshape(pool_w.shape[1:])
    w_in_g, w_out_g, pool_g = _gather_weights([w_in_b[None], w_out_b[None], pool_b], [2, 1, 1], "gather_weights")
    w_in_g, w_out_g = w_in_g[0], w_out_g[0]

    c_all = _all_gather_small(_as_rows(c[0]), "gather_c").reshape(N_DEV, -1)[:, :d]
    craw = jnp.concatenate([c_all, jnp.broadcast_to(c_ctx[None], (N_DEV, d))], axis=0)
    ada = _ada_forward(craw, w_ada[0], "ada_forward")
    ada_all = _all_gather_small(ada, "gather_ada")
    mod_all = ada_all.transpose(1, 0, 2).reshape(ada.shape[0], -1) + b_ada[0]
    mod = lax.dynamic_index_in_dim(mod_all, me, 0, keepdims=False)
    mod_c = mod_all[N_DEV]
    shift, scale, gate = mod[:d], mod[d:2 * d], mod[2 * d:]
    zeros6 = jnp.zeros((SUBLANES - 2, d), _F32)
    mods = jnp.stack([jnp.concatenate([mod_c[None, d:2 * d], mod_c[None, :d], zeros6], axis=0),
                      jnp.concatenate([scale[None], shift[None], zeros6], axis=0)])

    h_all = _prenorm(ctx2, x2, norm_pre, mods, tr, "prenorm")
    proj = _matmul(h_all, w_in_g, tm=1088, name="proj")
    tables = _rope_tables(l, n)
    q, k, v = _qkv_post(proj, tables, q_norm, k_norm, heads, kv_heads, tr, "qkv_post")
    attn_o, y, lse = _attention(q, k, v, proj, l, mix, tq, "attention")
    raws, ds = [], []
    for gi in range(n_groups):
        y, raw, dsave = _pool_forward(gi, proj, y, pool_g, pool_scale, l, heads, kv_heads, tp, f"pool_forward_{gi}")
        raws.append(raw)
        ds.append(dsave)
    out = _matmul(y, w_out_g, name="out_proj")
    dxn, dout, dgate8, dgpost8, loss8 = _post(out, x2, target, gate[None], norm_post, tr, "post")

    dy = _matmul(dout, w_out_g, tb=True, name="d_y")
    gw_out = _matmul(y, dout, ta=True, name="grad_w_out").reshape(N_DEV, mix // N_DEV, d)
    xi, yi, ci = _position()
    slab_ids = jnp.stack([_dev_index(*chip, ci) for chip in _chip_order(xi, yi)]).astype(jnp.int32)
    got_out = _exchange_sibling([gw_out], "exchange_sibling_w_out")[0]
    sum_out = _pre_add(slab_ids, gw_out, got_out, "pre_add_w_out")
    flight_out = _exchange_chips_start([sum_out], "exchange_chips_start_w_out")
    dq, dgate_attn, dk, dv = _attention_backward(
        q, k, v, attn_o, dy, proj, lse, flight_out[-1], l, tq, "attention_backward")
    dqkv, dgq8, dgk8 = _qkv_post_backward(proj, dq, dk, dv, tables, q_norm, k_norm, l, tr, "qkv_post_backward")
    dus, dgps, gpw, dps8 = [], [], [], []
    for gi in range(n_groups):
        dgp, draw, dd, dps = _pool_backward_gate(
            gi, dy, proj, raws[gi], pool_g, pool_scale, l, heads, kv_heads, tp, f"pool_backward_gate_{gi}")
        dus.append(_pool_backward_window(gi, dd, tp, f"pool_backward_window_{gi}"))
        dgps.append(dgp)
        dps8.append(dps)
        gpw.append(_matmul(ds[gi], draw, ta=True, name=f"grad_pool_w_{gi}"))
    latent_cols = jnp.concatenate([dgate_attn] + dus + dgps, axis=1)
    dproj = jnp.concatenate([dqkv, jnp.pad(latent_cols, ((l, 0), (0, 0)))], axis=1)
    gw_in = _matmul(h_all, dproj, ta=True, tk=2176, col_slabs=N_DEV, name="grad_w_in")
    pr = pool_w.shape[2]
    gpw8 = jnp.stack(gpw).reshape(n_groups, N_DEV, pr, pg).transpose(1, 0, 2, 3).reshape(N_DEV, n_groups * pr, pg)
    got_in, got_pw = _exchange_sibling([gw_in, gpw8], "exchange_sibling_w_in")
    sums_in = [_pre_add(slab_ids, gw_in, got_in, "pre_add_w_in"), _pre_add(slab_ids, gpw8, got_pw, "pre_add_pool_w")]
    flight_in = _exchange_chips_start(sums_in, "exchange_chips_start_w_in")
    dh = _matmul(dproj, w_in_g, tb=True, tm=1088, after=flight_in[-1], name="d_h")
    grad_x, dmods, dgpre8 = _prenorm_backward(dh, ctx2, x2, dxn, norm_pre, mods, tr, "prenorm_backward")

    dmod_lat = jnp.concatenate([dmods[1, 1], dmods[1, 0], dgate8[0]])
    dmod_ctx = jnp.concatenate([dmods[0, 1], dmods[0, 0], jnp.zeros((d,), _F32)])
    small = jnp.concatenate([dmod_lat, dmod_ctx, dgpre8[0], dgpost8[0], dgq8[0], dgk8[0]] + [p[0] for p in dps8]
                            + [loss8[0, :1]])
    gathered = _all_gather_small(_as_rows(small), "gather_small").reshape(N_DEV, -1)
    o = 0
    take = lambda size: (gathered[:, o:o + size], o + size)
    g_mod, o = take(3 * d)
    g_modc, o = take(3 * d)
    g_pre, o = take(d)
    g_post, o = take(d)
    g_q, o = take(HEAD_DIM)
    g_k, o = take(HEAD_DIM)
    g_ps, o = take(n_groups * pg)
    g_loss, o = take(1)
    cols = w_ada.shape[-1]
    mine = lambda a: lax.dynamic_slice_in_dim(a, me * cols, cols, axis=1)
    dmod_rows = jnp.concatenate([mine(g_mod), mine(g_modc)], axis=0)
    g_wada, dl_wada, nm_wada, nv_wada, dcact = _ada_backward(craw, dmod_rows, w_ada[0], m_w_ada[0], v_w_ada[0], "ada_backward")
    dcc = _all_gather_small(_as_rows(dcact[0]), "gather_dcc").reshape(N_DEV, -1)[:, :d]

    sizes = [d, 3 * d, d, d, HEAD_DIM, HEAD_DIM, n_groups * pg]
    pack = lambda parts: jnp.stack([_as_rows(jnp.concatenate([p[dev] for p in parts])) for dev in range(N_DEV)])
    zero = lambda size: jnp.zeros((N_DEV, size), _F32)
    parts = pack([dcc, g_mod, g_pre, g_post, g_q, g_k, g_ps])
    extra = pack([zero(d), g_modc, zero(d), zero(d), zero(HEAD_DIM), zero(HEAD_DIM), zero(n_groups * pg)])
    through_silu = _as_rows(jnp.concatenate([jnp.ones((d,), _F32), jnp.zeros((sum(sizes[1:]),), _F32)]))
    cat = lambda items: _as_rows(jnp.concatenate([a.reshape(-1) for a in items]))
    ws = [c_ctx, b_ada, norm_pre, norm_post, q_norm, k_norm, pool_scale]
    ms = [m_c_ctx, m_b_ada, m_norm_pre, m_norm_post, m_q_norm, m_k_norm, m_pool_scale]
    vs = [v_c_ctx, v_b_ada, v_norm_pre, v_norm_post, v_q_norm, v_k_norm, v_pool_scale]
    rep = _adam_replicated(parts, extra, through_silu, cat(ws), cat(ms), cat(vs), "adam_replicated")

    def split(packed):
        flat_, outs, at = packed.reshape(-1), [], 0
        for w, size in zip(ws, sizes):
            outs.append(flat_[at:at + size].reshape(w.shape))
            at += size
        return outs

    g_rep, dl_rep, nm_rep, nv_rep = [split(r) for r in rep]

    far_out = _exchange_chips_wait(*flight_out[:4], grad_x, "exchange_chips_wait_w_out")[0]
    far_in, far_pw = _exchange_chips_wait(*flight_in[:4], rep[0], "exchange_chips_wait_w_in")
    two = lambda a: a.reshape(-1, a.shape[-1])
    sharded = []
    for g, got, far, w, m, v_, name in zip(
            (gw_in, gw_out, gpw8), (got_in, got_out, got_pw), (far_in, far_out, far_pw), (w_in, w_out, pool_w),
            (m_w_in, m_w_out, m_pool_w), (v_w_in, v_w_out, v_pool_w), ("adam_w_in", "adam_w_out", "adam_pool_w")):
        res = _adam_sharded(slab_ids, g, got, far, two(w), two(m), two(v_), name)
        sharded.append([r.reshape(w.shape) for r in res])
    (g_w_in, dl_w_in, nm_w_in, nv_w_in), (g_w_out, dl_w_out, nm_w_out, nv_w_out), (g_pw, dl_pw, nm_pw, nv_pw) = sharded

    loss_sum = g_loss[0, 0]
    for dev in range(1, N_DEV):
        loss_sum = loss_sum + g_loss[dev, 0]
    loss = (0.5 / d) * loss_sum

    def ordered(rep_list, ada_, w_in_, pw_, w_out_):
        return [rep_list[0], ada_[None], rep_list[1], rep_list[2], rep_list[3], w_in_, rep_list[4], rep_list[5],
                pw_, rep_list[6], w_out_]

    return (loss, grad_x[None],
            *ordered(g_rep, g_wada, g_w_in, g_pw, g_w_out),
            *ordered(dl_rep, dl_wada, dl_w_in, dl_pw, dl_w_out),
            *ordered(nm_rep, nm_wada, nm_w_in, nm_pw, nm_w_out),
            *ordered(nv_rep, nv_wada, nv_w_in, nv_pw, nv_w_out))
```

```python
import functools

import jax
import jax.numpy as jnp
from jax import lax
from jax.experimental import pallas as pl
from jax.experimental.pallas import tpu as pltpu

HEAD_DIM = 128
GQA_GROUP = 4
ATTN_SUB_HEADS = 2
LOG2_E = 1.4426950408889634
GRID_W = 64
ROPE_PAIRS = HEAD_DIM // 4
ROPE_THETA = 10000.0
ATTN_SCALE = HEAD_DIM ** -0.5
EPS = 1e-6
POOL_WINDOWS = (2, 4, 8, 16)
POOL_HALO = 8
N_DEV = 8
N_CHIPS = 4
ADAM_LR = 0.001
ADAM_B1 = 0.9
ADAM_B2 = 0.999
ADAM_EPS = 1e-08
ADAM_WD = 0.01
ADAM_STEP = 10

LANES = 128
SUBLANES = 8
BF16_ROWS = 16

_MESH = pl.DeviceIdType.MESH
_ANY = pl.BlockSpec(memory_space=pl.ANY)
_VMEM = pl.BlockSpec(memory_space=pltpu.VMEM)
_HBM = pl.BlockSpec(memory_space=pltpu.HBM)
_SEM = pl.BlockSpec(memory_space=pltpu.SEMAPHORE)
_EFFECT = pltpu.SideEffectType.DATAFLOW_SIDE_EFFECTING
_F32 = jnp.float32
_BF16 = jnp.bfloat16


def _tile(dim, pref, align):
    t = min(pref, dim)
    t -= t % align
    while t >= align:
        if dim % t == 0:
            return t
        t -= align
    return dim


def _position():
    return lax.axis_index("x"), lax.axis_index("y"), lax.axis_index("c")


def _flip(v, bit):
    return 1 - v if bit else v


def _dev_index(x, y, c):
    return 4 * x + 2 * y + c


def _silu(g):
    return g * jax.nn.sigmoid(g)


def _silu_grad(g):
    s = jax.nn.sigmoid(g)
    return s * (1.0 + g * (1.0 - s))


def _adamw(w, g, m, v):
    m = ADAM_B1 * m + (1.0 - ADAM_B1) * g
    v = ADAM_B2 * v + (1.0 - ADAM_B2) * (g * g)
    m_hat = m / (1.0 - ADAM_B1 ** ADAM_STEP)
    v_hat = v / (1.0 - ADAM_B2 ** ADAM_STEP)
    delta = -ADAM_LR * (m_hat / (jnp.sqrt(v_hat) + ADAM_EPS) + ADAM_WD * w)
    return delta, m, v


def _all_gather_small(v, name):
    rows, cols = v.shape

    def body(v_ref, out_ref, send_sems, recv_sems):
        x, y, c = _position()
        me = _dev_index(x, y, c)
        out_ref[me] = v_ref[...]
        peers = [(_flip(x, k & 4), _flip(y, k & 2), _flip(c, k & 1)) for k in range(1, N_DEV)]

        def copy(k, block, to):
            return pltpu.make_async_remote_copy(
                src_ref=v_ref, dst_ref=out_ref.at[block], send_sem=send_sems.at[k], recv_sem=recv_sems.at[k],
                device_id=to, device_id_type=_MESH)

        sends = [copy(k, me, p) for k, p in enumerate(peers)]
        for s in sends:
            s.start()
        for k, p in enumerate(peers):
            copy(k, _dev_index(*p), p).wait_recv()
        for s in sends:
            s.wait_send()

    return pl.pallas_call(
        body, name=name,
        out_shape=jax.ShapeDtypeStruct((N_DEV, rows, cols), v.dtype),
        in_specs=[_VMEM], out_specs=_VMEM,
        scratch_shapes=[pltpu.SemaphoreType.DMA((N_DEV - 1,)), pltpu.SemaphoreType.DMA((N_DEV - 1,))],
    )(v)


def _window(ref, axis, size, j):
    start = pl.multiple_of(j * size, size)
    if axis == 0:
        return ref.at[pl.ds(start, size)]
    if axis == 1:
        return ref.at[:, pl.ds(start, size), :]
    return ref.at[:, :, pl.ds(start, size)]


def _gather_weights(shards, axes, name):
    n = len(shards)
    sizes = [s.shape[a] for s, a in zip(shards, axes)]
    out_shapes = [
        jax.ShapeDtypeStruct(tuple(d * N_DEV if i == a else d for i, d in enumerate(s.shape)), s.dtype)
        for s, a in zip(shards, axes)]

    def body(*refs):
        srcs, outs = refs[:n], refs[n:2 * n]
        send_sems, recv_sems, local_sems = refs[2 * n:]
        x, y, c = _position()
        me, sibling = (x, y, c), (x, y, 1 - c)
        chips = [(1 - x, y), (x, 1 - y), (1 - x, 1 - y)]
        firsts, passed, locals_ = [], [], []
        for a in range(n):
            def rows(block, a=a):
                return _window(outs[a], axes[a], sizes[a], _dev_index(*block))

            def copy(k, block, to, src=None, a=a, rows=rows):
                return pltpu.make_async_remote_copy(
                    src_ref=rows(block) if src is None else src, dst_ref=rows(block),
                    send_sem=send_sems.at[7 * a + k], recv_sem=recv_sems.at[7 * a + k],
                    device_id=to, device_id_type=_MESH)

            mine = pltpu.make_async_copy(srcs[a], rows(me), local_sems.at[a])
            mine.start()
            locals_.append(mine)
            first = [copy(0, me, sibling, src=srcs[a])]
            first += [copy(1 + j, me, (*chip, c), src=srcs[a]) for j, chip in enumerate(chips)]
            for cp in first:
                cp.start()
            firsts.append((first, copy))
        for a in range(n):
            first, copy = firsts[a]
            fwd = [copy(4 + j, (*chip, c), sibling) for j, chip in enumerate(chips)]
            for j, chip in enumerate(chips):
                copy(1 + j, (*chip, c), me).wait_recv()
                fwd[j].start()
            passed.append(fwd)
        for a in range(n):
            first, copy = firsts[a]
            copy(0, sibling, me).wait_recv()
            for j, chip in enumerate(chips):
                copy(4 + j, (*chip, 1 - c), me).wait_recv()
            for cp in first + passed[a]:
                cp.wait_send()
            locals_[a].wait()

    return pl.pallas_call(
        body, name=name, out_shape=out_shapes,
        in_specs=[_ANY] * n, out_specs=[_ANY] * n,
        scratch_shapes=[pltpu.SemaphoreType.DMA((7 * n,)), pltpu.SemaphoreType.DMA((7 * n,)),
                        pltpu.SemaphoreType.DMA((n,))],
    )(*shards)


def _slab_copies(bufs, send_sems, recv_sems):
    x, y, c = _position()
    me = _dev_index(x, y, c)
    peers = [(_flip(x, k & 4), _flip(y, k & 2), _flip(c, k & 1)) for k in range(1, N_DEV)]
    return [pltpu.make_async_remote_copy(
        src_ref=buf.at[me], dst_ref=buf.at[me],
        send_sem=send_sems.at[(N_DEV - 1) * a + k], recv_sem=recv_sems.at[(N_DEV - 1) * a + k],
        device_id=peer, device_id_type=_MESH)
        for a, buf in enumerate(bufs) for k, peer in enumerate(peers)]


def _gather_slabs_start(bufs, after, name):
    n = len(bufs)
    n_copies = (N_DEV - 1) * n

    def body(*refs):
        send_sems, recv_sems, token = refs[n + 1], refs[n + 2], refs[-1]
        for cp in _slab_copies(refs[:n], send_sems, recv_sems):
            cp.start()
        token[...] = jnp.zeros_like(token)

    outs = pl.pallas_call(
        body, name=name,
        out_shape=(pltpu.SemaphoreType.DMA((n_copies,)), pltpu.SemaphoreType.DMA((n_copies,)),
                   *[pltpu.HBM(b.shape, b.dtype) for b in bufs], jax.ShapeDtypeStruct((SUBLANES, LANES), _F32)),
        in_specs=[_HBM] * n + [_ANY], out_specs=(_SEM, _SEM, *[_HBM] * n, _VMEM),
        input_output_aliases={i: 2 + i for i in range(n)},
        compiler_params=pltpu.CompilerParams(has_side_effects=_EFFECT),
    )(*[pltpu.with_memory_space_constraint(b, pltpu.HBM) for b in bufs], after)
    return outs[0], outs[1], list(outs[2:2 + n]), outs[-1]


def _gather_slabs_wait(send_sems, recv_sems, bufs, after, name):
    n = len(bufs)

    def body(*refs):
        for cp in _slab_copies(refs[:n], refs[n], refs[n + 1]):
            cp.wait_send()
            cp.wait_recv()

    outs = pl.pallas_call(
        body, name=name, out_shape=tuple(pltpu.HBM(b.shape, b.dtype) for b in bufs),
        in_specs=[_HBM] * n + [_SEM, _SEM, _ANY], out_specs=[_HBM] * n,
        input_output_aliases={i: i for i in range(n)},
        compiler_params=pltpu.CompilerParams(has_side_effects=_EFFECT),
    )(*bufs, send_sems, recv_sems, after)
    return list(outs)


def _chip_order(x, y):
    return [(x, y), (1 - x, y), (x, 1 - y), (1 - x, 1 - y)]


def _exchange_sibling(grads, name):
    n = len(grads)
    out_shapes = [jax.ShapeDtypeStruct((N_CHIPS,) + g.shape[1:], g.dtype) for g in grads]

    def body(*refs):
        srcs, gots = refs[:n], refs[n:2 * n]
        send_sems, recv_sems = refs[2 * n:]
        x, y, c = _position()
        sends = []
        for a in range(n):
            for s, chip in enumerate(_chip_order(x, y)):
                k = N_CHIPS * a + s
                give = pltpu.make_async_remote_copy(
                    src_ref=srcs[a].at[_dev_index(*chip, 1 - c)], dst_ref=gots[a].at[s],
                    send_sem=send_sems.at[k], recv_sem=recv_sems.at[k], device_id=(x, y, 1 - c), device_id_type=_MESH)
                give.start()
                sends.append(give)
        for cp in sends:
            cp.wait_recv()
        for cp in sends:
            cp.wait_send()

    return pl.pallas_call(
        body, name=name, out_shape=out_shapes,
        in_specs=[_ANY] * n, out_specs=[_ANY] * n,
        scratch_shapes=[pltpu.SemaphoreType.DMA((N_CHIPS * n,)), pltpu.SemaphoreType.DMA((N_CHIPS * n,))],
    )(*grads)


def _chip_copies(srcs, lands, send_sems, recv_sems):
    x, y, c = _position()
    return [pltpu.make_async_remote_copy(
        src_ref=srcs[a].at[k], dst_ref=lands[a].at[k],
        send_sem=send_sems.at[(N_CHIPS - 1) * a + k], recv_sem=recv_sems.at[(N_CHIPS - 1) * a + k],
        device_id=(*chip, c), device_id_type=_MESH)
        for a in range(len(srcs)) for k, chip in enumerate(_chip_order(x, y)[1:])]


def _exchange_chips_start(sums, name):
    n = len(sums)
    n_copies = (N_CHIPS - 1) * n

    def body(*refs):
        srcs, lands = refs[:n], refs[n:2 * n]
        send_sems, recv_sems, token = refs[2 * n], refs[2 * n + 1], refs[-1]
        for cp in _chip_copies(srcs, lands, send_sems, recv_sems):
            cp.start()
        token[...] = jnp.zeros_like(token)

    hbm = [pltpu.HBM(s.shape, s.dtype) for s in sums]
    outs = pl.pallas_call(
        body, name=name,
        out_shape=(pltpu.SemaphoreType.DMA((n_copies,)), pltpu.SemaphoreType.DMA((n_copies,)), *hbm, *hbm,
                   jax.ShapeDtypeStruct((SUBLANES, LANES), _F32)),
        in_specs=[_HBM] * (2 * n), out_specs=(_SEM, _SEM, *[_HBM] * (2 * n), _VMEM),
        input_output_aliases={i: 2 + i for i in range(2 * n)},
        compiler_params=pltpu.CompilerParams(has_side_effects=_EFFECT),
    )(*[pltpu.with_memory_space_constraint(s, pltpu.HBM) for s in sums],
      *[pltpu.with_memory_space_constraint(lax.empty(s.shape, s.dtype), pltpu.HBM) for s in sums])
    return outs[0], outs[1], list(outs[2:2 + n]), list(outs[2 + n:2 + 2 * n]), outs[-1]


def _exchange_chips_wait(send_sems, recv_sems, srcs, lands, after, name):
    n = len(srcs)

    def body(*refs):
        for cp in _chip_copies(refs[:n], refs[n:2 * n], refs[2 * n], refs[2 * n + 1]):
            cp.wait_send()
            cp.wait_recv()

    hbm = [pltpu.HBM(s.shape, s.dtype) for s in srcs]
    outs = pl.pallas_call(
        body, name=name, out_shape=(*hbm, *hbm),
        in_specs=[_HBM] * (2 * n) + [_SEM, _SEM, _ANY], out_specs=[_HBM] * (2 * n),
        input_output_aliases={i: i for i in range(2 * n)},
        compiler_params=pltpu.CompilerParams(has_side_effects=_EFFECT),
    )(*srcs, *lands, send_sems, recv_sems, after)
    return list(outs[n:])


def _matmul(a, b, *, ta=False, tb=False, out_dtype=_F32, tm=1024, tn=1024, tk=4608, col_slabs=None, after=None, name):
    kdim, m = a.shape if ta else a.shape[::-1]
    n = b.shape[0] if tb else b.shape[1]
    tm = _tile(m, tm, LANES if ta else BF16_ROWS)
    tn = n // col_slabs if col_slabs else _tile(n, tn, LANES)
    tk = _tile(kdim, tk, BF16_ROWS if ta else LANES)
    nk = kdim // tk
    dims = (((0 if ta else 1,), (1 if tb else 0,)), ((), ()))

    def body_whole_k(a_ref, b_ref, *rest):
        o_ref = rest[-1]
        part = lax.dot_general(a_ref[...], b_ref[...], dims, preferred_element_type=_F32)
        o_ref[...] = part.astype(out_dtype).reshape(o_ref.shape)

    def body_split_k(a_ref, b_ref, *rest):
        o_ref, acc_ref = rest[-2:]
        k = pl.program_id(2)

        @pl.when(k == 0)
        def _():
            acc_ref[...] = jnp.zeros_like(acc_ref)

        acc_ref[...] += lax.dot_general(a_ref[...], b_ref[...], dims, preferred_element_type=_F32)

        @pl.when(k == nk - 1)
        def _():
            o_ref[...] = acc_ref[...].astype(out_dtype).reshape(o_ref.shape)

    a_spec = pl.BlockSpec((tk, tm), lambda i, j, k: (k, i)) if ta else pl.BlockSpec((tm, tk), lambda i, j, k: (i, k))
    b_spec = pl.BlockSpec((tn, tk), lambda i, j, k: (j, k)) if tb else pl.BlockSpec((tk, tn), lambda i, j, k: (k, j))
    if col_slabs:
        out_spec = pl.BlockSpec((1, tm, tn), lambda i, j, k: (j, i, 0))
        out_shape = jax.ShapeDtypeStruct((col_slabs, m, tn), out_dtype)
    else:
        out_spec = pl.BlockSpec((tm, tn), lambda i, j, k: (i, j))
        out_shape = jax.ShapeDtypeStruct((m, n), out_dtype)
    extra = [] if after is None else [after]
    return pl.pallas_call(
        body_whole_k if nk == 1 else body_split_k, name=name, grid=(m // tm, n // tn, nk),
        in_specs=[a_spec, b_spec] + [pl.BlockSpec(t.shape, lambda i, j, k: (0, 0)) for t in extra],
        out_specs=out_spec, out_shape=out_shape,
        scratch_shapes=[] if nk == 1 else [pltpu.VMEM((tm, tn), _F32)],
        compiler_params=pltpu.CompilerParams(dimension_semantics=("parallel", "parallel", "arbitrary")),
    )(a, b, *extra)


def _row_tile(rows, cols):
    return _tile(rows, max(BF16_ROWS, min(512, (1 << 19) // cols)), BF16_ROWS)


def _cast_bf16(a, name):
    r, c = a.shape
    tr = _row_tile(r, c)

    def body(a_ref, o_ref):
        o_ref[...] = a_ref[...].astype(_BF16)

    blk = pl.BlockSpec((tr, c), lambda i: (i, 0))
    return pl.pallas_call(
        body, name=name, grid=(r // tr,), in_specs=[blk], out_specs=blk,
        out_shape=jax.ShapeDtypeStruct((r, c), _BF16),
        compiler_params=pltpu.CompilerParams(dimension_semantics=("parallel",)),
    )(a)


def _cast_into_slab(slab_ids, a, name):
    r, c = a.shape
    tr = _row_tile(r, c)

    def body(ids_ref, a_ref, o_ref):
        del ids_ref
        o_ref[0] = a_ref[...].astype(_BF16)

    return pl.pallas_call(
        body, name=name,
        grid_spec=pltpu.PrefetchScalarGridSpec(
            num_scalar_prefetch=1, grid=(r // tr,),
            in_specs=[pl.BlockSpec((tr, c), lambda i, ids: (i, 0))],
            out_specs=pl.BlockSpec((1, tr, c), lambda i, ids: (ids[0], i, 0))),
        out_shape=jax.ShapeDtypeStruct((N_DEV, r, c), _BF16),
        compiler_params=pltpu.CompilerParams(dimension_semantics=("parallel",)),
    )(slab_ids, a)


def _pre_add(slab_ids, grad, got, name):
    _, r, c = grad.shape
    tr = _row_tile(r, c)

    def body(ids_ref, a_ref, b_ref, o_ref):
        del ids_ref
        o_ref[...] = (a_ref[...] + b_ref[...]).astype(_BF16)

    return pl.pallas_call(
        body, name=name,
        grid_spec=pltpu.PrefetchScalarGridSpec(
            num_scalar_prefetch=1, grid=(N_CHIPS - 1, r // tr),
            in_specs=[pl.BlockSpec((1, tr, c), lambda s, i, ids: (ids[s + 1], i, 0)),
                      pl.BlockSpec((1, tr, c), lambda s, i, ids: (s + 1, i, 0))],
            out_specs=pl.BlockSpec((1, tr, c), lambda s, i, ids: (s, i, 0))),
        out_shape=jax.ShapeDtypeStruct((N_CHIPS - 1, r, c), _BF16),
        compiler_params=pltpu.CompilerParams(dimension_semantics=("parallel", "parallel")),
    )(slab_ids, grad, got)


def _ada_forward(craw, w_shard, name):
    d, cols = w_shard.shape
    tk = _tile(d, 512, LANES)

    def body(c_ref, w_ref, o_ref):
        @pl.when(pl.program_id(0) == 0)
        def _():
            o_ref[...] = jnp.zeros_like(o_ref)

        o_ref[...] += jnp.dot(_silu(c_ref[...]).astype(_BF16), w_ref[...].astype(_BF16), preferred_element_type=_F32)

    return pl.pallas_call(
        body, name=name, grid=(d // tk,),
        in_specs=[pl.BlockSpec((craw.shape[0], tk), lambda k: (0, k)), pl.BlockSpec((tk, cols), lambda k: (k, 0))],
        out_specs=pl.BlockSpec((craw.shape[0], cols), lambda k: (0, 0)),
        out_shape=jax.ShapeDtypeStruct((craw.shape[0], cols), _F32),
        compiler_params=pltpu.CompilerParams(dimension_semantics=("arbitrary",)),
    )(craw, w_shard)


def _ada_backward(craw, dmod, w, m, v, name):
    d, cols = w.shape
    rows = craw.shape[0]
    tr = _tile(d, 256, LANES)

    def body(c_ref, dm_ref, w_ref, m_ref, v_ref, g_ref, dl_ref, nm_ref, nv_ref, dc_ref):
        act = _silu(c_ref[...]).astype(_BF16)
        dmb = dm_ref[...].astype(_BF16)
        wv = w_ref[...]
        g = lax.dot_general(act, dmb, (((0,), (0,)), ((), ())), preferred_element_type=_F32)
        delta, nm, nv = _adamw(wv, g, m_ref[...], v_ref[...])
        g_ref[...] = g
        dl_ref[...] = delta
        nm_ref[...] = nm
        nv_ref[...] = nv
        dc = lax.dot_general(dmb, wv.astype(_BF16), (((1,), (1,)), ((), ())), preferred_element_type=_F32)
        dc_ref[...] = jnp.broadcast_to(jnp.sum(dc[N_DEV:], axis=0, keepdims=True), dc_ref.shape)

    blk = pl.BlockSpec((tr, cols), lambda i: (i, 0))
    return pl.pallas_call(
        body, name=name, grid=(d // tr,),
        in_specs=[pl.BlockSpec((rows, tr), lambda i: (0, i)), pl.BlockSpec((rows, cols), lambda i: (0, 0)), blk, blk, blk],
        out_specs=[blk, blk, blk, blk, pl.BlockSpec((SUBLANES, tr), lambda i: (0, i))],
        out_shape=[jax.ShapeDtypeStruct((d, cols), _F32)] * 4 + [jax.ShapeDtypeStruct((SUBLANES, d), _F32)],
        compiler_params=pltpu.CompilerParams(dimension_semantics=("parallel",)),
    )(craw, dmod, w, m, v)


def _rms(xf):
    return lax.rsqrt(jnp.mean(xf * xf, axis=-1, keepdims=True) + EPS)


def _prenorm(ctx, x, g_pre, mods, tr, name):
    l, d = ctx.shape
    n = x.shape[0]
    nbc = l // tr

    def body(ctx_ref, x_ref, g_ref, mod_ref, h_ref):
        def emit(src_ref):
            xf = src_ref[...]
            y = (xf * _rms(xf)) * g_ref[...]
            h_ref[...] = (y * (1.0 + mod_ref[0, 0:1, :]) + mod_ref[0, 1:2, :]).astype(_BF16)

        is_ctx = pl.program_id(0) < nbc
        pl.when(is_ctx)(lambda: emit(ctx_ref))
        pl.when(jnp.logical_not(is_ctx))(lambda: emit(x_ref))

    return pl.pallas_call(
        body, name=name, grid=((l + n) // tr,),
        in_specs=[pl.BlockSpec((tr, d), lambda i: (jnp.minimum(i, nbc - 1), 0)),
                  pl.BlockSpec((tr, d), lambda i: (jnp.maximum(i - nbc, 0), 0)),
                  pl.BlockSpec((1, d), lambda i: (0, 0)),
                  pl.BlockSpec((1, SUBLANES, d), lambda i: ((i >= nbc).astype(jnp.int32), 0, 0))],
        out_specs=pl.BlockSpec((tr, d), lambda i: (i, 0)),
        out_shape=jax.ShapeDtypeStruct((l + n, d), _BF16),
        compiler_params=pltpu.CompilerParams(dimension_semantics=("arbitrary",)),
    )(ctx, x, g_pre, mods)


def _prenorm_backward(dh, ctx, x, dxn, g_pre, mods, tr, name):
    l, d = ctx.shape
    n = x.shape[0]
    nbc = l // tr

    def body(dh_ref, ctx_ref, x_ref, dxn_ref, g_ref, mod_ref, gx_ref, dmod_ref, dg_ref):
        i = pl.program_id(0)

        @pl.when(i == 0)
        def _():
            dg_ref[...] = jnp.zeros_like(dg_ref)

        @pl.when(jnp.logical_or(i == 0, i == nbc))
        def _():
            dmod_ref[...] = jnp.zeros_like(dmod_ref)

        def emit(src_ref, latent):
            xf = src_ref[...]
            r = _rms(xf)
            xn = xf * r
            dhv = dh_ref[...]
            one_scale = 1.0 + mod_ref[0, 0:1, :]
            dmod_ref[0, 0:1, :] += jnp.sum(dhv * (xn * g_ref[...]), axis=0, keepdims=True)
            dmod_ref[0, 1:2, :] += jnp.sum(dhv, axis=0, keepdims=True)
            dyg = dhv * one_scale
            dg_ref[0:1, :] += jnp.sum(dyg * xn, axis=0, keepdims=True)
            if latent:
                dn = dyg * g_ref[...]
                gx_ref[...] = dxn_ref[...] + r * (dn - xn * jnp.mean(dn * xn, axis=-1, keepdims=True))

        pl.when(i < nbc)(lambda: emit(ctx_ref, False))
        pl.when(i >= nbc)(lambda: emit(x_ref, True))

    lat = pl.BlockSpec((tr, d), lambda i: (jnp.maximum(i - nbc, 0), 0))
    sel = pl.BlockSpec((1, SUBLANES, d), lambda i: ((i >= nbc).astype(jnp.int32), 0, 0))
    return pl.pallas_call(
        body, name=name, grid=((l + n) // tr,),
        in_specs=[pl.BlockSpec((tr, d), lambda i: (i, 0)),
                  pl.BlockSpec((tr, d), lambda i: (jnp.minimum(i, nbc - 1), 0)),
                  lat, lat, pl.BlockSpec((1, d), lambda i: (0, 0)), sel],
        out_specs=[lat, sel, pl.BlockSpec((SUBLANES, d), lambda i: (0, 0))],
        out_shape=[jax.ShapeDtypeStruct((n, d), _F32), jax.ShapeDtypeStruct((2, SUBLANES, d), _F32),
                   jax.ShapeDtypeStruct((SUBLANES, d), _F32)],
        compiler_params=pltpu.CompilerParams(dimension_semantics=("arbitrary",)),
    )(dh, ctx, x, dxn, g_pre, mods)


def _rope_tables(l, n):
    rows = n // GRID_W
    row = jnp.repeat(jnp.arange(rows, dtype=_F32), GRID_W)
    col = jnp.tile(jnp.arange(GRID_W, dtype=_F32), rows)
    inv = ROPE_THETA ** (-jnp.arange(ROPE_PAIRS, dtype=_F32) / ROPE_PAIRS)
    ang_r, ang_c = row[:, None] * inv, col[:, None] * inv
    cr, sr, cc, sc = jnp.cos(ang_r), jnp.sin(ang_r), jnp.cos(ang_c), jnp.sin(ang_c)
    zero = jnp.zeros_like(sr)
    tc = jnp.concatenate([cr, cr, cc, cc], axis=-1)
    ta = jnp.concatenate([-sr, zero, -sc, zero], axis=-1)
    tb = jnp.concatenate([zero, sr, zero, sc], axis=-1)
    pad = lambda t, fill: jnp.concatenate([jnp.full((l, HEAD_DIM), fill, _F32), t], axis=0)
    return pad(tc, 1.0), pad(ta, 0.0), pad(tb, 0.0)


def _rope(y, tc, ta, tb):
    return y * tc + pltpu.roll(y, HEAD_DIM - ROPE_PAIRS, 1) * ta + pltpu.roll(y, ROPE_PAIRS, 1) * tb


def _rope_transposed(dy, tc, ta, tb):
    return dy * tc + pltpu.roll(dy * ta, ROPE_PAIRS, 1) + pltpu.roll(dy * tb, HEAD_DIM - ROPE_PAIRS, 1)


def _qkv_post(proj, tables, g_q, g_k, heads, kv_heads, tr, name):
    t = proj.shape[0]
    aw, kw = heads * HEAD_DIM, kv_heads * HEAD_DIM
    w3 = aw + 2 * kw

    def body(p_ref, tc_ref, ta_ref, tb_ref, gq_ref, gk_ref, q_ref, k_ref, v_ref):
        tabs = (tc_ref[...], ta_ref[...], tb_ref[...])

        def norm_rope(col, gain):
            xh = p_ref[:, col:col + HEAD_DIM]
            return _rope((xh * _rms(xh)) * gain, *tabs).astype(_BF16)

        for h in range(heads):
            q_ref[h] = norm_rope(h * HEAD_DIM, gq_ref[...])
        for h in range(kv_heads):
            k_ref[h] = norm_rope(aw + h * HEAD_DIM, gk_ref[...])
            v_ref[h] = p_ref[:, aw + kw + h * HEAD_DIM:aw + kw + (h + 1) * HEAD_DIM].astype(_BF16)

    tab = pl.BlockSpec((tr, HEAD_DIM), lambda i: (i, 0))
    gain = pl.BlockSpec((1, HEAD_DIM), lambda i: (0, 0))
    return pl.pallas_call(
        body, name=name, grid=(t // tr,),
        in_specs=[pl.BlockSpec((tr, w3), lambda i: (i, 0)), tab, tab, tab, gain, gain],
        out_specs=[pl.BlockSpec((heads, tr, HEAD_DIM), lambda i: (0, i, 0)),
                   pl.BlockSpec((kv_heads, tr, HEAD_DIM), lambda i: (0, i, 0)),
                   pl.BlockSpec((kv_heads, tr, HEAD_DIM), lambda i: (0, i, 0))],
        out_shape=[jax.ShapeDtypeStruct((heads, t, HEAD_DIM), _BF16),
                   jax.ShapeDtypeStruct((kv_heads, t, HEAD_DIM), _BF16),
                   jax.ShapeDtypeStruct((kv_heads, t, HEAD_DIM), _BF16)],
        compiler_params=pltpu.CompilerParams(dimension_semantics=("parallel",)),
    )(proj, *tables, g_q, g_k)


def _qkv_post_backward(proj, dq, dk, dv, tables, g_q, g_k, l, tr, name):
    t = proj.shape[0]
    heads, kv_heads = dq.shape[0], dk.shape[0]
    aw, kw = heads * HEAD_DIM, kv_heads * HEAD_DIM
    w3 = aw + 2 * kw
    nbc = l // tr

    def body(p_ref, dq_ref, dk_ref, dv_ref, tc_ref, ta_ref, tb_ref, gq_ref, gk_ref, o_ref, dgq_ref, dgk_ref):
        i = pl.program_id(0)

        @pl.when(i == 0)
        def _():
            dgq_ref[...] = jnp.zeros_like(dgq_ref)
            dgk_ref[...] = jnp.zeros_like(dgk_ref)

        tabs = (tc_ref[...], ta_ref[...], tb_ref[...])
        latent = i >= nbc

        def back(col, dout, gain, dg_ref):
            xh = p_ref[:, col:col + HEAD_DIM]
            r = _rms(xh)
            xn = xh * r
            dy = _rope_transposed(dout, *tabs)
            dg_ref[0:1, :] += jnp.sum(dy * xn, axis=0, keepdims=True)
            dn = dy * gain
            o_ref[:, col:col + HEAD_DIM] = (r * (dn - xn * jnp.mean(dn * xn, axis=-1, keepdims=True))).astype(_BF16)

        for h in range(heads):
            back(h * HEAD_DIM, jnp.where(latent, dq_ref[h], 0.0), gq_ref[...], dgq_ref)
        for h in range(kv_heads):
            back(aw + h * HEAD_DIM, dk_ref[h], gk_ref[...], dgk_ref)
            o_ref[:, aw + kw + h * HEAD_DIM:aw + kw + (h + 1) * HEAD_DIM] = dv_ref[h].astype(_BF16)

    tab = pl.BlockSpec((tr, HEAD_DIM), lambda i: (i, 0))
    gain = pl.BlockSpec((1, HEAD_DIM), lambda i: (0, 0))
    acc = pl.BlockSpec((SUBLANES, HEAD_DIM), lambda i: (0, 0))
    return pl.pallas_call(
        body, name=name, grid=(t // tr,),
        in_specs=[pl.BlockSpec((tr, w3), lambda i: (i, 0)),
                  pl.BlockSpec((heads, tr, HEAD_DIM), lambda i: (0, jnp.maximum(i - nbc, 0), 0)),
                  pl.BlockSpec((kv_heads, tr, HEAD_DIM), lambda i: (0, i, 0)),
                  pl.BlockSpec((kv_heads, tr, HEAD_DIM), lambda i: (0, i, 0)),
                  tab, tab, tab, gain, gain],
        out_specs=[pl.BlockSpec((tr, w3), lambda i: (i, 0)), acc, acc],
        out_shape=[jax.ShapeDtypeStruct((t, w3), _BF16), jax.ShapeDtypeStruct((SUBLANES, HEAD_DIM), _F32),
                   jax.ShapeDtypeStruct((SUBLANES, HEAD_DIM), _F32)],
        compiler_params=pltpu.CompilerParams(dimension_semantics=("arbitrary",)),
    )(proj, dq, dk, dv, *tables, g_q, g_k)


def _attention(q, k, v, proj, l, mix, tq, name):
    heads, t, _ = q.shape
    kv_heads = k.shape[0]
    n = t - l
    rows = GQA_GROUP * tq
    gw = GQA_GROUP * HEAD_DIM
    aw = heads * HEAD_DIM
    gate_col = (aw + 2 * kv_heads * HEAD_DIM) // gw
    off = l // tq

    def body(q_ref, k_ref, v_ref, g_ref, o_ref, y_ref, lse_ref):
        lane = lax.broadcasted_iota(jnp.int32, (tq, LANES), 1)
        lse_blk = jnp.zeros((tq, LANES), _F32)
        for first in range(0, GQA_GROUP, ATTN_SUB_HEADS):
            qs = q_ref[first:first + ATTN_SUB_HEADS].reshape(ATTN_SUB_HEADS * tq, HEAD_DIM)
            raw = lax.dot_general(qs, k_ref[0], (((1,), (1,)), ((), ())), preferred_element_type=_F32)
            m = jnp.max(raw, axis=-1, keepdims=True)
            p = jnp.exp2((raw - m) * (ATTN_SCALE * LOG2_E))
            denom = jnp.sum(p, axis=-1, keepdims=True)
            os_ = jnp.dot(p.astype(_BF16), v_ref[0], preferred_element_type=_F32) / denom
            lse_s = m * ATTN_SCALE + jnp.log(denom)
            for j in range(ATTN_SUB_HEADS):
                g = first + j
                og = os_[j * tq:(j + 1) * tq]
                cols = slice(g * HEAD_DIM, (g + 1) * HEAD_DIM)
                o_ref[:, cols] = og
                y_ref[:, cols] = (og * _silu(g_ref[:, cols])).astype(_BF16)
                lse_blk = jnp.where(lane == g, lse_s[j * tq:(j + 1) * tq], lse_blk)
        lse_ref[0] = lse_blk

    return pl.pallas_call(
        body, name=name, grid=(kv_heads, n // tq),
        in_specs=[pl.BlockSpec((GQA_GROUP, tq, HEAD_DIM), lambda h, i: (h, i + off, 0)),
                  pl.BlockSpec((1, t, HEAD_DIM), lambda h, i: (h, 0, 0)),
                  pl.BlockSpec((1, t, HEAD_DIM), lambda h, i: (h, 0, 0)),
                  pl.BlockSpec((tq, gw), lambda h, i: (i + off, gate_col + h))],
        out_specs=[pl.BlockSpec((tq, gw), lambda h, i: (i, h)),
                   pl.BlockSpec((tq, gw), lambda h, i: (i, h)),
                   pl.BlockSpec((1, tq, LANES), lambda h, i: (h, i, 0))],
        out_shape=[jax.ShapeDtypeStruct((n, aw), _F32), jax.ShapeDtypeStruct((n, mix), _BF16),
                   jax.ShapeDtypeStruct((kv_heads, n, LANES), _F32)],
        compiler_params=pltpu.CompilerParams(dimension_semantics=("parallel", "parallel")),
    )(q, k, v, proj)


def _attention_backward(q, k, v, attn_o, dy, proj, lse, after, l, tq, name):
    heads, t, _ = q.shape
    kv_heads = k.shape[0]
    n = t - l
    rows = GQA_GROUP * tq
    gw = GQA_GROUP * HEAD_DIM
    aw = heads * HEAD_DIM
    gate_col = (aw + 2 * kv_heads * HEAD_DIM) // gw
    off = l // tq
    n_parts = 2 if t % (2 * BF16_ROWS) == 0 else 1
    part = t // n_parts

    def body(q_ref, k_ref, v_ref, o_ref, dy_ref, g_ref, lse_ref, after_ref, dq_ref, dg_ref, dk_ref, dv_ref):
        del after_ref

        @pl.when(pl.program_id(1) == 0)
        def _():
            dk_ref[...] = jnp.zeros_like(dk_ref)
            dv_ref[...] = jnp.zeros_like(dv_ref)

        q4 = q_ref[...].reshape(rows, HEAD_DIM)
        do_parts, delta_parts, lse_parts = [], [], []
        lse_blk = lse_ref[0]
        for g in range(GQA_GROUP):
            cols = slice(g * HEAD_DIM, (g + 1) * HEAD_DIM)
            gate, og, dyg = g_ref[:, cols], o_ref[:, cols], dy_ref[:, cols]
            dog = dyg * _silu(gate)
            dg_ref[:, cols] = (dyg * og * _silu_grad(gate)).astype(_BF16)
            do_parts.append(dog)
            delta_parts.append(jnp.sum(dog * og, axis=-1, keepdims=True))
            lse_parts.append(lse_blk[:, g:g + 1])
        do4 = jnp.concatenate(do_parts, axis=0).astype(_BF16)
        delta4 = jnp.concatenate(delta_parts, axis=0)
        lse4 = jnp.concatenate(lse_parts, axis=0)
        dq4 = jnp.zeros((rows, HEAD_DIM), _F32)
        for part_i in range(n_parts):
            keys = slice(part_i * part, (part_i + 1) * part)
            ks, vs = k_ref[0, keys, :], v_ref[0, keys, :]
            s = lax.dot_general(q4, ks, (((1,), (1,)), ((), ())), preferred_element_type=_F32) * ATTN_SCALE
            p = jnp.exp(s - lse4)
            dp = lax.dot_general(do4, vs, (((1,), (1,)), ((), ())), preferred_element_type=_F32)
            ds = (p * (dp - delta4) * ATTN_SCALE).astype(_BF16)
            dq4 = dq4 + jnp.dot(ds, ks, preferred_element_type=_F32)
            dk_ref[0, keys, :] += lax.dot_general(ds, q4, (((0,), (0,)), ((), ())), preferred_element_type=_F32)
            dv_ref[0, keys, :] += lax.dot_general(
                p.astype(_BF16), do4, (((0,), (0,)), ((), ())), preferred_element_type=_F32)
        dq_ref[...] = dq4.reshape(GQA_GROUP, tq, HEAD_DIM)

    kv_spec = pl.BlockSpec((1, t, HEAD_DIM), lambda h, i: (h, 0, 0))
    tok = pl.BlockSpec((tq, gw), lambda h, i: (i, h))
    return pl.pallas_call(
        body, name=name, grid=(kv_heads, n // tq),
        in_specs=[pl.BlockSpec((GQA_GROUP, tq, HEAD_DIM), lambda h, i: (h, i + off, 0)), kv_spec, kv_spec,
                  tok, tok, pl.BlockSpec((tq, gw), lambda h, i: (i + off, gate_col + h)),
                  pl.BlockSpec((1, tq, LANES), lambda h, i: (h, i, 0)),
                  pl.BlockSpec(after.shape, lambda h, i: (0, 0))],
        out_specs=[pl.BlockSpec((GQA_GROUP, tq, HEAD_DIM), lambda h, i: (h, i, 0)), tok, kv_spec, kv_spec],
        out_shape=[jax.ShapeDtypeStruct((heads, n, HEAD_DIM), _F32), jax.ShapeDtypeStruct((n, aw), _BF16),
                   jax.ShapeDtypeStruct((kv_heads, t, HEAD_DIM), _F32), jax.ShapeDtypeStruct((kv_heads, t, HEAD_DIM), _F32)],
        compiler_params=pltpu.CompilerParams(dimension_semantics=("parallel", "arbitrary")),
    )(q, k, v, attn_o, dy, proj, lse, after)


def _halo_specs(tp, width, col, row_off, total_rows):
    per = tp // POOL_HALO
    first = row_off // POOL_HALO
    last = total_rows // POOL_HALO - 1
    return [pl.BlockSpec((tp, width), lambda i: (i + row_off // tp, col)),
            pl.BlockSpec((POOL_HALO, width), lambda i: (jnp.maximum(first + i * per - 1, 0), col)),
            pl.BlockSpec((POOL_HALO, width), lambda i: (jnp.minimum(first + (i + 1) * per, last), col))]


def _with_halo(cur, prev, nxt, t0, n):
    tp = cur.shape[0]
    r8 = lax.broadcasted_iota(jnp.int32, (POOL_HALO, 1), 0)
    prev = jnp.where(t0 - POOL_HALO + r8 >= 0, prev, 0.0)
    nxt = jnp.where(t0 + tp + r8 < n, nxt, 0.0)
    return jnp.concatenate([prev, cur, nxt], axis=0)


def _shift_rows(a, s):
    return pltpu.roll(a, s % a.shape[0], 0)


def _window_sum(e, w, mirrored):
    a = e + _shift_rows(e, -1 if mirrored else 1)
    s = 1
    while 2 * s < w:
        a = _shift_rows(a, s) + _shift_rows(a, -s)
        s *= 2
    return a


def _window_count(t, w, n):
    half = w // 2
    return (jnp.minimum(t + half, n) - jnp.maximum(t - half, 0)).astype(_F32)


def _pool_forward(gi, proj, y, pool_w, pool_scale, l, heads, kv_heads, tp, name):
    t = proj.shape[0]
    n = t - l
    pg = pool_w.shape[-1]
    w = POOL_WINDOWS[gi]
    aw, kw = heads * HEAD_DIM, kv_heads * HEAD_DIM
    u_col = (2 * aw + 2 * kw) // pg + gi
    gate_col = (2 * aw + 2 * kw + len(POOL_WINDOWS) * pg) // pg + gi

    def body(u_ref, up_ref, un_ref, g_ref, w_ref, sc_ref, y_in_ref, y_ref, raw_ref, d_ref):
        del y_in_ref
        t0 = pl.program_id(0) * tp
        cur = u_ref[...]
        win = _window_sum(_with_halo(cur, up_ref[...], un_ref[...], t0, n), w, False)[POOL_HALO:POOL_HALO + tp]
        tok = t0 + lax.broadcasted_iota(jnp.int32, (tp, 1), 0)
        d = (win / _window_count(tok, w, n) - cur).astype(_BF16)
        raw = jnp.dot(d, w_ref[...].reshape(pg, pg), preferred_element_type=_F32)
        d_ref[...] = d
        raw_ref[...] = raw
        y_ref[...] = ((raw * sc_ref[...]) * _silu(g_ref[...])).astype(_BF16)

    blk = pl.BlockSpec((tp, pg), lambda i: (i, 0))
    return pl.pallas_call(
        body, name=name, grid=(n // tp,),
        in_specs=_halo_specs(tp, pg, u_col, l, t) + [
            pl.BlockSpec((tp, pg), lambda i: (i + l // tp, gate_col)),
            pl.BlockSpec((N_DEV, 1, pg // N_DEV, pg), lambda i: (0, gi, 0, 0)),
            pl.BlockSpec((1, pg), lambda i: (0, gi)), _ANY],
        out_specs=[pl.BlockSpec((tp, pg), lambda i: (i, aw // pg + gi)), blk, blk],
        out_shape=[jax.ShapeDtypeStruct(y.shape, y.dtype), jax.ShapeDtypeStruct((n, pg), _F32),
                   jax.ShapeDtypeStruct((n, pg), _BF16)],
        input_output_aliases={6: 0},
        compiler_params=pltpu.CompilerParams(dimension_semantics=("arbitrary",)),
    )(proj, proj, proj, proj, pool_w, pool_scale, y)


def _pool_backward_gate(gi, dy, proj, raw, pool_w, pool_scale, l, heads, kv_heads, tp, name):
    n, pg = raw.shape
    aw, kw = heads * HEAD_DIM, kv_heads * HEAD_DIM
    gate_col = (2 * aw + 2 * kw + len(POOL_WINDOWS) * pg) // pg + gi

    def body(dy_ref, g_ref, raw_ref, w_ref, sc_ref, dg_ref, dr_ref, dd_ref, ds_ref):
        @pl.when(pl.program_id(0) == 0)
        def _():
            ds_ref[...] = jnp.zeros_like(ds_ref)

        gate, rawv, dyv, scale = g_ref[...], raw_ref[...], dy_ref[...], sc_ref[...]
        dpool = dyv * _silu(gate)
        dg_ref[...] = (dyv * (rawv * scale) * _silu_grad(gate)).astype(_BF16)
        ds_ref[0:1, :] += jnp.sum(dpool * rawv, axis=0, keepdims=True)
        draw = (dpool * scale).astype(_BF16)
        dr_ref[...] = draw
        dd_ref[...] = lax.dot_general(
            draw, w_ref[...].reshape(pg, pg), (((1,), (1,)), ((), ())), preferred_element_type=_F32)

    blk = pl.BlockSpec((tp, pg), lambda i: (i, 0))
    return pl.pallas_call(
        body, name=name, grid=(n // tp,),
        in_specs=[pl.BlockSpec((tp, pg), lambda i: (i, aw // pg + gi)),
                  pl.BlockSpec((tp, pg), lambda i: (i + l // tp, gate_col)), blk,
                  pl.BlockSpec((N_DEV, 1, pg // N_DEV, pg), lambda i: (0, gi, 0, 0)),
                  pl.BlockSpec((1, pg), lambda i: (0, gi))],
        out_specs=[blk, blk, blk, pl.BlockSpec((SUBLANES, pg), lambda i: (0, 0))],
        out_shape=[jax.ShapeDtypeStruct((n, pg), _BF16), jax.ShapeDtypeStruct((n, pg), _BF16),
                   jax.ShapeDtypeStruct((n, pg), _F32), jax.ShapeDtypeStruct((SUBLANES, pg), _F32)],
        compiler_params=pltpu.CompilerParams(dimension_semantics=("arbitrary",)),
    )(dy, proj, raw, pool_w, pool_scale)


def _pool_backward_window(gi, dd, tp, name):
    n, pg = dd.shape
    w = POOL_WINDOWS[gi]

    def body(c_ref, p_ref, n_ref, du_ref):
        t0 = pl.program_id(0) * tp
        cur = c_ref[...]
        e = _with_halo(cur, p_ref[...], n_ref[...], t0, n)
        tok = t0 - POOL_HALO + lax.broadcasted_iota(jnp.int32, (tp + 2 * POOL_HALO, 1), 0)
        e = e / jnp.maximum(_window_count(tok, w, n), 1.0)
        du_ref[...] = (_window_sum(e, w, True)[POOL_HALO:POOL_HALO + tp] - cur).astype(_BF16)

    return pl.pallas_call(
        body, name=name, grid=(n // tp,),
        in_specs=_halo_specs(tp, pg, 0, 0, n), out_specs=pl.BlockSpec((tp, pg), lambda i: (i, 0)),
        out_shape=jax.ShapeDtypeStruct((n, pg), _BF16),
        compiler_params=pltpu.CompilerParams(dimension_semantics=("parallel",)),
    )(dd, dd, dd)


def _post(out, x, target, gate, g_post, tr, name):
    n, d = out.shape

    def body(o_ref, x_ref, t_ref, gate_ref, g_ref, dxn_ref, do_ref, dgate_ref, dg_ref, loss_ref):
        @pl.when(pl.program_id(0) == 0)
        def _():
            dgate_ref[...] = jnp.zeros_like(dgate_ref)
            dg_ref[...] = jnp.zeros_like(dg_ref)
            loss_ref[...] = jnp.zeros_like(loss_ref)

        ov = o_ref[...]
        r = _rms(ov)
        on = ov * r
        normed = on * g_ref[...]
        err = (x_ref[...] + gate_ref[...] * normed) - t_ref[...]
        loss_ref[...] += jnp.sum(err * err)
        dxn = err / d
        dxn_ref[...] = dxn
        dgate_ref[0:1, :] += jnp.sum(dxn * normed, axis=0, keepdims=True)
        dr = dxn * gate_ref[...]
        dg_ref[0:1, :] += jnp.sum(dr * on, axis=0, keepdims=True)
        dn = dr * g_ref[...]
        do_ref[...] = (r * (dn - on * jnp.mean(dn * on, axis=-1, keepdims=True))).astype(_BF16)

    blk = pl.BlockSpec((tr, d), lambda i: (i, 0))
    vec = pl.BlockSpec((1, d), lambda i: (0, 0))
    acc = pl.BlockSpec((SUBLANES, d), lambda i: (0, 0))
    return pl.pallas_call(
        body, name=name, grid=(n // tr,),
        in_specs=[blk, blk, blk, vec, vec],
        out_specs=[blk, blk, acc, acc, pl.BlockSpec((SUBLANES, LANES), lambda i: (0, 0))],
        out_shape=[jax.ShapeDtypeStruct((n, d), _F32), jax.ShapeDtypeStruct((n, d), _BF16),
                   jax.ShapeDtypeStruct((SUBLANES, d), _F32), jax.ShapeDtypeStruct((SUBLANES, d), _F32),
                   jax.ShapeDtypeStruct((SUBLANES, LANES), _F32)],
        compiler_params=pltpu.CompilerParams(dimension_semantics=("arbitrary",)),
    )(out, x, target, gate, g_post)


def _adam_sharded(slab_ids, grad, got, far, w, m, v, name):
    r, c = w.shape
    tr = _tile(r, max(BF16_ROWS, min(256, (1 << 18) // c)), BF16_ROWS)

    def body(ids_ref, own_ref, got_ref, far_ref, w_ref, m_ref, v_ref, g_ref, dl_ref, nm_ref, nv_ref):
        del ids_ref
        g = own_ref[0] + got_ref[0]
        for k in range(N_CHIPS - 1):
            g = g + far_ref[k].astype(_F32)
        delta, nm, nv = _adamw(w_ref[...], g, m_ref[...], v_ref[...])
        g_ref[...] = g
        dl_ref[...] = delta
        nm_ref[...] = nm
        nv_ref[...] = nv

    blk = pl.BlockSpec((tr, c), lambda i, ids: (i, 0))
    return pl.pallas_call(
        body, name=name,
        grid_spec=pltpu.PrefetchScalarGridSpec(
            num_scalar_prefetch=1, grid=(r // tr,),
            in_specs=[pl.BlockSpec((1, tr, c), lambda i, ids: (ids[0], i, 0)),
                      pl.BlockSpec((1, tr, c), lambda i, ids: (0, i, 0)),
                      pl.BlockSpec((N_CHIPS - 1, tr, c), lambda i, ids: (0, i, 0)), blk, blk, blk],
            out_specs=[blk] * 4),
        out_shape=[jax.ShapeDtypeStruct((r, c), _F32)] * 4,
        compiler_params=pltpu.CompilerParams(dimension_semantics=("parallel",)),
    )(slab_ids, grad, got, far, w, m, v)


def _adam_replicated(parts, extra, through_silu, w, m, v, name):
    def body(p_ref, e_ref, s_ref, w_ref, m_ref, v_ref, g_ref, dl_ref, nm_ref, nv_ref):
        total = p_ref[0] + e_ref[0]
        for dev in range(1, N_DEV):
            total = total + (p_ref[dev] + e_ref[dev])
        g = jnp.where(s_ref[...] > 0.5, total * _silu_grad(w_ref[...]), total)
        delta, nm, nv = _adamw(w_ref[...], g, m_ref[...], v_ref[...])
        g_ref[...] = g
        dl_ref[...] = delta
        nm_ref[...] = nm
        nv_ref[...] = nv

    return pl.pallas_call(
        body, name=name, in_specs=[_VMEM] * 6, out_specs=[_VMEM] * 4,
        out_shape=[jax.ShapeDtypeStruct(w.shape, _F32)] * 4,
    )(parts, extra, through_silu, w, m, v)


def _as_rows(vec):
    size = vec.shape[0]
    padded = -(-size // (SUBLANES * LANES)) * SUBLANES * LANES
    return jnp.pad(vec, (0, padded - size)).reshape(padded // LANES, LANES)


def kernel(x, c, ctx, c_ctx, w_ada, b_ada, norm_pre, norm_post, w_in, q_norm, k_norm, pool_w, pool_scale, w_out, loss_target, m_c_ctx, m_w_ada, m_b_ada, m_norm_pre, m_norm_post, m_w_in, m_q_norm, m_k_norm, m_pool_w, m_pool_scale, m_w_out, v_c_ctx, v_w_ada, v_b_ada, v_norm_pre, v_norm_post, v_w_in, v_q_norm, v_k_norm, v_pool_w, v_pool_scale, v_w_out):
    me = _dev_index(*_position())
    x2, ctx2, target = x[0], ctx[0], loss_target[0]
    n, d = x2.shape
    l = ctx2.shape[0]
    t = l + n
    aw = d // 2
    heads = aw // HEAD_DIM
    kv_heads = heads // GQA_GROUP
    kw = kv_heads * HEAD_DIM
    n_groups = len(POOL_WINDOWS)
    pg = (d - aw) // n_groups
    mix = d
    tr = _tile(l, 128, BF16_ROWS)
    tq = _tile(l, 128, BF16_ROWS)
    tp = _tile(l, 512, POOL_HALO)

    xi, yi, ci = _position()
    slab_ids = jnp.stack([_dev_index(*chip, ci) for chip in _chip_order(xi, yi)]).astype(jnp.int32)

    w_in_b = _cast_bf16(w_in[0], "cast_w_in")
    w_in_g = _gather_weights([w_in_b[None]], [2], "gather_w_in")[0][0]
    late = [_cast_into_slab(slab_ids, w_out[0], "cast_w_out"),
            _cast_into_slab(slab_ids, pool_w[0].reshape(-1, pg), "cast_pool_w")]
    flight_w = _gather_slabs_start(late, w_in_g, "gather_late_start")

    c_all = _all_gather_small(_as_rows(c[0]), "gather_c").reshape(N_DEV, -1)[:, :d]
    craw = jnp.concatenate([c_all, jnp.broadcast_to(c_ctx[None], (N_DEV, d))], axis=0)
    ada = _ada_forward(craw, w_ada[0], "ada_forward")
    ada_all = _all_gather_small(ada, "gather_ada")
    mod_all = ada_all.transpose(1, 0, 2).reshape(ada.shape[0], -1) + b_ada[0]
    mod = lax.dynamic_index_in_dim(mod_all, me, 0, keepdims=False)
    mod_c = mod_all[N_DEV]
    shift, scale, gate = mod[:d], mod[d:2 * d], mod[2 * d:]
    zeros6 = jnp.zeros((SUBLANES - 2, d), _F32)
    mods = jnp.stack([jnp.concatenate([mod_c[None, d:2 * d], mod_c[None, :d], zeros6], axis=0),
                      jnp.concatenate([scale[None], shift[None], zeros6], axis=0)])

    h_all = _prenorm(ctx2, x2, norm_pre, mods, tr, "prenorm")
    proj = _matmul(h_all, w_in_g, tm=1088, after=flight_w[-1], name="proj")
    tables = _rope_tables(l, n)
    q, k, v = _qkv_post(proj, tables, q_norm, k_norm, heads, kv_heads, tr, "qkv_post")
    attn_o, y, lse = _attention(q, k, v, proj, l, mix, tq, "attention")
    w_out_g8, pool_g8 = _gather_slabs_wait(*flight_w[:3], attn_o, "gather_late_wait")
    w_out_g = w_out_g8.reshape(mix, d)
    pool_g = pool_g8.reshape(N_DEV, n_groups, pg // N_DEV, pg)
    raws, ds = [], []
    for gi in range(n_groups):
        y, raw, dsave = _pool_forward(gi, proj, y, pool_g, pool_scale, l, heads, kv_heads, tp, f"pool_forward_{gi}")
        raws.append(raw)
        ds.append(dsave)
    out = _matmul(y, w_out_g, name="out_proj")
    dxn, dout, dgate8, dgpost8, loss8 = _post(out, x2, target, gate[None], norm_post, tr, "post")

    dy = _matmul(dout, w_out_g, tb=True, name="d_y")
    gw_out = _matmul(y, dout, ta=True, name="grad_w_out").reshape(N_DEV, mix // N_DEV, d)
    got_out = _exchange_sibling([gw_out], "exchange_sibling_w_out")[0]
    sum_out = _pre_add(slab_ids, gw_out, got_out, "pre_add_w_out")
    flight_out = _exchange_chips_start([sum_out], "exchange_chips_start_w_out")
    dq, dgate_attn, dk, dv = _attention_backward(
        q, k, v, attn_o, dy, proj, lse, flight_out[-1], l, tq, "attention_backward")
    dqkv, dgq8, dgk8 = _qkv_post_backward(proj, dq, dk, dv, tables, q_norm, k_norm, l, tr, "qkv_post_backward")
    dus, dgps, gpw, dps8 = [], [], [], []
    for gi in range(n_groups):
        dgp, draw, dd, dps = _pool_backward_gate(
            gi, dy, proj, raws[gi], pool_g, pool_scale, l, heads, kv_heads, tp, f"pool_backward_gate_{gi}")
        dus.append(_pool_backward_window(gi, dd, tp, f"pool_backward_window_{gi}"))
        dgps.append(dgp)
        dps8.append(dps)
        gpw.append(_matmul(ds[gi], draw, ta=True, name=f"grad_pool_w_{gi}"))
    latent_cols = jnp.concatenate([dgate_attn] + dus + dgps, axis=1)
    dproj = jnp.concatenate([dqkv, jnp.pad(latent_cols, ((l, 0), (0, 0)))], axis=1)
    gw_in = _matmul(h_all, dproj, ta=True, col_slabs=N_DEV, name="grad_w_in")
    pr = pool_w.shape[2]
    gpw8 = jnp.stack(gpw).reshape(n_groups, N_DEV, pr, pg).transpose(1, 0, 2, 3).reshape(N_DEV, n_groups * pr, pg)
    got_in, got_pw = _exchange_sibling([gw_in, gpw8], "exchange_sibling_w_in")
    sums_in = [_pre_add(slab_ids, gw_in, got_in, "pre_add_w_in"), _pre_add(slab_ids, gpw8, got_pw, "pre_add_pool_w")]
    flight_in = _exchange_chips_start(sums_in, "exchange_chips_start_w_in")
    dh = _matmul(dproj, w_in_g, tb=True, tm=1088, tk=3072, after=flight_in[-1], name="d_h")
    grad_x, dmods, dgpre8 = _prenorm_backward(dh, ctx2, x2, dxn, norm_pre, mods, tr, "prenorm_backward")

    dmod_lat = jnp.concatenate([dmods[1, 1], dmods[1, 0], dgate8[0]])
    dmod_ctx = jnp.concatenate([dmods[0, 1], dmods[0, 0], jnp.zeros((d,), _F32)])
    small = jnp.concatenate([dmod_lat, dmod_ctx, dgpre8[0], dgpost8[0], dgq8[0], dgk8[0]] + [p[0] for p in dps8]
                            + [loss8[0, :1]])
    gathered = _all_gather_small(_as_rows(small), "gather_small").reshape(N_DEV, -1)
    o = 0
    take = lambda size: (gathered[:, o:o + size], o + size)
    g_mod, o = take(3 * d)
    g_modc, o = take(3 * d)
    g_pre, o = take(d)
    g_post, o = take(d)
    g_q, o = take(HEAD_DIM)
    g_k, o = take(HEAD_DIM)
    g_ps, o = take(n_groups * pg)
    g_loss, o = take(1)
    cols = w_ada.shape[-1]
    mine = lambda a: lax.dynamic_slice_in_dim(a, me * cols, cols, axis=1)
    dmod_rows = jnp.concatenate([mine(g_mod), mine(g_modc)], axis=0)
    g_wada, dl_wada, nm_wada, nv_wada, dcact = _ada_backward(craw, dmod_rows, w_ada[0], m_w_ada[0], v_w_ada[0], "ada_backward")
    dcc = _all_gather_small(_as_rows(dcact[0]), "gather_dcc").reshape(N_DEV, -1)[:, :d]

    sizes = [d, 3 * d, d, d, HEAD_DIM, HEAD_DIM, n_groups * pg]
    pack = lambda parts: jnp.stack([_as_rows(jnp.concatenate([p[dev] for p in parts])) for dev in range(N_DEV)])
    zero = lambda size: jnp.zeros((N_DEV, size), _F32)
    parts = pack([dcc, g_mod, g_pre, g_post, g_q, g_k, g_ps])
    extra = pack([zero(d), g_modc, zero(d), zero(d), zero(HEAD_DIM), zero(HEAD_DIM), zero(n_groups * pg)])
    through_silu = _as_rows(jnp.concatenate([jnp.ones((d,), _F32), jnp.zeros((sum(sizes[1:]),), _F32)]))
    cat = lambda items: _as_rows(jnp.concatenate([a.reshape(-1) for a in items]))
    ws = [c_ctx, b_ada, norm_pre, norm_post, q_norm, k_norm, pool_scale]
    ms = [m_c_ctx, m_b_ada, m_norm_pre, m_norm_post, m_q_norm, m_k_norm, m_pool_scale]
    vs = [v_c_ctx, v_b_ada, v_norm_pre, v_norm_post, v_q_norm, v_k_norm, v_pool_scale]
    rep = _adam_replicated(parts, extra, through_silu, cat(ws), cat(ms), cat(vs), "adam_replicated")

    def split(packed):
        flat_, outs, at = packed.reshape(-1), [], 0
        for w, size in zip(ws, sizes):
            outs.append(flat_[at:at + size].reshape(w.shape))
            at += size
        return outs

    g_rep, dl_rep, nm_rep, nv_rep = [split(r) for r in rep]

    far_out = _exchange_chips_wait(*flight_out[:4], grad_x, "exchange_chips_wait_w_out")[0]
    far_in, far_pw = _exchange_chips_wait(*flight_in[:4], rep[0], "exchange_chips_wait_w_in")
    two = lambda a: a.reshape(-1, a.shape[-1])
    sharded = []
    for g, got, far, w, m, v_, name in zip(
            (gw_in, gw_out, gpw8), (got_in, got_out, got_pw), (far_in, far_out, far_pw), (w_in, w_out, pool_w),
            (m_w_in, m_w_out, m_pool_w), (v_w_in, v_w_out, v_pool_w), ("adam_w_in", "adam_w_out", "adam_pool_w")):
        res = _adam_sharded(slab_ids, g, got, far, two(w), two(m), two(v_), name)
        sharded.append([r.reshape(w.shape) for r in res])
    (g_w_in, dl_w_in, nm_w_in, nv_w_in), (g_w_out, dl_w_out, nm_w_out, nv_w_out), (g_pw, dl_pw, nm_pw, nv_pw) = sharded

    loss_sum = g_loss[0, 0]
    for dev in range(1, N_DEV):
        loss_sum = loss_sum + g_loss[dev, 0]
    loss = (0.5 / d) * loss_sum

    def ordered(rep_list, ada_, w_in_, pw_, w_out_):
        return [rep_list[0], ada_[None], rep_list[1], rep_list[2], rep_list[3], w_in_, rep_list[4], rep_list[5],
                pw_, rep_list[6], w_out_]

    return (loss, grad_x[None],
            *ordered(g_rep, g_wada, g_w_in, g_pw, g_w_out),
            *ordered(dl_rep, dl_wada, dl_w_in, dl_pw, dl_w_out),
            *ordered(nm_rep, nm_wada, nm_w_in, nm_pw, nm_w_out),
            *ordered(nv_rep, nv_wada, nv_w_in, nv_pw, nv_w_out))
```

```python
import functools

import jax
import jax.numpy as jnp
from jax import lax
from jax.experimental import pallas as pl
from jax.experimental.pallas import tpu as pltpu

HEAD_DIM = 128
GQA_GROUP = 4
ATTN_SUB_HEADS = 2
LOG2_E = 1.4426950408889634
GRID_W = 64
ROPE_PAIRS = HEAD_DIM // 4
ROPE_THETA = 10000.0
ATTN_SCALE = HEAD_DIM ** -0.5
EPS = 1e-6
POOL_WINDOWS = (2, 4, 8, 16)
POOL_HALO = 8
N_DEV = 8
N_CHIPS = 4
ADAM_LR = 0.001
ADAM_B1 = 0.9
ADAM_B2 = 0.999
ADAM_EPS = 1e-08
ADAM_WD = 0.01
ADAM_STEP = 10

LANES = 128
SUBLANES = 8
BF16_ROWS = 16

_MESH = pl.DeviceIdType.MESH
_ANY = pl.BlockSpec(memory_space=pl.ANY)
_VMEM = pl.BlockSpec(memory_space=pltpu.VMEM)
_HBM = pl.BlockSpec(memory_space=pltpu.HBM)
_SEM = pl.BlockSpec(memory_space=pltpu.SEMAPHORE)
_EFFECT = pltpu.SideEffectType.DATAFLOW_SIDE_EFFECTING
_F32 = jnp.float32
_BF16 = jnp.bfloat16


def _tile(dim, pref, align):
    t = min(pref, dim)
    t -= t % align
    while t >= align:
        if dim % t == 0:
            return t
        t -= align
    return dim


def _position():
    return lax.axis_index("x"), lax.axis_index("y"), lax.axis_index("c")


def _flip(v, bit):
    return 1 - v if bit else v


def _dev_index(x, y, c):
    return 4 * x + 2 * y + c


def _silu(g):
    return g * jax.nn.sigmoid(g)


def _silu_grad(g):
    s = jax.nn.sigmoid(g)
    return s * (1.0 + g * (1.0 - s))


def _adamw(w, g, m, v):
    m = ADAM_B1 * m + (1.0 - ADAM_B1) * g
    v = ADAM_B2 * v + (1.0 - ADAM_B2) * (g * g)
    m_hat = m / (1.0 - ADAM_B1 ** ADAM_STEP)
    v_hat = v / (1.0 - ADAM_B2 ** ADAM_STEP)
    delta = -ADAM_LR * (m_hat / (jnp.sqrt(v_hat) + ADAM_EPS) + ADAM_WD * w)
    return delta, m, v


def _all_gather_small(v, name):
    rows, cols = v.shape

    def body(v_ref, out_ref, send_sems, recv_sems):
        x, y, c = _position()
        me = _dev_index(x, y, c)
        out_ref[me] = v_ref[...]
        peers = [(_flip(x, k & 4), _flip(y, k & 2), _flip(c, k & 1)) for k in range(1, N_DEV)]

        def copy(k, block, to):
            return pltpu.make_async_remote_copy(
                src_ref=v_ref, dst_ref=out_ref.at[block], send_sem=send_sems.at[k], recv_sem=recv_sems.at[k],
                device_id=to, device_id_type=_MESH)

        sends = [copy(k, me, p) for k, p in enumerate(peers)]
        for s in sends:
            s.start()
        for k, p in enumerate(peers):
            copy(k, _dev_index(*p), p).wait_recv()
        for s in sends:
            s.wait_send()

    return pl.pallas_call(
        body, name=name,
        out_shape=jax.ShapeDtypeStruct((N_DEV, rows, cols), v.dtype),
        in_specs=[_VMEM], out_specs=_VMEM,
        scratch_shapes=[pltpu.SemaphoreType.DMA((N_DEV - 1,)), pltpu.SemaphoreType.DMA((N_DEV - 1,))],
    )(v)


def _window(ref, axis, size, j):
    start = pl.multiple_of(j * size, size)
    if axis == 0:
        return ref.at[pl.ds(start, size)]
    if axis == 1:
        return ref.at[:, pl.ds(start, size), :]
    return ref.at[:, :, pl.ds(start, size)]


def _gather_weights(shards, axes, name):
    n = len(shards)
    sizes = [s.shape[a] for s, a in zip(shards, axes)]
    out_shapes = [
        jax.ShapeDtypeStruct(tuple(d * N_DEV if i == a else d for i, d in enumerate(s.shape)), s.dtype)
        for s, a in zip(shards, axes)]

    def body(*refs):
        srcs, outs = refs[:n], refs[n:2 * n]
        send_sems, recv_sems, local_sems = refs[2 * n:]
        x, y, c = _position()
        me, sibling = (x, y, c), (x, y, 1 - c)
        chips = [(1 - x, y), (x, 1 - y), (1 - x, 1 - y)]
        firsts, passed, locals_ = [], [], []
        for a in range(n):
            def rows(block, a=a):
                return _window(outs[a], axes[a], sizes[a], _dev_index(*block))

            def copy(k, block, to, src=None, a=a, rows=rows):
                return pltpu.make_async_remote_copy(
                    src_ref=rows(block) if src is None else src, dst_ref=rows(block),
                    send_sem=send_sems.at[7 * a + k], recv_sem=recv_sems.at[7 * a + k],
                    device_id=to, device_id_type=_MESH)

            mine = pltpu.make_async_copy(srcs[a], rows(me), local_sems.at[a])
            mine.start()
            locals_.append(mine)
            first = [copy(0, me, sibling, src=srcs[a])]
            first += [copy(1 + j, me, (*chip, c), src=srcs[a]) for j, chip in enumerate(chips)]
            for cp in first:
                cp.start()
            firsts.append((first, copy))
        for a in range(n):
            first, copy = firsts[a]
            fwd = [copy(4 + j, (*chip, c), sibling) for j, chip in enumerate(chips)]
            for j, chip in enumerate(chips):
                copy(1 + j, (*chip, c), me).wait_recv()
                fwd[j].start()
            passed.append(fwd)
        for a in range(n):
            first, copy = firsts[a]
            copy(0, sibling, me).wait_recv()
            for j, chip in enumerate(chips):
                copy(4 + j, (*chip, 1 - c), me).wait_recv()
            for cp in first + passed[a]:
                cp.wait_send()
            locals_[a].wait()

    return pl.pallas_call(
        body, name=name, out_shape=out_shapes,
        in_specs=[_ANY] * n, out_specs=[_ANY] * n,
        scratch_shapes=[pltpu.SemaphoreType.DMA((7 * n,)), pltpu.SemaphoreType.DMA((7 * n,)),
                        pltpu.SemaphoreType.DMA((n,))],
    )(*shards)


def _slab_copies(bufs, send_sems, recv_sems):
    x, y, c = _position()
    me = _dev_index(x, y, c)
    peers = [(_flip(x, k & 4), _flip(y, k & 2), _flip(c, k & 1)) for k in range(1, N_DEV)]
    return [pltpu.make_async_remote_copy(
        src_ref=buf.at[me], dst_ref=buf.at[me],
        send_sem=send_sems.at[(N_DEV - 1) * a + k], recv_sem=recv_sems.at[(N_DEV - 1) * a + k],
        device_id=peer, device_id_type=_MESH)
        for a, buf in enumerate(bufs) for k, peer in enumerate(peers)]


def _gather_slabs_start(bufs, after, name):
    n = len(bufs)
    n_copies = (N_DEV - 1) * n

    def body(*refs):
        send_sems, recv_sems, token = refs[n + 1], refs[n + 2], refs[-1]
        for cp in _slab_copies(refs[:n], send_sems, recv_sems):
            cp.start()
        token[...] = jnp.zeros_like(token)

    outs = pl.pallas_call(
        body, name=name,
        out_shape=(pltpu.SemaphoreType.DMA((n_copies,)), pltpu.SemaphoreType.DMA((n_copies,)),
                   *[pltpu.HBM(b.shape, b.dtype) for b in bufs], jax.ShapeDtypeStruct((SUBLANES, LANES), _F32)),
        in_specs=[_HBM] * n + [_ANY], out_specs=(_SEM, _SEM, *[_HBM] * n, _VMEM),
        input_output_aliases={i: 2 + i for i in range(n)},
        compiler_params=pltpu.CompilerParams(has_side_effects=_EFFECT),
    )(*[pltpu.with_memory_space_constraint(b, pltpu.HBM) for b in bufs], after)
    return outs[0], outs[1], list(outs[2:2 + n]), outs[-1]


def _gather_slabs_wait(send_sems, recv_sems, bufs, after, name):
    n = len(bufs)

    def body(*refs):
        for cp in _slab_copies(refs[:n], refs[n], refs[n + 1]):
            cp.wait_send()
            cp.wait_recv()

    outs = pl.pallas_call(
        body, name=name, out_shape=tuple(pltpu.HBM(b.shape, b.dtype) for b in bufs),
        in_specs=[_HBM] * n + [_SEM, _SEM, _ANY], out_specs=[_HBM] * n,
        input_output_aliases={i: i for i in range(n)},
        compiler_params=pltpu.CompilerParams(has_side_effects=_EFFECT),
    )(*bufs, send_sems, recv_sems, after)
    return list(outs)


def _chip_order(x, y):
    return [(x, y), (1 - x, y), (x, 1 - y), (1 - x, 1 - y)]


def _exchange_sibling(grads, name):
    n = len(grads)
    out_shapes = [jax.ShapeDtypeStruct((N_CHIPS,) + g.shape[1:], g.dtype) for g in grads]

    def body(*refs):
        srcs, gots = refs[:n], refs[n:2 * n]
        send_sems, recv_sems = refs[2 * n:]
        x, y, c = _position()
        sends = []
        for a in range(n):
            for s, chip in enumerate(_chip_order(x, y)):
                k = N_CHIPS * a + s
                give = pltpu.make_async_remote_copy(
                    src_ref=srcs[a].at[_dev_index(*chip, 1 - c)], dst_ref=gots[a].at[s],
                    send_sem=send_sems.at[k], recv_sem=recv_sems.at[k], device_id=(x, y, 1 - c), device_id_type=_MESH)
                give.start()
                sends.append(give)
        for cp in sends:
            cp.wait_recv()
        for cp in sends:
            cp.wait_send()

    return pl.pallas_call(
        body, name=name, out_shape=out_shapes,
        in_specs=[_ANY] * n, out_specs=[_ANY] * n,
        scratch_shapes=[pltpu.SemaphoreType.DMA((N_CHIPS * n,)), pltpu.SemaphoreType.DMA((N_CHIPS * n,))],
    )(*grads)


def _chip_copies(srcs, lands, send_sems, recv_sems):
    x, y, c = _position()
    return [pltpu.make_async_remote_copy(
        src_ref=srcs[a].at[k], dst_ref=lands[a].at[k],
        send_sem=send_sems.at[(N_CHIPS - 1) * a + k], recv_sem=recv_sems.at[(N_CHIPS - 1) * a + k],
        device_id=(*chip, c), device_id_type=_MESH)
        for a in range(len(srcs)) for k, chip in enumerate(_chip_order(x, y)[1:])]


def _sibling_copies(srcs, lands, send_sems, recv_sems):
    x, y, c = _position()
    return [pltpu.make_async_remote_copy(
        src_ref=srcs[a].at[s], dst_ref=lands[a].at[s],
        send_sem=send_sems.at[N_CHIPS * a + s], recv_sem=recv_sems.at[N_CHIPS * a + s],
        device_id=(x, y, 1 - c), device_id_type=_MESH)
        for a in range(len(srcs)) for s in range(N_CHIPS)]


def _exchange_start(copies, sums, name):
    n = len(sums)
    n_copies = sum(s.shape[0] for s in sums)

    def body(*refs):
        srcs, lands = refs[:n], refs[n:2 * n]
        send_sems, recv_sems, token = refs[2 * n], refs[2 * n + 1], refs[-1]
        for cp in copies(srcs, lands, send_sems, recv_sems):
            cp.start()
        token[...] = jnp.zeros_like(token)

    hbm = [pltpu.HBM(s.shape, s.dtype) for s in sums]
    outs = pl.pallas_call(
        body, name=name,
        out_shape=(pltpu.SemaphoreType.DMA((n_copies,)), pltpu.SemaphoreType.DMA((n_copies,)), *hbm, *hbm,
                   jax.ShapeDtypeStruct((SUBLANES, LANES), _F32)),
        in_specs=[_HBM] * (2 * n), out_specs=(_SEM, _SEM, *[_HBM] * (2 * n), _VMEM),
        input_output_aliases={i: 2 + i for i in range(2 * n)},
        compiler_params=pltpu.CompilerParams(has_side_effects=_EFFECT),
    )(*[pltpu.with_memory_space_constraint(s, pltpu.HBM) for s in sums],
      *[pltpu.with_memory_space_constraint(lax.empty(s.shape, s.dtype), pltpu.HBM) for s in sums])
    return outs[0], outs[1], list(outs[2:2 + n]), list(outs[2 + n:2 + 2 * n]), outs[-1]


def _exchange_wait(copies, send_sems, recv_sems, srcs, lands, after, name):
    n = len(srcs)

    def body(*refs):
        for cp in copies(refs[:n], refs[n:2 * n], refs[2 * n], refs[2 * n + 1]):
            cp.wait_send()
            cp.wait_recv()

    hbm = [pltpu.HBM(s.shape, s.dtype) for s in srcs]
    outs = pl.pallas_call(
        body, name=name, out_shape=(*hbm, *hbm),
        in_specs=[_HBM] * (2 * n) + [_SEM, _SEM, _ANY], out_specs=[_HBM] * (2 * n),
        input_output_aliases={i: i for i in range(2 * n)},
        compiler_params=pltpu.CompilerParams(has_side_effects=_EFFECT),
    )(*srcs, *lands, send_sems, recv_sems, after)
    return list(outs[n:])


def _matmul(a, b, *, ta=False, tb=False, out_dtype=_F32, tm=1024, tn=1024, tk=4608, col_slabs=None, after=None, name):
    kdim, m = a.shape if ta else a.shape[::-1]
    n = b.shape[0] if tb else b.shape[1]
    tm = _tile(m, tm, LANES if ta else BF16_ROWS)
    tn = n // col_slabs if col_slabs else _tile(n, tn, LANES)
    tk = _tile(kdim, tk, BF16_ROWS if ta else LANES)
    nk = kdim // tk
    dims = (((0 if ta else 1,), (1 if tb else 0,)), ((), ()))

    def body_whole_k(a_ref, b_ref, *rest):
        o_ref = rest[-1]
        part = lax.dot_general(a_ref[...], b_ref[...], dims, preferred_element_type=_F32)
        o_ref[...] = part.astype(out_dtype).reshape(o_ref.shape)

    def body_split_k(a_ref, b_ref, *rest):
        o_ref, acc_ref = rest[-2:]
        k = pl.program_id(2)

        @pl.when(k == 0)
        def _():
            acc_ref[...] = jnp.zeros_like(acc_ref)

        acc_ref[...] += lax.dot_general(a_ref[...], b_ref[...], dims, preferred_element_type=_F32)

        @pl.when(k == nk - 1)
        def _():
            o_ref[...] = acc_ref[...].astype(out_dtype).reshape(o_ref.shape)

    a_spec = pl.BlockSpec((tk, tm), lambda i, j, k: (k, i)) if ta else pl.BlockSpec((tm, tk), lambda i, j, k: (i, k))
    b_spec = pl.BlockSpec((tn, tk), lambda i, j, k: (j, k)) if tb else pl.BlockSpec((tk, tn), lambda i, j, k: (k, j))
    if col_slabs:
        out_spec = pl.BlockSpec((1, tm, tn), lambda i, j, k: (j, i, 0))
        out_shape = jax.ShapeDtypeStruct((col_slabs, m, tn), out_dtype)
    else:
        out_spec = pl.BlockSpec((tm, tn), lambda i, j, k: (i, j))
        out_shape = jax.ShapeDtypeStruct((m, n), out_dtype)
    extra = [] if after is None else [after]
    return pl.pallas_call(
        body_whole_k if nk == 1 else body_split_k, name=name, grid=(m // tm, n // tn, nk),
        in_specs=[a_spec, b_spec] + [pl.BlockSpec(t.shape, lambda i, j, k: (0, 0)) for t in extra],
        out_specs=out_spec, out_shape=out_shape,
        scratch_shapes=[] if nk == 1 else [pltpu.VMEM((tm, tn), _F32)],
        compiler_params=pltpu.CompilerParams(dimension_semantics=("parallel", "parallel", "arbitrary")),
    )(a, b, *extra)


def _matmul_slabs(a, b, ids, width, name, after=None):
    kdim, m = a.shape
    n_slabs = ids.shape[0]
    tm = _tile(m, 1024, LANES)

    def body(ids_ref, a_ref, b_ref, *rest):
        del ids_ref
        rest[-1][0] = lax.dot_general(a_ref[...], b_ref[...], (((0,), (0,)), ((), ())), preferred_element_type=_F32)

    extra = [] if after is None else [after]
    return pl.pallas_call(
        body, name=name,
        grid_spec=pltpu.PrefetchScalarGridSpec(
            num_scalar_prefetch=1, grid=(m // tm, n_slabs),
            in_specs=[pl.BlockSpec((kdim, tm), lambda i, j, ids: (0, i)),
                      pl.BlockSpec((kdim, width), lambda i, j, ids: (0, ids[j]))]
            + [pl.BlockSpec(t.shape, lambda i, j, ids: (0, 0)) for t in extra],
            out_specs=pl.BlockSpec((1, tm, width), lambda i, j, ids: (j, i, 0))),
        out_shape=jax.ShapeDtypeStruct((n_slabs, m, width), _F32),
        compiler_params=pltpu.CompilerParams(dimension_semantics=("parallel", "parallel")),
    )(ids, a, b, *extra)


def _row_tile(rows, cols):
    return _tile(rows, max(BF16_ROWS, min(512, (1 << 19) // cols)), BF16_ROWS)


def _cast_bf16(a, name):
    r, c = a.shape
    tr = _row_tile(r, c)

    def body(a_ref, o_ref):
        o_ref[...] = a_ref[...].astype(_BF16)

    blk = pl.BlockSpec((tr, c), lambda i: (i, 0))
    return pl.pallas_call(
        body, name=name, grid=(r // tr,), in_specs=[blk], out_specs=blk,
        out_shape=jax.ShapeDtypeStruct((r, c), _BF16),
        compiler_params=pltpu.CompilerParams(dimension_semantics=("parallel",)),
    )(a)


def _cast_into_slab(slab_ids, a, name):
    r, c = a.shape
    tr = _row_tile(r, c)

    def body(ids_ref, a_ref, o_ref):
        del ids_ref
        o_ref[0] = a_ref[...].astype(_BF16)

    return pl.pallas_call(
        body, name=name,
        grid_spec=pltpu.PrefetchScalarGridSpec(
            num_scalar_prefetch=1, grid=(r // tr,),
            in_specs=[pl.BlockSpec((tr, c), lambda i, ids: (i, 0))],
            out_specs=pl.BlockSpec((1, tr, c), lambda i, ids: (ids[0], i, 0))),
        out_shape=jax.ShapeDtypeStruct((N_DEV, r, c), _BF16),
        compiler_params=pltpu.CompilerParams(dimension_semantics=("parallel",)),
    )(slab_ids, a)


def _pre_add(slab_ids, grad, got, name):
    _, r, c = grad.shape
    tr = _row_tile(r, c)

    def body(ids_ref, a_ref, b_ref, o_ref):
        del ids_ref
        o_ref[...] = (a_ref[...] + b_ref[...]).astype(_BF16)

    return pl.pallas_call(
        body, name=name,
        grid_spec=pltpu.PrefetchScalarGridSpec(
            num_scalar_prefetch=1, grid=(N_CHIPS - 1, r // tr),
            in_specs=[pl.BlockSpec((1, tr, c), lambda s, i, ids: (ids[s + 1], i, 0)),
                      pl.BlockSpec((1, tr, c), lambda s, i, ids: (s + 1, i, 0))],
            out_specs=pl.BlockSpec((1, tr, c), lambda s, i, ids: (s, i, 0))),
        out_shape=jax.ShapeDtypeStruct((N_CHIPS - 1, r, c), _BF16),
        compiler_params=pltpu.CompilerParams(dimension_semantics=("parallel", "parallel")),
    )(slab_ids, grad, got)


def _ada_forward(craw, w_shard, name):
    d, cols = w_shard.shape
    tk = _tile(d, 512, LANES)

    def body(c_ref, w_ref, o_ref):
        @pl.when(pl.program_id(0) == 0)
        def _():
            o_ref[...] = jnp.zeros_like(o_ref)

        o_ref[...] += jnp.dot(_silu(c_ref[...]).astype(_BF16), w_ref[...].astype(_BF16), preferred_element_type=_F32)

    return pl.pallas_call(
        body, name=name, grid=(d // tk,),
        in_specs=[pl.BlockSpec((craw.shape[0], tk), lambda k: (0, k)), pl.BlockSpec((tk, cols), lambda k: (k, 0))],
        out_specs=pl.BlockSpec((craw.shape[0], cols), lambda k: (0, 0)),
        out_shape=jax.ShapeDtypeStruct((craw.shape[0], cols), _F32),
        compiler_params=pltpu.CompilerParams(dimension_semantics=("arbitrary",)),
    )(craw, w_shard)


def _ada_backward(craw, dmod, w, m, v, name):
    d, cols = w.shape
    rows = craw.shape[0]
    tr = _tile(d, 256, LANES)

    def body(c_ref, dm_ref, w_ref, m_ref, v_ref, g_ref, dl_ref, nm_ref, nv_ref, dc_ref):
        act = _silu(c_ref[...]).astype(_BF16)
        dmb = dm_ref[...].astype(_BF16)
        wv = w_ref[...]
        g = lax.dot_general(act, dmb, (((0,), (0,)), ((), ())), preferred_element_type=_F32)
        delta, nm, nv = _adamw(wv, g, m_ref[...], v_ref[...])
        g_ref[...] = g
        dl_ref[...] = delta
        nm_ref[...] = nm
        nv_ref[...] = nv
        dc = lax.dot_general(dmb, wv.astype(_BF16), (((1,), (1,)), ((), ())), preferred_element_type=_F32)
        dc_ref[...] = jnp.broadcast_to(jnp.sum(dc[N_DEV:], axis=0, keepdims=True), dc_ref.shape)

    blk = pl.BlockSpec((tr, cols), lambda i: (i, 0))
    return pl.pallas_call(
        body, name=name, grid=(d // tr,),
        in_specs=[pl.BlockSpec((rows, tr), lambda i: (0, i)), pl.BlockSpec((rows, cols), lambda i: (0, 0)), blk, blk, blk],
        out_specs=[blk, blk, blk, blk, pl.BlockSpec((SUBLANES, tr), lambda i: (0, i))],
        out_shape=[jax.ShapeDtypeStruct((d, cols), _F32)] * 4 + [jax.ShapeDtypeStruct((SUBLANES, d), _F32)],
        compiler_params=pltpu.CompilerParams(dimension_semantics=("parallel",)),
    )(craw, dmod, w, m, v)


def _rms(xf):
    return lax.rsqrt(jnp.mean(xf * xf, axis=-1, keepdims=True) + EPS)


def _prenorm(ctx, x, g_pre, mods, tr, name):
    l, d = ctx.shape
    n = x.shape[0]
    nbc = l // tr

    def body(ctx_ref, x_ref, g_ref, mod_ref, h_ref):
        def emit(src_ref):
            xf = src_ref[...]
            y = (xf * _rms(xf)) * g_ref[...]
            h_ref[...] = (y * (1.0 + mod_ref[0, 0:1, :]) + mod_ref[0, 1:2, :]).astype(_BF16)

        is_ctx = pl.program_id(0) < nbc
        pl.when(is_ctx)(lambda: emit(ctx_ref))
        pl.when(jnp.logical_not(is_ctx))(lambda: emit(x_ref))

    return pl.pallas_call(
        body, name=name, grid=((l + n) // tr,),
        in_specs=[pl.BlockSpec((tr, d), lambda i: (jnp.minimum(i, nbc - 1), 0)),
                  pl.BlockSpec((tr, d), lambda i: (jnp.maximum(i - nbc, 0), 0)),
                  pl.BlockSpec((1, d), lambda i: (0, 0)),
                  pl.BlockSpec((1, SUBLANES, d), lambda i: ((i >= nbc).astype(jnp.int32), 0, 0))],
        out_specs=pl.BlockSpec((tr, d), lambda i: (i, 0)),
        out_shape=jax.ShapeDtypeStruct((l + n, d), _BF16),
        compiler_params=pltpu.CompilerParams(dimension_semantics=("arbitrary",)),
    )(ctx, x, g_pre, mods)


def _prenorm_backward(dh, ctx, x, dxn, g_pre, mods, tr, name):
    l, d = ctx.shape
    n = x.shape[0]
    nbc = l // tr

    def body(dh_ref, ctx_ref, x_ref, dxn_ref, g_ref, mod_ref, gx_ref, dmod_ref, dg_ref):
        i = pl.program_id(0)

        @pl.when(i == 0)
        def _():
            dg_ref[...] = jnp.zeros_like(dg_ref)

        @pl.when(jnp.logical_or(i == 0, i == nbc))
        def _():
            dmod_ref[...] = jnp.zeros_like(dmod_ref)

        def emit(src_ref, latent):
            xf = src_ref[...]
            r = _rms(xf)
            xn = xf * r
            dhv = dh_ref[...]
            one_scale = 1.0 + mod_ref[0, 0:1, :]
            dmod_ref[0, 0:1, :] += jnp.sum(dhv * (xn * g_ref[...]), axis=0, keepdims=True)
            dmod_ref[0, 1:2, :] += jnp.sum(dhv, axis=0, keepdims=True)
            dyg = dhv * one_scale
            dg_ref[0:1, :] += jnp.sum(dyg * xn, axis=0, keepdims=True)
            if latent:
                dn = dyg * g_ref[...]
                gx_ref[...] = dxn_ref[...] + r * (dn - xn * jnp.mean(dn * xn, axis=-1, keepdims=True))

        pl.when(i < nbc)(lambda: emit(ctx_ref, False))
        pl.when(i >= nbc)(lambda: emit(x_ref, True))

    lat = pl.BlockSpec((tr, d), lambda i: (jnp.maximum(i - nbc, 0), 0))
    sel = pl.BlockSpec((1, SUBLANES, d), lambda i: ((i >= nbc).astype(jnp.int32), 0, 0))
    return pl.pallas_call(
        body, name=name, grid=((l + n) // tr,),
        in_specs=[pl.BlockSpec((tr, d), lambda i: (i, 0)),
                  pl.BlockSpec((tr, d), lambda i: (jnp.minimum(i, nbc - 1), 0)),
                  lat, lat, pl.BlockSpec((1, d), lambda i: (0, 0)), sel],
        out_specs=[lat, sel, pl.BlockSpec((SUBLANES, d), lambda i: (0, 0))],
        out_shape=[jax.ShapeDtypeStruct((n, d), _F32), jax.ShapeDtypeStruct((2, SUBLANES, d), _F32),
                   jax.ShapeDtypeStruct((SUBLANES, d), _F32)],
        compiler_params=pltpu.CompilerParams(dimension_semantics=("arbitrary",)),
    )(dh, ctx, x, dxn, g_pre, mods)


def _rope_tables(l, n):
    rows = n // GRID_W
    row = jnp.repeat(jnp.arange(rows, dtype=_F32), GRID_W)
    col = jnp.tile(jnp.arange(GRID_W, dtype=_F32), rows)
    inv = ROPE_THETA ** (-jnp.arange(ROPE_PAIRS, dtype=_F32) / ROPE_PAIRS)
    ang_r, ang_c = row[:, None] * inv, col[:, None] * inv
    cr, sr, cc, sc = jnp.cos(ang_r), jnp.sin(ang_r), jnp.cos(ang_c), jnp.sin(ang_c)
    zero = jnp.zeros_like(sr)
    tc = jnp.concatenate([cr, cr, cc, cc], axis=-1)
    ta = jnp.concatenate([-sr, zero, -sc, zero], axis=-1)
    tb = jnp.concatenate([zero, sr, zero, sc], axis=-1)
    pad = lambda t, fill: jnp.concatenate([jnp.full((l, HEAD_DIM), fill, _F32), t], axis=0)
    return pad(tc, 1.0), pad(ta, 0.0), pad(tb, 0.0)


def _rope(y, tc, ta, tb):
    return y * tc + pltpu.roll(y, HEAD_DIM - ROPE_PAIRS, 1) * ta + pltpu.roll(y, ROPE_PAIRS, 1) * tb


def _rope_transposed(dy, tc, ta, tb):
    return dy * tc + pltpu.roll(dy * ta, ROPE_PAIRS, 1) + pltpu.roll(dy * tb, HEAD_DIM - ROPE_PAIRS, 1)


def _qkv_post(proj, tables, g_q, g_k, heads, kv_heads, tr, name):
    t = proj.shape[0]
    aw, kw = heads * HEAD_DIM, kv_heads * HEAD_DIM
    w3 = aw + 2 * kw

    def body(p_ref, tc_ref, ta_ref, tb_ref, gq_ref, gk_ref, q_ref, k_ref, v_ref):
        tabs = (tc_ref[...], ta_ref[...], tb_ref[...])

        def norm_rope(col, gain):
            xh = p_ref[:, col:col + HEAD_DIM]
            return _rope((xh * _rms(xh)) * gain, *tabs).astype(_BF16)

        for h in range(heads):
            q_ref[h] = norm_rope(h * HEAD_DIM, gq_ref[...])
        for h in range(kv_heads):
            k_ref[h] = norm_rope(aw + h * HEAD_DIM, gk_ref[...])
            v_ref[h] = p_ref[:, aw + kw + h * HEAD_DIM:aw + kw + (h + 1) * HEAD_DIM].astype(_BF16)

    tab = pl.BlockSpec((tr, HEAD_DIM), lambda i: (i, 0))
    gain = pl.BlockSpec((1, HEAD_DIM), lambda i: (0, 0))
    return pl.pallas_call(
        body, name=name, grid=(t // tr,),
        in_specs=[pl.BlockSpec((tr, w3), lambda i: (i, 0)), tab, tab, tab, gain, gain],
        out_specs=[pl.BlockSpec((heads, tr, HEAD_DIM), lambda i: (0, i, 0)),
                   pl.BlockSpec((kv_heads, tr, HEAD_DIM), lambda i: (0, i, 0)),
                   pl.BlockSpec((kv_heads, tr, HEAD_DIM), lambda i: (0, i, 0))],
        out_shape=[jax.ShapeDtypeStruct((heads, t, HEAD_DIM), _BF16),
                   jax.ShapeDtypeStruct((kv_heads, t, HEAD_DIM), _BF16),
                   jax.ShapeDtypeStruct((kv_heads, t, HEAD_DIM), _BF16)],
        compiler_params=pltpu.CompilerParams(dimension_semantics=("parallel",)),
    )(proj, *tables, g_q, g_k)


def _qkv_post_backward(proj, dq, dk, dv, tables, g_q, g_k, l, tr, name):
    t = proj.shape[0]
    heads, kv_heads = dq.shape[0], dk.shape[0]
    aw, kw = heads * HEAD_DIM, kv_heads * HEAD_DIM
    w3 = aw + 2 * kw
    nbc = l // tr

    def body(p_ref, dq_ref, dk_ref, dv_ref, tc_ref, ta_ref, tb_ref, gq_ref, gk_ref, o_ref, dgq_ref, dgk_ref):
        i = pl.program_id(0)

        @pl.when(i == 0)
        def _():
            dgq_ref[...] = jnp.zeros_like(dgq_ref)
            dgk_ref[...] = jnp.zeros_like(dgk_ref)

        tabs = (tc_ref[...], ta_ref[...], tb_ref[...])
        latent = i >= nbc

        def back(col, dout, gain, dg_ref):
            xh = p_ref[:, col:col + HEAD_DIM]
            r = _rms(xh)
            xn = xh * r
            dy = _rope_transposed(dout, *tabs)
            dg_ref[0:1, :] += jnp.sum(dy * xn, axis=0, keepdims=True)
            dn = dy * gain
            o_ref[:, col:col + HEAD_DIM] = (r * (dn - xn * jnp.mean(dn * xn, axis=-1, keepdims=True))).astype(_BF16)

        for h in range(heads):
            back(h * HEAD_DIM, jnp.where(latent, dq_ref[h], 0.0), gq_ref[...], dgq_ref)
        for h in range(kv_heads):
            back(aw + h * HEAD_DIM, dk_ref[h], gk_ref[...], dgk_ref)
            o_ref[:, aw + kw + h * HEAD_DIM:aw + kw + (h + 1) * HEAD_DIM] = dv_ref[h].astype(_BF16)

    tab = pl.BlockSpec((tr, HEAD_DIM), lambda i: (i, 0))
    gain = pl.BlockSpec((1, HEAD_DIM), lambda i: (0, 0))
    acc = pl.BlockSpec((SUBLANES, HEAD_DIM), lambda i: (0, 0))
    return pl.pallas_call(
        body, name=name, grid=(t // tr,),
        in_specs=[pl.BlockSpec((tr, w3), lambda i: (i, 0)),
                  pl.BlockSpec((heads, tr, HEAD_DIM), lambda i: (0, jnp.maximum(i - nbc, 0), 0)),
                  pl.BlockSpec((kv_heads, tr, HEAD_DIM), lambda i: (0, i, 0)),
                  pl.BlockSpec((kv_heads, tr, HEAD_DIM), lambda i: (0, i, 0)),
                  tab, tab, tab, gain, gain],
        out_specs=[pl.BlockSpec((tr, w3), lambda i: (i, 0)), acc, acc],
        out_shape=[jax.ShapeDtypeStruct((t, w3), _BF16), jax.ShapeDtypeStruct((SUBLANES, HEAD_DIM), _F32),
                   jax.ShapeDtypeStruct((SUBLANES, HEAD_DIM), _F32)],
        compiler_params=pltpu.CompilerParams(dimension_semantics=("arbitrary",)),
    )(proj, dq, dk, dv, *tables, g_q, g_k)


def _attention(q, k, v, proj, l, mix, tq, name):
    heads, t, _ = q.shape
    kv_heads = k.shape[0]
    n = t - l
    rows = GQA_GROUP * tq
    gw = GQA_GROUP * HEAD_DIM
    aw = heads * HEAD_DIM
    gate_col = (aw + 2 * kv_heads * HEAD_DIM) // gw
    off = l // tq

    def body(q_ref, k_ref, v_ref, g_ref, o_ref, y_ref, lse_ref):
        lane = lax.broadcasted_iota(jnp.int32, (tq, LANES), 1)
        lse_blk = jnp.zeros((tq, LANES), _F32)
        for first in range(0, GQA_GROUP, ATTN_SUB_HEADS):
            qs = q_ref[first:first + ATTN_SUB_HEADS].reshape(ATTN_SUB_HEADS * tq, HEAD_DIM)
            raw = lax.dot_general(qs, k_ref[0], (((1,), (1,)), ((), ())), preferred_element_type=_F32)
            m = jnp.max(raw, axis=-1, keepdims=True)
            p = jnp.exp2((raw - m) * (ATTN_SCALE * LOG2_E))
            denom = jnp.sum(p, axis=-1, keepdims=True)
            os_ = jnp.dot(p.astype(_BF16), v_ref[0], preferred_element_type=_F32) / denom
            lse_s = m * ATTN_SCALE + jnp.log(denom)
            for j in range(ATTN_SUB_HEADS):
                g = first + j
                og = os_[j * tq:(j + 1) * tq]
                cols = slice(g * HEAD_DIM, (g + 1) * HEAD_DIM)
                o_ref[:, cols] = og
                y_ref[:, cols] = (og * _silu(g_ref[:, cols])).astype(_BF16)
                lse_blk = jnp.where(lane == g, lse_s[j * tq:(j + 1) * tq], lse_blk)
        lse_ref[0] = lse_blk

    return pl.pallas_call(
        body, name=name, grid=(kv_heads, n // tq),
        in_specs=[pl.BlockSpec((GQA_GROUP, tq, HEAD_DIM), lambda h, i: (h, i + off, 0)),
                  pl.BlockSpec((1, t, HEAD_DIM), lambda h, i: (h, 0, 0)),
                  pl.BlockSpec((1, t, HEAD_DIM), lambda h, i: (h, 0, 0)),
                  pl.BlockSpec((tq, gw), lambda h, i: (i + off, gate_col + h))],
        out_specs=[pl.BlockSpec((tq, gw), lambda h, i: (i, h)),
                   pl.BlockSpec((tq, gw), lambda h, i: (i, h)),
                   pl.BlockSpec((1, tq, LANES), lambda h, i: (h, i, 0))],
        out_shape=[jax.ShapeDtypeStruct((n, aw), _F32), jax.ShapeDtypeStruct((n, mix), _BF16),
                   jax.ShapeDtypeStruct((kv_heads, n, LANES), _F32)],
        compiler_params=pltpu.CompilerParams(dimension_semantics=("parallel", "parallel")),
    )(q, k, v, proj)


def _attention_backward(q, k, v, attn_o, dy, proj, lse, after, l, tq, name):
    heads, t, _ = q.shape
    kv_heads = k.shape[0]
    n = t - l
    rows = GQA_GROUP * tq
    gw = GQA_GROUP * HEAD_DIM
    aw = heads * HEAD_DIM
    gate_col = (aw + 2 * kv_heads * HEAD_DIM) // gw
    off = l // tq
    n_parts = 2 if t % (2 * BF16_ROWS) == 0 else 1
    part = t // n_parts

    def body(q_ref, k_ref, v_ref, o_ref, dy_ref, g_ref, lse_ref, after_ref, dq_ref, dg_ref, dk_ref, dv_ref):
        del after_ref

        @pl.when(pl.program_id(1) == 0)
        def _():
            dk_ref[...] = jnp.zeros_like(dk_ref)
            dv_ref[...] = jnp.zeros_like(dv_ref)

        q4 = q_ref[...].reshape(rows, HEAD_DIM)
        do_parts, delta_parts, lse_parts = [], [], []
        lse_blk = lse_ref[0]
        for g in range(GQA_GROUP):
            cols = slice(g * HEAD_DIM, (g + 1) * HEAD_DIM)
            gate, og, dyg = g_ref[:, cols], o_ref[:, cols], dy_ref[:, cols]
            dog = dyg * _silu(gate)
            dg_ref[:, cols] = (dyg * og * _silu_grad(gate)).astype(_BF16)
            do_parts.append(dog)
            delta_parts.append(jnp.sum(dog * og, axis=-1, keepdims=True))
            lse_parts.append(lse_blk[:, g:g + 1])
        do4 = jnp.concatenate(do_parts, axis=0).astype(_BF16)
        delta4 = jnp.concatenate(delta_parts, axis=0)
        lse4 = jnp.concatenate(lse_parts, axis=0)
        dq4 = jnp.zeros((rows, HEAD_DIM), _F32)
        for part_i in range(n_parts):
            keys = slice(part_i * part, (part_i + 1) * part)
            ks, vs = k_ref[0, keys, :], v_ref[0, keys, :]
            s = lax.dot_general(q4, ks, (((1,), (1,)), ((), ())), preferred_element_type=_F32) * ATTN_SCALE
            p = jnp.exp(s - lse4)
            dp = lax.dot_general(do4, vs, (((1,), (1,)), ((), ())), preferred_element_type=_F32)
            ds = (p * (dp - delta4) * ATTN_SCALE).astype(_BF16)
            dq4 = dq4 + jnp.dot(ds, ks, preferred_element_type=_F32)
            dk_ref[0, keys, :] += lax.dot_general(ds, q4, (((0,), (0,)), ((), ())), preferred_element_type=_F32)
            dv_ref[0, keys, :] += lax.dot_general(
                p.astype(_BF16), do4, (((0,), (0,)), ((), ())), preferred_element_type=_F32)
        dq_ref[...] = dq4.reshape(GQA_GROUP, tq, HEAD_DIM)

    kv_spec = pl.BlockSpec((1, t, HEAD_DIM), lambda h, i: (h, 0, 0))
    tok = pl.BlockSpec((tq, gw), lambda h, i: (i, h))
    return pl.pallas_call(
        body, name=name, grid=(kv_heads, n // tq),
        in_specs=[pl.BlockSpec((GQA_GROUP, tq, HEAD_DIM), lambda h, i: (h, i + off, 0)), kv_spec, kv_spec,
                  tok, tok, pl.BlockSpec((tq, gw), lambda h, i: (i + off, gate_col + h)),
                  pl.BlockSpec((1, tq, LANES), lambda h, i: (h, i, 0)),
                  pl.BlockSpec(after.shape, lambda h, i: (0, 0))],
        out_specs=[pl.BlockSpec((GQA_GROUP, tq, HEAD_DIM), lambda h, i: (h, i, 0)), tok, kv_spec, kv_spec],
        out_shape=[jax.ShapeDtypeStruct((heads, n, HEAD_DIM), _F32), jax.ShapeDtypeStruct((n, aw), _BF16),
                   jax.ShapeDtypeStruct((kv_heads, t, HEAD_DIM), _F32), jax.ShapeDtypeStruct((kv_heads, t, HEAD_DIM), _F32)],
        compiler_params=pltpu.CompilerParams(dimension_semantics=("parallel", "arbitrary")),
    )(q, k, v, attn_o, dy, proj, lse, after)


def _halo_specs(tp, width, col, row_off, total_rows):
    per = tp // POOL_HALO
    first = row_off // POOL_HALO
    last = total_rows // POOL_HALO - 1
    return [pl.BlockSpec((tp, width), lambda i: (i + row_off // tp, col)),
            pl.BlockSpec((POOL_HALO, width), lambda i: (jnp.maximum(first + i * per - 1, 0), col)),
            pl.BlockSpec((POOL_HALO, width), lambda i: (jnp.minimum(first + (i + 1) * per, last), col))]


def _with_halo(cur, prev, nxt, t0, n):
    tp = cur.shape[0]
    r8 = lax.broadcasted_iota(jnp.int32, (POOL_HALO, 1), 0)
    prev = jnp.where(t0 - POOL_HALO + r8 >= 0, prev, 0.0)
    nxt = jnp.where(t0 + tp + r8 < n, nxt, 0.0)
    return jnp.concatenate([prev, cur, nxt], axis=0)


def _shift_rows(a, s):
    return pltpu.roll(a, s % a.shape[0], 0)


def _window_sum(e, w, mirrored):
    a = e + _shift_rows(e, -1 if mirrored else 1)
    s = 1
    while 2 * s < w:
        a = _shift_rows(a, s) + _shift_rows(a, -s)
        s *= 2
    return a


def _window_count(t, w, n):
    half = w // 2
    return (jnp.minimum(t + half, n) - jnp.maximum(t - half, 0)).astype(_F32)


def _pool_forward(gi, proj, y, pool_w, pool_scale, l, heads, kv_heads, tp, name):
    t = proj.shape[0]
    n = t - l
    pg = pool_w.shape[-1]
    w = POOL_WINDOWS[gi]
    aw, kw = heads * HEAD_DIM, kv_heads * HEAD_DIM
    u_col = (2 * aw + 2 * kw) // pg + gi
    gate_col = (2 * aw + 2 * kw + len(POOL_WINDOWS) * pg) // pg + gi

    def body(u_ref, up_ref, un_ref, g_ref, w_ref, sc_ref, y_in_ref, y_ref, raw_ref, d_ref):
        del y_in_ref
        t0 = pl.program_id(0) * tp
        cur = u_ref[...]
        win = _window_sum(_with_halo(cur, up_ref[...], un_ref[...], t0, n), w, False)[POOL_HALO:POOL_HALO + tp]
        tok = t0 + lax.broadcasted_iota(jnp.int32, (tp, 1), 0)
        d = (win / _window_count(tok, w, n) - cur).astype(_BF16)
        raw = jnp.dot(d, w_ref[...].reshape(pg, pg), preferred_element_type=_F32)
        d_ref[...] = d
        raw_ref[...] = raw
        y_ref[...] = ((raw * sc_ref[...]) * _silu(g_ref[...])).astype(_BF16)

    blk = pl.BlockSpec((tp, pg), lambda i: (i, 0))
    return pl.pallas_call(
        body, name=name, grid=(n // tp,),
        in_specs=_halo_specs(tp, pg, u_col, l, t) + [
            pl.BlockSpec((tp, pg), lambda i: (i + l // tp, gate_col)),
            pl.BlockSpec((N_DEV, 1, pg // N_DEV, pg), lambda i: (0, gi, 0, 0)),
            pl.BlockSpec((1, pg), lambda i: (0, gi)), _ANY],
        out_specs=[pl.BlockSpec((tp, pg), lambda i: (i, aw // pg + gi)), blk, blk],
        out_shape=[jax.ShapeDtypeStruct(y.shape, y.dtype), jax.ShapeDtypeStruct((n, pg), _F32),
                   jax.ShapeDtypeStruct((n, pg), _BF16)],
        input_output_aliases={6: 0},
        compiler_params=pltpu.CompilerParams(dimension_semantics=("arbitrary",)),
    )(proj, proj, proj, proj, pool_w, pool_scale, y)


def _pool_backward_gate(gi, dy, proj, raw, pool_w, pool_scale, l, heads, kv_heads, tp, name):
    n, pg = raw.shape
    aw, kw = heads * HEAD_DIM, kv_heads * HEAD_DIM
    gate_col = (2 * aw + 2 * kw + len(POOL_WINDOWS) * pg) // pg + gi

    def body(dy_ref, g_ref, raw_ref, w_ref, sc_ref, dg_ref, dr_ref, dd_ref, ds_ref):
        @pl.when(pl.program_id(0) == 0)
        def _():
            ds_ref[...] = jnp.zeros_like(ds_ref)

        gate, rawv, dyv, scale = g_ref[...], raw_ref[...], dy_ref[...], sc_ref[...]
        dpool = dyv * _silu(gate)
        dg_ref[...] = (dyv * (rawv * scale) * _silu_grad(gate)).astype(_BF16)
        ds_ref[0:1, :] += jnp.sum(dpool * rawv, axis=0, keepdims=True)
        draw = (dpool * scale).astype(_BF16)
        dr_ref[...] = draw
        dd_ref[...] = lax.dot_general(
            draw, w_ref[...].reshape(pg, pg), (((1,), (1,)), ((), ())), preferred_element_type=_F32)

    blk = pl.BlockSpec((tp, pg), lambda i: (i, 0))
    return pl.pallas_call(
        body, name=name, grid=(n // tp,),
        in_specs=[pl.BlockSpec((tp, pg), lambda i: (i, aw // pg + gi)),
                  pl.BlockSpec((tp, pg), lambda i: (i + l // tp, gate_col)), blk,
                  pl.BlockSpec((N_DEV, 1, pg // N_DEV, pg), lambda i: (0, gi, 0, 0)),
                  pl.BlockSpec((1, pg), lambda i: (0, gi))],
        out_specs=[blk, blk, blk, pl.BlockSpec((SUBLANES, pg), lambda i: (0, 0))],
        out_shape=[jax.ShapeDtypeStruct((n, pg), _BF16), jax.ShapeDtypeStruct((n, pg), _BF16),
                   jax.ShapeDtypeStruct((n, pg), _F32), jax.ShapeDtypeStruct((SUBLANES, pg), _F32)],
        compiler_params=pltpu.CompilerParams(dimension_semantics=("arbitrary",)),
    )(dy, proj, raw, pool_w, pool_scale)


def _pool_backward_window(gi, dd, tp, name):
    n, pg = dd.shape
    w = POOL_WINDOWS[gi]

    def body(c_ref, p_ref, n_ref, du_ref):
        t0 = pl.program_id(0) * tp
        cur = c_ref[...]
        e = _with_halo(cur, p_ref[...], n_ref[...], t0, n)
        tok = t0 - POOL_HALO + lax.broadcasted_iota(jnp.int32, (tp + 2 * POOL_HALO, 1), 0)
        e = e / jnp.maximum(_window_count(tok, w, n), 1.0)
        du_ref[...] = (_window_sum(e, w, True)[POOL_HALO:POOL_HALO + tp] - cur).astype(_BF16)

    return pl.pallas_call(
        body, name=name, grid=(n // tp,),
        in_specs=_halo_specs(tp, pg, 0, 0, n), out_specs=pl.BlockSpec((tp, pg), lambda i: (i, 0)),
        out_shape=jax.ShapeDtypeStruct((n, pg), _BF16),
        compiler_params=pltpu.CompilerParams(dimension_semantics=("parallel",)),
    )(dd, dd, dd)


def _post(out, x, target, gate, g_post, tr, name):
    n, d = out.shape

    def body(o_ref, x_ref, t_ref, gate_ref, g_ref, dxn_ref, do_ref, dgate_ref, dg_ref, loss_ref):
        @pl.when(pl.program_id(0) == 0)
        def _():
            dgate_ref[...] = jnp.zeros_like(dgate_ref)
            dg_ref[...] = jnp.zeros_like(dg_ref)
            loss_ref[...] = jnp.zeros_like(loss_ref)

        ov = o_ref[...]
        r = _rms(ov)
        on = ov * r
        normed = on * g_ref[...]
        err = (x_ref[...] + gate_ref[...] * normed) - t_ref[...]
        loss_ref[...] += jnp.sum(err * err)
        dxn = err / d
        dxn_ref[...] = dxn
        dgate_ref[0:1, :] += jnp.sum(dxn * normed, axis=0, keepdims=True)
        dr = dxn * gate_ref[...]
        dg_ref[0:1, :] += jnp.sum(dr * on, axis=0, keepdims=True)
        dn = dr * g_ref[...]
        do_ref[...] = (r * (dn - on * jnp.mean(dn * on, axis=-1, keepdims=True))).astype(_BF16)

    blk = pl.BlockSpec((tr, d), lambda i: (i, 0))
    vec = pl.BlockSpec((1, d), lambda i: (0, 0))
    acc = pl.BlockSpec((SUBLANES, d), lambda i: (0, 0))
    return pl.pallas_call(
        body, name=name, grid=(n // tr,),
        in_specs=[blk, blk, blk, vec, vec],
        out_specs=[blk, blk, acc, acc, pl.BlockSpec((SUBLANES, LANES), lambda i: (0, 0))],
        out_shape=[jax.ShapeDtypeStruct((n, d), _F32), jax.ShapeDtypeStruct((n, d), _BF16),
                   jax.ShapeDtypeStruct((SUBLANES, d), _F32), jax.ShapeDtypeStruct((SUBLANES, d), _F32),
                   jax.ShapeDtypeStruct((SUBLANES, LANES), _F32)],
        compiler_params=pltpu.CompilerParams(dimension_semantics=("arbitrary",)),
    )(out, x, target, gate, g_post)


def _adam_sharded(slab_ids, grad, got, far, w, m, v, name):
    r, c = w.shape
    tr = _tile(r, max(BF16_ROWS, min(256, (1 << 18) // c)), BF16_ROWS)

    def body(ids_ref, own_ref, got_ref, far_ref, w_ref, m_ref, v_ref, g_ref, dl_ref, nm_ref, nv_ref):
        del ids_ref
        g = own_ref[0] + got_ref[0]
        for k in range(N_CHIPS - 1):
            g = g + far_ref[k].astype(_F32)
        delta, nm, nv = _adamw(w_ref[...], g, m_ref[...], v_ref[...])
        g_ref[...] = g
        dl_ref[...] = delta
        nm_ref[...] = nm
        nv_ref[...] = nv

    blk = pl.BlockSpec((tr, c), lambda i, ids: (i, 0))
    return pl.pallas_call(
        body, name=name,
        grid_spec=pltpu.PrefetchScalarGridSpec(
            num_scalar_prefetch=1, grid=(r // tr,),
            in_specs=[pl.BlockSpec((1, tr, c), lambda i, ids: (ids[0], i, 0)),
                      pl.BlockSpec((1, tr, c), lambda i, ids: (0, i, 0)),
                      pl.BlockSpec((N_CHIPS - 1, tr, c), lambda i, ids: (0, i, 0)), blk, blk, blk],
            out_specs=[blk] * 4),
        out_shape=[jax.ShapeDtypeStruct((r, c), _F32)] * 4,
        compiler_params=pltpu.CompilerParams(dimension_semantics=("parallel",)),
    )(slab_ids, grad, got, far, w, m, v)


def _adam_replicated(parts, extra, through_silu, w, m, v, name):
    def body(p_ref, e_ref, s_ref, w_ref, m_ref, v_ref, g_ref, dl_ref, nm_ref, nv_ref):
        total = p_ref[0] + e_ref[0]
        for dev in range(1, N_DEV):
            total = total + (p_ref[dev] + e_ref[dev])
        g = jnp.where(s_ref[...] > 0.5, total * _silu_grad(w_ref[...]), total)
        delta, nm, nv = _adamw(w_ref[...], g, m_ref[...], v_ref[...])
        g_ref[...] = g
        dl_ref[...] = delta
        nm_ref[...] = nm
        nv_ref[...] = nv

    return pl.pallas_call(
        body, name=name, in_specs=[_VMEM] * 6, out_specs=[_VMEM] * 4,
        out_shape=[jax.ShapeDtypeStruct(w.shape, _F32)] * 4,
    )(parts, extra, through_silu, w, m, v)


def _as_rows(vec):
    size = vec.shape[0]
    padded = -(-size // (SUBLANES * LANES)) * SUBLANES * LANES
    return jnp.pad(vec, (0, padded - size)).reshape(padded // LANES, LANES)


def kernel(x, c, ctx, c_ctx, w_ada, b_ada, norm_pre, norm_post, w_in, q_norm, k_norm, pool_w, pool_scale, w_out, loss_target, m_c_ctx, m_w_ada, m_b_ada, m_norm_pre, m_norm_post, m_w_in, m_q_norm, m_k_norm, m_pool_w, m_pool_scale, m_w_out, v_c_ctx, v_w_ada, v_b_ada, v_norm_pre, v_norm_post, v_w_in, v_q_norm, v_k_norm, v_pool_w, v_pool_scale, v_w_out):
    me = _dev_index(*_position())
    x2, ctx2, target = x[0], ctx[0], loss_target[0]
    n, d = x2.shape
    l = ctx2.shape[0]
    t = l + n
    aw = d // 2
    heads = aw // HEAD_DIM
    kv_heads = heads // GQA_GROUP
    kw = kv_heads * HEAD_DIM
    n_groups = len(POOL_WINDOWS)
    pg = (d - aw) // n_groups
    mix = d
    tr = _tile(l, 128, BF16_ROWS)
    tq = _tile(l, 128, BF16_ROWS)
    tp = _tile(l, 512, POOL_HALO)

    xi, yi, ci = _position()
    slab_ids = jnp.stack([_dev_index(*chip, ci) for chip in _chip_order(xi, yi)]).astype(jnp.int32)

    w_in_b = _cast_bf16(w_in[0], "cast_w_in")
    w_in_g = _gather_weights([w_in_b[None]], [2], "gather_w_in")[0][0]
    late = [_cast_into_slab(slab_ids, w_out[0], "cast_w_out"),
            _cast_into_slab(slab_ids, pool_w[0].reshape(-1, pg), "cast_pool_w")]
    flight_w = _gather_slabs_start(late, w_in_g, "gather_late_start")

    c_all = _all_gather_small(_as_rows(c[0]), "gather_c").reshape(N_DEV, -1)[:, :d]
    craw = jnp.concatenate([c_all, jnp.broadcast_to(c_ctx[None], (N_DEV, d))], axis=0)
    ada = _ada_forward(craw, w_ada[0], "ada_forward")
    ada_all = _all_gather_small(ada, "gather_ada")
    mod_all = ada_all.transpose(1, 0, 2).reshape(ada.shape[0], -1) + b_ada[0]
    mod = lax.dynamic_index_in_dim(mod_all, me, 0, keepdims=False)
    mod_c = mod_all[N_DEV]
    shift, scale, gate = mod[:d], mod[d:2 * d], mod[2 * d:]
    zeros6 = jnp.zeros((SUBLANES - 2, d), _F32)
    mods = jnp.stack([jnp.concatenate([mod_c[None, d:2 * d], mod_c[None, :d], zeros6], axis=0),
                      jnp.concatenate([scale[None], shift[None], zeros6], axis=0)])

    h_all = _prenorm(ctx2, x2, norm_pre, mods, tr, "prenorm")
    proj = _matmul(h_all, w_in_g, tm=1088, after=flight_w[-1], name="proj")
    tables = _rope_tables(l, n)
    q, k, v = _qkv_post(proj, tables, q_norm, k_norm, heads, kv_heads, tr, "qkv_post")
    attn_o, y, lse = _attention(q, k, v, proj, l, mix, tq, "attention")
    w_out_g8, pool_g8 = _gather_slabs_wait(*flight_w[:3], attn_o, "gather_late_wait")
    w_out_g = w_out_g8.reshape(mix, d)
    pool_g = pool_g8.reshape(N_DEV, n_groups, pg // N_DEV, pg)
    raws, ds = [], []
    for gi in range(n_groups):
        y, raw, dsave = _pool_forward(gi, proj, y, pool_g, pool_scale, l, heads, kv_heads, tp, f"pool_forward_{gi}")
        raws.append(raw)
        ds.append(dsave)
    out = _matmul(y, w_out_g, name="out_proj")
    dxn, dout, dgate8, dgpost8, loss8 = _post(out, x2, target, gate[None], norm_post, tr, "post")

    dy = _matmul(dout, w_out_g, tb=True, name="d_y")
    gw_out = _matmul(y, dout, ta=True, name="grad_w_out").reshape(N_DEV, mix // N_DEV, d)
    got_out = _exchange_sibling([gw_out], "exchange_sibling_w_out")[0]
    sum_out = _pre_add(slab_ids, gw_out, got_out, "pre_add_w_out")
    flight_out = _exchange_start(_chip_copies, [sum_out], "exchange_chips_start_w_out")
    dq, dgate_attn, dk, dv = _attention_backward(
        q, k, v, attn_o, dy, proj, lse, flight_out[-1], l, tq, "attention_backward")
    dqkv, dgq8, dgk8 = _qkv_post_backward(proj, dq, dk, dv, tables, q_norm, k_norm, l, tr, "qkv_post_backward")
    dus, dgps, gpw, dps8 = [], [], [], []
    for gi in range(n_groups):
        dgp, draw, dd, dps = _pool_backward_gate(
            gi, dy, proj, raws[gi], pool_g, pool_scale, l, heads, kv_heads, tp, f"pool_backward_gate_{gi}")
        dus.append(_pool_backward_window(gi, dd, tp, f"pool_backward_window_{gi}"))
        dgps.append(dgp)
        dps8.append(dps)
        gpw.append(_matmul(ds[gi], draw, ta=True, name=f"grad_pool_w_{gi}"))
    latent_cols = jnp.concatenate([dgate_attn] + dus + dgps, axis=1)
    dproj = jnp.concatenate([dqkv, jnp.pad(latent_cols, ((l, 0), (0, 0)))], axis=1)
    cw = w_in.shape[-1]
    other_ids = jnp.stack([_dev_index(*chip, 1 - ci) for chip in _chip_order(xi, yi)]).astype(jnp.int32)
    chip_slabs = jnp.arange(N_CHIPS, dtype=jnp.int32)
    pr = pool_w.shape[2]
    gpw8 = jnp.stack(gpw).reshape(n_groups, N_DEV, pr, pg).transpose(1, 0, 2, 3).reshape(N_DEV, n_groups * pr, pg)
    give_in = _matmul_slabs(h_all, dproj, other_ids, cw, "grad_w_in_sibling")
    flight_sib = _exchange_start(_sibling_copies, [give_in, jnp.take(gpw8, other_ids, axis=0)], "exchange_sibling_start")
    gw_in = _matmul_slabs(h_all, dproj, slab_ids, cw, "grad_w_in_own", after=flight_sib[-1])
    gpw_own = jnp.take(gpw8, slab_ids, axis=0)
    got_in, got_pw = _exchange_wait(_sibling_copies, *flight_sib[:4], gw_in, "exchange_sibling_wait")
    sums_in = [_pre_add(chip_slabs, gw_in, got_in, "pre_add_w_in"), _pre_add(chip_slabs, gpw_own, got_pw, "pre_add_pool_w")]
    flight_in = _exchange_start(_chip_copies, sums_in, "exchange_chips_start_w_in")
    dh = _matmul(dproj, w_in_g, tb=True, tm=1088, tk=3072, after=flight_in[-1], name="d_h")
    grad_x, dmods, dgpre8 = _prenorm_backward(dh, ctx2, x2, dxn, norm_pre, mods, tr, "prenorm_backward")

    dmod_lat = jnp.concatenate([dmods[1, 1], dmods[1, 0], dgate8[0]])
    dmod_ctx = jnp.concatenate([dmods[0, 1], dmods[0, 0], jnp.zeros((d,), _F32)])
    small = jnp.concatenate([dmod_lat, dmod_ctx, dgpre8[0], dgpost8[0], dgq8[0], dgk8[0]] + [p[0] for p in dps8]
                            + [loss8[0, :1]])
    gathered = _all_gather_small(_as_rows(small), "gather_small").reshape(N_DEV, -1)
    o = 0
    take = lambda size: (gathered[:, o:o + size], o + size)
    g_mod, o = take(3 * d)
    g_modc, o = take(3 * d)
    g_pre, o = take(d)
    g_post, o = take(d)
    g_q, o = take(HEAD_DIM)
    g_k, o = take(HEAD_DIM)
    g_ps, o = take(n_groups * pg)
    g_loss, o = take(1)
    cols = w_ada.shape[-1]
    mine = lambda a: lax.dynamic_slice_in_dim(a, me * cols, cols, axis=1)
    dmod_rows = jnp.concatenate([mine(g_mod), mine(g_modc)], axis=0)
    g_wada, dl_wada, nm_wada, nv_wada, dcact = _ada_backward(craw, dmod_rows, w_ada[0], m_w_ada[0], v_w_ada[0], "ada_backward")
    dcc = _all_gather_small(_as_rows(dcact[0]), "gather_dcc").reshape(N_DEV, -1)[:, :d]

    sizes = [d, 3 * d, d, d, HEAD_DIM, HEAD_DIM, n_groups * pg]
    def pack(parts):
        rows = jnp.concatenate(parts, axis=1)
        padded = -(-rows.shape[1] // (SUBLANES * LANES)) * SUBLANES * LANES
        return jnp.pad(rows, ((0, 0), (0, padded - rows.shape[1]))).reshape(N_DEV, padded // LANES, LANES)

    zero = lambda size: jnp.zeros((N_DEV, size), _F32)
    parts = pack([dcc, g_mod, g_pre, g_post, g_q, g_k, g_ps])
    extra = pack([zero(d), g_modc, zero(d), zero(d), zero(HEAD_DIM), zero(HEAD_DIM), zero(n_groups * pg)])
    through_silu = _as_rows(jnp.concatenate([jnp.ones((d,), _F32), jnp.zeros((sum(sizes[1:]),), _F32)]))
    cat = lambda items: _as_rows(jnp.concatenate([a.reshape(-1) for a in items]))
    ws = [c_ctx, b_ada, norm_pre, norm_post, q_norm, k_norm, pool_scale]
    ms = [m_c_ctx, m_b_ada, m_norm_pre, m_norm_post, m_q_norm, m_k_norm, m_pool_scale]
    vs = [v_c_ctx, v_b_ada, v_norm_pre, v_norm_post, v_q_norm, v_k_norm, v_pool_scale]
    rep = _adam_replicated(parts, extra, through_silu, cat(ws), cat(ms), cat(vs), "adam_replicated")

    def split(packed):
        flat_, outs, at = packed.reshape(-1), [], 0
        for w, size in zip(ws, sizes):
            outs.append(flat_[at:at + size].reshape(w.shape))
            at += size
        return outs

    g_rep, dl_rep, nm_rep, nv_rep = [split(r) for r in rep]

    far_out = _exchange_wait(_chip_copies, *flight_out[:4], grad_x, "exchange_chips_wait_w_out")[0]
    far_in, far_pw = _exchange_wait(_chip_copies, *flight_in[:4], rep[0], "exchange_chips_wait_w_in")
    two = lambda a: a.reshape(-1, a.shape[-1])
    sharded = []
    for ids, g, got, far, w, m, v_, name in zip(
            (chip_slabs, slab_ids, chip_slabs), (gw_in, gw_out, gpw_own), (got_in, got_out, got_pw),
            (far_in, far_out, far_pw), (w_in, w_out, pool_w), (m_w_in, m_w_out, m_pool_w),
            (v_w_in, v_w_out, v_pool_w), ("adam_w_in", "adam_w_out", "adam_pool_w")):
        res = _adam_sharded(ids, g, got, far, two(w), two(m), two(v_), name)
        sharded.append([r.reshape(w.shape) for r in res])
    (g_w_in, dl_w_in, nm_w_in, nv_w_in), (g_w_out, dl_w_out, nm_w_out, nv_w_out), (g_pw, dl_pw, nm_pw, nv_pw) = sharded

    loss_sum = g_loss[0, 0]
    for dev in range(1, N_DEV):
        loss_sum = loss_sum + g_loss[dev, 0]
    loss = (0.5 / d) * loss_sum

    def ordered(rep_list, ada_, w_in_, pw_, w_out_):
        return [rep_list[0], ada_[None], rep_list[1], rep_list[2], rep_list[3], w_in_, rep_list[4], rep_list[5],
                pw_, rep_list[6], w_out_]

    return (loss, grad_x[None],
            *ordered(g_rep, g_wada, g_w_in, g_pw, g_w_out),
            *ordered(dl_rep, dl_wada, dl_w_in, dl_pw, dl_w_out),
            *ordered(nm_rep, nm_wada, nm_w_in, nm_pw, nm_w_out),
            *ordered(nv_rep, nv_wada, nv_w_in, nv_pw, nv_w_out))
```

```python
import functools

import jax
import jax.numpy as jnp
from jax import lax
from jax.experimental import pallas as pl
from jax.experimental.pallas import tpu as pltpu

HEAD_DIM = 128
GQA_GROUP = 4
ATTN_SUB_HEADS = 2
LOG2_E = 1.4426950408889634
GRID_W = 64
ROPE_PAIRS = HEAD_DIM // 4
ROPE_THETA = 10000.0
ATTN_SCALE = HEAD_DIM ** -0.5
EPS = 1e-6
POOL_WINDOWS = (2, 4, 8, 16)
POOL_HALO = 8
N_DEV = 8
N_CHIPS = 4
ADAM_LR = 0.001
ADAM_B1 = 0.9
ADAM_B2 = 0.999
ADAM_EPS = 1e-08
ADAM_WD = 0.01
ADAM_STEP = 10

LANES = 128
SUBLANES = 8
BF16_ROWS = 16

_MESH = pl.DeviceIdType.MESH
_ANY = pl.BlockSpec(memory_space=pl.ANY)
_VMEM = pl.BlockSpec(memory_space=pltpu.VMEM)
_HBM = pl.BlockSpec(memory_space=pltpu.HBM)
_SEM = pl.BlockSpec(memory_space=pltpu.SEMAPHORE)
_EFFECT = pltpu.SideEffectType.DATAFLOW_SIDE_EFFECTING
_F32 = jnp.float32
_BF16 = jnp.bfloat16


def _tile(dim, pref, align):
    t = min(pref, dim)
    t -= t % align
    while t >= align:
        if dim % t == 0:
            return t
        t -= align
    return dim


def _position():
    return lax.axis_index("x"), lax.axis_index("y"), lax.axis_index("c")


def _flip(v, bit):
    return 1 - v if bit else v


def _dev_index(x, y, c):
    return 4 * x + 2 * y + c


def _silu(g):
    return g * jax.nn.sigmoid(g)


def _silu_grad(g):
    s = jax.nn.sigmoid(g)
    return s * (1.0 + g * (1.0 - s))


def _adamw(w, g, m, v):
    m = ADAM_B1 * m + (1.0 - ADAM_B1) * g
    v = ADAM_B2 * v + (1.0 - ADAM_B2) * (g * g)
    m_hat = m / (1.0 - ADAM_B1 ** ADAM_STEP)
    v_hat = v / (1.0 - ADAM_B2 ** ADAM_STEP)
    delta = -ADAM_LR * (m_hat / (jnp.sqrt(v_hat) + ADAM_EPS) + ADAM_WD * w)
    return delta, m, v


def _all_gather_small(v, name):
    rows, cols = v.shape

    def body(v_ref, out_ref, send_sems, recv_sems):
        x, y, c = _position()
        me = _dev_index(x, y, c)
        out_ref[me] = v_ref[...]
        peers = [(_flip(x, k & 4), _flip(y, k & 2), _flip(c, k & 1)) for k in range(1, N_DEV)]

        def copy(k, block, to):
            return pltpu.make_async_remote_copy(
                src_ref=v_ref, dst_ref=out_ref.at[block], send_sem=send_sems.at[k], recv_sem=recv_sems.at[k],
                device_id=to, device_id_type=_MESH)

        sends = [copy(k, me, p) for k, p in enumerate(peers)]
        for s in sends:
            s.start()
        for k, p in enumerate(peers):
            copy(k, _dev_index(*p), p).wait_recv()
        for s in sends:
            s.wait_send()

    return pl.pallas_call(
        body, name=name,
        out_shape=jax.ShapeDtypeStruct((N_DEV, rows, cols), v.dtype),
        in_specs=[_VMEM], out_specs=_VMEM,
        scratch_shapes=[pltpu.SemaphoreType.DMA((N_DEV - 1,)), pltpu.SemaphoreType.DMA((N_DEV - 1,))],
    )(v)


def _route(x, y, c):
    first = (x + (1 - c) * (1 - 2 * x), y + c * (1 - 2 * y))
    second = (x + c * (1 - 2 * x), y + (1 - c) * (1 - 2 * y))
    return first, second, (1 - x, 1 - y)


def _w_in_order(x, y, c):
    first, second, diagonal = _route(x, y, c)
    return [(x, y, c), (x, y, 1 - c), (*first, c), (*second, 1 - c), (*second, c), (*first, 1 - c),
            (*diagonal, c), (*diagonal, 1 - c)]


W_IN_HOPS = {"a": 2, "b": 3, "c": 1, "d": 1}


def _w_in_hop_copies(group, wg, width, send_sems, recv_sems):
    x, y, c = _position()
    me, sibling = (x, y, c), (x, y, 1 - c)
    first, second, diagonal = _route(x, y, c)

    def cp(k, block, to):
        cols = wg.at[:, pl.ds(pl.multiple_of(_dev_index(*block) * width, width), width)]
        return pltpu.make_async_remote_copy(
            src_ref=cols, dst_ref=cols, send_sem=send_sems.at[k], recv_sem=recv_sems.at[k],
            device_id=to, device_id_type=_MESH)

    if group == "a":
        return [cp(0, me, sibling), cp(1, me, (*first, c))]
    if group == "b":
        return [cp(0, me, (*second, c)), cp(1, (*first, c), (*second, c)), cp(2, (*first, c), sibling)]
    if group == "c":
        return [cp(0, (*second, c), sibling)]
    return [cp(0, (*diagonal, c), sibling)]


def _w_in_hop(wg, width, waits, start, after, name):
    n_sem = 2 * len(waits)

    def body(*refs):
        wg_ref = refs[0]
        for i, (group, _, _, arrivals, sends) in enumerate(waits):
            cps = _w_in_hop_copies(group, wg_ref, width, refs[1 + 2 * i], refs[2 + 2 * i])
            for k in arrivals:
                cps[k].wait_recv()
            for k in sends:
                cps[k].wait_send()
        if start:
            for cp in _w_in_hop_copies(start, wg_ref, width, refs[n_sem + 2], refs[n_sem + 3]):
                cp.start()
        refs[-1][...] = jnp.zeros_like(refs[-1])

    sems = [s for w in waits for s in w[1:3]]
    new = [pltpu.SemaphoreType.DMA((W_IN_HOPS[start],))] * 2 if start else []
    outs = pl.pallas_call(
        body, name=name,
        out_shape=(*new, pltpu.HBM(wg.shape, wg.dtype), jax.ShapeDtypeStruct((SUBLANES, LANES), _F32)),
        in_specs=[_HBM] + [_SEM] * n_sem + [_ANY], out_specs=(*[_SEM] * len(new), _HBM, _VMEM),
        input_output_aliases={0: len(new)},
        compiler_params=pltpu.CompilerParams(has_side_effects=_EFFECT),
    )(pltpu.with_memory_space_constraint(wg, pltpu.HBM), *sems, after)
    return outs


def _slab_copies(bufs, send_sems, recv_sems):
    x, y, c = _position()
    me = _dev_index(x, y, c)
    peers = [(_flip(x, k & 4), _flip(y, k & 2), _flip(c, k & 1)) for k in range(1, N_DEV)]
    return [pltpu.make_async_remote_copy(
        src_ref=buf.at[me], dst_ref=buf.at[me],
        send_sem=send_sems.at[(N_DEV - 1) * a + k], recv_sem=recv_sems.at[(N_DEV - 1) * a + k],
        device_id=peer, device_id_type=_MESH)
        for a, buf in enumerate(bufs) for k, peer in enumerate(peers)]


def _gather_slabs_start(bufs, after, name):
    n = len(bufs)
    n_copies = (N_DEV - 1) * n

    def body(*refs):
        send_sems, recv_sems, token = refs[n + 1], refs[n + 2], refs[-1]
        for cp in _slab_copies(refs[:n], send_sems, recv_sems):
            cp.start()
        token[...] = jnp.zeros_like(token)

    outs = pl.pallas_call(
        body, name=name,
        out_shape=(pltpu.SemaphoreType.DMA((n_copies,)), pltpu.SemaphoreType.DMA((n_copies,)),
                   *[pltpu.HBM(b.shape, b.dtype) for b in bufs], jax.ShapeDtypeStruct((SUBLANES, LANES), _F32)),
        in_specs=[_HBM] * n + [_ANY], out_specs=(_SEM, _SEM, *[_HBM] * n, _VMEM),
        input_output_aliases={i: 2 + i for i in range(n)},
        compiler_params=pltpu.CompilerParams(has_side_effects=_EFFECT),
    )(*[pltpu.with_memory_space_constraint(b, pltpu.HBM) for b in bufs], after)
    return outs[0], outs[1], list(outs[2:2 + n]), outs[-1]


def _gather_slabs_wait(send_sems, recv_sems, bufs, after, name):
    n = len(bufs)

    def body(*refs):
        for cp in _slab_copies(refs[:n], refs[n], refs[n + 1]):
            cp.wait_send()
            cp.wait_recv()

    outs = pl.pallas_call(
        body, name=name, out_shape=tuple(pltpu.HBM(b.shape, b.dtype) for b in bufs),
        in_specs=[_HBM] * n + [_SEM, _SEM, _ANY], out_specs=[_HBM] * n,
        input_output_aliases={i: i for i in range(n)},
        compiler_params=pltpu.CompilerParams(has_side_effects=_EFFECT),
    )(*bufs, send_sems, recv_sems, after)
    return list(outs)


def _chip_order(x, y):
    return [(x, y), (1 - x, y), (x, 1 - y), (1 - x, 1 - y)]


def _exchange_sibling(grads, name):
    n = len(grads)
    out_shapes = [jax.ShapeDtypeStruct((N_CHIPS,) + g.shape[1:], g.dtype) for g in grads]

    def body(*refs):
        srcs, gots = refs[:n], refs[n:2 * n]
        send_sems, recv_sems = refs[2 * n:]
        x, y, c = _position()
        sends = []
        for a in range(n):
            for s, chip in enumerate(_chip_order(x, y)):
                k = N_CHIPS * a + s
                give = pltpu.make_async_remote_copy(
                    src_ref=srcs[a].at[_dev_index(*chip, 1 - c)], dst_ref=gots[a].at[s],
                    send_sem=send_sems.at[k], recv_sem=recv_sems.at[k], device_id=(x, y, 1 - c), device_id_type=_MESH)
                give.start()
                sends.append(give)
        for cp in sends:
            cp.wait_recv()
        for cp in sends:
            cp.wait_send()

    return pl.pallas_call(
        body, name=name, out_shape=out_shapes,
        in_specs=[_ANY] * n, out_specs=[_ANY] * n,
        scratch_shapes=[pltpu.SemaphoreType.DMA((N_CHIPS * n,)), pltpu.SemaphoreType.DMA((N_CHIPS * n,))],
    )(*grads)


def _chip_copies(srcs, lands, send_sems, recv_sems):
    x, y, c = _position()
    return [pltpu.make_async_remote_copy(
        src_ref=srcs[a].at[k], dst_ref=lands[a].at[k],
        send_sem=send_sems.at[(N_CHIPS - 1) * a + k], recv_sem=recv_sems.at[(N_CHIPS - 1) * a + k],
        device_id=(*chip, c), device_id_type=_MESH)
        for a in range(len(srcs)) for k, chip in enumerate(_chip_order(x, y)[1:])]


def _sibling_copies(srcs, lands, send_sems, recv_sems):
    x, y, c = _position()
    return [pltpu.make_async_remote_copy(
        src_ref=srcs[a].at[s], dst_ref=lands[a].at[s],
        send_sem=send_sems.at[N_CHIPS * a + s], recv_sem=recv_sems.at[N_CHIPS * a + s],
        device_id=(x, y, 1 - c), device_id_type=_MESH)
        for a in range(len(srcs)) for s in range(N_CHIPS)]


def _exchange_start(copies, sums, name):
    n = len(sums)
    n_copies = sum(s.shape[0] for s in sums)

    def body(*refs):
        srcs, lands = refs[:n], refs[n:2 * n]
        send_sems, recv_sems, token = refs[2 * n], refs[2 * n + 1], refs[-1]
        for cp in copies(srcs, lands, send_sems, recv_sems):
            cp.start()
        token[...] = jnp.zeros_like(token)

    hbm = [pltpu.HBM(s.shape, s.dtype) for s in sums]
    outs = pl.pallas_call(
        body, name=name,
        out_shape=(pltpu.SemaphoreType.DMA((n_copies,)), pltpu.SemaphoreType.DMA((n_copies,)), *hbm, *hbm,
                   jax.ShapeDtypeStruct((SUBLANES, LANES), _F32)),
        in_specs=[_HBM] * (2 * n), out_specs=(_SEM, _SEM, *[_HBM] * (2 * n), _VMEM),
        input_output_aliases={i: 2 + i for i in range(2 * n)},
        compiler_params=pltpu.CompilerParams(has_side_effects=_EFFECT),
    )(*[pltpu.with_memory_space_constraint(s, pltpu.HBM) for s in sums],
      *[pltpu.with_memory_space_constraint(lax.empty(s.shape, s.dtype), pltpu.HBM) for s in sums])
    return outs[0], outs[1], list(outs[2:2 + n]), list(outs[2 + n:2 + 2 * n]), outs[-1]


def _exchange_wait(copies, send_sems, recv_sems, srcs, lands, after, name):
    n = len(srcs)

    def body(*refs):
        for cp in copies(refs[:n], refs[n:2 * n], refs[2 * n], refs[2 * n + 1]):
            cp.wait_send()
            cp.wait_recv()

    hbm = [pltpu.HBM(s.shape, s.dtype) for s in srcs]
    outs = pl.pallas_call(
        body, name=name, out_shape=(*hbm, *hbm),
        in_specs=[_HBM] * (2 * n) + [_SEM, _SEM, _ANY], out_specs=[_HBM] * (2 * n),
        input_output_aliases={i: i for i in range(2 * n)},
        compiler_params=pltpu.CompilerParams(has_side_effects=_EFFECT),
    )(*srcs, *lands, send_sems, recv_sems, after)
    return list(outs[n:])


def _matmul(a, b, *, ta=False, tb=False, out_dtype=_F32, tm=1024, tn=1024, tk=4608, col_slabs=None, after=None, name):
    kdim, m = a.shape if ta else a.shape[::-1]
    n = b.shape[0] if tb else b.shape[1]
    tm = _tile(m, tm, LANES if ta else BF16_ROWS)
    tn = n // col_slabs if col_slabs else _tile(n, tn, LANES)
    tk = _tile(kdim, tk, BF16_ROWS if ta else LANES)
    nk = kdim // tk
    dims = (((0 if ta else 1,), (1 if tb else 0,)), ((), ()))

    def body_whole_k(a_ref, b_ref, *rest):
        o_ref = rest[-1]
        part = lax.dot_general(a_ref[...], b_ref[...], dims, preferred_element_type=_F32)
        o_ref[...] = part.astype(out_dtype).reshape(o_ref.shape)

    def body_split_k(a_ref, b_ref, *rest):
        o_ref, acc_ref = rest[-2:]
        k = pl.program_id(2)

        @pl.when(k == 0)
        def _():
            acc_ref[...] = jnp.zeros_like(acc_ref)

        acc_ref[...] += lax.dot_general(a_ref[...], b_ref[...], dims, preferred_element_type=_F32)

        @pl.when(k == nk - 1)
        def _():
            o_ref[...] = acc_ref[...].astype(out_dtype).reshape(o_ref.shape)

    a_spec = pl.BlockSpec((tk, tm), lambda i, j, k: (k, i)) if ta else pl.BlockSpec((tm, tk), lambda i, j, k: (i, k))
    b_spec = pl.BlockSpec((tn, tk), lambda i, j, k: (j, k)) if tb else pl.BlockSpec((tk, tn), lambda i, j, k: (k, j))
    if col_slabs:
        out_spec = pl.BlockSpec((1, tm, tn), lambda i, j, k: (j, i, 0))
        out_shape = jax.ShapeDtypeStruct((col_slabs, m, tn), out_dtype)
    else:
        out_spec = pl.BlockSpec((tm, tn), lambda i, j, k: (i, j))
        out_shape = jax.ShapeDtypeStruct((m, n), out_dtype)
    extra = [] if after is None else [after]
    return pl.pallas_call(
        body_whole_k if nk == 1 else body_split_k, name=name, grid=(m // tm, n // tn, nk),
        in_specs=[a_spec, b_spec] + [pl.BlockSpec(t.shape, lambda i, j, k: (0, 0)) for t in extra],
        out_specs=out_spec, out_shape=out_shape,
        scratch_shapes=[] if nk == 1 else [pltpu.VMEM((tm, tn), _F32)],
        compiler_params=pltpu.CompilerParams(dimension_semantics=("parallel", "parallel", "arbitrary")),
    )(a, b, *extra)


def _proj_block(a, wg, dst, order_ids, slot, width, after, name):
    m, kdim = a.shape
    tm = _tile(m, 1088, BF16_ROWS)

    def body(ids_ref, a_ref, w_ref, after_ref, dst_ref, o_ref):
        del ids_ref, after_ref, dst_ref
        o_ref[...] = jnp.dot(a_ref[...], w_ref[...], preferred_element_type=_F32)

    return pl.pallas_call(
        body, name=name,
        grid_spec=pltpu.PrefetchScalarGridSpec(
            num_scalar_prefetch=1, grid=(m // tm,),
            in_specs=[pl.BlockSpec((tm, kdim), lambda i, ids: (i, 0)),
                      pl.BlockSpec((kdim, width), lambda i, ids: (0, ids[slot])),
                      pl.BlockSpec(after.shape, lambda i, ids: (0, 0)), _ANY],
            out_specs=pl.BlockSpec((tm, width), lambda i, ids: (i, ids[slot]))),
        out_shape=jax.ShapeDtypeStruct(dst.shape, dst.dtype),
        input_output_aliases={4: 0},
        compiler_params=pltpu.CompilerParams(dimension_semantics=("arbitrary",)),
    )(order_ids, a, wg, after, dst)


def _cast_into_columns(slab_ids, a, n_blocks, name):
    r, c = a.shape
    tr = _row_tile(r, c)

    def body(ids_ref, a_ref, o_ref):
        del ids_ref
        o_ref[...] = a_ref[...].astype(_BF16)

    return pl.pallas_call(
        body, name=name,
        grid_spec=pltpu.PrefetchScalarGridSpec(
            num_scalar_prefetch=1, grid=(r // tr,),
            in_specs=[pl.BlockSpec((tr, c), lambda i, ids: (i, 0))],
            out_specs=pl.BlockSpec((tr, c), lambda i, ids: (i, ids[0]))),
        out_shape=jax.ShapeDtypeStruct((r, n_blocks * c), _BF16),
        compiler_params=pltpu.CompilerParams(dimension_semantics=("parallel",)),
    )(slab_ids, a)


def _matmul_slabs(a, b, ids, width, name, after=None):
    kdim, m = a.shape
    n_slabs = ids.shape[0]
    tm = _tile(m, 1024, LANES)

    def body(ids_ref, a_ref, b_ref, *rest):
        del ids_ref
        rest[-1][0] = lax.dot_general(a_ref[...], b_ref[...], (((0,), (0,)), ((), ())), preferred_element_type=_F32)

    extra = [] if after is None else [after]
    return pl.pallas_call(
        body, name=name,
        grid_spec=pltpu.PrefetchScalarGridSpec(
            num_scalar_prefetch=1, grid=(m // tm, n_slabs),
            in_specs=[pl.BlockSpec((kdim, tm), lambda i, j, ids: (0, i)),
                      pl.BlockSpec((kdim, width), lambda i, j, ids: (0, ids[j]))]
            + [pl.BlockSpec(t.shape, lambda i, j, ids: (0, 0)) for t in extra],
            out_specs=pl.BlockSpec((1, tm, width), lambda i, j, ids: (j, i, 0))),
        out_shape=jax.ShapeDtypeStruct((n_slabs, m, width), _F32),
        compiler_params=pltpu.CompilerParams(dimension_semantics=("parallel", "parallel")),
    )(ids, a, b, *extra)


def _row_tile(rows, cols):
    return _tile(rows, max(BF16_ROWS, min(512, (1 << 19) // cols)), BF16_ROWS)


def _cast_bf16(a, name):
    r, c = a.shape
    tr = _row_tile(r, c)

    def body(a_ref, o_ref):
        o_ref[...] = a_ref[...].astype(_BF16)

    blk = pl.BlockSpec((tr, c), lambda i: (i, 0))
    return pl.pallas_call(
        body, name=name, grid=(r // tr,), in_specs=[blk], out_specs=blk,
        out_shape=jax.ShapeDtypeStruct((r, c), _BF16),
        compiler_params=pltpu.CompilerParams(dimension_semantics=("parallel",)),
    )(a)


def _cast_into_slab(slab_ids, a, name):
    r, c = a.shape
    tr = _row_tile(r, c)

    def body(ids_ref, a_ref, o_ref):
        del ids_ref
        o_ref[0] = a_ref[...].astype(_BF16)

    return pl.pallas_call(
        body, name=name,
        grid_spec=pltpu.PrefetchScalarGridSpec(
            num_scalar_prefetch=1, grid=(r // tr,),
            in_specs=[pl.BlockSpec((tr, c), lambda i, ids: (i, 0))],
            out_specs=pl.BlockSpec((1, tr, c), lambda i, ids: (ids[0], i, 0))),
        out_shape=jax.ShapeDtypeStruct((N_DEV, r, c), _BF16),
        compiler_params=pltpu.CompilerParams(dimension_semantics=("parallel",)),
    )(slab_ids, a)


def _pre_add(slab_ids, grad, got, name):
    _, r, c = grad.shape
    tr = _row_tile(r, c)

    def body(ids_ref, a_ref, b_ref, o_ref):
        del ids_ref
        o_ref[...] = (a_ref[...] + b_ref[...]).astype(_BF16)

    return pl.pallas_call(
        body, name=name,
        grid_spec=pltpu.PrefetchScalarGridSpec(
            num_scalar_prefetch=1, grid=(N_CHIPS - 1, r // tr),
            in_specs=[pl.BlockSpec((1, tr, c), lambda s, i, ids: (ids[s + 1], i, 0)),
                      pl.BlockSpec((1, tr, c), lambda s, i, ids: (s + 1, i, 0))],
            out_specs=pl.BlockSpec((1, tr, c), lambda s, i, ids: (s, i, 0))),
        out_shape=jax.ShapeDtypeStruct((N_CHIPS - 1, r, c), _BF16),
        compiler_params=pltpu.CompilerParams(dimension_semantics=("parallel", "parallel")),
    )(slab_ids, grad, got)


def _ada_forward(craw, w_shard, name):
    d, cols = w_shard.shape
    tk = _tile(d, 512, LANES)

    def body(c_ref, w_ref, o_ref):
        @pl.when(pl.program_id(0) == 0)
        def _():
            o_ref[...] = jnp.zeros_like(o_ref)

        o_ref[...] += jnp.dot(_silu(c_ref[...]).astype(_BF16), w_ref[...].astype(_BF16), preferred_element_type=_F32)

    return pl.pallas_call(
        body, name=name, grid=(d // tk,),
        in_specs=[pl.BlockSpec((craw.shape[0], tk), lambda k: (0, k)), pl.BlockSpec((tk, cols), lambda k: (k, 0))],
        out_specs=pl.BlockSpec((craw.shape[0], cols), lambda k: (0, 0)),
        out_shape=jax.ShapeDtypeStruct((craw.shape[0], cols), _F32),
        compiler_params=pltpu.CompilerParams(dimension_semantics=("arbitrary",)),
    )(craw, w_shard)


def _ada_backward(craw, dmod, w, m, v, name):
    d, cols = w.shape
    rows = craw.shape[0]
    tr = _tile(d, 256, LANES)

    def body(c_ref, dm_ref, w_ref, m_ref, v_ref, g_ref, dl_ref, nm_ref, nv_ref, dc_ref):
        act = _silu(c_ref[...]).astype(_BF16)
        dmb = dm_ref[...].astype(_BF16)
        wv = w_ref[...]
        g = lax.dot_general(act, dmb, (((0,), (0,)), ((), ())), preferred_element_type=_F32)
        delta, nm, nv = _adamw(wv, g, m_ref[...], v_ref[...])
        g_ref[...] = g
        dl_ref[...] = delta
        nm_ref[...] = nm
        nv_ref[...] = nv
        dc = lax.dot_general(dmb, wv.astype(_BF16), (((1,), (1,)), ((), ())), preferred_element_type=_F32)
        dc_ref[...] = jnp.broadcast_to(jnp.sum(dc[N_DEV:], axis=0, keepdims=True), dc_ref.shape)

    blk = pl.BlockSpec((tr, cols), lambda i: (i, 0))
    return pl.pallas_call(
        body, name=name, grid=(d // tr,),
        in_specs=[pl.BlockSpec((rows, tr), lambda i: (0, i)), pl.BlockSpec((rows, cols), lambda i: (0, 0)), blk, blk, blk],
        out_specs=[blk, blk, blk, blk, pl.BlockSpec((SUBLANES, tr), lambda i: (0, i))],
        out_shape=[jax.ShapeDtypeStruct((d, cols), _F32)] * 4 + [jax.ShapeDtypeStruct((SUBLANES, d), _F32)],
        compiler_params=pltpu.CompilerParams(dimension_semantics=("parallel",)),
    )(craw, dmod, w, m, v)


def _rms(xf):
    return lax.rsqrt(jnp.mean(xf * xf, axis=-1, keepdims=True) + EPS)


def _prenorm(ctx, x, g_pre, mods, tr, name):
    l, d = ctx.shape
    n = x.shape[0]
    nbc = l // tr

    def body(ctx_ref, x_ref, g_ref, mod_ref, h_ref):
        def emit(src_ref):
            xf = src_ref[...]
            y = (xf * _rms(xf)) * g_ref[...]
            h_ref[...] = (y * (1.0 + mod_ref[0, 0:1, :]) + mod_ref[0, 1:2, :]).astype(_BF16)

        is_ctx = pl.program_id(0) < nbc
        pl.when(is_ctx)(lambda: emit(ctx_ref))
        pl.when(jnp.logical_not(is_ctx))(lambda: emit(x_ref))

    return pl.pallas_call(
        body, name=name, grid=((l + n) // tr,),
        in_specs=[pl.BlockSpec((tr, d), lambda i: (jnp.minimum(i, nbc - 1), 0)),
                  pl.BlockSpec((tr, d), lambda i: (jnp.maximum(i - nbc, 0), 0)),
                  pl.BlockSpec((1, d), lambda i: (0, 0)),
                  pl.BlockSpec((1, SUBLANES, d), lambda i: ((i >= nbc).astype(jnp.int32), 0, 0))],
        out_specs=pl.BlockSpec((tr, d), lambda i: (i, 0)),
        out_shape=jax.ShapeDtypeStruct((l + n, d), _BF16),
        compiler_params=pltpu.CompilerParams(dimension_semantics=("arbitrary",)),
    )(ctx, x, g_pre, mods)


def _prenorm_backward(dh, ctx, x, dxn, g_pre, mods, tr, name):
    l, d = ctx.shape
    n = x.shape[0]
    nbc = l // tr

    def body(dh_ref, ctx_ref, x_ref, dxn_ref, g_ref, mod_ref, gx_ref, dmod_ref, dg_ref):
        i = pl.program_id(0)

        @pl.when(i == 0)
        def _():
            dg_ref[...] = jnp.zeros_like(dg_ref)

        @pl.when(jnp.logical_or(i == 0, i == nbc))
        def _():
            dmod_ref[...] = jnp.zeros_like(dmod_ref)

        def emit(src_ref, latent):
            xf = src_ref[...]
            r = _rms(xf)
            xn = xf * r
            dhv = dh_ref[...]
            one_scale = 1.0 + mod_ref[0, 0:1, :]
            dmod_ref[0, 0:1, :] += jnp.sum(dhv * (xn * g_ref[...]), axis=0, keepdims=True)
            dmod_ref[0, 1:2, :] += jnp.sum(dhv, axis=0, keepdims=True)
            dyg = dhv * one_scale
            dg_ref[0:1, :] += jnp.sum(dyg * xn, axis=0, keepdims=True)
            if latent:
                dn = dyg * g_ref[...]
                gx_ref[...] = dxn_ref[...] + r * (dn - xn * jnp.mean(dn * xn, axis=-1, keepdims=True))

        pl.when(i < nbc)(lambda: emit(ctx_ref, False))
        pl.when(i >= nbc)(lambda: emit(x_ref, True))

    lat = pl.BlockSpec((tr, d), lambda i: (jnp.maximum(i - nbc, 0), 0))
    sel = pl.BlockSpec((1, SUBLANES, d), lambda i: ((i >= nbc).astype(jnp.int32), 0, 0))
    return pl.pallas_call(
        body, name=name, grid=((l + n) // tr,),
        in_specs=[pl.BlockSpec((tr, d), lambda i: (i, 0)),
                  pl.BlockSpec((tr, d), lambda i: (jnp.minimum(i, nbc - 1), 0)),
                  lat, lat, pl.BlockSpec((1, d), lambda i: (0, 0)), sel],
        out_specs=[lat, sel, pl.BlockSpec((SUBLANES, d), lambda i: (0, 0))],
        out_shape=[jax.ShapeDtypeStruct((n, d), _F32), jax.ShapeDtypeStruct((2, SUBLANES, d), _F32),
                   jax.ShapeDtypeStruct((SUBLANES, d), _F32)],
        compiler_params=pltpu.CompilerParams(dimension_semantics=("arbitrary",)),
    )(dh, ctx, x, dxn, g_pre, mods)


def _rope_tables(l, n):
    rows = n // GRID_W
    row = jnp.repeat(jnp.arange(rows, dtype=_F32), GRID_W)
    col = jnp.tile(jnp.arange(GRID_W, dtype=_F32), rows)
    inv = ROPE_THETA ** (-jnp.arange(ROPE_PAIRS, dtype=_F32) / ROPE_PAIRS)
    ang_r, ang_c = row[:, None] * inv, col[:, None] * inv
    cr, sr, cc, sc = jnp.cos(ang_r), jnp.sin(ang_r), jnp.cos(ang_c), jnp.sin(ang_c)
    zero = jnp.zeros_like(sr)
    tc = jnp.concatenate([cr, cr, cc, cc], axis=-1)
    ta = jnp.concatenate([-sr, zero, -sc, zero], axis=-1)
    tb = jnp.concatenate([zero, sr, zero, sc], axis=-1)
    pad = lambda t, fill: jnp.concatenate([jnp.full((l, HEAD_DIM), fill, _F32), t], axis=0)
    return pad(tc, 1.0), pad(ta, 0.0), pad(tb, 0.0)


def _rope(y, tc, ta, tb):
    return y * tc + pltpu.roll(y, HEAD_DIM - ROPE_PAIRS, 1) * ta + pltpu.roll(y, ROPE_PAIRS, 1) * tb


def _rope_transposed(dy, tc, ta, tb):
    return dy * tc + pltpu.roll(dy * ta, ROPE_PAIRS, 1) + pltpu.roll(dy * tb, HEAD_DIM - ROPE_PAIRS, 1)


def _qkv_post(proj, tables, g_q, g_k, heads, kv_heads, tr, name):
    t = proj.shape[0]
    aw, kw = heads * HEAD_DIM, kv_heads * HEAD_DIM
    w3 = aw + 2 * kw

    def body(p_ref, tc_ref, ta_ref, tb_ref, gq_ref, gk_ref, q_ref, k_ref, v_ref):
        tabs = (tc_ref[...], ta_ref[...], tb_ref[...])

        def norm_rope(col, gain):
            xh = p_ref[:, col:col + HEAD_DIM]
            return _rope((xh * _rms(xh)) * gain, *tabs).astype(_BF16)

        for h in range(heads):
            q_ref[h] = norm_rope(h * HEAD_DIM, gq_ref[...])
        for h in range(kv_heads):
            k_ref[h] = norm_rope(aw + h * HEAD_DIM, gk_ref[...])
            v_ref[h] = p_ref[:, aw + kw + h * HEAD_DIM:aw + kw + (h + 1) * HEAD_DIM].astype(_BF16)

    tab = pl.BlockSpec((tr, HEAD_DIM), lambda i: (i, 0))
    gain = pl.BlockSpec((1, HEAD_DIM), lambda i: (0, 0))
    return pl.pallas_call(
        body, name=name, grid=(t // tr,),
        in_specs=[pl.BlockSpec((tr, w3), lambda i: (i, 0)), tab, tab, tab, gain, gain],
        out_specs=[pl.BlockSpec((heads, tr, HEAD_DIM), lambda i: (0, i, 0)),
                   pl.BlockSpec((kv_heads, tr, HEAD_DIM), lambda i: (0, i, 0)),
                   pl.BlockSpec((kv_heads, tr, HEAD_DIM), lambda i: (0, i, 0))],
        out_shape=[jax.ShapeDtypeStruct((heads, t, HEAD_DIM), _BF16),
                   jax.ShapeDtypeStruct((kv_heads, t, HEAD_DIM), _BF16),
                   jax.ShapeDtypeStruct((kv_heads, t, HEAD_DIM), _BF16)],
        compiler_params=pltpu.CompilerParams(dimension_semantics=("parallel",)),
    )(proj, *tables, g_q, g_k)


def _qkv_post_backward(proj, dq, dk, dv, tables, g_q, g_k, l, tr, name):
    t = proj.shape[0]
    heads, kv_heads = dq.shape[0], dk.shape[0]
    aw, kw = heads * HEAD_DIM, kv_heads * HEAD_DIM
    w3 = aw + 2 * kw
    nbc = l // tr

    def body(p_ref, dq_ref, dk_ref, dv_ref, tc_ref, ta_ref, tb_ref, gq_ref, gk_ref, o_ref, dgq_ref, dgk_ref):
        i = pl.program_id(0)

        @pl.when(i == 0)
        def _():
            dgq_ref[...] = jnp.zeros_like(dgq_ref)
            dgk_ref[...] = jnp.zeros_like(dgk_ref)

        tabs = (tc_ref[...], ta_ref[...], tb_ref[...])
        latent = i >= nbc

        def back(col, dout, gain, dg_ref):
            xh = p_ref[:, col:col + HEAD_DIM]
            r = _rms(xh)
            xn = xh * r
            dy = _rope_transposed(dout, *tabs)
            dg_ref[0:1, :] += jnp.sum(dy * xn, axis=0, keepdims=True)
            dn = dy * gain
            o_ref[:, col:col + HEAD_DIM] = (r * (dn - xn * jnp.mean(dn * xn, axis=-1, keepdims=True))).astype(_BF16)

        for h in range(heads):
            back(h * HEAD_DIM, jnp.where(latent, dq_ref[h], 0.0), gq_ref[...], dgq_ref)
        for h in range(kv_heads):
            back(aw + h * HEAD_DIM, dk_ref[h], gk_ref[...], dgk_ref)
            o_ref[:, aw + kw + h * HEAD_DIM:aw + kw + (h + 1) * HEAD_DIM] = dv_ref[h].astype(_BF16)

    tab = pl.BlockSpec((tr, HEAD_DIM), lambda i: (i, 0))
    gain = pl.BlockSpec((1, HEAD_DIM), lambda i: (0, 0))
    acc = pl.BlockSpec((SUBLANES, HEAD_DIM), lambda i: (0, 0))
    return pl.pallas_call(
        body, name=name, grid=(t // tr,),
        in_specs=[pl.BlockSpec((tr, w3), lambda i: (i, 0)),
                  pl.BlockSpec((heads, tr, HEAD_DIM), lambda i: (0, jnp.maximum(i - nbc, 0), 0)),
                  pl.BlockSpec((kv_heads, tr, HEAD_DIM), lambda i: (0, i, 0)),
                  pl.BlockSpec((kv_heads, tr, HEAD_DIM), lambda i: (0, i, 0)),
                  tab, tab, tab, gain, gain],
        out_specs=[pl.BlockSpec((tr, w3), lambda i: (i, 0)), acc, acc],
        out_shape=[jax.ShapeDtypeStruct((t, w3), _BF16), jax.ShapeDtypeStruct((SUBLANES, HEAD_DIM), _F32),
                   jax.ShapeDtypeStruct((SUBLANES, HEAD_DIM), _F32)],
        compiler_params=pltpu.CompilerParams(dimension_semantics=("arbitrary",)),
    )(proj, dq, dk, dv, *tables, g_q, g_k)


def _attention(q, k, v, proj, l, mix, tq, name):
    heads, t, _ = q.shape
    kv_heads = k.shape[0]
    n = t - l
    rows = GQA_GROUP * tq
    gw = GQA_GROUP * HEAD_DIM
    aw = heads * HEAD_DIM
    gate_col = (aw + 2 * kv_heads * HEAD_DIM) // gw
    off = l // tq

    def body(q_ref, k_ref, v_ref, g_ref, o_ref, y_ref, lse_ref):
        lane = lax.broadcasted_iota(jnp.int32, (tq, LANES), 1)
        lse_blk = jnp.zeros((tq, LANES), _F32)
        for first in range(0, GQA_GROUP, ATTN_SUB_HEADS):
            qs = q_ref[first:first + ATTN_SUB_HEADS].reshape(ATTN_SUB_HEADS * tq, HEAD_DIM)
            raw = lax.dot_general(qs, k_ref[0], (((1,), (1,)), ((), ())), preferred_element_type=_F32)
            m = jnp.max(raw, axis=-1, keepdims=True)
            p = jnp.exp2((raw - m) * (ATTN_SCALE * LOG2_E))
            denom = jnp.sum(p, axis=-1, keepdims=True)
            os_ = jnp.dot(p.astype(_BF16), v_ref[0], preferred_element_type=_F32) / denom
            lse_s = m * ATTN_SCALE + jnp.log(denom)
            for j in range(ATTN_SUB_HEADS):
                g = first + j
                og = os_[j * tq:(j + 1) * tq]
                cols = slice(g * HEAD_DIM, (g + 1) * HEAD_DIM)
                o_ref[:, cols] = og
                y_ref[:, cols] = (og * _silu(g_ref[:, cols])).astype(_BF16)
                lse_blk = jnp.where(lane == g, lse_s[j * tq:(j + 1) * tq], lse_blk)
        lse_ref[0] = lse_blk

    return pl.pallas_call(
        body, name=name, grid=(kv_heads, n // tq),
        in_specs=[pl.BlockSpec((GQA_GROUP, tq, HEAD_DIM), lambda h, i: (h, i + off, 0)),
                  pl.BlockSpec((1, t, HEAD_DIM), lambda h, i: (h, 0, 0)),
                  pl.BlockSpec((1, t, HEAD_DIM), lambda h, i: (h, 0, 0)),
                  pl.BlockSpec((tq, gw), lambda h, i: (i + off, gate_col + h))],
        out_specs=[pl.BlockSpec((tq, gw), lambda h, i: (i, h)),
                   pl.BlockSpec((tq, gw), lambda h, i: (i, h)),
                   pl.BlockSpec((1, tq, LANES), lambda h, i: (h, i, 0))],
        out_shape=[jax.ShapeDtypeStruct((n, aw), _F32), jax.ShapeDtypeStruct((n, mix), _BF16),
                   jax.ShapeDtypeStruct((kv_heads, n, LANES), _F32)],
        compiler_params=pltpu.CompilerParams(dimension_semantics=("parallel", "parallel")),
    )(q, k, v, proj)


def _attention_backward(q, k, v, attn_o, dy, proj, lse, after, l, tq, name):
    heads, t, _ = q.shape
    kv_heads = k.shape[0]
    n = t - l
    rows = GQA_GROUP * tq
    gw = GQA_GROUP * HEAD_DIM
    aw = heads * HEAD_DIM
    gate_col = (aw + 2 * kv_heads * HEAD_DIM) // gw
    off = l // tq
    n_parts = 2 if t % (2 * BF16_ROWS) == 0 else 1
    part = t // n_parts

    def body(q_ref, k_ref, v_ref, o_ref, dy_ref, g_ref, lse_ref, after_ref, dq_ref, dg_ref, dk_ref, dv_ref):
        del after_ref

        @pl.when(pl.program_id(1) == 0)
        def _():
            dk_ref[...] = jnp.zeros_like(dk_ref)
            dv_ref[...] = jnp.zeros_like(dv_ref)

        q4 = q_ref[...].reshape(rows, HEAD_DIM)
        do_parts, delta_parts, lse_parts = [], [], []
        lse_blk = lse_ref[0]
        for g in range(GQA_GROUP):
            cols = slice(g * HEAD_DIM, (g + 1) * HEAD_DIM)
            gate, og, dyg = g_ref[:, cols], o_ref[:, cols], dy_ref[:, cols]
            dog = dyg * _silu(gate)
            dg_ref[:, cols] = (dyg * og * _silu_grad(gate)).astype(_BF16)
            do_parts.append(dog)
            delta_parts.append(jnp.sum(dog * og, axis=-1, keepdims=True))
            lse_parts.append(lse_blk[:, g:g + 1])
        do4 = jnp.concatenate(do_parts, axis=0).astype(_BF16)
        delta4 = jnp.concatenate(delta_parts, axis=0)
        lse4 = jnp.concatenate(lse_parts, axis=0)
        dq4 = jnp.zeros((rows, HEAD_DIM), _F32)
        for part_i in range(n_parts):
            keys = slice(part_i * part, (part_i + 1) * part)
            ks, vs = k_ref[0, keys, :], v_ref[0, keys, :]
            s = lax.dot_general(q4, ks, (((1,), (1,)), ((), ())), preferred_element_type=_F32) * ATTN_SCALE
            p = jnp.exp(s - lse4)
            dp = lax.dot_general(do4, vs, (((1,), (1,)), ((), ())), preferred_element_type=_F32)
            ds = (p * (dp - delta4) * ATTN_SCALE).astype(_BF16)
            dq4 = dq4 + jnp.dot(ds, ks, preferred_element_type=_F32)
            dk_ref[0, keys, :] += lax.dot_general(ds, q4, (((0,), (0,)), ((), ())), preferred_element_type=_F32)
            dv_ref[0, keys, :] += lax.dot_general(
                p.astype(_BF16), do4, (((0,), (0,)), ((), ())), preferred_element_type=_F32)
        dq_ref[...] = dq4.reshape(GQA_GROUP, tq, HEAD_DIM)

    kv_spec = pl.BlockSpec((1, t, HEAD_DIM), lambda h, i: (h, 0, 0))
    tok = pl.BlockSpec((tq, gw), lambda h, i: (i, h))
    return pl.pallas_call(
        body, name=name, grid=(kv_heads, n // tq),
        in_specs=[pl.BlockSpec((GQA_GROUP, tq, HEAD_DIM), lambda h, i: (h, i + off, 0)), kv_spec, kv_spec,
                  tok, tok, pl.BlockSpec((tq, gw), lambda h, i: (i + off, gate_col + h)),
                  pl.BlockSpec((1, tq, LANES), lambda h, i: (h, i, 0)),
                  pl.BlockSpec(after.shape, lambda h, i: (0, 0))],
        out_specs=[pl.BlockSpec((GQA_GROUP, tq, HEAD_DIM), lambda h, i: (h, i, 0)), tok, kv_spec, kv_spec],
        out_shape=[jax.ShapeDtypeStruct((heads, n, HEAD_DIM), _F32), jax.ShapeDtypeStruct((n, aw), _BF16),
                   jax.ShapeDtypeStruct((kv_heads, t, HEAD_DIM), _F32), jax.ShapeDtypeStruct((kv_heads, t, HEAD_DIM), _F32)],
        compiler_params=pltpu.CompilerParams(dimension_semantics=("parallel", "arbitrary")),
    )(q, k, v, attn_o, dy, proj, lse, after)


def _halo_specs(tp, width, col, row_off, total_rows):
    per = tp // POOL_HALO
    first = row_off // POOL_HALO
    last = total_rows // POOL_HALO - 1
    return [pl.BlockSpec((tp, width), lambda i: (i + row_off // tp, col)),
            pl.BlockSpec((POOL_HALO, width), lambda i: (jnp.maximum(first + i * per - 1, 0), col)),
            pl.BlockSpec((POOL_HALO, width), lambda i: (jnp.minimum(first + (i + 1) * per, last), col))]


def _with_halo(cur, prev, nxt, t0, n):
    tp = cur.shape[0]
    r8 = lax.broadcasted_iota(jnp.int32, (POOL_HALO, 1), 0)
    prev = jnp.where(t0 - POOL_HALO + r8 >= 0, prev, 0.0)
    nxt = jnp.where(t0 + tp + r8 < n, nxt, 0.0)
    return jnp.concatenate([prev, cur, nxt], axis=0)


def _shift_rows(a, s):
    return pltpu.roll(a, s % a.shape[0], 0)


def _window_sum(e, w, mirrored):
    a = e + _shift_rows(e, -1 if mirrored else 1)
    s = 1
    while 2 * s < w:
        a = _shift_rows(a, s) + _shift_rows(a, -s)
        s *= 2
    return a


def _window_count(t, w, n):
    half = w // 2
    return (jnp.minimum(t + half, n) - jnp.maximum(t - half, 0)).astype(_F32)


def _pool_forward(gi, proj, y, pool_w, pool_scale, l, heads, kv_heads, tp, name):
    t = proj.shape[0]
    n = t - l
    pg = pool_w.shape[-1]
    w = POOL_WINDOWS[gi]
    aw, kw = heads * HEAD_DIM, kv_heads * HEAD_DIM
    u_col = (2 * aw + 2 * kw) // pg + gi
    gate_col = (2 * aw + 2 * kw + len(POOL_WINDOWS) * pg) // pg + gi

    def body(u_ref, up_ref, un_ref, g_ref, w_ref, sc_ref, y_in_ref, y_ref, raw_ref, d_ref):
        del y_in_ref
        t0 = pl.program_id(0) * tp
        cur = u_ref[...]
        win = _window_sum(_with_halo(cur, up_ref[...], un_ref[...], t0, n), w, False)[POOL_HALO:POOL_HALO + tp]
        tok = t0 + lax.broadcasted_iota(jnp.int32, (tp, 1), 0)
        d = (win / _window_count(tok, w, n) - cur).astype(_BF16)
        raw = jnp.dot(d, w_ref[...].reshape(pg, pg), preferred_element_type=_F32)
        d_ref[...] = d
        raw_ref[...] = raw
        y_ref[...] = ((raw * sc_ref[...]) * _silu(g_ref[...])).astype(_BF16)

    blk = pl.BlockSpec((tp, pg), lambda i: (i, 0))
    return pl.pallas_call(
        body, name=name, grid=(n // tp,),
        in_specs=_halo_specs(tp, pg, u_col, l, t) + [
            pl.BlockSpec((tp, pg), lambda i: (i + l // tp, gate_col)),
            pl.BlockSpec((N_DEV, 1, pg // N_DEV, pg), lambda i: (0, gi, 0, 0)),
            pl.BlockSpec((1, pg), lambda i: (0, gi)), _ANY],
        out_specs=[pl.BlockSpec((tp, pg), lambda i: (i, aw // pg + gi)), blk, blk],
        out_shape=[jax.ShapeDtypeStruct(y.shape, y.dtype), jax.ShapeDtypeStruct((n, pg), _F32),
                   jax.ShapeDtypeStruct((n, pg), _BF16)],
        input_output_aliases={6: 0},
        compiler_params=pltpu.CompilerParams(dimension_semantics=("arbitrary",)),
    )(proj, proj, proj, proj, pool_w, pool_scale, y)


def _pool_backward_gate(gi, dy, proj, raw, pool_w, pool_scale, l, heads, kv_heads, tp, name):
    n, pg = raw.shape
    aw, kw = heads * HEAD_DIM, kv_heads * HEAD_DIM
    gate_col = (2 * aw + 2 * kw + len(POOL_WINDOWS) * pg) // pg + gi

    def body(dy_ref, g_ref, raw_ref, w_ref, sc_ref, dg_ref, dr_ref, dd_ref, ds_ref):
        @pl.when(pl.program_id(0) == 0)
        def _():
            ds_ref[...] = jnp.zeros_like(ds_ref)

        gate, rawv, dyv, scale = g_ref[...], raw_ref[...], dy_ref[...], sc_ref[...]
        dpool = dyv * _silu(gate)
        dg_ref[...] = (dyv * (rawv * scale) * _silu_grad(gate)).astype(_BF16)
        ds_ref[0:1, :] += jnp.sum(dpool * rawv, axis=0, keepdims=True)
        draw = (dpool * scale).astype(_BF16)
        dr_ref[...] = draw
        dd_ref[...] = lax.dot_general(
            draw, w_ref[...].reshape(pg, pg), (((1,), (1,)), ((), ())), preferred_element_type=_F32)

    blk = pl.BlockSpec((tp, pg), lambda i: (i, 0))
    return pl.pallas_call(
        body, name=name, grid=(n // tp,),
        in_specs=[pl.BlockSpec((tp, pg), lambda i: (i, aw // pg + gi)),
                  pl.BlockSpec((tp, pg), lambda i: (i + l // tp, gate_col)), blk,
                  pl.BlockSpec((N_DEV, 1, pg // N_DEV, pg), lambda i: (0, gi, 0, 0)),
                  pl.BlockSpec((1, pg), lambda i: (0, gi))],
        out_specs=[blk, blk, blk, pl.BlockSpec((SUBLANES, pg), lambda i: (0, 0))],
        out_shape=[jax.ShapeDtypeStruct((n, pg), _BF16), jax.ShapeDtypeStruct((n, pg), _BF16),
                   jax.ShapeDtypeStruct((n, pg), _F32), jax.ShapeDtypeStruct((SUBLANES, pg), _F32)],
        compiler_params=pltpu.CompilerParams(dimension_semantics=("arbitrary",)),
    )(dy, proj, raw, pool_w, pool_scale)


def _pool_backward_window(gi, dd, tp, name):
    n, pg = dd.shape
    w = POOL_WINDOWS[gi]

    def body(c_ref, p_ref, n_ref, du_ref):
        t0 = pl.program_id(0) * tp
        cur = c_ref[...]
        e = _with_halo(cur, p_ref[...], n_ref[...], t0, n)
        tok = t0 - POOL_HALO + lax.broadcasted_iota(jnp.int32, (tp + 2 * POOL_HALO, 1), 0)
        e = e / jnp.maximum(_window_count(tok, w, n), 1.0)
        du_ref[...] = (_window_sum(e, w, True)[POOL_HALO:POOL_HALO + tp] - cur).astype(_BF16)

    return pl.pallas_call(
        body, name=name, grid=(n // tp,),
        in_specs=_halo_specs(tp, pg, 0, 0, n), out_specs=pl.BlockSpec((tp, pg), lambda i: (i, 0)),
        out_shape=jax.ShapeDtypeStruct((n, pg), _BF16),
        compiler_params=pltpu.CompilerParams(dimension_semantics=("parallel",)),
    )(dd, dd, dd)


def _post(out, x, target, gate, g_post, tr, name):
    n, d = out.shape

    def body(o_ref, x_ref, t_ref, gate_ref, g_ref, dxn_ref, do_ref, dgate_ref, dg_ref, loss_ref):
        @pl.when(pl.program_id(0) == 0)
        def _():
            dgate_ref[...] = jnp.zeros_like(dgate_ref)
            dg_ref[...] = jnp.zeros_like(dg_ref)
            loss_ref[...] = jnp.zeros_like(loss_ref)

        ov = o_ref[...]
        r = _rms(ov)
        on = ov * r
        normed = on * g_ref[...]
        err = (x_ref[...] + gate_ref[...] * normed) - t_ref[...]
        loss_ref[...] += jnp.sum(err * err)
        dxn = err / d
        dxn_ref[...] = dxn
        dgate_ref[0:1, :] += jnp.sum(dxn * normed, axis=0, keepdims=True)
        dr = dxn * gate_ref[...]
        dg_ref[0:1, :] += jnp.sum(dr * on, axis=0, keepdims=True)
        dn = dr * g_ref[...]
        do_ref[...] = (r * (dn - on * jnp.mean(dn * on, axis=-1, keepdims=True))).astype(_BF16)

    blk = pl.BlockSpec((tr, d), lambda i: (i, 0))
    vec = pl.BlockSpec((1, d), lambda i: (0, 0))
    acc = pl.BlockSpec((SUBLANES, d), lambda i: (0, 0))
    return pl.pallas_call(
        body, name=name, grid=(n // tr,),
        in_specs=[blk, blk, blk, vec, vec],
        out_specs=[blk, blk, acc, acc, pl.BlockSpec((SUBLANES, LANES), lambda i: (0, 0))],
        out_shape=[jax.ShapeDtypeStruct((n, d), _F32), jax.ShapeDtypeStruct((n, d), _BF16),
                   jax.ShapeDtypeStruct((SUBLANES, d), _F32), jax.ShapeDtypeStruct((SUBLANES, d), _F32),
                   jax.ShapeDtypeStruct((SUBLANES, LANES), _F32)],
        compiler_params=pltpu.CompilerParams(dimension_semantics=("arbitrary",)),
    )(out, x, target, gate, g_post)


def _adam_sharded(slab_ids, grad, got, far, w, m, v, name):
    r, c = w.shape
    tr = _tile(r, max(BF16_ROWS, min(256, (1 << 18) // c)), BF16_ROWS)

    def body(ids_ref, own_ref, got_ref, far_ref, w_ref, m_ref, v_ref, g_ref, dl_ref, nm_ref, nv_ref):
        del ids_ref
        g = own_ref[0] + got_ref[0]
        for k in range(N_CHIPS - 1):
            g = g + far_ref[k].astype(_F32)
        delta, nm, nv = _adamw(w_ref[...], g, m_ref[...], v_ref[...])
        g_ref[...] = g
        dl_ref[...] = delta
        nm_ref[...] = nm
        nv_ref[...] = nv

    blk = pl.BlockSpec((tr, c), lambda i, ids: (i, 0))
    return pl.pallas_call(
        body, name=name,
        grid_spec=pltpu.PrefetchScalarGridSpec(
            num_scalar_prefetch=1, grid=(r // tr,),
            in_specs=[pl.BlockSpec((1, tr, c), lambda i, ids: (ids[0], i, 0)),
                      pl.BlockSpec((1, tr, c), lambda i, ids: (0, i, 0)),
                      pl.BlockSpec((N_CHIPS - 1, tr, c), lambda i, ids: (0, i, 0)), blk, blk, blk],
            out_specs=[blk] * 4),
        out_shape=[jax.ShapeDtypeStruct((r, c), _F32)] * 4,
        compiler_params=pltpu.CompilerParams(dimension_semantics=("parallel",)),
    )(slab_ids, grad, got, far, w, m, v)


def _adam_replicated(parts, extra, through_silu, w, m, v, name):
    def body(p_ref, e_ref, s_ref, w_ref, m_ref, v_ref, g_ref, dl_ref, nm_ref, nv_ref):
        total = p_ref[0] + e_ref[0]
        for dev in range(1, N_DEV):
            total = total + (p_ref[dev] + e_ref[dev])
        g = jnp.where(s_ref[...] > 0.5, total * _silu_grad(w_ref[...]), total)
        delta, nm, nv = _adamw(w_ref[...], g, m_ref[...], v_ref[...])
        g_ref[...] = g
        dl_ref[...] = delta
        nm_ref[...] = nm
        nv_ref[...] = nv

    return pl.pallas_call(
        body, name=name, in_specs=[_VMEM] * 6, out_specs=[_VMEM] * 4,
        out_shape=[jax.ShapeDtypeStruct(w.shape, _F32)] * 4,
    )(parts, extra, through_silu, w, m, v)


def _as_rows(vec):
    size = vec.shape[0]
    padded = -(-size // (SUBLANES * LANES)) * SUBLANES * LANES
    return jnp.pad(vec, (0, padded - size)).reshape(padded // LANES, LANES)


def kernel(x, c, ctx, c_ctx, w_ada, b_ada, norm_pre, norm_post, w_in, q_norm, k_norm, pool_w, pool_scale, w_out, loss_target, m_c_ctx, m_w_ada, m_b_ada, m_norm_pre, m_norm_post, m_w_in, m_q_norm, m_k_norm, m_pool_w, m_pool_scale, m_w_out, v_c_ctx, v_w_ada, v_b_ada, v_norm_pre, v_norm_post, v_w_in, v_q_norm, v_k_norm, v_pool_w, v_pool_scale, v_w_out):
    me = _dev_index(*_position())
    x2, ctx2, target = x[0], ctx[0], loss_target[0]
    n, d = x2.shape
    l = ctx2.shape[0]
    t = l + n
    aw = d // 2
    heads = aw // HEAD_DIM
    kv_heads = heads // GQA_GROUP
    kw = kv_heads * HEAD_DIM
    n_groups = len(POOL_WINDOWS)
    pg = (d - aw) // n_groups
    mix = d
    tr = _tile(l, 128, BF16_ROWS)
    tq = _tile(l, 128, BF16_ROWS)
    tp = _tile(l, 512, POOL_HALO)

    xi, yi, ci = _position()
    slab_ids = jnp.stack([_dev_index(*chip, ci) for chip in _chip_order(xi, yi)]).astype(jnp.int32)

    cw = w_in.shape[-1]
    wg = _cast_into_columns(slab_ids, w_in[0], N_DEV, "cast_w_in")
    hop_a = _w_in_hop(wg, cw, [], "a", norm_pre, "gather_w_in_a")
    late = [_cast_into_slab(slab_ids, w_out[0], "cast_w_out"),
            _cast_into_slab(slab_ids, pool_w[0].reshape(-1, pg), "cast_pool_w")]

    c_all = _all_gather_small(_as_rows(c[0]), "gather_c").reshape(N_DEV, -1)[:, :d]
    craw = jnp.concatenate([c_all, jnp.broadcast_to(c_ctx[None], (N_DEV, d))], axis=0)
    ada = _ada_forward(craw, w_ada[0], "ada_forward")
    ada_all = _all_gather_small(ada, "gather_ada")
    mod_all = ada_all.transpose(1, 0, 2).reshape(ada.shape[0], -1) + b_ada[0]
    mod = lax.dynamic_index_in_dim(mod_all, me, 0, keepdims=False)
    mod_c = mod_all[N_DEV]
    shift, scale, gate = mod[:d], mod[d:2 * d], mod[2 * d:]
    zeros6 = jnp.zeros((SUBLANES - 2, d), _F32)
    mods = jnp.stack([jnp.concatenate([mod_c[None, d:2 * d], mod_c[None, :d], zeros6], axis=0),
                      jnp.concatenate([scale[None], shift[None], zeros6], axis=0)])

    h_all = _prenorm(ctx2, x2, norm_pre, mods, tr, "prenorm")
    order_ids = jnp.stack([_dev_index(*dev) for dev in _w_in_order(xi, yi, ci)]).astype(jnp.int32)
    proj = lax.empty((t, N_DEV * cw), _F32)
    a_s, a_r, wg, tok = hop_a
    proj = _proj_block(h_all, wg, proj, order_ids, 0, cw, tok, "proj_0")
    wg, tok = _w_in_hop(wg, cw, [("a", a_s, a_r, [0], [])], None, proj, "gather_w_in_sibling")
    proj = _proj_block(h_all, wg, proj, order_ids, 1, cw, tok, "proj_1")
    b_s, b_r, wg, tok = _w_in_hop(wg, cw, [("a", a_s, a_r, [1], [])], "b", proj, "gather_w_in_b")
    proj = _proj_block(h_all, wg, proj, order_ids, 2, cw, tok, "proj_2")
    wg, tok = _w_in_hop(wg, cw, [("b", b_s, b_r, [2], [])], None, proj, "gather_w_in_b2")
    proj = _proj_block(h_all, wg, proj, order_ids, 3, cw, tok, "proj_3")
    c_s, c_r, wg, tok = _w_in_hop(wg, cw, [("b", b_s, b_r, [0], [])], "c", proj, "gather_w_in_c")
    proj = _proj_block(h_all, wg, proj, order_ids, 4, cw, tok, "proj_4")
    wg, tok = _w_in_hop(wg, cw, [("c", c_s, c_r, [0], [])], None, proj, "gather_w_in_c0")
    proj = _proj_block(h_all, wg, proj, order_ids, 5, cw, tok, "proj_5")
    d_s, d_r, wg, tok = _w_in_hop(wg, cw, [("b", b_s, b_r, [1], [])], "d", proj, "gather_w_in_d")
    proj = _proj_block(h_all, wg, proj, order_ids, 6, cw, tok, "proj_6")
    w_in_g, tok = _w_in_hop(
        wg, cw, [("d", d_s, d_r, [0], [0]), ("a", a_s, a_r, [], [0, 1]), ("b", b_s, b_r, [], [0, 1, 2]),
                 ("c", c_s, c_r, [], [0])], None, proj, "gather_w_in_end")
    proj = _proj_block(h_all, w_in_g, proj, order_ids, 7, cw, tok, "proj_7")
    flight_w = _gather_slabs_start(late, w_in_g, "gather_late_start")
    tables = _rope_tables(l, n)
    q, k, v = _qkv_post(proj, tables, q_norm, k_norm, heads, kv_heads, tr, "qkv_post")
    attn_o, y, lse = _attention(q, k, v, proj, l, mix, tq, "attention")
    w_out_g8, pool_g8 = _gather_slabs_wait(*flight_w[:3], attn_o, "gather_late_wait")
    w_out_g = w_out_g8.reshape(mix, d)
    pool_g = pool_g8.reshape(N_DEV, n_groups, pg // N_DEV, pg)
    raws, ds = [], []
    for gi in range(n_groups):
        y, raw, dsave = _pool_forward(gi, proj, y, pool_g, pool_scale, l, heads, kv_heads, tp, f"pool_forward_{gi}")
        raws.append(raw)
        ds.append(dsave)
    out = _matmul(y, w_out_g, name="out_proj")
    dxn, dout, dgate8, dgpost8, loss8 = _post(out, x2, target, gate[None], norm_post, tr, "post")

    dy = _matmul(dout, w_out_g, tb=True, name="d_y")
    gw_out = _matmul(y, dout, ta=True, name="grad_w_out").reshape(N_DEV, mix // N_DEV, d)
    got_out = _exchange_sibling([gw_out], "exchange_sibling_w_out")[0]
    sum_out = _pre_add(slab_ids, gw_out, got_out, "pre_add_w_out")
    flight_out = _exchange_start(_chip_copies, [sum_out], "exchange_chips_start_w_out")
    dq, dgate_attn, dk, dv = _attention_backward(
        q, k, v, attn_o, dy, proj, lse, flight_out[-1], l, tq, "attention_backward")
    dqkv, dgq8, dgk8 = _qkv_post_backward(proj, dq, dk, dv, tables, q_norm, k_norm, l, tr, "qkv_post_backward")
    dus, dgps, gpw, dps8 = [], [], [], []
    for gi in range(n_groups):
        dgp, draw, dd, dps = _pool_backward_gate(
            gi, dy, proj, raws[gi], pool_g, pool_scale, l, heads, kv_heads, tp, f"pool_backward_gate_{gi}")
        dus.append(_pool_backward_window(gi, dd, tp, f"pool_backward_window_{gi}"))
        dgps.append(dgp)
        dps8.append(dps)
        gpw.append(_matmul(ds[gi], draw, ta=True, name=f"grad_pool_w_{gi}"))
    latent_cols = jnp.concatenate([dgate_attn] + dus + dgps, axis=1)
    dproj = jnp.concatenate([dqkv, jnp.pad(latent_cols, ((l, 0), (0, 0)))], axis=1)
    cw = w_in.shape[-1]
    other_ids = jnp.stack([_dev_index(*chip, 1 - ci) for chip in _chip_order(xi, yi)]).astype(jnp.int32)
    chip_slabs = jnp.arange(N_CHIPS, dtype=jnp.int32)
    pr = pool_w.shape[2]
    gpw8 = jnp.stack(gpw).reshape(n_groups, N_DEV, pr, pg).transpose(1, 0, 2, 3).reshape(N_DEV, n_groups * pr, pg)
    give_in = _matmul_slabs(h_all, dproj, other_ids, cw, "grad_w_in_sibling")
    flight_sib = _exchange_start(_sibling_copies, [give_in, jnp.take(gpw8, other_ids, axis=0)], "exchange_sibling_start")
    gw_in = _matmul_slabs(h_all, dproj, slab_ids, cw, "grad_w_in_own", after=flight_sib[-1])
    gpw_own = jnp.take(gpw8, slab_ids, axis=0)
    got_in, got_pw = _exchange_wait(_sibling_copies, *flight_sib[:4], gw_in, "exchange_sibling_wait")
    sums_in = [_pre_add(chip_slabs, gw_in, got_in, "pre_add_w_in"), _pre_add(chip_slabs, gpw_own, got_pw, "pre_add_pool_w")]
    flight_in = _exchange_start(_chip_copies, sums_in, "exchange_chips_start_w_in")
    dh = _matmul(dproj, w_in_g, tb=True, tm=1088, tk=3072, after=flight_in[-1], name="d_h")
    grad_x, dmods, dgpre8 = _prenorm_backward(dh, ctx2, x2, dxn, norm_pre, mods, tr, "prenorm_backward")

    dmod_lat = jnp.concatenate([dmods[1, 1], dmods[1, 0], dgate8[0]])
    dmod_ctx = jnp.concatenate([dmods[0, 1], dmods[0, 0], jnp.zeros((d,), _F32)])
    small = jnp.concatenate([dmod_lat, dmod_ctx, dgpre8[0], dgpost8[0], dgq8[0], dgk8[0]] + [p[0] for p in dps8]
                            + [loss8[0, :1]])
    gathered = _all_gather_small(_as_rows(small), "gather_small").reshape(N_DEV, -1)
    o = 0
    take = lambda size: (gathered[:, o:o + size], o + size)
    g_mod, o = take(3 * d)
    g_modc, o = take(3 * d)
    g_pre, o = take(d)
    g_post, o = take(d)
    g_q, o = take(HEAD_DIM)
    g_k, o = take(HEAD_DIM)
    g_ps, o = take(n_groups * pg)
    g_loss, o = take(1)
    cols = w_ada.shape[-1]
    mine = lambda a: lax.dynamic_slice_in_dim(a, me * cols, cols, axis=1)
    dmod_rows = jnp.concatenate([mine(g_mod), mine(g_modc)], axis=0)
    g_wada, dl_wada, nm_wada, nv_wada, dcact = _ada_backward(craw, dmod_rows, w_ada[0], m_w_ada[0], v_w_ada[0], "ada_backward")
    dcc = _all_gather_small(_as_rows(dcact[0]), "gather_dcc").reshape(N_DEV, -1)[:, :d]

    sizes = [d, 3 * d, d, d, HEAD_DIM, HEAD_DIM, n_groups * pg]
    def pack(parts):
        rows = jnp.concatenate(parts, axis=1)
        padded = -(-rows.shape[1] // (SUBLANES * LANES)) * SUBLANES * LANES
        return jnp.pad(rows, ((0, 0), (0, padded - rows.shape[1]))).reshape(N_DEV, padded // LANES, LANES)

    zero = lambda size: jnp.zeros((N_DEV, size), _F32)
    parts = pack([dcc, g_mod, g_pre, g_post, g_q, g_k, g_ps])
    extra = pack([zero(d), g_modc, zero(d), zero(d), zero(HEAD_DIM), zero(HEAD_DIM), zero(n_groups * pg)])
    through_silu = _as_rows(jnp.concatenate([jnp.ones((d,), _F32), jnp.zeros((sum(sizes[1:]),), _F32)]))
    cat = lambda items: _as_rows(jnp.concatenate([a.reshape(-1) for a in items]))
    ws = [c_ctx, b_ada, norm_pre, norm_post, q_norm, k_norm, pool_scale]
    ms = [m_c_ctx, m_b_ada, m_norm_pre, m_norm_post, m_q_norm, m_k_norm, m_pool_scale]
    vs = [v_c_ctx, v_b_ada, v_norm_pre, v_norm_post, v_q_norm, v_k_norm, v_pool_scale]
    rep = _adam_replicated(parts, extra, through_silu, cat(ws), cat(ms), cat(vs), "adam_replicated")

    def split(packed):
        flat_, outs, at = packed.reshape(-1), [], 0
        for w, size in zip(ws, sizes):
            outs.append(flat_[at:at + size].reshape(w.shape))
            at += size
        return outs

    g_rep, dl_rep, nm_rep, nv_rep = [split(r) for r in rep]

    far_out = _exchange_wait(_chip_copies, *flight_out[:4], grad_x, "exchange_chips_wait_w_out")[0]
    far_in, far_pw = _exchange_wait(_chip_copies, *flight_in[:4], rep[0], "exchange_chips_wait_w_in")
    two = lambda a: a.reshape(-1, a.shape[-1])
    sharded = []
    for ids, g, got, far, w, m, v_, name in zip(
            (chip_slabs, slab_ids, chip_slabs), (gw_in, gw_out, gpw_own), (got_in, got_out, got_pw),
            (far_in, far_out, far_pw), (w_in, w_out, pool_w), (m_w_in, m_w_out, m_pool_w),
            (v_w_in, v_w_out, v_pool_w), ("adam_w_in", "adam_w_out", "adam_pool_w")):
        res = _adam_sharded(ids, g, got, far, two(w), two(m), two(v_), name)
        sharded.append([r.reshape(w.shape) for r in res])
    (g_w_in, dl_w_in, nm_w_in, nv_w_in), (g_w_out, dl_w_out, nm_w_out, nv_w_out), (g_pw, dl_pw, nm_pw, nv_pw) = sharded

    loss_sum = g_loss[0, 0]
    for dev in range(1, N_DEV):
        loss_sum = loss_sum + g_loss[dev, 0]
    loss = (0.5 / d) * loss_sum

    def ordered(rep_list, ada_, w_in_, pw_, w_out_):
        return [rep_list[0], ada_[None], rep_list[1], rep_list[2], rep_list[3], w_in_, rep_list[4], rep_list[5],
                pw_, rep_list[6], w_out_]

    return (loss, grad_x[None],
            *ordered(g_rep, g_wada, g_w_in, g_pw, g_w_out),
            *ordered(dl_rep, dl_wada, dl_w_in, dl_pw, dl_w_out),
            *ordered(nm_rep, nm_wada, nm_w_in, nm_pw, nm_w_out),
            *ordered(nv_rep, nv_wada, nv_w_in, nv_pw, nv_w_out))
```

```python
import functools

import jax
import jax.numpy as jnp
from jax import lax
from jax.experimental import pallas as pl
from jax.experimental.pallas import tpu as pltpu

HEAD_DIM = 128
GQA_GROUP = 4
ATTN_SUB_HEADS = 2
LOG2_E = 1.4426950408889634
GRID_W = 64
ROPE_PAIRS = HEAD_DIM // 4
ROPE_THETA = 10000.0
ATTN_SCALE = HEAD_DIM ** -0.5
EPS = 1e-6
POOL_WINDOWS = (2, 4, 8, 16)
POOL_HALO = 8
N_DEV = 8
N_CHIPS = 4
ADAM_LR = 0.001
ADAM_B1 = 0.9
ADAM_B2 = 0.999
ADAM_EPS = 1e-08
ADAM_WD = 0.01
ADAM_STEP = 10

LANES = 128
SUBLANES = 8
BF16_ROWS = 16

_MESH = pl.DeviceIdType.MESH
_ANY = pl.BlockSpec(memory_space=pl.ANY)
_VMEM = pl.BlockSpec(memory_space=pltpu.VMEM)
_HBM = pl.BlockSpec(memory_space=pltpu.HBM)
_SEM = pl.BlockSpec(memory_space=pltpu.SEMAPHORE)
_EFFECT = pltpu.SideEffectType.DATAFLOW_SIDE_EFFECTING
_F32 = jnp.float32
_BF16 = jnp.bfloat16


def _tile(dim, pref, align):
    t = min(pref, dim)
    t -= t % align
    while t >= align:
        if dim % t == 0:
            return t
        t -= align
    return dim


def _position():
    return lax.axis_index("x"), lax.axis_index("y"), lax.axis_index("c")


def _flip(v, bit):
    return 1 - v if bit else v


def _dev_index(x, y, c):
    return 4 * x + 2 * y + c


def _silu(g):
    return g * jax.nn.sigmoid(g)


def _silu_grad(g):
    s = jax.nn.sigmoid(g)
    return s * (1.0 + g * (1.0 - s))


def _adamw(w, g, m, v):
    m = ADAM_B1 * m + (1.0 - ADAM_B1) * g
    v = ADAM_B2 * v + (1.0 - ADAM_B2) * (g * g)
    m_hat = m / (1.0 - ADAM_B1 ** ADAM_STEP)
    v_hat = v / (1.0 - ADAM_B2 ** ADAM_STEP)
    delta = -ADAM_LR * (m_hat / (jnp.sqrt(v_hat) + ADAM_EPS) + ADAM_WD * w)
    return delta, m, v


def _all_gather_small(v, name):
    rows, cols = v.shape

    def body(v_ref, out_ref, send_sems, recv_sems):
        x, y, c = _position()
        me = _dev_index(x, y, c)
        out_ref[me] = v_ref[...]
        peers = [(_flip(x, k & 4), _flip(y, k & 2), _flip(c, k & 1)) for k in range(1, N_DEV)]

        def copy(k, block, to):
            return pltpu.make_async_remote_copy(
                src_ref=v_ref, dst_ref=out_ref.at[block], send_sem=send_sems.at[k], recv_sem=recv_sems.at[k],
                device_id=to, device_id_type=_MESH)

        sends = [copy(k, me, p) for k, p in enumerate(peers)]
        for s in sends:
            s.start()
        for k, p in enumerate(peers):
            copy(k, _dev_index(*p), p).wait_recv()
        for s in sends:
            s.wait_send()

    return pl.pallas_call(
        body, name=name,
        out_shape=jax.ShapeDtypeStruct((N_DEV, rows, cols), v.dtype),
        in_specs=[_VMEM], out_specs=_VMEM,
        scratch_shapes=[pltpu.SemaphoreType.DMA((N_DEV - 1,)), pltpu.SemaphoreType.DMA((N_DEV - 1,))],
    )(v)


def _route(x, y, c):
    first = (x + (1 - c) * (1 - 2 * x), y + c * (1 - 2 * y))
    second = (x + c * (1 - 2 * x), y + (1 - c) * (1 - 2 * y))
    return first, second, (1 - x, 1 - y)


def _w_in_order(x, y, c):
    first, second, diagonal = _route(x, y, c)
    return [(x, y, c), (x, y, 1 - c), (*first, c), (*second, 1 - c), (*second, c), (*first, 1 - c),
            (*diagonal, c), (*diagonal, 1 - c)]


W_IN_HOPS = {"a": 2, "b": 3, "c": 1, "d": 1}


def _w_in_hop_copies(group, wg, width, send_sems, recv_sems):
    x, y, c = _position()
    me, sibling = (x, y, c), (x, y, 1 - c)
    first, second, diagonal = _route(x, y, c)

    def cp(k, block, to):
        cols = wg.at[:, pl.ds(pl.multiple_of(_dev_index(*block) * width, width), width)]
        return pltpu.make_async_remote_copy(
            src_ref=cols, dst_ref=cols, send_sem=send_sems.at[k], recv_sem=recv_sems.at[k],
            device_id=to, device_id_type=_MESH)

    if group == "a":
        return [cp(0, me, sibling), cp(1, me, (*first, c))]
    if group == "b":
        return [cp(0, me, (*second, c)), cp(1, (*first, c), (*second, c)), cp(2, (*first, c), sibling)]
    if group == "c":
        return [cp(0, (*second, c), sibling)]
    return [cp(0, (*diagonal, c), sibling)]


def _w_in_hop(wg, width, waits, start, after, name):
    n_sem = 2 * len(waits)

    def body(*refs):
        wg_ref = refs[0]
        for i, (group, _, _, arrivals, sends) in enumerate(waits):
            cps = _w_in_hop_copies(group, wg_ref, width, refs[1 + 2 * i], refs[2 + 2 * i])
            for k in arrivals:
                cps[k].wait_recv()
            for k in sends:
                cps[k].wait_send()
        if start:
            for cp in _w_in_hop_copies(start, wg_ref, width, refs[n_sem + 2], refs[n_sem + 3]):
                cp.start()
        refs[-1][...] = jnp.zeros_like(refs[-1])

    sems = [s for w in waits for s in w[1:3]]
    new = [pltpu.SemaphoreType.DMA((W_IN_HOPS[start],))] * 2 if start else []
    outs = pl.pallas_call(
        body, name=name,
        out_shape=(*new, pltpu.HBM(wg.shape, wg.dtype), jax.ShapeDtypeStruct((SUBLANES, LANES), _F32)),
        in_specs=[_HBM] + [_SEM] * n_sem + [_ANY], out_specs=(*[_SEM] * len(new), _HBM, _VMEM),
        input_output_aliases={0: len(new)},
        compiler_params=pltpu.CompilerParams(has_side_effects=_EFFECT),
    )(pltpu.with_memory_space_constraint(wg, pltpu.HBM), *sems, after)
    return outs


def _slab_copies(bufs, send_sems, recv_sems):
    x, y, c = _position()
    me = _dev_index(x, y, c)
    peers = [(_flip(x, k & 4), _flip(y, k & 2), _flip(c, k & 1)) for k in range(1, N_DEV)]
    return [pltpu.make_async_remote_copy(
        src_ref=buf.at[me], dst_ref=buf.at[me],
        send_sem=send_sems.at[(N_DEV - 1) * a + k], recv_sem=recv_sems.at[(N_DEV - 1) * a + k],
        device_id=peer, device_id_type=_MESH)
        for a, buf in enumerate(bufs) for k, peer in enumerate(peers)]


def _gather_slabs_start(bufs, after, name):
    n = len(bufs)
    n_copies = (N_DEV - 1) * n

    def body(*refs):
        send_sems, recv_sems, token = refs[n + 1], refs[n + 2], refs[-1]
        for cp in _slab_copies(refs[:n], send_sems, recv_sems):
            cp.start()
        token[...] = jnp.zeros_like(token)

    outs = pl.pallas_call(
        body, name=name,
        out_shape=(pltpu.SemaphoreType.DMA((n_copies,)), pltpu.SemaphoreType.DMA((n_copies,)),
                   *[pltpu.HBM(b.shape, b.dtype) for b in bufs], jax.ShapeDtypeStruct((SUBLANES, LANES), _F32)),
        in_specs=[_HBM] * n + [_ANY], out_specs=(_SEM, _SEM, *[_HBM] * n, _VMEM),
        input_output_aliases={i: 2 + i for i in range(n)},
        compiler_params=pltpu.CompilerParams(has_side_effects=_EFFECT),
    )(*[pltpu.with_memory_space_constraint(b, pltpu.HBM) for b in bufs], after)
    return outs[0], outs[1], list(outs[2:2 + n]), outs[-1]


def _gather_slabs_wait(send_sems, recv_sems, bufs, after, name):
    n = len(bufs)

    def body(*refs):
        for cp in _slab_copies(refs[:n], refs[n], refs[n + 1]):
            cp.wait_send()
            cp.wait_recv()

    outs = pl.pallas_call(
        body, name=name, out_shape=tuple(pltpu.HBM(b.shape, b.dtype) for b in bufs),
        in_specs=[_HBM] * n + [_SEM, _SEM, _ANY], out_specs=[_HBM] * n,
        input_output_aliases={i: i for i in range(n)},
        compiler_params=pltpu.CompilerParams(has_side_effects=_EFFECT),
    )(*bufs, send_sems, recv_sems, after)
    return list(outs)


def _chip_order(x, y):
    return [(x, y), (1 - x, y), (x, 1 - y), (1 - x, 1 - y)]


def _exchange_sibling(grads, name):
    n = len(grads)
    out_shapes = [jax.ShapeDtypeStruct((N_CHIPS,) + g.shape[1:], g.dtype) for g in grads]

    def body(*refs):
        srcs, gots = refs[:n], refs[n:2 * n]
        send_sems, recv_sems = refs[2 * n:]
        x, y, c = _position()
        sends = []
        for a in range(n):
            for s, chip in enumerate(_chip_order(x, y)):
                k = N_CHIPS * a + s
                give = pltpu.make_async_remote_copy(
                    src_ref=srcs[a].at[_dev_index(*chip, 1 - c)], dst_ref=gots[a].at[s],
                    send_sem=send_sems.at[k], recv_sem=recv_sems.at[k], device_id=(x, y, 1 - c), device_id_type=_MESH)
                give.start()
                sends.append(give)
        for cp in sends:
            cp.wait_recv()
        for cp in sends:
            cp.wait_send()

    return pl.pallas_call(
        body, name=name, out_shape=out_shapes,
        in_specs=[_ANY] * n, out_specs=[_ANY] * n,
        scratch_shapes=[pltpu.SemaphoreType.DMA((N_CHIPS * n,)), pltpu.SemaphoreType.DMA((N_CHIPS * n,))],
    )(*grads)


def _chip_copies(srcs, lands, send_sems, recv_sems):
    x, y, c = _position()
    return [pltpu.make_async_remote_copy(
        src_ref=srcs[a].at[k], dst_ref=lands[a].at[k],
        send_sem=send_sems.at[(N_CHIPS - 1) * a + k], recv_sem=recv_sems.at[(N_CHIPS - 1) * a + k],
        device_id=(*chip, c), device_id_type=_MESH)
        for a in range(len(srcs)) for k, chip in enumerate(_chip_order(x, y)[1:])]


def _sibling_copies(srcs, lands, send_sems, recv_sems):
    x, y, c = _position()
    return [pltpu.make_async_remote_copy(
        src_ref=srcs[a].at[s], dst_ref=lands[a].at[s],
        send_sem=send_sems.at[N_CHIPS * a + s], recv_sem=recv_sems.at[N_CHIPS * a + s],
        device_id=(x, y, 1 - c), device_id_type=_MESH)
        for a in range(len(srcs)) for s in range(N_CHIPS)]


def _exchange_start(copies, sums, name):
    n = len(sums)
    n_copies = sum(s.shape[0] for s in sums)

    def body(*refs):
        srcs, lands = refs[:n], refs[n:2 * n]
        send_sems, recv_sems, token = refs[2 * n], refs[2 * n + 1], refs[-1]
        for cp in copies(srcs, lands, send_sems, recv_sems):
            cp.start()
        token[...] = jnp.zeros_like(token)

    hbm = [pltpu.HBM(s.shape, s.dtype) for s in sums]
    outs = pl.pallas_call(
        body, name=name,
        out_shape=(pltpu.SemaphoreType.DMA((n_copies,)), pltpu.SemaphoreType.DMA((n_copies,)), *hbm, *hbm,
                   jax.ShapeDtypeStruct((SUBLANES, LANES), _F32)),
        in_specs=[_HBM] * (2 * n), out_specs=(_SEM, _SEM, *[_HBM] * (2 * n), _VMEM),
        input_output_aliases={i: 2 + i for i in range(2 * n)},
        compiler_params=pltpu.CompilerParams(has_side_effects=_EFFECT),
    )(*[pltpu.with_memory_space_constraint(s, pltpu.HBM) for s in sums],
      *[pltpu.with_memory_space_constraint(lax.empty(s.shape, s.dtype), pltpu.HBM) for s in sums])
    return outs[0], outs[1], list(outs[2:2 + n]), list(outs[2 + n:2 + 2 * n]), outs[-1]


def _exchange_wait(copies, send_sems, recv_sems, srcs, lands, after, name):
    n = len(srcs)

    def body(*refs):
        for cp in copies(refs[:n], refs[n:2 * n], refs[2 * n], refs[2 * n + 1]):
            cp.wait_send()
            cp.wait_recv()

    hbm = [pltpu.HBM(s.shape, s.dtype) for s in srcs]
    outs = pl.pallas_call(
        body, name=name, out_shape=(*hbm, *hbm),
        in_specs=[_HBM] * (2 * n) + [_SEM, _SEM, _ANY], out_specs=[_HBM] * (2 * n),
        input_output_aliases={i: i for i in range(2 * n)},
        compiler_params=pltpu.CompilerParams(has_side_effects=_EFFECT),
    )(*srcs, *lands, send_sems, recv_sems, after)
    return list(outs[n:])


def _matmul(a, b, *, ta=False, tb=False, out_dtype=_F32, tm=1024, tn=1024, tk=4608, col_slabs=None, after=None, name):
    kdim, m = a.shape if ta else a.shape[::-1]
    n = b.shape[0] if tb else b.shape[1]
    tm = _tile(m, tm, LANES if ta else BF16_ROWS)
    tn = n // col_slabs if col_slabs else _tile(n, tn, LANES)
    tk = _tile(kdim, tk, BF16_ROWS if ta else LANES)
    nk = kdim // tk
    dims = (((0 if ta else 1,), (1 if tb else 0,)), ((), ()))

    def body_whole_k(a_ref, b_ref, *rest):
        o_ref = rest[-1]
        part = lax.dot_general(a_ref[...], b_ref[...], dims, preferred_element_type=_F32)
        o_ref[...] = part.astype(out_dtype).reshape(o_ref.shape)

    def body_split_k(a_ref, b_ref, *rest):
        o_ref, acc_ref = rest[-2:]
        k = pl.program_id(2)

        @pl.when(k == 0)
        def _():
            acc_ref[...] = jnp.zeros_like(acc_ref)

        acc_ref[...] += lax.dot_general(a_ref[...], b_ref[...], dims, preferred_element_type=_F32)

        @pl.when(k == nk - 1)
        def _():
            o_ref[...] = acc_ref[...].astype(out_dtype).reshape(o_ref.shape)

    a_spec = pl.BlockSpec((tk, tm), lambda i, j, k: (k, i)) if ta else pl.BlockSpec((tm, tk), lambda i, j, k: (i, k))
    b_spec = pl.BlockSpec((tn, tk), lambda i, j, k: (j, k)) if tb else pl.BlockSpec((tk, tn), lambda i, j, k: (k, j))
    if col_slabs:
        out_spec = pl.BlockSpec((1, tm, tn), lambda i, j, k: (j, i, 0))
        out_shape = jax.ShapeDtypeStruct((col_slabs, m, tn), out_dtype)
    else:
        out_spec = pl.BlockSpec((tm, tn), lambda i, j, k: (i, j))
        out_shape = jax.ShapeDtypeStruct((m, n), out_dtype)
    extra = [] if after is None else [after]
    return pl.pallas_call(
        body_whole_k if nk == 1 else body_split_k, name=name, grid=(m // tm, n // tn, nk),
        in_specs=[a_spec, b_spec] + [pl.BlockSpec(t.shape, lambda i, j, k: (0, 0)) for t in extra],
        out_specs=out_spec, out_shape=out_shape,
        scratch_shapes=[] if nk == 1 else [pltpu.VMEM((tm, tn), _F32)],
        compiler_params=pltpu.CompilerParams(dimension_semantics=("parallel", "parallel", "arbitrary")),
    )(a, b, *extra)


def _proj_blocks(a, wg, dst, order_ids, first, count, width, after, name):
    m, kdim = a.shape
    tm = _tile(m, 1088, BF16_ROWS)

    def body(ids_ref, a_ref, w_ref, after_ref, dst_ref, o_ref):
        del ids_ref, after_ref, dst_ref
        o_ref[...] = jnp.dot(a_ref[...], w_ref[...], preferred_element_type=_F32)

    return pl.pallas_call(
        body, name=name,
        grid_spec=pltpu.PrefetchScalarGridSpec(
            num_scalar_prefetch=1, grid=(count, m // tm),
            in_specs=[pl.BlockSpec((tm, kdim), lambda j, i, ids: (i, 0)),
                      pl.BlockSpec((kdim, width), lambda j, i, ids: (0, ids[first + j])),
                      pl.BlockSpec(after.shape, lambda j, i, ids: (0, 0)), _ANY],
            out_specs=pl.BlockSpec((tm, width), lambda j, i, ids: (i, ids[first + j]))),
        out_shape=jax.ShapeDtypeStruct(dst.shape, dst.dtype),
        input_output_aliases={4: 0},
        compiler_params=pltpu.CompilerParams(dimension_semantics=("arbitrary", "arbitrary")),
    )(order_ids, a, wg, after, dst)


def _cast_into_columns(slab_ids, a, n_blocks, name):
    r, c = a.shape
    tr = _row_tile(r, c)

    def body(ids_ref, a_ref, o_ref):
        del ids_ref
        o_ref[...] = a_ref[...].astype(_BF16)

    return pl.pallas_call(
        body, name=name,
        grid_spec=pltpu.PrefetchScalarGridSpec(
            num_scalar_prefetch=1, grid=(r // tr,),
            in_specs=[pl.BlockSpec((tr, c), lambda i, ids: (i, 0))],
            out_specs=pl.BlockSpec((tr, c), lambda i, ids: (i, ids[0]))),
        out_shape=jax.ShapeDtypeStruct((r, n_blocks * c), _BF16),
        compiler_params=pltpu.CompilerParams(dimension_semantics=("parallel",)),
    )(slab_ids, a)


def _matmul_slabs(a, b, ids, width, name, after=None):
    kdim, m = a.shape
    n_slabs = ids.shape[0]
    tm = _tile(m, 1024, LANES)

    def body(ids_ref, a_ref, b_ref, *rest):
        del ids_ref
        rest[-1][0] = lax.dot_general(a_ref[...], b_ref[...], (((0,), (0,)), ((), ())), preferred_element_type=_F32)

    extra = [] if after is None else [after]
    return pl.pallas_call(
        body, name=name,
        grid_spec=pltpu.PrefetchScalarGridSpec(
            num_scalar_prefetch=1, grid=(m // tm, n_slabs),
            in_specs=[pl.BlockSpec((kdim, tm), lambda i, j, ids: (0, i)),
                      pl.BlockSpec((kdim, width), lambda i, j, ids: (0, ids[j]))]
            + [pl.BlockSpec(t.shape, lambda i, j, ids: (0, 0)) for t in extra],
            out_specs=pl.BlockSpec((1, tm, width), lambda i, j, ids: (j, i, 0))),
        out_shape=jax.ShapeDtypeStruct((n_slabs, m, width), _F32),
        compiler_params=pltpu.CompilerParams(dimension_semantics=("parallel", "parallel")),
    )(ids, a, b, *extra)


def _row_tile(rows, cols):
    return _tile(rows, max(BF16_ROWS, min(512, (1 << 19) // cols)), BF16_ROWS)


def _cast_bf16(a, name):
    r, c = a.shape
    tr = _row_tile(r, c)

    def body(a_ref, o_ref):
        o_ref[...] = a_ref[...].astype(_BF16)

    blk = pl.BlockSpec((tr, c), lambda i: (i, 0))
    return pl.pallas_call(
        body, name=name, grid=(r // tr,), in_specs=[blk], out_specs=blk,
        out_shape=jax.ShapeDtypeStruct((r, c), _BF16),
        compiler_params=pltpu.CompilerParams(dimension_semantics=("parallel",)),
    )(a)


def _cast_into_slab(slab_ids, a, name):
    r, c = a.shape
    tr = _row_tile(r, c)

    def body(ids_ref, a_ref, o_ref):
        del ids_ref
        o_ref[0] = a_ref[...].astype(_BF16)

    return pl.pallas_call(
        body, name=name,
        grid_spec=pltpu.PrefetchScalarGridSpec(
            num_scalar_prefetch=1, grid=(r // tr,),
            in_specs=[pl.BlockSpec((tr, c), lambda i, ids: (i, 0))],
            out_specs=pl.BlockSpec((1, tr, c), lambda i, ids: (ids[0], i, 0))),
        out_shape=jax.ShapeDtypeStruct((N_DEV, r, c), _BF16),
        compiler_params=pltpu.CompilerParams(dimension_semantics=("parallel",)),
    )(slab_ids, a)


def _pre_add(slab_ids, grad, got, name):
    _, r, c = grad.shape
    tr = _row_tile(r, c)

    def body(ids_ref, a_ref, b_ref, o_ref):
        del ids_ref
        o_ref[...] = (a_ref[...] + b_ref[...]).astype(_BF16)

    return pl.pallas_call(
        body, name=name,
        grid_spec=pltpu.PrefetchScalarGridSpec(
            num_scalar_prefetch=1, grid=(N_CHIPS - 1, r // tr),
            in_specs=[pl.BlockSpec((1, tr, c), lambda s, i, ids: (ids[s + 1], i, 0)),
                      pl.BlockSpec((1, tr, c), lambda s, i, ids: (s + 1, i, 0))],
            out_specs=pl.BlockSpec((1, tr, c), lambda s, i, ids: (s, i, 0))),
        out_shape=jax.ShapeDtypeStruct((N_CHIPS - 1, r, c), _BF16),
        compiler_params=pltpu.CompilerParams(dimension_semantics=("parallel", "parallel")),
    )(slab_ids, grad, got)


def _ada_forward(craw, w_shard, name):
    d, cols = w_shard.shape
    tk = _tile(d, 512, LANES)

    def body(c_ref, w_ref, o_ref):
        @pl.when(pl.program_id(0) == 0)
        def _():
            o_ref[...] = jnp.zeros_like(o_ref)

        o_ref[...] += jnp.dot(_silu(c_ref[...]).astype(_BF16), w_ref[...].astype(_BF16), preferred_element_type=_F32)

    return pl.pallas_call(
        body, name=name, grid=(d // tk,),
        in_specs=[pl.BlockSpec((craw.shape[0], tk), lambda k: (0, k)), pl.BlockSpec((tk, cols), lambda k: (k, 0))],
        out_specs=pl.BlockSpec((craw.shape[0], cols), lambda k: (0, 0)),
        out_shape=jax.ShapeDtypeStruct((craw.shape[0], cols), _F32),
        compiler_params=pltpu.CompilerParams(dimension_semantics=("arbitrary",)),
    )(craw, w_shard)


def _ada_backward(craw, dmod, w, m, v, name):
    d, cols = w.shape
    rows = craw.shape[0]
    tr = _tile(d, 256, LANES)

    def body(c_ref, dm_ref, w_ref, m_ref, v_ref, g_ref, dl_ref, nm_ref, nv_ref, dc_ref):
        act = _silu(c_ref[...]).astype(_BF16)
        dmb = dm_ref[...].astype(_BF16)
        wv = w_ref[...]
        g = lax.dot_general(act, dmb, (((0,), (0,)), ((), ())), preferred_element_type=_F32)
        delta, nm, nv = _adamw(wv, g, m_ref[...], v_ref[...])
        g_ref[...] = g
        dl_ref[...] = delta
        nm_ref[...] = nm
        nv_ref[...] = nv
        dc = lax.dot_general(dmb, wv.astype(_BF16), (((1,), (1,)), ((), ())), preferred_element_type=_F32)
        dc_ref[...] = jnp.broadcast_to(jnp.sum(dc[N_DEV:], axis=0, keepdims=True), dc_ref.shape)

    blk = pl.BlockSpec((tr, cols), lambda i: (i, 0))
    return pl.pallas_call(
        body, name=name, grid=(d // tr,),
        in_specs=[pl.BlockSpec((rows, tr), lambda i: (0, i)), pl.BlockSpec((rows, cols), lambda i: (0, 0)), blk, blk, blk],
        out_specs=[blk, blk, blk, blk, pl.BlockSpec((SUBLANES, tr), lambda i: (0, i))],
        out_shape=[jax.ShapeDtypeStruct((d, cols), _F32)] * 4 + [jax.ShapeDtypeStruct((SUBLANES, d), _F32)],
        compiler_params=pltpu.CompilerParams(dimension_semantics=("parallel",)),
    )(craw, dmod, w, m, v)


def _rms(xf):
    return lax.rsqrt(jnp.mean(xf * xf, axis=-1, keepdims=True) + EPS)


def _prenorm(ctx, x, g_pre, mods, tr, name):
    l, d = ctx.shape
    n = x.shape[0]
    nbc = l // tr

    def body(ctx_ref, x_ref, g_ref, mod_ref, h_ref):
        def emit(src_ref):
            xf = src_ref[...]
            y = (xf * _rms(xf)) * g_ref[...]
            h_ref[...] = (y * (1.0 + mod_ref[0, 0:1, :]) + mod_ref[0, 1:2, :]).astype(_BF16)

        is_ctx = pl.program_id(0) < nbc
        pl.when(is_ctx)(lambda: emit(ctx_ref))
        pl.when(jnp.logical_not(is_ctx))(lambda: emit(x_ref))

    return pl.pallas_call(
        body, name=name, grid=((l + n) // tr,),
        in_specs=[pl.BlockSpec((tr, d), lambda i: (jnp.minimum(i, nbc - 1), 0)),
                  pl.BlockSpec((tr, d), lambda i: (jnp.maximum(i - nbc, 0), 0)),
                  pl.BlockSpec((1, d), lambda i: (0, 0)),
                  pl.BlockSpec((1, SUBLANES, d), lambda i: ((i >= nbc).astype(jnp.int32), 0, 0))],
        out_specs=pl.BlockSpec((tr, d), lambda i: (i, 0)),
        out_shape=jax.ShapeDtypeStruct((l + n, d), _BF16),
        compiler_params=pltpu.CompilerParams(dimension_semantics=("arbitrary",)),
    )(ctx, x, g_pre, mods)


def _prenorm_backward(dh, ctx, x, dxn, g_pre, mods, tr, name):
    l, d = ctx.shape
    n = x.shape[0]
    nbc = l // tr

    def body(dh_ref, ctx_ref, x_ref, dxn_ref, g_ref, mod_ref, gx_ref, dmod_ref, dg_ref):
        i = pl.program_id(0)

        @pl.when(i == 0)
        def _():
            dg_ref[...] = jnp.zeros_like(dg_ref)

        @pl.when(jnp.logical_or(i == 0, i == nbc))
        def _():
            dmod_ref[...] = jnp.zeros_like(dmod_ref)

        def emit(src_ref, latent):
            xf = src_ref[...]
            r = _rms(xf)
            xn = xf * r
            dhv = dh_ref[...]
            one_scale = 1.0 + mod_ref[0, 0:1, :]
            dmod_ref[0, 0:1, :] += jnp.sum(dhv * (xn * g_ref[...]), axis=0, keepdims=True)
            dmod_ref[0, 1:2, :] += jnp.sum(dhv, axis=0, keepdims=True)
            dyg = dhv * one_scale
            dg_ref[0:1, :] += jnp.sum(dyg * xn, axis=0, keepdims=True)
            if latent:
                dn = dyg * g_ref[...]
                gx_ref[...] = dxn_ref[...] + r * (dn - xn * jnp.mean(dn * xn, axis=-1, keepdims=True))

        pl.when(i < nbc)(lambda: emit(ctx_ref, False))
        pl.when(i >= nbc)(lambda: emit(x_ref, True))

    lat = pl.BlockSpec((tr, d), lambda i: (jnp.maximum(i - nbc, 0), 0))
    sel = pl.BlockSpec((1, SUBLANES, d), lambda i: ((i >= nbc).astype(jnp.int32), 0, 0))
    return pl.pallas_call(
        body, name=name, grid=((l + n) // tr,),
        in_specs=[pl.BlockSpec((tr, d), lambda i: (i, 0)),
                  pl.BlockSpec((tr, d), lambda i: (jnp.minimum(i, nbc - 1), 0)),
                  lat, lat, pl.BlockSpec((1, d), lambda i: (0, 0)), sel],
        out_specs=[lat, sel, pl.BlockSpec((SUBLANES, d), lambda i: (0, 0))],
        out_shape=[jax.ShapeDtypeStruct((n, d), _F32), jax.ShapeDtypeStruct((2, SUBLANES, d), _F32),
                   jax.ShapeDtypeStruct((SUBLANES, d), _F32)],
        compiler_params=pltpu.CompilerParams(dimension_semantics=("arbitrary",)),
    )(dh, ctx, x, dxn, g_pre, mods)


def _rope_tables(l, n):
    rows = n // GRID_W
    row = jnp.repeat(jnp.arange(rows, dtype=_F32), GRID_W)
    col = jnp.tile(jnp.arange(GRID_W, dtype=_F32), rows)
    inv = ROPE_THETA ** (-jnp.arange(ROPE_PAIRS, dtype=_F32) / ROPE_PAIRS)
    ang_r, ang_c = row[:, None] * inv, col[:, None] * inv
    cr, sr, cc, sc = jnp.cos(ang_r), jnp.sin(ang_r), jnp.cos(ang_c), jnp.sin(ang_c)
    zero = jnp.zeros_like(sr)
    tc = jnp.concatenate([cr, cr, cc, cc], axis=-1)
    ta = jnp.concatenate([-sr, zero, -sc, zero], axis=-1)
    tb = jnp.concatenate([zero, sr, zero, sc], axis=-1)
    pad = lambda t, fill: jnp.concatenate([jnp.full((l, HEAD_DIM), fill, _F32), t], axis=0)
    return pad(tc, 1.0), pad(ta, 0.0), pad(tb, 0.0)


def _rope(y, tc, ta, tb):
    return y * tc + pltpu.roll(y, HEAD_DIM - ROPE_PAIRS, 1) * ta + pltpu.roll(y, ROPE_PAIRS, 1) * tb


def _rope_transposed(dy, tc, ta, tb):
    return dy * tc + pltpu.roll(dy * ta, ROPE_PAIRS, 1) + pltpu.roll(dy * tb, HEAD_DIM - ROPE_PAIRS, 1)


def _qkv_post(proj, tables, g_q, g_k, heads, kv_heads, tr, name):
    t = proj.shape[0]
    aw, kw = heads * HEAD_DIM, kv_heads * HEAD_DIM
    w3 = aw + 2 * kw

    def body(p_ref, tc_ref, ta_ref, tb_ref, gq_ref, gk_ref, q_ref, k_ref, v_ref):
        tabs = (tc_ref[...], ta_ref[...], tb_ref[...])

        def norm_rope(col, gain):
            xh = p_ref[:, col:col + HEAD_DIM]
            return _rope((xh * _rms(xh)) * gain, *tabs).astype(_BF16)

        for h in range(heads):
            q_ref[h] = norm_rope(h * HEAD_DIM, gq_ref[...])
        for h in range(kv_heads):
            k_ref[h] = norm_rope(aw + h * HEAD_DIM, gk_ref[...])
            v_ref[h] = p_ref[:, aw + kw + h * HEAD_DIM:aw + kw + (h + 1) * HEAD_DIM].astype(_BF16)

    tab = pl.BlockSpec((tr, HEAD_DIM), lambda i: (i, 0))
    gain = pl.BlockSpec((1, HEAD_DIM), lambda i: (0, 0))
    return pl.pallas_call(
        body, name=name, grid=(t // tr,),
        in_specs=[pl.BlockSpec((tr, w3), lambda i: (i, 0)), tab, tab, tab, gain, gain],
        out_specs=[pl.BlockSpec((heads, tr, HEAD_DIM), lambda i: (0, i, 0)),
                   pl.BlockSpec((kv_heads, tr, HEAD_DIM), lambda i: (0, i, 0)),
                   pl.BlockSpec((kv_heads, tr, HEAD_DIM), lambda i: (0, i, 0))],
        out_shape=[jax.ShapeDtypeStruct((heads, t, HEAD_DIM), _BF16),
                   jax.ShapeDtypeStruct((kv_heads, t, HEAD_DIM), _BF16),
                   jax.ShapeDtypeStruct((kv_heads, t, HEAD_DIM), _BF16)],
        compiler_params=pltpu.CompilerParams(dimension_semantics=("parallel",)),
    )(proj, *tables, g_q, g_k)


def _qkv_post_backward(proj, dq, dk, dv, tables, g_q, g_k, l, tr, name):
    t = proj.shape[0]
    heads, kv_heads = dq.shape[0], dk.shape[0]
    aw, kw = heads * HEAD_DIM, kv_heads * HEAD_DIM
    w3 = aw + 2 * kw
    nbc = l // tr

    def body(p_ref, dq_ref, dk_ref, dv_ref, tc_ref, ta_ref, tb_ref, gq_ref, gk_ref, o_ref, dgq_ref, dgk_ref):
        i = pl.program_id(0)

        @pl.when(i == 0)
        def _():
            dgq_ref[...] = jnp.zeros_like(dgq_ref)
            dgk_ref[...] = jnp.zeros_like(dgk_ref)

        tabs = (tc_ref[...], ta_ref[...], tb_ref[...])
        latent = i >= nbc

        def back(col, dout, gain, dg_ref):
            xh = p_ref[:, col:col + HEAD_DIM]
            r = _rms(xh)
            xn = xh * r
            dy = _rope_transposed(dout, *tabs)
            dg_ref[0:1, :] += jnp.sum(dy * xn, axis=0, keepdims=True)
            dn = dy * gain
            o_ref[:, col:col + HEAD_DIM] = (r * (dn - xn * jnp.mean(dn * xn, axis=-1, keepdims=True))).astype(_BF16)

        for h in range(heads):
            back(h * HEAD_DIM, jnp.where(latent, dq_ref[h], 0.0), gq_ref[...], dgq_ref)
        for h in range(kv_heads):
            back(aw + h * HEAD_DIM, dk_ref[h], gk_ref[...], dgk_ref)
            o_ref[:, aw + kw + h * HEAD_DIM:aw + kw + (h + 1) * HEAD_DIM] = dv_ref[h].astype(_BF16)

    tab = pl.BlockSpec((tr, HEAD_DIM), lambda i: (i, 0))
    gain = pl.BlockSpec((1, HEAD_DIM), lambda i: (0, 0))
    acc = pl.BlockSpec((SUBLANES, HEAD_DIM), lambda i: (0, 0))
    return pl.pallas_call(
        body, name=name, grid=(t // tr,),
        in_specs=[pl.BlockSpec((tr, w3), lambda i: (i, 0)),
                  pl.BlockSpec((heads, tr, HEAD_DIM), lambda i: (0, jnp.maximum(i - nbc, 0), 0)),
                  pl.BlockSpec((kv_heads, tr, HEAD_DIM), lambda i: (0, i, 0)),
                  pl.BlockSpec((kv_heads, tr, HEAD_DIM), lambda i: (0, i, 0)),
                  tab, tab, tab, gain, gain],
        out_specs=[pl.BlockSpec((tr, w3), lambda i: (i, 0)), acc, acc],
        out_shape=[jax.ShapeDtypeStruct((t, w3), _BF16), jax.ShapeDtypeStruct((SUBLANES, HEAD_DIM), _F32),
                   jax.ShapeDtypeStruct((SUBLANES, HEAD_DIM), _F32)],
        compiler_params=pltpu.CompilerParams(dimension_semantics=("arbitrary",)),
    )(proj, dq, dk, dv, *tables, g_q, g_k)


def _attention(q, k, v, proj, l, mix, tq, name):
    heads, t, _ = q.shape
    kv_heads = k.shape[0]
    n = t - l
    rows = GQA_GROUP * tq
    gw = GQA_GROUP * HEAD_DIM
    aw = heads * HEAD_DIM
    gate_col = (aw + 2 * kv_heads * HEAD_DIM) // gw
    off = l // tq

    def body(q_ref, k_ref, v_ref, g_ref, o_ref, y_ref, lse_ref):
        lane = lax.broadcasted_iota(jnp.int32, (tq, LANES), 1)
        lse_blk = jnp.zeros((tq, LANES), _F32)
        for first in range(0, GQA_GROUP, ATTN_SUB_HEADS):
            qs = q_ref[first:first + ATTN_SUB_HEADS].reshape(ATTN_SUB_HEADS * tq, HEAD_DIM)
            raw = lax.dot_general(qs, k_ref[0], (((1,), (1,)), ((), ())), preferred_element_type=_F32)
            m = jnp.max(raw, axis=-1, keepdims=True)
            p = jnp.exp2((raw - m) * (ATTN_SCALE * LOG2_E))
            denom = jnp.sum(p, axis=-1, keepdims=True)
            os_ = jnp.dot(p.astype(_BF16), v_ref[0], preferred_element_type=_F32) / denom
            lse_s = m * ATTN_SCALE + jnp.log(denom)
            for j in range(ATTN_SUB_HEADS):
                g = first + j
                og = os_[j * tq:(j + 1) * tq]
                cols = slice(g * HEAD_DIM, (g + 1) * HEAD_DIM)
                o_ref[:, cols] = og
                y_ref[:, cols] = (og * _silu(g_ref[:, cols])).astype(_BF16)
                lse_blk = jnp.where(lane == g, lse_s[j * tq:(j + 1) * tq], lse_blk)
        lse_ref[0] = lse_blk

    return pl.pallas_call(
        body, name=name, grid=(kv_heads, n // tq),
        in_specs=[pl.BlockSpec((GQA_GROUP, tq, HEAD_DIM), lambda h, i: (h, i + off, 0)),
                  pl.BlockSpec((1, t, HEAD_DIM), lambda h, i: (h, 0, 0)),
                  pl.BlockSpec((1, t, HEAD_DIM), lambda h, i: (h, 0, 0)),
                  pl.BlockSpec((tq, gw), lambda h, i: (i + off, gate_col + h))],
        out_specs=[pl.BlockSpec((tq, gw), lambda h, i: (i, h)),
                   pl.BlockSpec((tq, gw), lambda h, i: (i, h)),
                   pl.BlockSpec((1, tq, LANES), lambda h, i: (h, i, 0))],
        out_shape=[jax.ShapeDtypeStruct((n, aw), _F32), jax.ShapeDtypeStruct((n, mix), _BF16),
                   jax.ShapeDtypeStruct((kv_heads, n, LANES), _F32)],
        compiler_params=pltpu.CompilerParams(dimension_semantics=("parallel", "parallel")),
    )(q, k, v, proj)


def _attention_backward(q, k, v, attn_o, dy, proj, lse, after, l, tq, name):
    heads, t, _ = q.shape
    kv_heads = k.shape[0]
    n = t - l
    rows = GQA_GROUP * tq
    gw = GQA_GROUP * HEAD_DIM
    aw = heads * HEAD_DIM
    gate_col = (aw + 2 * kv_heads * HEAD_DIM) // gw
    off = l // tq
    n_parts = 2 if t % (2 * BF16_ROWS) == 0 else 1
    part = t // n_parts

    def body(q_ref, k_ref, v_ref, o_ref, dy_ref, g_ref, lse_ref, after_ref, dq_ref, dg_ref, dk_ref, dv_ref):
        del after_ref

        @pl.when(pl.program_id(1) == 0)
        def _():
            dk_ref[...] = jnp.zeros_like(dk_ref)
            dv_ref[...] = jnp.zeros_like(dv_ref)

        q4 = q_ref[...].reshape(rows, HEAD_DIM)
        do_parts, delta_parts, lse_parts = [], [], []
        lse_blk = lse_ref[0]
        for g in range(GQA_GROUP):
            cols = slice(g * HEAD_DIM, (g + 1) * HEAD_DIM)
            gate, og, dyg = g_ref[:, cols], o_ref[:, cols], dy_ref[:, cols]
            dog = dyg * _silu(gate)
            dg_ref[:, cols] = (dyg * og * _silu_grad(gate)).astype(_BF16)
            do_parts.append(dog)
            delta_parts.append(jnp.sum(dog * og, axis=-1, keepdims=True))
            lse_parts.append(lse_blk[:, g:g + 1])
        do4 = jnp.concatenate(do_parts, axis=0).astype(_BF16)
        delta4 = jnp.concatenate(delta_parts, axis=0)
        lse4 = jnp.concatenate(lse_parts, axis=0)
        dq4 = jnp.zeros((rows, HEAD_DIM), _F32)
        for part_i in range(n_parts):
            keys = slice(part_i * part, (part_i + 1) * part)
            ks, vs = k_ref[0, keys, :], v_ref[0, keys, :]
            s = lax.dot_general(q4, ks, (((1,), (1,)), ((), ())), preferred_element_type=_F32) * ATTN_SCALE
            p = jnp.exp(s - lse4)
            dp = lax.dot_general(do4, vs, (((1,), (1,)), ((), ())), preferred_element_type=_F32)
            ds = (p * (dp - delta4) * ATTN_SCALE).astype(_BF16)
            dq4 = dq4 + jnp.dot(ds, ks, preferred_element_type=_F32)
            dk_ref[0, keys, :] += lax.dot_general(ds, q4, (((0,), (0,)), ((), ())), preferred_element_type=_F32)
            dv_ref[0, keys, :] += lax.dot_general(
                p.astype(_BF16), do4, (((0,), (0,)), ((), ())), preferred_element_type=_F32)
        dq_ref[...] = dq4.reshape(GQA_GROUP, tq, HEAD_DIM)

    kv_spec = pl.BlockSpec((1, t, HEAD_DIM), lambda h, i: (h, 0, 0))
    tok = pl.BlockSpec((tq, gw), lambda h, i: (i, h))
    return pl.pallas_call(
        body, name=name, grid=(kv_heads, n // tq),
        in_specs=[pl.BlockSpec((GQA_GROUP, tq, HEAD_DIM), lambda h, i: (h, i + off, 0)), kv_spec, kv_spec,
                  tok, tok, pl.BlockSpec((tq, gw), lambda h, i: (i + off, gate_col + h)),
                  pl.BlockSpec((1, tq, LANES), lambda h, i: (h, i, 0)),
                  pl.BlockSpec(after.shape, lambda h, i: (0, 0))],
        out_specs=[pl.BlockSpec((GQA_GROUP, tq, HEAD_DIM), lambda h, i: (h, i, 0)), tok, kv_spec, kv_spec],
        out_shape=[jax.ShapeDtypeStruct((heads, n, HEAD_DIM), _F32), jax.ShapeDtypeStruct((n, aw), _BF16),
                   jax.ShapeDtypeStruct((kv_heads, t, HEAD_DIM), _F32), jax.ShapeDtypeStruct((kv_heads, t, HEAD_DIM), _F32)],
        compiler_params=pltpu.CompilerParams(dimension_semantics=("parallel", "arbitrary")),
    )(q, k, v, attn_o, dy, proj, lse, after)


def _halo_specs(tp, width, col, row_off, total_rows):
    per = tp // POOL_HALO
    first = row_off // POOL_HALO
    last = total_rows // POOL_HALO - 1
    return [pl.BlockSpec((tp, width), lambda i: (i + row_off // tp, col)),
            pl.BlockSpec((POOL_HALO, width), lambda i: (jnp.maximum(first + i * per - 1, 0), col)),
            pl.BlockSpec((POOL_HALO, width), lambda i: (jnp.minimum(first + (i + 1) * per, last), col))]


def _with_halo(cur, prev, nxt, t0, n):
    tp = cur.shape[0]
    r8 = lax.broadcasted_iota(jnp.int32, (POOL_HALO, 1), 0)
    prev = jnp.where(t0 - POOL_HALO + r8 >= 0, prev, 0.0)
    nxt = jnp.where(t0 + tp + r8 < n, nxt, 0.0)
    return jnp.concatenate([prev, cur, nxt], axis=0)


def _shift_rows(a, s):
    return pltpu.roll(a, s % a.shape[0], 0)


def _window_sum(e, w, mirrored):
    a = e + _shift_rows(e, -1 if mirrored else 1)
    s = 1
    while 2 * s < w:
        a = _shift_rows(a, s) + _shift_rows(a, -s)
        s *= 2
    return a


def _window_count(t, w, n):
    half = w // 2
    return (jnp.minimum(t + half, n) - jnp.maximum(t - half, 0)).astype(_F32)


def _pool_forward(gi, proj, y, pool_w, pool_scale, l, heads, kv_heads, tp, name):
    t = proj.shape[0]
    n = t - l
    pg = pool_w.shape[-1]
    w = POOL_WINDOWS[gi]
    aw, kw = heads * HEAD_DIM, kv_heads * HEAD_DIM
    u_col = (2 * aw + 2 * kw) // pg + gi
    gate_col = (2 * aw + 2 * kw + len(POOL_WINDOWS) * pg) // pg + gi

    def body(u_ref, up_ref, un_ref, g_ref, w_ref, sc_ref, y_in_ref, y_ref, raw_ref, d_ref):
        del y_in_ref
        t0 = pl.program_id(0) * tp
        cur = u_ref[...]
        win = _window_sum(_with_halo(cur, up_ref[...], un_ref[...], t0, n), w, False)[POOL_HALO:POOL_HALO + tp]
        tok = t0 + lax.broadcasted_iota(jnp.int32, (tp, 1), 0)
        d = (win / _window_count(tok, w, n) - cur).astype(_BF16)
        raw = jnp.dot(d, w_ref[...].reshape(pg, pg), preferred_element_type=_F32)
        d_ref[...] = d
        raw_ref[...] = raw
        y_ref[...] = ((raw * sc_ref[...]) * _silu(g_ref[...])).astype(_BF16)

    blk = pl.BlockSpec((tp, pg), lambda i: (i, 0))
    return pl.pallas_call(
        body, name=name, grid=(n // tp,),
        in_specs=_halo_specs(tp, pg, u_col, l, t) + [
            pl.BlockSpec((tp, pg), lambda i: (i + l // tp, gate_col)),
            pl.BlockSpec((N_DEV, 1, pg // N_DEV, pg), lambda i: (0, gi, 0, 0)),
            pl.BlockSpec((1, pg), lambda i: (0, gi)), _ANY],
        out_specs=[pl.BlockSpec((tp, pg), lambda i: (i, aw // pg + gi)), blk, blk],
        out_shape=[jax.ShapeDtypeStruct(y.shape, y.dtype), jax.ShapeDtypeStruct((n, pg), _F32),
                   jax.ShapeDtypeStruct((n, pg), _BF16)],
        input_output_aliases={6: 0},
        compiler_params=pltpu.CompilerParams(dimension_semantics=("arbitrary",)),
    )(proj, proj, proj, proj, pool_w, pool_scale, y)


def _pool_backward_gate(gi, dy, proj, raw, pool_w, pool_scale, l, heads, kv_heads, tp, name):
    n, pg = raw.shape
    aw, kw = heads * HEAD_DIM, kv_heads * HEAD_DIM
    gate_col = (2 * aw + 2 * kw + len(POOL_WINDOWS) * pg) // pg + gi

    def body(dy_ref, g_ref, raw_ref, w_ref, sc_ref, dg_ref, dr_ref, dd_ref, ds_ref):
        @pl.when(pl.program_id(0) == 0)
        def _():
            ds_ref[...] = jnp.zeros_like(ds_ref)

        gate, rawv, dyv, scale = g_ref[...], raw_ref[...], dy_ref[...], sc_ref[...]
        dpool = dyv * _silu(gate)
        dg_ref[...] = (dyv * (rawv * scale) * _silu_grad(gate)).astype(_BF16)
        ds_ref[0:1, :] += jnp.sum(dpool * rawv, axis=0, keepdims=True)
        draw = (dpool * scale).astype(_BF16)
        dr_ref[...] = draw
        dd_ref[...] = lax.dot_general(
            draw, w_ref[...].reshape(pg, pg), (((1,), (1,)), ((), ())), preferred_element_type=_F32)

    blk = pl.BlockSpec((tp, pg), lambda i: (i, 0))
    return pl.pallas_call(
        body, name=name, grid=(n // tp,),
        in_specs=[pl.BlockSpec((tp, pg), lambda i: (i, aw // pg + gi)),
                  pl.BlockSpec((tp, pg), lambda i: (i + l // tp, gate_col)), blk,
                  pl.BlockSpec((N_DEV, 1, pg // N_DEV, pg), lambda i: (0, gi, 0, 0)),
                  pl.BlockSpec((1, pg), lambda i: (0, gi))],
        out_specs=[blk, blk, blk, pl.BlockSpec((SUBLANES, pg), lambda i: (0, 0))],
        out_shape=[jax.ShapeDtypeStruct((n, pg), _BF16), jax.ShapeDtypeStruct((n, pg), _BF16),
                   jax.ShapeDtypeStruct((n, pg), _F32), jax.ShapeDtypeStruct((SUBLANES, pg), _F32)],
        compiler_params=pltpu.CompilerParams(dimension_semantics=("arbitrary",)),
    )(dy, proj, raw, pool_w, pool_scale)


def _pool_backward_window(gi, dd, tp, name):
    n, pg = dd.shape
    w = POOL_WINDOWS[gi]

    def body(c_ref, p_ref, n_ref, du_ref):
        t0 = pl.program_id(0) * tp
        cur = c_ref[...]
        e = _with_halo(cur, p_ref[...], n_ref[...], t0, n)
        tok = t0 - POOL_HALO + lax.broadcasted_iota(jnp.int32, (tp + 2 * POOL_HALO, 1), 0)
        e = e / jnp.maximum(_window_count(tok, w, n), 1.0)
        du_ref[...] = (_window_sum(e, w, True)[POOL_HALO:POOL_HALO + tp] - cur).astype(_BF16)

    return pl.pallas_call(
        body, name=name, grid=(n // tp,),
        in_specs=_halo_specs(tp, pg, 0, 0, n), out_specs=pl.BlockSpec((tp, pg), lambda i: (i, 0)),
        out_shape=jax.ShapeDtypeStruct((n, pg), _BF16),
        compiler_params=pltpu.CompilerParams(dimension_semantics=("parallel",)),
    )(dd, dd, dd)


def _post(out, x, target, gate, g_post, tr, name):
    n, d = out.shape

    def body(o_ref, x_ref, t_ref, gate_ref, g_ref, dxn_ref, do_ref, dgate_ref, dg_ref, loss_ref):
        @pl.when(pl.program_id(0) == 0)
        def _():
            dgate_ref[...] = jnp.zeros_like(dgate_ref)
            dg_ref[...] = jnp.zeros_like(dg_ref)
            loss_ref[...] = jnp.zeros_like(loss_ref)

        ov = o_ref[...]
        r = _rms(ov)
        on = ov * r
        normed = on * g_ref[...]
        err = (x_ref[...] + gate_ref[...] * normed) - t_ref[...]
        loss_ref[...] += jnp.sum(err * err)
        dxn = err / d
        dxn_ref[...] = dxn
        dgate_ref[0:1, :] += jnp.sum(dxn * normed, axis=0, keepdims=True)
        dr = dxn * gate_ref[...]
        dg_ref[0:1, :] += jnp.sum(dr * on, axis=0, keepdims=True)
        dn = dr * g_ref[...]
        do_ref[...] = (r * (dn - on * jnp.mean(dn * on, axis=-1, keepdims=True))).astype(_BF16)

    blk = pl.BlockSpec((tr, d), lambda i: (i, 0))
    vec = pl.BlockSpec((1, d), lambda i: (0, 0))
    acc = pl.BlockSpec((SUBLANES, d), lambda i: (0, 0))
    return pl.pallas_call(
        body, name=name, grid=(n // tr,),
        in_specs=[blk, blk, blk, vec, vec],
        out_specs=[blk, blk, acc, acc, pl.BlockSpec((SUBLANES, LANES), lambda i: (0, 0))],
        out_shape=[jax.ShapeDtypeStruct((n, d), _F32), jax.ShapeDtypeStruct((n, d), _BF16),
                   jax.ShapeDtypeStruct((SUBLANES, d), _F32), jax.ShapeDtypeStruct((SUBLANES, d), _F32),
                   jax.ShapeDtypeStruct((SUBLANES, LANES), _F32)],
        compiler_params=pltpu.CompilerParams(dimension_semantics=("arbitrary",)),
    )(out, x, target, gate, g_post)


def _adam_sharded(slab_ids, grad, got, far, w, m, v, name):
    r, c = w.shape
    tr = _tile(r, max(BF16_ROWS, min(256, (1 << 18) // c)), BF16_ROWS)

    def body(ids_ref, own_ref, got_ref, far_ref, w_ref, m_ref, v_ref, g_ref, dl_ref, nm_ref, nv_ref):
        del ids_ref
        g = own_ref[0] + got_ref[0]
        for k in range(N_CHIPS - 1):
            g = g + far_ref[k].astype(_F32)
        delta, nm, nv = _adamw(w_ref[...], g, m_ref[...], v_ref[...])
        g_ref[...] = g
        dl_ref[...] = delta
        nm_ref[...] = nm
        nv_ref[...] = nv

    blk = pl.BlockSpec((tr, c), lambda i, ids: (i, 0))
    return pl.pallas_call(
        body, name=name,
        grid_spec=pltpu.PrefetchScalarGridSpec(
            num_scalar_prefetch=1, grid=(r // tr,),
            in_specs=[pl.BlockSpec((1, tr, c), lambda i, ids: (ids[0], i, 0)),
                      pl.BlockSpec((1, tr, c), lambda i, ids: (0, i, 0)),
                      pl.BlockSpec((N_CHIPS - 1, tr, c), lambda i, ids: (0, i, 0)), blk, blk, blk],
            out_specs=[blk] * 4),
        out_shape=[jax.ShapeDtypeStruct((r, c), _F32)] * 4,
        compiler_params=pltpu.CompilerParams(dimension_semantics=("parallel",)),
    )(slab_ids, grad, got, far, w, m, v)


def _adam_replicated(parts, extra, through_silu, w, m, v, name):
    def body(p_ref, e_ref, s_ref, w_ref, m_ref, v_ref, g_ref, dl_ref, nm_ref, nv_ref):
        total = p_ref[0] + e_ref[0]
        for dev in range(1, N_DEV):
            total = total + (p_ref[dev] + e_ref[dev])
        g = jnp.where(s_ref[...] > 0.5, total * _silu_grad(w_ref[...]), total)
        delta, nm, nv = _adamw(w_ref[...], g, m_ref[...], v_ref[...])
        g_ref[...] = g
        dl_ref[...] = delta
        nm_ref[...] = nm
        nv_ref[...] = nv

    return pl.pallas_call(
        body, name=name, in_specs=[_VMEM] * 6, out_specs=[_VMEM] * 4,
        out_shape=[jax.ShapeDtypeStruct(w.shape, _F32)] * 4,
    )(parts, extra, through_silu, w, m, v)


def _as_rows(vec):
    size = vec.shape[0]
    padded = -(-size // (SUBLANES * LANES)) * SUBLANES * LANES
    return jnp.pad(vec, (0, padded - size)).reshape(padded // LANES, LANES)


def kernel(x, c, ctx, c_ctx, w_ada, b_ada, norm_pre, norm_post, w_in, q_norm, k_norm, pool_w, pool_scale, w_out, loss_target, m_c_ctx, m_w_ada, m_b_ada, m_norm_pre, m_norm_post, m_w_in, m_q_norm, m_k_norm, m_pool_w, m_pool_scale, m_w_out, v_c_ctx, v_w_ada, v_b_ada, v_norm_pre, v_norm_post, v_w_in, v_q_norm, v_k_norm, v_pool_w, v_pool_scale, v_w_out):
    me = _dev_index(*_position())
    x2, ctx2, target = x[0], ctx[0], loss_target[0]
    n, d = x2.shape
    l = ctx2.shape[0]
    t = l + n
    aw = d // 2
    heads = aw // HEAD_DIM
    kv_heads = heads // GQA_GROUP
    kw = kv_heads * HEAD_DIM
    n_groups = len(POOL_WINDOWS)
    pg = (d - aw) // n_groups
    mix = d
    tr = _tile(l, 128, BF16_ROWS)
    tq = _tile(l, 128, BF16_ROWS)
    tp = _tile(l, 512, POOL_HALO)

    xi, yi, ci = _position()
    slab_ids = jnp.stack([_dev_index(*chip, ci) for chip in _chip_order(xi, yi)]).astype(jnp.int32)

    cw = w_in.shape[-1]
    wg = _cast_into_columns(slab_ids, w_in[0], N_DEV, "cast_w_in")
    hop_a = _w_in_hop(wg, cw, [], "a", norm_pre, "gather_w_in_a")
    late = [_cast_into_slab(slab_ids, w_out[0], "cast_w_out"),
            _cast_into_slab(slab_ids, pool_w[0].reshape(-1, pg), "cast_pool_w")]

    c_all = _all_gather_small(_as_rows(c[0]), "gather_c").reshape(N_DEV, -1)[:, :d]
    craw = jnp.concatenate([c_all, jnp.broadcast_to(c_ctx[None], (N_DEV, d))], axis=0)
    ada = _ada_forward(craw, w_ada[0], "ada_forward")
    ada_all = _all_gather_small(ada, "gather_ada")
    mod_all = ada_all.transpose(1, 0, 2).reshape(ada.shape[0], -1) + b_ada[0]
    mod = lax.dynamic_index_in_dim(mod_all, me, 0, keepdims=False)
    mod_c = mod_all[N_DEV]
    shift, scale, gate = mod[:d], mod[d:2 * d], mod[2 * d:]
    zeros6 = jnp.zeros((SUBLANES - 2, d), _F32)
    mods = jnp.stack([jnp.concatenate([mod_c[None, d:2 * d], mod_c[None, :d], zeros6], axis=0),
                      jnp.concatenate([scale[None], shift[None], zeros6], axis=0)])

    h_all = _prenorm(ctx2, x2, norm_pre, mods, tr, "prenorm")
    order_ids = jnp.stack([_dev_index(*dev) for dev in _w_in_order(xi, yi, ci)]).astype(jnp.int32)
    proj = lax.empty((t, N_DEV * cw), _F32)
    a_s, a_r, wg, tok = hop_a
    b_s, b_r, wg, tok = _w_in_hop(wg, cw, [("a", a_s, a_r, [0, 1], [])], "b", h_all, "gather_w_in_b")
    proj = _proj_blocks(h_all, wg, proj, order_ids, 0, 3, cw, tok, "proj_0")
    c_s, c_r, wg, tok = _w_in_hop(wg, cw, [("b", b_s, b_r, [2, 0], [])], "c", proj, "gather_w_in_c")
    proj = _proj_blocks(h_all, wg, proj, order_ids, 3, 2, cw, tok, "proj_1")
    d_s, d_r, wg, tok = _w_in_hop(
        wg, cw, [("c", c_s, c_r, [0], []), ("b", b_s, b_r, [1], [])], "d", proj, "gather_w_in_d")
    proj = _proj_blocks(h_all, wg, proj, order_ids, 5, 2, cw, tok, "proj_2")
    w_in_g, tok = _w_in_hop(
        wg, cw, [("d", d_s, d_r, [0], [0]), ("a", a_s, a_r, [], [0, 1]), ("b", b_s, b_r, [], [0, 1, 2]),
                 ("c", c_s, c_r, [], [0])], None, proj, "gather_w_in_end")
    flight_w = _gather_slabs_start(late, w_in_g, "gather_late_start")
    proj = _proj_blocks(h_all, w_in_g, proj, order_ids, 7, 1, cw, flight_w[-1], "proj_3")
    tables = _rope_tables(l, n)
    q, k, v = _qkv_post(proj, tables, q_norm, k_norm, heads, kv_heads, tr, "qkv_post")
    attn_o, y, lse = _attention(q, k, v, proj, l, mix, tq, "attention")
    w_out_g8, pool_g8 = _gather_slabs_wait(*flight_w[:3], attn_o, "gather_late_wait")
    w_out_g = w_out_g8.reshape(mix, d)
    pool_g = pool_g8.reshape(N_DEV, n_groups, pg // N_DEV, pg)
    raws, ds = [], []
    for gi in range(n_groups):
        y, raw, dsave = _pool_forward(gi, proj, y, pool_g, pool_scale, l, heads, kv_heads, tp, f"pool_forward_{gi}")
        raws.append(raw)
        ds.append(dsave)
    out = _matmul(y, w_out_g, name="out_proj")
    dxn, dout, dgate8, dgpost8, loss8 = _post(out, x2, target, gate[None], norm_post, tr, "post")

    dy = _matmul(dout, w_out_g, tb=True, name="d_y")
    gw_out = _matmul(y, dout, ta=True, name="grad_w_out").reshape(N_DEV, mix // N_DEV, d)
    got_out = _exchange_sibling([gw_out], "exchange_sibling_w_out")[0]
    sum_out = _pre_add(slab_ids, gw_out, got_out, "pre_add_w_out")
    flight_out = _exchange_start(_chip_copies, [sum_out], "exchange_chips_start_w_out")
    dq, dgate_attn, dk, dv = _attention_backward(
        q, k, v, attn_o, dy, proj, lse, flight_out[-1], l, tq, "attention_backward")
    dqkv, dgq8, dgk8 = _qkv_post_backward(proj, dq, dk, dv, tables, q_norm, k_norm, l, tr, "qkv_post_backward")
    dus, dgps, gpw, dps8 = [], [], [], []
    for gi in range(n_groups):
        dgp, draw, dd, dps = _pool_backward_gate(
            gi, dy, proj, raws[gi], pool_g, pool_scale, l, heads, kv_heads, tp, f"pool_backward_gate_{gi}")
        dus.append(_pool_backward_window(gi, dd, tp, f"pool_backward_window_{gi}"))
        dgps.append(dgp)
        dps8.append(dps)
        gpw.append(_matmul(ds[gi], draw, ta=True, name=f"grad_pool_w_{gi}"))
    latent_cols = jnp.concatenate([dgate_attn] + dus + dgps, axis=1)
    dproj = jnp.concatenate([dqkv, jnp.pad(latent_cols, ((l, 0), (0, 0)))], axis=1)
    cw = w_in.shape[-1]
    other_ids = jnp.stack([_dev_index(*chip, 1 - ci) for chip in _chip_order(xi, yi)]).astype(jnp.int32)
    chip_slabs = jnp.arange(N_CHIPS, dtype=jnp.int32)
    pr = pool_w.shape[2]
    gpw8 = jnp.stack(gpw).reshape(n_groups, N_DEV, pr, pg).transpose(1, 0, 2, 3).reshape(N_DEV, n_groups * pr, pg)
    give_in = _matmul_slabs(h_all, dproj, other_ids, cw, "grad_w_in_sibling")
    flight_sib = _exchange_start(_sibling_copies, [give_in, jnp.take(gpw8, other_ids, axis=0)], "exchange_sibling_start")
    gw_in = _matmul_slabs(h_all, dproj, slab_ids, cw, "grad_w_in_own", after=flight_sib[-1])
    gpw_own = jnp.take(gpw8, slab_ids, axis=0)
    got_in, got_pw = _exchange_wait(_sibling_copies, *flight_sib[:4], gw_in, "exchange_sibling_wait")
    sums_in = [_pre_add(chip_slabs, gw_in, got_in, "pre_add_w_in"), _pre_add(chip_slabs, gpw_own, got_pw, "pre_add_pool_w")]
    flight_in = _exchange_start(_chip_copies, sums_in, "exchange_chips_start_w_in")
    dh = _matmul(dproj, w_in_g, tb=True, tm=1088, tk=3072, after=flight_in[-1], name="d_h")
    grad_x, dmods, dgpre8 = _prenorm_backward(dh, ctx2, x2, dxn, norm_pre, mods, tr, "prenorm_backward")

    dmod_lat = jnp.concatenate([dmods[1, 1], dmods[1, 0], dgate8[0]])
    dmod_ctx = jnp.concatenate([dmods[0, 1], dmods[0, 0], jnp.zeros((d,), _F32)])
    small = jnp.concatenate([dmod_lat, dmod_ctx, dgpre8[0], dgpost8[0], dgq8[0], dgk8[0]] + [p[0] for p in dps8]
                            + [loss8[0, :1]])
    gathered = _all_gather_small(_as_rows(small), "gather_small").reshape(N_DEV, -1)
    o = 0
    take = lambda size: (gathered[:, o:o + size], o + size)
    g_mod, o = take(3 * d)
    g_modc, o = take(3 * d)
    g_pre, o = take(d)
    g_post, o = take(d)
    g_q, o = take(HEAD_DIM)
    g_k, o = take(HEAD_DIM)
    g_ps, o = take(n_groups * pg)
    g_loss, o = take(1)
    cols = w_ada.shape[-1]
    mine = lambda a: lax.dynamic_slice_in_dim(a, me * cols, cols, axis=1)
    dmod_rows = jnp.concatenate([mine(g_mod), mine(g_modc)], axis=0)
    g_wada, dl_wada, nm_wada, nv_wada, dcact = _ada_backward(craw, dmod_rows, w_ada[0], m_w_ada[0], v_w_ada[0], "ada_backward")
    dcc = _all_gather_small(_as_rows(dcact[0]), "gather_dcc").reshape(N_DEV, -1)[:, :d]

    sizes = [d, 3 * d, d, d, HEAD_DIM, HEAD_DIM, n_groups * pg]
    def pack(parts):
        rows = jnp.concatenate(parts, axis=1)
        padded = -(-rows.shape[1] // (SUBLANES * LANES)) * SUBLANES * LANES
        return jnp.pad(rows, ((0, 0), (0, padded - rows.shape[1]))).reshape(N_DEV, padded // LANES, LANES)

    zero = lambda size: jnp.zeros((N_DEV, size), _F32)
    parts = pack([dcc, g_mod, g_pre, g_post, g_q, g_k, g_ps])
    extra = pack([zero(d), g_modc, zero(d), zero(d), zero(HEAD_DIM), zero(HEAD_DIM), zero(n_groups * pg)])
    through_silu = _as_rows(jnp.concatenate([jnp.ones((d,), _F32), jnp.zeros((sum(sizes[1:]),), _F32)]))
    cat = lambda items: _as_rows(jnp.concatenate([a.reshape(-1) for a in items]))
    ws = [c_ctx, b_ada, norm_pre, norm_post, q_norm, k_norm, pool_scale]
    ms = [m_c_ctx, m_b_ada, m_norm_pre, m_norm_post, m_q_norm, m_k_norm, m_pool_scale]
    vs = [v_c_ctx, v_b_ada, v_norm_pre, v_norm_post, v_q_norm, v_k_norm, v_pool_scale]
    rep = _adam_replicated(parts, extra, through_silu, cat(ws), cat(ms), cat(vs), "adam_replicated")

    def split(packed):
        flat_, outs, at = packed.reshape(-1), [], 0
        for w, size in zip(ws, sizes):
            outs.append(flat_[at:at + size].reshape(w.shape))
            at += size
        return outs

    g_rep, dl_rep, nm_rep, nv_rep = [split(r) for r in rep]

    far_out = _exchange_wait(_chip_copies, *flight_out[:4], grad_x, "exchange_chips_wait_w_out")[0]
    far_in, far_pw = _exchange_wait(_chip_copies, *flight_in[:4], rep[0], "exchange_chips_wait_w_in")
    two = lambda a: a.reshape(-1, a.shape[-1])
    sharded = []
    for ids, g, got, far, w, m, v_, name in zip(
            (chip_slabs, slab_ids, chip_slabs), (gw_in, gw_out, gpw_own), (got_in, got_out, got_pw),
            (far_in, far_out, far_pw), (w_in, w_out, pool_w), (m_w_in, m_w_out, m_pool_w),
            (v_w_in, v_w_out, v_pool_w), ("adam_w_in", "adam_w_out", "adam_pool_w")):
        res = _adam_sharded(ids, g, got, far, two(w), two(m), two(v_), name)
        sharded.append([r.reshape(w.shape) for r in res])
    (g_w_in, dl_w_in, nm_w_in, nv_w_in), (g_w_out, dl_w_out, nm_w_out, nv_w_out), (g_pw, dl_pw, nm_pw, nv_pw) = sharded

    loss_sum = g_loss[0, 0]
    for dev in range(1, N_DEV):
        loss_sum = loss_sum + g_loss[dev, 0]
    loss = (0.5 / d) * loss_sum

    def ordered(rep_list, ada_, w_in_, pw_, w_out_):
        return [rep_list[0], ada_[None], rep_list[1], rep_list[2], rep_list[3], w_in_, rep_list[4], rep_list[5],
                pw_, rep_list[6], w_out_]

    return (loss, grad_x[None],
            *ordered(g_rep, g_wada, g_w_in, g_pw, g_w_out),
            *ordered(dl_rep, dl_wada, dl_w_in, dl_pw, dl_w_out),
            *ordered(nm_rep, nm_wada, nm_w_in, nm_pw, nm_w_out),
            *ordered(nv_rep, nv_wada, nv_w_in, nv_pw, nv_w_out))
```

```python
import functools

import jax
import jax.numpy as jnp
from jax import lax
from jax.experimental import pallas as pl
from jax.experimental.pallas import tpu as pltpu

HEAD_DIM = 128
GQA_GROUP = 4
ATTN_SUB_HEADS = 2
ATTN_KEY_PARTS = 2
LOG2_E = 1.4426950408889634
GRID_W = 64
ROPE_PAIRS = HEAD_DIM // 4
ROPE_THETA = 10000.0
ATTN_SCALE = HEAD_DIM ** -0.5
EPS = 1e-6
POOL_WINDOWS = (2, 4, 8, 16)
POOL_HALO = 8
N_DEV = 8
N_CHIPS = 4
ADAM_LR = 0.001
ADAM_B1 = 0.9
ADAM_B2 = 0.999
ADAM_EPS = 1e-08
ADAM_WD = 0.01
ADAM_STEP = 10

LANES = 128
SUBLANES = 8
BF16_ROWS = 16

_MESH = pl.DeviceIdType.MESH
_ANY = pl.BlockSpec(memory_space=pl.ANY)
_VMEM = pl.BlockSpec(memory_space=pltpu.VMEM)
_HBM = pl.BlockSpec(memory_space=pltpu.HBM)
_SEM = pl.BlockSpec(memory_space=pltpu.SEMAPHORE)
_EFFECT = pltpu.SideEffectType.DATAFLOW_SIDE_EFFECTING
_F32 = jnp.float32
_BF16 = jnp.bfloat16


def _tile(dim, pref, align):
    t = min(pref, dim)
    t -= t % align
    while t >= align:
        if dim % t == 0:
            return t
        t -= align
    return dim


def _position():
    return lax.axis_index("x"), lax.axis_index("y"), lax.axis_index("c")


def _flip(v, bit):
    return 1 - v if bit else v


def _dev_index(x, y, c):
    return 4 * x + 2 * y + c


def _silu(g):
    return g * jax.nn.sigmoid(g)


def _silu_grad(g):
    s = jax.nn.sigmoid(g)
    return s * (1.0 + g * (1.0 - s))


def _adamw(w, g, m, v):
    m = ADAM_B1 * m + (1.0 - ADAM_B1) * g
    v = ADAM_B2 * v + (1.0 - ADAM_B2) * (g * g)
    m_hat = m / (1.0 - ADAM_B1 ** ADAM_STEP)
    v_hat = v / (1.0 - ADAM_B2 ** ADAM_STEP)
    delta = -ADAM_LR * (m_hat / (jnp.sqrt(v_hat) + ADAM_EPS) + ADAM_WD * w)
    return delta, m, v


def _all_gather_small(v, name):
    rows, cols = v.shape

    def body(v_ref, out_ref, send_sems, recv_sems):
        x, y, c = _position()
        me = _dev_index(x, y, c)
        out_ref[me] = v_ref[...]
        peers = [(_flip(x, k & 4), _flip(y, k & 2), _flip(c, k & 1)) for k in range(1, N_DEV)]

        def copy(k, block, to):
            return pltpu.make_async_remote_copy(
                src_ref=v_ref, dst_ref=out_ref.at[block], send_sem=send_sems.at[k], recv_sem=recv_sems.at[k],
                device_id=to, device_id_type=_MESH)

        sends = [copy(k, me, p) for k, p in enumerate(peers)]
        for s in sends:
            s.start()
        for k, p in enumerate(peers):
            copy(k, _dev_index(*p), p).wait_recv()
        for s in sends:
            s.wait_send()

    return pl.pallas_call(
        body, name=name,
        out_shape=jax.ShapeDtypeStruct((N_DEV, rows, cols), v.dtype),
        in_specs=[_VMEM], out_specs=_VMEM,
        scratch_shapes=[pltpu.SemaphoreType.DMA((N_DEV - 1,)), pltpu.SemaphoreType.DMA((N_DEV - 1,))],
    )(v)


def _route(x, y, c):
    first = (x + (1 - c) * (1 - 2 * x), y + c * (1 - 2 * y))
    second = (x + c * (1 - 2 * x), y + (1 - c) * (1 - 2 * y))
    return first, second, (1 - x, 1 - y)


def _w_in_order(x, y, c):
    first, second, diagonal = _route(x, y, c)
    return [(x, y, c), (x, y, 1 - c), (*first, c), (*second, 1 - c), (*second, c), (*first, 1 - c),
            (*diagonal, c), (*diagonal, 1 - c)]


W_IN_HOPS = {"a": 2, "b": 3, "c": 1, "d": 1}


def _w_in_hop_copies(group, wg, width, send_sems, recv_sems):
    x, y, c = _position()
    me, sibling = (x, y, c), (x, y, 1 - c)
    first, second, diagonal = _route(x, y, c)

    def cp(k, block, to):
        cols = wg.at[:, pl.ds(pl.multiple_of(_dev_index(*block) * width, width), width)]
        return pltpu.make_async_remote_copy(
            src_ref=cols, dst_ref=cols, send_sem=send_sems.at[k], recv_sem=recv_sems.at[k],
            device_id=to, device_id_type=_MESH)

    if group == "a":
        return [cp(0, me, sibling), cp(1, me, (*first, c))]
    if group == "b":
        return [cp(0, me, (*second, c)), cp(1, (*first, c), (*second, c)), cp(2, (*first, c), sibling)]
    if group == "c":
        return [cp(0, (*second, c), sibling)]
    return [cp(0, (*diagonal, c), sibling)]


def _w_in_hop(wg, width, waits, start, after, name):
    n_sem = 2 * len(waits)

    def body(*refs):
        wg_ref = refs[0]
        for i, (group, _, _, arrivals, sends) in enumerate(waits):
            cps = _w_in_hop_copies(group, wg_ref, width, refs[1 + 2 * i], refs[2 + 2 * i])
            for k in arrivals:
                cps[k].wait_recv()
            for k in sends:
                cps[k].wait_send()
        if start:
            for cp in _w_in_hop_copies(start, wg_ref, width, refs[n_sem + 2], refs[n_sem + 3]):
                cp.start()
        refs[-1][...] = jnp.zeros_like(refs[-1])

    sems = [s for w in waits for s in w[1:3]]
    new = [pltpu.SemaphoreType.DMA((W_IN_HOPS[start],))] * 2 if start else []
    outs = pl.pallas_call(
        body, name=name,
        out_shape=(*new, pltpu.HBM(wg.shape, wg.dtype), jax.ShapeDtypeStruct((SUBLANES, LANES), _F32)),
        in_specs=[_HBM] + [_SEM] * n_sem + [_ANY], out_specs=(*[_SEM] * len(new), _HBM, _VMEM),
        input_output_aliases={0: len(new)},
        compiler_params=pltpu.CompilerParams(has_side_effects=_EFFECT),
    )(pltpu.with_memory_space_constraint(wg, pltpu.HBM), *sems, after)
    return outs


def _slab_copies(bufs, send_sems, recv_sems):
    x, y, c = _position()
    me = _dev_index(x, y, c)
    peers = [(_flip(x, k & 4), _flip(y, k & 2), _flip(c, k & 1)) for k in range(1, N_DEV)]
    return [pltpu.make_async_remote_copy(
        src_ref=buf.at[me], dst_ref=buf.at[me],
        send_sem=send_sems.at[(N_DEV - 1) * a + k], recv_sem=recv_sems.at[(N_DEV - 1) * a + k],
        device_id=peer, device_id_type=_MESH)
        for a, buf in enumerate(bufs) for k, peer in enumerate(peers)]


def _gather_slabs_start(bufs, after, name):
    n = len(bufs)
    n_copies = (N_DEV - 1) * n

    def body(*refs):
        send_sems, recv_sems, token = refs[n + 1], refs[n + 2], refs[-1]
        for cp in _slab_copies(refs[:n], send_sems, recv_sems):
            cp.start()
        token[...] = jnp.zeros_like(token)

    outs = pl.pallas_call(
        body, name=name,
        out_shape=(pltpu.SemaphoreType.DMA((n_copies,)), pltpu.SemaphoreType.DMA((n_copies,)),
                   *[pltpu.HBM(b.shape, b.dtype) for b in bufs], jax.ShapeDtypeStruct((SUBLANES, LANES), _F32)),
        in_specs=[_HBM] * n + [_ANY], out_specs=(_SEM, _SEM, *[_HBM] * n, _VMEM),
        input_output_aliases={i: 2 + i for i in range(n)},
        compiler_params=pltpu.CompilerParams(has_side_effects=_EFFECT),
    )(*[pltpu.with_memory_space_constraint(b, pltpu.HBM) for b in bufs], after)
    return outs[0], outs[1], list(outs[2:2 + n]), outs[-1]


def _gather_slabs_wait(send_sems, recv_sems, bufs, after, name):
    n = len(bufs)

    def body(*refs):
        for cp in _slab_copies(refs[:n], refs[n], refs[n + 1]):
            cp.wait_send()
            cp.wait_recv()

    outs = pl.pallas_call(
        body, name=name, out_shape=tuple(pltpu.HBM(b.shape, b.dtype) for b in bufs),
        in_specs=[_HBM] * n + [_SEM, _SEM, _ANY], out_specs=[_HBM] * n,
        input_output_aliases={i: i for i in range(n)},
        compiler_params=pltpu.CompilerParams(has_side_effects=_EFFECT),
    )(*bufs, send_sems, recv_sems, after)
    return list(outs)


def _chip_order(x, y):
    return [(x, y), (1 - x, y), (x, 1 - y), (1 - x, 1 - y)]


def _chip_copies(srcs, lands, send_sems, recv_sems):
    x, y, c = _position()
    return [pltpu.make_async_remote_copy(
        src_ref=srcs[a].at[k], dst_ref=lands[a].at[k],
        send_sem=send_sems.at[(N_CHIPS - 1) * a + k], recv_sem=recv_sems.at[(N_CHIPS - 1) * a + k],
        device_id=(*chip, c), device_id_type=_MESH)
        for a in range(len(srcs)) for k, chip in enumerate(_chip_order(x, y)[1:])]


def _sibling_copies(srcs, lands, send_sems, recv_sems):
    x, y, c = _position()
    return [pltpu.make_async_remote_copy(
        src_ref=srcs[a].at[s], dst_ref=lands[a].at[s],
        send_sem=send_sems.at[N_CHIPS * a + s], recv_sem=recv_sems.at[N_CHIPS * a + s],
        device_id=(x, y, 1 - c), device_id_type=_MESH)
        for a in range(len(srcs)) for s in range(N_CHIPS)]


def _sibling_copies_by_device(srcs, lands, send_sems, recv_sems):
    x, y, c = _position()
    return [pltpu.make_async_remote_copy(
        src_ref=srcs[a].at[_dev_index(*chip, 1 - c)], dst_ref=lands[a].at[s],
        send_sem=send_sems.at[N_CHIPS * a + s], recv_sem=recv_sems.at[N_CHIPS * a + s],
        device_id=(x, y, 1 - c), device_id_type=_MESH)
        for a in range(len(srcs)) for s, chip in enumerate(_chip_order(x, y))]


def _exchange_start(copies, sums, name, land_slabs=None):
    n = len(sums)
    land_shapes = [((land_slabs or s.shape[0]),) + s.shape[1:] for s in sums]
    n_copies = sum(shape[0] for shape in land_shapes)

    def body(*refs):
        srcs, lands = refs[:n], refs[n:2 * n]
        send_sems, recv_sems, token = refs[2 * n], refs[2 * n + 1], refs[-1]
        for cp in copies(srcs, lands, send_sems, recv_sems):
            cp.start()
        token[...] = jnp.zeros_like(token)

    hbm = [pltpu.HBM(s.shape, s.dtype) for s in sums] + [pltpu.HBM(shape, s.dtype) for shape, s in zip(land_shapes, sums)]
    outs = pl.pallas_call(
        body, name=name,
        out_shape=(pltpu.SemaphoreType.DMA((n_copies,)), pltpu.SemaphoreType.DMA((n_copies,)), *hbm,
                   jax.ShapeDtypeStruct((SUBLANES, LANES), _F32)),
        in_specs=[_HBM] * (2 * n), out_specs=(_SEM, _SEM, *[_HBM] * (2 * n), _VMEM),
        input_output_aliases={i: 2 + i for i in range(2 * n)},
        compiler_params=pltpu.CompilerParams(has_side_effects=_EFFECT),
    )(*[pltpu.with_memory_space_constraint(s, pltpu.HBM) for s in sums],
      *[pltpu.with_memory_space_constraint(lax.empty(shape, s.dtype), pltpu.HBM) for shape, s in zip(land_shapes, sums)])
    return outs[0], outs[1], list(outs[2:2 + n]), list(outs[2 + n:2 + 2 * n]), outs[-1]


def _exchange_wait(copies, send_sems, recv_sems, srcs, lands, after, name, with_sources=False):
    n = len(srcs)

    def body(*refs):
        for cp in copies(refs[:n], refs[n:2 * n], refs[2 * n], refs[2 * n + 1]):
            cp.wait_send()
            cp.wait_recv()

    hbm = [pltpu.HBM(s.shape, s.dtype) for s in (*srcs, *lands)]
    outs = pl.pallas_call(
        body, name=name, out_shape=tuple(hbm),
        in_specs=[_HBM] * (2 * n) + [_SEM, _SEM, _ANY], out_specs=[_HBM] * (2 * n),
        input_output_aliases={i: i for i in range(2 * n)},
        compiler_params=pltpu.CompilerParams(has_side_effects=_EFFECT),
    )(*srcs, *lands, send_sems, recv_sems, after)
    return list(outs) if with_sources else list(outs[n:])


def _matmul(a, b, *, ta=False, tb=False, out_dtype=_F32, tm=1024, tn=1024, tk=4608, col_slabs=None, after=None, name):
    kdim, m = a.shape if ta else a.shape[::-1]
    n = b.shape[0] if tb else b.shape[1]
    tm = _tile(m, tm, LANES if ta else BF16_ROWS)
    tn = n // col_slabs if col_slabs else _tile(n, tn, LANES)
    tk = _tile(kdim, tk, BF16_ROWS if ta else LANES)
    nk = kdim // tk
    dims = (((0 if ta else 1,), (1 if tb else 0,)), ((), ()))

    def body_whole_k(a_ref, b_ref, *rest):
        o_ref = rest[-1]
        part = lax.dot_general(a_ref[...], b_ref[...], dims, preferred_element_type=_F32)
        o_ref[...] = part.astype(out_dtype).reshape(o_ref.shape)

    def body_split_k(a_ref, b_ref, *rest):
        o_ref, acc_ref = rest[-2:]
        k = pl.program_id(2)

        @pl.when(k == 0)
        def _():
            acc_ref[...] = jnp.zeros_like(acc_ref)

        acc_ref[...] += lax.dot_general(a_ref[...], b_ref[...], dims, preferred_element_type=_F32)

        @pl.when(k == nk - 1)
        def _():
            o_ref[...] = acc_ref[...].astype(out_dtype).reshape(o_ref.shape)

    a_spec = pl.BlockSpec((tk, tm), lambda i, j, k: (k, i)) if ta else pl.BlockSpec((tm, tk), lambda i, j, k: (i, k))
    b_spec = pl.BlockSpec((tn, tk), lambda i, j, k: (j, k)) if tb else pl.BlockSpec((tk, tn), lambda i, j, k: (k, j))
    if col_slabs:
        out_spec = pl.BlockSpec((1, tm, tn), lambda i, j, k: (j, i, 0))
        out_shape = jax.ShapeDtypeStruct((col_slabs, m, tn), out_dtype)
    else:
        out_spec = pl.BlockSpec((tm, tn), lambda i, j, k: (i, j))
        out_shape = jax.ShapeDtypeStruct((m, n), out_dtype)
    extra = [] if after is None else [after]
    return pl.pallas_call(
        body_whole_k if nk == 1 else body_split_k, name=name, grid=(m // tm, n // tn, nk),
        in_specs=[a_spec, b_spec] + [pl.BlockSpec(t.shape, lambda i, j, k: (0, 0)) for t in extra],
        out_specs=out_spec, out_shape=out_shape,
        scratch_shapes=[] if nk == 1 else [pltpu.VMEM((tm, tn), _F32)],
        compiler_params=pltpu.CompilerParams(dimension_semantics=("parallel", "parallel", "arbitrary")),
    )(a, b, *extra)


def _proj_blocks(a, wg, dst, order_ids, first, count, width, after, name):
    m, kdim = a.shape
    tm = _tile(m, 1088, BF16_ROWS)

    def body(ids_ref, a_ref, w_ref, after_ref, dst_ref, o_ref):
        del ids_ref, after_ref, dst_ref
        o_ref[...] = jnp.dot(a_ref[...], w_ref[...], preferred_element_type=_F32)

    return pl.pallas_call(
        body, name=name,
        grid_spec=pltpu.PrefetchScalarGridSpec(
            num_scalar_prefetch=1, grid=(count, m // tm),
            in_specs=[pl.BlockSpec((tm, kdim), lambda j, i, ids: (i, 0)),
                      pl.BlockSpec((kdim, width), lambda j, i, ids: (0, ids[first + j])),
                      pl.BlockSpec(after.shape, lambda j, i, ids: (0, 0)), _ANY],
            out_specs=pl.BlockSpec((tm, width), lambda j, i, ids: (i, ids[first + j]))),
        out_shape=jax.ShapeDtypeStruct(dst.shape, dst.dtype),
        input_output_aliases={4: 0},
        compiler_params=pltpu.CompilerParams(dimension_semantics=("arbitrary", "arbitrary")),
    )(order_ids, a, wg, after, dst)


def _cast_into_columns(slab_ids, a, n_blocks, name):
    r, c = a.shape
    tr = _row_tile(r, c)

    def body(ids_ref, a_ref, o_ref):
        del ids_ref
        o_ref[...] = a_ref[...].astype(_BF16)

    return pl.pallas_call(
        body, name=name,
        grid_spec=pltpu.PrefetchScalarGridSpec(
            num_scalar_prefetch=1, grid=(r // tr,),
            in_specs=[pl.BlockSpec((tr, c), lambda i, ids: (i, 0))],
            out_specs=pl.BlockSpec((tr, c), lambda i, ids: (i, ids[0]))),
        out_shape=jax.ShapeDtypeStruct((r, n_blocks * c), _BF16),
        compiler_params=pltpu.CompilerParams(dimension_semantics=("parallel",)),
    )(slab_ids, a)


def _matmul_slabs(a, b, ids, width, name, after=None):
    kdim, m = a.shape
    n_slabs = ids.shape[0]
    tm = _tile(m, 1024, LANES)

    def body(ids_ref, a_ref, b_ref, *rest):
        del ids_ref
        rest[-1][0] = lax.dot_general(a_ref[...], b_ref[...], (((0,), (0,)), ((), ())), preferred_element_type=_F32)

    extra = [] if after is None else [after]
    return pl.pallas_call(
        body, name=name,
        grid_spec=pltpu.PrefetchScalarGridSpec(
            num_scalar_prefetch=1, grid=(m // tm, n_slabs),
            in_specs=[pl.BlockSpec((kdim, tm), lambda i, j, ids: (0, i)),
                      pl.BlockSpec((kdim, width), lambda i, j, ids: (0, ids[j]))]
            + [pl.BlockSpec(t.shape, lambda i, j, ids: (0, 0)) for t in extra],
            out_specs=pl.BlockSpec((1, tm, width), lambda i, j, ids: (j, i, 0))),
        out_shape=jax.ShapeDtypeStruct((n_slabs, m, width), _F32),
        compiler_params=pltpu.CompilerParams(dimension_semantics=("parallel", "parallel")),
    )(ids, a, b, *extra)


def _row_tile(rows, cols):
    return _tile(rows, max(BF16_ROWS, min(512, (1 << 19) // cols)), BF16_ROWS)


def _cast_bf16(a, name):
    r, c = a.shape
    tr = _row_tile(r, c)

    def body(a_ref, o_ref):
        o_ref[...] = a_ref[...].astype(_BF16)

    blk = pl.BlockSpec((tr, c), lambda i: (i, 0))
    return pl.pallas_call(
        body, name=name, grid=(r // tr,), in_specs=[blk], out_specs=blk,
        out_shape=jax.ShapeDtypeStruct((r, c), _BF16),
        compiler_params=pltpu.CompilerParams(dimension_semantics=("parallel",)),
    )(a)


def _cast_into_slab(slab_ids, a, name):
    r, c = a.shape
    tr = _row_tile(r, c)

    def body(ids_ref, a_ref, o_ref):
        del ids_ref
        o_ref[0] = a_ref[...].astype(_BF16)

    return pl.pallas_call(
        body, name=name,
        grid_spec=pltpu.PrefetchScalarGridSpec(
            num_scalar_prefetch=1, grid=(r // tr,),
            in_specs=[pl.BlockSpec((tr, c), lambda i, ids: (i, 0))],
            out_specs=pl.BlockSpec((1, tr, c), lambda i, ids: (ids[0], i, 0))),
        out_shape=jax.ShapeDtypeStruct((N_DEV, r, c), _BF16),
        compiler_params=pltpu.CompilerParams(dimension_semantics=("parallel",)),
    )(slab_ids, a)


def _pre_add(slab_ids, grad, got, name):
    _, r, c = grad.shape
    tr = _row_tile(r, c)

    def body(ids_ref, a_ref, b_ref, o_ref):
        del ids_ref
        o_ref[...] = (a_ref[...] + b_ref[...]).astype(_BF16)

    return pl.pallas_call(
        body, name=name,
        grid_spec=pltpu.PrefetchScalarGridSpec(
            num_scalar_prefetch=1, grid=(N_CHIPS - 1, r // tr),
            in_specs=[pl.BlockSpec((1, tr, c), lambda s, i, ids: (ids[s + 1], i, 0)),
                      pl.BlockSpec((1, tr, c), lambda s, i, ids: (s + 1, i, 0))],
            out_specs=pl.BlockSpec((1, tr, c), lambda s, i, ids: (s, i, 0))),
        out_shape=jax.ShapeDtypeStruct((N_CHIPS - 1, r, c), _BF16),
        compiler_params=pltpu.CompilerParams(dimension_semantics=("parallel", "parallel")),
    )(slab_ids, grad, got)


def _ada_forward(craw, w_shard, name):
    d, cols = w_shard.shape
    tk = _tile(d, 512, LANES)

    def body(c_ref, w_ref, o_ref):
        @pl.when(pl.program_id(0) == 0)
        def _():
            o_ref[...] = jnp.zeros_like(o_ref)

        o_ref[...] += jnp.dot(_silu(c_ref[...]).astype(_BF16), w_ref[...].astype(_BF16), preferred_element_type=_F32)

    return pl.pallas_call(
        body, name=name, grid=(d // tk,),
        in_specs=[pl.BlockSpec((craw.shape[0], tk), lambda k: (0, k)), pl.BlockSpec((tk, cols), lambda k: (k, 0))],
        out_specs=pl.BlockSpec((craw.shape[0], cols), lambda k: (0, 0)),
        out_shape=jax.ShapeDtypeStruct((craw.shape[0], cols), _F32),
        compiler_params=pltpu.CompilerParams(dimension_semantics=("arbitrary",)),
    )(craw, w_shard)


def _ada_backward(craw, dmod, w, m, v, name):
    d, cols = w.shape
    rows = craw.shape[0]
    tr = _tile(d, 256, LANES)

    def body(c_ref, dm_ref, w_ref, m_ref, v_ref, g_ref, dl_ref, nm_ref, nv_ref, dc_ref):
        act = _silu(c_ref[...]).astype(_BF16)
        dmb = dm_ref[...].astype(_BF16)
        wv = w_ref[...]
        g = lax.dot_general(act, dmb, (((0,), (0,)), ((), ())), preferred_element_type=_F32)
        delta, nm, nv = _adamw(wv, g, m_ref[...], v_ref[...])
        g_ref[...] = g
        dl_ref[...] = delta
        nm_ref[...] = nm
        nv_ref[...] = nv
        dc = lax.dot_general(dmb, wv.astype(_BF16), (((1,), (1,)), ((), ())), preferred_element_type=_F32)
        dc_ref[...] = jnp.broadcast_to(jnp.sum(dc[N_DEV:], axis=0, keepdims=True), dc_ref.shape)

    blk = pl.BlockSpec((tr, cols), lambda i: (i, 0))
    return pl.pallas_call(
        body, name=name, grid=(d // tr,),
        in_specs=[pl.BlockSpec((rows, tr), lambda i: (0, i)), pl.BlockSpec((rows, cols), lambda i: (0, 0)), blk, blk, blk],
        out_specs=[blk, blk, blk, blk, pl.BlockSpec((SUBLANES, tr), lambda i: (0, i))],
        out_shape=[jax.ShapeDtypeStruct((d, cols), _F32)] * 4 + [jax.ShapeDtypeStruct((SUBLANES, d), _F32)],
        compiler_params=pltpu.CompilerParams(dimension_semantics=("parallel",)),
    )(craw, dmod, w, m, v)


def _rms(xf):
    return lax.rsqrt(jnp.mean(xf * xf, axis=-1, keepdims=True) + EPS)


def _head_mean(v):
    hi = v.astype(_BF16)
    lo = (v - hi.astype(_F32)).astype(_BF16)
    ones = jnp.full((2 * HEAD_DIM, HEAD_DIM), 1.0 / HEAD_DIM, _BF16)
    return jnp.dot(jnp.concatenate([hi, lo], axis=1), ones, preferred_element_type=_F32)


def _prenorm(ctx, x, g_pre, mods, tr, name):
    l, d = ctx.shape
    n = x.shape[0]
    nbc = l // tr

    def body(ctx_ref, x_ref, g_ref, mod_ref, h_ref):
        def emit(src_ref):
            xf = src_ref[...]
            y = (xf * _rms(xf)) * g_ref[...]
            h_ref[...] = (y * (1.0 + mod_ref[0, 0:1, :]) + mod_ref[0, 1:2, :]).astype(_BF16)

        is_ctx = pl.program_id(0) < nbc
        pl.when(is_ctx)(lambda: emit(ctx_ref))
        pl.when(jnp.logical_not(is_ctx))(lambda: emit(x_ref))

    return pl.pallas_call(
        body, name=name, grid=((l + n) // tr,),
        in_specs=[pl.BlockSpec((tr, d), lambda i: (jnp.minimum(i, nbc - 1), 0)),
                  pl.BlockSpec((tr, d), lambda i: (jnp.maximum(i - nbc, 0), 0)),
                  pl.BlockSpec((1, d), lambda i: (0, 0)),
                  pl.BlockSpec((1, SUBLANES, d), lambda i: ((i >= nbc).astype(jnp.int32), 0, 0))],
        out_specs=pl.BlockSpec((tr, d), lambda i: (i, 0)),
        out_shape=jax.ShapeDtypeStruct((l + n, d), _BF16),
        compiler_params=pltpu.CompilerParams(dimension_semantics=("arbitrary",)),
    )(ctx, x, g_pre, mods)


def _prenorm_backward(dh, ctx, x, dxn, g_pre, mods, tr, name):
    l, d = ctx.shape
    n = x.shape[0]
    nbc = l // tr

    def body(dh_ref, ctx_ref, x_ref, dxn_ref, g_ref, mod_ref, gx_ref, dmod_ref, dg_ref):
        i = pl.program_id(0)

        @pl.when(i == 0)
        def _():
            dg_ref[...] = jnp.zeros_like(dg_ref)

        @pl.when(jnp.logical_or(i == 0, i == nbc))
        def _():
            dmod_ref[...] = jnp.zeros_like(dmod_ref)

        def emit(src_ref, latent):
            xf = src_ref[...]
            r = _rms(xf)
            xn = xf * r
            dhv = dh_ref[...]
            one_scale = 1.0 + mod_ref[0, 0:1, :]
            dmod_ref[0, 0:1, :] += jnp.sum(dhv * (xn * g_ref[...]), axis=0, keepdims=True)
            dmod_ref[0, 1:2, :] += jnp.sum(dhv, axis=0, keepdims=True)
            dyg = dhv * one_scale
            dg_ref[0:1, :] += jnp.sum(dyg * xn, axis=0, keepdims=True)
            if latent:
                dn = dyg * g_ref[...]
                gx_ref[...] = dxn_ref[...] + r * (dn - xn * jnp.mean(dn * xn, axis=-1, keepdims=True))

        pl.when(i < nbc)(lambda: emit(ctx_ref, False))
        pl.when(i >= nbc)(lambda: emit(x_ref, True))

    lat = pl.BlockSpec((tr, d), lambda i: (jnp.maximum(i - nbc, 0), 0))
    sel = pl.BlockSpec((1, SUBLANES, d), lambda i: ((i >= nbc).astype(jnp.int32), 0, 0))
    return pl.pallas_call(
        body, name=name, grid=((l + n) // tr,),
        in_specs=[pl.BlockSpec((tr, d), lambda i: (i, 0)),
                  pl.BlockSpec((tr, d), lambda i: (jnp.minimum(i, nbc - 1), 0)),
                  lat, lat, pl.BlockSpec((1, d), lambda i: (0, 0)), sel],
        out_specs=[lat, sel, pl.BlockSpec((SUBLANES, d), lambda i: (0, 0))],
        out_shape=[jax.ShapeDtypeStruct((n, d), _F32), jax.ShapeDtypeStruct((2, SUBLANES, d), _F32),
                   jax.ShapeDtypeStruct((SUBLANES, d), _F32)],
        compiler_params=pltpu.CompilerParams(dimension_semantics=("arbitrary",)),
    )(dh, ctx, x, dxn, g_pre, mods)


def _rope_tables(l, n):
    rows = n // GRID_W
    row = jnp.repeat(jnp.arange(rows, dtype=_F32), GRID_W)
    col = jnp.tile(jnp.arange(GRID_W, dtype=_F32), rows)
    inv = ROPE_THETA ** (-jnp.arange(ROPE_PAIRS, dtype=_F32) / ROPE_PAIRS)
    ang_r, ang_c = row[:, None] * inv, col[:, None] * inv
    cr, sr, cc, sc = jnp.cos(ang_r), jnp.sin(ang_r), jnp.cos(ang_c), jnp.sin(ang_c)
    zero = jnp.zeros_like(sr)
    tc = jnp.concatenate([cr, cr, cc, cc], axis=-1)
    ta = jnp.concatenate([-sr, zero, -sc, zero], axis=-1)
    tb = jnp.concatenate([zero, sr, zero, sc], axis=-1)
    pad = lambda t, fill: jnp.concatenate([jnp.full((l, HEAD_DIM), fill, _F32), t], axis=0)
    return pad(tc, 1.0), pad(ta, 0.0), pad(tb, 0.0)


def _rope(y, tc, ta, tb):
    return y * tc + pltpu.roll(y, HEAD_DIM - ROPE_PAIRS, 1) * ta + pltpu.roll(y, ROPE_PAIRS, 1) * tb


def _rope_transposed(dy, tc, ta, tb):
    return dy * tc + pltpu.roll(dy * ta, ROPE_PAIRS, 1) + pltpu.roll(dy * tb, HEAD_DIM - ROPE_PAIRS, 1)


def _qkv_post(proj, tables, g_q, g_k, heads, kv_heads, tr, name):
    t = proj.shape[0]
    aw, kw = heads * HEAD_DIM, kv_heads * HEAD_DIM
    w3 = aw + 2 * kw

    def body(p_ref, tc_ref, ta_ref, tb_ref, gq_ref, gk_ref, q_ref, k_ref, v_ref):
        tabs = (tc_ref[...], ta_ref[...], tb_ref[...])

        def norm_rope(col, gain):
            xh = p_ref[:, col:col + HEAD_DIM]
            return _rope((xh * lax.rsqrt(_head_mean(xh * xh) + EPS)) * gain, *tabs).astype(_BF16)

        for h in range(heads):
            q_ref[h] = norm_rope(h * HEAD_DIM, gq_ref[...])
        for h in range(kv_heads):
            k_ref[h] = norm_rope(aw + h * HEAD_DIM, gk_ref[...])
            v_ref[h] = p_ref[:, aw + kw + h * HEAD_DIM:aw + kw + (h + 1) * HEAD_DIM].astype(_BF16)

    tab = pl.BlockSpec((tr, HEAD_DIM), lambda i: (i, 0))
    gain = pl.BlockSpec((1, HEAD_DIM), lambda i: (0, 0))
    return pl.pallas_call(
        body, name=name, grid=(t // tr,),
        in_specs=[pl.BlockSpec((tr, w3), lambda i: (i, 0)), tab, tab, tab, gain, gain],
        out_specs=[pl.BlockSpec((heads, tr, HEAD_DIM), lambda i: (0, i, 0)),
                   pl.BlockSpec((kv_heads, tr, HEAD_DIM), lambda i: (0, i, 0)),
                   pl.BlockSpec((kv_heads, tr, HEAD_DIM), lambda i: (0, i, 0))],
        out_shape=[jax.ShapeDtypeStruct((heads, t, HEAD_DIM), _BF16),
                   jax.ShapeDtypeStruct((kv_heads, t, HEAD_DIM), _BF16),
                   jax.ShapeDtypeStruct((kv_heads, t, HEAD_DIM), _BF16)],
        compiler_params=pltpu.CompilerParams(dimension_semantics=("parallel",)),
    )(proj, *tables, g_q, g_k)


def _qkv_post_backward(proj, dq, dk, dv, tables, g_q, g_k, l, tr, name):
    t = proj.shape[0]
    heads, kv_heads = dq.shape[0], dk.shape[0]
    aw, kw = heads * HEAD_DIM, kv_heads * HEAD_DIM
    w3 = aw + 2 * kw
    nbc = l // tr

    def body(p_ref, dq_ref, dk_ref, dv_ref, tc_ref, ta_ref, tb_ref, gq_ref, gk_ref, o_ref, dgq_ref, dgk_ref):
        i = pl.program_id(0)

        @pl.when(i == 0)
        def _():
            dgq_ref[...] = jnp.zeros_like(dgq_ref)
            dgk_ref[...] = jnp.zeros_like(dgk_ref)

        tabs = (tc_ref[...], ta_ref[...], tb_ref[...])
        latent = i >= nbc

        def back(col, dout, gain, dg_ref):
            xh = p_ref[:, col:col + HEAD_DIM]
            r = lax.rsqrt(_head_mean(xh * xh) + EPS)
            xn = xh * r
            dy = _rope_transposed(dout, *tabs)
            dg_ref[0:1, :] += jnp.sum(dy * xn, axis=0, keepdims=True)
            dn = dy * gain
            o_ref[:, col:col + HEAD_DIM] = (r * (dn - xn * _head_mean(dn * xn))).astype(_BF16)

        for h in range(heads):
            back(h * HEAD_DIM, jnp.where(latent, dq_ref[h], 0.0), gq_ref[...], dgq_ref)
        for h in range(kv_heads):
            back(aw + h * HEAD_DIM, dk_ref[h], gk_ref[...], dgk_ref)
            o_ref[:, aw + kw + h * HEAD_DIM:aw + kw + (h + 1) * HEAD_DIM] = dv_ref[h].astype(_BF16)

    tab = pl.BlockSpec((tr, HEAD_DIM), lambda i: (i, 0))
    gain = pl.BlockSpec((1, HEAD_DIM), lambda i: (0, 0))
    acc = pl.BlockSpec((SUBLANES, HEAD_DIM), lambda i: (0, 0))
    return pl.pallas_call(
        body, name=name, grid=(t // tr,),
        in_specs=[pl.BlockSpec((tr, w3), lambda i: (i, 0)),
                  pl.BlockSpec((heads, tr, HEAD_DIM), lambda i: (0, jnp.maximum(i - nbc, 0), 0)),
                  pl.BlockSpec((kv_heads, tr, HEAD_DIM), lambda i: (0, i, 0)),
                  pl.BlockSpec((kv_heads, tr, HEAD_DIM), lambda i: (0, i, 0)),
                  tab, tab, tab, gain, gain],
        out_specs=[pl.BlockSpec((tr, w3), lambda i: (i, 0)), acc, acc],
        out_shape=[jax.ShapeDtypeStruct((t, w3), _BF16), jax.ShapeDtypeStruct((SUBLANES, HEAD_DIM), _F32),
                   jax.ShapeDtypeStruct((SUBLANES, HEAD_DIM), _F32)],
        compiler_params=pltpu.CompilerParams(dimension_semantics=("arbitrary",)),
    )(proj, dq, dk, dv, *tables, g_q, g_k)


def _attention(q, k, v, proj, l, mix, tq, name):
    heads, t, _ = q.shape
    kv_heads = k.shape[0]
    n = t - l
    rows = GQA_GROUP * tq
    gw = GQA_GROUP * HEAD_DIM
    aw = heads * HEAD_DIM
    gate_col = (aw + 2 * kv_heads * HEAD_DIM) // gw
    off = l // tq

    def body(q_ref, k_ref, v_ref, g_ref, o_ref, y_ref, lse_ref):
        lane = lax.broadcasted_iota(jnp.int32, (tq, LANES), 1)
        lse_blk = jnp.zeros((tq, LANES), _F32)
        for first in range(0, GQA_GROUP, ATTN_SUB_HEADS):
            qs = q_ref[first:first + ATTN_SUB_HEADS].reshape(ATTN_SUB_HEADS * tq, HEAD_DIM)
            raw = lax.dot_general(qs, k_ref[0], (((1,), (1,)), ((), ())), preferred_element_type=_F32)
            m = jnp.max(raw, axis=-1, keepdims=True)
            p = jnp.exp2((raw - m) * (ATTN_SCALE * LOG2_E))
            denom = jnp.sum(p, axis=-1, keepdims=True)
            os_ = jnp.dot(p.astype(_BF16), v_ref[0], preferred_element_type=_F32) / denom
            lse_s = m * ATTN_SCALE + jnp.log(denom)
            for j in range(ATTN_SUB_HEADS):
                g = first + j
                og = os_[j * tq:(j + 1) * tq]
                cols = slice(g * HEAD_DIM, (g + 1) * HEAD_DIM)
                o_ref[:, cols] = og
                y_ref[:, cols] = (og * _silu(g_ref[:, cols])).astype(_BF16)
                lse_blk = jnp.where(lane == g, lse_s[j * tq:(j + 1) * tq], lse_blk)
        lse_ref[0] = lse_blk

    return pl.pallas_call(
        body, name=name, grid=(kv_heads, n // tq),
        in_specs=[pl.BlockSpec((GQA_GROUP, tq, HEAD_DIM), lambda h, i: (h, i + off, 0)),
                  pl.BlockSpec((1, t, HEAD_DIM), lambda h, i: (h, 0, 0)),
                  pl.BlockSpec((1, t, HEAD_DIM), lambda h, i: (h, 0, 0)),
                  pl.BlockSpec((tq, gw), lambda h, i: (i + off, gate_col + h))],
        out_specs=[pl.BlockSpec((tq, gw), lambda h, i: (i, h)),
                   pl.BlockSpec((tq, gw), lambda h, i: (i, h)),
                   pl.BlockSpec((1, tq, LANES), lambda h, i: (h, i, 0))],
        out_shape=[jax.ShapeDtypeStruct((n, aw), _F32), jax.ShapeDtypeStruct((n, mix), _BF16),
                   jax.ShapeDtypeStruct((kv_heads, n, LANES), _F32)],
        compiler_params=pltpu.CompilerParams(dimension_semantics=("parallel", "parallel")),
    )(q, k, v, proj)


def _attention_backward(q, k, v, attn_o, dy, proj, lse, after, l, tq, name):
    heads, t, _ = q.shape
    kv_heads = k.shape[0]
    n = t - l
    rows = GQA_GROUP * tq
    gw = GQA_GROUP * HEAD_DIM
    aw = heads * HEAD_DIM
    gate_col = (aw + 2 * kv_heads * HEAD_DIM) // gw
    off = l // tq
    n_parts = next(p for p in (ATTN_KEY_PARTS, 2, 1) if t % (p * BF16_ROWS) == 0)
    part = t // n_parts

    def body(q_ref, k_ref, v_ref, o_ref, dy_ref, g_ref, lse_ref, after_ref, dq_ref, dg_ref, dk_ref, dv_ref):
        del after_ref

        @pl.when(pl.program_id(1) == 0)
        def _():
            dk_ref[...] = jnp.zeros_like(dk_ref)
            dv_ref[...] = jnp.zeros_like(dv_ref)

        q4 = q_ref[...].reshape(rows, HEAD_DIM)
        do_parts, delta_parts, lse_parts = [], [], []
        lse_blk = lse_ref[0]
        for g in range(GQA_GROUP):
            cols = slice(g * HEAD_DIM, (g + 1) * HEAD_DIM)
            gate, og, dyg = g_ref[:, cols], o_ref[:, cols], dy_ref[:, cols]
            dog = dyg * _silu(gate)
            dg_ref[:, cols] = (dyg * og * _silu_grad(gate)).astype(_BF16)
            do_parts.append(dog)
            delta_parts.append(jnp.sum(dog * og, axis=-1, keepdims=True))
            lse_parts.append(lse_blk[:, g:g + 1])
        do4 = jnp.concatenate(do_parts, axis=0).astype(_BF16)
        delta4 = jnp.concatenate(delta_parts, axis=0)
        lse4 = jnp.concatenate(lse_parts, axis=0)
        dq4 = jnp.zeros((rows, HEAD_DIM), _F32)
        for part_i in range(n_parts):
            keys = slice(part_i * part, (part_i + 1) * part)
            ks, vs = k_ref[0, keys, :], v_ref[0, keys, :]
            s = lax.dot_general(q4, ks, (((1,), (1,)), ((), ())), preferred_element_type=_F32) * ATTN_SCALE
            p = jnp.exp(s - lse4)
            dp = lax.dot_general(do4, vs, (((1,), (1,)), ((), ())), preferred_element_type=_F32)
            ds = (p * (dp - delta4) * ATTN_SCALE).astype(_BF16)
            dq4 = dq4 + jnp.dot(ds, ks, preferred_element_type=_F32)
            dk_ref[0, keys, :] += lax.dot_general(ds, q4, (((0,), (0,)), ((), ())), preferred_element_type=_F32)
            dv_ref[0, keys, :] += lax.dot_general(
                p.astype(_BF16), do4, (((0,), (0,)), ((), ())), preferred_element_type=_F32)
        dq_ref[...] = dq4.reshape(GQA_GROUP, tq, HEAD_DIM)

    kv_spec = pl.BlockSpec((1, t, HEAD_DIM), lambda h, i: (h, 0, 0))
    tok = pl.BlockSpec((tq, gw), lambda h, i: (i, h))
    return pl.pallas_call(
        body, name=name, grid=(kv_heads, n // tq),
        in_specs=[pl.BlockSpec((GQA_GROUP, tq, HEAD_DIM), lambda h, i: (h, i + off, 0)), kv_spec, kv_spec,
                  tok, tok, pl.BlockSpec((tq, gw), lambda h, i: (i + off, gate_col + h)),
                  pl.BlockSpec((1, tq, LANES), lambda h, i: (h, i, 0)),
                  pl.BlockSpec(after.shape, lambda h, i: (0, 0))],
        out_specs=[pl.BlockSpec((GQA_GROUP, tq, HEAD_DIM), lambda h, i: (h, i, 0)), tok, kv_spec, kv_spec],
        out_shape=[jax.ShapeDtypeStruct((heads, n, HEAD_DIM), _F32), jax.ShapeDtypeStruct((n, aw), _BF16),
                   jax.ShapeDtypeStruct((kv_heads, t, HEAD_DIM), _F32), jax.ShapeDtypeStruct((kv_heads, t, HEAD_DIM), _F32)],
        compiler_params=pltpu.CompilerParams(dimension_semantics=("parallel", "arbitrary")),
    )(q, k, v, attn_o, dy, proj, lse, after)


def _halo_specs(tp, width, col, row_off, total_rows):
    per = tp // POOL_HALO
    first = row_off // POOL_HALO
    last = total_rows // POOL_HALO - 1
    return [pl.BlockSpec((tp, width), lambda i: (i + row_off // tp, col)),
            pl.BlockSpec((POOL_HALO, width), lambda i: (jnp.maximum(first + i * per - 1, 0), col)),
            pl.BlockSpec((POOL_HALO, width), lambda i: (jnp.minimum(first + (i + 1) * per, last), col))]


def _with_halo(cur, prev, nxt, t0, n):
    tp = cur.shape[0]
    r8 = lax.broadcasted_iota(jnp.int32, (POOL_HALO, 1), 0)
    prev = jnp.where(t0 - POOL_HALO + r8 >= 0, prev, 0.0)
    nxt = jnp.where(t0 + tp + r8 < n, nxt, 0.0)
    return jnp.concatenate([prev, cur, nxt], axis=0)


def _shift_rows(a, s):
    return pltpu.roll(a, s % a.shape[0], 0)


def _window_sum(e, w, mirrored):
    a = e + _shift_rows(e, -1 if mirrored else 1)
    s = 1
    while 2 * s < w:
        a = _shift_rows(a, s) + _shift_rows(a, -s)
        s *= 2
    return a


def _window_count(t, w, n):
    half = w // 2
    return (jnp.minimum(t + half, n) - jnp.maximum(t - half, 0)).astype(_F32)


def _pool_forward(gi, proj, y, pool_w, pool_scale, l, heads, kv_heads, tp, name):
    t = proj.shape[0]
    n = t - l
    pg = pool_w.shape[-1]
    w = POOL_WINDOWS[gi]
    aw, kw = heads * HEAD_DIM, kv_heads * HEAD_DIM
    u_col = (2 * aw + 2 * kw) // pg + gi
    gate_col = (2 * aw + 2 * kw + len(POOL_WINDOWS) * pg) // pg + gi

    def body(u_ref, up_ref, un_ref, g_ref, w_ref, sc_ref, y_in_ref, y_ref, raw_ref, d_ref):
        del y_in_ref
        t0 = pl.program_id(0) * tp
        cur = u_ref[...]
        win = _window_sum(_with_halo(cur, up_ref[...], un_ref[...], t0, n), w, False)[POOL_HALO:POOL_HALO + tp]
        tok = t0 + lax.broadcasted_iota(jnp.int32, (tp, 1), 0)
        d = (win / _window_count(tok, w, n) - cur).astype(_BF16)
        raw = jnp.dot(d, w_ref[...].reshape(pg, pg), preferred_element_type=_F32)
        d_ref[...] = d
        raw_ref[...] = raw
        y_ref[...] = ((raw * sc_ref[...]) * _silu(g_ref[...])).astype(_BF16)

    blk = pl.BlockSpec((tp, pg), lambda i: (i, 0))
    return pl.pallas_call(
        body, name=name, grid=(n // tp,),
        in_specs=_halo_specs(tp, pg, u_col, l, t) + [
            pl.BlockSpec((tp, pg), lambda i: (i + l // tp, gate_col)),
            pl.BlockSpec((N_DEV, 1, pg // N_DEV, pg), lambda i: (0, gi, 0, 0)),
            pl.BlockSpec((1, pg), lambda i: (0, gi)), _ANY],
        out_specs=[pl.BlockSpec((tp, pg), lambda i: (i, aw // pg + gi)), blk, blk],
        out_shape=[jax.ShapeDtypeStruct(y.shape, y.dtype), jax.ShapeDtypeStruct((n, pg), _F32),
                   jax.ShapeDtypeStruct((n, pg), _BF16)],
        input_output_aliases={6: 0},
        compiler_params=pltpu.CompilerParams(dimension_semantics=("arbitrary",)),
    )(proj, proj, proj, proj, pool_w, pool_scale, y)


def _pool_backward_gate(gi, dy, proj, raw, pool_w, pool_scale, l, heads, kv_heads, tp, name):
    n, pg = raw.shape
    aw, kw = heads * HEAD_DIM, kv_heads * HEAD_DIM
    gate_col = (2 * aw + 2 * kw + len(POOL_WINDOWS) * pg) // pg + gi

    def body(dy_ref, g_ref, raw_ref, w_ref, sc_ref, dg_ref, dr_ref, dd_ref, ds_ref):
        @pl.when(pl.program_id(0) == 0)
        def _():
            ds_ref[...] = jnp.zeros_like(ds_ref)

        gate, rawv, dyv, scale = g_ref[...], raw_ref[...], dy_ref[...], sc_ref[...]
        dpool = dyv * _silu(gate)
        dg_ref[...] = (dyv * (rawv * scale) * _silu_grad(gate)).astype(_BF16)
        ds_ref[0:1, :] += jnp.sum(dpool * rawv, axis=0, keepdims=True)
        draw = (dpool * scale).astype(_BF16)
        dr_ref[...] = draw
        dd_ref[...] = lax.dot_general(
            draw, w_ref[...].reshape(pg, pg), (((1,), (1,)), ((), ())), preferred_element_type=_F32)

    blk = pl.BlockSpec((tp, pg), lambda i: (i, 0))
    return pl.pallas_call(
        body, name=name, grid=(n // tp,),
        in_specs=[pl.BlockSpec((tp, pg), lambda i: (i, aw // pg + gi)),
                  pl.BlockSpec((tp, pg), lambda i: (i + l // tp, gate_col)), blk,
                  pl.BlockSpec((N_DEV, 1, pg // N_DEV, pg), lambda i: (0, gi, 0, 0)),
                  pl.BlockSpec((1, pg), lambda i: (0, gi))],
        out_specs=[blk, blk, blk, pl.BlockSpec((SUBLANES, pg), lambda i: (0, 0))],
        out_shape=[jax.ShapeDtypeStruct((n, pg), _BF16), jax.ShapeDtypeStruct((n, pg), _BF16),
                   jax.ShapeDtypeStruct((n, pg), _F32), jax.ShapeDtypeStruct((SUBLANES, pg), _F32)],
        compiler_params=pltpu.CompilerParams(dimension_semantics=("arbitrary",)),
    )(dy, proj, raw, pool_w, pool_scale)


def _pool_backward_window(gi, dd, tp, name):
    n, pg = dd.shape
    w = POOL_WINDOWS[gi]

    def body(c_ref, p_ref, n_ref, du_ref):
        t0 = pl.program_id(0) * tp
        cur = c_ref[...]
        e = _with_halo(cur, p_ref[...], n_ref[...], t0, n)
        tok = t0 - POOL_HALO + lax.broadcasted_iota(jnp.int32, (tp + 2 * POOL_HALO, 1), 0)
        e = e / jnp.maximum(_window_count(tok, w, n), 1.0)
        du_ref[...] = (_window_sum(e, w, True)[POOL_HALO:POOL_HALO + tp] - cur).astype(_BF16)

    return pl.pallas_call(
        body, name=name, grid=(n // tp,),
        in_specs=_halo_specs(tp, pg, 0, 0, n), out_specs=pl.BlockSpec((tp, pg), lambda i: (i, 0)),
        out_shape=jax.ShapeDtypeStruct((n, pg), _BF16),
        compiler_params=pltpu.CompilerParams(dimension_semantics=("parallel",)),
    )(dd, dd, dd)


def _post(out, x, target, gate, g_post, tr, name):
    n, d = out.shape

    def body(o_ref, x_ref, t_ref, gate_ref, g_ref, dxn_ref, do_ref, dgate_ref, dg_ref, loss_ref):
        @pl.when(pl.program_id(0) == 0)
        def _():
            dgate_ref[...] = jnp.zeros_like(dgate_ref)
            dg_ref[...] = jnp.zeros_like(dg_ref)
            loss_ref[...] = jnp.zeros_like(loss_ref)

        ov = o_ref[...]
        r = _rms(ov)
        on = ov * r
        normed = on * g_ref[...]
        err = (x_ref[...] + gate_ref[...] * normed) - t_ref[...]
        loss_ref[...] += jnp.sum(err * err)
        dxn = err / d
        dxn_ref[...] = dxn
        dgate_ref[0:1, :] += jnp.sum(dxn * normed, axis=0, keepdims=True)
        dr = dxn * gate_ref[...]
        dg_ref[0:1, :] += jnp.sum(dr * on, axis=0, keepdims=True)
        dn = dr * g_ref[...]
        do_ref[...] = (r * (dn - on * jnp.mean(dn * on, axis=-1, keepdims=True))).astype(_BF16)

    blk = pl.BlockSpec((tr, d), lambda i: (i, 0))
    vec = pl.BlockSpec((1, d), lambda i: (0, 0))
    acc = pl.BlockSpec((SUBLANES, d), lambda i: (0, 0))
    return pl.pallas_call(
        body, name=name, grid=(n // tr,),
        in_specs=[blk, blk, blk, vec, vec],
        out_specs=[blk, blk, acc, acc, pl.BlockSpec((SUBLANES, LANES), lambda i: (0, 0))],
        out_shape=[jax.ShapeDtypeStruct((n, d), _F32), jax.ShapeDtypeStruct((n, d), _BF16),
                   jax.ShapeDtypeStruct((SUBLANES, d), _F32), jax.ShapeDtypeStruct((SUBLANES, d), _F32),
                   jax.ShapeDtypeStruct((SUBLANES, LANES), _F32)],
        compiler_params=pltpu.CompilerParams(dimension_semantics=("arbitrary",)),
    )(out, x, target, gate, g_post)


def _adam_sharded(slab_ids, grad, got, far, w, m, v, name):
    r, c = w.shape
    tr = _tile(r, max(BF16_ROWS, min(256, (1 << 18) // c)), BF16_ROWS)

    def body(ids_ref, own_ref, got_ref, far_ref, w_ref, m_ref, v_ref, g_ref, dl_ref, nm_ref, nv_ref):
        del ids_ref
        g = own_ref[0] + got_ref[0]
        for k in range(N_CHIPS - 1):
            g = g + far_ref[k].astype(_F32)
        delta, nm, nv = _adamw(w_ref[...], g, m_ref[...], v_ref[...])
        g_ref[...] = g
        dl_ref[...] = delta
        nm_ref[...] = nm
        nv_ref[...] = nv

    blk = pl.BlockSpec((tr, c), lambda i, ids: (i, 0))
    return pl.pallas_call(
        body, name=name,
        grid_spec=pltpu.PrefetchScalarGridSpec(
            num_scalar_prefetch=1, grid=(r // tr,),
            in_specs=[pl.BlockSpec((1, tr, c), lambda i, ids: (ids[0], i, 0)),
                      pl.BlockSpec((1, tr, c), lambda i, ids: (0, i, 0)),
                      pl.BlockSpec((N_CHIPS - 1, tr, c), lambda i, ids: (0, i, 0)), blk, blk, blk],
            out_specs=[blk] * 4),
        out_shape=[jax.ShapeDtypeStruct((r, c), _F32)] * 4,
        compiler_params=pltpu.CompilerParams(dimension_semantics=("parallel",)),
    )(slab_ids, grad, got, far, w, m, v)


def _adam_replicated(parts, extra, through_silu, w, m, v, name):
    def body(p_ref, e_ref, s_ref, w_ref, m_ref, v_ref, g_ref, dl_ref, nm_ref, nv_ref):
        total = p_ref[0] + e_ref[0]
        for dev in range(1, N_DEV):
            total = total + (p_ref[dev] + e_ref[dev])
        g = jnp.where(s_ref[...] > 0.5, total * _silu_grad(w_ref[...]), total)
        delta, nm, nv = _adamw(w_ref[...], g, m_ref[...], v_ref[...])
        g_ref[...] = g
        dl_ref[...] = delta
        nm_ref[...] = nm
        nv_ref[...] = nv

    return pl.pallas_call(
        body, name=name, in_specs=[_VMEM] * 6, out_specs=[_VMEM] * 4,
        out_shape=[jax.ShapeDtypeStruct(w.shape, _F32)] * 4,
    )(parts, extra, through_silu, w, m, v)


def _as_rows(vec):
    size = vec.shape[0]
    padded = -(-size // (SUBLANES * LANES)) * SUBLANES * LANES
    return jnp.pad(vec, (0, padded - size)).reshape(padded // LANES, LANES)


def kernel(x, c, ctx, c_ctx, w_ada, b_ada, norm_pre, norm_post, w_in, q_norm, k_norm, pool_w, pool_scale, w_out, loss_target, m_c_ctx, m_w_ada, m_b_ada, m_norm_pre, m_norm_post, m_w_in, m_q_norm, m_k_norm, m_pool_w, m_pool_scale, m_w_out, v_c_ctx, v_w_ada, v_b_ada, v_norm_pre, v_norm_post, v_w_in, v_q_norm, v_k_norm, v_pool_w, v_pool_scale, v_w_out):
    me = _dev_index(*_position())
    x2, ctx2, target = x[0], ctx[0], loss_target[0]
    n, d = x2.shape
    l = ctx2.shape[0]
    t = l + n
    aw = d // 2
    heads = aw // HEAD_DIM
    kv_heads = heads // GQA_GROUP
    kw = kv_heads * HEAD_DIM
    n_groups = len(POOL_WINDOWS)
    pg = (d - aw) // n_groups
    mix = d
    tr = _tile(l, 128, BF16_ROWS)
    tq = _tile(l, 128, BF16_ROWS)
    tp = _tile(l, 512, POOL_HALO)

    xi, yi, ci = _position()
    slab_ids = jnp.stack([_dev_index(*chip, ci) for chip in _chip_order(xi, yi)]).astype(jnp.int32)

    cw = w_in.shape[-1]
    wg = _cast_into_columns(slab_ids, w_in[0], N_DEV, "cast_w_in")
    hop_a = _w_in_hop(wg, cw, [], "a", norm_pre, "gather_w_in_a")
    late = [_cast_into_slab(slab_ids, w_out[0], "cast_w_out"),
            _cast_into_slab(slab_ids, pool_w[0].reshape(-1, pg), "cast_pool_w")]

    c_row = c[0] + hop_a[-1][0, 0]
    c_all = _all_gather_small(_as_rows(c_row), "gather_c").reshape(N_DEV, -1)[:, :d]
    craw = jnp.concatenate([c_all, jnp.broadcast_to(c_ctx[None], (N_DEV, d))], axis=0)
    ada = _ada_forward(craw, w_ada[0], "ada_forward")
    ada_all = _all_gather_small(ada, "gather_ada")
    mod_all = ada_all.transpose(1, 0, 2).reshape(ada.shape[0], -1) + b_ada[0]
    mod = lax.dynamic_index_in_dim(mod_all, me, 0, keepdims=False)
    mod_c = mod_all[N_DEV]
    shift, scale, gate = mod[:d], mod[d:2 * d], mod[2 * d:]
    zeros6 = jnp.zeros((SUBLANES - 2, d), _F32)
    mods = jnp.stack([jnp.concatenate([mod_c[None, d:2 * d], mod_c[None, :d], zeros6], axis=0),
                      jnp.concatenate([scale[None], shift[None], zeros6], axis=0)])

    h_all = _prenorm(ctx2, x2, norm_pre, mods, tr, "prenorm")
    order_ids = jnp.stack([_dev_index(*dev) for dev in _w_in_order(xi, yi, ci)]).astype(jnp.int32)
    proj = lax.empty((t, N_DEV * cw), _F32)
    a_s, a_r, wg, tok = hop_a
    b_s, b_r, wg, tok = _w_in_hop(wg, cw, [("a", a_s, a_r, [0, 1], [])], "b", h_all, "gather_w_in_b")
    proj = _proj_blocks(h_all, wg, proj, order_ids, 0, 3, cw, tok, "proj_0")
    c_s, c_r, wg, tok = _w_in_hop(wg, cw, [("b", b_s, b_r, [2, 0], [])], "c", proj, "gather_w_in_c")
    proj = _proj_blocks(h_all, wg, proj, order_ids, 3, 2, cw, tok, "proj_1")
    d_s, d_r, wg, tok = _w_in_hop(
        wg, cw, [("c", c_s, c_r, [0], []), ("b", b_s, b_r, [1], [])], "d", proj, "gather_w_in_d")
    proj = _proj_blocks(h_all, wg, proj, order_ids, 5, 2, cw, tok, "proj_2")
    w_in_g, tok = _w_in_hop(
        wg, cw, [("d", d_s, d_r, [0], [0]), ("a", a_s, a_r, [], [0, 1]), ("b", b_s, b_r, [], [0, 1, 2]),
                 ("c", c_s, c_r, [], [0])], None, proj, "gather_w_in_end")
    flight_w = _gather_slabs_start(late, w_in_g, "gather_late_start")
    proj = _proj_blocks(h_all, w_in_g, proj, order_ids, 7, 1, cw, flight_w[-1], "proj_3")
    tables = _rope_tables(l, n)
    q, k, v = _qkv_post(proj, tables, q_norm, k_norm, heads, kv_heads, tr, "qkv_post")
    attn_o, y, lse = _attention(q, k, v, proj, l, mix, tq, "attention")
    w_out_g8, pool_g8 = _gather_slabs_wait(*flight_w[:3], attn_o, "gather_late_wait")
    w_out_g = w_out_g8.reshape(mix, d)
    pool_g = pool_g8.reshape(N_DEV, n_groups, pg // N_DEV, pg)
    raws, ds = [], []
    for gi in range(n_groups):
        y, raw, dsave = _pool_forward(gi, proj, y, pool_g, pool_scale, l, heads, kv_heads, tp, f"pool_forward_{gi}")
        raws.append(raw)
        ds.append(dsave)
    out = _matmul(y, w_out_g, name="out_proj")
    dxn, dout, dgate8, dgpost8, loss8 = _post(out, x2, target, gate[None], norm_post, tr, "post")

    gw_out = _matmul(y, dout, ta=True, name="grad_w_out").reshape(N_DEV, mix // N_DEV, d)
    flight_so = _exchange_start(_sibling_copies_by_device, [gw_out], "exchange_sibling_start_w_out", land_slabs=N_CHIPS)
    dy = _matmul(dout, w_out_g, tb=True, after=flight_so[-1], name="d_y")
    gw_out, got_out = _exchange_wait(
        _sibling_copies_by_device, *flight_so[:4], dy, "exchange_sibling_wait_w_out", with_sources=True)
    sum_out = _pre_add(slab_ids, gw_out, got_out, "pre_add_w_out")
    flight_out = _exchange_start(_chip_copies, [sum_out], "exchange_chips_start_w_out")
    dq, dgate_attn, dk, dv = _attention_backward(
        q, k, v, attn_o, dy, proj, lse, flight_out[-1], l, tq, "attention_backward")
    dqkv, dgq8, dgk8 = _qkv_post_backward(proj, dq, dk, dv, tables, q_norm, k_norm, l, tr, "qkv_post_backward")
    dus, dgps, gpw, dps8 = [], [], [], []
    for gi in range(n_groups):
        dgp, draw, dd, dps = _pool_backward_gate(
            gi, dy, proj, raws[gi], pool_g, pool_scale, l, heads, kv_heads, tp, f"pool_backward_gate_{gi}")
        dus.append(_pool_backward_window(gi, dd, tp, f"pool_backward_window_{gi}"))
        dgps.append(dgp)
        dps8.append(dps)
        gpw.append(_matmul(ds[gi], draw, ta=True, name=f"grad_pool_w_{gi}"))
    latent_cols = jnp.concatenate([dgate_attn] + dus + dgps, axis=1)
    dproj = jnp.concatenate([dqkv, jnp.pad(latent_cols, ((l, 0), (0, 0)))], axis=1)
    cw = w_in.shape[-1]
    other_ids = jnp.stack([_dev_index(*chip, 1 - ci) for chip in _chip_order(xi, yi)]).astype(jnp.int32)
    chip_slabs = jnp.arange(N_CHIPS, dtype=jnp.int32)
    pr = pool_w.shape[2]
    gpw8 = jnp.stack(gpw).reshape(n_groups, N_DEV, pr, pg).transpose(1, 0, 2, 3).reshape(N_DEV, n_groups * pr, pg)
    give_in = _matmul_slabs(h_all, dproj, other_ids, cw, "grad_w_in_sibling")
    flight_sib = _exchange_start(_sibling_copies, [give_in, jnp.take(gpw8, other_ids, axis=0)], "exchange_sibling_start")
    gw_in = _matmul_slabs(h_all, dproj, slab_ids, cw, "grad_w_in_own", after=flight_sib[-1])
    gpw_own = jnp.take(gpw8, slab_ids, axis=0)
    got_in, got_pw = _exchange_wait(_sibling_copies, *flight_sib[:4], gw_in, "exchange_sibling_wait")
    sums_in = [_pre_add(chip_slabs, gw_in, got_in, "pre_add_w_in"), _pre_add(chip_slabs, gpw_own, got_pw, "pre_add_pool_w")]
    flight_in = _exchange_start(_chip_copies, sums_in, "exchange_chips_start_w_in")
    dh = _matmul(dproj, w_in_g, tb=True, tm=1088, tk=3072, after=flight_in[-1], name="d_h")
    grad_x, dmods, dgpre8 = _prenorm_backward(dh, ctx2, x2, dxn, norm_pre, mods, tr, "prenorm_backward")

    dmod_lat = jnp.concatenate([dmods[1, 1], dmods[1, 0], dgate8[0]])
    dmod_ctx = jnp.concatenate([dmods[0, 1], dmods[0, 0], jnp.zeros((d,), _F32)])
    small = jnp.concatenate([dmod_lat, dmod_ctx, dgpre8[0], dgpost8[0], dgq8[0], dgk8[0]] + [p[0] for p in dps8]
                            + [loss8[0, :1]])
    gathered = _all_gather_small(_as_rows(small), "gather_small").reshape(N_DEV, -1)
    o = 0
    take = lambda size: (gathered[:, o:o + size], o + size)
    g_mod, o = take(3 * d)
    g_modc, o = take(3 * d)
    g_pre, o = take(d)
    g_post, o = take(d)
    g_q, o = take(HEAD_DIM)
    g_k, o = take(HEAD_DIM)
    g_ps, o = take(n_groups * pg)
    g_loss, o = take(1)
    cols = w_ada.shape[-1]
    mine = lambda a: lax.dynamic_slice_in_dim(a, me * cols, cols, axis=1)
    dmod_rows = jnp.concatenate([mine(g_mod), mine(g_modc)], axis=0)
    g_wada, dl_wada, nm_wada, nv_wada, dcact = _ada_backward(craw, dmod_rows, w_ada[0], m_w_ada[0], v_w_ada[0], "ada_backward")
    dcc = _all_gather_small(_as_rows(dcact[0]), "gather_dcc").reshape(N_DEV, -1)[:, :d]

    sizes = [d, 3 * d, d, d, HEAD_DIM, HEAD_DIM, n_groups * pg]
    def pack(parts):
        rows = jnp.concatenate(parts, axis=1)
        padded = -(-rows.shape[1] // (SUBLANES * LANES)) * SUBLANES * LANES
        return jnp.pad(rows, ((0, 0), (0, padded - rows.shape[1]))).reshape(N_DEV, padded // LANES, LANES)

    zero = lambda size: jnp.zeros((N_DEV, size), _F32)
    parts = pack([dcc, g_mod, g_pre, g_post, g_q, g_k, g_ps])
    extra = pack([zero(d), g_modc, zero(d), zero(d), zero(HEAD_DIM), zero(HEAD_DIM), zero(n_groups * pg)])
    through_silu = _as_rows(jnp.concatenate([jnp.ones((d,), _F32), jnp.zeros((sum(sizes[1:]),), _F32)]))
    cat = lambda items: _as_rows(jnp.concatenate([a.reshape(-1) for a in items]))
    ws = [c_ctx, b_ada, norm_pre, norm_post, q_norm, k_norm, pool_scale]
    ms = [m_c_ctx, m_b_ada, m_norm_pre, m_norm_post, m_q_norm, m_k_norm, m_pool_scale]
    vs = [v_c_ctx, v_b_ada, v_norm_pre, v_norm_post, v_q_norm, v_k_norm, v_pool_scale]
    rep = _adam_replicated(parts, extra, through_silu, cat(ws), cat(ms), cat(vs), "adam_replicated")

    def split(packed):
        flat_, outs, at = packed.reshape(-1), [], 0
        for w, size in zip(ws, sizes):
            outs.append(flat_[at:at + size].reshape(w.shape))
            at += size
        return outs

    g_rep, dl_rep, nm_rep, nv_rep = [split(r) for r in rep]

    far_out = _exchange_wait(_chip_copies, *flight_out[:4], grad_x, "exchange_chips_wait_w_out")[0]
    far_in, far_pw = _exchange_wait(_chip_copies, *flight_in[:4], rep[0], "exchange_chips_wait_w_in")
    two = lambda a: a.reshape(-1, a.shape[-1])
    sharded = []
    for ids, g, got, far, w, m, v_, name in zip(
            (chip_slabs, slab_ids, chip_slabs), (gw_in, gw_out, gpw_own), (got_in, got_out, got_pw),
            (far_in, far_out, far_pw), (w_in, w_out, pool_w), (m_w_in, m_w_out, m_pool_w),
            (v_w_in, v_w_out, v_pool_w), ("adam_w_in", "adam_w_out", "adam_pool_w")):
        res = _adam_sharded(ids, g, got, far, two(w), two(m), two(v_), name)
        sharded.append([r.reshape(w.shape) for r in res])
    (g_w_in, dl_w_in, nm_w_in, nv_w_in), (g_w_out, dl_w_out, nm_w_out, nv_w_out), (g_pw, dl_pw, nm_pw, nv_pw) = sharded

    loss_sum = g_loss[0, 0]
    for dev in range(1, N_DEV):
        loss_sum = loss_sum + g_loss[dev, 0]
    loss = (0.5 / d) * loss_sum

    def ordered(rep_list, ada_, w_in_, pw_, w_out_):
        return [rep_list[0], ada_[None], rep_list[1], rep_list[2], rep_list[3], w_in_, rep_list[4], rep_list[5],
                pw_, rep_list[6], w_out_]

    return (loss, grad_x[None],
            *ordered(g_rep, g_wada, g_w_in, g_pw, g_w_out),
            *ordered(dl_rep, dl_wada, dl_w_in, dl_pw, dl_w_out),
            *ordered(nm_rep, nm_wada, nm_w_in, nm_pw, nm_w_out),
            *ordered(nv_rep, nv_wada, nv_w_in, nv_pw, nv_w_out))
```

```python
import functools

import jax
import jax.numpy as jnp
from jax import lax
from jax.experimental import pallas as pl
from jax.experimental.pallas import tpu as pltpu

HEAD_DIM = 128
GQA_GROUP = 4
ATTN_SUB_HEADS = 2
ATTN_KEY_PARTS = 2
LOG2_E = 1.4426950408889634
GRID_W = 64
ROPE_PAIRS = HEAD_DIM // 4
ROPE_THETA = 10000.0
ATTN_SCALE = HEAD_DIM ** -0.5
EPS = 1e-6
POOL_WINDOWS = (2, 4, 8, 16)
POOL_HALO = 8
N_DEV = 8
N_CHIPS = 4
ADAM_LR = 0.001
ADAM_B1 = 0.9
ADAM_B2 = 0.999
ADAM_EPS = 1e-08
ADAM_WD = 0.01
ADAM_STEP = 10

LANES = 128
SUBLANES = 8
BF16_ROWS = 16

_MESH = pl.DeviceIdType.MESH
_ANY = pl.BlockSpec(memory_space=pl.ANY)
_VMEM = pl.BlockSpec(memory_space=pltpu.VMEM)
_HBM = pl.BlockSpec(memory_space=pltpu.HBM)
_SEM = pl.BlockSpec(memory_space=pltpu.SEMAPHORE)
_EFFECT = pltpu.SideEffectType.DATAFLOW_SIDE_EFFECTING
_F32 = jnp.float32
_BF16 = jnp.bfloat16


def _tile(dim, pref, align):
    t = min(pref, dim)
    t -= t % align
    while t >= align:
        if dim % t == 0:
            return t
        t -= align
    return dim


def _position():
    return lax.axis_index("x"), lax.axis_index("y"), lax.axis_index("c")


def _flip(v, bit):
    return 1 - v if bit else v


def _dev_index(x, y, c):
    return 4 * x + 2 * y + c


def _silu(g):
    return g * jax.nn.sigmoid(g)


def _silu_grad(g):
    s = jax.nn.sigmoid(g)
    return s * (1.0 + g * (1.0 - s))


def _adamw(w, g, m, v):
    m = ADAM_B1 * m + (1.0 - ADAM_B1) * g
    v = ADAM_B2 * v + (1.0 - ADAM_B2) * (g * g)
    m_hat = m / (1.0 - ADAM_B1 ** ADAM_STEP)
    v_hat = v / (1.0 - ADAM_B2 ** ADAM_STEP)
    delta = -ADAM_LR * (m_hat / (jnp.sqrt(v_hat) + ADAM_EPS) + ADAM_WD * w)
    return delta, m, v


def _all_gather_small(v, name):
    rows, cols = v.shape

    def body(v_ref, out_ref, send_sems, recv_sems):
        x, y, c = _position()
        me = _dev_index(x, y, c)
        out_ref[me] = v_ref[...]
        peers = [(_flip(x, k & 4), _flip(y, k & 2), _flip(c, k & 1)) for k in range(1, N_DEV)]

        def copy(k, block, to):
            return pltpu.make_async_remote_copy(
                src_ref=v_ref, dst_ref=out_ref.at[block], send_sem=send_sems.at[k], recv_sem=recv_sems.at[k],
                device_id=to, device_id_type=_MESH)

        sends = [copy(k, me, p) for k, p in enumerate(peers)]
        for s in sends:
            s.start()
        for k, p in enumerate(peers):
            copy(k, _dev_index(*p), p).wait_recv()
        for s in sends:
            s.wait_send()

    return pl.pallas_call(
        body, name=name,
        out_shape=jax.ShapeDtypeStruct((N_DEV, rows, cols), v.dtype),
        in_specs=[_VMEM], out_specs=_VMEM,
        scratch_shapes=[pltpu.SemaphoreType.DMA((N_DEV - 1,)), pltpu.SemaphoreType.DMA((N_DEV - 1,))],
    )(v)


def _route(x, y, c):
    first = (x + (1 - c) * (1 - 2 * x), y + c * (1 - 2 * y))
    second = (x + c * (1 - 2 * x), y + (1 - c) * (1 - 2 * y))
    return first, second, (1 - x, 1 - y)


def _w_in_order(x, y, c):
    first, second, diagonal = _route(x, y, c)
    return [(x, y, c), (x, y, 1 - c), (*first, c), (*second, 1 - c), (*second, c), (*first, 1 - c),
            (*diagonal, c), (*diagonal, 1 - c)]


W_IN_HOPS = {"a": 2, "b": 3, "c": 1, "d": 1}


def _w_in_hop_copies(group, wg, width, send_sems, recv_sems):
    x, y, c = _position()
    me, sibling = (x, y, c), (x, y, 1 - c)
    first, second, diagonal = _route(x, y, c)

    def cp(k, block, to):
        cols = wg.at[:, pl.ds(pl.multiple_of(_dev_index(*block) * width, width), width)]
        return pltpu.make_async_remote_copy(
            src_ref=cols, dst_ref=cols, send_sem=send_sems.at[k], recv_sem=recv_sems.at[k],
            device_id=to, device_id_type=_MESH)

    if group == "a":
        return [cp(0, me, sibling), cp(1, me, (*first, c))]
    if group == "b":
        return [cp(0, me, (*second, c)), cp(1, (*first, c), (*second, c)), cp(2, (*first, c), sibling)]
    if group == "c":
        return [cp(0, (*second, c), sibling)]
    return [cp(0, (*diagonal, c), sibling)]


def _w_in_hop(wg, width, waits, start, after, name):
    n_sem = 2 * len(waits)

    def body(*refs):
        wg_ref = refs[0]
        for i, (group, _, _, arrivals, sends) in enumerate(waits):
            cps = _w_in_hop_copies(group, wg_ref, width, refs[1 + 2 * i], refs[2 + 2 * i])
            for k in arrivals:
                cps[k].wait_recv()
            for k in sends:
                cps[k].wait_send()
        if start:
            for cp in _w_in_hop_copies(start, wg_ref, width, refs[n_sem + 2], refs[n_sem + 3]):
                cp.start()
        refs[-1][...] = jnp.zeros_like(refs[-1])

    sems = [s for w in waits for s in w[1:3]]
    new = [pltpu.SemaphoreType.DMA((W_IN_HOPS[start],))] * 2 if start else []
    outs = pl.pallas_call(
        body, name=name,
        out_shape=(*new, pltpu.HBM(wg.shape, wg.dtype), jax.ShapeDtypeStruct((SUBLANES, LANES), _F32)),
        in_specs=[_HBM] + [_SEM] * n_sem + [_ANY], out_specs=(*[_SEM] * len(new), _HBM, _VMEM),
        input_output_aliases={0: len(new)},
        compiler_params=pltpu.CompilerParams(has_side_effects=_EFFECT),
    )(pltpu.with_memory_space_constraint(wg, pltpu.HBM), *sems, after)
    return outs


def _slab_copies(bufs, send_sems, recv_sems):
    x, y, c = _position()
    me = _dev_index(x, y, c)
    peers = [(_flip(x, k & 4), _flip(y, k & 2), _flip(c, k & 1)) for k in range(1, N_DEV)]
    return [pltpu.make_async_remote_copy(
        src_ref=buf.at[me], dst_ref=buf.at[me],
        send_sem=send_sems.at[(N_DEV - 1) * a + k], recv_sem=recv_sems.at[(N_DEV - 1) * a + k],
        device_id=peer, device_id_type=_MESH)
        for a, buf in enumerate(bufs) for k, peer in enumerate(peers)]


def _gather_slabs_start(bufs, after, name):
    n = len(bufs)
    n_copies = (N_DEV - 1) * n

    def body(*refs):
        send_sems, recv_sems, token = refs[n + 1], refs[n + 2], refs[-1]
        for cp in _slab_copies(refs[:n], send_sems, recv_sems):
            cp.start()
        token[...] = jnp.zeros_like(token)

    outs = pl.pallas_call(
        body, name=name,
        out_shape=(pltpu.SemaphoreType.DMA((n_copies,)), pltpu.SemaphoreType.DMA((n_copies,)),
                   *[pltpu.HBM(b.shape, b.dtype) for b in bufs], jax.ShapeDtypeStruct((SUBLANES, LANES), _F32)),
        in_specs=[_HBM] * n + [_ANY], out_specs=(_SEM, _SEM, *[_HBM] * n, _VMEM),
        input_output_aliases={i: 2 + i for i in range(n)},
        compiler_params=pltpu.CompilerParams(has_side_effects=_EFFECT),
    )(*[pltpu.with_memory_space_constraint(b, pltpu.HBM) for b in bufs], after)
    return outs[0], outs[1], list(outs[2:2 + n]), outs[-1]


def _gather_slabs_wait(send_sems, recv_sems, bufs, after, name):
    n = len(bufs)

    def body(*refs):
        for cp in _slab_copies(refs[:n], refs[n], refs[n + 1]):
            cp.wait_send()
            cp.wait_recv()

    outs = pl.pallas_call(
        body, name=name, out_shape=tuple(pltpu.HBM(b.shape, b.dtype) for b in bufs),
        in_specs=[_HBM] * n + [_SEM, _SEM, _ANY], out_specs=[_HBM] * n,
        input_output_aliases={i: i for i in range(n)},
        compiler_params=pltpu.CompilerParams(has_side_effects=_EFFECT),
    )(*bufs, send_sems, recv_sems, after)
    return list(outs)


def _chip_order(x, y):
    return [(x, y), (1 - x, y), (x, 1 - y), (1 - x, 1 - y)]


def _chip_copies(srcs, lands, send_sems, recv_sems):
    x, y, c = _position()
    return [pltpu.make_async_remote_copy(
        src_ref=srcs[a].at[k], dst_ref=lands[a].at[k],
        send_sem=send_sems.at[(N_CHIPS - 1) * a + k], recv_sem=recv_sems.at[(N_CHIPS - 1) * a + k],
        device_id=(*chip, c), device_id_type=_MESH)
        for a in range(len(srcs)) for k, chip in enumerate(_chip_order(x, y)[1:])]


def _sibling_copies(srcs, lands, send_sems, recv_sems):
    x, y, c = _position()
    return [pltpu.make_async_remote_copy(
        src_ref=srcs[a].at[s], dst_ref=lands[a].at[s],
        send_sem=send_sems.at[N_CHIPS * a + s], recv_sem=recv_sems.at[N_CHIPS * a + s],
        device_id=(x, y, 1 - c), device_id_type=_MESH)
        for a in range(len(srcs)) for s in range(N_CHIPS)]


def _sibling_copies_by_device(srcs, lands, send_sems, recv_sems):
    x, y, c = _position()
    return [pltpu.make_async_remote_copy(
        src_ref=srcs[a].at[_dev_index(*chip, 1 - c)], dst_ref=lands[a].at[s],
        send_sem=send_sems.at[N_CHIPS * a + s], recv_sem=recv_sems.at[N_CHIPS * a + s],
        device_id=(x, y, 1 - c), device_id_type=_MESH)
        for a in range(len(srcs)) for s, chip in enumerate(_chip_order(x, y))]


def _exchange_start(copies, sums, name, land_slabs=None):
    n = len(sums)
    land_shapes = [((land_slabs or s.shape[0]),) + s.shape[1:] for s in sums]
    n_copies = sum(shape[0] for shape in land_shapes)

    def body(*refs):
        srcs, lands = refs[:n], refs[n:2 * n]
        send_sems, recv_sems, token = refs[2 * n], refs[2 * n + 1], refs[-1]
        for cp in copies(srcs, lands, send_sems, recv_sems):
            cp.start()
        token[...] = jnp.zeros_like(token)

    hbm = [pltpu.HBM(s.shape, s.dtype) for s in sums] + [pltpu.HBM(shape, s.dtype) for shape, s in zip(land_shapes, sums)]
    outs = pl.pallas_call(
        body, name=name,
        out_shape=(pltpu.SemaphoreType.DMA((n_copies,)), pltpu.SemaphoreType.DMA((n_copies,)), *hbm,
                   jax.ShapeDtypeStruct((SUBLANES, LANES), _F32)),
        in_specs=[_HBM] * (2 * n), out_specs=(_SEM, _SEM, *[_HBM] * (2 * n), _VMEM),
        input_output_aliases={i: 2 + i for i in range(2 * n)},
        compiler_params=pltpu.CompilerParams(has_side_effects=_EFFECT),
    )(*[pltpu.with_memory_space_constraint(s, pltpu.HBM) for s in sums],
      *[pltpu.with_memory_space_constraint(lax.empty(shape, s.dtype), pltpu.HBM) for shape, s in zip(land_shapes, sums)])
    return outs[0], outs[1], list(outs[2:2 + n]), list(outs[2 + n:2 + 2 * n]), outs[-1]


def _exchange_wait(copies, send_sems, recv_sems, srcs, lands, after, name, with_sources=False):
    n = len(srcs)

    def body(*refs):
        for cp in copies(refs[:n], refs[n:2 * n], refs[2 * n], refs[2 * n + 1]):
            cp.wait_send()
            cp.wait_recv()

    hbm = [pltpu.HBM(s.shape, s.dtype) for s in (*srcs, *lands)]
    outs = pl.pallas_call(
        body, name=name, out_shape=tuple(hbm),
        in_specs=[_HBM] * (2 * n) + [_SEM, _SEM, _ANY], out_specs=[_HBM] * (2 * n),
        input_output_aliases={i: i for i in range(2 * n)},
        compiler_params=pltpu.CompilerParams(has_side_effects=_EFFECT),
    )(*srcs, *lands, send_sems, recv_sems, after)
    return list(outs) if with_sources else list(outs[n:])


def _matmul(a, b, *, ta=False, tb=False, out_dtype=_F32, tm=1024, tn=1024, tk=4608, col_slabs=None, after=None, name):
    kdim, m = a.shape if ta else a.shape[::-1]
    n = b.shape[0] if tb else b.shape[1]
    tm = _tile(m, tm, LANES if ta else BF16_ROWS)
    tn = n // col_slabs if col_slabs else _tile(n, tn, LANES)
    tk = _tile(kdim, tk, BF16_ROWS if ta else LANES)
    nk = kdim // tk
    dims = (((0 if ta else 1,), (1 if tb else 0,)), ((), ()))

    def body_whole_k(a_ref, b_ref, *rest):
        o_ref = rest[-1]
        part = lax.dot_general(a_ref[...], b_ref[...], dims, preferred_element_type=_F32)
        o_ref[...] = part.astype(out_dtype).reshape(o_ref.shape)

    def body_split_k(a_ref, b_ref, *rest):
        o_ref, acc_ref = rest[-2:]
        k = pl.program_id(2)

        @pl.when(k == 0)
        def _():
            acc_ref[...] = jnp.zeros_like(acc_ref)

        acc_ref[...] += lax.dot_general(a_ref[...], b_ref[...], dims, preferred_element_type=_F32)

        @pl.when(k == nk - 1)
        def _():
            o_ref[...] = acc_ref[...].astype(out_dtype).reshape(o_ref.shape)

    a_spec = pl.BlockSpec((tk, tm), lambda i, j, k: (k, i)) if ta else pl.BlockSpec((tm, tk), lambda i, j, k: (i, k))
    b_spec = pl.BlockSpec((tn, tk), lambda i, j, k: (j, k)) if tb else pl.BlockSpec((tk, tn), lambda i, j, k: (k, j))
    if col_slabs:
        out_spec = pl.BlockSpec((1, tm, tn), lambda i, j, k: (j, i, 0))
        out_shape = jax.ShapeDtypeStruct((col_slabs, m, tn), out_dtype)
    else:
        out_spec = pl.BlockSpec((tm, tn), lambda i, j, k: (i, j))
        out_shape = jax.ShapeDtypeStruct((m, n), out_dtype)
    extra = [] if after is None else [after]
    return pl.pallas_call(
        body_whole_k if nk == 1 else body_split_k, name=name, grid=(m // tm, n // tn, nk),
        in_specs=[a_spec, b_spec] + [pl.BlockSpec(t.shape, lambda i, j, k: (0, 0)) for t in extra],
        out_specs=out_spec, out_shape=out_shape,
        scratch_shapes=[] if nk == 1 else [pltpu.VMEM((tm, tn), _F32)],
        compiler_params=pltpu.CompilerParams(dimension_semantics=("parallel", "parallel", "arbitrary")),
    )(a, b, *extra)


def _proj_blocks(a, wg, dst, order_ids, first, count, width, after, name):
    m, kdim = a.shape
    tm = _tile(m, 1088, BF16_ROWS)

    def body(ids_ref, a_ref, w_ref, after_ref, dst_ref, o_ref):
        del ids_ref, after_ref, dst_ref
        o_ref[...] = jnp.dot(a_ref[...], w_ref[...], preferred_element_type=_F32)

    return pl.pallas_call(
        body, name=name,
        grid_spec=pltpu.PrefetchScalarGridSpec(
            num_scalar_prefetch=1, grid=(count, m // tm),
            in_specs=[pl.BlockSpec((tm, kdim), lambda j, i, ids: (i, 0)),
                      pl.BlockSpec((kdim, width), lambda j, i, ids: (0, ids[first + j])),
                      pl.BlockSpec(after.shape, lambda j, i, ids: (0, 0)), _ANY],
            out_specs=pl.BlockSpec((tm, width), lambda j, i, ids: (i, ids[first + j]))),
        out_shape=jax.ShapeDtypeStruct(dst.shape, dst.dtype),
        input_output_aliases={4: 0},
        compiler_params=pltpu.CompilerParams(dimension_semantics=("arbitrary", "arbitrary")),
    )(order_ids, a, wg, after, dst)


def _cast_into_columns(slab_ids, a, n_blocks, name):
    r, c = a.shape
    tr = _row_tile(r, c)

    def body(ids_ref, a_ref, o_ref):
        del ids_ref
        o_ref[...] = a_ref[...].astype(_BF16)

    return pl.pallas_call(
        body, name=name,
        grid_spec=pltpu.PrefetchScalarGridSpec(
            num_scalar_prefetch=1, grid=(r // tr,),
            in_specs=[pl.BlockSpec((tr, c), lambda i, ids: (i, 0))],
            out_specs=pl.BlockSpec((tr, c), lambda i, ids: (i, ids[0]))),
        out_shape=jax.ShapeDtypeStruct((r, n_blocks * c), _BF16),
        compiler_params=pltpu.CompilerParams(dimension_semantics=("parallel",)),
    )(slab_ids, a)


def _matmul_slabs(a, b, ids, width, name, after=None):
    kdim, m = a.shape
    n_slabs = ids.shape[0]
    tm = _tile(m, 1024, LANES)

    def body(ids_ref, a_ref, b_ref, *rest):
        del ids_ref
        rest[-1][0] = lax.dot_general(a_ref[...], b_ref[...], (((0,), (0,)), ((), ())), preferred_element_type=_F32)

    extra = [] if after is None else [after]
    return pl.pallas_call(
        body, name=name,
        grid_spec=pltpu.PrefetchScalarGridSpec(
            num_scalar_prefetch=1, grid=(m // tm, n_slabs),
            in_specs=[pl.BlockSpec((kdim, tm), lambda i, j, ids: (0, i)),
                      pl.BlockSpec((kdim, width), lambda i, j, ids: (0, ids[j]))]
            + [pl.BlockSpec(t.shape, lambda i, j, ids: (0, 0)) for t in extra],
            out_specs=pl.BlockSpec((1, tm, width), lambda i, j, ids: (j, i, 0))),
        out_shape=jax.ShapeDtypeStruct((n_slabs, m, width), _F32),
        compiler_params=pltpu.CompilerParams(dimension_semantics=("parallel", "parallel")),
    )(ids, a, b, *extra)


def _row_tile(rows, cols):
    return _tile(rows, max(BF16_ROWS, min(512, (1 << 19) // cols)), BF16_ROWS)


def _cast_bf16(a, name):
    r, c = a.shape
    tr = _row_tile(r, c)

    def body(a_ref, o_ref):
        o_ref[...] = a_ref[...].astype(_BF16)

    blk = pl.BlockSpec((tr, c), lambda i: (i, 0))
    return pl.pallas_call(
        body, name=name, grid=(r // tr,), in_specs=[blk], out_specs=blk,
        out_shape=jax.ShapeDtypeStruct((r, c), _BF16),
        compiler_params=pltpu.CompilerParams(dimension_semantics=("parallel",)),
    )(a)


def _cast_into_slab(slab_ids, a, name):
    r, c = a.shape
    tr = _row_tile(r, c)

    def body(ids_ref, a_ref, o_ref):
        del ids_ref
        o_ref[0] = a_ref[...].astype(_BF16)

    return pl.pallas_call(
        body, name=name,
        grid_spec=pltpu.PrefetchScalarGridSpec(
            num_scalar_prefetch=1, grid=(r // tr,),
            in_specs=[pl.BlockSpec((tr, c), lambda i, ids: (i, 0))],
            out_specs=pl.BlockSpec((1, tr, c), lambda i, ids: (ids[0], i, 0))),
        out_shape=jax.ShapeDtypeStruct((N_DEV, r, c), _BF16),
        compiler_params=pltpu.CompilerParams(dimension_semantics=("parallel",)),
    )(slab_ids, a)


def _pre_add(slab_ids, grad, got, name):
    _, r, c = grad.shape
    tr = _row_tile(r, c)

    def body(ids_ref, a_ref, b_ref, o_ref):
        del ids_ref
        o_ref[...] = (a_ref[...] + b_ref[...]).astype(_BF16)

    return pl.pallas_call(
        body, name=name,
        grid_spec=pltpu.PrefetchScalarGridSpec(
            num_scalar_prefetch=1, grid=(N_CHIPS - 1, r // tr),
            in_specs=[pl.BlockSpec((1, tr, c), lambda s, i, ids: (ids[s + 1], i, 0)),
                      pl.BlockSpec((1, tr, c), lambda s, i, ids: (s + 1, i, 0))],
            out_specs=pl.BlockSpec((1, tr, c), lambda s, i, ids: (s, i, 0))),
        out_shape=jax.ShapeDtypeStruct((N_CHIPS - 1, r, c), _BF16),
        compiler_params=pltpu.CompilerParams(dimension_semantics=("parallel", "parallel")),
    )(slab_ids, grad, got)


def _ada_forward(craw, w_shard, name):
    d, cols = w_shard.shape
    tk = _tile(d, 512, LANES)

    def body(c_ref, w_ref, o_ref):
        @pl.when(pl.program_id(0) == 0)
        def _():
            o_ref[...] = jnp.zeros_like(o_ref)

        o_ref[...] += jnp.dot(_silu(c_ref[...]).astype(_BF16), w_ref[...].astype(_BF16), preferred_element_type=_F32)

    return pl.pallas_call(
        body, name=name, grid=(d // tk,),
        in_specs=[pl.BlockSpec((craw.shape[0], tk), lambda k: (0, k)), pl.BlockSpec((tk, cols), lambda k: (k, 0))],
        out_specs=pl.BlockSpec((craw.shape[0], cols), lambda k: (0, 0)),
        out_shape=jax.ShapeDtypeStruct((craw.shape[0], cols), _F32),
        compiler_params=pltpu.CompilerParams(dimension_semantics=("arbitrary",)),
    )(craw, w_shard)


def _ada_backward(craw, dmod, w, m, v, name):
    d, cols = w.shape
    rows = craw.shape[0]
    tr = _tile(d, 256, LANES)

    def body(c_ref, dm_ref, w_ref, m_ref, v_ref, g_ref, dl_ref, nm_ref, nv_ref, dc_ref):
        act = _silu(c_ref[...]).astype(_BF16)
        dmb = dm_ref[...].astype(_BF16)
        wv = w_ref[...]
        g = lax.dot_general(act, dmb, (((0,), (0,)), ((), ())), preferred_element_type=_F32)
        delta, nm, nv = _adamw(wv, g, m_ref[...], v_ref[...])
        g_ref[...] = g
        dl_ref[...] = delta
        nm_ref[...] = nm
        nv_ref[...] = nv
        dc = lax.dot_general(dmb, wv.astype(_BF16), (((1,), (1,)), ((), ())), preferred_element_type=_F32)
        dc_ref[...] = jnp.broadcast_to(jnp.sum(dc[N_DEV:], axis=0, keepdims=True), dc_ref.shape)

    blk = pl.BlockSpec((tr, cols), lambda i: (i, 0))
    return pl.pallas_call(
        body, name=name, grid=(d // tr,),
        in_specs=[pl.BlockSpec((rows, tr), lambda i: (0, i)), pl.BlockSpec((rows, cols), lambda i: (0, 0)), blk, blk, blk],
        out_specs=[blk, blk, blk, blk, pl.BlockSpec((SUBLANES, tr), lambda i: (0, i))],
        out_shape=[jax.ShapeDtypeStruct((d, cols), _F32)] * 4 + [jax.ShapeDtypeStruct((SUBLANES, d), _F32)],
        compiler_params=pltpu.CompilerParams(dimension_semantics=("parallel",)),
    )(craw, dmod, w, m, v)


def _rms(xf):
    return lax.rsqrt(jnp.mean(xf * xf, axis=-1, keepdims=True) + EPS)


def _head_mean(v):
    hi = v.astype(_BF16)
    lo = (v - hi.astype(_F32)).astype(_BF16)
    ones = jnp.full((2 * HEAD_DIM, HEAD_DIM), 1.0 / HEAD_DIM, _BF16)
    return jnp.dot(jnp.concatenate([hi, lo], axis=1), ones, preferred_element_type=_F32)


def _prenorm(ctx, x, g_pre, mods, tr, name):
    l, d = ctx.shape
    n = x.shape[0]
    nbc = l // tr

    def body(ctx_ref, x_ref, g_ref, mod_ref, h_ref):
        def emit(src_ref):
            xf = src_ref[...]
            y = (xf * _rms(xf)) * g_ref[...]
            h_ref[...] = (y * (1.0 + mod_ref[0, 0:1, :]) + mod_ref[0, 1:2, :]).astype(_BF16)

        is_ctx = pl.program_id(0) < nbc
        pl.when(is_ctx)(lambda: emit(ctx_ref))
        pl.when(jnp.logical_not(is_ctx))(lambda: emit(x_ref))

    return pl.pallas_call(
        body, name=name, grid=((l + n) // tr,),
        in_specs=[pl.BlockSpec((tr, d), lambda i: (jnp.minimum(i, nbc - 1), 0)),
                  pl.BlockSpec((tr, d), lambda i: (jnp.maximum(i - nbc, 0), 0)),
                  pl.BlockSpec((1, d), lambda i: (0, 0)),
                  pl.BlockSpec((1, SUBLANES, d), lambda i: ((i >= nbc).astype(jnp.int32), 0, 0))],
        out_specs=pl.BlockSpec((tr, d), lambda i: (i, 0)),
        out_shape=jax.ShapeDtypeStruct((l + n, d), _BF16),
        compiler_params=pltpu.CompilerParams(dimension_semantics=("arbitrary",)),
    )(ctx, x, g_pre, mods)


def _prenorm_backward(dh, ctx, x, dxn, g_pre, mods, tr, name):
    l, d = ctx.shape
    n = x.shape[0]
    nbc = l // tr

    def body(dh_ref, ctx_ref, x_ref, dxn_ref, g_ref, mod_ref, gx_ref, dmod_ref, dg_ref):
        i = pl.program_id(0)

        @pl.when(i == 0)
        def _():
            dg_ref[...] = jnp.zeros_like(dg_ref)

        @pl.when(jnp.logical_or(i == 0, i == nbc))
        def _():
            dmod_ref[...] = jnp.zeros_like(dmod_ref)

        def emit(src_ref, latent):
            xf = src_ref[...]
            r = _rms(xf)
            xn = xf * r
            dhv = dh_ref[...]
            one_scale = 1.0 + mod_ref[0, 0:1, :]
            dmod_ref[0, 0:1, :] += jnp.sum(dhv * (xn * g_ref[...]), axis=0, keepdims=True)
            dmod_ref[0, 1:2, :] += jnp.sum(dhv, axis=0, keepdims=True)
            dyg = dhv * one_scale
            dg_ref[0:1, :] += jnp.sum(dyg * xn, axis=0, keepdims=True)
            if latent:
                dn = dyg * g_ref[...]
                gx_ref[...] = dxn_ref[...] + r * (dn - xn * jnp.mean(dn * xn, axis=-1, keepdims=True))

        pl.when(i < nbc)(lambda: emit(ctx_ref, False))
        pl.when(i >= nbc)(lambda: emit(x_ref, True))

    lat = pl.BlockSpec((tr, d), lambda i: (jnp.maximum(i - nbc, 0), 0))
    sel = pl.BlockSpec((1, SUBLANES, d), lambda i: ((i >= nbc).astype(jnp.int32), 0, 0))
    return pl.pallas_call(
        body, name=name, grid=((l + n) // tr,),
        in_specs=[pl.BlockSpec((tr, d), lambda i: (i, 0)),
                  pl.BlockSpec((tr, d), lambda i: (jnp.minimum(i, nbc - 1), 0)),
                  lat, lat, pl.BlockSpec((1, d), lambda i: (0, 0)), sel],
        out_specs=[lat, sel, pl.BlockSpec((SUBLANES, d), lambda i: (0, 0))],
        out_shape=[jax.ShapeDtypeStruct((n, d), _F32), jax.ShapeDtypeStruct((2, SUBLANES, d), _F32),
                   jax.ShapeDtypeStruct((SUBLANES, d), _F32)],
        compiler_params=pltpu.CompilerParams(dimension_semantics=("arbitrary",)),
    )(dh, ctx, x, dxn, g_pre, mods)


def _rope_tables(l, n):
    rows = n // GRID_W
    row = jnp.repeat(jnp.arange(rows, dtype=_F32), GRID_W)
    col = jnp.tile(jnp.arange(GRID_W, dtype=_F32), rows)
    inv = ROPE_THETA ** (-jnp.arange(ROPE_PAIRS, dtype=_F32) / ROPE_PAIRS)
    ang_r, ang_c = row[:, None] * inv, col[:, None] * inv
    cr, sr, cc, sc = jnp.cos(ang_r), jnp.sin(ang_r), jnp.cos(ang_c), jnp.sin(ang_c)
    zero = jnp.zeros_like(sr)
    tc = jnp.concatenate([cr, cr, cc, cc], axis=-1)
    ta = jnp.concatenate([-sr, zero, -sc, zero], axis=-1)
    tb = jnp.concatenate([zero, sr, zero, sc], axis=-1)
    pad = lambda t, fill: jnp.concatenate([jnp.full((l, HEAD_DIM), fill, _F32), t], axis=0)
    return pad(tc, 1.0), pad(ta, 0.0), pad(tb, 0.0)


def _rope(y, tc, ta, tb):
    return y * tc + pltpu.roll(y, HEAD_DIM - ROPE_PAIRS, 1) * ta + pltpu.roll(y, ROPE_PAIRS, 1) * tb


def _rope_transposed(dy, tc, ta, tb):
    return dy * tc + pltpu.roll(dy * ta, ROPE_PAIRS, 1) + pltpu.roll(dy * tb, HEAD_DIM - ROPE_PAIRS, 1)


def _qkv_post(proj, tables, g_q, g_k, heads, kv_heads, tr, name):
    t = proj.shape[0]
    aw, kw = heads * HEAD_DIM, kv_heads * HEAD_DIM
    w3 = aw + 2 * kw

    def body(p_ref, tc_ref, ta_ref, tb_ref, gq_ref, gk_ref, q_ref, k_ref, v_ref):
        tabs = (tc_ref[...], ta_ref[...], tb_ref[...])

        def norm_rope(col, gain):
            xh = p_ref[:, col:col + HEAD_DIM]
            return _rope((xh * lax.rsqrt(_head_mean(xh * xh) + EPS)) * gain, *tabs).astype(_BF16)

        for h in range(heads):
            q_ref[h] = norm_rope(h * HEAD_DIM, gq_ref[...])
        for h in range(kv_heads):
            k_ref[h] = norm_rope(aw + h * HEAD_DIM, gk_ref[...])
            v_ref[h] = p_ref[:, aw + kw + h * HEAD_DIM:aw + kw + (h + 1) * HEAD_DIM].astype(_BF16)

    tab = pl.BlockSpec((tr, HEAD_DIM), lambda i: (i, 0))
    gain = pl.BlockSpec((1, HEAD_DIM), lambda i: (0, 0))
    return pl.pallas_call(
        body, name=name, grid=(t // tr,),
        in_specs=[pl.BlockSpec((tr, w3), lambda i: (i, 0)), tab, tab, tab, gain, gain],
        out_specs=[pl.BlockSpec((heads, tr, HEAD_DIM), lambda i: (0, i, 0)),
                   pl.BlockSpec((kv_heads, tr, HEAD_DIM), lambda i: (0, i, 0)),
                   pl.BlockSpec((kv_heads, tr, HEAD_DIM), lambda i: (0, i, 0))],
        out_shape=[jax.ShapeDtypeStruct((heads, t, HEAD_DIM), _BF16),
                   jax.ShapeDtypeStruct((kv_heads, t, HEAD_DIM), _BF16),
                   jax.ShapeDtypeStruct((kv_heads, t, HEAD_DIM), _BF16)],
        compiler_params=pltpu.CompilerParams(dimension_semantics=("parallel",)),
    )(proj, *tables, g_q, g_k)


def _qkv_post_backward(proj, dq, dk, dv, tables, g_q, g_k, dproj, l, tr, name):
    t = proj.shape[0]
    heads, kv_heads = dq.shape[0], dk.shape[0]
    aw, kw = heads * HEAD_DIM, kv_heads * HEAD_DIM
    w3 = aw + 2 * kw
    nbc = l // tr

    def body(p_ref, dq_ref, dk_ref, dv_ref, tc_ref, ta_ref, tb_ref, gq_ref, gk_ref, dproj_ref, o_ref, dgq_ref, dgk_ref):
        del dproj_ref
        i = pl.program_id(0)

        @pl.when(i == 0)
        def _():
            dgq_ref[...] = jnp.zeros_like(dgq_ref)
            dgk_ref[...] = jnp.zeros_like(dgk_ref)

        tabs = (tc_ref[...], ta_ref[...], tb_ref[...])
        latent = i >= nbc

        def back(col, dout, gain, dg_ref):
            xh = p_ref[:, col:col + HEAD_DIM]
            r = lax.rsqrt(_head_mean(xh * xh) + EPS)
            xn = xh * r
            dy = _rope_transposed(dout, *tabs)
            dg_ref[0:1, :] += jnp.sum(dy * xn, axis=0, keepdims=True)
            dn = dy * gain
            o_ref[:, col:col + HEAD_DIM] = (r * (dn - xn * _head_mean(dn * xn))).astype(_BF16)

        for h in range(heads):
            back(h * HEAD_DIM, jnp.where(latent, dq_ref[h], 0.0), gq_ref[...], dgq_ref)
        for h in range(kv_heads):
            back(aw + h * HEAD_DIM, dk_ref[h], gk_ref[...], dgk_ref)
            o_ref[:, aw + kw + h * HEAD_DIM:aw + kw + (h + 1) * HEAD_DIM] = dv_ref[h].astype(_BF16)

    tab = pl.BlockSpec((tr, HEAD_DIM), lambda i: (i, 0))
    gain = pl.BlockSpec((1, HEAD_DIM), lambda i: (0, 0))
    acc = pl.BlockSpec((SUBLANES, HEAD_DIM), lambda i: (0, 0))
    return pl.pallas_call(
        body, name=name, grid=(t // tr,),
        in_specs=[pl.BlockSpec((tr, w3), lambda i: (i, 0)),
                  pl.BlockSpec((heads, tr, HEAD_DIM), lambda i: (0, jnp.maximum(i - nbc, 0), 0)),
                  pl.BlockSpec((kv_heads, tr, HEAD_DIM), lambda i: (0, i, 0)),
                  pl.BlockSpec((kv_heads, tr, HEAD_DIM), lambda i: (0, i, 0)),
                  tab, tab, tab, gain, gain, _ANY],
        out_specs=[pl.BlockSpec((tr, w3), lambda i: (i, 0)), acc, acc],
        out_shape=[jax.ShapeDtypeStruct(dproj.shape, dproj.dtype), jax.ShapeDtypeStruct((SUBLANES, HEAD_DIM), _F32),
                   jax.ShapeDtypeStruct((SUBLANES, HEAD_DIM), _F32)],
        input_output_aliases={9: 0},
        compiler_params=pltpu.CompilerParams(dimension_semantics=("arbitrary",)),
    )(proj, dq, dk, dv, *tables, g_q, g_k, dproj)


def _zero_context_rows(dproj, l, w3, tr, name):
    t, iw = dproj.shape

    def body(dproj_ref, o_ref):
        del dproj_ref
        o_ref[...] = jnp.zeros_like(o_ref)

    return pl.pallas_call(
        body, name=name, grid=(l // tr, iw // w3 - 1),
        in_specs=[_ANY], out_specs=pl.BlockSpec((tr, w3), lambda i, j: (i, j + 1)),
        out_shape=jax.ShapeDtypeStruct(dproj.shape, dproj.dtype), input_output_aliases={0: 0},
        compiler_params=pltpu.CompilerParams(dimension_semantics=("parallel", "parallel")),
    )(dproj)


def _attention(q, k, v, proj, l, mix, tq, name):
    heads, t, _ = q.shape
    kv_heads = k.shape[0]
    n = t - l
    rows = GQA_GROUP * tq
    gw = GQA_GROUP * HEAD_DIM
    aw = heads * HEAD_DIM
    gate_col = (aw + 2 * kv_heads * HEAD_DIM) // gw
    off = l // tq

    def body(q_ref, k_ref, v_ref, g_ref, o_ref, y_ref, lse_ref):
        lane = lax.broadcasted_iota(jnp.int32, (tq, LANES), 1)
        lse_blk = jnp.zeros((tq, LANES), _F32)
        for first in range(0, GQA_GROUP, ATTN_SUB_HEADS):
            qs = q_ref[first:first + ATTN_SUB_HEADS].reshape(ATTN_SUB_HEADS * tq, HEAD_DIM)
            raw = lax.dot_general(qs, k_ref[0], (((1,), (1,)), ((), ())), preferred_element_type=_F32)
            m = jnp.max(raw, axis=-1, keepdims=True)
            p = jnp.exp2((raw - m) * (ATTN_SCALE * LOG2_E))
            denom = jnp.sum(p, axis=-1, keepdims=True)
            os_ = jnp.dot(p.astype(_BF16), v_ref[0], preferred_element_type=_F32) / denom
            lse_s = m * ATTN_SCALE + jnp.log(denom)
            for j in range(ATTN_SUB_HEADS):
                g = first + j
                og = os_[j * tq:(j + 1) * tq]
                cols = slice(g * HEAD_DIM, (g + 1) * HEAD_DIM)
                o_ref[:, cols] = og
                y_ref[:, cols] = (og * _silu(g_ref[:, cols])).astype(_BF16)
                lse_blk = jnp.where(lane == g, lse_s[j * tq:(j + 1) * tq], lse_blk)
        lse_ref[0] = lse_blk

    return pl.pallas_call(
        body, name=name, grid=(kv_heads, n // tq),
        in_specs=[pl.BlockSpec((GQA_GROUP, tq, HEAD_DIM), lambda h, i: (h, i + off, 0)),
                  pl.BlockSpec((1, t, HEAD_DIM), lambda h, i: (h, 0, 0)),
                  pl.BlockSpec((1, t, HEAD_DIM), lambda h, i: (h, 0, 0)),
                  pl.BlockSpec((tq, gw), lambda h, i: (i + off, gate_col + h))],
        out_specs=[pl.BlockSpec((tq, gw), lambda h, i: (i, h)),
                   pl.BlockSpec((tq, gw), lambda h, i: (i, h)),
                   pl.BlockSpec((1, tq, LANES), lambda h, i: (h, i, 0))],
        out_shape=[jax.ShapeDtypeStruct((n, aw), _F32), jax.ShapeDtypeStruct((n, mix), _BF16),
                   jax.ShapeDtypeStruct((kv_heads, n, LANES), _F32)],
        compiler_params=pltpu.CompilerParams(dimension_semantics=("parallel", "parallel")),
    )(q, k, v, proj)


def _attention_backward(q, k, v, attn_o, dy, proj, lse, after, dproj, l, tq, name):
    heads, t, _ = q.shape
    kv_heads = k.shape[0]
    n = t - l
    rows = GQA_GROUP * tq
    gw = GQA_GROUP * HEAD_DIM
    aw = heads * HEAD_DIM
    gate_col = (aw + 2 * kv_heads * HEAD_DIM) // gw
    off = l // tq
    n_parts = next(p for p in (ATTN_KEY_PARTS, 2, 1) if t % (p * BF16_ROWS) == 0)
    part = t // n_parts

    def body(q_ref, k_ref, v_ref, o_ref, dy_ref, g_ref, lse_ref, after_ref, dproj_ref, dq_ref, dg_ref, dk_ref, dv_ref):
        del after_ref, dproj_ref

        @pl.when(pl.program_id(1) == 0)
        def _():
            dk_ref[...] = jnp.zeros_like(dk_ref)
            dv_ref[...] = jnp.zeros_like(dv_ref)

        q4 = q_ref[...].reshape(rows, HEAD_DIM)
        do_parts, delta_parts, lse_parts = [], [], []
        lse_blk = lse_ref[0]
        for g in range(GQA_GROUP):
            cols = slice(g * HEAD_DIM, (g + 1) * HEAD_DIM)
            gate, og, dyg = g_ref[:, cols], o_ref[:, cols], dy_ref[:, cols]
            dog = dyg * _silu(gate)
            dg_ref[:, cols] = (dyg * og * _silu_grad(gate)).astype(_BF16)
            do_parts.append(dog)
            delta_parts.append(jnp.sum(dog * og, axis=-1, keepdims=True))
            lse_parts.append(lse_blk[:, g:g + 1])
        do4 = jnp.concatenate(do_parts, axis=0).astype(_BF16)
        delta4 = jnp.concatenate(delta_parts, axis=0)
        lse4 = jnp.concatenate(lse_parts, axis=0)
        dq4 = jnp.zeros((rows, HEAD_DIM), _F32)
        for part_i in range(n_parts):
            keys = slice(part_i * part, (part_i + 1) * part)
            ks, vs = k_ref[0, keys, :], v_ref[0, keys, :]
            s = lax.dot_general(q4, ks, (((1,), (1,)), ((), ())), preferred_element_type=_F32) * ATTN_SCALE
            p = jnp.exp(s - lse4)
            dp = lax.dot_general(do4, vs, (((1,), (1,)), ((), ())), preferred_element_type=_F32)
            ds = (p * (dp - delta4) * ATTN_SCALE).astype(_BF16)
            dq4 = dq4 + jnp.dot(ds, ks, preferred_element_type=_F32)
            dk_ref[0, keys, :] += lax.dot_general(ds, q4, (((0,), (0,)), ((), ())), preferred_element_type=_F32)
            dv_ref[0, keys, :] += lax.dot_general(
                p.astype(_BF16), do4, (((0,), (0,)), ((), ())), preferred_element_type=_F32)
        dq_ref[...] = dq4.reshape(GQA_GROUP, tq, HEAD_DIM)

    kv_spec = pl.BlockSpec((1, t, HEAD_DIM), lambda h, i: (h, 0, 0))
    tok = pl.BlockSpec((tq, gw), lambda h, i: (i, h))
    gate = pl.BlockSpec((tq, gw), lambda h, i: (i + off, gate_col + h))
    return pl.pallas_call(
        body, name=name, grid=(kv_heads, n // tq),
        in_specs=[pl.BlockSpec((GQA_GROUP, tq, HEAD_DIM), lambda h, i: (h, i + off, 0)), kv_spec, kv_spec,
                  tok, tok, gate, pl.BlockSpec((1, tq, LANES), lambda h, i: (h, i, 0)),
                  pl.BlockSpec(after.shape, lambda h, i: (0, 0)), _ANY],
        out_specs=[pl.BlockSpec((GQA_GROUP, tq, HEAD_DIM), lambda h, i: (h, i, 0)), gate, kv_spec, kv_spec],
        out_shape=[jax.ShapeDtypeStruct((heads, n, HEAD_DIM), _F32), jax.ShapeDtypeStruct(dproj.shape, dproj.dtype),
                   jax.ShapeDtypeStruct((kv_heads, t, HEAD_DIM), _F32), jax.ShapeDtypeStruct((kv_heads, t, HEAD_DIM), _F32)],
        input_output_aliases={8: 1},
        compiler_params=pltpu.CompilerParams(dimension_semantics=("parallel", "arbitrary")),
    )(q, k, v, attn_o, dy, proj, lse, after, dproj)


def _halo_specs(tp, width, col, row_off, total_rows):
    per = tp // POOL_HALO
    first = row_off // POOL_HALO
    last = total_rows // POOL_HALO - 1
    return [pl.BlockSpec((tp, width), lambda i: (i + row_off // tp, col)),
            pl.BlockSpec((POOL_HALO, width), lambda i: (jnp.maximum(first + i * per - 1, 0), col)),
            pl.BlockSpec((POOL_HALO, width), lambda i: (jnp.minimum(first + (i + 1) * per, last), col))]


def _with_halo(cur, prev, nxt, t0, n):
    tp = cur.shape[0]
    r8 = lax.broadcasted_iota(jnp.int32, (POOL_HALO, 1), 0)
    prev = jnp.where(t0 - POOL_HALO + r8 >= 0, prev, 0.0)
    nxt = jnp.where(t0 + tp + r8 < n, nxt, 0.0)
    return jnp.concatenate([prev, cur, nxt], axis=0)


def _shift_rows(a, s):
    return pltpu.roll(a, s % a.shape[0], 0)


def _window_sum(e, w, mirrored):
    a = e + _shift_rows(e, -1 if mirrored else 1)
    s = 1
    while 2 * s < w:
        a = _shift_rows(a, s) + _shift_rows(a, -s)
        s *= 2
    return a


def _window_count(t, w, n):
    half = w // 2
    return (jnp.minimum(t + half, n) - jnp.maximum(t - half, 0)).astype(_F32)


def _pool_forward(gi, proj, y, pool_w, pool_scale, l, heads, kv_heads, tp, name):
    t = proj.shape[0]
    n = t - l
    pg = pool_w.shape[-1]
    w = POOL_WINDOWS[gi]
    aw, kw = heads * HEAD_DIM, kv_heads * HEAD_DIM
    u_col = (2 * aw + 2 * kw) // pg + gi
    gate_col = (2 * aw + 2 * kw + len(POOL_WINDOWS) * pg) // pg + gi

    def body(u_ref, up_ref, un_ref, g_ref, w_ref, sc_ref, y_in_ref, y_ref, raw_ref, d_ref):
        del y_in_ref
        t0 = pl.program_id(0) * tp
        cur = u_ref[...]
        win = _window_sum(_with_halo(cur, up_ref[...], un_ref[...], t0, n), w, False)[POOL_HALO:POOL_HALO + tp]
        tok = t0 + lax.broadcasted_iota(jnp.int32, (tp, 1), 0)
        d = (win / _window_count(tok, w, n) - cur).astype(_BF16)
        raw = jnp.dot(d, w_ref[...].reshape(pg, pg), preferred_element_type=_F32)
        d_ref[...] = d
        raw_ref[...] = raw
        y_ref[...] = ((raw * sc_ref[...]) * _silu(g_ref[...])).astype(_BF16)

    blk = pl.BlockSpec((tp, pg), lambda i: (i, 0))
    return pl.pallas_call(
        body, name=name, grid=(n // tp,),
        in_specs=_halo_specs(tp, pg, u_col, l, t) + [
            pl.BlockSpec((tp, pg), lambda i: (i + l // tp, gate_col)),
            pl.BlockSpec((N_DEV, 1, pg // N_DEV, pg), lambda i: (0, gi, 0, 0)),
            pl.BlockSpec((1, pg), lambda i: (0, gi)), _ANY],
        out_specs=[pl.BlockSpec((tp, pg), lambda i: (i, aw // pg + gi)), blk, blk],
        out_shape=[jax.ShapeDtypeStruct(y.shape, y.dtype), jax.ShapeDtypeStruct((n, pg), _F32),
                   jax.ShapeDtypeStruct((n, pg), _BF16)],
        input_output_aliases={6: 0},
        compiler_params=pltpu.CompilerParams(dimension_semantics=("arbitrary",)),
    )(proj, proj, proj, proj, pool_w, pool_scale, y)


def _pool_backward_gate(gi, dy, proj, raw, pool_w, pool_scale, dproj, l, heads, kv_heads, tp, name):
    n, pg = raw.shape
    aw, kw = heads * HEAD_DIM, kv_heads * HEAD_DIM
    gate_col = (2 * aw + 2 * kw + len(POOL_WINDOWS) * pg) // pg + gi

    def body(dy_ref, g_ref, raw_ref, w_ref, sc_ref, dproj_ref, dg_ref, dr_ref, dd_ref, ds_ref):
        del dproj_ref

        @pl.when(pl.program_id(0) == 0)
        def _():
            ds_ref[...] = jnp.zeros_like(ds_ref)

        gate, rawv, dyv, scale = g_ref[...], raw_ref[...], dy_ref[...], sc_ref[...]
        dpool = dyv * _silu(gate)
        dg_ref[...] = (dyv * (rawv * scale) * _silu_grad(gate)).astype(_BF16)
        ds_ref[0:1, :] += jnp.sum(dpool * rawv, axis=0, keepdims=True)
        draw = (dpool * scale).astype(_BF16)
        dr_ref[...] = draw
        dd_ref[...] = lax.dot_general(
            draw, w_ref[...].reshape(pg, pg), (((1,), (1,)), ((), ())), preferred_element_type=_F32)

    blk = pl.BlockSpec((tp, pg), lambda i: (i, 0))
    gate = pl.BlockSpec((tp, pg), lambda i: (i + l // tp, gate_col))
    return pl.pallas_call(
        body, name=name, grid=(n // tp,),
        in_specs=[pl.BlockSpec((tp, pg), lambda i: (i, aw // pg + gi)), gate, blk,
                  pl.BlockSpec((N_DEV, 1, pg // N_DEV, pg), lambda i: (0, gi, 0, 0)),
                  pl.BlockSpec((1, pg), lambda i: (0, gi)), _ANY],
        out_specs=[gate, blk, blk, pl.BlockSpec((SUBLANES, pg), lambda i: (0, 0))],
        out_shape=[jax.ShapeDtypeStruct(dproj.shape, dproj.dtype), jax.ShapeDtypeStruct((n, pg), _BF16),
                   jax.ShapeDtypeStruct((n, pg), _F32), jax.ShapeDtypeStruct((SUBLANES, pg), _F32)],
        input_output_aliases={5: 0},
        compiler_params=pltpu.CompilerParams(dimension_semantics=("arbitrary",)),
    )(dy, proj, raw, pool_w, pool_scale, dproj)


def _pool_backward_window(gi, dd, dproj, l, col, tp, name):
    n, pg = dd.shape
    w = POOL_WINDOWS[gi]

    def body(c_ref, p_ref, n_ref, dproj_ref, du_ref):
        del dproj_ref
        t0 = pl.program_id(0) * tp
        cur = c_ref[...]
        e = _with_halo(cur, p_ref[...], n_ref[...], t0, n)
        tok = t0 - POOL_HALO + lax.broadcasted_iota(jnp.int32, (tp + 2 * POOL_HALO, 1), 0)
        e = e / jnp.maximum(_window_count(tok, w, n), 1.0)
        du_ref[...] = (_window_sum(e, w, True)[POOL_HALO:POOL_HALO + tp] - cur).astype(_BF16)

    return pl.pallas_call(
        body, name=name, grid=(n // tp,),
        in_specs=_halo_specs(tp, pg, 0, 0, n) + [_ANY],
        out_specs=pl.BlockSpec((tp, pg), lambda i: (i + l // tp, col)),
        out_shape=jax.ShapeDtypeStruct(dproj.shape, dproj.dtype), input_output_aliases={3: 0},
        compiler_params=pltpu.CompilerParams(dimension_semantics=("arbitrary",)),
    )(dd, dd, dd, dproj)


def _post(out, x, target, gate, g_post, tr, name):
    n, d = out.shape

    def body(o_ref, x_ref, t_ref, gate_ref, g_ref, dxn_ref, do_ref, dgate_ref, dg_ref, loss_ref):
        @pl.when(pl.program_id(0) == 0)
        def _():
            dgate_ref[...] = jnp.zeros_like(dgate_ref)
            dg_ref[...] = jnp.zeros_like(dg_ref)
            loss_ref[...] = jnp.zeros_like(loss_ref)

        ov = o_ref[...]
        r = _rms(ov)
        on = ov * r
        normed = on * g_ref[...]
        err = (x_ref[...] + gate_ref[...] * normed) - t_ref[...]
        loss_ref[...] += jnp.sum(err * err)
        dxn = err / d
        dxn_ref[...] = dxn
        dgate_ref[0:1, :] += jnp.sum(dxn * normed, axis=0, keepdims=True)
        dr = dxn * gate_ref[...]
        dg_ref[0:1, :] += jnp.sum(dr * on, axis=0, keepdims=True)
        dn = dr * g_ref[...]
        do_ref[...] = (r * (dn - on * jnp.mean(dn * on, axis=-1, keepdims=True))).astype(_BF16)

    blk = pl.BlockSpec((tr, d), lambda i: (i, 0))
    vec = pl.BlockSpec((1, d), lambda i: (0, 0))
    acc = pl.BlockSpec((SUBLANES, d), lambda i: (0, 0))
    return pl.pallas_call(
        body, name=name, grid=(n // tr,),
        in_specs=[blk, blk, blk, vec, vec],
        out_specs=[blk, blk, acc, acc, pl.BlockSpec((SUBLANES, LANES), lambda i: (0, 0))],
        out_shape=[jax.ShapeDtypeStruct((n, d), _F32), jax.ShapeDtypeStruct((n, d), _BF16),
                   jax.ShapeDtypeStruct((SUBLANES, d), _F32), jax.ShapeDtypeStruct((SUBLANES, d), _F32),
                   jax.ShapeDtypeStruct((SUBLANES, LANES), _F32)],
        compiler_params=pltpu.CompilerParams(dimension_semantics=("arbitrary",)),
    )(out, x, target, gate, g_post)


def _adam_sharded(slab_ids, grad, got, far, w, m, v, name):
    r, c = w.shape
    tr = _tile(r, max(BF16_ROWS, min(256, (1 << 18) // c)), BF16_ROWS)

    def body(ids_ref, own_ref, got_ref, far_ref, w_ref, m_ref, v_ref, g_ref, dl_ref, nm_ref, nv_ref):
        del ids_ref
        g = own_ref[0] + got_ref[0]
        for k in range(N_CHIPS - 1):
            g = g + far_ref[k].astype(_F32)
        delta, nm, nv = _adamw(w_ref[...], g, m_ref[...], v_ref[...])
        g_ref[...] = g
        dl_ref[...] = delta
        nm_ref[...] = nm
        nv_ref[...] = nv

    blk = pl.BlockSpec((tr, c), lambda i, ids: (i, 0))
    return pl.pallas_call(
        body, name=name,
        grid_spec=pltpu.PrefetchScalarGridSpec(
            num_scalar_prefetch=1, grid=(r // tr,),
            in_specs=[pl.BlockSpec((1, tr, c), lambda i, ids: (ids[0], i, 0)),
                      pl.BlockSpec((1, tr, c), lambda i, ids: (0, i, 0)),
                      pl.BlockSpec((N_CHIPS - 1, tr, c), lambda i, ids: (0, i, 0)), blk, blk, blk],
            out_specs=[blk] * 4),
        out_shape=[jax.ShapeDtypeStruct((r, c), _F32)] * 4,
        compiler_params=pltpu.CompilerParams(dimension_semantics=("parallel",)),
    )(slab_ids, grad, got, far, w, m, v)


def _adam_replicated(parts, extra, through_silu, w, m, v, name):
    def body(p_ref, e_ref, s_ref, w_ref, m_ref, v_ref, g_ref, dl_ref, nm_ref, nv_ref):
        total = p_ref[0] + e_ref[0]
        for dev in range(1, N_DEV):
            total = total + (p_ref[dev] + e_ref[dev])
        g = jnp.where(s_ref[...] > 0.5, total * _silu_grad(w_ref[...]), total)
        delta, nm, nv = _adamw(w_ref[...], g, m_ref[...], v_ref[...])
        g_ref[...] = g
        dl_ref[...] = delta
        nm_ref[...] = nm
        nv_ref[...] = nv

    return pl.pallas_call(
        body, name=name, in_specs=[_VMEM] * 6, out_specs=[_VMEM] * 4,
        out_shape=[jax.ShapeDtypeStruct(w.shape, _F32)] * 4,
    )(parts, extra, through_silu, w, m, v)


def _as_rows(vec):
    size = vec.shape[0]
    padded = -(-size // (SUBLANES * LANES)) * SUBLANES * LANES
    return jnp.pad(vec, (0, padded - size)).reshape(padded // LANES, LANES)


def kernel(x, c, ctx, c_ctx, w_ada, b_ada, norm_pre, norm_post, w_in, q_norm, k_norm, pool_w, pool_scale, w_out, loss_target, m_c_ctx, m_w_ada, m_b_ada, m_norm_pre, m_norm_post, m_w_in, m_q_norm, m_k_norm, m_pool_w, m_pool_scale, m_w_out, v_c_ctx, v_w_ada, v_b_ada, v_norm_pre, v_norm_post, v_w_in, v_q_norm, v_k_norm, v_pool_w, v_pool_scale, v_w_out):
    me = _dev_index(*_position())
    x2, ctx2, target = x[0], ctx[0], loss_target[0]
    n, d = x2.shape
    l = ctx2.shape[0]
    t = l + n
    aw = d // 2
    heads = aw // HEAD_DIM
    kv_heads = heads // GQA_GROUP
    kw = kv_heads * HEAD_DIM
    n_groups = len(POOL_WINDOWS)
    pg = (d - aw) // n_groups
    mix = d
    tr = _tile(l, 128, BF16_ROWS)
    tq = _tile(l, 128, BF16_ROWS)
    tp = _tile(l, 512, POOL_HALO)

    xi, yi, ci = _position()
    slab_ids = jnp.stack([_dev_index(*chip, ci) for chip in _chip_order(xi, yi)]).astype(jnp.int32)

    cw = w_in.shape[-1]
    wg = _cast_into_columns(slab_ids, w_in[0], N_DEV, "cast_w_in")
    late = [_cast_into_slab(slab_ids, w_out[0], "cast_w_out"),
            _cast_into_slab(slab_ids, pool_w[0].reshape(-1, pg), "cast_pool_w")]

    c_all = _all_gather_small(_as_rows(c[0]), "gather_c").reshape(N_DEV, -1)[:, :d]
    craw = jnp.concatenate([c_all, jnp.broadcast_to(c_ctx[None], (N_DEV, d))], axis=0)
    ada = _ada_forward(craw, w_ada[0], "ada_forward")
    ada_all = _all_gather_small(ada, "gather_ada")
    mod_all = ada_all.transpose(1, 0, 2).reshape(ada.shape[0], -1) + b_ada[0]
    mod = lax.dynamic_index_in_dim(mod_all, me, 0, keepdims=False)
    mod_c = mod_all[N_DEV]
    shift, scale, gate = mod[:d], mod[d:2 * d], mod[2 * d:]
    zeros6 = jnp.zeros((SUBLANES - 2, d), _F32)
    mods = jnp.stack([jnp.concatenate([mod_c[None, d:2 * d], mod_c[None, :d], zeros6], axis=0),
                      jnp.concatenate([scale[None], shift[None], zeros6], axis=0)])

    a_s, a_r, wg, tok = _w_in_hop(wg, cw, [], "a", ada_all, "gather_w_in_a")
    h_all = _prenorm(ctx2, x2, norm_pre, mods + tok[0, 0], tr, "prenorm")
    order_ids = jnp.stack([_dev_index(*dev) for dev in _w_in_order(xi, yi, ci)]).astype(jnp.int32)
    proj = lax.empty((t, N_DEV * cw), _F32)
    proj = _proj_blocks(h_all, wg, proj, order_ids, 0, 1, cw, tok, "proj_0")
    b_s, b_r, wg, tok = _w_in_hop(wg, cw, [("a", a_s, a_r, [0, 1], [])], "b", proj, "gather_w_in_b")
    proj = _proj_blocks(h_all, wg, proj, order_ids, 1, 2, cw, tok, "proj_1")
    c_s, c_r, wg, tok = _w_in_hop(wg, cw, [("b", b_s, b_r, [2, 0], [])], "c", proj, "gather_w_in_c")
    proj = _proj_blocks(h_all, wg, proj, order_ids, 3, 2, cw, tok, "proj_2")
    d_s, d_r, wg, tok = _w_in_hop(
        wg, cw, [("c", c_s, c_r, [0], []), ("b", b_s, b_r, [1], [])], "d", proj, "gather_w_in_d")
    proj = _proj_blocks(h_all, wg, proj, order_ids, 5, 2, cw, tok, "proj_3")
    w_in_g, tok = _w_in_hop(
        wg, cw, [("d", d_s, d_r, [0], [0]), ("a", a_s, a_r, [], [0, 1]), ("b", b_s, b_r, [], [0, 1, 2]),
                 ("c", c_s, c_r, [], [0])], None, proj, "gather_w_in_end")
    flight_w = _gather_slabs_start(late, w_in_g, "gather_late_start")
    proj = _proj_blocks(h_all, w_in_g, proj, order_ids, 7, 1, cw, flight_w[-1], "proj_4")
    tables = _rope_tables(l, n)
    q, k, v = _qkv_post(proj, tables, q_norm, k_norm, heads, kv_heads, tr, "qkv_post")
    attn_o, y, lse = _attention(q, k, v, proj, l, mix, tq, "attention")
    w_out_g8, pool_g8 = _gather_slabs_wait(*flight_w[:3], attn_o, "gather_late_wait")
    w_out_g = w_out_g8.reshape(mix, d)
    pool_g = pool_g8.reshape(N_DEV, n_groups, pg // N_DEV, pg)
    raws, ds = [], []
    for gi in range(n_groups):
        y, raw, dsave = _pool_forward(gi, proj, y, pool_g, pool_scale, l, heads, kv_heads, tp, f"pool_forward_{gi}")
        raws.append(raw)
        ds.append(dsave)
    out = _matmul(y, w_out_g, name="out_proj")
    dxn, dout, dgate8, dgpost8, loss8 = _post(out, x2, target, gate[None], norm_post, tr, "post")

    gw_out = _matmul(y, dout, ta=True, name="grad_w_out").reshape(N_DEV, mix // N_DEV, d)
    flight_so = _exchange_start(_sibling_copies_by_device, [gw_out], "exchange_sibling_start_w_out", land_slabs=N_CHIPS)
    dy = _matmul(dout, w_out_g, tb=True, after=flight_so[-1], name="d_y")
    gw_out, got_out = _exchange_wait(
        _sibling_copies_by_device, *flight_so[:4], dy, "exchange_sibling_wait_w_out", with_sources=True)
    sum_out = _pre_add(slab_ids, gw_out, got_out, "pre_add_w_out")
    flight_out = _exchange_start(_chip_copies, [sum_out], "exchange_chips_start_w_out")
    w3 = aw + 2 * kw
    dq, dproj, dk, dv = _attention_backward(
        q, k, v, attn_o, dy, proj, lse, flight_out[-1], lax.empty(proj.shape, _BF16), l, tq, "attention_backward")
    dproj, dgq8, dgk8 = _qkv_post_backward(proj, dq, dk, dv, tables, q_norm, k_norm, dproj, l, tr, "qkv_post_backward")
    dproj = _zero_context_rows(dproj, l, w3, tr, "zero_context_rows")
    gpw, dps8 = [], []
    for gi in range(n_groups):
        dproj, draw, dd, dps = _pool_backward_gate(
            gi, dy, proj, raws[gi], pool_g, pool_scale, dproj, l, heads, kv_heads, tp, f"pool_backward_gate_{gi}")
        dproj = _pool_backward_window(gi, dd, dproj, l, (w3 + aw) // pg + gi, tp, f"pool_backward_window_{gi}")
        dps8.append(dps)
        gpw.append(_matmul(ds[gi], draw, ta=True, name=f"grad_pool_w_{gi}"))
    cw = w_in.shape[-1]
    other_ids = jnp.stack([_dev_index(*chip, 1 - ci) for chip in _chip_order(xi, yi)]).astype(jnp.int32)
    chip_slabs = jnp.arange(N_CHIPS, dtype=jnp.int32)
    pr = pool_w.shape[2]
    gpw8 = jnp.stack(gpw).reshape(n_groups, N_DEV, pr, pg).transpose(1, 0, 2, 3).reshape(N_DEV, n_groups * pr, pg)
    give_in = _matmul_slabs(h_all, dproj, other_ids, cw, "grad_w_in_sibling")
    flight_sib = _exchange_start(_sibling_copies, [give_in, jnp.take(gpw8, other_ids, axis=0)], "exchange_sibling_start")
    gw_in = _matmul_slabs(h_all, dproj, slab_ids, cw, "grad_w_in_own", after=flight_sib[-1])
    gpw_own = jnp.take(gpw8, slab_ids, axis=0)
    got_in, got_pw = _exchange_wait(_sibling_copies, *flight_sib[:4], gw_in, "exchange_sibling_wait")
    sums_in = [_pre_add(chip_slabs, gw_in, got_in, "pre_add_w_in"), _pre_add(chip_slabs, gpw_own, got_pw, "pre_add_pool_w")]
    flight_in = _exchange_start(_chip_copies, sums_in, "exchange_chips_start_w_in")
    dh = _matmul(dproj, w_in_g, tb=True, tm=1088, tk=3072, after=flight_in[-1], name="d_h")
    grad_x, dmods, dgpre8 = _prenorm_backward(dh, ctx2, x2, dxn, norm_pre, mods, tr, "prenorm_backward")

    dmod_lat = jnp.concatenate([dmods[1, 1], dmods[1, 0], dgate8[0]])
    dmod_ctx = jnp.concatenate([dmods[0, 1], dmods[0, 0], jnp.zeros((d,), _F32)])
    small = jnp.concatenate([dmod_lat, dmod_ctx, dgpre8[0], dgpost8[0], dgq8[0], dgk8[0]] + [p[0] for p in dps8]
                            + [loss8[0, :1]])
    gathered = _all_gather_small(_as_rows(small), "gather_small").reshape(N_DEV, -1)
    o = 0
    take = lambda size: (gathered[:, o:o + size], o + size)
    g_mod, o = take(3 * d)
    g_modc, o = take(3 * d)
    g_pre, o = take(d)
    g_post, o = take(d)
    g_q, o = take(HEAD_DIM)
    g_k, o = take(HEAD_DIM)
    g_ps, o = take(n_groups * pg)
    g_loss, o = take(1)
    cols = w_ada.shape[-1]
    mine = lambda a: lax.dynamic_slice_in_dim(a, me * cols, cols, axis=1)
    dmod_rows = jnp.concatenate([mine(g_mod), mine(g_modc)], axis=0)
    g_wada, dl_wada, nm_wada, nv_wada, dcact = _ada_backward(craw, dmod_rows, w_ada[0], m_w_ada[0], v_w_ada[0], "ada_backward")
    dcc = _all_gather_small(_as_rows(dcact[0]), "gather_dcc").reshape(N_DEV, -1)[:, :d]

    sizes = [d, 3 * d, d, d, HEAD_DIM, HEAD_DIM, n_groups * pg]
    def pack(parts):
        rows = jnp.concatenate(parts, axis=1)
        padded = -(-rows.shape[1] // (SUBLANES * LANES)) * SUBLANES * LANES
        return jnp.pad(rows, ((0, 0), (0, padded - rows.shape[1]))).reshape(N_DEV, padded // LANES, LANES)

    zero = lambda size: jnp.zeros((N_DEV, size), _F32)
    parts = pack([dcc, g_mod, g_pre, g_post, g_q, g_k, g_ps])
    extra = pack([zero(d), g_modc, zero(d), zero(d), zero(HEAD_DIM), zero(HEAD_DIM), zero(n_groups * pg)])
    through_silu = _as_rows(jnp.concatenate([jnp.ones((d,), _F32), jnp.zeros((sum(sizes[1:]),), _F32)]))
    cat = lambda items: _as_rows(jnp.concatenate([a.reshape(-1) for a in items]))
    ws = [c_ctx, b_ada, norm_pre, norm_post, q_norm, k_norm, pool_scale]
    ms = [m_c_ctx, m_b_ada, m_norm_pre, m_norm_post, m_q_norm, m_k_norm, m_pool_scale]
    vs = [v_c_ctx, v_b_ada, v_norm_pre, v_norm_post, v_q_norm, v_k_norm, v_pool_scale]
    rep = _adam_replicated(parts, extra, through_silu, cat(ws), cat(ms), cat(vs), "adam_replicated")

    def split(packed):
        flat_, outs, at = packed.reshape(-1), [], 0
        for w, size in zip(ws, sizes):
            outs.append(flat_[at:at + size].reshape(w.shape))
            at += size
        return outs

    g_rep, dl_rep, nm_rep, nv_rep = [split(r) for r in rep]

    far_out = _exchange_wait(_chip_copies, *flight_out[:4], grad_x, "exchange_chips_wait_w_out")[0]
    far_in, far_pw = _exchange_wait(_chip_copies, *flight_in[:4], rep[0], "exchange_chips_wait_w_in")
    two = lambda a: a.reshape(-1, a.shape[-1])
    sharded = []
    for ids, g, got, far, w, m, v_, name in zip(
            (chip_slabs, slab_ids, chip_slabs), (gw_in, gw_out, gpw_own), (got_in, got_out, got_pw),
            (far_in, far_out, far_pw), (w_in, w_out, pool_w), (m_w_in, m_w_out, m_pool_w),
            (v_w_in, v_w_out, v_pool_w), ("adam_w_in", "adam_w_out", "adam_pool_w")):
        res = _adam_sharded(ids, g, got, far, two(w), two(m), two(v_), name)
        sharded.append([r.reshape(w.shape) for r in res])
    (g_w_in, dl_w_in, nm_w_in, nv_w_in), (g_w_out, dl_w_out, nm_w_out, nv_w_out), (g_pw, dl_pw, nm_pw, nv_pw) = sharded

    loss_sum = g_loss[0, 0]
    for dev in range(1, N_DEV):
        loss_sum = loss_sum + g_loss[dev, 0]
    loss = (0.5 / d) * loss_sum

    def ordered(rep_list, ada_, w_in_, pw_, w_out_):
        return [rep_list[0], ada_[None], rep_list[1], rep_list[2], rep_list[3], w_in_, rep_list[4], rep_list[5],
                pw_, rep_list[6], w_out_]

    return (loss, grad_x[None],
            *ordered(g_rep, g_wada, g_w_in, g_pw, g_w_out),
            *ordered(dl_rep, dl_wada, dl_w_in, dl_pw, dl_w_out),
            *ordered(nm_rep, nm_wada, nm_w_in, nm_pw, nm_w_out),
            *ordered(nv_rep, nv_wada, nv_w_in, nv_pw, nv_w_out))
```

```python
import functools

import jax
import jax.numpy as jnp
from jax import lax
from jax.experimental import pallas as pl
from jax.experimental.pallas import tpu as pltpu

HEAD_DIM = 128
GQA_GROUP = 4
ATTN_SUB_HEADS = 1
ATTN_BWD_SUB_HEADS = 1
ATTN_KEY_PARTS = 2
LOG2_E = 1.4426950408889634
GRID_W = 64
ROPE_PAIRS = HEAD_DIM // 4
ROPE_THETA = 10000.0
ATTN_SCALE = HEAD_DIM ** -0.5
EPS = 1e-6
POOL_WINDOWS = (2, 4, 8, 16)
POOL_HALO = 8
N_DEV = 8
N_CHIPS = 4
ADAM_LR = 0.001
ADAM_B1 = 0.9
ADAM_B2 = 0.999
ADAM_EPS = 1e-08
ADAM_WD = 0.01
ADAM_STEP = 10

LANES = 128
SUBLANES = 8
BF16_ROWS = 16

_MESH = pl.DeviceIdType.MESH
_ANY = pl.BlockSpec(memory_space=pl.ANY)
_VMEM = pl.BlockSpec(memory_space=pltpu.VMEM)
_HBM = pl.BlockSpec(memory_space=pltpu.HBM)
_SEM = pl.BlockSpec(memory_space=pltpu.SEMAPHORE)
_EFFECT = pltpu.SideEffectType.DATAFLOW_SIDE_EFFECTING
_F32 = jnp.float32
_BF16 = jnp.bfloat16


def _tile(dim, pref, align):
    t = min(pref, dim)
    t -= t % align
    while t >= align:
        if dim % t == 0:
            return t
        t -= align
    return dim


def _position():
    return lax.axis_index("x"), lax.axis_index("y"), lax.axis_index("c")


def _flip(v, bit):
    return 1 - v if bit else v


def _dev_index(x, y, c):
    return 4 * x + 2 * y + c


def _silu(g):
    return g * jax.nn.sigmoid(g)


def _silu_grad(g):
    s = jax.nn.sigmoid(g)
    return s * (1.0 + g * (1.0 - s))


def _adamw(w, g, m, v):
    m = ADAM_B1 * m + (1.0 - ADAM_B1) * g
    v = ADAM_B2 * v + (1.0 - ADAM_B2) * (g * g)
    m_hat = m / (1.0 - ADAM_B1 ** ADAM_STEP)
    v_hat = v / (1.0 - ADAM_B2 ** ADAM_STEP)
    delta = -ADAM_LR * (m_hat / (jnp.sqrt(v_hat) + ADAM_EPS) + ADAM_WD * w)
    return delta, m, v


def _all_gather_small(v, name):
    rows, cols = v.shape

    def body(v_ref, out_ref, send_sems, recv_sems):
        x, y, c = _position()
        me = _dev_index(x, y, c)
        out_ref[me] = v_ref[...]
        peers = [(_flip(x, k & 4), _flip(y, k & 2), _flip(c, k & 1)) for k in range(1, N_DEV)]

        def copy(k, block, to):
            return pltpu.make_async_remote_copy(
                src_ref=v_ref, dst_ref=out_ref.at[block], send_sem=send_sems.at[k], recv_sem=recv_sems.at[k],
                device_id=to, device_id_type=_MESH)

        sends = [copy(k, me, p) for k, p in enumerate(peers)]
        for s in sends:
            s.start()
        for k, p in enumerate(peers):
            copy(k, _dev_index(*p), p).wait_recv()
        for s in sends:
            s.wait_send()

    return pl.pallas_call(
        body, name=name,
        out_shape=jax.ShapeDtypeStruct((N_DEV, rows, cols), v.dtype),
        in_specs=[_VMEM], out_specs=_VMEM,
        scratch_shapes=[pltpu.SemaphoreType.DMA((N_DEV - 1,)), pltpu.SemaphoreType.DMA((N_DEV - 1,))],
    )(v)


def _route(x, y, c):
    first = (x + (1 - c) * (1 - 2 * x), y + c * (1 - 2 * y))
    second = (x + c * (1 - 2 * x), y + (1 - c) * (1 - 2 * y))
    return first, second, (1 - x, 1 - y)


def _w_in_order(x, y, c):
    first, second, diagonal = _route(x, y, c)
    return [(x, y, c), (x, y, 1 - c), (*first, c), (*second, 1 - c), (*second, c), (*first, 1 - c),
            (*diagonal, c), (*diagonal, 1 - c)]


W_IN_HOPS = {"a": 2, "b": 3, "c": 1, "d": 1}


def _w_in_hop_copies(group, wg, width, send_sems, recv_sems):
    x, y, c = _position()
    me, sibling = (x, y, c), (x, y, 1 - c)
    first, second, diagonal = _route(x, y, c)

    def cp(k, block, to):
        cols = wg.at[:, pl.ds(pl.multiple_of(_dev_index(*block) * width, width), width)]
        return pltpu.make_async_remote_copy(
            src_ref=cols, dst_ref=cols, send_sem=send_sems.at[k], recv_sem=recv_sems.at[k],
            device_id=to, device_id_type=_MESH)

    if group == "a":
        return [cp(0, me, sibling), cp(1, me, (*first, c))]
    if group == "b":
        return [cp(0, me, (*second, c)), cp(1, (*first, c), (*second, c)), cp(2, (*first, c), sibling)]
    if group == "c":
        return [cp(0, (*second, c), sibling)]
    return [cp(0, (*diagonal, c), sibling)]


def _w_in_hop(wg, width, waits, start, after, name):
    n_sem = 2 * len(waits)

    def body(*refs):
        wg_ref = refs[0]
        for i, (group, _, _, arrivals, sends) in enumerate(waits):
            cps = _w_in_hop_copies(group, wg_ref, width, refs[1 + 2 * i], refs[2 + 2 * i])
            for k in arrivals:
                cps[k].wait_recv()
            for k in sends:
                cps[k].wait_send()
        if start:
            for cp in _w_in_hop_copies(start, wg_ref, width, refs[n_sem + 2], refs[n_sem + 3]):
                cp.start()
        refs[-1][...] = jnp.zeros_like(refs[-1])

    sems = [s for w in waits for s in w[1:3]]
    new = [pltpu.SemaphoreType.DMA((W_IN_HOPS[start],))] * 2 if start else []
    outs = pl.pallas_call(
        body, name=name,
        out_shape=(*new, pltpu.HBM(wg.shape, wg.dtype), jax.ShapeDtypeStruct((SUBLANES, LANES), _F32)),
        in_specs=[_HBM] + [_SEM] * n_sem + [_ANY], out_specs=(*[_SEM] * len(new), _HBM, _VMEM),
        input_output_aliases={0: len(new)},
        compiler_params=pltpu.CompilerParams(has_side_effects=_EFFECT),
    )(pltpu.with_memory_space_constraint(wg, pltpu.HBM), *sems, after)
    return outs


def _slab_copies(bufs, send_sems, recv_sems):
    x, y, c = _position()
    me = _dev_index(x, y, c)
    peers = [(_flip(x, k & 4), _flip(y, k & 2), _flip(c, k & 1)) for k in range(1, N_DEV)]
    return [pltpu.make_async_remote_copy(
        src_ref=buf.at[me], dst_ref=buf.at[me],
        send_sem=send_sems.at[(N_DEV - 1) * a + k], recv_sem=recv_sems.at[(N_DEV - 1) * a + k],
        device_id=peer, device_id_type=_MESH)
        for a, buf in enumerate(bufs) for k, peer in enumerate(peers)]


def _gather_slabs_start(bufs, after, name):
    n = len(bufs)
    n_copies = (N_DEV - 1) * n

    def body(*refs):
        send_sems, recv_sems, token = refs[n + 1], refs[n + 2], refs[-1]
        for cp in _slab_copies(refs[:n], send_sems, recv_sems):
            cp.start()
        token[...] = jnp.zeros_like(token)

    outs = pl.pallas_call(
        body, name=name,
        out_shape=(pltpu.SemaphoreType.DMA((n_copies,)), pltpu.SemaphoreType.DMA((n_copies,)),
                   *[pltpu.HBM(b.shape, b.dtype) for b in bufs], jax.ShapeDtypeStruct((SUBLANES, LANES), _F32)),
        in_specs=[_HBM] * n + [_ANY], out_specs=(_SEM, _SEM, *[_HBM] * n, _VMEM),
        input_output_aliases={i: 2 + i for i in range(n)},
        compiler_params=pltpu.CompilerParams(has_side_effects=_EFFECT),
    )(*[pltpu.with_memory_space_constraint(b, pltpu.HBM) for b in bufs], after)
    return outs[0], outs[1], list(outs[2:2 + n]), outs[-1]


def _gather_slabs_wait(send_sems, recv_sems, bufs, after, name):
    n = len(bufs)

    def body(*refs):
        for cp in _slab_copies(refs[:n], refs[n], refs[n + 1]):
            cp.wait_send()
            cp.wait_recv()

    outs = pl.pallas_call(
        body, name=name, out_shape=tuple(pltpu.HBM(b.shape, b.dtype) for b in bufs),
        in_specs=[_HBM] * n + [_SEM, _SEM, _ANY], out_specs=[_HBM] * n,
        input_output_aliases={i: i for i in range(n)},
        compiler_params=pltpu.CompilerParams(has_side_effects=_EFFECT),
    )(*bufs, send_sems, recv_sems, after)
    return list(outs)


def _chip_order(x, y):
    return [(x, y), (1 - x, y), (x, 1 - y), (1 - x, 1 - y)]


def _chip_copies(srcs, lands, send_sems, recv_sems):
    x, y, c = _position()
    return [pltpu.make_async_remote_copy(
        src_ref=srcs[a].at[k], dst_ref=lands[a].at[k],
        send_sem=send_sems.at[(N_CHIPS - 1) * a + k], recv_sem=recv_sems.at[(N_CHIPS - 1) * a + k],
        device_id=(*chip, c), device_id_type=_MESH)
        for a in range(len(srcs)) for k, chip in enumerate(_chip_order(x, y)[1:])]


def _sibling_copies(srcs, lands, send_sems, recv_sems):
    x, y, c = _position()
    return [pltpu.make_async_remote_copy(
        src_ref=srcs[a].at[s], dst_ref=lands[a].at[s],
        send_sem=send_sems.at[N_CHIPS * a + s], recv_sem=recv_sems.at[N_CHIPS * a + s],
        device_id=(x, y, 1 - c), device_id_type=_MESH)
        for a in range(len(srcs)) for s in range(N_CHIPS)]


def _sibling_copies_by_device(srcs, lands, send_sems, recv_sems):
    x, y, c = _position()
    return [pltpu.make_async_remote_copy(
        src_ref=srcs[a].at[_dev_index(*chip, 1 - c)], dst_ref=lands[a].at[s],
        send_sem=send_sems.at[N_CHIPS * a + s], recv_sem=recv_sems.at[N_CHIPS * a + s],
        device_id=(x, y, 1 - c), device_id_type=_MESH)
        for a in range(len(srcs)) for s, chip in enumerate(_chip_order(x, y))]


def _exchange_start(copies, sums, name, land_slabs=None):
    n = len(sums)
    land_shapes = [((land_slabs or s.shape[0]),) + s.shape[1:] for s in sums]
    n_copies = sum(shape[0] for shape in land_shapes)

    def body(*refs):
        srcs, lands = refs[:n], refs[n:2 * n]
        send_sems, recv_sems, token = refs[2 * n], refs[2 * n + 1], refs[-1]
        for cp in copies(srcs, lands, send_sems, recv_sems):
            cp.start()
        token[...] = jnp.zeros_like(token)

    hbm = [pltpu.HBM(s.shape, s.dtype) for s in sums] + [pltpu.HBM(shape, s.dtype) for shape, s in zip(land_shapes, sums)]
    outs = pl.pallas_call(
        body, name=name,
        out_shape=(pltpu.SemaphoreType.DMA((n_copies,)), pltpu.SemaphoreType.DMA((n_copies,)), *hbm,
                   jax.ShapeDtypeStruct((SUBLANES, LANES), _F32)),
        in_specs=[_HBM] * (2 * n), out_specs=(_SEM, _SEM, *[_HBM] * (2 * n), _VMEM),
        input_output_aliases={i: 2 + i for i in range(2 * n)},
        compiler_params=pltpu.CompilerParams(has_side_effects=_EFFECT),
    )(*[pltpu.with_memory_space_constraint(s, pltpu.HBM) for s in sums],
      *[pltpu.with_memory_space_constraint(lax.empty(shape, s.dtype), pltpu.HBM) for shape, s in zip(land_shapes, sums)])
    return outs[0], outs[1], list(outs[2:2 + n]), list(outs[2 + n:2 + 2 * n]), outs[-1]


def _exchange_wait(copies, send_sems, recv_sems, srcs, lands, after, name, with_sources=False):
    n = len(srcs)

    def body(*refs):
        for cp in copies(refs[:n], refs[n:2 * n], refs[2 * n], refs[2 * n + 1]):
            cp.wait_send()
            cp.wait_recv()

    hbm = [pltpu.HBM(s.shape, s.dtype) for s in (*srcs, *lands)]
    outs = pl.pallas_call(
        body, name=name, out_shape=tuple(hbm),
        in_specs=[_HBM] * (2 * n) + [_SEM, _SEM, _ANY], out_specs=[_HBM] * (2 * n),
        input_output_aliases={i: i for i in range(2 * n)},
        compiler_params=pltpu.CompilerParams(has_side_effects=_EFFECT),
    )(*srcs, *lands, send_sems, recv_sems, after)
    return list(outs) if with_sources else list(outs[n:])


def _matmul(a, b, *, ta=False, tb=False, out_dtype=_F32, tm=1024, tn=1024, tk=4608, col_slabs=None, after=None, name):
    kdim, m = a.shape if ta else a.shape[::-1]
    n = b.shape[0] if tb else b.shape[1]
    tm = _tile(m, tm, LANES if ta else BF16_ROWS)
    tn = n // col_slabs if col_slabs else _tile(n, tn, LANES)
    tk = _tile(kdim, tk, BF16_ROWS if ta else LANES)
    nk = kdim // tk
    dims = (((0 if ta else 1,), (1 if tb else 0,)), ((), ()))

    def body_whole_k(a_ref, b_ref, *rest):
        o_ref = rest[-1]
        part = lax.dot_general(a_ref[...], b_ref[...], dims, preferred_element_type=_F32)
        o_ref[...] = part.astype(out_dtype).reshape(o_ref.shape)

    def body_split_k(a_ref, b_ref, *rest):
        o_ref, acc_ref = rest[-2:]
        k = pl.program_id(2)

        @pl.when(k == 0)
        def _():
            acc_ref[...] = jnp.zeros_like(acc_ref)

        acc_ref[...] += lax.dot_general(a_ref[...], b_ref[...], dims, preferred_element_type=_F32)

        @pl.when(k == nk - 1)
        def _():
            o_ref[...] = acc_ref[...].astype(out_dtype).reshape(o_ref.shape)

    a_spec = pl.BlockSpec((tk, tm), lambda i, j, k: (k, i)) if ta else pl.BlockSpec((tm, tk), lambda i, j, k: (i, k))
    b_spec = pl.BlockSpec((tn, tk), lambda i, j, k: (j, k)) if tb else pl.BlockSpec((tk, tn), lambda i, j, k: (k, j))
    if col_slabs:
        out_spec = pl.BlockSpec((1, tm, tn), lambda i, j, k: (j, i, 0))
        out_shape = jax.ShapeDtypeStruct((col_slabs, m, tn), out_dtype)
    else:
        out_spec = pl.BlockSpec((tm, tn), lambda i, j, k: (i, j))
        out_shape = jax.ShapeDtypeStruct((m, n), out_dtype)
    extra = [] if after is None else [after]
    return pl.pallas_call(
        body_whole_k if nk == 1 else body_split_k, name=name, grid=(m // tm, n // tn, nk),
        in_specs=[a_spec, b_spec] + [pl.BlockSpec(t.shape, lambda i, j, k: (0, 0)) for t in extra],
        out_specs=out_spec, out_shape=out_shape,
        scratch_shapes=[] if nk == 1 else [pltpu.VMEM((tm, tn), _F32)],
        compiler_params=pltpu.CompilerParams(dimension_semantics=("parallel", "parallel", "arbitrary")),
    )(a, b, *extra)


def _proj_blocks(a, wg, dst, order_ids, first, count, width, after, name):
    m, kdim = a.shape
    tm = _tile(m, 1088, BF16_ROWS)

    def body(ids_ref, a_ref, w_ref, after_ref, dst_ref, o_ref):
        del ids_ref, after_ref, dst_ref
        o_ref[...] = jnp.dot(a_ref[...], w_ref[...], preferred_element_type=_F32)

    return pl.pallas_call(
        body, name=name,
        grid_spec=pltpu.PrefetchScalarGridSpec(
            num_scalar_prefetch=1, grid=(count, m // tm),
            in_specs=[pl.BlockSpec((tm, kdim), lambda j, i, ids: (i, 0)),
                      pl.BlockSpec((kdim, width), lambda j, i, ids: (0, ids[first + j])),
                      pl.BlockSpec(after.shape, lambda j, i, ids: (0, 0)), _ANY],
            out_specs=pl.BlockSpec((tm, width), lambda j, i, ids: (i, ids[first + j]))),
        out_shape=jax.ShapeDtypeStruct(dst.shape, dst.dtype),
        input_output_aliases={4: 0},
        compiler_params=pltpu.CompilerParams(dimension_semantics=("arbitrary", "arbitrary")),
    )(order_ids, a, wg, after, dst)


def _cast_into_columns(slab_ids, a, n_blocks, name):
    r, c = a.shape
    tr = _row_tile(r, c)

    def body(ids_ref, a_ref, o_ref):
        del ids_ref
        o_ref[...] = a_ref[...].astype(_BF16)

    return pl.pallas_call(
        body, name=name,
        grid_spec=pltpu.PrefetchScalarGridSpec(
            num_scalar_prefetch=1, grid=(r // tr,),
            in_specs=[pl.BlockSpec((tr, c), lambda i, ids: (i, 0))],
            out_specs=pl.BlockSpec((tr, c), lambda i, ids: (i, ids[0]))),
        out_shape=jax.ShapeDtypeStruct((r, n_blocks * c), _BF16),
        compiler_params=pltpu.CompilerParams(dimension_semantics=("parallel",)),
    )(slab_ids, a)


def _matmul_slabs(a, b, ids, width, name, after=None):
    kdim, m = a.shape
    n_slabs = ids.shape[0]
    tm = _tile(m, 1024, LANES)

    def body(ids_ref, a_ref, b_ref, *rest):
        del ids_ref
        rest[-1][0] = lax.dot_general(a_ref[...], b_ref[...], (((0,), (0,)), ((), ())), preferred_element_type=_F32)

    extra = [] if after is None else [after]
    return pl.pallas_call(
        body, name=name,
        grid_spec=pltpu.PrefetchScalarGridSpec(
            num_scalar_prefetch=1, grid=(m // tm, n_slabs),
            in_specs=[pl.BlockSpec((kdim, tm), lambda i, j, ids: (0, i)),
                      pl.BlockSpec((kdim, width), lambda i, j, ids: (0, ids[j]))]
            + [pl.BlockSpec(t.shape, lambda i, j, ids: (0, 0)) for t in extra],
            out_specs=pl.BlockSpec((1, tm, width), lambda i, j, ids: (j, i, 0))),
        out_shape=jax.ShapeDtypeStruct((n_slabs, m, width), _F32),
        compiler_params=pltpu.CompilerParams(dimension_semantics=("parallel", "parallel")),
    )(ids, a, b, *extra)


def _row_tile(rows, cols):
    return _tile(rows, max(BF16_ROWS, min(512, (1 << 19) // cols)), BF16_ROWS)


def _cast_bf16(a, name):
    r, c = a.shape
    tr = _row_tile(r, c)

    def body(a_ref, o_ref):
        o_ref[...] = a_ref[...].astype(_BF16)

    blk = pl.BlockSpec((tr, c), lambda i: (i, 0))
    return pl.pallas_call(
        body, name=name, grid=(r // tr,), in_specs=[blk], out_specs=blk,
        out_shape=jax.ShapeDtypeStruct((r, c), _BF16),
        compiler_params=pltpu.CompilerParams(dimension_semantics=("parallel",)),
    )(a)


def _cast_into_slab(slab_ids, a, name):
    r, c = a.shape
    tr = _row_tile(r, c)

    def body(ids_ref, a_ref, o_ref):
        del ids_ref
        o_ref[0] = a_ref[...].astype(_BF16)

    return pl.pallas_call(
        body, name=name,
        grid_spec=pltpu.PrefetchScalarGridSpec(
            num_scalar_prefetch=1, grid=(r // tr,),
            in_specs=[pl.BlockSpec((tr, c), lambda i, ids: (i, 0))],
            out_specs=pl.BlockSpec((1, tr, c), lambda i, ids: (ids[0], i, 0))),
        out_shape=jax.ShapeDtypeStruct((N_DEV, r, c), _BF16),
        compiler_params=pltpu.CompilerParams(dimension_semantics=("parallel",)),
    )(slab_ids, a)


def _pre_add(slab_ids, grad, got, name):
    _, r, c = grad.shape
    tr = _row_tile(r, c)

    def body(ids_ref, a_ref, b_ref, o_ref):
        del ids_ref
        o_ref[...] = (a_ref[...] + b_ref[...]).astype(_BF16)

    return pl.pallas_call(
        body, name=name,
        grid_spec=pltpu.PrefetchScalarGridSpec(
            num_scalar_prefetch=1, grid=(N_CHIPS - 1, r // tr),
            in_specs=[pl.BlockSpec((1, tr, c), lambda s, i, ids: (ids[s + 1], i, 0)),
                      pl.BlockSpec((1, tr, c), lambda s, i, ids: (s + 1, i, 0))],
            out_specs=pl.BlockSpec((1, tr, c), lambda s, i, ids: (s, i, 0))),
        out_shape=jax.ShapeDtypeStruct((N_CHIPS - 1, r, c), _BF16),
        compiler_params=pltpu.CompilerParams(dimension_semantics=("parallel", "parallel")),
    )(slab_ids, grad, got)


def _ada_forward(craw, w_shard, name):
    d, cols = w_shard.shape
    tk = _tile(d, 512, LANES)

    def body(c_ref, w_ref, o_ref):
        @pl.when(pl.program_id(0) == 0)
        def _():
            o_ref[...] = jnp.zeros_like(o_ref)

        o_ref[...] += jnp.dot(_silu(c_ref[...]).astype(_BF16), w_ref[...].astype(_BF16), preferred_element_type=_F32)

    return pl.pallas_call(
        body, name=name, grid=(d // tk,),
        in_specs=[pl.BlockSpec((craw.shape[0], tk), lambda k: (0, k)), pl.BlockSpec((tk, cols), lambda k: (k, 0))],
        out_specs=pl.BlockSpec((craw.shape[0], cols), lambda k: (0, 0)),
        out_shape=jax.ShapeDtypeStruct((craw.shape[0], cols), _F32),
        compiler_params=pltpu.CompilerParams(dimension_semantics=("arbitrary",)),
    )(craw, w_shard)


def _ada_backward(craw, dmod, w, m, v, name):
    d, cols = w.shape
    rows = craw.shape[0]
    tr = _tile(d, 256, LANES)

    def body(c_ref, dm_ref, w_ref, m_ref, v_ref, g_ref, dl_ref, nm_ref, nv_ref, dc_ref):
        act = _silu(c_ref[...]).astype(_BF16)
        dmb = dm_ref[...].astype(_BF16)
        wv = w_ref[...]
        g = lax.dot_general(act, dmb, (((0,), (0,)), ((), ())), preferred_element_type=_F32)
        delta, nm, nv = _adamw(wv, g, m_ref[...], v_ref[...])
        g_ref[...] = g
        dl_ref[...] = delta
        nm_ref[...] = nm
        nv_ref[...] = nv
        dc = lax.dot_general(dmb, wv.astype(_BF16), (((1,), (1,)), ((), ())), preferred_element_type=_F32)
        dc_ref[...] = jnp.broadcast_to(jnp.sum(dc[N_DEV:], axis=0, keepdims=True), dc_ref.shape)

    blk = pl.BlockSpec((tr, cols), lambda i: (i, 0))
    return pl.pallas_call(
        body, name=name, grid=(d // tr,),
        in_specs=[pl.BlockSpec((rows, tr), lambda i: (0, i)), pl.BlockSpec((rows, cols), lambda i: (0, 0)), blk, blk, blk],
        out_specs=[blk, blk, blk, blk, pl.BlockSpec((SUBLANES, tr), lambda i: (0, i))],
        out_shape=[jax.ShapeDtypeStruct((d, cols), _F32)] * 4 + [jax.ShapeDtypeStruct((SUBLANES, d), _F32)],
        compiler_params=pltpu.CompilerParams(dimension_semantics=("parallel",)),
    )(craw, dmod, w, m, v)


def _rms(xf):
    return lax.rsqrt(jnp.mean(xf * xf, axis=-1, keepdims=True) + EPS)


def _head_mean(v):
    hi = v.astype(_BF16)
    lo = (v - hi.astype(_F32)).astype(_BF16)
    ones = jnp.full((2 * HEAD_DIM, HEAD_DIM), 1.0 / HEAD_DIM, _BF16)
    return jnp.dot(jnp.concatenate([hi, lo], axis=1), ones, preferred_element_type=_F32)


def _prenorm(ctx, x, g_pre, mods, tr, name):
    l, d = ctx.shape
    n = x.shape[0]
    nbc = l // tr

    def body(ctx_ref, x_ref, g_ref, mod_ref, h_ref):
        def emit(src_ref):
            xf = src_ref[...]
            y = (xf * _rms(xf)) * g_ref[...]
            h_ref[...] = (y * (1.0 + mod_ref[0, 0:1, :]) + mod_ref[0, 1:2, :]).astype(_BF16)

        is_ctx = pl.program_id(0) < nbc
        pl.when(is_ctx)(lambda: emit(ctx_ref))
        pl.when(jnp.logical_not(is_ctx))(lambda: emit(x_ref))

    return pl.pallas_call(
        body, name=name, grid=((l + n) // tr,),
        in_specs=[pl.BlockSpec((tr, d), lambda i: (jnp.minimum(i, nbc - 1), 0)),
                  pl.BlockSpec((tr, d), lambda i: (jnp.maximum(i - nbc, 0), 0)),
                  pl.BlockSpec((1, d), lambda i: (0, 0)),
                  pl.BlockSpec((1, SUBLANES, d), lambda i: ((i >= nbc).astype(jnp.int32), 0, 0))],
        out_specs=pl.BlockSpec((tr, d), lambda i: (i, 0)),
        out_shape=jax.ShapeDtypeStruct((l + n, d), _BF16),
        compiler_params=pltpu.CompilerParams(dimension_semantics=("arbitrary",)),
    )(ctx, x, g_pre, mods)


def _prenorm_backward(dh, ctx, x, dxn, g_pre, mods, tr, name):
    l, d = ctx.shape
    n = x.shape[0]
    nbc = l // tr

    def body(dh_ref, ctx_ref, x_ref, dxn_ref, g_ref, mod_ref, gx_ref, dmod_ref, dg_ref):
        i = pl.program_id(0)

        @pl.when(i == 0)
        def _():
            dg_ref[...] = jnp.zeros_like(dg_ref)

        @pl.when(jnp.logical_or(i == 0, i == nbc))
        def _():
            dmod_ref[...] = jnp.zeros_like(dmod_ref)

        def emit(src_ref, latent):
            xf = src_ref[...]
            r = _rms(xf)
            xn = xf * r
            dhv = dh_ref[...]
            one_scale = 1.0 + mod_ref[0, 0:1, :]
            dmod_ref[0, 0:1, :] += jnp.sum(dhv * (xn * g_ref[...]), axis=0, keepdims=True)
            dmod_ref[0, 1:2, :] += jnp.sum(dhv, axis=0, keepdims=True)
            dyg = dhv * one_scale
            dg_ref[0:1, :] += jnp.sum(dyg * xn, axis=0, keepdims=True)
            if latent:
                dn = dyg * g_ref[...]
                gx_ref[...] = dxn_ref[...] + r * (dn - xn * jnp.mean(dn * xn, axis=-1, keepdims=True))

        pl.when(i < nbc)(lambda: emit(ctx_ref, False))
        pl.when(i >= nbc)(lambda: emit(x_ref, True))

    lat = pl.BlockSpec((tr, d), lambda i: (jnp.maximum(i - nbc, 0), 0))
    sel = pl.BlockSpec((1, SUBLANES, d), lambda i: ((i >= nbc).astype(jnp.int32), 0, 0))
    return pl.pallas_call(
        body, name=name, grid=((l + n) // tr,),
        in_specs=[pl.BlockSpec((tr, d), lambda i: (i, 0)),
                  pl.BlockSpec((tr, d), lambda i: (jnp.minimum(i, nbc - 1), 0)),
                  lat, lat, pl.BlockSpec((1, d), lambda i: (0, 0)), sel],
        out_specs=[lat, sel, pl.BlockSpec((SUBLANES, d), lambda i: (0, 0))],
        out_shape=[jax.ShapeDtypeStruct((n, d), _F32), jax.ShapeDtypeStruct((2, SUBLANES, d), _F32),
                   jax.ShapeDtypeStruct((SUBLANES, d), _F32)],
        compiler_params=pltpu.CompilerParams(dimension_semantics=("arbitrary",)),
    )(dh, ctx, x, dxn, g_pre, mods)


def _rope_tables(l, n):
    rows = n // GRID_W
    row = jnp.repeat(jnp.arange(rows, dtype=_F32), GRID_W)
    col = jnp.tile(jnp.arange(GRID_W, dtype=_F32), rows)
    inv = ROPE_THETA ** (-jnp.arange(ROPE_PAIRS, dtype=_F32) / ROPE_PAIRS)
    ang_r, ang_c = row[:, None] * inv, col[:, None] * inv
    cr, sr, cc, sc = jnp.cos(ang_r), jnp.sin(ang_r), jnp.cos(ang_c), jnp.sin(ang_c)
    zero = jnp.zeros_like(sr)
    tc = jnp.concatenate([cr, cr, cc, cc], axis=-1)
    ta = jnp.concatenate([-sr, zero, -sc, zero], axis=-1)
    tb = jnp.concatenate([zero, sr, zero, sc], axis=-1)
    pad = lambda t, fill: jnp.concatenate([jnp.full((l, HEAD_DIM), fill, _F32), t], axis=0)
    return pad(tc, 1.0), pad(ta, 0.0), pad(tb, 0.0)


def _rope(y, tc, ta, tb):
    return y * tc + pltpu.roll(y, HEAD_DIM - ROPE_PAIRS, 1) * ta + pltpu.roll(y, ROPE_PAIRS, 1) * tb


def _rope_transposed(dy, tc, ta, tb):
    return dy * tc + pltpu.roll(dy * ta, ROPE_PAIRS, 1) + pltpu.roll(dy * tb, HEAD_DIM - ROPE_PAIRS, 1)


def _qkv_post(proj, tables, g_q, g_k, heads, kv_heads, tr, name):
    t = proj.shape[0]
    aw, kw = heads * HEAD_DIM, kv_heads * HEAD_DIM
    w3 = aw + 2 * kw

    def body(p_ref, tc_ref, ta_ref, tb_ref, gq_ref, gk_ref, q_ref, k_ref, v_ref):
        tabs = (tc_ref[...], ta_ref[...], tb_ref[...])

        def norm_rope(col, gain):
            xh = p_ref[:, col:col + HEAD_DIM]
            return _rope((xh * lax.rsqrt(_head_mean(xh * xh) + EPS)) * gain, *tabs).astype(_BF16)

        for h in range(heads):
            q_ref[h] = norm_rope(h * HEAD_DIM, gq_ref[...])
        for h in range(kv_heads):
            k_ref[h] = norm_rope(aw + h * HEAD_DIM, gk_ref[...])
            v_ref[h] = p_ref[:, aw + kw + h * HEAD_DIM:aw + kw + (h + 1) * HEAD_DIM].astype(_BF16)

    tab = pl.BlockSpec((tr, HEAD_DIM), lambda i: (i, 0))
    gain = pl.BlockSpec((1, HEAD_DIM), lambda i: (0, 0))
    return pl.pallas_call(
        body, name=name, grid=(t // tr,),
        in_specs=[pl.BlockSpec((tr, w3), lambda i: (i, 0)), tab, tab, tab, gain, gain],
        out_specs=[pl.BlockSpec((heads, tr, HEAD_DIM), lambda i: (0, i, 0)),
                   pl.BlockSpec((kv_heads, tr, HEAD_DIM), lambda i: (0, i, 0)),
                   pl.BlockSpec((kv_heads, tr, HEAD_DIM), lambda i: (0, i, 0))],
        out_shape=[jax.ShapeDtypeStruct((heads, t, HEAD_DIM), _BF16),
                   jax.ShapeDtypeStruct((kv_heads, t, HEAD_DIM), _BF16),
                   jax.ShapeDtypeStruct((kv_heads, t, HEAD_DIM), _BF16)],
        compiler_params=pltpu.CompilerParams(dimension_semantics=("parallel",)),
    )(proj, *tables, g_q, g_k)


def _qkv_post_backward(proj, dq, dk, dv, tables, g_q, g_k, dproj, l, tr, name):
    t = proj.shape[0]
    heads, kv_heads = dq.shape[0], dk.shape[0]
    aw, kw = heads * HEAD_DIM, kv_heads * HEAD_DIM
    w3 = aw + 2 * kw
    nbc = l // tr

    def body(p_ref, dq_ref, dk_ref, dv_ref, tc_ref, ta_ref, tb_ref, gq_ref, gk_ref, dproj_ref, o_ref, dgq_ref, dgk_ref):
        del dproj_ref
        i = pl.program_id(0)

        @pl.when(i == 0)
        def _():
            dgq_ref[...] = jnp.zeros_like(dgq_ref)
            dgk_ref[...] = jnp.zeros_like(dgk_ref)

        tabs = (tc_ref[...], ta_ref[...], tb_ref[...])
        latent = i >= nbc

        def back(col, dout, gain, dg_ref):
            xh = p_ref[:, col:col + HEAD_DIM]
            r = lax.rsqrt(_head_mean(xh * xh) + EPS)
            xn = xh * r
            dy = _rope_transposed(dout, *tabs)
            dg_ref[0:1, :] += jnp.sum(dy * xn, axis=0, keepdims=True)
            dn = dy * gain
            o_ref[:, col:col + HEAD_DIM] = (r * (dn - xn * _head_mean(dn * xn))).astype(_BF16)

        for h in range(heads):
            back(h * HEAD_DIM, jnp.where(latent, dq_ref[h], 0.0), gq_ref[...], dgq_ref)
        for h in range(kv_heads):
            back(aw + h * HEAD_DIM, dk_ref[h], gk_ref[...], dgk_ref)
            o_ref[:, aw + kw + h * HEAD_DIM:aw + kw + (h + 1) * HEAD_DIM] = dv_ref[h].astype(_BF16)

    tab = pl.BlockSpec((tr, HEAD_DIM), lambda i: (i, 0))
    gain = pl.BlockSpec((1, HEAD_DIM), lambda i: (0, 0))
    acc = pl.BlockSpec((SUBLANES, HEAD_DIM), lambda i: (0, 0))
    return pl.pallas_call(
        body, name=name, grid=(t // tr,),
        in_specs=[pl.BlockSpec((tr, w3), lambda i: (i, 0)),
                  pl.BlockSpec((heads, tr, HEAD_DIM), lambda i: (0, jnp.maximum(i - nbc, 0), 0)),
                  pl.BlockSpec((kv_heads, tr, HEAD_DIM), lambda i: (0, i, 0)),
                  pl.BlockSpec((kv_heads, tr, HEAD_DIM), lambda i: (0, i, 0)),
                  tab, tab, tab, gain, gain, _ANY],
        out_specs=[pl.BlockSpec((tr, w3), lambda i: (i, 0)), acc, acc],
        out_shape=[jax.ShapeDtypeStruct(dproj.shape, dproj.dtype), jax.ShapeDtypeStruct((SUBLANES, HEAD_DIM), _F32),
                   jax.ShapeDtypeStruct((SUBLANES, HEAD_DIM), _F32)],
        input_output_aliases={9: 0},
        compiler_params=pltpu.CompilerParams(dimension_semantics=("arbitrary",)),
    )(proj, dq, dk, dv, *tables, g_q, g_k, dproj)


def _zero_context_rows(dproj, l, w3, tr, name):
    t, iw = dproj.shape

    def body(dproj_ref, o_ref):
        del dproj_ref
        o_ref[...] = jnp.zeros_like(o_ref)

    return pl.pallas_call(
        body, name=name, grid=(l // tr, iw // w3 - 1),
        in_specs=[_ANY], out_specs=pl.BlockSpec((tr, w3), lambda i, j: (i, j + 1)),
        out_shape=jax.ShapeDtypeStruct(dproj.shape, dproj.dtype), input_output_aliases={0: 0},
        compiler_params=pltpu.CompilerParams(dimension_semantics=("parallel", "parallel")),
    )(dproj)


def _attention(q, k, v, proj, l, mix, tq, name):
    heads, t, _ = q.shape
    kv_heads = k.shape[0]
    n = t - l
    rows = GQA_GROUP * tq
    gw = GQA_GROUP * HEAD_DIM
    aw = heads * HEAD_DIM
    gate_col = (aw + 2 * kv_heads * HEAD_DIM) // gw
    off = l // tq

    def body(q_ref, k_ref, v_ref, g_ref, o_ref, y_ref, lse_ref):
        lane = lax.broadcasted_iota(jnp.int32, (tq, LANES), 1)
        lse_blk = jnp.zeros((tq, LANES), _F32)
        for first in range(0, GQA_GROUP, ATTN_SUB_HEADS):
            qs = q_ref[first:first + ATTN_SUB_HEADS].reshape(ATTN_SUB_HEADS * tq, HEAD_DIM)
            raw = lax.dot_general(qs, k_ref[0], (((1,), (1,)), ((), ())), preferred_element_type=_F32)
            m = jnp.max(raw, axis=-1, keepdims=True)
            p = jnp.exp2((raw - m) * (ATTN_SCALE * LOG2_E))
            denom = jnp.sum(p, axis=-1, keepdims=True)
            os_ = jnp.dot(p.astype(_BF16), v_ref[0], preferred_element_type=_F32) / denom
            lse_s = m * ATTN_SCALE + jnp.log(denom)
            for j in range(ATTN_SUB_HEADS):
                g = first + j
                og = os_[j * tq:(j + 1) * tq]
                cols = slice(g * HEAD_DIM, (g + 1) * HEAD_DIM)
                o_ref[:, cols] = og
                y_ref[:, cols] = (og * _silu(g_ref[:, cols])).astype(_BF16)
                lse_blk = jnp.where(lane == g, lse_s[j * tq:(j + 1) * tq], lse_blk)
        lse_ref[0] = lse_blk

    return pl.pallas_call(
        body, name=name, grid=(kv_heads, n // tq),
        in_specs=[pl.BlockSpec((GQA_GROUP, tq, HEAD_DIM), lambda h, i: (h, i + off, 0)),
                  pl.BlockSpec((1, t, HEAD_DIM), lambda h, i: (h, 0, 0)),
                  pl.BlockSpec((1, t, HEAD_DIM), lambda h, i: (h, 0, 0)),
                  pl.BlockSpec((tq, gw), lambda h, i: (i + off, gate_col + h))],
        out_specs=[pl.BlockSpec((tq, gw), lambda h, i: (i, h)),
                   pl.BlockSpec((tq, gw), lambda h, i: (i, h)),
                   pl.BlockSpec((1, tq, LANES), lambda h, i: (h, i, 0))],
        out_shape=[jax.ShapeDtypeStruct((n, aw), _F32), jax.ShapeDtypeStruct((n, mix), _BF16),
                   jax.ShapeDtypeStruct((kv_heads, n, LANES), _F32)],
        compiler_params=pltpu.CompilerParams(dimension_semantics=("parallel", "parallel")),
    )(q, k, v, proj)


def _attention_backward(q, k, v, attn_o, dy, proj, lse, after, dproj, l, tq, name):
    heads, t, _ = q.shape
    kv_heads = k.shape[0]
    n = t - l
    rows = GQA_GROUP * tq
    gw = GQA_GROUP * HEAD_DIM
    aw = heads * HEAD_DIM
    gate_col = (aw + 2 * kv_heads * HEAD_DIM) // gw
    off = l // tq
    n_parts = next(p for p in (ATTN_KEY_PARTS, 2, 1) if t % (p * BF16_ROWS) == 0)
    part = t // n_parts

    def body(q_ref, k_ref, v_ref, o_ref, dy_ref, g_ref, lse_ref, after_ref, dproj_ref, dq_ref, dg_ref, dk_ref, dv_ref):
        del after_ref, dproj_ref

        @pl.when(pl.program_id(1) == 0)
        def _():
            dk_ref[...] = jnp.zeros_like(dk_ref)
            dv_ref[...] = jnp.zeros_like(dv_ref)

        lse_blk = lse_ref[0]
        for first in range(0, GQA_GROUP, ATTN_BWD_SUB_HEADS):
            qs = q_ref[first:first + ATTN_BWD_SUB_HEADS].reshape(ATTN_BWD_SUB_HEADS * tq, HEAD_DIM)
            do_parts, delta_parts, lse_parts = [], [], []
            for g in range(first, first + ATTN_BWD_SUB_HEADS):
                cols = slice(g * HEAD_DIM, (g + 1) * HEAD_DIM)
                gate, og, dyg = g_ref[:, cols], o_ref[:, cols], dy_ref[:, cols]
                dog = dyg * _silu(gate)
                dg_ref[:, cols] = (dyg * og * _silu_grad(gate)).astype(_BF16)
                do_parts.append(dog)
                delta_parts.append(jnp.sum(dog * og, axis=-1, keepdims=True))
                lse_parts.append(lse_blk[:, g:g + 1])
            dos = jnp.concatenate(do_parts, axis=0).astype(_BF16)
            delta = jnp.concatenate(delta_parts, axis=0)
            lse2 = jnp.concatenate(lse_parts, axis=0) * LOG2_E
            dqs = jnp.zeros((ATTN_BWD_SUB_HEADS * tq, HEAD_DIM), _F32)
            for part_i in range(n_parts):
                keys = slice(part_i * part, (part_i + 1) * part)
                ks, vs = k_ref[0, keys, :], v_ref[0, keys, :]
                raw = lax.dot_general(qs, ks, (((1,), (1,)), ((), ())), preferred_element_type=_F32)
                p = jnp.exp2(raw * (ATTN_SCALE * LOG2_E) - lse2)
                dp = lax.dot_general(dos, vs, (((1,), (1,)), ((), ())), preferred_element_type=_F32)
                ds = (p * (dp - delta)).astype(_BF16)
                dqs = dqs + jnp.dot(ds, ks, preferred_element_type=_F32)
                dk_ref[0, keys, :] += ATTN_SCALE * lax.dot_general(
                    ds, qs, (((0,), (0,)), ((), ())), preferred_element_type=_F32)
                dv_ref[0, keys, :] += lax.dot_general(
                    p.astype(_BF16), dos, (((0,), (0,)), ((), ())), preferred_element_type=_F32)
            dq_ref[first:first + ATTN_BWD_SUB_HEADS] = (ATTN_SCALE * dqs).reshape(ATTN_BWD_SUB_HEADS, tq, HEAD_DIM)

    kv_spec = pl.BlockSpec((1, t, HEAD_DIM), lambda h, i: (h, 0, 0))
    tok = pl.BlockSpec((tq, gw), lambda h, i: (i, h))
    gate = pl.BlockSpec((tq, gw), lambda h, i: (i + off, gate_col + h))
    return pl.pallas_call(
        body, name=name, grid=(kv_heads, n // tq),
        in_specs=[pl.BlockSpec((GQA_GROUP, tq, HEAD_DIM), lambda h, i: (h, i + off, 0)), kv_spec, kv_spec,
                  tok, tok, gate, pl.BlockSpec((1, tq, LANES), lambda h, i: (h, i, 0)),
                  pl.BlockSpec(after.shape, lambda h, i: (0, 0)), _ANY],
        out_specs=[pl.BlockSpec((GQA_GROUP, tq, HEAD_DIM), lambda h, i: (h, i, 0)), gate, kv_spec, kv_spec],
        out_shape=[jax.ShapeDtypeStruct((heads, n, HEAD_DIM), _F32), jax.ShapeDtypeStruct(dproj.shape, dproj.dtype),
                   jax.ShapeDtypeStruct((kv_heads, t, HEAD_DIM), _F32), jax.ShapeDtypeStruct((kv_heads, t, HEAD_DIM), _F32)],
        input_output_aliases={8: 1},
        compiler_params=pltpu.CompilerParams(dimension_semantics=("parallel", "arbitrary")),
    )(q, k, v, attn_o, dy, proj, lse, after, dproj)


def _halo_specs(tp, width, col, row_off, total_rows):
    per = tp // POOL_HALO
    first = row_off // POOL_HALO
    last = total_rows // POOL_HALO - 1
    return [pl.BlockSpec((tp, width), lambda i: (i + row_off // tp, col)),
            pl.BlockSpec((POOL_HALO, width), lambda i: (jnp.maximum(first + i * per - 1, 0), col)),
            pl.BlockSpec((POOL_HALO, width), lambda i: (jnp.minimum(first + (i + 1) * per, last), col))]


def _with_halo(cur, prev, nxt, t0, n):
    tp = cur.shape[0]
    r8 = lax.broadcasted_iota(jnp.int32, (POOL_HALO, 1), 0)
    prev = jnp.where(t0 - POOL_HALO + r8 >= 0, prev, 0.0)
    nxt = jnp.where(t0 + tp + r8 < n, nxt, 0.0)
    return jnp.concatenate([prev, cur, nxt], axis=0)


def _shift_rows(a, s):
    return pltpu.roll(a, s % a.shape[0], 0)


def _window_sum(e, w, mirrored):
    a = e + _shift_rows(e, -1 if mirrored else 1)
    s = 1
    while 2 * s < w:
        a = _shift_rows(a, s) + _shift_rows(a, -s)
        s *= 2
    return a


def _window_count(t, w, n):
    half = w // 2
    return (jnp.minimum(t + half, n) - jnp.maximum(t - half, 0)).astype(_F32)


def _pool_forward(gi, proj, y, pool_w, pool_scale, l, heads, kv_heads, tp, name):
    t = proj.shape[0]
    n = t - l
    pg = pool_w.shape[-1]
    w = POOL_WINDOWS[gi]
    aw, kw = heads * HEAD_DIM, kv_heads * HEAD_DIM
    u_col = (2 * aw + 2 * kw) // pg + gi
    gate_col = (2 * aw + 2 * kw + len(POOL_WINDOWS) * pg) // pg + gi

    def body(u_ref, up_ref, un_ref, g_ref, w_ref, sc_ref, y_in_ref, y_ref, raw_ref, d_ref):
        del y_in_ref
        t0 = pl.program_id(0) * tp
        cur = u_ref[...]
        win = _window_sum(_with_halo(cur, up_ref[...], un_ref[...], t0, n), w, False)[POOL_HALO:POOL_HALO + tp]
        tok = t0 + lax.broadcasted_iota(jnp.int32, (tp, 1), 0)
        d = (win / _window_count(tok, w, n) - cur).astype(_BF16)
        raw = jnp.dot(d, w_ref[...].reshape(pg, pg), preferred_element_type=_F32)
        d_ref[...] = d
        raw_ref[...] = raw
        y_ref[...] = ((raw * sc_ref[...]) * _silu(g_ref[...])).astype(_BF16)

    blk = pl.BlockSpec((tp, pg), lambda i: (i, 0))
    return pl.pallas_call(
        body, name=name, grid=(n // tp,),
        in_specs=_halo_specs(tp, pg, u_col, l, t) + [
            pl.BlockSpec((tp, pg), lambda i: (i + l // tp, gate_col)),
            pl.BlockSpec((N_DEV, 1, pg // N_DEV, pg), lambda i: (0, gi, 0, 0)),
            pl.BlockSpec((1, pg), lambda i: (0, gi)), _ANY],
        out_specs=[pl.BlockSpec((tp, pg), lambda i: (i, aw // pg + gi)), blk, blk],
        out_shape=[jax.ShapeDtypeStruct(y.shape, y.dtype), jax.ShapeDtypeStruct((n, pg), _F32),
                   jax.ShapeDtypeStruct((n, pg), _BF16)],
        input_output_aliases={6: 0},
        compiler_params=pltpu.CompilerParams(dimension_semantics=("arbitrary",)),
    )(proj, proj, proj, proj, pool_w, pool_scale, y)


def _pool_backward_gate(gi, dy, proj, raw, pool_w, pool_scale, dproj, l, heads, kv_heads, tp, name):
    n, pg = raw.shape
    aw, kw = heads * HEAD_DIM, kv_heads * HEAD_DIM
    gate_col = (2 * aw + 2 * kw + len(POOL_WINDOWS) * pg) // pg + gi

    def body(dy_ref, g_ref, raw_ref, w_ref, sc_ref, dproj_ref, dg_ref, dr_ref, dd_ref, ds_ref):
        del dproj_ref

        @pl.when(pl.program_id(0) == 0)
        def _():
            ds_ref[...] = jnp.zeros_like(ds_ref)

        gate, rawv, dyv, scale = g_ref[...], raw_ref[...], dy_ref[...], sc_ref[...]
        dpool = dyv * _silu(gate)
        dg_ref[...] = (dyv * (rawv * scale) * _silu_grad(gate)).astype(_BF16)
        ds_ref[0:1, :] += jnp.sum(dpool * rawv, axis=0, keepdims=True)
        draw = (dpool * scale).astype(_BF16)
        dr_ref[...] = draw
        dd_ref[...] = lax.dot_general(
            draw, w_ref[...].reshape(pg, pg), (((1,), (1,)), ((), ())), preferred_element_type=_F32)

    blk = pl.BlockSpec((tp, pg), lambda i: (i, 0))
    gate = pl.BlockSpec((tp, pg), lambda i: (i + l // tp, gate_col))
    return pl.pallas_call(
        body, name=name, grid=(n // tp,),
        in_specs=[pl.BlockSpec((tp, pg), lambda i: (i, aw // pg + gi)), gate, blk,
                  pl.BlockSpec((N_DEV, 1, pg // N_DEV, pg), lambda i: (0, gi, 0, 0)),
                  pl.BlockSpec((1, pg), lambda i: (0, gi)), _ANY],
        out_specs=[gate, blk, blk, pl.BlockSpec((SUBLANES, pg), lambda i: (0, 0))],
        out_shape=[jax.ShapeDtypeStruct(dproj.shape, dproj.dtype), jax.ShapeDtypeStruct((n, pg), _BF16),
                   jax.ShapeDtypeStruct((n, pg), _F32), jax.ShapeDtypeStruct((SUBLANES, pg), _F32)],
        input_output_aliases={5: 0},
        compiler_params=pltpu.CompilerParams(dimension_semantics=("arbitrary",)),
    )(dy, proj, raw, pool_w, pool_scale, dproj)


def _pool_backward_window(gi, dd, dproj, l, col, tp, name):
    n, pg = dd.shape
    w = POOL_WINDOWS[gi]

    def body(c_ref, p_ref, n_ref, dproj_ref, du_ref):
        del dproj_ref
        t0 = pl.program_id(0) * tp
        cur = c_ref[...]
        e = _with_halo(cur, p_ref[...], n_ref[...], t0, n)
        tok = t0 - POOL_HALO + lax.broadcasted_iota(jnp.int32, (tp + 2 * POOL_HALO, 1), 0)
        e = e / jnp.maximum(_window_count(tok, w, n), 1.0)
        du_ref[...] = (_window_sum(e, w, True)[POOL_HALO:POOL_HALO + tp] - cur).astype(_BF16)

    return pl.pallas_call(
        body, name=name, grid=(n // tp,),
        in_specs=_halo_specs(tp, pg, 0, 0, n) + [_ANY],
        out_specs=pl.BlockSpec((tp, pg), lambda i: (i + l // tp, col)),
        out_shape=jax.ShapeDtypeStruct(dproj.shape, dproj.dtype), input_output_aliases={3: 0},
        compiler_params=pltpu.CompilerParams(dimension_semantics=("arbitrary",)),
    )(dd, dd, dd, dproj)


def _post(out, x, target, gate, g_post, tr, name):
    n, d = out.shape

    def body(o_ref, x_ref, t_ref, gate_ref, g_ref, dxn_ref, do_ref, dgate_ref, dg_ref, loss_ref):
        @pl.when(pl.program_id(0) == 0)
        def _():
            dgate_ref[...] = jnp.zeros_like(dgate_ref)
            dg_ref[...] = jnp.zeros_like(dg_ref)
            loss_ref[...] = jnp.zeros_like(loss_ref)

        ov = o_ref[...]
        r = _rms(ov)
        on = ov * r
        normed = on * g_ref[...]
        err = (x_ref[...] + gate_ref[...] * normed) - t_ref[...]
        loss_ref[...] += jnp.sum(err * err)
        dxn = err / d
        dxn_ref[...] = dxn
        dgate_ref[0:1, :] += jnp.sum(dxn * normed, axis=0, keepdims=True)
        dr = dxn * gate_ref[...]
        dg_ref[0:1, :] += jnp.sum(dr * on, axis=0, keepdims=True)
        dn = dr * g_ref[...]
        do_ref[...] = (r * (dn - on * jnp.mean(dn * on, axis=-1, keepdims=True))).astype(_BF16)

    blk = pl.BlockSpec((tr, d), lambda i: (i, 0))
    vec = pl.BlockSpec((1, d), lambda i: (0, 0))
    acc = pl.BlockSpec((SUBLANES, d), lambda i: (0, 0))
    return pl.pallas_call(
        body, name=name, grid=(n // tr,),
        in_specs=[blk, blk, blk, vec, vec],
        out_specs=[blk, blk, acc, acc, pl.BlockSpec((SUBLANES, LANES), lambda i: (0, 0))],
        out_shape=[jax.ShapeDtypeStruct((n, d), _F32), jax.ShapeDtypeStruct((n, d), _BF16),
                   jax.ShapeDtypeStruct((SUBLANES, d), _F32), jax.ShapeDtypeStruct((SUBLANES, d), _F32),
                   jax.ShapeDtypeStruct((SUBLANES, LANES), _F32)],
        compiler_params=pltpu.CompilerParams(dimension_semantics=("arbitrary",)),
    )(out, x, target, gate, g_post)


def _adam_sharded(slab_ids, grad, got, far, w, m, v, name):
    r, c = w.shape
    tr = _tile(r, max(BF16_ROWS, min(256, (1 << 18) // c)), BF16_ROWS)

    def body(ids_ref, own_ref, got_ref, far_ref, w_ref, m_ref, v_ref, g_ref, dl_ref, nm_ref, nv_ref):
        del ids_ref
        g = own_ref[0] + got_ref[0]
        for k in range(N_CHIPS - 1):
            g = g + far_ref[k].astype(_F32)
        delta, nm, nv = _adamw(w_ref[...], g, m_ref[...], v_ref[...])
        g_ref[...] = g
        dl_ref[...] = delta
        nm_ref[...] = nm
        nv_ref[...] = nv

    blk = pl.BlockSpec((tr, c), lambda i, ids: (i, 0))
    return pl.pallas_call(
        body, name=name,
        grid_spec=pltpu.PrefetchScalarGridSpec(
            num_scalar_prefetch=1, grid=(r // tr,),
            in_specs=[pl.BlockSpec((1, tr, c), lambda i, ids: (ids[0], i, 0)),
                      pl.BlockSpec((1, tr, c), lambda i, ids: (0, i, 0)),
                      pl.BlockSpec((N_CHIPS - 1, tr, c), lambda i, ids: (0, i, 0)), blk, blk, blk],
            out_specs=[blk] * 4),
        out_shape=[jax.ShapeDtypeStruct((r, c), _F32)] * 4,
        compiler_params=pltpu.CompilerParams(dimension_semantics=("parallel",)),
    )(slab_ids, grad, got, far, w, m, v)


def _adam_replicated(parts, extra, through_silu, w, m, v, name):
    def body(p_ref, e_ref, s_ref, w_ref, m_ref, v_ref, g_ref, dl_ref, nm_ref, nv_ref):
        total = p_ref[0] + e_ref[0]
        for dev in range(1, N_DEV):
            total = total + (p_ref[dev] + e_ref[dev])
        g = jnp.where(s_ref[...] > 0.5, total * _silu_grad(w_ref[...]), total)
        delta, nm, nv = _adamw(w_ref[...], g, m_ref[...], v_ref[...])
        g_ref[...] = g
        dl_ref[...] = delta
        nm_ref[...] = nm
        nv_ref[...] = nv

    return pl.pallas_call(
        body, name=name, in_specs=[_VMEM] * 6, out_specs=[_VMEM] * 4,
        out_shape=[jax.ShapeDtypeStruct(w.shape, _F32)] * 4,
    )(parts, extra, through_silu, w, m, v)


def _as_rows(vec):
    size = vec.shape[0]
    padded = -(-size // (SUBLANES * LANES)) * SUBLANES * LANES
    return jnp.pad(vec, (0, padded - size)).reshape(padded // LANES, LANES)


def kernel(x, c, ctx, c_ctx, w_ada, b_ada, norm_pre, norm_post, w_in, q_norm, k_norm, pool_w, pool_scale, w_out, loss_target, m_c_ctx, m_w_ada, m_b_ada, m_norm_pre, m_norm_post, m_w_in, m_q_norm, m_k_norm, m_pool_w, m_pool_scale, m_w_out, v_c_ctx, v_w_ada, v_b_ada, v_norm_pre, v_norm_post, v_w_in, v_q_norm, v_k_norm, v_pool_w, v_pool_scale, v_w_out):
    me = _dev_index(*_position())
    x2, ctx2, target = x[0], ctx[0], loss_target[0]
    n, d = x2.shape
    l = ctx2.shape[0]
    t = l + n
    aw = d // 2
    heads = aw // HEAD_DIM
    kv_heads = heads // GQA_GROUP
    kw = kv_heads * HEAD_DIM
    n_groups = len(POOL_WINDOWS)
    pg = (d - aw) // n_groups
    mix = d
    tr = _tile(l, 128, BF16_ROWS)
    tq = _tile(l, 128, BF16_ROWS)
    tp = _tile(l, 512, POOL_HALO)

    xi, yi, ci = _position()
    slab_ids = jnp.stack([_dev_index(*chip, ci) for chip in _chip_order(xi, yi)]).astype(jnp.int32)

    cw = w_in.shape[-1]
    wg = _cast_into_columns(slab_ids, w_in[0], N_DEV, "cast_w_in")
    late = [_cast_into_slab(slab_ids, w_out[0], "cast_w_out"),
            _cast_into_slab(slab_ids, pool_w[0].reshape(-1, pg), "cast_pool_w")]

    c_all = _all_gather_small(_as_rows(c[0]), "gather_c").reshape(N_DEV, -1)[:, :d]
    craw = jnp.concatenate([c_all, jnp.broadcast_to(c_ctx[None], (N_DEV, d))], axis=0)
    ada = _ada_forward(craw, w_ada[0], "ada_forward")
    ada_all = _all_gather_small(ada, "gather_ada")
    mod_all = ada_all.transpose(1, 0, 2).reshape(ada.shape[0], -1) + b_ada[0]
    mod = lax.dynamic_index_in_dim(mod_all, me, 0, keepdims=False)
    mod_c = mod_all[N_DEV]
    shift, scale, gate = mod[:d], mod[d:2 * d], mod[2 * d:]
    zeros6 = jnp.zeros((SUBLANES - 2, d), _F32)
    mods = jnp.stack([jnp.concatenate([mod_c[None, d:2 * d], mod_c[None, :d], zeros6], axis=0),
                      jnp.concatenate([scale[None], shift[None], zeros6], axis=0)])

    a_s, a_r, wg, tok = _w_in_hop(wg, cw, [], "a", ada_all, "gather_w_in_a")
    h_all = _prenorm(ctx2, x2, norm_pre, mods + tok[0, 0], tr, "prenorm")
    order_ids = jnp.stack([_dev_index(*dev) for dev in _w_in_order(xi, yi, ci)]).astype(jnp.int32)
    proj = lax.empty((t, N_DEV * cw), _F32)
    proj = _proj_blocks(h_all, wg, proj, order_ids, 0, 1, cw, tok, "proj_0")
    b_s, b_r, wg, tok = _w_in_hop(wg, cw, [("a", a_s, a_r, [0, 1], [])], "b", proj, "gather_w_in_b")
    proj = _proj_blocks(h_all, wg, proj, order_ids, 1, 2, cw, tok, "proj_1")
    c_s, c_r, wg, tok = _w_in_hop(wg, cw, [("b", b_s, b_r, [2, 0], [])], "c", proj, "gather_w_in_c")
    proj = _proj_blocks(h_all, wg, proj, order_ids, 3, 2, cw, tok, "proj_2")
    d_s, d_r, wg, tok = _w_in_hop(
        wg, cw, [("c", c_s, c_r, [0], []), ("b", b_s, b_r, [1], [])], "d", proj, "gather_w_in_d")
    proj = _proj_blocks(h_all, wg, proj, order_ids, 5, 2, cw, tok, "proj_3")
    w_in_g, tok = _w_in_hop(
        wg, cw, [("d", d_s, d_r, [0], [0]), ("a", a_s, a_r, [], [0, 1]), ("b", b_s, b_r, [], [0, 1, 2]),
                 ("c", c_s, c_r, [], [0])], None, proj, "gather_w_in_end")
    flight_w = _gather_slabs_start(late, w_in_g, "gather_late_start")
    proj = _proj_blocks(h_all, w_in_g, proj, order_ids, 7, 1, cw, flight_w[-1], "proj_4")
    tables = _rope_tables(l, n)
    q, k, v = _qkv_post(proj, tables, q_norm, k_norm, heads, kv_heads, tr, "qkv_post")
    attn_o, y, lse = _attention(q, k, v, proj, l, mix, _tile(l, 256, BF16_ROWS), "attention")
    w_out_g8, pool_g8 = _gather_slabs_wait(*flight_w[:3], attn_o, "gather_late_wait")
    w_out_g = w_out_g8.reshape(mix, d)
    pool_g = pool_g8.reshape(N_DEV, n_groups, pg // N_DEV, pg)
    raws, ds = [], []
    for gi in range(n_groups):
        y, raw, dsave = _pool_forward(gi, proj, y, pool_g, pool_scale, l, heads, kv_heads, tp, f"pool_forward_{gi}")
        raws.append(raw)
        ds.append(dsave)
    out = _matmul(y, w_out_g, name="out_proj")
    dxn, dout, dgate8, dgpost8, loss8 = _post(out, x2, target, gate[None], norm_post, tr, "post")

    gw_out = _matmul(y, dout, ta=True, name="grad_w_out").reshape(N_DEV, mix // N_DEV, d)
    flight_so = _exchange_start(_sibling_copies_by_device, [gw_out], "exchange_sibling_start_w_out", land_slabs=N_CHIPS)
    dy = _matmul(dout, w_out_g, tb=True, after=flight_so[-1], name="d_y")
    gw_out, got_out = _exchange_wait(
        _sibling_copies_by_device, *flight_so[:4], dy, "exchange_sibling_wait_w_out", with_sources=True)
    sum_out = _pre_add(slab_ids, gw_out, got_out, "pre_add_w_out")
    flight_out = _exchange_start(_chip_copies, [sum_out], "exchange_chips_start_w_out")
    w3 = aw + 2 * kw
    dq, dproj, dk, dv = _attention_backward(
        q, k, v, attn_o, dy, proj, lse, flight_out[-1], lax.empty(proj.shape, _BF16), l, tq, "attention_backward")
    dproj, dgq8, dgk8 = _qkv_post_backward(proj, dq, dk, dv, tables, q_norm, k_norm, dproj, l, tr, "qkv_post_backward")
    dproj = _zero_context_rows(dproj, l, w3, tr, "zero_context_rows")
    gpw, dps8 = [], []
    for gi in range(n_groups):
        dproj, draw, dd, dps = _pool_backward_gate(
            gi, dy, proj, raws[gi], pool_g, pool_scale, dproj, l, heads, kv_heads, tp, f"pool_backward_gate_{gi}")
        dproj = _pool_backward_window(gi, dd, dproj, l, (w3 + aw) // pg + gi, tp, f"pool_backward_window_{gi}")
        dps8.append(dps)
        gpw.append(_matmul(ds[gi], draw, ta=True, name=f"grad_pool_w_{gi}"))
    cw = w_in.shape[-1]
    other_ids = jnp.stack([_dev_index(*chip, 1 - ci) for chip in _chip_order(xi, yi)]).astype(jnp.int32)
    chip_slabs = jnp.arange(N_CHIPS, dtype=jnp.int32)
    pr = pool_w.shape[2]
    gpw8 = jnp.stack(gpw).reshape(n_groups, N_DEV, pr, pg).transpose(1, 0, 2, 3).reshape(N_DEV, n_groups * pr, pg)
    give_in = _matmul_slabs(h_all, dproj, other_ids, cw, "grad_w_in_sibling")
    flight_sib = _exchange_start(_sibling_copies, [give_in, jnp.take(gpw8, other_ids, axis=0)], "exchange_sibling_start")
    gw_in = _matmul_slabs(h_all, dproj, slab_ids, cw, "grad_w_in_own", after=flight_sib[-1])
    gpw_own = jnp.take(gpw8, slab_ids, axis=0)
    got_in, got_pw = _exchange_wait(_sibling_copies, *flight_sib[:4], gw_in, "exchange_sibling_wait")
    sums_in = [_pre_add(chip_slabs, gw_in, got_in, "pre_add_w_in"), _pre_add(chip_slabs, gpw_own, got_pw, "pre_add_pool_w")]
    flight_in = _exchange_start(_chip_copies, sums_in, "exchange_chips_start_w_in")
    dh = _matmul(dproj, w_in_g, tb=True, tm=1088, tk=3072, after=flight_in[-1], name="d_h")
    grad_x, dmods, dgpre8 = _prenorm_backward(dh, ctx2, x2, dxn, norm_pre, mods, tr, "prenorm_backward")

    dmod_lat = jnp.concatenate([dmods[1, 1], dmods[1, 0], dgate8[0]])
    dmod_ctx = jnp.concatenate([dmods[0, 1], dmods[0, 0], jnp.zeros((d,), _F32)])
    small = jnp.concatenate([dmod_lat, dmod_ctx, dgpre8[0], dgpost8[0], dgq8[0], dgk8[0]] + [p[0] for p in dps8]
                            + [loss8[0, :1]])
    gathered = _all_gather_small(_as_rows(small), "gather_small").reshape(N_DEV, -1)
    o = 0
    take = lambda size: (gathered[:, o:o + size], o + size)
    g_mod, o = take(3 * d)
    g_modc, o = take(3 * d)
    g_pre, o = take(d)
    g_post, o = take(d)
    g_q, o = take(HEAD_DIM)
    g_k, o = take(HEAD_DIM)
    g_ps, o = take(n_groups * pg)
    g_loss, o = take(1)
    cols = w_ada.shape[-1]
    mine = lambda a: lax.dynamic_slice_in_dim(a, me * cols, cols, axis=1)
    dmod_rows = jnp.concatenate([mine(g_mod), mine(g_modc)], axis=0)
    g_wada, dl_wada, nm_wada, nv_wada, dcact = _ada_backward(craw, dmod_rows, w_ada[0], m_w_ada[0], v_w_ada[0], "ada_backward")
    dcc = _all_gather_small(_as_rows(dcact[0]), "gather_dcc").reshape(N_DEV, -1)[:, :d]

    sizes = [d, 3 * d, d, d, HEAD_DIM, HEAD_DIM, n_groups * pg]
    def pack(parts):
        rows = jnp.concatenate(parts, axis=1)
        padded = -(-rows.shape[1] // (SUBLANES * LANES)) * SUBLANES * LANES
        return jnp.pad(rows, ((0, 0), (0, padded - rows.shape[1]))).reshape(N_DEV, padded // LANES, LANES)

    zero = lambda size: jnp.zeros((N_DEV, size), _F32)
    parts = pack([dcc, g_mod, g_pre, g_post, g_q, g_k, g_ps])
    extra = pack([zero(d), g_modc, zero(d), zero(d), zero(HEAD_DIM), zero(HEAD_DIM), zero(n_groups * pg)])
    through_silu = _as_rows(jnp.concatenate([jnp.ones((d,), _F32), jnp.zeros((sum(sizes[1:]),), _F32)]))
    cat = lambda items: _as_rows(jnp.concatenate([a.reshape(-1) for a in items]))
    ws = [c_ctx, b_ada, norm_pre, norm_post, q_norm, k_norm, pool_scale]
    ms = [m_c_ctx, m_b_ada, m_norm_pre, m_norm_post, m_q_norm, m_k_norm, m_pool_scale]
    vs = [v_c_ctx, v_b_ada, v_norm_pre, v_norm_post, v_q_norm, v_k_norm, v_pool_scale]
    rep = _adam_replicated(parts, extra, through_silu, cat(ws), cat(ms), cat(vs), "adam_replicated")

    def split(packed):
        flat_, outs, at = packed.reshape(-1), [], 0
        for w, size in zip(ws, sizes):
            outs.append(flat_[at:at + size].reshape(w.shape))
            at += size
        return outs

    g_rep, dl_rep, nm_rep, nv_rep = [split(r) for r in rep]

    far_out = _exchange_wait(_chip_copies, *flight_out[:4], grad_x, "exchange_chips_wait_w_out")[0]
    far_in, far_pw = _exchange_wait(_chip_copies, *flight_in[:4], rep[0], "exchange_chips_wait_w_in")
    two = lambda a: a.reshape(-1, a.shape[-1])
    sharded = []
    for ids, g, got, far, w, m, v_, name in zip(
            (chip_slabs, slab_ids, chip_slabs), (gw_in, gw_out, gpw_own), (got_in, got_out, got_pw),
            (far_in, far_out, far_pw), (w_in, w_out, pool_w), (m_w_in, m_w_out, m_pool_w),
            (v_w_in, v_w_out, v_pool_w), ("adam_w_in", "adam_w_out", "adam_pool_w")):
        res = _adam_sharded(ids, g, got, far, two(w), two(m), two(v_), name)
        sharded.append([r.reshape(w.shape) for r in res])
    (g_w_in, dl_w_in, nm_w_in, nv_w_in), (g_w_out, dl_w_out, nm_w_out, nv_w_out), (g_pw, dl_pw, nm_pw, nv_pw) = sharded

    loss_sum = g_loss[0, 0]
    for dev in range(1, N_DEV):
        loss_sum = loss_sum + g_loss[dev, 0]
    loss = (0.5 / d) * loss_sum

    def ordered(rep_list, ada_, w_in_, pw_, w_out_):
        return [rep_list[0], ada_[None], rep_list[1], rep_list[2], rep_list[3], w_in_, rep_list[4], rep_list[5],
                pw_, rep_list[6], w_out_]

    return (loss, grad_x[None],
            *ordered(g_rep, g_wada, g_w_in, g_pw, g_w_out),
            *ordered(dl_rep, dl_wada, dl_w_in, dl_pw, dl_w_out),
            *ordered(nm_rep, nm_wada, nm_w_in, nm_pw, nm_w_out),
            *ordered(nv_rep, nv_wada, nv_w_in, nv_pw, nv_w_out))
```

```python
import functools

import jax
import jax.numpy as jnp
from jax import lax
from jax.experimental import pallas as pl
from jax.experimental.pallas import tpu as pltpu

HEAD_DIM = 128
GQA_GROUP = 4
ATTN_SUB_HEADS = 1
ATTN_BWD_SUB_HEADS = 1
ATTN_KEY_PARTS = 2
LOG2_E = 1.4426950408889634
GRID_W = 64
ROPE_PAIRS = HEAD_DIM // 4
ROPE_THETA = 10000.0
ATTN_SCALE = HEAD_DIM ** -0.5
EPS = 1e-6
POOL_WINDOWS = (2, 4, 8, 16)
POOL_HALO = 8
N_DEV = 8
N_CHIPS = 4
ADAM_LR = 0.001
ADAM_B1 = 0.9
ADAM_B2 = 0.999
ADAM_EPS = 1e-08
ADAM_WD = 0.01
ADAM_STEP = 10

LANES = 128
SUBLANES = 8
BF16_ROWS = 16

_MESH = pl.DeviceIdType.MESH
_ANY = pl.BlockSpec(memory_space=pl.ANY)
_VMEM = pl.BlockSpec(memory_space=pltpu.VMEM)
_HBM = pl.BlockSpec(memory_space=pltpu.HBM)
_SEM = pl.BlockSpec(memory_space=pltpu.SEMAPHORE)
_EFFECT = pltpu.SideEffectType.DATAFLOW_SIDE_EFFECTING
_F32 = jnp.float32
_BF16 = jnp.bfloat16


def _tile(dim, pref, align):
    t = min(pref, dim)
    t -= t % align
    while t >= align:
        if dim % t == 0:
            return t
        t -= align
    return dim


def _position():
    return lax.axis_index("x"), lax.axis_index("y"), lax.axis_index("c")


def _flip(v, bit):
    return 1 - v if bit else v


def _dev_index(x, y, c):
    return 4 * x + 2 * y + c


def _silu(g):
    return g * jax.nn.sigmoid(g)


def _silu_grad(g):
    s = jax.nn.sigmoid(g)
    return s * (1.0 + g * (1.0 - s))


def _adamw(w, g, m, v):
    m = ADAM_B1 * m + (1.0 - ADAM_B1) * g
    v = ADAM_B2 * v + (1.0 - ADAM_B2) * (g * g)
    m_hat = m / (1.0 - ADAM_B1 ** ADAM_STEP)
    v_hat = v / (1.0 - ADAM_B2 ** ADAM_STEP)
    delta = -ADAM_LR * (m_hat / (jnp.sqrt(v_hat) + ADAM_EPS) + ADAM_WD * w)
    return delta, m, v


def _all_gather_small(v, name):
    rows, cols = v.shape

    def body(v_ref, out_ref, send_sems, recv_sems):
        x, y, c = _position()
        me = _dev_index(x, y, c)
        out_ref[me] = v_ref[...]
        peers = [(_flip(x, k & 4), _flip(y, k & 2), _flip(c, k & 1)) for k in range(1, N_DEV)]

        def copy(k, block, to):
            return pltpu.make_async_remote_copy(
                src_ref=v_ref, dst_ref=out_ref.at[block], send_sem=send_sems.at[k], recv_sem=recv_sems.at[k],
                device_id=to, device_id_type=_MESH)

        sends = [copy(k, me, p) for k, p in enumerate(peers)]
        for s in sends:
            s.start()
        for k, p in enumerate(peers):
            copy(k, _dev_index(*p), p).wait_recv()
        for s in sends:
            s.wait_send()

    return pl.pallas_call(
        body, name=name,
        out_shape=jax.ShapeDtypeStruct((N_DEV, rows, cols), v.dtype),
        in_specs=[_VMEM], out_specs=_VMEM,
        scratch_shapes=[pltpu.SemaphoreType.DMA((N_DEV - 1,)), pltpu.SemaphoreType.DMA((N_DEV - 1,))],
    )(v)


def _route(x, y, c):
    first = (x + (1 - c) * (1 - 2 * x), y + c * (1 - 2 * y))
    second = (x + c * (1 - 2 * x), y + (1 - c) * (1 - 2 * y))
    return first, second, (1 - x, 1 - y)


def _w_in_order(x, y, c):
    first, second, diagonal = _route(x, y, c)
    return [(x, y, c), (x, y, 1 - c), (*first, c), (*second, 1 - c), (*second, c), (*first, 1 - c),
            (*diagonal, c), (*diagonal, 1 - c)]


W_IN_HOPS = {"a": 2, "b": 3, "c": 1, "d": 1}


def _w_in_hop_copies(group, wg, width, send_sems, recv_sems):
    x, y, c = _position()
    me, sibling = (x, y, c), (x, y, 1 - c)
    first, second, diagonal = _route(x, y, c)

    def cp(k, block, to):
        cols = wg.at[:, pl.ds(pl.multiple_of(_dev_index(*block) * width, width), width)]
        return pltpu.make_async_remote_copy(
            src_ref=cols, dst_ref=cols, send_sem=send_sems.at[k], recv_sem=recv_sems.at[k],
            device_id=to, device_id_type=_MESH)

    if group == "a":
        return [cp(0, me, sibling), cp(1, me, (*first, c))]
    if group == "b":
        return [cp(0, me, (*second, c)), cp(1, (*first, c), (*second, c)), cp(2, (*first, c), sibling)]
    if group == "c":
        return [cp(0, (*second, c), sibling)]
    return [cp(0, (*diagonal, c), sibling)]


def _w_in_hop(wg, width, waits, start, after, name):
    n_sem = 2 * len(waits)

    def body(*refs):
        wg_ref = refs[0]
        for i, (group, _, _, arrivals, sends) in enumerate(waits):
            cps = _w_in_hop_copies(group, wg_ref, width, refs[1 + 2 * i], refs[2 + 2 * i])
            for k in arrivals:
                cps[k].wait_recv()
            for k in sends:
                cps[k].wait_send()
        if start:
            for cp in _w_in_hop_copies(start, wg_ref, width, refs[n_sem + 2], refs[n_sem + 3]):
                cp.start()
        refs[-1][...] = jnp.zeros_like(refs[-1])

    sems = [s for w in waits for s in w[1:3]]
    new = [pltpu.SemaphoreType.DMA((W_IN_HOPS[start],))] * 2 if start else []
    outs = pl.pallas_call(
        body, name=name,
        out_shape=(*new, pltpu.HBM(wg.shape, wg.dtype), jax.ShapeDtypeStruct((SUBLANES, LANES), _F32)),
        in_specs=[_HBM] + [_SEM] * n_sem + [_ANY], out_specs=(*[_SEM] * len(new), _HBM, _VMEM),
        input_output_aliases={0: len(new)},
        compiler_params=pltpu.CompilerParams(has_side_effects=_EFFECT),
    )(pltpu.with_memory_space_constraint(wg, pltpu.HBM), *sems, after)
    return outs


def _slab_copies(bufs, send_sems, recv_sems):
    x, y, c = _position()
    me = _dev_index(x, y, c)
    peers = [(_flip(x, k & 4), _flip(y, k & 2), _flip(c, k & 1)) for k in range(1, N_DEV)]
    return [pltpu.make_async_remote_copy(
        src_ref=buf.at[me], dst_ref=buf.at[me],
        send_sem=send_sems.at[(N_DEV - 1) * a + k], recv_sem=recv_sems.at[(N_DEV - 1) * a + k],
        device_id=peer, device_id_type=_MESH)
        for a, buf in enumerate(bufs) for k, peer in enumerate(peers)]


def _gather_slabs_start(bufs, after, name):
    n = len(bufs)
    n_copies = (N_DEV - 1) * n

    def body(*refs):
        send_sems, recv_sems, token = refs[n + 1], refs[n + 2], refs[-1]
        for cp in _slab_copies(refs[:n], send_sems, recv_sems):
            cp.start()
        token[...] = jnp.zeros_like(token)

    outs = pl.pallas_call(
        body, name=name,
        out_shape=(pltpu.SemaphoreType.DMA((n_copies,)), pltpu.SemaphoreType.DMA((n_copies,)),
                   *[pltpu.HBM(b.shape, b.dtype) for b in bufs], jax.ShapeDtypeStruct((SUBLANES, LANES), _F32)),
        in_specs=[_HBM] * n + [_ANY], out_specs=(_SEM, _SEM, *[_HBM] * n, _VMEM),
        input_output_aliases={i: 2 + i for i in range(n)},
        compiler_params=pltpu.CompilerParams(has_side_effects=_EFFECT),
    )(*[pltpu.with_memory_space_constraint(b, pltpu.HBM) for b in bufs], after)
    return outs[0], outs[1], list(outs[2:2 + n]), outs[-1]


def _gather_slabs_wait(send_sems, recv_sems, bufs, after, name):
    n = len(bufs)

    def body(*refs):
        for cp in _slab_copies(refs[:n], refs[n], refs[n + 1]):
            cp.wait_send()
            cp.wait_recv()

    outs = pl.pallas_call(
        body, name=name, out_shape=tuple(pltpu.HBM(b.shape, b.dtype) for b in bufs),
        in_specs=[_HBM] * n + [_SEM, _SEM, _ANY], out_specs=[_HBM] * n,
        input_output_aliases={i: i for i in range(n)},
        compiler_params=pltpu.CompilerParams(has_side_effects=_EFFECT),
    )(*bufs, send_sems, recv_sems, after)
    return list(outs)


def _chip_order(x, y):
    return [(x, y), (1 - x, y), (x, 1 - y), (1 - x, 1 - y)]


def _chip_copies(srcs, lands, send_sems, recv_sems):
    x, y, c = _position()
    return [pltpu.make_async_remote_copy(
        src_ref=srcs[a].at[k], dst_ref=lands[a].at[k],
        send_sem=send_sems.at[(N_CHIPS - 1) * a + k], recv_sem=recv_sems.at[(N_CHIPS - 1) * a + k],
        device_id=(*chip, c), device_id_type=_MESH)
        for a in range(len(srcs)) for k, chip in enumerate(_chip_order(x, y)[1:])]


def _sibling_copies(srcs, lands, send_sems, recv_sems):
    x, y, c = _position()
    return [pltpu.make_async_remote_copy(
        src_ref=srcs[a].at[s], dst_ref=lands[a].at[s],
        send_sem=send_sems.at[N_CHIPS * a + s], recv_sem=recv_sems.at[N_CHIPS * a + s],
        device_id=(x, y, 1 - c), device_id_type=_MESH)
        for a in range(len(srcs)) for s in range(N_CHIPS)]


def _sibling_copies_by_device(srcs, lands, send_sems, recv_sems):
    x, y, c = _position()
    return [pltpu.make_async_remote_copy(
        src_ref=srcs[a].at[_dev_index(*chip, 1 - c)], dst_ref=lands[a].at[s],
        send_sem=send_sems.at[N_CHIPS * a + s], recv_sem=recv_sems.at[N_CHIPS * a + s],
        device_id=(x, y, 1 - c), device_id_type=_MESH)
        for a in range(len(srcs)) for s, chip in enumerate(_chip_order(x, y))]


def _exchange_start(copies, sums, name, land_slabs=None):
    n = len(sums)
    land_shapes = [((land_slabs or s.shape[0]),) + s.shape[1:] for s in sums]
    n_copies = sum(shape[0] for shape in land_shapes)

    def body(*refs):
        srcs, lands = refs[:n], refs[n:2 * n]
        send_sems, recv_sems, token = refs[2 * n], refs[2 * n + 1], refs[-1]
        for cp in copies(srcs, lands, send_sems, recv_sems):
            cp.start()
        token[...] = jnp.zeros_like(token)

    hbm = [pltpu.HBM(s.shape, s.dtype) for s in sums] + [pltpu.HBM(shape, s.dtype) for shape, s in zip(land_shapes, sums)]
    outs = pl.pallas_call(
        body, name=name,
        out_shape=(pltpu.SemaphoreType.DMA((n_copies,)), pltpu.SemaphoreType.DMA((n_copies,)), *hbm,
                   jax.ShapeDtypeStruct((SUBLANES, LANES), _F32)),
        in_specs=[_HBM] * (2 * n), out_specs=(_SEM, _SEM, *[_HBM] * (2 * n), _VMEM),
        input_output_aliases={i: 2 + i for i in range(2 * n)},
        compiler_params=pltpu.CompilerParams(has_side_effects=_EFFECT),
    )(*[pltpu.with_memory_space_constraint(s, pltpu.HBM) for s in sums],
      *[pltpu.with_memory_space_constraint(lax.empty(shape, s.dtype), pltpu.HBM) for shape, s in zip(land_shapes, sums)])
    return outs[0], outs[1], list(outs[2:2 + n]), list(outs[2 + n:2 + 2 * n]), outs[-1]


def _exchange_wait(copies, send_sems, recv_sems, srcs, lands, after, name, with_sources=False):
    n = len(srcs)

    def body(*refs):
        for cp in copies(refs[:n], refs[n:2 * n], refs[2 * n], refs[2 * n + 1]):
            cp.wait_send()
            cp.wait_recv()

    hbm = [pltpu.HBM(s.shape, s.dtype) for s in (*srcs, *lands)]
    outs = pl.pallas_call(
        body, name=name, out_shape=tuple(hbm),
        in_specs=[_HBM] * (2 * n) + [_SEM, _SEM, _ANY], out_specs=[_HBM] * (2 * n),
        input_output_aliases={i: i for i in range(2 * n)},
        compiler_params=pltpu.CompilerParams(has_side_effects=_EFFECT),
    )(*srcs, *lands, send_sems, recv_sems, after)
    return list(outs) if with_sources else list(outs[n:])


def _matmul(a, b, *, ta=False, tb=False, out_dtype=_F32, tm=1024, tn=1024, tk=4608, col_slabs=None, after=None, name):
    kdim, m = a.shape if ta else a.shape[::-1]
    n = b.shape[0] if tb else b.shape[1]
    tm = _tile(m, tm, LANES if ta else BF16_ROWS)
    tn = n // col_slabs if col_slabs else _tile(n, tn, LANES)
    tk = _tile(kdim, tk, BF16_ROWS if ta else LANES)
    nk = kdim // tk
    dims = (((0 if ta else 1,), (1 if tb else 0,)), ((), ()))

    def body_whole_k(a_ref, b_ref, *rest):
        o_ref = rest[-1]
        part = lax.dot_general(a_ref[...], b_ref[...], dims, preferred_element_type=_F32)
        o_ref[...] = part.astype(out_dtype).reshape(o_ref.shape)

    def body_split_k(a_ref, b_ref, *rest):
        o_ref, acc_ref = rest[-2:]
        k = pl.program_id(2)

        @pl.when(k == 0)
        def _():
            acc_ref[...] = jnp.zeros_like(acc_ref)

        acc_ref[...] += lax.dot_general(a_ref[...], b_ref[...], dims, preferred_element_type=_F32)

        @pl.when(k == nk - 1)
        def _():
            o_ref[...] = acc_ref[...].astype(out_dtype).reshape(o_ref.shape)

    a_spec = pl.BlockSpec((tk, tm), lambda i, j, k: (k, i)) if ta else pl.BlockSpec((tm, tk), lambda i, j, k: (i, k))
    b_spec = pl.BlockSpec((tn, tk), lambda i, j, k: (j, k)) if tb else pl.BlockSpec((tk, tn), lambda i, j, k: (k, j))
    if col_slabs:
        out_spec = pl.BlockSpec((1, tm, tn), lambda i, j, k: (j, i, 0))
        out_shape = jax.ShapeDtypeStruct((col_slabs, m, tn), out_dtype)
    else:
        out_spec = pl.BlockSpec((tm, tn), lambda i, j, k: (i, j))
        out_shape = jax.ShapeDtypeStruct((m, n), out_dtype)
    extra = [] if after is None else [after]
    return pl.pallas_call(
        body_whole_k if nk == 1 else body_split_k, name=name, grid=(m // tm, n // tn, nk),
        in_specs=[a_spec, b_spec] + [pl.BlockSpec(t.shape, lambda i, j, k: (0, 0)) for t in extra],
        out_specs=out_spec, out_shape=out_shape,
        scratch_shapes=[] if nk == 1 else [pltpu.VMEM((tm, tn), _F32)],
        compiler_params=pltpu.CompilerParams(dimension_semantics=("parallel", "parallel", "arbitrary")),
    )(a, b, *extra)


def _proj_blocks(a, wg, dst, order_ids, first, count, width, after, name):
    m, kdim = a.shape
    tm = _tile(m, 1088, BF16_ROWS)

    def body(ids_ref, a_ref, w_ref, after_ref, dst_ref, o_ref):
        del ids_ref, after_ref, dst_ref
        o_ref[...] = jnp.dot(a_ref[...], w_ref[...], preferred_element_type=_F32)

    return pl.pallas_call(
        body, name=name,
        grid_spec=pltpu.PrefetchScalarGridSpec(
            num_scalar_prefetch=1, grid=(count, m // tm),
            in_specs=[pl.BlockSpec((tm, kdim), lambda j, i, ids: (i, 0)),
                      pl.BlockSpec((kdim, width), lambda j, i, ids: (0, ids[first + j])),
                      pl.BlockSpec(after.shape, lambda j, i, ids: (0, 0)), _ANY],
            out_specs=pl.BlockSpec((tm, width), lambda j, i, ids: (i, ids[first + j]))),
        out_shape=jax.ShapeDtypeStruct(dst.shape, dst.dtype),
        input_output_aliases={4: 0},
        compiler_params=pltpu.CompilerParams(dimension_semantics=("arbitrary", "arbitrary")),
    )(order_ids, a, wg, after, dst)


def _cast_into_columns(slab_ids, a, n_blocks, name):
    r, c = a.shape
    tr = _row_tile(r, c)

    def body(ids_ref, a_ref, o_ref):
        del ids_ref
        o_ref[...] = a_ref[...].astype(_BF16)

    return pl.pallas_call(
        body, name=name,
        grid_spec=pltpu.PrefetchScalarGridSpec(
            num_scalar_prefetch=1, grid=(r // tr,),
            in_specs=[pl.BlockSpec((tr, c), lambda i, ids: (i, 0))],
            out_specs=pl.BlockSpec((tr, c), lambda i, ids: (i, ids[0]))),
        out_shape=jax.ShapeDtypeStruct((r, n_blocks * c), _BF16),
        compiler_params=pltpu.CompilerParams(dimension_semantics=("parallel",)),
    )(slab_ids, a)


def _matmul_slabs(a, b, ids, width, name, after=None):
    kdim, m = a.shape
    n_slabs = ids.shape[0]
    tm = _tile(m, 1024, LANES)

    def body(ids_ref, a_ref, b_ref, *rest):
        del ids_ref
        rest[-1][0] = lax.dot_general(a_ref[...], b_ref[...], (((0,), (0,)), ((), ())), preferred_element_type=_F32)

    extra = [] if after is None else [after]
    return pl.pallas_call(
        body, name=name,
        grid_spec=pltpu.PrefetchScalarGridSpec(
            num_scalar_prefetch=1, grid=(m // tm, n_slabs),
            in_specs=[pl.BlockSpec((kdim, tm), lambda i, j, ids: (0, i)),
                      pl.BlockSpec((kdim, width), lambda i, j, ids: (0, ids[j]))]
            + [pl.BlockSpec(t.shape, lambda i, j, ids: (0, 0)) for t in extra],
            out_specs=pl.BlockSpec((1, tm, width), lambda i, j, ids: (j, i, 0))),
        out_shape=jax.ShapeDtypeStruct((n_slabs, m, width), _F32),
        compiler_params=pltpu.CompilerParams(dimension_semantics=("parallel", "parallel")),
    )(ids, a, b, *extra)


def _row_tile(rows, cols):
    return _tile(rows, max(BF16_ROWS, min(512, (1 << 19) // cols)), BF16_ROWS)


def _cast_bf16(a, name):
    r, c = a.shape
    tr = _row_tile(r, c)

    def body(a_ref, o_ref):
        o_ref[...] = a_ref[...].astype(_BF16)

    blk = pl.BlockSpec((tr, c), lambda i: (i, 0))
    return pl.pallas_call(
        body, name=name, grid=(r // tr,), in_specs=[blk], out_specs=blk,
        out_shape=jax.ShapeDtypeStruct((r, c), _BF16),
        compiler_params=pltpu.CompilerParams(dimension_semantics=("parallel",)),
    )(a)


def _cast_into_slab(slab_ids, a, name):
    r, c = a.shape
    tr = _row_tile(r, c)

    def body(ids_ref, a_ref, o_ref):
        del ids_ref
        o_ref[0] = a_ref[...].astype(_BF16)

    return pl.pallas_call(
        body, name=name,
        grid_spec=pltpu.PrefetchScalarGridSpec(
            num_scalar_prefetch=1, grid=(r // tr,),
            in_specs=[pl.BlockSpec((tr, c), lambda i, ids: (i, 0))],
            out_specs=pl.BlockSpec((1, tr, c), lambda i, ids: (ids[0], i, 0))),
        out_shape=jax.ShapeDtypeStruct((N_DEV, r, c), _BF16),
        compiler_params=pltpu.CompilerParams(dimension_semantics=("parallel",)),
    )(slab_ids, a)


def _pre_add(slab_ids, grad, got, name):
    _, r, c = grad.shape
    tr = _row_tile(r, c)

    def body(ids_ref, a_ref, b_ref, o_ref):
        del ids_ref
        o_ref[...] = (a_ref[...] + b_ref[...]).astype(_BF16)

    return pl.pallas_call(
        body, name=name,
        grid_spec=pltpu.PrefetchScalarGridSpec(
            num_scalar_prefetch=1, grid=(N_CHIPS - 1, r // tr),
            in_specs=[pl.BlockSpec((1, tr, c), lambda s, i, ids: (ids[s + 1], i, 0)),
                      pl.BlockSpec((1, tr, c), lambda s, i, ids: (s + 1, i, 0))],
            out_specs=pl.BlockSpec((1, tr, c), lambda s, i, ids: (s, i, 0))),
        out_shape=jax.ShapeDtypeStruct((N_CHIPS - 1, r, c), _BF16),
        compiler_params=pltpu.CompilerParams(dimension_semantics=("parallel", "parallel")),
    )(slab_ids, grad, got)


def _ada_forward(craw, w_shard, name):
    d, cols = w_shard.shape
    tk = _tile(d, 512, LANES)

    def body(c_ref, w_ref, o_ref):
        @pl.when(pl.program_id(0) == 0)
        def _():
            o_ref[...] = jnp.zeros_like(o_ref)

        o_ref[...] += jnp.dot(_silu(c_ref[...]).astype(_BF16), w_ref[...].astype(_BF16), preferred_element_type=_F32)

    return pl.pallas_call(
        body, name=name, grid=(d // tk,),
        in_specs=[pl.BlockSpec((craw.shape[0], tk), lambda k: (0, k)), pl.BlockSpec((tk, cols), lambda k: (k, 0))],
        out_specs=pl.BlockSpec((craw.shape[0], cols), lambda k: (0, 0)),
        out_shape=jax.ShapeDtypeStruct((craw.shape[0], cols), _F32),
        compiler_params=pltpu.CompilerParams(dimension_semantics=("arbitrary",)),
    )(craw, w_shard)


def _ada_backward(craw, dmod, w, m, v, name):
    d, cols = w.shape
    rows = craw.shape[0]
    tr = _tile(d, 256, LANES)

    def body(c_ref, dm_ref, w_ref, m_ref, v_ref, g_ref, dl_ref, nm_ref, nv_ref, dc_ref):
        act = _silu(c_ref[...]).astype(_BF16)
        dmb = dm_ref[...].astype(_BF16)
        wv = w_ref[...]
        g = lax.dot_general(act, dmb, (((0,), (0,)), ((), ())), preferred_element_type=_F32)
        delta, nm, nv = _adamw(wv, g, m_ref[...], v_ref[...])
        g_ref[...] = g
        dl_ref[...] = delta
        nm_ref[...] = nm
        nv_ref[...] = nv
        dc = lax.dot_general(dmb, wv.astype(_BF16), (((1,), (1,)), ((), ())), preferred_element_type=_F32)
        dc_ref[...] = jnp.broadcast_to(jnp.sum(dc[N_DEV:], axis=0, keepdims=True), dc_ref.shape)

    blk = pl.BlockSpec((tr, cols), lambda i: (i, 0))
    return pl.pallas_call(
        body, name=name, grid=(d // tr,),
        in_specs=[pl.BlockSpec((rows, tr), lambda i: (0, i)), pl.BlockSpec((rows, cols), lambda i: (0, 0)), blk, blk, blk],
        out_specs=[blk, blk, blk, blk, pl.BlockSpec((SUBLANES, tr), lambda i: (0, i))],
        out_shape=[jax.ShapeDtypeStruct((d, cols), _F32)] * 4 + [jax.ShapeDtypeStruct((SUBLANES, d), _F32)],
        compiler_params=pltpu.CompilerParams(dimension_semantics=("parallel",)),
    )(craw, dmod, w, m, v)


def _rms(xf):
    return lax.rsqrt(jnp.mean(xf * xf, axis=-1, keepdims=True) + EPS)


def _head_mean(v):
    hi = v.astype(_BF16)
    lo = (v - hi.astype(_F32)).astype(_BF16)
    ones = jnp.full((2 * HEAD_DIM, HEAD_DIM), 1.0 / HEAD_DIM, _BF16)
    return jnp.dot(jnp.concatenate([hi, lo], axis=1), ones, preferred_element_type=_F32)


def _prenorm(ctx, x, g_pre, mods, tr, name):
    l, d = ctx.shape
    n = x.shape[0]
    nbl = n // tr

    def body(ctx_ref, x_ref, g_ref, mod_ref, h_ref):
        def emit(src_ref):
            xf = src_ref[...]
            y = (xf * _rms(xf)) * g_ref[...]
            h_ref[...] = (y * (1.0 + mod_ref[0, 0:1, :]) + mod_ref[0, 1:2, :]).astype(_BF16)

        is_latent = pl.program_id(0) < nbl
        pl.when(is_latent)(lambda: emit(x_ref))
        pl.when(jnp.logical_not(is_latent))(lambda: emit(ctx_ref))

    return pl.pallas_call(
        body, name=name, grid=((l + n) // tr,),
        in_specs=[pl.BlockSpec((tr, d), lambda i: (jnp.maximum(i - nbl, 0), 0)),
                  pl.BlockSpec((tr, d), lambda i: (jnp.minimum(i, nbl - 1), 0)),
                  pl.BlockSpec((1, d), lambda i: (0, 0)),
                  pl.BlockSpec((1, SUBLANES, d), lambda i: ((i < nbl).astype(jnp.int32), 0, 0))],
        out_specs=pl.BlockSpec((tr, d), lambda i: (i, 0)),
        out_shape=jax.ShapeDtypeStruct((l + n, d), _BF16),
        compiler_params=pltpu.CompilerParams(dimension_semantics=("arbitrary",)),
    )(ctx, x, g_pre, mods)


def _prenorm_backward(dh, ctx, x, dxn, g_pre, mods, tr, name):
    l, d = ctx.shape
    n = x.shape[0]
    nbl = n // tr

    def body(dh_ref, ctx_ref, x_ref, dxn_ref, g_ref, mod_ref, gx_ref, dmod_ref, dg_ref):
        i = pl.program_id(0)

        @pl.when(i == 0)
        def _():
            dg_ref[...] = jnp.zeros_like(dg_ref)

        @pl.when(jnp.logical_or(i == 0, i == nbl))
        def _():
            dmod_ref[...] = jnp.zeros_like(dmod_ref)

        def emit(src_ref, latent):
            xf = src_ref[...]
            r = _rms(xf)
            xn = xf * r
            dhv = dh_ref[...]
            one_scale = 1.0 + mod_ref[0, 0:1, :]
            dmod_ref[0, 0:1, :] += jnp.sum(dhv * (xn * g_ref[...]), axis=0, keepdims=True)
            dmod_ref[0, 1:2, :] += jnp.sum(dhv, axis=0, keepdims=True)
            dyg = dhv * one_scale
            dg_ref[0:1, :] += jnp.sum(dyg * xn, axis=0, keepdims=True)
            if latent:
                dn = dyg * g_ref[...]
                gx_ref[...] = dxn_ref[...] + r * (dn - xn * jnp.mean(dn * xn, axis=-1, keepdims=True))

        pl.when(i < nbl)(lambda: emit(x_ref, True))
        pl.when(i >= nbl)(lambda: emit(ctx_ref, False))

    lat = pl.BlockSpec((tr, d), lambda i: (jnp.minimum(i, nbl - 1), 0))
    sel = pl.BlockSpec((1, SUBLANES, d), lambda i: ((i < nbl).astype(jnp.int32), 0, 0))
    return pl.pallas_call(
        body, name=name, grid=((l + n) // tr,),
        in_specs=[pl.BlockSpec((tr, d), lambda i: (i, 0)),
                  pl.BlockSpec((tr, d), lambda i: (jnp.maximum(i - nbl, 0), 0)),
                  lat, lat, pl.BlockSpec((1, d), lambda i: (0, 0)), sel],
        out_specs=[lat, sel, pl.BlockSpec((SUBLANES, d), lambda i: (0, 0))],
        out_shape=[jax.ShapeDtypeStruct((n, d), _F32), jax.ShapeDtypeStruct((2, SUBLANES, d), _F32),
                   jax.ShapeDtypeStruct((SUBLANES, d), _F32)],
        compiler_params=pltpu.CompilerParams(dimension_semantics=("arbitrary",)),
    )(dh, ctx, x, dxn, g_pre, mods)


def _rope_tables(l, n):
    rows = n // GRID_W
    row = jnp.repeat(jnp.arange(rows, dtype=_F32), GRID_W)
    col = jnp.tile(jnp.arange(GRID_W, dtype=_F32), rows)
    inv = ROPE_THETA ** (-jnp.arange(ROPE_PAIRS, dtype=_F32) / ROPE_PAIRS)
    ang_r, ang_c = row[:, None] * inv, col[:, None] * inv
    cr, sr, cc, sc = jnp.cos(ang_r), jnp.sin(ang_r), jnp.cos(ang_c), jnp.sin(ang_c)
    zero = jnp.zeros_like(sr)
    tc = jnp.concatenate([cr, cr, cc, cc], axis=-1)
    ta = jnp.concatenate([-sr, zero, -sc, zero], axis=-1)
    tb = jnp.concatenate([zero, sr, zero, sc], axis=-1)
    pad = lambda t, fill: jnp.concatenate([t, jnp.full((l, HEAD_DIM), fill, _F32)], axis=0)
    return pad(tc, 1.0), pad(ta, 0.0), pad(tb, 0.0)


def _rope(y, tc, ta, tb):
    return y * tc + pltpu.roll(y, HEAD_DIM - ROPE_PAIRS, 1) * ta + pltpu.roll(y, ROPE_PAIRS, 1) * tb


def _rope_transposed(dy, tc, ta, tb):
    return dy * tc + pltpu.roll(dy * ta, ROPE_PAIRS, 1) + pltpu.roll(dy * tb, HEAD_DIM - ROPE_PAIRS, 1)


def _qkv_post(proj, tables, g_q, g_k, heads, kv_heads, tr, name):
    t = proj.shape[0]
    aw, kw = heads * HEAD_DIM, kv_heads * HEAD_DIM
    w3 = aw + 2 * kw

    def body(p_ref, tc_ref, ta_ref, tb_ref, gq_ref, gk_ref, q_ref, k_ref, v_ref):
        tabs = (tc_ref[...], ta_ref[...], tb_ref[...])

        def norm_rope(col, gain):
            xh = p_ref[:, col:col + HEAD_DIM]
            return _rope((xh * lax.rsqrt(_head_mean(xh * xh) + EPS)) * gain, *tabs).astype(_BF16)

        for h in range(heads):
            q_ref[h] = norm_rope(h * HEAD_DIM, gq_ref[...])
        for h in range(kv_heads):
            k_ref[h] = norm_rope(aw + h * HEAD_DIM, gk_ref[...])
            v_ref[h] = p_ref[:, aw + kw + h * HEAD_DIM:aw + kw + (h + 1) * HEAD_DIM].astype(_BF16)

    tab = pl.BlockSpec((tr, HEAD_DIM), lambda i: (i, 0))
    gain = pl.BlockSpec((1, HEAD_DIM), lambda i: (0, 0))
    return pl.pallas_call(
        body, name=name, grid=(t // tr,),
        in_specs=[pl.BlockSpec((tr, w3), lambda i: (i, 0)), tab, tab, tab, gain, gain],
        out_specs=[pl.BlockSpec((heads, tr, HEAD_DIM), lambda i: (0, i, 0)),
                   pl.BlockSpec((kv_heads, tr, HEAD_DIM), lambda i: (0, i, 0)),
                   pl.BlockSpec((kv_heads, tr, HEAD_DIM), lambda i: (0, i, 0))],
        out_shape=[jax.ShapeDtypeStruct((heads, t, HEAD_DIM), _BF16),
                   jax.ShapeDtypeStruct((kv_heads, t, HEAD_DIM), _BF16),
                   jax.ShapeDtypeStruct((kv_heads, t, HEAD_DIM), _BF16)],
        compiler_params=pltpu.CompilerParams(dimension_semantics=("parallel",)),
    )(proj, *tables, g_q, g_k)


def _qkv_post_backward(proj, dq, dk, dv, tables, g_q, g_k, dproj, l, tr, name):
    t = proj.shape[0]
    heads, kv_heads = dq.shape[0], dk.shape[0]
    aw, kw = heads * HEAD_DIM, kv_heads * HEAD_DIM
    w3 = aw + 2 * kw
    nbl = (t - l) // tr

    def body(p_ref, dq_ref, dk_ref, dv_ref, tc_ref, ta_ref, tb_ref, gq_ref, gk_ref, dproj_ref, o_ref, dgq_ref, dgk_ref):
        del dproj_ref
        i = pl.program_id(0)

        @pl.when(i == 0)
        def _():
            dgq_ref[...] = jnp.zeros_like(dgq_ref)
            dgk_ref[...] = jnp.zeros_like(dgk_ref)

        tabs = (tc_ref[...], ta_ref[...], tb_ref[...])
        latent = i < nbl

        def back(col, dout, gain, dg_ref):
            xh = p_ref[:, col:col + HEAD_DIM]
            r = lax.rsqrt(_head_mean(xh * xh) + EPS)
            xn = xh * r
            dy = _rope_transposed(dout, *tabs)
            dg_ref[0:1, :] += jnp.sum(dy * xn, axis=0, keepdims=True)
            dn = dy * gain
            o_ref[:, col:col + HEAD_DIM] = (r * (dn - xn * _head_mean(dn * xn))).astype(_BF16)

        for h in range(heads):
            back(h * HEAD_DIM, jnp.where(latent, dq_ref[h], 0.0), gq_ref[...], dgq_ref)
        for h in range(kv_heads):
            back(aw + h * HEAD_DIM, dk_ref[h], gk_ref[...], dgk_ref)
            o_ref[:, aw + kw + h * HEAD_DIM:aw + kw + (h + 1) * HEAD_DIM] = dv_ref[h].astype(_BF16)

    tab = pl.BlockSpec((tr, HEAD_DIM), lambda i: (i, 0))
    gain = pl.BlockSpec((1, HEAD_DIM), lambda i: (0, 0))
    acc = pl.BlockSpec((SUBLANES, HEAD_DIM), lambda i: (0, 0))
    return pl.pallas_call(
        body, name=name, grid=(t // tr,),
        in_specs=[pl.BlockSpec((tr, w3), lambda i: (i, 0)),
                  pl.BlockSpec((heads, tr, HEAD_DIM), lambda i: (0, jnp.minimum(i, nbl - 1), 0)),
                  pl.BlockSpec((kv_heads, tr, HEAD_DIM), lambda i: (0, i, 0)),
                  pl.BlockSpec((kv_heads, tr, HEAD_DIM), lambda i: (0, i, 0)),
                  tab, tab, tab, gain, gain, _ANY],
        out_specs=[pl.BlockSpec((tr, w3), lambda i: (i, 0)), acc, acc],
        out_shape=[jax.ShapeDtypeStruct(dproj.shape, dproj.dtype), jax.ShapeDtypeStruct((SUBLANES, HEAD_DIM), _F32),
                   jax.ShapeDtypeStruct((SUBLANES, HEAD_DIM), _F32)],
        input_output_aliases={9: 0},
        compiler_params=pltpu.CompilerParams(dimension_semantics=("arbitrary",)),
    )(proj, dq, dk, dv, *tables, g_q, g_k, dproj)


def _zero_context_rows(dproj, l, w3, tr, name):
    t, iw = dproj.shape
    first = (t - l) // tr

    def body(dproj_ref, o_ref):
        del dproj_ref
        o_ref[...] = jnp.zeros_like(o_ref)

    return pl.pallas_call(
        body, name=name, grid=(l // tr, iw // w3 - 1),
        in_specs=[_ANY], out_specs=pl.BlockSpec((tr, w3), lambda i, j: (first + i, j + 1)),
        out_shape=jax.ShapeDtypeStruct(dproj.shape, dproj.dtype), input_output_aliases={0: 0},
        compiler_params=pltpu.CompilerParams(dimension_semantics=("parallel", "parallel")),
    )(dproj)


def _attention(q, k, v, proj, l, mix, tq, name):
    heads, t, _ = q.shape
    kv_heads = k.shape[0]
    n = t - l
    rows = GQA_GROUP * tq
    gw = GQA_GROUP * HEAD_DIM
    aw = heads * HEAD_DIM
    gate_col = (aw + 2 * kv_heads * HEAD_DIM) // gw

    def body(q_ref, k_ref, v_ref, g_ref, o_ref, y_ref, lse_ref):
        lane = lax.broadcasted_iota(jnp.int32, (tq, LANES), 1)
        lse_blk = jnp.zeros((tq, LANES), _F32)
        for first in range(0, GQA_GROUP, ATTN_SUB_HEADS):
            qs = q_ref[first:first + ATTN_SUB_HEADS].reshape(ATTN_SUB_HEADS * tq, HEAD_DIM)
            raw = lax.dot_general(qs, k_ref[0], (((1,), (1,)), ((), ())), preferred_element_type=_F32)
            m = jnp.max(raw, axis=-1, keepdims=True)
            p = jnp.exp2((raw - m) * (ATTN_SCALE * LOG2_E))
            denom = jnp.sum(p, axis=-1, keepdims=True)
            os_ = jnp.dot(p.astype(_BF16), v_ref[0], preferred_element_type=_F32) / denom
            lse_s = m * ATTN_SCALE + jnp.log(denom)
            for j in range(ATTN_SUB_HEADS):
                g = first + j
                og = os_[j * tq:(j + 1) * tq]
                cols = slice(g * HEAD_DIM, (g + 1) * HEAD_DIM)
                o_ref[:, cols] = og
                y_ref[:, cols] = (og * _silu(g_ref[:, cols])).astype(_BF16)
                lse_blk = jnp.where(lane == g, lse_s[j * tq:(j + 1) * tq], lse_blk)
        lse_ref[0] = lse_blk

    return pl.pallas_call(
        body, name=name, grid=(kv_heads, n // tq),
        in_specs=[pl.BlockSpec((GQA_GROUP, tq, HEAD_DIM), lambda h, i: (h, i, 0)),
                  pl.BlockSpec((1, t, HEAD_DIM), lambda h, i: (h, 0, 0)),
                  pl.BlockSpec((1, t, HEAD_DIM), lambda h, i: (h, 0, 0)),
                  pl.BlockSpec((tq, gw), lambda h, i: (i, gate_col + h))],
        out_specs=[pl.BlockSpec((tq, gw), lambda h, i: (i, h)),
                   pl.BlockSpec((tq, gw), lambda h, i: (i, h)),
                   pl.BlockSpec((1, tq, LANES), lambda h, i: (h, i, 0))],
        out_shape=[jax.ShapeDtypeStruct((n, aw), _F32), jax.ShapeDtypeStruct((n, mix), _BF16),
                   jax.ShapeDtypeStruct((kv_heads, n, LANES), _F32)],
        compiler_params=pltpu.CompilerParams(dimension_semantics=("parallel", "parallel")),
    )(q, k, v, proj)


def _attention_backward(q, k, v, attn_o, dy, proj, lse, after, dproj, l, tq, name):
    heads, t, _ = q.shape
    kv_heads = k.shape[0]
    n = t - l
    rows = GQA_GROUP * tq
    gw = GQA_GROUP * HEAD_DIM
    aw = heads * HEAD_DIM
    gate_col = (aw + 2 * kv_heads * HEAD_DIM) // gw
    n_parts = next(p for p in (ATTN_KEY_PARTS, 2, 1) if t % (p * BF16_ROWS) == 0)
    part = t // n_parts

    def body(q_ref, k_ref, v_ref, o_ref, dy_ref, g_ref, lse_ref, after_ref, dproj_ref, dq_ref, dg_ref, dk_ref, dv_ref):
        del after_ref, dproj_ref

        @pl.when(pl.program_id(1) == 0)
        def _():
            dk_ref[...] = jnp.zeros_like(dk_ref)
            dv_ref[...] = jnp.zeros_like(dv_ref)

        lse_blk = lse_ref[0]
        for first in range(0, GQA_GROUP, ATTN_BWD_SUB_HEADS):
            qs = q_ref[first:first + ATTN_BWD_SUB_HEADS].reshape(ATTN_BWD_SUB_HEADS * tq, HEAD_DIM)
            do_parts, delta_parts, lse_parts = [], [], []
            for g in range(first, first + ATTN_BWD_SUB_HEADS):
                cols = slice(g * HEAD_DIM, (g + 1) * HEAD_DIM)
                gate, og, dyg = g_ref[:, cols], o_ref[:, cols], dy_ref[:, cols]
                dog = dyg * _silu(gate)
                dg_ref[:, cols] = (dyg * og * _silu_grad(gate)).astype(_BF16)
                do_parts.append(dog)
                delta_parts.append(jnp.sum(dog * og, axis=-1, keepdims=True))
                lse_parts.append(lse_blk[:, g:g + 1])
            dos = jnp.concatenate(do_parts, axis=0).astype(_BF16)
            delta = jnp.concatenate(delta_parts, axis=0)
            lse2 = jnp.concatenate(lse_parts, axis=0) * LOG2_E
            dqs = jnp.zeros((ATTN_BWD_SUB_HEADS * tq, HEAD_DIM), _F32)
            for part_i in range(n_parts):
                keys = slice(part_i * part, (part_i + 1) * part)
                ks, vs = k_ref[0, keys, :], v_ref[0, keys, :]
                raw = lax.dot_general(qs, ks, (((1,), (1,)), ((), ())), preferred_element_type=_F32)
                p = jnp.exp2(raw * (ATTN_SCALE * LOG2_E) - lse2)
                dp = lax.dot_general(dos, vs, (((1,), (1,)), ((), ())), preferred_element_type=_F32)
                ds = (p * (dp - delta)).astype(_BF16)
                dqs = dqs + jnp.dot(ds, ks, preferred_element_type=_F32)
                dk_ref[0, keys, :] += ATTN_SCALE * lax.dot_general(
                    ds, qs, (((0,), (0,)), ((), ())), preferred_element_type=_F32)
                dv_ref[0, keys, :] += lax.dot_general(
                    p.astype(_BF16), dos, (((0,), (0,)), ((), ())), preferred_element_type=_F32)
            dq_ref[first:first + ATTN_BWD_SUB_HEADS] = (ATTN_SCALE * dqs).reshape(ATTN_BWD_SUB_HEADS, tq, HEAD_DIM)

    kv_spec = pl.BlockSpec((1, t, HEAD_DIM), lambda h, i: (h, 0, 0))
    tok = pl.BlockSpec((tq, gw), lambda h, i: (i, h))
    gate = pl.BlockSpec((tq, gw), lambda h, i: (i, gate_col + h))
    return pl.pallas_call(
        body, name=name, grid=(kv_heads, n // tq),
        in_specs=[pl.BlockSpec((GQA_GROUP, tq, HEAD_DIM), lambda h, i: (h, i, 0)), kv_spec, kv_spec,
                  tok, tok, gate, pl.BlockSpec((1, tq, LANES), lambda h, i: (h, i, 0)),
                  pl.BlockSpec(after.shape, lambda h, i: (0, 0)), _ANY],
        out_specs=[pl.BlockSpec((GQA_GROUP, tq, HEAD_DIM), lambda h, i: (h, i, 0)), gate, kv_spec, kv_spec],
        out_shape=[jax.ShapeDtypeStruct((heads, n, HEAD_DIM), _F32), jax.ShapeDtypeStruct(dproj.shape, dproj.dtype),
                   jax.ShapeDtypeStruct((kv_heads, t, HEAD_DIM), _F32), jax.ShapeDtypeStruct((kv_heads, t, HEAD_DIM), _F32)],
        input_output_aliases={8: 1},
        compiler_params=pltpu.CompilerParams(dimension_semantics=("parallel", "arbitrary")),
    )(q, k, v, attn_o, dy, proj, lse, after, dproj)


def _halo_specs(tp, width, col, row_off, total_rows):
    per = tp // POOL_HALO
    first = row_off // POOL_HALO
    last = total_rows // POOL_HALO - 1
    return [pl.BlockSpec((tp, width), lambda i: (i + row_off // tp, col)),
            pl.BlockSpec((POOL_HALO, width), lambda i: (jnp.maximum(first + i * per - 1, 0), col)),
            pl.BlockSpec((POOL_HALO, width), lambda i: (jnp.minimum(first + (i + 1) * per, last), col))]


def _with_halo(cur, prev, nxt, t0, n):
    tp = cur.shape[0]
    r8 = lax.broadcasted_iota(jnp.int32, (POOL_HALO, 1), 0)
    prev = jnp.where(t0 - POOL_HALO + r8 >= 0, prev, 0.0)
    nxt = jnp.where(t0 + tp + r8 < n, nxt, 0.0)
    return jnp.concatenate([prev, cur, nxt], axis=0)


def _shift_rows(a, s):
    return pltpu.roll(a, s % a.shape[0], 0)


def _window_sum(e, w, mirrored):
    a = e + _shift_rows(e, -1 if mirrored else 1)
    s = 1
    while 2 * s < w:
        a = _shift_rows(a, s) + _shift_rows(a, -s)
        s *= 2
    return a


def _window_count(t, w, n):
    half = w // 2
    return (jnp.minimum(t + half, n) - jnp.maximum(t - half, 0)).astype(_F32)


def _pool_forward(gi, proj, y, pool_w, pool_scale, l, heads, kv_heads, tp, name):
    t = proj.shape[0]
    n = t - l
    pg = pool_w.shape[-1]
    w = POOL_WINDOWS[gi]
    aw, kw = heads * HEAD_DIM, kv_heads * HEAD_DIM
    u_col = (2 * aw + 2 * kw) // pg + gi
    gate_col = (2 * aw + 2 * kw + len(POOL_WINDOWS) * pg) // pg + gi

    def body(u_ref, up_ref, un_ref, g_ref, w_ref, sc_ref, y_in_ref, y_ref, raw_ref, d_ref):
        del y_in_ref
        t0 = pl.program_id(0) * tp
        cur = u_ref[...]
        win = _window_sum(_with_halo(cur, up_ref[...], un_ref[...], t0, n), w, False)[POOL_HALO:POOL_HALO + tp]
        tok = t0 + lax.broadcasted_iota(jnp.int32, (tp, 1), 0)
        d = (win / _window_count(tok, w, n) - cur).astype(_BF16)
        raw = jnp.dot(d, w_ref[...].reshape(pg, pg), preferred_element_type=_F32)
        d_ref[...] = d
        raw_ref[...] = raw
        y_ref[...] = ((raw * sc_ref[...]) * _silu(g_ref[...])).astype(_BF16)

    blk = pl.BlockSpec((tp, pg), lambda i: (i, 0))
    return pl.pallas_call(
        body, name=name, grid=(n // tp,),
        in_specs=_halo_specs(tp, pg, u_col, 0, t) + [
            pl.BlockSpec((tp, pg), lambda i: (i, gate_col)),
            pl.BlockSpec((N_DEV, 1, pg // N_DEV, pg), lambda i: (0, gi, 0, 0)),
            pl.BlockSpec((1, pg), lambda i: (0, gi)), _ANY],
        out_specs=[pl.BlockSpec((tp, pg), lambda i: (i, aw // pg + gi)), blk, blk],
        out_shape=[jax.ShapeDtypeStruct(y.shape, y.dtype), jax.ShapeDtypeStruct((n, pg), _F32),
                   jax.ShapeDtypeStruct((n, pg), _BF16)],
        input_output_aliases={6: 0},
        compiler_params=pltpu.CompilerParams(dimension_semantics=("arbitrary",)),
    )(proj, proj, proj, proj, pool_w, pool_scale, y)


def _pool_backward_gate(gi, dy, proj, raw, pool_w, pool_scale, dproj, l, heads, kv_heads, tp, name):
    n, pg = raw.shape
    aw, kw = heads * HEAD_DIM, kv_heads * HEAD_DIM
    gate_col = (2 * aw + 2 * kw + len(POOL_WINDOWS) * pg) // pg + gi

    def body(dy_ref, g_ref, raw_ref, w_ref, sc_ref, dproj_ref, dg_ref, dr_ref, dd_ref, ds_ref):
        del dproj_ref

        @pl.when(pl.program_id(0) == 0)
        def _():
            ds_ref[...] = jnp.zeros_like(ds_ref)

        gate, rawv, dyv, scale = g_ref[...], raw_ref[...], dy_ref[...], sc_ref[...]
        dpool = dyv * _silu(gate)
        dg_ref[...] = (dyv * (rawv * scale) * _silu_grad(gate)).astype(_BF16)
        ds_ref[0:1, :] += jnp.sum(dpool * rawv, axis=0, keepdims=True)
        draw = (dpool * scale).astype(_BF16)
        dr_ref[...] = draw
        dd_ref[...] = lax.dot_general(
            draw, w_ref[...].reshape(pg, pg), (((1,), (1,)), ((), ())), preferred_element_type=_F32)

    blk = pl.BlockSpec((tp, pg), lambda i: (i, 0))
    gate = pl.BlockSpec((tp, pg), lambda i: (i, gate_col))
    return pl.pallas_call(
        body, name=name, grid=(n // tp,),
        in_specs=[pl.BlockSpec((tp, pg), lambda i: (i, aw // pg + gi)), gate, blk,
                  pl.BlockSpec((N_DEV, 1, pg // N_DEV, pg), lambda i: (0, gi, 0, 0)),
                  pl.BlockSpec((1, pg), lambda i: (0, gi)), _ANY],
        out_specs=[gate, blk, blk, pl.BlockSpec((SUBLANES, pg), lambda i: (0, 0))],
        out_shape=[jax.ShapeDtypeStruct(dproj.shape, dproj.dtype), jax.ShapeDtypeStruct((n, pg), _BF16),
                   jax.ShapeDtypeStruct((n, pg), _F32), jax.ShapeDtypeStruct((SUBLANES, pg), _F32)],
        input_output_aliases={5: 0},
        compiler_params=pltpu.CompilerParams(dimension_semantics=("arbitrary",)),
    )(dy, proj, raw, pool_w, pool_scale, dproj)


def _pool_backward_window(gi, dd, dproj, l, col, tp, name):
    n, pg = dd.shape
    w = POOL_WINDOWS[gi]

    def body(c_ref, p_ref, n_ref, dproj_ref, du_ref):
        del dproj_ref
        t0 = pl.program_id(0) * tp
        cur = c_ref[...]
        e = _with_halo(cur, p_ref[...], n_ref[...], t0, n)
        tok = t0 - POOL_HALO + lax.broadcasted_iota(jnp.int32, (tp + 2 * POOL_HALO, 1), 0)
        e = e / jnp.maximum(_window_count(tok, w, n), 1.0)
        du_ref[...] = (_window_sum(e, w, True)[POOL_HALO:POOL_HALO + tp] - cur).astype(_BF16)

    return pl.pallas_call(
        body, name=name, grid=(n // tp,),
        in_specs=_halo_specs(tp, pg, 0, 0, n) + [_ANY],
        out_specs=pl.BlockSpec((tp, pg), lambda i: (i, col)),
        out_shape=jax.ShapeDtypeStruct(dproj.shape, dproj.dtype), input_output_aliases={3: 0},
        compiler_params=pltpu.CompilerParams(dimension_semantics=("arbitrary",)),
    )(dd, dd, dd, dproj)


def _post(out, x, target, gate, g_post, tr, name):
    n, d = out.shape

    def body(o_ref, x_ref, t_ref, gate_ref, g_ref, dxn_ref, do_ref, dgate_ref, dg_ref, loss_ref):
        @pl.when(pl.program_id(0) == 0)
        def _():
            dgate_ref[...] = jnp.zeros_like(dgate_ref)
            dg_ref[...] = jnp.zeros_like(dg_ref)
            loss_ref[...] = jnp.zeros_like(loss_ref)

        ov = o_ref[...]
        r = _rms(ov)
        on = ov * r
        normed = on * g_ref[...]
        err = (x_ref[...] + gate_ref[...] * normed) - t_ref[...]
        loss_ref[...] += jnp.sum(err * err)
        dxn = err / d
        dxn_ref[...] = dxn
        dgate_ref[0:1, :] += jnp.sum(dxn * normed, axis=0, keepdims=True)
        dr = dxn * gate_ref[...]
        dg_ref[0:1, :] += jnp.sum(dr * on, axis=0, keepdims=True)
        dn = dr * g_ref[...]
        do_ref[...] = (r * (dn - on * jnp.mean(dn * on, axis=-1, keepdims=True))).astype(_BF16)

    blk = pl.BlockSpec((tr, d), lambda i: (i, 0))
    vec = pl.BlockSpec((1, d), lambda i: (0, 0))
    acc = pl.BlockSpec((SUBLANES, d), lambda i: (0, 0))
    return pl.pallas_call(
        body, name=name, grid=(n // tr,),
        in_specs=[blk, blk, blk, vec, vec],
        out_specs=[blk, blk, acc, acc, pl.BlockSpec((SUBLANES, LANES), lambda i: (0, 0))],
        out_shape=[jax.ShapeDtypeStruct((n, d), _F32), jax.ShapeDtypeStruct((n, d), _BF16),
                   jax.ShapeDtypeStruct((SUBLANES, d), _F32), jax.ShapeDtypeStruct((SUBLANES, d), _F32),
                   jax.ShapeDtypeStruct((SUBLANES, LANES), _F32)],
        compiler_params=pltpu.CompilerParams(dimension_semantics=("arbitrary",)),
    )(out, x, target, gate, g_post)


def _adam_sharded(slab_ids, grad, got, far, w, m, v, name):
    r, c = w.shape
    tr = _tile(r, max(BF16_ROWS, min(256, (1 << 18) // c)), BF16_ROWS)

    def body(ids_ref, own_ref, got_ref, far_ref, w_ref, m_ref, v_ref, g_ref, dl_ref, nm_ref, nv_ref):
        del ids_ref
        g = own_ref[0] + got_ref[0]
        for k in range(N_CHIPS - 1):
            g = g + far_ref[k].astype(_F32)
        delta, nm, nv = _adamw(w_ref[...], g, m_ref[...], v_ref[...])
        g_ref[...] = g
        dl_ref[...] = delta
        nm_ref[...] = nm
        nv_ref[...] = nv

    blk = pl.BlockSpec((tr, c), lambda i, ids: (i, 0))
    return pl.pallas_call(
        body, name=name,
        grid_spec=pltpu.PrefetchScalarGridSpec(
            num_scalar_prefetch=1, grid=(r // tr,),
            in_specs=[pl.BlockSpec((1, tr, c), lambda i, ids: (ids[0], i, 0)),
                      pl.BlockSpec((1, tr, c), lambda i, ids: (0, i, 0)),
                      pl.BlockSpec((N_CHIPS - 1, tr, c), lambda i, ids: (0, i, 0)), blk, blk, blk],
            out_specs=[blk] * 4),
        out_shape=[jax.ShapeDtypeStruct((r, c), _F32)] * 4,
        compiler_params=pltpu.CompilerParams(dimension_semantics=("parallel",)),
    )(slab_ids, grad, got, far, w, m, v)


def _adam_replicated(parts, extra, through_silu, w, m, v, name):
    def body(p_ref, e_ref, s_ref, w_ref, m_ref, v_ref, g_ref, dl_ref, nm_ref, nv_ref):
        total = p_ref[0] + e_ref[0]
        for dev in range(1, N_DEV):
            total = total + (p_ref[dev] + e_ref[dev])
        g = jnp.where(s_ref[...] > 0.5, total * _silu_grad(w_ref[...]), total)
        delta, nm, nv = _adamw(w_ref[...], g, m_ref[...], v_ref[...])
        g_ref[...] = g
        dl_ref[...] = delta
        nm_ref[...] = nm
        nv_ref[...] = nv

    return pl.pallas_call(
        body, name=name, in_specs=[_VMEM] * 6, out_specs=[_VMEM] * 4,
        out_shape=[jax.ShapeDtypeStruct(w.shape, _F32)] * 4,
    )(parts, extra, through_silu, w, m, v)


def _as_rows(vec):
    size = vec.shape[0]
    padded = -(-size // (SUBLANES * LANES)) * SUBLANES * LANES
    return jnp.pad(vec, (0, padded - size)).reshape(padded // LANES, LANES)


def kernel(x, c, ctx, c_ctx, w_ada, b_ada, norm_pre, norm_post, w_in, q_norm, k_norm, pool_w, pool_scale, w_out, loss_target, m_c_ctx, m_w_ada, m_b_ada, m_norm_pre, m_norm_post, m_w_in, m_q_norm, m_k_norm, m_pool_w, m_pool_scale, m_w_out, v_c_ctx, v_w_ada, v_b_ada, v_norm_pre, v_norm_post, v_w_in, v_q_norm, v_k_norm, v_pool_w, v_pool_scale, v_w_out):
    me = _dev_index(*_position())
    x2, ctx2, target = x[0], ctx[0], loss_target[0]
    n, d = x2.shape
    l = ctx2.shape[0]
    t = l + n
    aw = d // 2
    heads = aw // HEAD_DIM
    kv_heads = heads // GQA_GROUP
    kw = kv_heads * HEAD_DIM
    n_groups = len(POOL_WINDOWS)
    pg = (d - aw) // n_groups
    mix = d
    tr = _tile(l, 128, BF16_ROWS)
    tq = _tile(l, 128, BF16_ROWS)
    tp = _tile(n, 1024, POOL_HALO)

    xi, yi, ci = _position()
    slab_ids = jnp.stack([_dev_index(*chip, ci) for chip in _chip_order(xi, yi)]).astype(jnp.int32)

    cw = w_in.shape[-1]
    wg = _cast_into_columns(slab_ids, w_in[0], N_DEV, "cast_w_in")
    late = [_cast_into_slab(slab_ids, w_out[0], "cast_w_out"),
            _cast_into_slab(slab_ids, pool_w[0].reshape(-1, pg), "cast_pool_w")]

    c_all = _all_gather_small(_as_rows(c[0]), "gather_c").reshape(N_DEV, -1)[:, :d]
    craw = jnp.concatenate([c_all, jnp.broadcast_to(c_ctx[None], (N_DEV, d))], axis=0)
    ada = _ada_forward(craw, w_ada[0], "ada_forward")
    ada_all = _all_gather_small(ada, "gather_ada")
    mod_all = ada_all.transpose(1, 0, 2).reshape(ada.shape[0], -1) + b_ada[0]
    mod = lax.dynamic_index_in_dim(mod_all, me, 0, keepdims=False)
    mod_c = mod_all[N_DEV]
    shift, scale, gate = mod[:d], mod[d:2 * d], mod[2 * d:]
    zeros6 = jnp.zeros((SUBLANES - 2, d), _F32)
    mods = jnp.stack([jnp.concatenate([mod_c[None, d:2 * d], mod_c[None, :d], zeros6], axis=0),
                      jnp.concatenate([scale[None], shift[None], zeros6], axis=0)])

    a_s, a_r, wg, tok = _w_in_hop(wg, cw, [], "a", ada_all, "gather_w_in_a")
    h_all = _prenorm(ctx2, x2, norm_pre, mods + tok[0, 0], tr, "prenorm")
    order_ids = jnp.stack([_dev_index(*dev) for dev in _w_in_order(xi, yi, ci)]).astype(jnp.int32)
    proj = lax.empty((t, N_DEV * cw), _F32)
    proj = _proj_blocks(h_all, wg, proj, order_ids, 0, 1, cw, tok, "proj_0")
    b_s, b_r, wg, tok = _w_in_hop(wg, cw, [("a", a_s, a_r, [0, 1], [])], "b", proj, "gather_w_in_b")
    proj = _proj_blocks(h_all, wg, proj, order_ids, 1, 2, cw, tok, "proj_1")
    c_s, c_r, wg, tok = _w_in_hop(wg, cw, [("b", b_s, b_r, [2, 0], [])], "c", proj, "gather_w_in_c")
    proj = _proj_blocks(h_all, wg, proj, order_ids, 3, 2, cw, tok, "proj_2")
    d_s, d_r, wg, tok = _w_in_hop(
        wg, cw, [("c", c_s, c_r, [0], []), ("b", b_s, b_r, [1], [])], "d", proj, "gather_w_in_d")
    proj = _proj_blocks(h_all, wg, proj, order_ids, 5, 2, cw, tok, "proj_3")
    w_in_g, tok = _w_in_hop(
        wg, cw, [("d", d_s, d_r, [0], [0]), ("a", a_s, a_r, [], [0, 1]), ("b", b_s, b_r, [], [0, 1, 2]),
                 ("c", c_s, c_r, [], [0])], None, proj, "gather_w_in_end")
    flight_w = _gather_slabs_start(late, w_in_g, "gather_late_start")
    proj = _proj_blocks(h_all, w_in_g, proj, order_ids, 7, 1, cw, flight_w[-1], "proj_4")
    tables = _rope_tables(l, n)
    q, k, v = _qkv_post(proj, tables, q_norm, k_norm, heads, kv_heads, tr, "qkv_post")
    attn_o, y, lse = _attention(q, k, v, proj, l, mix, _tile(l, 256, BF16_ROWS), "attention")
    w_out_g8, pool_g8 = _gather_slabs_wait(*flight_w[:3], attn_o, "gather_late_wait")
    w_out_g = w_out_g8.reshape(mix, d)
    pool_g = pool_g8.reshape(N_DEV, n_groups, pg // N_DEV, pg)
    raws, ds = [], []
    for gi in range(n_groups):
        y, raw, dsave = _pool_forward(gi, proj, y, pool_g, pool_scale, l, heads, kv_heads, tp, f"pool_forward_{gi}")
        raws.append(raw)
        ds.append(dsave)
    out = _matmul(y, w_out_g, name="out_proj")
    dxn, dout, dgate8, dgpost8, loss8 = _post(out, x2, target, gate[None], norm_post, tr, "post")

    gw_out = _matmul(y, dout, ta=True, name="grad_w_out").reshape(N_DEV, mix // N_DEV, d)
    flight_so = _exchange_start(_sibling_copies_by_device, [gw_out], "exchange_sibling_start_w_out", land_slabs=N_CHIPS)
    dy = _matmul(dout, w_out_g, tb=True, after=flight_so[-1], name="d_y")
    gw_out, got_out = _exchange_wait(
        _sibling_copies_by_device, *flight_so[:4], dy, "exchange_sibling_wait_w_out", with_sources=True)
    sum_out = _pre_add(slab_ids, gw_out, got_out, "pre_add_w_out")
    flight_out = _exchange_start(_chip_copies, [sum_out], "exchange_chips_start_w_out")
    w3 = aw + 2 * kw
    dq, dproj, dk, dv = _attention_backward(
        q, k, v, attn_o, dy, proj, lse, flight_out[-1], lax.empty(proj.shape, _BF16), l, tq, "attention_backward")
    dproj, dgq8, dgk8 = _qkv_post_backward(proj, dq, dk, dv, tables, q_norm, k_norm, dproj, l, tr, "qkv_post_backward")
    dproj = _zero_context_rows(dproj, l, w3, tr, "zero_context_rows")
    gpw, dps8 = [], []
    for gi in range(n_groups):
        dproj, draw, dd, dps = _pool_backward_gate(
            gi, dy, proj, raws[gi], pool_g, pool_scale, dproj, l, heads, kv_heads, tp, f"pool_backward_gate_{gi}")
        dproj = _pool_backward_window(gi, dd, dproj, l, (w3 + aw) // pg + gi, tp, f"pool_backward_window_{gi}")
        dps8.append(dps)
        gpw.append(_matmul(ds[gi], draw, ta=True, name=f"grad_pool_w_{gi}"))
    cw = w_in.shape[-1]
    other_ids = jnp.stack([_dev_index(*chip, 1 - ci) for chip in _chip_order(xi, yi)]).astype(jnp.int32)
    chip_slabs = jnp.arange(N_CHIPS, dtype=jnp.int32)
    pr = pool_w.shape[2]
    gpw8 = jnp.stack(gpw).reshape(n_groups, N_DEV, pr, pg).transpose(1, 0, 2, 3).reshape(N_DEV, n_groups * pr, pg)
    give_in = _matmul_slabs(h_all, dproj, other_ids, cw, "grad_w_in_sibling")
    flight_sib = _exchange_start(_sibling_copies, [give_in, jnp.take(gpw8, other_ids, axis=0)], "exchange_sibling_start")
    gw_in = _matmul_slabs(h_all, dproj, slab_ids, cw, "grad_w_in_own", after=flight_sib[-1])
    gpw_own = jnp.take(gpw8, slab_ids, axis=0)
    got_in, got_pw = _exchange_wait(_sibling_copies, *flight_sib[:4], gw_in, "exchange_sibling_wait")
    sums_in = [_pre_add(chip_slabs, gw_in, got_in, "pre_add_w_in"), _pre_add(chip_slabs, gpw_own, got_pw, "pre_add_pool_w")]
    flight_in = _exchange_start(_chip_copies, sums_in, "exchange_chips_start_w_in")
    dh = _matmul(dproj, w_in_g, tb=True, tm=1088, tk=3072, after=flight_in[-1], name="d_h")
    grad_x, dmods, dgpre8 = _prenorm_backward(dh, ctx2, x2, dxn, norm_pre, mods, tr, "prenorm_backward")

    dmod_lat = jnp.concatenate([dmods[1, 1], dmods[1, 0], dgate8[0]])
    dmod_ctx = jnp.concatenate([dmods[0, 1], dmods[0, 0], jnp.zeros((d,), _F32)])
    small = jnp.concatenate([dmod_lat, dmod_ctx, dgpre8[0], dgpost8[0], dgq8[0], dgk8[0]] + [p[0] for p in dps8]
                            + [loss8[0, :1]])
    gathered = _all_gather_small(_as_rows(small), "gather_small").reshape(N_DEV, -1)
    o = 0
    take = lambda size: (gathered[:, o:o + size], o + size)
    g_mod, o = take(3 * d)
    g_modc, o = take(3 * d)
    g_pre, o = take(d)
    g_post, o = take(d)
    g_q, o = take(HEAD_DIM)
    g_k, o = take(HEAD_DIM)
    g_ps, o = take(n_groups * pg)
    g_loss, o = take(1)
    cols = w_ada.shape[-1]
    mine = lambda a: lax.dynamic_slice_in_dim(a, me * cols, cols, axis=1)
    dmod_rows = jnp.concatenate([mine(g_mod), mine(g_modc)], axis=0)
    g_wada, dl_wada, nm_wada, nv_wada, dcact = _ada_backward(craw, dmod_rows, w_ada[0], m_w_ada[0], v_w_ada[0], "ada_backward")
    dcc = _all_gather_small(_as_rows(dcact[0]), "gather_dcc").reshape(N_DEV, -1)[:, :d]

    sizes = [d, 3 * d, d, d, HEAD_DIM, HEAD_DIM, n_groups * pg]
    def pack(parts):
        rows = jnp.concatenate(parts, axis=1)
        padded = -(-rows.shape[1] // (SUBLANES * LANES)) * SUBLANES * LANES
        return jnp.pad(rows, ((0, 0), (0, padded - rows.shape[1]))).reshape(N_DEV, padded // LANES, LANES)

    zero = lambda size: jnp.zeros((N_DEV, size), _F32)
    parts = pack([dcc, g_mod, g_pre, g_post, g_q, g_k, g_ps])
    extra = pack([zero(d), g_modc, zero(d), zero(d), zero(HEAD_DIM), zero(HEAD_DIM), zero(n_groups * pg)])
    through_silu = _as_rows(jnp.concatenate([jnp.ones((d,), _F32), jnp.zeros((sum(sizes[1:]),), _F32)]))
    cat = lambda items: _as_rows(jnp.concatenate([a.reshape(-1) for a in items]))
    ws = [c_ctx, b_ada, norm_pre, norm_post, q_norm, k_norm, pool_scale]
    ms = [m_c_ctx, m_b_ada, m_norm_pre, m_norm_post, m_q_norm, m_k_norm, m_pool_scale]
    vs = [v_c_ctx, v_b_ada, v_norm_pre, v_norm_post, v_q_norm, v_k_norm, v_pool_scale]
    rep = _adam_replicated(parts, extra, through_silu, cat(ws), cat(ms), cat(vs), "adam_replicated")

    def split(packed):
        flat_, outs, at = packed.reshape(-1), [], 0
        for w, size in zip(ws, sizes):
            outs.append(flat_[at:at + size].reshape(w.shape))
            at += size
        return outs

    g_rep, dl_rep, nm_rep, nv_rep = [split(r) for r in rep]

    far_out = _exchange_wait(_chip_copies, *flight_out[:4], grad_x, "exchange_chips_wait_w_out")[0]
    far_in, far_pw = _exchange_wait(_chip_copies, *flight_in[:4], rep[0], "exchange_chips_wait_w_in")
    two = lambda a: a.reshape(-1, a.shape[-1])
    sharded = []
    for ids, g, got, far, w, m, v_, name in zip(
            (chip_slabs, slab_ids, chip_slabs), (gw_in, gw_out, gpw_own), (got_in, got_out, got_pw),
            (far_in, far_out, far_pw), (w_in, w_out, pool_w), (m_w_in, m_w_out, m_pool_w),
            (v_w_in, v_w_out, v_pool_w), ("adam_w_in", "adam_w_out", "adam_pool_w")):
        res = _adam_sharded(ids, g, got, far, two(w), two(m), two(v_), name)
        sharded.append([r.reshape(w.shape) for r in res])
    (g_w_in, dl_w_in, nm_w_in, nv_w_in), (g_w_out, dl_w_out, nm_w_out, nv_w_out), (g_pw, dl_pw, nm_pw, nv_pw) = sharded

    loss_sum = g_loss[0, 0]
    for dev in range(1, N_DEV):
        loss_sum = loss_sum + g_loss[dev, 0]
    loss = (0.5 / d) * loss_sum

    def ordered(rep_list, ada_, w_in_, pw_, w_out_):
        return [rep_list[0], ada_[None], rep_list[1], rep_list[2], rep_list[3], w_in_, rep_list[4], rep_list[5],
                pw_, rep_list[6], w_out_]

    return (loss, grad_x[None],
            *ordered(g_rep, g_wada, g_w_in, g_pw, g_w_out),
            *ordered(dl_rep, dl_wada, dl_w_in, dl_pw, dl_w_out),
            *ordered(nm_rep, nm_wada, nm_w_in, nm_pw, nm_w_out),
            *ordered(nv_rep, nv_wada, nv_w_in, nv_pw, nv_w_out))
```

```python
import jax
import jax.numpy as jnp
from jax import lax
from jax.experimental import pallas as pl
from jax.experimental.pallas import tpu as pltpu

HEAD_DIM = 128
GQA_GROUP = 4
ATTN_SUB_HEADS = 1
ATTN_BWD_SUB_HEADS = 1
ATTN_KEY_PARTS = 2
LOG2_E = 1.4426950408889634
GRID_W = 64
ROPE_PAIRS = HEAD_DIM // 4
ROPE_THETA = 10000.0
ATTN_SCALE = HEAD_DIM ** -0.5
EPS = 1e-6
POOL_WINDOWS = (2, 4, 8, 16)
POOL_HALO = 8
N_DEV = 8
N_CHIPS = 4
ADAM_LR = 0.001
ADAM_B1 = 0.9
ADAM_B2 = 0.999
ADAM_EPS = 1e-08
ADAM_WD = 0.01
ADAM_STEP = 10

LANES = 128
SUBLANES = 8
BF16_ROWS = 16

_MESH = pl.DeviceIdType.MESH
_ANY = pl.BlockSpec(memory_space=pl.ANY)
_VMEM = pl.BlockSpec(memory_space=pltpu.VMEM)
_HBM = pl.BlockSpec(memory_space=pltpu.HBM)
_SEM = pl.BlockSpec(memory_space=pltpu.SEMAPHORE)
_EFFECT = pltpu.SideEffectType.DATAFLOW_SIDE_EFFECTING
_F32 = jnp.float32
_BF16 = jnp.bfloat16


def _tile(dim, pref, align):
    t = min(pref, dim)
    t -= t % align
    while t >= align:
        if dim % t == 0:
            return t
        t -= align
    return dim


def _position():
    return lax.axis_index("x"), lax.axis_index("y"), lax.axis_index("c")


def _flip(v, bit):
    return 1 - v if bit else v


def _dev_index(x, y, c):
    return 4 * x + 2 * y + c


def _silu(g):
    return g * jax.nn.sigmoid(g)


def _silu_grad(g):
    s = jax.nn.sigmoid(g)
    return s * (1.0 + g * (1.0 - s))


def _adamw(w, g, m, v):
    m = ADAM_B1 * m + (1.0 - ADAM_B1) * g
    v = ADAM_B2 * v + (1.0 - ADAM_B2) * (g * g)
    m_hat = m / (1.0 - ADAM_B1 ** ADAM_STEP)
    v_hat = v / (1.0 - ADAM_B2 ** ADAM_STEP)
    delta = -ADAM_LR * (m_hat / (jnp.sqrt(v_hat) + ADAM_EPS) + ADAM_WD * w)
    return delta, m, v


def _all_gather_small(v, name):
    rows, cols = v.shape

    def body(v_ref, out_ref, send_sems, recv_sems):
        x, y, c = _position()
        me = _dev_index(x, y, c)
        out_ref[me] = v_ref[...]
        peers = [(_flip(x, k & 4), _flip(y, k & 2), _flip(c, k & 1)) for k in range(1, N_DEV)]

        def copy(k, block, to):
            return pltpu.make_async_remote_copy(
                src_ref=v_ref, dst_ref=out_ref.at[block], send_sem=send_sems.at[k], recv_sem=recv_sems.at[k],
                device_id=to, device_id_type=_MESH)

        sends = [copy(k, me, p) for k, p in enumerate(peers)]
        for s in sends:
            s.start()
        for k, p in enumerate(peers):
            copy(k, _dev_index(*p), p).wait_recv()
        for s in sends:
            s.wait_send()

    return pl.pallas_call(
        body, name=name,
        out_shape=jax.ShapeDtypeStruct((N_DEV, rows, cols), v.dtype),
        in_specs=[_VMEM], out_specs=_VMEM,
        scratch_shapes=[pltpu.SemaphoreType.DMA((N_DEV - 1,)), pltpu.SemaphoreType.DMA((N_DEV - 1,))],
    )(v)


def _route(x, y, c):
    first = (x + (1 - c) * (1 - 2 * x), y + c * (1 - 2 * y))
    second = (x + c * (1 - 2 * x), y + (1 - c) * (1 - 2 * y))
    return first, second, (1 - x, 1 - y)


def _w_in_order(x, y, c):
    first, second, diagonal = _route(x, y, c)
    return [(x, y, c), (x, y, 1 - c), (*first, c), (*second, 1 - c), (*second, c), (*first, 1 - c),
            (*diagonal, c), (*diagonal, 1 - c)]


W_IN_HOPS = {"a": 2, "b": 3, "c": 1, "d": 1}


def _w_in_hop_copies(group, wg, width, send_sems, recv_sems):
    x, y, c = _position()
    me, sibling = (x, y, c), (x, y, 1 - c)
    first, second, diagonal = _route(x, y, c)

    def cp(k, block, to):
        cols = wg.at[:, pl.ds(pl.multiple_of(_dev_index(*block) * width, width), width)]
        return pltpu.make_async_remote_copy(
            src_ref=cols, dst_ref=cols, send_sem=send_sems.at[k], recv_sem=recv_sems.at[k],
            device_id=to, device_id_type=_MESH)

    if group == "a":
        return [cp(0, me, sibling), cp(1, me, (*first, c))]
    if group == "b":
        return [cp(0, me, (*second, c)), cp(1, (*first, c), (*second, c)), cp(2, (*first, c), sibling)]
    if group == "c":
        return [cp(0, (*second, c), sibling)]
    return [cp(0, (*diagonal, c), sibling)]


def _w_in_hop(wg, width, waits, start, after, name):
    n_sem = 2 * len(waits)

    def body(*refs):
        wg_ref = refs[0]
        for i, (group, _, _, arrivals, sends) in enumerate(waits):
            cps = _w_in_hop_copies(group, wg_ref, width, refs[1 + 2 * i], refs[2 + 2 * i])
            for k in arrivals:
                cps[k].wait_recv()
            for k in sends:
                cps[k].wait_send()
        if start:
            for cp in _w_in_hop_copies(start, wg_ref, width, refs[n_sem + 2], refs[n_sem + 3]):
                cp.start()
        refs[-1][...] = jnp.zeros_like(refs[-1])

    sems = [s for w in waits for s in w[1:3]]
    new = [pltpu.SemaphoreType.DMA((W_IN_HOPS[start],))] * 2 if start else []
    outs = pl.pallas_call(
        body, name=name,
        out_shape=(*new, pltpu.HBM(wg.shape, wg.dtype), jax.ShapeDtypeStruct((SUBLANES, LANES), _F32)),
        in_specs=[_HBM] + [_SEM] * n_sem + [_ANY], out_specs=(*[_SEM] * len(new), _HBM, _VMEM),
        input_output_aliases={0: len(new)},
        compiler_params=pltpu.CompilerParams(has_side_effects=_EFFECT),
    )(pltpu.with_memory_space_constraint(wg, pltpu.HBM), *sems, after)
    return outs


def _slab_copies(bufs, send_sems, recv_sems):
    x, y, c = _position()
    me = _dev_index(x, y, c)
    peers = [(_flip(x, k & 4), _flip(y, k & 2), _flip(c, k & 1)) for k in range(1, N_DEV)]
    return [pltpu.make_async_remote_copy(
        src_ref=buf.at[me], dst_ref=buf.at[me],
        send_sem=send_sems.at[(N_DEV - 1) * a + k], recv_sem=recv_sems.at[(N_DEV - 1) * a + k],
        device_id=peer, device_id_type=_MESH)
        for a, buf in enumerate(bufs) for k, peer in enumerate(peers)]


def _gather_slabs_start(bufs, after, name):
    n = len(bufs)
    n_copies = (N_DEV - 1) * n

    def body(*refs):
        send_sems, recv_sems, token = refs[n + 1], refs[n + 2], refs[-1]
        for cp in _slab_copies(refs[:n], send_sems, recv_sems):
            cp.start()
        token[...] = jnp.zeros_like(token)

    outs = pl.pallas_call(
        body, name=name,
        out_shape=(pltpu.SemaphoreType.DMA((n_copies,)), pltpu.SemaphoreType.DMA((n_copies,)),
                   *[pltpu.HBM(b.shape, b.dtype) for b in bufs], jax.ShapeDtypeStruct((SUBLANES, LANES), _F32)),
        in_specs=[_HBM] * n + [_ANY], out_specs=(_SEM, _SEM, *[_HBM] * n, _VMEM),
        input_output_aliases={i: 2 + i for i in range(n)},
        compiler_params=pltpu.CompilerParams(has_side_effects=_EFFECT),
    )(*[pltpu.with_memory_space_constraint(b, pltpu.HBM) for b in bufs], after)
    return outs[0], outs[1], list(outs[2:2 + n]), outs[-1]


def _gather_slabs_wait(send_sems, recv_sems, bufs, after, name):
    n = len(bufs)

    def body(*refs):
        for cp in _slab_copies(refs[:n], refs[n], refs[n + 1]):
            cp.wait_send()
            cp.wait_recv()

    outs = pl.pallas_call(
        body, name=name, out_shape=tuple(pltpu.HBM(b.shape, b.dtype) for b in bufs),
        in_specs=[_HBM] * n + [_SEM, _SEM, _ANY], out_specs=[_HBM] * n,
        input_output_aliases={i: i for i in range(n)},
        compiler_params=pltpu.CompilerParams(has_side_effects=_EFFECT),
    )(*bufs, send_sems, recv_sems, after)
    return list(outs)


def _chip_order(x, y):
    return [(x, y), (1 - x, y), (x, 1 - y), (1 - x, 1 - y)]


def _chip_copies(srcs, lands, send_sems, recv_sems):
    x, y, c = _position()
    return [pltpu.make_async_remote_copy(
        src_ref=srcs[a].at[k], dst_ref=lands[a].at[k],
        send_sem=send_sems.at[(N_CHIPS - 1) * a + k], recv_sem=recv_sems.at[(N_CHIPS - 1) * a + k],
        device_id=(*chip, c), device_id_type=_MESH)
        for a in range(len(srcs)) for k, chip in enumerate(_chip_order(x, y)[1:])]


def _sibling_copies(srcs, lands, send_sems, recv_sems):
    x, y, c = _position()
    return [pltpu.make_async_remote_copy(
        src_ref=srcs[a].at[s], dst_ref=lands[a].at[s],
        send_sem=send_sems.at[N_CHIPS * a + s], recv_sem=recv_sems.at[N_CHIPS * a + s],
        device_id=(x, y, 1 - c), device_id_type=_MESH)
        for a in range(len(srcs)) for s in range(N_CHIPS)]


def _sibling_copies_by_device(srcs, lands, send_sems, recv_sems):
    x, y, c = _position()
    return [pltpu.make_async_remote_copy(
        src_ref=srcs[a].at[_dev_index(*chip, 1 - c)], dst_ref=lands[a].at[s],
        send_sem=send_sems.at[N_CHIPS * a + s], recv_sem=recv_sems.at[N_CHIPS * a + s],
        device_id=(x, y, 1 - c), device_id_type=_MESH)
        for a in range(len(srcs)) for s, chip in enumerate(_chip_order(x, y))]


def _exchange_start(copies, sums, name, land_slabs=None):
    n = len(sums)
    land_shapes = [((land_slabs or s.shape[0]),) + s.shape[1:] for s in sums]
    n_copies = sum(shape[0] for shape in land_shapes)

    def body(*refs):
        srcs, lands = refs[:n], refs[n:2 * n]
        send_sems, recv_sems, token = refs[2 * n], refs[2 * n + 1], refs[-1]
        for cp in copies(srcs, lands, send_sems, recv_sems):
            cp.start()
        token[...] = jnp.zeros_like(token)

    hbm = [pltpu.HBM(s.shape, s.dtype) for s in sums] + [pltpu.HBM(shape, s.dtype) for shape, s in zip(land_shapes, sums)]
    outs = pl.pallas_call(
        body, name=name,
        out_shape=(pltpu.SemaphoreType.DMA((n_copies,)), pltpu.SemaphoreType.DMA((n_copies,)), *hbm,
                   jax.ShapeDtypeStruct((SUBLANES, LANES), _F32)),
        in_specs=[_HBM] * (2 * n), out_specs=(_SEM, _SEM, *[_HBM] * (2 * n), _VMEM),
        input_output_aliases={i: 2 + i for i in range(2 * n)},
        compiler_params=pltpu.CompilerParams(has_side_effects=_EFFECT),
    )(*[pltpu.with_memory_space_constraint(s, pltpu.HBM) for s in sums],
      *[pltpu.with_memory_space_constraint(lax.empty(shape, s.dtype), pltpu.HBM) for shape, s in zip(land_shapes, sums)])
    return outs[0], outs[1], list(outs[2:2 + n]), list(outs[2 + n:2 + 2 * n]), outs[-1]


def _exchange_wait(copies, send_sems, recv_sems, srcs, lands, after, name, with_sources=False):
    n = len(srcs)

    def body(*refs):
        for cp in copies(refs[:n], refs[n:2 * n], refs[2 * n], refs[2 * n + 1]):
            cp.wait_send()
            cp.wait_recv()

    hbm = [pltpu.HBM(s.shape, s.dtype) for s in (*srcs, *lands)]
    outs = pl.pallas_call(
        body, name=name, out_shape=tuple(hbm),
        in_specs=[_HBM] * (2 * n) + [_SEM, _SEM, _ANY], out_specs=[_HBM] * (2 * n),
        input_output_aliases={i: i for i in range(2 * n)},
        compiler_params=pltpu.CompilerParams(has_side_effects=_EFFECT),
    )(*srcs, *lands, send_sems, recv_sems, after)
    return list(outs) if with_sources else list(outs[n:])


def _matmul(a, b, *, ta=False, tb=False, out_dtype=_F32, tm=1024, tn=1024, tk=4608, after=None, name):
    kdim, m = a.shape if ta else a.shape[::-1]
    n = b.shape[0] if tb else b.shape[1]
    tm = _tile(m, tm, LANES if ta else BF16_ROWS)
    tn = _tile(n, tn, LANES)
    tk = _tile(kdim, tk, BF16_ROWS if ta else LANES)
    nk = kdim // tk
    dims = (((0 if ta else 1,), (1 if tb else 0,)), ((), ()))

    def body_whole_k(a_ref, b_ref, *rest):
        rest[-1][...] = lax.dot_general(a_ref[...], b_ref[...], dims, preferred_element_type=_F32).astype(out_dtype)

    def body_split_k(a_ref, b_ref, *rest):
        o_ref, acc_ref = rest[-2:]
        k = pl.program_id(2)

        @pl.when(k == 0)
        def _():
            acc_ref[...] = jnp.zeros_like(acc_ref)

        acc_ref[...] += lax.dot_general(a_ref[...], b_ref[...], dims, preferred_element_type=_F32)

        @pl.when(k == nk - 1)
        def _():
            o_ref[...] = acc_ref[...].astype(out_dtype)

    a_spec = pl.BlockSpec((tk, tm), lambda i, j, k: (k, i)) if ta else pl.BlockSpec((tm, tk), lambda i, j, k: (i, k))
    b_spec = pl.BlockSpec((tn, tk), lambda i, j, k: (j, k)) if tb else pl.BlockSpec((tk, tn), lambda i, j, k: (k, j))
    extra = [] if after is None else [after]
    return pl.pallas_call(
        body_whole_k if nk == 1 else body_split_k, name=name, grid=(m // tm, n // tn, nk),
        in_specs=[a_spec, b_spec] + [pl.BlockSpec(t.shape, lambda i, j, k: (0, 0)) for t in extra],
        out_specs=pl.BlockSpec((tm, tn), lambda i, j, k: (i, j)), out_shape=jax.ShapeDtypeStruct((m, n), out_dtype),
        scratch_shapes=[] if nk == 1 else [pltpu.VMEM((tm, tn), _F32)],
        compiler_params=pltpu.CompilerParams(dimension_semantics=("parallel", "parallel", "arbitrary")),
    )(a, b, *extra)


def _proj_blocks(a, wg, dst, order_ids, first, count, width, after, name):
    m, kdim = a.shape
    tm = _tile(m, 1088, BF16_ROWS)

    def body(ids_ref, a_ref, w_ref, after_ref, dst_ref, o_ref):
        del ids_ref, after_ref, dst_ref
        o_ref[...] = jnp.dot(a_ref[...], w_ref[...], preferred_element_type=_F32)

    return pl.pallas_call(
        body, name=name,
        grid_spec=pltpu.PrefetchScalarGridSpec(
            num_scalar_prefetch=1, grid=(count, m // tm),
            in_specs=[pl.BlockSpec((tm, kdim), lambda j, i, ids: (i, 0)),
                      pl.BlockSpec((kdim, width), lambda j, i, ids: (0, ids[first + j])),
                      pl.BlockSpec(after.shape, lambda j, i, ids: (0, 0)), _ANY],
            out_specs=pl.BlockSpec((tm, width), lambda j, i, ids: (i, ids[first + j]))),
        out_shape=jax.ShapeDtypeStruct(dst.shape, dst.dtype),
        input_output_aliases={4: 0},
        compiler_params=pltpu.CompilerParams(dimension_semantics=("arbitrary", "arbitrary")),
    )(order_ids, a, wg, after, dst)


def _cast_into_columns(slab_ids, a, n_blocks, name):
    r, c = a.shape
    tr = _row_tile(r, c)

    def body(ids_ref, a_ref, o_ref):
        del ids_ref
        o_ref[...] = a_ref[...].astype(_BF16)

    return pl.pallas_call(
        body, name=name,
        grid_spec=pltpu.PrefetchScalarGridSpec(
            num_scalar_prefetch=1, grid=(r // tr,),
            in_specs=[pl.BlockSpec((tr, c), lambda i, ids: (i, 0))],
            out_specs=pl.BlockSpec((tr, c), lambda i, ids: (i, ids[0]))),
        out_shape=jax.ShapeDtypeStruct((r, n_blocks * c), _BF16),
        compiler_params=pltpu.CompilerParams(dimension_semantics=("parallel",)),
    )(slab_ids, a)


def _matmul_slabs(a, b, ids, width, name, after=None):
    kdim, m = a.shape
    n_slabs = ids.shape[0]
    tm = _tile(m, 1024, LANES)

    def body(ids_ref, a_ref, b_ref, *rest):
        del ids_ref
        rest[-1][0] = lax.dot_general(a_ref[...], b_ref[...], (((0,), (0,)), ((), ())), preferred_element_type=_F32)

    extra = [] if after is None else [after]
    return pl.pallas_call(
        body, name=name,
        grid_spec=pltpu.PrefetchScalarGridSpec(
            num_scalar_prefetch=1, grid=(m // tm, n_slabs),
            in_specs=[pl.BlockSpec((kdim, tm), lambda i, j, ids: (0, i)),
                      pl.BlockSpec((kdim, width), lambda i, j, ids: (0, ids[j]))]
            + [pl.BlockSpec(t.shape, lambda i, j, ids: (0, 0)) for t in extra],
            out_specs=pl.BlockSpec((1, tm, width), lambda i, j, ids: (j, i, 0))),
        out_shape=jax.ShapeDtypeStruct((n_slabs, m, width), _F32),
        compiler_params=pltpu.CompilerParams(dimension_semantics=("parallel", "parallel")),
    )(ids, a, b, *extra)


def _row_tile(rows, cols):
    return _tile(rows, max(BF16_ROWS, min(512, (1 << 19) // cols)), BF16_ROWS)


def _cast_into_slab(slab_ids, a, name):
    r, c = a.shape
    tr = _row_tile(r, c)

    def body(ids_ref, a_ref, o_ref):
        del ids_ref
        o_ref[0] = a_ref[...].astype(_BF16)

    return pl.pallas_call(
        body, name=name,
        grid_spec=pltpu.PrefetchScalarGridSpec(
            num_scalar_prefetch=1, grid=(r // tr,),
            in_specs=[pl.BlockSpec((tr, c), lambda i, ids: (i, 0))],
            out_specs=pl.BlockSpec((1, tr, c), lambda i, ids: (ids[0], i, 0))),
        out_shape=jax.ShapeDtypeStruct((N_DEV, r, c), _BF16),
        compiler_params=pltpu.CompilerParams(dimension_semantics=("parallel",)),
    )(slab_ids, a)


def _pre_add(slab_ids, grad, got, name):
    _, r, c = grad.shape
    tr = _row_tile(r, c)

    def body(ids_ref, a_ref, b_ref, o_ref):
        del ids_ref
        o_ref[...] = (a_ref[...] + b_ref[...]).astype(_BF16)

    return pl.pallas_call(
        body, name=name,
        grid_spec=pltpu.PrefetchScalarGridSpec(
            num_scalar_prefetch=1, grid=(N_CHIPS - 1, r // tr),
            in_specs=[pl.BlockSpec((1, tr, c), lambda s, i, ids: (ids[s + 1], i, 0)),
                      pl.BlockSpec((1, tr, c), lambda s, i, ids: (s + 1, i, 0))],
            out_specs=pl.BlockSpec((1, tr, c), lambda s, i, ids: (s, i, 0))),
        out_shape=jax.ShapeDtypeStruct((N_CHIPS - 1, r, c), _BF16),
        compiler_params=pltpu.CompilerParams(dimension_semantics=("parallel", "parallel")),
    )(slab_ids, grad, got)


def _ada_forward(craw, w_shard, name):
    d, cols = w_shard.shape
    tk = _tile(d, 512, LANES)

    def body(c_ref, w_ref, o_ref):
        @pl.when(pl.program_id(0) == 0)
        def _():
            o_ref[...] = jnp.zeros_like(o_ref)

        o_ref[...] += jnp.dot(_silu(c_ref[...]).astype(_BF16), w_ref[...].astype(_BF16), preferred_element_type=_F32)

    return pl.pallas_call(
        body, name=name, grid=(d // tk,),
        in_specs=[pl.BlockSpec((craw.shape[0], tk), lambda k: (0, k)), pl.BlockSpec((tk, cols), lambda k: (k, 0))],
        out_specs=pl.BlockSpec((craw.shape[0], cols), lambda k: (0, 0)),
        out_shape=jax.ShapeDtypeStruct((craw.shape[0], cols), _F32),
        compiler_params=pltpu.CompilerParams(dimension_semantics=("arbitrary",)),
    )(craw, w_shard)


def _ada_backward(craw, dmod, w, m, v, name):
    d, cols = w.shape
    rows = craw.shape[0]
    tr = _tile(d, 256, LANES)

    def body(c_ref, dm_ref, w_ref, m_ref, v_ref, g_ref, dl_ref, nm_ref, nv_ref, dc_ref):
        act = _silu(c_ref[...]).astype(_BF16)
        dmb = dm_ref[...].astype(_BF16)
        wv = w_ref[...]
        g = lax.dot_general(act, dmb, (((0,), (0,)), ((), ())), preferred_element_type=_F32)
        delta, nm, nv = _adamw(wv, g, m_ref[...], v_ref[...])
        g_ref[...] = g
        dl_ref[...] = delta
        nm_ref[...] = nm
        nv_ref[...] = nv
        dc = lax.dot_general(dmb, wv.astype(_BF16), (((1,), (1,)), ((), ())), preferred_element_type=_F32)
        dc_ref[...] = jnp.broadcast_to(jnp.sum(dc[N_DEV:], axis=0, keepdims=True), dc_ref.shape)

    blk = pl.BlockSpec((tr, cols), lambda i: (i, 0))
    return pl.pallas_call(
        body, name=name, grid=(d // tr,),
        in_specs=[pl.BlockSpec((rows, tr), lambda i: (0, i)), pl.BlockSpec((rows, cols), lambda i: (0, 0)), blk, blk, blk],
        out_specs=[blk, blk, blk, blk, pl.BlockSpec((SUBLANES, tr), lambda i: (0, i))],
        out_shape=[jax.ShapeDtypeStruct((d, cols), _F32)] * 4 + [jax.ShapeDtypeStruct((SUBLANES, d), _F32)],
        compiler_params=pltpu.CompilerParams(dimension_semantics=("parallel",)),
    )(craw, dmod, w, m, v)


def _rms(xf):
    return lax.rsqrt(jnp.mean(xf * xf, axis=-1, keepdims=True) + EPS)


def _head_mean(v):
    hi = v.astype(_BF16)
    lo = (v - hi.astype(_F32)).astype(_BF16)
    ones = jnp.full((2 * HEAD_DIM, HEAD_DIM), 1.0 / HEAD_DIM, _BF16)
    return jnp.dot(jnp.concatenate([hi, lo], axis=1), ones, preferred_element_type=_F32)


def _prenorm(ctx, x, g_pre, mods, tr, name):
    l, d = ctx.shape
    n = x.shape[0]
    nbl = n // tr

    def body(ctx_ref, x_ref, g_ref, mod_ref, h_ref):
        def emit(src_ref):
            xf = src_ref[...]
            y = (xf * _rms(xf)) * g_ref[...]
            h_ref[...] = (y * (1.0 + mod_ref[0, 0:1, :]) + mod_ref[0, 1:2, :]).astype(_BF16)

        is_latent = pl.program_id(0) < nbl
        pl.when(is_latent)(lambda: emit(x_ref))
        pl.when(jnp.logical_not(is_latent))(lambda: emit(ctx_ref))

    return pl.pallas_call(
        body, name=name, grid=((l + n) // tr,),
        in_specs=[pl.BlockSpec((tr, d), lambda i: (jnp.maximum(i - nbl, 0), 0)),
                  pl.BlockSpec((tr, d), lambda i: (jnp.minimum(i, nbl - 1), 0)),
                  pl.BlockSpec((1, d), lambda i: (0, 0)),
                  pl.BlockSpec((1, SUBLANES, d), lambda i: ((i < nbl).astype(jnp.int32), 0, 0))],
        out_specs=pl.BlockSpec((tr, d), lambda i: (i, 0)),
        out_shape=jax.ShapeDtypeStruct((l + n, d), _BF16),
        compiler_params=pltpu.CompilerParams(dimension_semantics=("arbitrary",)),
    )(ctx, x, g_pre, mods)


def _prenorm_backward(dh, ctx, x, dxn, g_pre, mods, tr, name):
    l, d = ctx.shape
    n = x.shape[0]
    nbl = n // tr

    def body(dh_ref, ctx_ref, x_ref, dxn_ref, g_ref, mod_ref, gx_ref, dmod_ref, dg_ref):
        i = pl.program_id(0)

        @pl.when(i == 0)
        def _():
            dg_ref[...] = jnp.zeros_like(dg_ref)

        @pl.when(jnp.logical_or(i == 0, i == nbl))
        def _():
            dmod_ref[...] = jnp.zeros_like(dmod_ref)

        def emit(src_ref, latent):
            xf = src_ref[...]
            r = _rms(xf)
            xn = xf * r
            dhv = dh_ref[...]
            one_scale = 1.0 + mod_ref[0, 0:1, :]
            dmod_ref[0, 0:1, :] += jnp.sum(dhv * (xn * g_ref[...]), axis=0, keepdims=True)
            dmod_ref[0, 1:2, :] += jnp.sum(dhv, axis=0, keepdims=True)
            dyg = dhv * one_scale
            dg_ref[0:1, :] += jnp.sum(dyg * xn, axis=0, keepdims=True)
            if latent:
                dn = dyg * g_ref[...]
                gx_ref[...] = dxn_ref[...] + r * (dn - xn * jnp.mean(dn * xn, axis=-1, keepdims=True))

        pl.when(i < nbl)(lambda: emit(x_ref, True))
        pl.when(i >= nbl)(lambda: emit(ctx_ref, False))

    lat = pl.BlockSpec((tr, d), lambda i: (jnp.minimum(i, nbl - 1), 0))
    sel = pl.BlockSpec((1, SUBLANES, d), lambda i: ((i < nbl).astype(jnp.int32), 0, 0))
    return pl.pallas_call(
        body, name=name, grid=((l + n) // tr,),
        in_specs=[pl.BlockSpec((tr, d), lambda i: (i, 0)),
                  pl.BlockSpec((tr, d), lambda i: (jnp.maximum(i - nbl, 0), 0)),
                  lat, lat, pl.BlockSpec((1, d), lambda i: (0, 0)), sel],
        out_specs=[lat, sel, pl.BlockSpec((SUBLANES, d), lambda i: (0, 0))],
        out_shape=[jax.ShapeDtypeStruct((n, d), _F32), jax.ShapeDtypeStruct((2, SUBLANES, d), _F32),
                   jax.ShapeDtypeStruct((SUBLANES, d), _F32)],
        compiler_params=pltpu.CompilerParams(dimension_semantics=("arbitrary",)),
    )(dh, ctx, x, dxn, g_pre, mods)


def _rope_tables(l, n):
    rows = n // GRID_W
    inv = ROPE_THETA ** (-jnp.arange(ROPE_PAIRS, dtype=_F32) / ROPE_PAIRS)
    ang_r = jnp.arange(rows, dtype=_F32)[:, None] * inv
    ang_c = jnp.arange(GRID_W, dtype=_F32)[:, None] * inv
    per_row = lambda a: jnp.repeat(a, GRID_W, axis=0)
    per_col = lambda a: jnp.tile(a, (rows, 1))
    cr, sr, cc, sc = per_row(jnp.cos(ang_r)), per_row(jnp.sin(ang_r)), per_col(jnp.cos(ang_c)), per_col(jnp.sin(ang_c))
    zero = jnp.zeros_like(sr)
    tc = jnp.concatenate([cr, cr, cc, cc], axis=-1)
    ta = jnp.concatenate([-sr, zero, -sc, zero], axis=-1)
    tb = jnp.concatenate([zero, sr, zero, sc], axis=-1)
    pad = lambda t, fill: jnp.concatenate([t, jnp.full((l, HEAD_DIM), fill, _F32)], axis=0)
    return pad(tc, 1.0), pad(ta, 0.0), pad(tb, 0.0)


def _rope(y, tc, ta, tb):
    return y * tc + pltpu.roll(y, HEAD_DIM - ROPE_PAIRS, 1) * ta + pltpu.roll(y, ROPE_PAIRS, 1) * tb


def _rope_transposed(dy, tc, ta, tb):
    return dy * tc + pltpu.roll(dy * ta, ROPE_PAIRS, 1) + pltpu.roll(dy * tb, HEAD_DIM - ROPE_PAIRS, 1)


def _qkv_post(proj, tables, g_q, g_k, heads, kv_heads, tr, name):
    t = proj.shape[0]
    aw, kw = heads * HEAD_DIM, kv_heads * HEAD_DIM
    w3 = aw + 2 * kw

    def body(p_ref, tc_ref, ta_ref, tb_ref, gq_ref, gk_ref, q_ref, k_ref, v_ref):
        tabs = (tc_ref[...], ta_ref[...], tb_ref[...])

        def norm_rope(col, gain):
            xh = p_ref[:, col:col + HEAD_DIM]
            return _rope((xh * lax.rsqrt(_head_mean(xh * xh) + EPS)) * gain, *tabs).astype(_BF16)

        for h in range(heads):
            q_ref[h] = norm_rope(h * HEAD_DIM, gq_ref[...])
        for h in range(kv_heads):
            k_ref[h] = norm_rope(aw + h * HEAD_DIM, gk_ref[...])
            v_ref[h] = p_ref[:, aw + kw + h * HEAD_DIM:aw + kw + (h + 1) * HEAD_DIM].astype(_BF16)

    tab = pl.BlockSpec((tr, HEAD_DIM), lambda i: (i, 0))
    gain = pl.BlockSpec((1, HEAD_DIM), lambda i: (0, 0))
    return pl.pallas_call(
        body, name=name, grid=(t // tr,),
        in_specs=[pl.BlockSpec((tr, w3), lambda i: (i, 0)), tab, tab, tab, gain, gain],
        out_specs=[pl.BlockSpec((heads, tr, HEAD_DIM), lambda i: (0, i, 0)),
                   pl.BlockSpec((kv_heads, tr, HEAD_DIM), lambda i: (0, i, 0)),
                   pl.BlockSpec((kv_heads, tr, HEAD_DIM), lambda i: (0, i, 0))],
        out_shape=[jax.ShapeDtypeStruct((heads, t, HEAD_DIM), _BF16),
                   jax.ShapeDtypeStruct((kv_heads, t, HEAD_DIM), _BF16),
                   jax.ShapeDtypeStruct((kv_heads, t, HEAD_DIM), _BF16)],
        compiler_params=pltpu.CompilerParams(dimension_semantics=("parallel",)),
    )(proj, *tables, g_q, g_k)


def _qkv_post_backward(proj, dq, dk, dv, tables, g_q, g_k, dproj, l, tr, name):
    t = proj.shape[0]
    heads, kv_heads = dq.shape[0], dk.shape[0]
    aw, kw = heads * HEAD_DIM, kv_heads * HEAD_DIM
    w3 = aw + 2 * kw
    nbl = (t - l) // tr

    def body(p_ref, dq_ref, dk_ref, dv_ref, tc_ref, ta_ref, tb_ref, gq_ref, gk_ref, dproj_ref, o_ref, dgq_ref, dgk_ref):
        del dproj_ref
        i = pl.program_id(0)

        @pl.when(i == 0)
        def _():
            dgq_ref[...] = jnp.zeros_like(dgq_ref)
            dgk_ref[...] = jnp.zeros_like(dgk_ref)

        tabs = (tc_ref[...], ta_ref[...], tb_ref[...])
        latent = i < nbl

        def back(col, dout, gain, dg_ref):
            xh = p_ref[:, col:col + HEAD_DIM]
            r = lax.rsqrt(_head_mean(xh * xh) + EPS)
            xn = xh * r
            dy = _rope_transposed(dout, *tabs)
            dg_ref[0:1, :] += jnp.sum(dy * xn, axis=0, keepdims=True)
            dn = dy * gain
            o_ref[:, col:col + HEAD_DIM] = (r * (dn - xn * _head_mean(dn * xn))).astype(_BF16)

        for h in range(heads):
            back(h * HEAD_DIM, jnp.where(latent, dq_ref[h], 0.0), gq_ref[...], dgq_ref)
        for h in range(kv_heads):
            back(aw + h * HEAD_DIM, dk_ref[h], gk_ref[...], dgk_ref)
            o_ref[:, aw + kw + h * HEAD_DIM:aw + kw + (h + 1) * HEAD_DIM] = dv_ref[h].astype(_BF16)

    tab = pl.BlockSpec((tr, HEAD_DIM), lambda i: (i, 0))
    gain = pl.BlockSpec((1, HEAD_DIM), lambda i: (0, 0))
    acc = pl.BlockSpec((SUBLANES, HEAD_DIM), lambda i: (0, 0))
    return pl.pallas_call(
        body, name=name, grid=(t // tr,),
        in_specs=[pl.BlockSpec((tr, w3), lambda i: (i, 0)),
                  pl.BlockSpec((heads, tr, HEAD_DIM), lambda i: (0, jnp.minimum(i, nbl - 1), 0)),
                  pl.BlockSpec((kv_heads, tr, HEAD_DIM), lambda i: (0, i, 0)),
                  pl.BlockSpec((kv_heads, tr, HEAD_DIM), lambda i: (0, i, 0)),
                  tab, tab, tab, gain, gain, _ANY],
        out_specs=[pl.BlockSpec((tr, w3), lambda i: (i, 0)), acc, acc],
        out_shape=[jax.ShapeDtypeStruct(dproj.shape, dproj.dtype), jax.ShapeDtypeStruct((SUBLANES, HEAD_DIM), _F32),
                   jax.ShapeDtypeStruct((SUBLANES, HEAD_DIM), _F32)],
        input_output_aliases={9: 0},
        compiler_params=pltpu.CompilerParams(dimension_semantics=("arbitrary",)),
    )(proj, dq, dk, dv, *tables, g_q, g_k, dproj)


def _zero_context_rows(dproj, l, w3, tr, name):
    t, iw = dproj.shape
    first = (t - l) // tr

    def body(dproj_ref, o_ref):
        del dproj_ref
        o_ref[...] = jnp.zeros_like(o_ref)

    return pl.pallas_call(
        body, name=name, grid=(l // tr, iw // w3 - 1),
        in_specs=[_ANY], out_specs=pl.BlockSpec((tr, w3), lambda i, j: (first + i, j + 1)),
        out_shape=jax.ShapeDtypeStruct(dproj.shape, dproj.dtype), input_output_aliases={0: 0},
        compiler_params=pltpu.CompilerParams(dimension_semantics=("parallel", "parallel")),
    )(dproj)


def _attention(q, k, v, proj, l, mix, tq, name):
    heads, t, _ = q.shape
    kv_heads = k.shape[0]
    n = t - l
    gw = GQA_GROUP * HEAD_DIM
    aw = heads * HEAD_DIM
    gate_col = (aw + 2 * kv_heads * HEAD_DIM) // gw

    def body(q_ref, k_ref, v_ref, g_ref, o_ref, y_ref, lse_ref):
        lane = lax.broadcasted_iota(jnp.int32, (tq, LANES), 1)
        lse_blk = jnp.zeros((tq, LANES), _F32)
        for first in range(0, GQA_GROUP, ATTN_SUB_HEADS):
            qs = q_ref[first:first + ATTN_SUB_HEADS].reshape(ATTN_SUB_HEADS * tq, HEAD_DIM)
            raw = lax.dot_general(qs, k_ref[0], (((1,), (1,)), ((), ())), preferred_element_type=_F32)
            m = jnp.max(raw, axis=-1, keepdims=True)
            p = jnp.exp2((raw - m) * (ATTN_SCALE * LOG2_E))
            denom = jnp.sum(p, axis=-1, keepdims=True)
            os_ = jnp.dot(p.astype(_BF16), v_ref[0], preferred_element_type=_F32) / denom
            lse_s = m * ATTN_SCALE + jnp.log(denom)
            for j in range(ATTN_SUB_HEADS):
                g = first + j
                og = os_[j * tq:(j + 1) * tq]
                cols = slice(g * HEAD_DIM, (g + 1) * HEAD_DIM)
                o_ref[:, cols] = og
                y_ref[:, cols] = (og * _silu(g_ref[:, cols])).astype(_BF16)
                lse_blk = jnp.where(lane == g, lse_s[j * tq:(j + 1) * tq], lse_blk)
        lse_ref[0] = lse_blk

    return pl.pallas_call(
        body, name=name, grid=(kv_heads, n // tq),
        in_specs=[pl.BlockSpec((GQA_GROUP, tq, HEAD_DIM), lambda h, i: (h, i, 0)),
                  pl.BlockSpec((1, t, HEAD_DIM), lambda h, i: (h, 0, 0)),
                  pl.BlockSpec((1, t, HEAD_DIM), lambda h, i: (h, 0, 0)),
                  pl.BlockSpec((tq, gw), lambda h, i: (i, gate_col + h))],
        out_specs=[pl.BlockSpec((tq, gw), lambda h, i: (i, h)),
                   pl.BlockSpec((tq, gw), lambda h, i: (i, h)),
                   pl.BlockSpec((1, tq, LANES), lambda h, i: (h, i, 0))],
        out_shape=[jax.ShapeDtypeStruct((n, aw), _F32), jax.ShapeDtypeStruct((n, mix), _BF16),
                   jax.ShapeDtypeStruct((kv_heads, n, LANES), _F32)],
        compiler_params=pltpu.CompilerParams(dimension_semantics=("parallel", "parallel")),
    )(q, k, v, proj)


def _attention_backward(q, k, v, attn_o, dy, proj, lse, after, dproj, l, tq, name):
    heads, t, _ = q.shape
    kv_heads = k.shape[0]
    n = t - l
    gw = GQA_GROUP * HEAD_DIM
    aw = heads * HEAD_DIM
    gate_col = (aw + 2 * kv_heads * HEAD_DIM) // gw
    n_parts = next(p for p in (ATTN_KEY_PARTS, 2, 1) if t % (p * BF16_ROWS) == 0)
    part = t // n_parts

    def body(q_ref, k_ref, v_ref, o_ref, dy_ref, g_ref, lse_ref, after_ref, dproj_ref, dq_ref, dg_ref, dk_ref, dv_ref):
        del after_ref, dproj_ref

        @pl.when(pl.program_id(1) == 0)
        def _():
            dk_ref[...] = jnp.zeros_like(dk_ref)
            dv_ref[...] = jnp.zeros_like(dv_ref)

        lse_blk = lse_ref[0]
        for first in range(0, GQA_GROUP, ATTN_BWD_SUB_HEADS):
            qs = q_ref[first:first + ATTN_BWD_SUB_HEADS].reshape(ATTN_BWD_SUB_HEADS * tq, HEAD_DIM)
            do_parts, delta_parts, lse_parts = [], [], []
            for g in range(first, first + ATTN_BWD_SUB_HEADS):
                cols = slice(g * HEAD_DIM, (g + 1) * HEAD_DIM)
                gate, og, dyg = g_ref[:, cols], o_ref[:, cols], dy_ref[:, cols]
                dog = dyg * _silu(gate)
                dg_ref[:, cols] = (dyg * og * _silu_grad(gate)).astype(_BF16)
                do_parts.append(dog)
                delta_parts.append(jnp.sum(dog * og, axis=-1, keepdims=True))
                lse_parts.append(lse_blk[:, g:g + 1])
            dos = jnp.concatenate(do_parts, axis=0).astype(_BF16)
            delta = jnp.concatenate(delta_parts, axis=0)
            lse2 = jnp.concatenate(lse_parts, axis=0) * LOG2_E
            dqs = jnp.zeros((ATTN_BWD_SUB_HEADS * tq, HEAD_DIM), _F32)
            for part_i in range(n_parts):
                keys = slice(part_i * part, (part_i + 1) * part)
                ks, vs = k_ref[0, keys, :], v_ref[0, keys, :]
                raw = lax.dot_general(qs, ks, (((1,), (1,)), ((), ())), preferred_element_type=_F32)
                p = jnp.exp2(raw * (ATTN_SCALE * LOG2_E) - lse2)
                dp = lax.dot_general(dos, vs, (((1,), (1,)), ((), ())), preferred_element_type=_F32)
                ds = (p * (dp - delta)).astype(_BF16)
                dqs = dqs + jnp.dot(ds, ks, preferred_element_type=_F32)
                dk_ref[0, keys, :] += ATTN_SCALE * lax.dot_general(
                    ds, qs, (((0,), (0,)), ((), ())), preferred_element_type=_F32)
                dv_ref[0, keys, :] += lax.dot_general(
                    p.astype(_BF16), dos, (((0,), (0,)), ((), ())), preferred_element_type=_F32)
            dq_ref[first:first + ATTN_BWD_SUB_HEADS] = (ATTN_SCALE * dqs).reshape(ATTN_BWD_SUB_HEADS, tq, HEAD_DIM)

    kv_spec = pl.BlockSpec((1, t, HEAD_DIM), lambda h, i: (h, 0, 0))
    tok = pl.BlockSpec((tq, gw), lambda h, i: (i, h))
    gate = pl.BlockSpec((tq, gw), lambda h, i: (i, gate_col + h))
    return pl.pallas_call(
        body, name=name, grid=(kv_heads, n // tq),
        in_specs=[pl.BlockSpec((GQA_GROUP, tq, HEAD_DIM), lambda h, i: (h, i, 0)), kv_spec, kv_spec,
                  tok, tok, gate, pl.BlockSpec((1, tq, LANES), lambda h, i: (h, i, 0)),
                  pl.BlockSpec(after.shape, lambda h, i: (0, 0)), _ANY],
        out_specs=[pl.BlockSpec((GQA_GROUP, tq, HEAD_DIM), lambda h, i: (h, i, 0)), gate, kv_spec, kv_spec],
        out_shape=[jax.ShapeDtypeStruct((heads, n, HEAD_DIM), _F32), jax.ShapeDtypeStruct(dproj.shape, dproj.dtype),
                   jax.ShapeDtypeStruct((kv_heads, t, HEAD_DIM), _F32), jax.ShapeDtypeStruct((kv_heads, t, HEAD_DIM), _F32)],
        input_output_aliases={8: 1},
        compiler_params=pltpu.CompilerParams(dimension_semantics=("parallel", "arbitrary")),
    )(q, k, v, attn_o, dy, proj, lse, after, dproj)


def _halo_specs(tp, width, col, row_off, total_rows):
    per = tp // POOL_HALO
    first = row_off // POOL_HALO
    last = total_rows // POOL_HALO - 1
    return [pl.BlockSpec((tp, width), lambda i: (i + row_off // tp, col)),
            pl.BlockSpec((POOL_HALO, width), lambda i: (jnp.maximum(first + i * per - 1, 0), col)),
            pl.BlockSpec((POOL_HALO, width), lambda i: (jnp.minimum(first + (i + 1) * per, last), col))]


def _with_halo(cur, prev, nxt, t0, n):
    tp = cur.shape[0]
    r8 = lax.broadcasted_iota(jnp.int32, (POOL_HALO, 1), 0)
    prev = jnp.where(t0 - POOL_HALO + r8 >= 0, prev, 0.0)
    nxt = jnp.where(t0 + tp + r8 < n, nxt, 0.0)
    return jnp.concatenate([prev, cur, nxt], axis=0)


def _shift_rows(a, s):
    return pltpu.roll(a, s % a.shape[0], 0)


def _window_sum(e, w, mirrored):
    a = e + _shift_rows(e, -1 if mirrored else 1)
    s = 1
    while 2 * s < w:
        a = _shift_rows(a, s) + _shift_rows(a, -s)
        s *= 2
    return a


def _window_count(t, w, n):
    half = w // 2
    return (jnp.minimum(t + half, n) - jnp.maximum(t - half, 0)).astype(_F32)


def _pool_forward(gi, proj, y, pool_w, pool_scale, l, heads, kv_heads, tp, name):
    t = proj.shape[0]
    n = t - l
    pg = pool_w.shape[-1]
    w = POOL_WINDOWS[gi]
    aw, kw = heads * HEAD_DIM, kv_heads * HEAD_DIM
    u_col = (2 * aw + 2 * kw) // pg + gi
    gate_col = (2 * aw + 2 * kw + len(POOL_WINDOWS) * pg) // pg + gi

    def body(u_ref, up_ref, un_ref, g_ref, w_ref, sc_ref, y_in_ref, y_ref, raw_ref, d_ref):
        del y_in_ref
        t0 = pl.program_id(0) * tp
        cur = u_ref[...]
        win = _window_sum(_with_halo(cur, up_ref[...], un_ref[...], t0, n), w, False)[POOL_HALO:POOL_HALO + tp]
        tok = t0 + lax.broadcasted_iota(jnp.int32, (tp, 1), 0)
        d = (win / _window_count(tok, w, n) - cur).astype(_BF16)
        raw = jnp.dot(d, w_ref[...].reshape(pg, pg), preferred_element_type=_F32)
        d_ref[...] = d
        raw_ref[...] = raw
        y_ref[...] = ((raw * sc_ref[...]) * _silu(g_ref[...])).astype(_BF16)

    blk = pl.BlockSpec((tp, pg), lambda i: (i, 0))
    return pl.pallas_call(
        body, name=name, grid=(n // tp,),
        in_specs=_halo_specs(tp, pg, u_col, 0, t) + [
            pl.BlockSpec((tp, pg), lambda i: (i, gate_col)),
            pl.BlockSpec((N_DEV, 1, pg // N_DEV, pg), lambda i: (0, gi, 0, 0)),
            pl.BlockSpec((1, pg), lambda i: (0, gi)), _ANY],
        out_specs=[pl.BlockSpec((tp, pg), lambda i: (i, aw // pg + gi)), blk, blk],
        out_shape=[jax.ShapeDtypeStruct(y.shape, y.dtype), jax.ShapeDtypeStruct((n, pg), _F32),
                   jax.ShapeDtypeStruct((n, pg), _BF16)],
        input_output_aliases={6: 0},
        compiler_params=pltpu.CompilerParams(dimension_semantics=("arbitrary",)),
    )(proj, proj, proj, proj, pool_w, pool_scale, y)


def _pool_backward_gate(gi, dy, proj, raw, pool_w, pool_scale, dproj, l, heads, kv_heads, tp, name):
    n, pg = raw.shape
    aw, kw = heads * HEAD_DIM, kv_heads * HEAD_DIM
    gate_col = (2 * aw + 2 * kw + len(POOL_WINDOWS) * pg) // pg + gi

    def body(dy_ref, g_ref, raw_ref, w_ref, sc_ref, dproj_ref, dg_ref, dr_ref, dd_ref, ds_ref):
        del dproj_ref

        @pl.when(pl.program_id(0) == 0)
        def _():
            ds_ref[...] = jnp.zeros_like(ds_ref)

        gate, rawv, dyv, scale = g_ref[...], raw_ref[...], dy_ref[...], sc_ref[...]
        dpool = dyv * _silu(gate)
        dg_ref[...] = (dyv * (rawv * scale) * _silu_grad(gate)).astype(_BF16)
        ds_ref[0:1, :] += jnp.sum(dpool * rawv, axis=0, keepdims=True)
        draw = (dpool * scale).astype(_BF16)
        dr_ref[...] = draw
        dd_ref[...] = lax.dot_general(
            draw, w_ref[...].reshape(pg, pg), (((1,), (1,)), ((), ())), preferred_element_type=_F32)

    blk = pl.BlockSpec((tp, pg), lambda i: (i, 0))
    gate = pl.BlockSpec((tp, pg), lambda i: (i, gate_col))
    return pl.pallas_call(
        body, name=name, grid=(n // tp,),
        in_specs=[pl.BlockSpec((tp, pg), lambda i: (i, aw // pg + gi)), gate, blk,
                  pl.BlockSpec((N_DEV, 1, pg // N_DEV, pg), lambda i: (0, gi, 0, 0)),
                  pl.BlockSpec((1, pg), lambda i: (0, gi)), _ANY],
        out_specs=[gate, blk, blk, pl.BlockSpec((SUBLANES, pg), lambda i: (0, 0))],
        out_shape=[jax.ShapeDtypeStruct(dproj.shape, dproj.dtype), jax.ShapeDtypeStruct((n, pg), _BF16),
                   jax.ShapeDtypeStruct((n, pg), _F32), jax.ShapeDtypeStruct((SUBLANES, pg), _F32)],
        input_output_aliases={5: 0},
        compiler_params=pltpu.CompilerParams(dimension_semantics=("arbitrary",)),
    )(dy, proj, raw, pool_w, pool_scale, dproj)


def _pool_backward_window(gi, dd, dproj, l, col, tp, name):
    n, pg = dd.shape
    w = POOL_WINDOWS[gi]

    def body(c_ref, p_ref, n_ref, dproj_ref, du_ref):
        del dproj_ref
        t0 = pl.program_id(0) * tp
        cur = c_ref[...]
        e = _with_halo(cur, p_ref[...], n_ref[...], t0, n)
        tok = t0 - POOL_HALO + lax.broadcasted_iota(jnp.int32, (tp + 2 * POOL_HALO, 1), 0)
        e = e / jnp.maximum(_window_count(tok, w, n), 1.0)
        du_ref[...] = (_window_sum(e, w, True)[POOL_HALO:POOL_HALO + tp] - cur).astype(_BF16)

    return pl.pallas_call(
        body, name=name, grid=(n // tp,),
        in_specs=_halo_specs(tp, pg, 0, 0, n) + [_ANY],
        out_specs=pl.BlockSpec((tp, pg), lambda i: (i, col)),
        out_shape=jax.ShapeDtypeStruct(dproj.shape, dproj.dtype), input_output_aliases={3: 0},
        compiler_params=pltpu.CompilerParams(dimension_semantics=("arbitrary",)),
    )(dd, dd, dd, dproj)


def _post(out, x, target, gate, g_post, tr, name):
    n, d = out.shape

    def body(o_ref, x_ref, t_ref, gate_ref, g_ref, dxn_ref, do_ref, dgate_ref, dg_ref, loss_ref):
        @pl.when(pl.program_id(0) == 0)
        def _():
            dgate_ref[...] = jnp.zeros_like(dgate_ref)
            dg_ref[...] = jnp.zeros_like(dg_ref)
            loss_ref[...] = jnp.zeros_like(loss_ref)

        ov = o_ref[...]
        r = _rms(ov)
        on = ov * r
        normed = on * g_ref[...]
        err = (x_ref[...] + gate_ref[...] * normed) - t_ref[...]
        loss_ref[...] += jnp.sum(err * err)
        dxn = err / d
        dxn_ref[...] = dxn
        dgate_ref[0:1, :] += jnp.sum(dxn * normed, axis=0, keepdims=True)
        dr = dxn * gate_ref[...]
        dg_ref[0:1, :] += jnp.sum(dr * on, axis=0, keepdims=True)
        dn = dr * g_ref[...]
        do_ref[...] = (r * (dn - on * jnp.mean(dn * on, axis=-1, keepdims=True))).astype(_BF16)

    blk = pl.BlockSpec((tr, d), lambda i: (i, 0))
    vec = pl.BlockSpec((1, d), lambda i: (0, 0))
    acc = pl.BlockSpec((SUBLANES, d), lambda i: (0, 0))
    return pl.pallas_call(
        body, name=name, grid=(n // tr,),
        in_specs=[blk, blk, blk, vec, vec],
        out_specs=[blk, blk, acc, acc, pl.BlockSpec((SUBLANES, LANES), lambda i: (0, 0))],
        out_shape=[jax.ShapeDtypeStruct((n, d), _F32), jax.ShapeDtypeStruct((n, d), _BF16),
                   jax.ShapeDtypeStruct((SUBLANES, d), _F32), jax.ShapeDtypeStruct((SUBLANES, d), _F32),
                   jax.ShapeDtypeStruct((SUBLANES, LANES), _F32)],
        compiler_params=pltpu.CompilerParams(dimension_semantics=("arbitrary",)),
    )(out, x, target, gate, g_post)


def _adam_sharded(slab_ids, grad, got, far, w, m, v, name):
    r, c = w.shape
    tr = _tile(r, max(BF16_ROWS, min(256, (1 << 18) // c)), BF16_ROWS)

    def body(ids_ref, own_ref, got_ref, far_ref, w_ref, m_ref, v_ref, g_ref, dl_ref, nm_ref, nv_ref):
        del ids_ref
        g = own_ref[0] + got_ref[0]
        for k in range(N_CHIPS - 1):
            g = g + far_ref[k].astype(_F32)
        delta, nm, nv = _adamw(w_ref[...], g, m_ref[...], v_ref[...])
        g_ref[...] = g
        dl_ref[...] = delta
        nm_ref[...] = nm
        nv_ref[...] = nv

    blk = pl.BlockSpec((tr, c), lambda i, ids: (i, 0))
    return pl.pallas_call(
        body, name=name,
        grid_spec=pltpu.PrefetchScalarGridSpec(
            num_scalar_prefetch=1, grid=(r // tr,),
            in_specs=[pl.BlockSpec((1, tr, c), lambda i, ids: (ids[0], i, 0)),
                      pl.BlockSpec((1, tr, c), lambda i, ids: (0, i, 0)),
                      pl.BlockSpec((N_CHIPS - 1, tr, c), lambda i, ids: (0, i, 0)), blk, blk, blk],
            out_specs=[blk] * 4),
        out_shape=[jax.ShapeDtypeStruct((r, c), _F32)] * 4,
        compiler_params=pltpu.CompilerParams(dimension_semantics=("parallel",)),
    )(slab_ids, grad, got, far, w, m, v)


def _adam_replicated(parts, extra, through_silu, w, m, v, name):
    def body(p_ref, e_ref, s_ref, w_ref, m_ref, v_ref, g_ref, dl_ref, nm_ref, nv_ref):
        total = p_ref[0] + e_ref[0]
        for dev in range(1, N_DEV):
            total = total + (p_ref[dev] + e_ref[dev])
        g = jnp.where(s_ref[...] > 0.5, total * _silu_grad(w_ref[...]), total)
        delta, nm, nv = _adamw(w_ref[...], g, m_ref[...], v_ref[...])
        g_ref[...] = g
        dl_ref[...] = delta
        nm_ref[...] = nm
        nv_ref[...] = nv

    return pl.pallas_call(
        body, name=name, in_specs=[_VMEM] * 6, out_specs=[_VMEM] * 4,
        out_shape=[jax.ShapeDtypeStruct(w.shape, _F32)] * 4,
    )(parts, extra, through_silu, w, m, v)


def _as_rows(vec):
    size = vec.shape[0]
    padded = -(-size // (SUBLANES * LANES)) * SUBLANES * LANES
    return jnp.pad(vec, (0, padded - size)).reshape(padded // LANES, LANES)


def kernel(x, c, ctx, c_ctx, w_ada, b_ada, norm_pre, norm_post, w_in, q_norm, k_norm, pool_w, pool_scale, w_out, loss_target, m_c_ctx, m_w_ada, m_b_ada, m_norm_pre, m_norm_post, m_w_in, m_q_norm, m_k_norm, m_pool_w, m_pool_scale, m_w_out, v_c_ctx, v_w_ada, v_b_ada, v_norm_pre, v_norm_post, v_w_in, v_q_norm, v_k_norm, v_pool_w, v_pool_scale, v_w_out):
    me = _dev_index(*_position())
    x2, ctx2, target = x[0], ctx[0], loss_target[0]
    n, d = x2.shape
    l = ctx2.shape[0]
    t = l + n
    aw = d // 2
    heads = aw // HEAD_DIM
    kv_heads = heads // GQA_GROUP
    kw = kv_heads * HEAD_DIM
    n_groups = len(POOL_WINDOWS)
    pg = (d - aw) // n_groups
    mix = d
    tr = _tile(l, 128, BF16_ROWS)
    tq = _tile(l, 128, BF16_ROWS)
    tp = _tile(n, 1024, POOL_HALO)

    xi, yi, ci = _position()
    slab_ids = jnp.stack([_dev_index(*chip, ci) for chip in _chip_order(xi, yi)]).astype(jnp.int32)

    cw = w_in.shape[-1]
    wg = _cast_into_columns(slab_ids, w_in[0], N_DEV, "cast_w_in")
    late = [_cast_into_slab(slab_ids, w_out[0], "cast_w_out"),
            _cast_into_slab(slab_ids, pool_w[0].reshape(-1, pg), "cast_pool_w")]

    c_all = _all_gather_small(_as_rows(c[0]), "gather_c").reshape(N_DEV, -1)[:, :d]
    craw = jnp.concatenate([c_all, jnp.broadcast_to(c_ctx[None], (N_DEV, d))], axis=0)
    ada = _ada_forward(craw, w_ada[0], "ada_forward")
    ada_all = _all_gather_small(ada, "gather_ada")
    mod_all = ada_all.transpose(1, 0, 2).reshape(ada.shape[0], -1) + b_ada[0]
    mod = lax.dynamic_index_in_dim(mod_all, me, 0, keepdims=False)
    mod_c = mod_all[N_DEV]
    shift, scale, gate = mod[:d], mod[d:2 * d], mod[2 * d:]
    zeros6 = jnp.zeros((SUBLANES - 2, d), _F32)
    mods = jnp.stack([jnp.concatenate([mod_c[None, d:2 * d], mod_c[None, :d], zeros6], axis=0),
                      jnp.concatenate([scale[None], shift[None], zeros6], axis=0)])

    a_s, a_r, wg, tok = _w_in_hop(wg, cw, [], "a", ada_all, "gather_w_in_a")
    h_all = _prenorm(ctx2, x2, norm_pre, mods + tok[0, 0], tr, "prenorm")
    order_ids = jnp.stack([_dev_index(*dev) for dev in _w_in_order(xi, yi, ci)]).astype(jnp.int32)
    proj = lax.empty((t, N_DEV * cw), _F32)
    proj = _proj_blocks(h_all, wg, proj, order_ids, 0, 1, cw, tok, "proj_0")
    b_s, b_r, wg, tok = _w_in_hop(wg, cw, [("a", a_s, a_r, [0, 1], [])], "b", proj, "gather_w_in_b")
    proj = _proj_blocks(h_all, wg, proj, order_ids, 1, 2, cw, tok, "proj_1")
    c_s, c_r, wg, tok = _w_in_hop(wg, cw, [("b", b_s, b_r, [2, 0], [])], "c", proj, "gather_w_in_c")
    proj = _proj_blocks(h_all, wg, proj, order_ids, 3, 2, cw, tok, "proj_2")
    d_s, d_r, wg, tok = _w_in_hop(
        wg, cw, [("c", c_s, c_r, [0], []), ("b", b_s, b_r, [1], [])], "d", proj, "gather_w_in_d")
    proj = _proj_blocks(h_all, wg, proj, order_ids, 5, 2, cw, tok, "proj_3")
    w_in_g, tok = _w_in_hop(
        wg, cw, [("d", d_s, d_r, [0], [0]), ("a", a_s, a_r, [], [0, 1]), ("b", b_s, b_r, [], [0, 1, 2]),
                 ("c", c_s, c_r, [], [0])], None, proj, "gather_w_in_end")
    flight_w = _gather_slabs_start(late, w_in_g, "gather_late_start")
    proj = _proj_blocks(h_all, w_in_g, proj, order_ids, 7, 1, cw, flight_w[-1], "proj_4")
    tables = _rope_tables(l, n)
    q, k, v = _qkv_post(proj, tables, q_norm, k_norm, heads, kv_heads, tr, "qkv_post")
    attn_o, y, lse = _attention(q, k, v, proj, l, mix, _tile(l, 256, BF16_ROWS), "attention")
    w_out_g8, pool_g8 = _gather_slabs_wait(*flight_w[:3], attn_o, "gather_late_wait")
    w_out_g = w_out_g8.reshape(mix, d)
    pool_g = pool_g8.reshape(N_DEV, n_groups, pg // N_DEV, pg)
    raws, ds = [], []
    for gi in range(n_groups):
        y, raw, dsave = _pool_forward(gi, proj, y, pool_g, pool_scale, l, heads, kv_heads, tp, f"pool_forward_{gi}")
        raws.append(raw)
        ds.append(dsave)
    out = _matmul(y, w_out_g, name="out_proj")
    dxn, dout, dgate8, dgpost8, loss8 = _post(out, x2, target, gate[None], norm_post, tr, "post")

    gw_out = _matmul(y, dout, ta=True, name="grad_w_out").reshape(N_DEV, mix // N_DEV, d)
    flight_so = _exchange_start(_sibling_copies_by_device, [gw_out], "exchange_sibling_start_w_out", land_slabs=N_CHIPS)
    dy = _matmul(dout, w_out_g, tb=True, after=flight_so[-1], name="d_y")
    gw_out, got_out = _exchange_wait(
        _sibling_copies_by_device, *flight_so[:4], dy, "exchange_sibling_wait_w_out", with_sources=True)
    sum_out = _pre_add(slab_ids, gw_out, got_out, "pre_add_w_out")
    flight_out = _exchange_start(_chip_copies, [sum_out], "exchange_chips_start_w_out")
    w3 = aw + 2 * kw
    dq, dproj, dk, dv = _attention_backward(
        q, k, v, attn_o, dy, proj, lse, flight_out[-1], lax.empty(proj.shape, _BF16), l, tq, "attention_backward")
    dproj, dgq8, dgk8 = _qkv_post_backward(proj, dq, dk, dv, tables, q_norm, k_norm, dproj, l, tr, "qkv_post_backward")
    dproj = _zero_context_rows(dproj, l, w3, tr, "zero_context_rows")
    gpw, dps8 = [], []
    for gi in range(n_groups):
        dproj, draw, dd, dps = _pool_backward_gate(
            gi, dy, proj, raws[gi], pool_g, pool_scale, dproj, l, heads, kv_heads, tp, f"pool_backward_gate_{gi}")
        dproj = _pool_backward_window(gi, dd, dproj, l, (w3 + aw) // pg + gi, tp, f"pool_backward_window_{gi}")
        dps8.append(dps)
        gpw.append(_matmul(ds[gi], draw, ta=True, name=f"grad_pool_w_{gi}"))
    cw = w_in.shape[-1]
    other_ids = jnp.stack([_dev_index(*chip, 1 - ci) for chip in _chip_order(xi, yi)]).astype(jnp.int32)
    chip_slabs = jnp.arange(N_CHIPS, dtype=jnp.int32)
    pr = pool_w.shape[2]
    gpw8 = jnp.stack(gpw).reshape(n_groups, N_DEV, pr, pg).transpose(1, 0, 2, 3).reshape(N_DEV, n_groups * pr, pg)
    give_in = _matmul_slabs(h_all, dproj, other_ids, cw, "grad_w_in_sibling")
    flight_sib = _exchange_start(_sibling_copies, [give_in, jnp.take(gpw8, other_ids, axis=0)], "exchange_sibling_start")
    gw_in = _matmul_slabs(h_all, dproj, slab_ids, cw, "grad_w_in_own", after=flight_sib[-1])
    gpw_own = jnp.take(gpw8, slab_ids, axis=0)
    got_in, got_pw = _exchange_wait(_sibling_copies, *flight_sib[:4], gw_in, "exchange_sibling_wait")
    sums_in = [_pre_add(chip_slabs, gw_in, got_in, "pre_add_w_in"), _pre_add(chip_slabs, gpw_own, got_pw, "pre_add_pool_w")]
    flight_in = _exchange_start(_chip_copies, sums_in, "exchange_chips_start_w_in")
    dh = _matmul(dproj, w_in_g, tb=True, tm=1088, tk=3072, after=flight_in[-1], name="d_h")
    grad_x, dmods, dgpre8 = _prenorm_backward(dh, ctx2, x2, dxn, norm_pre, mods, tr, "prenorm_backward")

    dmod_lat = jnp.concatenate([dmods[1, 1], dmods[1, 0], dgate8[0]])
    dmod_ctx = jnp.concatenate([dmods[0, 1], dmods[0, 0], jnp.zeros((d,), _F32)])
    small = jnp.concatenate([dmod_lat, dmod_ctx, dgpre8[0], dgpost8[0], dgq8[0], dgk8[0]] + [p[0] for p in dps8]
                            + [loss8[0, :1]])
    gathered = _all_gather_small(_as_rows(small), "gather_small").reshape(N_DEV, -1)
    o = 0
    take = lambda size: (gathered[:, o:o + size], o + size)
    g_mod, o = take(3 * d)
    g_modc, o = take(3 * d)
    g_pre, o = take(d)
    g_post, o = take(d)
    g_q, o = take(HEAD_DIM)
    g_k, o = take(HEAD_DIM)
    g_ps, o = take(n_groups * pg)
    g_loss, o = take(1)
    cols = w_ada.shape[-1]
    mine = lambda a: lax.dynamic_slice_in_dim(a, me * cols, cols, axis=1)
    dmod_rows = jnp.concatenate([mine(g_mod), mine(g_modc)], axis=0)
    g_wada, dl_wada, nm_wada, nv_wada, dcact = _ada_backward(craw, dmod_rows, w_ada[0], m_w_ada[0], v_w_ada[0], "ada_backward")
    dcc = _all_gather_small(_as_rows(dcact[0]), "gather_dcc").reshape(N_DEV, -1)[:, :d]

    sizes = [d, 3 * d, d, d, HEAD_DIM, HEAD_DIM, n_groups * pg]
    def pack(parts):
        rows = jnp.concatenate(parts, axis=1)
        padded = -(-rows.shape[1] // (SUBLANES * LANES)) * SUBLANES * LANES
        return jnp.pad(rows, ((0, 0), (0, padded - rows.shape[1]))).reshape(N_DEV, padded // LANES, LANES)

    zero = lambda size: jnp.zeros((N_DEV, size), _F32)
    parts = pack([dcc, g_mod, g_pre, g_post, g_q, g_k, g_ps])
    extra = pack([zero(d), g_modc, zero(d), zero(d), zero(HEAD_DIM), zero(HEAD_DIM), zero(n_groups * pg)])
    through_silu = _as_rows(jnp.concatenate([jnp.ones((d,), _F32), jnp.zeros((sum(sizes[1:]),), _F32)]))
    cat = lambda items: _as_rows(jnp.concatenate([a.reshape(-1) for a in items]))
    ws = [c_ctx, b_ada, norm_pre, norm_post, q_norm, k_norm, pool_scale]
    ms = [m_c_ctx, m_b_ada, m_norm_pre, m_norm_post, m_q_norm, m_k_norm, m_pool_scale]
    vs = [v_c_ctx, v_b_ada, v_norm_pre, v_norm_post, v_q_norm, v_k_norm, v_pool_scale]
    rep = _adam_replicated(parts, extra, through_silu, cat(ws), cat(ms), cat(vs), "adam_replicated")

    def split(packed):
        flat_, outs, at = packed.reshape(-1), [], 0
        for w, size in zip(ws, sizes):
            outs.append(flat_[at:at + size].reshape(w.shape))
            at += size
        return outs

    g_rep, dl_rep, nm_rep, nv_rep = [split(r) for r in rep]

    far_out = _exchange_wait(_chip_copies, *flight_out[:4], grad_x, "exchange_chips_wait_w_out")[0]
    far_in, far_pw = _exchange_wait(_chip_copies, *flight_in[:4], rep[0], "exchange_chips_wait_w_in")
    two = lambda a: a.reshape(-1, a.shape[-1])
    sharded = []
    for ids, g, got, far, w, m, v_, name in zip(
            (chip_slabs, slab_ids, chip_slabs), (gw_in, gw_out, gpw_own), (got_in, got_out, got_pw),
            (far_in, far_out, far_pw), (w_in, w_out, pool_w), (m_w_in, m_w_out, m_pool_w),
            (v_w_in, v_w_out, v_pool_w), ("adam_w_in", "adam_w_out", "adam_pool_w")):
        res = _adam_sharded(ids, g, got, far, two(w), two(m), two(v_), name)
        sharded.append([r.reshape(w.shape) for r in res])
    (g_w_in, dl_w_in, nm_w_in, nv_w_in), (g_w_out, dl_w_out, nm_w_out, nv_w_out), (g_pw, dl_pw, nm_pw, nv_pw) = sharded

    loss_sum = g_loss[0, 0]
    for dev in range(1, N_DEV):
        loss_sum = loss_sum + g_loss[dev, 0]
    loss = (0.5 / d) * loss_sum

    def ordered(rep_list, ada_, w_in_, pw_, w_out_):
        return [rep_list[0], ada_[None], rep_list[1], rep_list[2], rep_list[3], w_in_, rep_list[4], rep_list[5],
                pw_, rep_list[6], w_out_]

    return (loss, grad_x[None],
            *ordered(g_rep, g_wada, g_w_in, g_pw, g_w_out),
            *ordered(dl_rep, dl_wada, dl_w_in, dl_pw, dl_w_out),
            *ordered(nm_rep, nm_wada, nm_w_in, nm_pw, nm_w_out),
            *ordered(nv_rep, nv_wada, nv_w_in, nv_pw, nv_w_out))
```

```python
import jax
import jax.numpy as jnp
from jax import lax
from jax.experimental import pallas as pl
from jax.experimental.pallas import tpu as pltpu

HEAD_DIM = 128
GQA_GROUP = 4
ATTN_SUB_HEADS = 1
ATTN_BWD_SUB_HEADS = 1
ATTN_KEY_PARTS = 2
LOG2_E = 1.4426950408889634
GRID_W = 64
ROPE_PAIRS = HEAD_DIM // 4
ROPE_THETA = 10000.0
ATTN_SCALE = HEAD_DIM ** -0.5
EPS = 1e-6
POOL_WINDOWS = (2, 4, 8, 16)
POOL_HALO = 8
N_DEV = 8
N_CHIPS = 4
ADAM_LR = 0.001
ADAM_B1 = 0.9
ADAM_B2 = 0.999
ADAM_EPS = 1e-08
ADAM_WD = 0.01
ADAM_STEP = 10

LANES = 128
SUBLANES = 8
BF16_ROWS = 16

_MESH = pl.DeviceIdType.MESH
_ANY = pl.BlockSpec(memory_space=pl.ANY)
_VMEM = pl.BlockSpec(memory_space=pltpu.VMEM)
_HBM = pl.BlockSpec(memory_space=pltpu.HBM)
_SEM = pl.BlockSpec(memory_space=pltpu.SEMAPHORE)
_EFFECT = pltpu.SideEffectType.DATAFLOW_SIDE_EFFECTING
_F32 = jnp.float32
_BF16 = jnp.bfloat16


def _tile(dim, pref, align):
    t = min(pref, dim)
    t -= t % align
    while t >= align:
        if dim % t == 0:
            return t
        t -= align
    return dim


def _position():
    return lax.axis_index("x"), lax.axis_index("y"), lax.axis_index("c")


def _flip(v, bit):
    return 1 - v if bit else v


def _dev_index(x, y, c):
    return 4 * x + 2 * y + c


def _silu(g):
    return g * jax.nn.sigmoid(g)


def _silu_grad(g):
    s = jax.nn.sigmoid(g)
    return s * (1.0 + g * (1.0 - s))


def _adamw(w, g, m, v):
    m = ADAM_B1 * m + (1.0 - ADAM_B1) * g
    v = ADAM_B2 * v + (1.0 - ADAM_B2) * (g * g)
    m_hat = m / (1.0 - ADAM_B1 ** ADAM_STEP)
    v_hat = v / (1.0 - ADAM_B2 ** ADAM_STEP)
    delta = -ADAM_LR * (m_hat / (jnp.sqrt(v_hat) + ADAM_EPS) + ADAM_WD * w)
    return delta, m, v


def _all_gather_small(v, name):
    rows, cols = v.shape

    def body(v_ref, out_ref, send_sems, recv_sems):
        x, y, c = _position()
        me = _dev_index(x, y, c)
        out_ref[me] = v_ref[...]
        peers = [(_flip(x, k & 4), _flip(y, k & 2), _flip(c, k & 1)) for k in range(1, N_DEV)]

        def copy(k, block, to):
            return pltpu.make_async_remote_copy(
                src_ref=v_ref, dst_ref=out_ref.at[block], send_sem=send_sems.at[k], recv_sem=recv_sems.at[k],
                device_id=to, device_id_type=_MESH)

        sends = [copy(k, me, p) for k, p in enumerate(peers)]
        for s in sends:
            s.start()
        for k, p in enumerate(peers):
            copy(k, _dev_index(*p), p).wait_recv()
        for s in sends:
            s.wait_send()

    return pl.pallas_call(
        body, name=name,
        out_shape=jax.ShapeDtypeStruct((N_DEV, rows, cols), v.dtype),
        in_specs=[_VMEM], out_specs=_VMEM,
        scratch_shapes=[pltpu.SemaphoreType.DMA((N_DEV - 1,)), pltpu.SemaphoreType.DMA((N_DEV - 1,))],
    )(v)


def _route(x, y, c):
    first = (x + (1 - c) * (1 - 2 * x), y + c * (1 - 2 * y))
    second = (x + c * (1 - 2 * x), y + (1 - c) * (1 - 2 * y))
    return first, second, (1 - x, 1 - y)


def _w_in_order(x, y, c):
    first, second, diagonal = _route(x, y, c)
    return [(x, y, c), (x, y, 1 - c), (*first, c), (*second, 1 - c), (*second, c), (*first, 1 - c),
            (*diagonal, c), (*diagonal, 1 - c)]


W_IN_HOPS = {"a": 2, "b": 3, "c": 1, "d": 1}


def _w_in_hop_copies(group, wg, width, send_sems, recv_sems):
    x, y, c = _position()
    me, sibling = (x, y, c), (x, y, 1 - c)
    first, second, diagonal = _route(x, y, c)

    def cp(k, block, to):
        cols = wg.at[:, pl.ds(pl.multiple_of(_dev_index(*block) * width, width), width)]
        return pltpu.make_async_remote_copy(
            src_ref=cols, dst_ref=cols, send_sem=send_sems.at[k], recv_sem=recv_sems.at[k],
            device_id=to, device_id_type=_MESH)

    if group == "a":
        return [cp(0, me, sibling), cp(1, me, (*first, c))]
    if group == "b":
        return [cp(0, me, (*second, c)), cp(1, (*first, c), (*second, c)), cp(2, (*first, c), sibling)]
    if group == "c":
        return [cp(0, (*second, c), sibling)]
    return [cp(0, (*diagonal, c), sibling)]


def _w_in_hop(wg, width, waits, start, after, name):
    n_sem = 2 * len(waits)

    def body(*refs):
        wg_ref = refs[0]
        for i, (group, _, _, arrivals, sends) in enumerate(waits):
            cps = _w_in_hop_copies(group, wg_ref, width, refs[1 + 2 * i], refs[2 + 2 * i])
            for k in arrivals:
                cps[k].wait_recv()
            for k in sends:
                cps[k].wait_send()
        if start:
            for cp in _w_in_hop_copies(start, wg_ref, width, refs[n_sem + 2], refs[n_sem + 3]):
                cp.start()
        refs[-1][...] = jnp.zeros_like(refs[-1])

    sems = [s for w in waits for s in w[1:3]]
    new = [pltpu.SemaphoreType.DMA((W_IN_HOPS[start],))] * 2 if start else []
    outs = pl.pallas_call(
        body, name=name,
        out_shape=(*new, pltpu.HBM(wg.shape, wg.dtype), jax.ShapeDtypeStruct((SUBLANES, LANES), _F32)),
        in_specs=[_HBM] + [_SEM] * n_sem + [_ANY], out_specs=(*[_SEM] * len(new), _HBM, _VMEM),
        input_output_aliases={0: len(new)},
        compiler_params=pltpu.CompilerParams(has_side_effects=_EFFECT),
    )(pltpu.with_memory_space_constraint(wg, pltpu.HBM), *sems, after)
    return outs


def _slab_copies(bufs, send_sems, recv_sems):
    x, y, c = _position()
    me = _dev_index(x, y, c)
    peers = [(_flip(x, k & 4), _flip(y, k & 2), _flip(c, k & 1)) for k in range(1, N_DEV)]
    return [pltpu.make_async_remote_copy(
        src_ref=buf.at[me], dst_ref=buf.at[me],
        send_sem=send_sems.at[(N_DEV - 1) * a + k], recv_sem=recv_sems.at[(N_DEV - 1) * a + k],
        device_id=peer, device_id_type=_MESH)
        for a, buf in enumerate(bufs) for k, peer in enumerate(peers)]


def _gather_slabs_start(bufs, after, name):
    n = len(bufs)
    n_copies = (N_DEV - 1) * n

    def body(*refs):
        send_sems, recv_sems, token = refs[n + 1], refs[n + 2], refs[-1]
        for cp in _slab_copies(refs[:n], send_sems, recv_sems):
            cp.start()
        token[...] = jnp.zeros_like(token)

    outs = pl.pallas_call(
        body, name=name,
        out_shape=(pltpu.SemaphoreType.DMA((n_copies,)), pltpu.SemaphoreType.DMA((n_copies,)),
                   *[pltpu.HBM(b.shape, b.dtype) for b in bufs], jax.ShapeDtypeStruct((SUBLANES, LANES), _F32)),
        in_specs=[_HBM] * n + [_ANY], out_specs=(_SEM, _SEM, *[_HBM] * n, _VMEM),
        input_output_aliases={i: 2 + i for i in range(n)},
        compiler_params=pltpu.CompilerParams(has_side_effects=_EFFECT),
    )(*[pltpu.with_memory_space_constraint(b, pltpu.HBM) for b in bufs], after)
    return outs[0], outs[1], list(outs[2:2 + n]), outs[-1]


def _gather_slabs_wait(send_sems, recv_sems, bufs, after, name):
    n = len(bufs)

    def body(*refs):
        for cp in _slab_copies(refs[:n], refs[n], refs[n + 1]):
            cp.wait_send()
            cp.wait_recv()

    outs = pl.pallas_call(
        body, name=name, out_shape=tuple(pltpu.HBM(b.shape, b.dtype) for b in bufs),
        in_specs=[_HBM] * n + [_SEM, _SEM, _ANY], out_specs=[_HBM] * n,
        input_output_aliases={i: i for i in range(n)},
        compiler_params=pltpu.CompilerParams(has_side_effects=_EFFECT),
    )(*bufs, send_sems, recv_sems, after)
    return list(outs)


def _chip_order(x, y):
    return [(x, y), (1 - x, y), (x, 1 - y), (1 - x, 1 - y)]


def _chip_copies(srcs, lands, send_sems, recv_sems):
    x, y, c = _position()
    return [pltpu.make_async_remote_copy(
        src_ref=srcs[a].at[k], dst_ref=lands[a].at[k],
        send_sem=send_sems.at[(N_CHIPS - 1) * a + k], recv_sem=recv_sems.at[(N_CHIPS - 1) * a + k],
        device_id=(*chip, c), device_id_type=_MESH)
        for a in range(len(srcs)) for k, chip in enumerate(_chip_order(x, y)[1:])]


def _sibling_copies(srcs, lands, send_sems, recv_sems):
    x, y, c = _position()
    return [pltpu.make_async_remote_copy(
        src_ref=srcs[a].at[s], dst_ref=lands[a].at[s],
        send_sem=send_sems.at[N_CHIPS * a + s], recv_sem=recv_sems.at[N_CHIPS * a + s],
        device_id=(x, y, 1 - c), device_id_type=_MESH)
        for a in range(len(srcs)) for s in range(N_CHIPS)]


def _sibling_copies_by_device(srcs, lands, send_sems, recv_sems):
    x, y, c = _position()
    return [pltpu.make_async_remote_copy(
        src_ref=srcs[a].at[_dev_index(*chip, 1 - c)], dst_ref=lands[a].at[s],
        send_sem=send_sems.at[N_CHIPS * a + s], recv_sem=recv_sems.at[N_CHIPS * a + s],
        device_id=(x, y, 1 - c), device_id_type=_MESH)
        for a in range(len(srcs)) for s, chip in enumerate(_chip_order(x, y))]


def _exchange_start(copies, sums, name, land_slabs=None):
    n = len(sums)
    land_shapes = [((land_slabs or s.shape[0]),) + s.shape[1:] for s in sums]
    n_copies = sum(shape[0] for shape in land_shapes)

    def body(*refs):
        srcs, lands = refs[:n], refs[n:2 * n]
        send_sems, recv_sems, token = refs[2 * n], refs[2 * n + 1], refs[-1]
        for cp in copies(srcs, lands, send_sems, recv_sems):
            cp.start()
        token[...] = jnp.zeros_like(token)

    hbm = [pltpu.HBM(s.shape, s.dtype) for s in sums] + [pltpu.HBM(shape, s.dtype) for shape, s in zip(land_shapes, sums)]
    outs = pl.pallas_call(
        body, name=name,
        out_shape=(pltpu.SemaphoreType.DMA((n_copies,)), pltpu.SemaphoreType.DMA((n_copies,)), *hbm,
                   jax.ShapeDtypeStruct((SUBLANES, LANES), _F32)),
        in_specs=[_HBM] * (2 * n), out_specs=(_SEM, _SEM, *[_HBM] * (2 * n), _VMEM),
        input_output_aliases={i: 2 + i for i in range(2 * n)},
        compiler_params=pltpu.CompilerParams(has_side_effects=_EFFECT),
    )(*[pltpu.with_memory_space_constraint(s, pltpu.HBM) for s in sums],
      *[pltpu.with_memory_space_constraint(lax.empty(shape, s.dtype), pltpu.HBM) for shape, s in zip(land_shapes, sums)])
    return outs[0], outs[1], list(outs[2:2 + n]), list(outs[2 + n:2 + 2 * n]), outs[-1]


def _exchange_wait(copies, send_sems, recv_sems, srcs, lands, after, name, with_sources=False):
    n = len(srcs)

    def body(*refs):
        for cp in copies(refs[:n], refs[n:2 * n], refs[2 * n], refs[2 * n + 1]):
            cp.wait_send()
            cp.wait_recv()

    hbm = [pltpu.HBM(s.shape, s.dtype) for s in (*srcs, *lands)]
    outs = pl.pallas_call(
        body, name=name, out_shape=tuple(hbm),
        in_specs=[_HBM] * (2 * n) + [_SEM, _SEM, _ANY], out_specs=[_HBM] * (2 * n),
        input_output_aliases={i: i for i in range(2 * n)},
        compiler_params=pltpu.CompilerParams(has_side_effects=_EFFECT),
    )(*srcs, *lands, send_sems, recv_sems, after)
    return list(outs) if with_sources else list(outs[n:])


def _matmul(a, b, *, ta=False, tb=False, out_dtype=_F32, tm=1024, tn=1024, tk=4608, after=None, name):
    kdim, m = a.shape if ta else a.shape[::-1]
    n = b.shape[0] if tb else b.shape[1]
    tm = _tile(m, tm, LANES if ta else BF16_ROWS)
    tn = _tile(n, tn, LANES)
    tk = _tile(kdim, tk, BF16_ROWS if ta else LANES)
    nk = kdim // tk
    dims = (((0 if ta else 1,), (1 if tb else 0,)), ((), ()))

    def body_whole_k(a_ref, b_ref, *rest):
        rest[-1][...] = lax.dot_general(a_ref[...], b_ref[...], dims, preferred_element_type=_F32).astype(out_dtype)

    def body_split_k(a_ref, b_ref, *rest):
        o_ref, acc_ref = rest[-2:]
        k = pl.program_id(2)

        @pl.when(k == 0)
        def _():
            acc_ref[...] = jnp.zeros_like(acc_ref)

        acc_ref[...] += lax.dot_general(a_ref[...], b_ref[...], dims, preferred_element_type=_F32)

        @pl.when(k == nk - 1)
        def _():
            o_ref[...] = acc_ref[...].astype(out_dtype)

    a_spec = pl.BlockSpec((tk, tm), lambda i, j, k: (k, i)) if ta else pl.BlockSpec((tm, tk), lambda i, j, k: (i, k))
    b_spec = pl.BlockSpec((tn, tk), lambda i, j, k: (j, k)) if tb else pl.BlockSpec((tk, tn), lambda i, j, k: (k, j))
    extra = [] if after is None else [after]
    return pl.pallas_call(
        body_whole_k if nk == 1 else body_split_k, name=name, grid=(m // tm, n // tn, nk),
        in_specs=[a_spec, b_spec] + [pl.BlockSpec(t.shape, lambda i, j, k: (0, 0)) for t in extra],
        out_specs=pl.BlockSpec((tm, tn), lambda i, j, k: (i, j)), out_shape=jax.ShapeDtypeStruct((m, n), out_dtype),
        scratch_shapes=[] if nk == 1 else [pltpu.VMEM((tm, tn), _F32)],
        compiler_params=pltpu.CompilerParams(dimension_semantics=("parallel", "parallel", "arbitrary")),
    )(a, b, *extra)


def _proj_blocks(a, wg, dst, order_ids, first, count, width, after, name):
    m, kdim = a.shape
    tm = _tile(m, 1088, BF16_ROWS)

    def body(ids_ref, a_ref, w_ref, after_ref, dst_ref, o_ref):
        del ids_ref, after_ref, dst_ref
        o_ref[...] = jnp.dot(a_ref[...], w_ref[...], preferred_element_type=_F32)

    return pl.pallas_call(
        body, name=name,
        grid_spec=pltpu.PrefetchScalarGridSpec(
            num_scalar_prefetch=1, grid=(count, m // tm),
            in_specs=[pl.BlockSpec((tm, kdim), lambda j, i, ids: (i, 0)),
                      pl.BlockSpec((kdim, width), lambda j, i, ids: (0, ids[first + j])),
                      pl.BlockSpec(after.shape, lambda j, i, ids: (0, 0)), _ANY],
            out_specs=pl.BlockSpec((tm, width), lambda j, i, ids: (i, ids[first + j]))),
        out_shape=jax.ShapeDtypeStruct(dst.shape, dst.dtype),
        input_output_aliases={4: 0},
        compiler_params=pltpu.CompilerParams(dimension_semantics=("arbitrary", "arbitrary")),
    )(order_ids, a, wg, after, dst)


def _cast_into_columns(slab_ids, a, n_blocks, name):
    r, c = a.shape
    tr = _row_tile(r, c)

    def body(ids_ref, a_ref, o_ref):
        del ids_ref
        o_ref[...] = a_ref[...].astype(_BF16)

    return pl.pallas_call(
        body, name=name,
        grid_spec=pltpu.PrefetchScalarGridSpec(
            num_scalar_prefetch=1, grid=(r // tr,),
            in_specs=[pl.BlockSpec((tr, c), lambda i, ids: (i, 0))],
            out_specs=pl.BlockSpec((tr, c), lambda i, ids: (i, ids[0]))),
        out_shape=jax.ShapeDtypeStruct((r, n_blocks * c), _BF16),
        compiler_params=pltpu.CompilerParams(dimension_semantics=("parallel",)),
    )(slab_ids, a)


def _matmul_slabs(a, b, ids, width, name, after=None):
    kdim, m = a.shape
    n_slabs = ids.shape[0]
    tm = _tile(m, 1024, LANES)

    def body(ids_ref, a_ref, b_ref, *rest):
        del ids_ref
        rest[-1][0] = lax.dot_general(a_ref[...], b_ref[...], (((0,), (0,)), ((), ())), preferred_element_type=_F32)

    extra = [] if after is None else [after]
    return pl.pallas_call(
        body, name=name,
        grid_spec=pltpu.PrefetchScalarGridSpec(
            num_scalar_prefetch=1, grid=(m // tm, n_slabs),
            in_specs=[pl.BlockSpec((kdim, tm), lambda i, j, ids: (0, i)),
                      pl.BlockSpec((kdim, width), lambda i, j, ids: (0, ids[j]))]
            + [pl.BlockSpec(t.shape, lambda i, j, ids: (0, 0)) for t in extra],
            out_specs=pl.BlockSpec((1, tm, width), lambda i, j, ids: (j, i, 0))),
        out_shape=jax.ShapeDtypeStruct((n_slabs, m, width), _F32),
        compiler_params=pltpu.CompilerParams(dimension_semantics=("parallel", "parallel")),
    )(ids, a, b, *extra)


def _row_tile(rows, cols):
    return _tile(rows, max(BF16_ROWS, min(512, (1 << 19) // cols)), BF16_ROWS)


def _cast_into_slab(slab_ids, a, name):
    r, c = a.shape
    tr = _row_tile(r, c)

    def body(ids_ref, a_ref, o_ref):
        del ids_ref
        o_ref[0] = a_ref[...].astype(_BF16)

    return pl.pallas_call(
        body, name=name,
        grid_spec=pltpu.PrefetchScalarGridSpec(
            num_scalar_prefetch=1, grid=(r // tr,),
            in_specs=[pl.BlockSpec((tr, c), lambda i, ids: (i, 0))],
            out_specs=pl.BlockSpec((1, tr, c), lambda i, ids: (ids[0], i, 0))),
        out_shape=jax.ShapeDtypeStruct((N_DEV, r, c), _BF16),
        compiler_params=pltpu.CompilerParams(dimension_semantics=("parallel",)),
    )(slab_ids, a)


def _pre_add(slab_ids, grad, got, name):
    _, r, c = grad.shape
    tr = _row_tile(r, c)

    def body(ids_ref, a_ref, b_ref, o_ref):
        del ids_ref
        o_ref[...] = (a_ref[...] + b_ref[...]).astype(_BF16)

    return pl.pallas_call(
        body, name=name,
        grid_spec=pltpu.PrefetchScalarGridSpec(
            num_scalar_prefetch=1, grid=(N_CHIPS - 1, r // tr),
            in_specs=[pl.BlockSpec((1, tr, c), lambda s, i, ids: (ids[s + 1], i, 0)),
                      pl.BlockSpec((1, tr, c), lambda s, i, ids: (s + 1, i, 0))],
            out_specs=pl.BlockSpec((1, tr, c), lambda s, i, ids: (s, i, 0))),
        out_shape=jax.ShapeDtypeStruct((N_CHIPS - 1, r, c), _BF16),
        compiler_params=pltpu.CompilerParams(dimension_semantics=("parallel", "parallel")),
    )(slab_ids, grad, got)


def _ada_forward(craw, w_shard, name):
    d, cols = w_shard.shape
    tk = _tile(d, 512, LANES)

    def body(c_ref, w_ref, o_ref):
        @pl.when(pl.program_id(0) == 0)
        def _():
            o_ref[...] = jnp.zeros_like(o_ref)

        o_ref[...] += jnp.dot(_silu(c_ref[...]).astype(_BF16), w_ref[...].astype(_BF16), preferred_element_type=_F32)

    return pl.pallas_call(
        body, name=name, grid=(d // tk,),
        in_specs=[pl.BlockSpec((craw.shape[0], tk), lambda k: (0, k)), pl.BlockSpec((tk, cols), lambda k: (k, 0))],
        out_specs=pl.BlockSpec((craw.shape[0], cols), lambda k: (0, 0)),
        out_shape=jax.ShapeDtypeStruct((craw.shape[0], cols), _F32),
        compiler_params=pltpu.CompilerParams(dimension_semantics=("arbitrary",)),
    )(craw, w_shard)


def _ada_backward(craw, dmod, w, m, v, name):
    d, cols = w.shape
    rows = craw.shape[0]
    tr = _tile(d, 256, LANES)

    def body(c_ref, dm_ref, w_ref, m_ref, v_ref, g_ref, dl_ref, nm_ref, nv_ref, dc_ref):
        act = _silu(c_ref[...]).astype(_BF16)
        dmb = dm_ref[...].astype(_BF16)
        wv = w_ref[...]
        g = lax.dot_general(act, dmb, (((0,), (0,)), ((), ())), preferred_element_type=_F32)
        delta, nm, nv = _adamw(wv, g, m_ref[...], v_ref[...])
        g_ref[...] = g
        dl_ref[...] = delta
        nm_ref[...] = nm
        nv_ref[...] = nv
        dc = lax.dot_general(dmb, wv.astype(_BF16), (((1,), (1,)), ((), ())), preferred_element_type=_F32)
        dc_ref[...] = jnp.broadcast_to(jnp.sum(dc[N_DEV:], axis=0, keepdims=True), dc_ref.shape)

    blk = pl.BlockSpec((tr, cols), lambda i: (i, 0))
    return pl.pallas_call(
        body, name=name, grid=(d // tr,),
        in_specs=[pl.BlockSpec((rows, tr), lambda i: (0, i)), pl.BlockSpec((rows, cols), lambda i: (0, 0)), blk, blk, blk],
        out_specs=[blk, blk, blk, blk, pl.BlockSpec((SUBLANES, tr), lambda i: (0, i))],
        out_shape=[jax.ShapeDtypeStruct((d, cols), _F32)] * 4 + [jax.ShapeDtypeStruct((SUBLANES, d), _F32)],
        compiler_params=pltpu.CompilerParams(dimension_semantics=("parallel",)),
    )(craw, dmod, w, m, v)


def _rms(xf):
    return lax.rsqrt(jnp.mean(xf * xf, axis=-1, keepdims=True) + EPS)


def _head_mean(v):
    hi = v.astype(_BF16)
    lo = (v - hi.astype(_F32)).astype(_BF16)
    ones = jnp.full((2 * HEAD_DIM, HEAD_DIM), 1.0 / HEAD_DIM, _BF16)
    return jnp.dot(jnp.concatenate([hi, lo], axis=1), ones, preferred_element_type=_F32)


def _prenorm(ctx, x, g_pre, mods, tr, name):
    l, d = ctx.shape
    n = x.shape[0]
    nbl = n // tr

    def body(ctx_ref, x_ref, g_ref, mod_ref, h_ref):
        def emit(src_ref):
            xf = src_ref[...]
            y = (xf * _rms(xf)) * g_ref[...]
            h_ref[...] = (y * (1.0 + mod_ref[0, 0:1, :]) + mod_ref[0, 1:2, :]).astype(_BF16)

        is_latent = pl.program_id(0) < nbl
        pl.when(is_latent)(lambda: emit(x_ref))
        pl.when(jnp.logical_not(is_latent))(lambda: emit(ctx_ref))

    return pl.pallas_call(
        body, name=name, grid=((l + n) // tr,),
        in_specs=[pl.BlockSpec((tr, d), lambda i: (jnp.maximum(i - nbl, 0), 0)),
                  pl.BlockSpec((tr, d), lambda i: (jnp.minimum(i, nbl - 1), 0)),
                  pl.BlockSpec((1, d), lambda i: (0, 0)),
                  pl.BlockSpec((1, SUBLANES, d), lambda i: ((i < nbl).astype(jnp.int32), 0, 0))],
        out_specs=pl.BlockSpec((tr, d), lambda i: (i, 0)),
        out_shape=jax.ShapeDtypeStruct((l + n, d), _BF16),
        compiler_params=pltpu.CompilerParams(dimension_semantics=("arbitrary",)),
    )(ctx, x, g_pre, mods)


def _prenorm_backward(dh, ctx, x, dxn, g_pre, mods, tr, name):
    l, d = ctx.shape
    n = x.shape[0]
    nbl = n // tr

    def body(dh_ref, ctx_ref, x_ref, dxn_ref, g_ref, mod_ref, gx_ref, dmod_ref, dg_ref):
        i = pl.program_id(0)

        @pl.when(i == 0)
        def _():
            dg_ref[...] = jnp.zeros_like(dg_ref)

        @pl.when(jnp.logical_or(i == 0, i == nbl))
        def _():
            dmod_ref[...] = jnp.zeros_like(dmod_ref)

        def emit(src_ref, latent):
            xf = src_ref[...]
            r = _rms(xf)
            xn = xf * r
            dhv = dh_ref[...]
            one_scale = 1.0 + mod_ref[0, 0:1, :]
            dmod_ref[0, 0:1, :] += jnp.sum(dhv * (xn * g_ref[...]), axis=0, keepdims=True)
            dmod_ref[0, 1:2, :] += jnp.sum(dhv, axis=0, keepdims=True)
            dyg = dhv * one_scale
            dg_ref[0:1, :] += jnp.sum(dyg * xn, axis=0, keepdims=True)
            if latent:
                dn = dyg * g_ref[...]
                gx_ref[...] = dxn_ref[...] + r * (dn - xn * jnp.mean(dn * xn, axis=-1, keepdims=True))

        pl.when(i < nbl)(lambda: emit(x_ref, True))
        pl.when(i >= nbl)(lambda: emit(ctx_ref, False))

    lat = pl.BlockSpec((tr, d), lambda i: (jnp.minimum(i, nbl - 1), 0))
    sel = pl.BlockSpec((1, SUBLANES, d), lambda i: ((i < nbl).astype(jnp.int32), 0, 0))
    return pl.pallas_call(
        body, name=name, grid=((l + n) // tr,),
        in_specs=[pl.BlockSpec((tr, d), lambda i: (i, 0)),
                  pl.BlockSpec((tr, d), lambda i: (jnp.maximum(i - nbl, 0), 0)),
                  lat, lat, pl.BlockSpec((1, d), lambda i: (0, 0)), sel],
        out_specs=[lat, sel, pl.BlockSpec((SUBLANES, d), lambda i: (0, 0))],
        out_shape=[jax.ShapeDtypeStruct((n, d), _F32), jax.ShapeDtypeStruct((2, SUBLANES, d), _F32),
                   jax.ShapeDtypeStruct((SUBLANES, d), _F32)],
        compiler_params=pltpu.CompilerParams(dimension_semantics=("arbitrary",)),
    )(dh, ctx, x, dxn, g_pre, mods)


def _rope_tables(l, n):
    rows = n // GRID_W
    inv = ROPE_THETA ** (-jnp.arange(ROPE_PAIRS, dtype=_F32) / ROPE_PAIRS)
    ang_r = jnp.arange(rows, dtype=_F32)[:, None] * inv
    ang_c = jnp.arange(GRID_W, dtype=_F32)[:, None] * inv
    per_row = lambda a: jnp.repeat(a, GRID_W, axis=0)
    per_col = lambda a: jnp.tile(a, (rows, 1))
    cr, sr, cc, sc = per_row(jnp.cos(ang_r)), per_row(jnp.sin(ang_r)), per_col(jnp.cos(ang_c)), per_col(jnp.sin(ang_c))
    zero = jnp.zeros_like(sr)
    tc = jnp.concatenate([cr, cr, cc, cc], axis=-1)
    ta = jnp.concatenate([-sr, zero, -sc, zero], axis=-1)
    tb = jnp.concatenate([zero, sr, zero, sc], axis=-1)
    pad = lambda t, fill: jnp.concatenate([t, jnp.full((l, HEAD_DIM), fill, _F32)], axis=0)
    return pad(tc, 1.0), pad(ta, 0.0), pad(tb, 0.0)


def _rope(y, tc, ta, tb):
    return y * tc + pltpu.roll(y, HEAD_DIM - ROPE_PAIRS, 1) * ta + pltpu.roll(y, ROPE_PAIRS, 1) * tb


def _rope_transposed(dy, tc, ta, tb):
    return dy * tc + pltpu.roll(dy * ta, ROPE_PAIRS, 1) + pltpu.roll(dy * tb, HEAD_DIM - ROPE_PAIRS, 1)


def _qkv_post(proj, tables, g_q, g_k, heads, kv_heads, tr, name):
    t = proj.shape[0]
    aw, kw = heads * HEAD_DIM, kv_heads * HEAD_DIM
    w3 = aw + 2 * kw

    def body(p_ref, tc_ref, ta_ref, tb_ref, gq_ref, gk_ref, q_ref, k_ref, v_ref):
        tabs = (tc_ref[...], ta_ref[...], tb_ref[...])

        def norm_rope(col, gain):
            xh = p_ref[:, col:col + HEAD_DIM]
            return _rope((xh * lax.rsqrt(_head_mean(xh * xh) + EPS)) * gain, *tabs).astype(_BF16)

        for h in range(heads):
            q_ref[h] = norm_rope(h * HEAD_DIM, gq_ref[...])
        for h in range(kv_heads):
            k_ref[h] = norm_rope(aw + h * HEAD_DIM, gk_ref[...])
            v_ref[h] = p_ref[:, aw + kw + h * HEAD_DIM:aw + kw + (h + 1) * HEAD_DIM].astype(_BF16)

    tab = pl.BlockSpec((tr, HEAD_DIM), lambda i: (i, 0))
    gain = pl.BlockSpec((1, HEAD_DIM), lambda i: (0, 0))
    return pl.pallas_call(
        body, name=name, grid=(t // tr,),
        in_specs=[pl.BlockSpec((tr, w3), lambda i: (i, 0)), tab, tab, tab, gain, gain],
        out_specs=[pl.BlockSpec((heads, tr, HEAD_DIM), lambda i: (0, i, 0)),
                   pl.BlockSpec((kv_heads, tr, HEAD_DIM), lambda i: (0, i, 0)),
                   pl.BlockSpec((kv_heads, tr, HEAD_DIM), lambda i: (0, i, 0))],
        out_shape=[jax.ShapeDtypeStruct((heads, t, HEAD_DIM), _BF16),
                   jax.ShapeDtypeStruct((kv_heads, t, HEAD_DIM), _BF16),
                   jax.ShapeDtypeStruct((kv_heads, t, HEAD_DIM), _BF16)],
        compiler_params=pltpu.CompilerParams(dimension_semantics=("parallel",)),
    )(proj, *tables, g_q, g_k)


def _qkv_post_backward(proj, dq, dk, dv, tables, g_q, g_k, dproj, l, tr, name):
    t = proj.shape[0]
    heads, kv_heads = dq.shape[0], dk.shape[0]
    aw, kw = heads * HEAD_DIM, kv_heads * HEAD_DIM
    w3 = aw + 2 * kw
    nbl = (t - l) // tr

    def body(p_ref, dq_ref, dk_ref, dv_ref, tc_ref, ta_ref, tb_ref, gq_ref, gk_ref, dproj_ref, o_ref, dgq_ref, dgk_ref):
        del dproj_ref
        i = pl.program_id(0)

        @pl.when(i == 0)
        def _():
            dgq_ref[...] = jnp.zeros_like(dgq_ref)
            dgk_ref[...] = jnp.zeros_like(dgk_ref)

        tabs = (tc_ref[...], ta_ref[...], tb_ref[...])
        latent = i < nbl

        def back(col, dout, gain, dg_ref):
            xh = p_ref[:, col:col + HEAD_DIM]
            r = lax.rsqrt(_head_mean(xh * xh) + EPS)
            xn = xh * r
            dy = _rope_transposed(dout, *tabs)
            dg_ref[0:1, :] += jnp.sum(dy * xn, axis=0, keepdims=True)
            dn = dy * gain
            o_ref[:, col:col + HEAD_DIM] = (r * (dn - xn * _head_mean(dn * xn))).astype(_BF16)

        for h in range(heads):
            back(h * HEAD_DIM, jnp.where(latent, dq_ref[h], 0.0), gq_ref[...], dgq_ref)
        for h in range(kv_heads):
            back(aw + h * HEAD_DIM, dk_ref[h], gk_ref[...], dgk_ref)
            o_ref[:, aw + kw + h * HEAD_DIM:aw + kw + (h + 1) * HEAD_DIM] = dv_ref[h].astype(_BF16)

    tab = pl.BlockSpec((tr, HEAD_DIM), lambda i: (i, 0))
    gain = pl.BlockSpec((1, HEAD_DIM), lambda i: (0, 0))
    acc = pl.BlockSpec((SUBLANES, HEAD_DIM), lambda i: (0, 0))
    return pl.pallas_call(
        body, name=name, grid=(t // tr,),
        in_specs=[pl.BlockSpec((tr, w3), lambda i: (i, 0)),
                  pl.BlockSpec((heads, tr, HEAD_DIM), lambda i: (0, jnp.minimum(i, nbl - 1), 0)),
                  pl.BlockSpec((kv_heads, tr, HEAD_DIM), lambda i: (0, i, 0)),
                  pl.BlockSpec((kv_heads, tr, HEAD_DIM), lambda i: (0, i, 0)),
                  tab, tab, tab, gain, gain, _ANY],
        out_specs=[pl.BlockSpec((tr, w3), lambda i: (i, 0)), acc, acc],
        out_shape=[jax.ShapeDtypeStruct(dproj.shape, dproj.dtype), jax.ShapeDtypeStruct((SUBLANES, HEAD_DIM), _F32),
                   jax.ShapeDtypeStruct((SUBLANES, HEAD_DIM), _F32)],
        input_output_aliases={9: 0},
        compiler_params=pltpu.CompilerParams(dimension_semantics=("arbitrary",)),
    )(proj, dq, dk, dv, *tables, g_q, g_k, dproj)


def _zero_context_rows(dproj, l, w3, tr, name):
    t, iw = dproj.shape
    first = (t - l) // tr

    def body(dproj_ref, o_ref):
        del dproj_ref
        o_ref[...] = jnp.zeros_like(o_ref)

    return pl.pallas_call(
        body, name=name, grid=(l // tr, iw // w3 - 1),
        in_specs=[_ANY], out_specs=pl.BlockSpec((tr, w3), lambda i, j: (first + i, j + 1)),
        out_shape=jax.ShapeDtypeStruct(dproj.shape, dproj.dtype), input_output_aliases={0: 0},
        compiler_params=pltpu.CompilerParams(dimension_semantics=("parallel", "parallel")),
    )(dproj)


def _attention(q, k, v, proj, l, mix, tq, name):
    heads, t, _ = q.shape
    kv_heads = k.shape[0]
    n = t - l
    gw = GQA_GROUP * HEAD_DIM
    aw = heads * HEAD_DIM
    gate_col = (aw + 2 * kv_heads * HEAD_DIM) // gw

    def body(q_ref, k_ref, v_ref, g_ref, o_ref, y_ref, lse_ref):
        lane = lax.broadcasted_iota(jnp.int32, (tq, LANES), 1)
        lse_blk = jnp.zeros((tq, LANES), _F32)
        firsts = list(range(0, GQA_GROUP, ATTN_SUB_HEADS))

        def scores(first):
            qs = q_ref[first:first + ATTN_SUB_HEADS].reshape(ATTN_SUB_HEADS * tq, HEAD_DIM)
            return lax.dot_general(qs, k_ref[0], (((1,), (1,)), ((), ())), preferred_element_type=_F32)

        raw_next = scores(firsts[0])
        for idx, first in enumerate(firsts):
            raw = raw_next
            if idx + 1 < len(firsts):
                raw_next = scores(firsts[idx + 1])
            m = jnp.max(raw, axis=-1, keepdims=True)
            p = jnp.exp2((raw - m) * (ATTN_SCALE * LOG2_E))
            denom = jnp.sum(p, axis=-1, keepdims=True)
            os_ = jnp.dot(p.astype(_BF16), v_ref[0], preferred_element_type=_F32) / denom
            lse_s = m * ATTN_SCALE + jnp.log(denom)
            for j in range(ATTN_SUB_HEADS):
                g = first + j
                og = os_[j * tq:(j + 1) * tq]
                cols = slice(g * HEAD_DIM, (g + 1) * HEAD_DIM)
                o_ref[:, cols] = og
                y_ref[:, cols] = (og * _silu(g_ref[:, cols])).astype(_BF16)
                lse_blk = jnp.where(lane == g, lse_s[j * tq:(j + 1) * tq], lse_blk)
        lse_ref[0] = lse_blk

    return pl.pallas_call(
        body, name=name, grid=(kv_heads, n // tq),
        in_specs=[pl.BlockSpec((GQA_GROUP, tq, HEAD_DIM), lambda h, i: (h, i, 0)),
                  pl.BlockSpec((1, t, HEAD_DIM), lambda h, i: (h, 0, 0)),
                  pl.BlockSpec((1, t, HEAD_DIM), lambda h, i: (h, 0, 0)),
                  pl.BlockSpec((tq, gw), lambda h, i: (i, gate_col + h))],
        out_specs=[pl.BlockSpec((tq, gw), lambda h, i: (i, h)),
                   pl.BlockSpec((tq, gw), lambda h, i: (i, h)),
                   pl.BlockSpec((1, tq, LANES), lambda h, i: (h, i, 0))],
        out_shape=[jax.ShapeDtypeStruct((n, aw), _F32), jax.ShapeDtypeStruct((n, mix), _BF16),
                   jax.ShapeDtypeStruct((kv_heads, n, LANES), _F32)],
        compiler_params=pltpu.CompilerParams(dimension_semantics=("parallel", "parallel")),
    )(q, k, v, proj)


def _attention_backward(q, k, v, attn_o, dy, proj, lse, after, dproj, l, tq, name):
    heads, t, _ = q.shape
    kv_heads = k.shape[0]
    n = t - l
    gw = GQA_GROUP * HEAD_DIM
    aw = heads * HEAD_DIM
    gate_col = (aw + 2 * kv_heads * HEAD_DIM) // gw
    n_parts = next(p for p in (ATTN_KEY_PARTS, 2, 1) if t % (p * BF16_ROWS) == 0)
    part = t // n_parts

    def body(q_ref, k_ref, v_ref, o_ref, dy_ref, g_ref, lse_ref, after_ref, dproj_ref, dq_ref, dg_ref, dk_ref, dv_ref):
        del after_ref, dproj_ref

        @pl.when(pl.program_id(1) == 0)
        def _():
            dk_ref[...] = jnp.zeros_like(dk_ref)
            dv_ref[...] = jnp.zeros_like(dv_ref)

        lse_blk = lse_ref[0]
        for first in range(0, GQA_GROUP, ATTN_BWD_SUB_HEADS):
            qs = q_ref[first:first + ATTN_BWD_SUB_HEADS].reshape(ATTN_BWD_SUB_HEADS * tq, HEAD_DIM)
            do_parts, delta_parts, lse_parts = [], [], []
            for g in range(first, first + ATTN_BWD_SUB_HEADS):
                cols = slice(g * HEAD_DIM, (g + 1) * HEAD_DIM)
                gate, og, dyg = g_ref[:, cols], o_ref[:, cols], dy_ref[:, cols]
                dog = dyg * _silu(gate)
                dg_ref[:, cols] = (dyg * og * _silu_grad(gate)).astype(_BF16)
                do_parts.append(dog)
                delta_parts.append(jnp.sum(dog * og, axis=-1, keepdims=True))
                lse_parts.append(lse_blk[:, g:g + 1])
            dos = jnp.concatenate(do_parts, axis=0).astype(_BF16)
            delta = jnp.concatenate(delta_parts, axis=0)
            lse2 = jnp.concatenate(lse_parts, axis=0) * LOG2_E
            dqs = jnp.zeros((ATTN_BWD_SUB_HEADS * tq, HEAD_DIM), _F32)
            for part_i in range(n_parts):
                keys = slice(part_i * part, (part_i + 1) * part)
                ks, vs = k_ref[0, keys, :], v_ref[0, keys, :]
                raw = lax.dot_general(qs, ks, (((1,), (1,)), ((), ())), preferred_element_type=_F32)
                p = jnp.exp2(raw * (ATTN_SCALE * LOG2_E) - lse2)
                dp = lax.dot_general(dos, vs, (((1,), (1,)), ((), ())), preferred_element_type=_F32)
                ds = (p * (dp - delta)).astype(_BF16)
                dqs = dqs + jnp.dot(ds, ks, preferred_element_type=_F32)
                dk_ref[0, keys, :] += ATTN_SCALE * lax.dot_general(
                    ds, qs, (((0,), (0,)), ((), ())), preferred_element_type=_F32)
                dv_ref[0, keys, :] += lax.dot_general(
                    p.astype(_BF16), dos, (((0,), (0,)), ((), ())), preferred_element_type=_F32)
            dq_ref[first:first + ATTN_BWD_SUB_HEADS] = (ATTN_SCALE * dqs).reshape(ATTN_BWD_SUB_HEADS, tq, HEAD_DIM)

    kv_spec = pl.BlockSpec((1, t, HEAD_DIM), lambda h, i: (h, 0, 0))
    tok = pl.BlockSpec((tq, gw), lambda h, i: (i, h))
    gate = pl.BlockSpec((tq, gw), lambda h, i: (i, gate_col + h))
    return pl.pallas_call(
        body, name=name, grid=(kv_heads, n // tq),
        in_specs=[pl.BlockSpec((GQA_GROUP, tq, HEAD_DIM), lambda h, i: (h, i, 0)), kv_spec, kv_spec,
                  tok, tok, gate, pl.BlockSpec((1, tq, LANES), lambda h, i: (h, i, 0)),
                  pl.BlockSpec(after.shape, lambda h, i: (0, 0)), _ANY],
        out_specs=[pl.BlockSpec((GQA_GROUP, tq, HEAD_DIM), lambda h, i: (h, i, 0)), gate, kv_spec, kv_spec],
        out_shape=[jax.ShapeDtypeStruct((heads, n, HEAD_DIM), _F32), jax.ShapeDtypeStruct(dproj.shape, dproj.dtype),
                   jax.ShapeDtypeStruct((kv_heads, t, HEAD_DIM), _F32), jax.ShapeDtypeStruct((kv_heads, t, HEAD_DIM), _F32)],
        input_output_aliases={8: 1},
        compiler_params=pltpu.CompilerParams(dimension_semantics=("parallel", "arbitrary")),
    )(q, k, v, attn_o, dy, proj, lse, after, dproj)


def _halo_specs(tp, width, col, row_off, total_rows):
    per = tp // POOL_HALO
    first = row_off // POOL_HALO
    last = total_rows // POOL_HALO - 1
    return [pl.BlockSpec((tp, width), lambda i: (i + row_off // tp, col)),
            pl.BlockSpec((POOL_HALO, width), lambda i: (jnp.maximum(first + i * per - 1, 0), col)),
            pl.BlockSpec((POOL_HALO, width), lambda i: (jnp.minimum(first + (i + 1) * per, last), col))]


def _with_halo(cur, prev, nxt, t0, n):
    tp = cur.shape[0]
    r8 = lax.broadcasted_iota(jnp.int32, (POOL_HALO, 1), 0)
    prev = jnp.where(t0 - POOL_HALO + r8 >= 0, prev, 0.0)
    nxt = jnp.where(t0 + tp + r8 < n, nxt, 0.0)
    return jnp.concatenate([prev, cur, nxt], axis=0)


def _shift_rows(a, s):
    return pltpu.roll(a, s % a.shape[0], 0)


def _window_sum(e, w, mirrored):
    a = e + _shift_rows(e, -1 if mirrored else 1)
    s = 1
    while 2 * s < w:
        a = _shift_rows(a, s) + _shift_rows(a, -s)
        s *= 2
    return a


def _window_count(t, w, n):
    half = w // 2
    return (jnp.minimum(t + half, n) - jnp.maximum(t - half, 0)).astype(_F32)


def _pool_forward(gi, proj, y, pool_w, pool_scale, l, heads, kv_heads, tp, name):
    t = proj.shape[0]
    n = t - l
    pg = pool_w.shape[-1]
    w = POOL_WINDOWS[gi]
    aw, kw = heads * HEAD_DIM, kv_heads * HEAD_DIM
    u_col = (2 * aw + 2 * kw) // pg + gi
    gate_col = (2 * aw + 2 * kw + len(POOL_WINDOWS) * pg) // pg + gi

    def body(u_ref, up_ref, un_ref, g_ref, w_ref, sc_ref, y_in_ref, y_ref, raw_ref, d_ref):
        del y_in_ref
        t0 = pl.program_id(0) * tp
        cur = u_ref[...]
        win = _window_sum(_with_halo(cur, up_ref[...], un_ref[...], t0, n), w, False)[POOL_HALO:POOL_HALO + tp]
        tok = t0 + lax.broadcasted_iota(jnp.int32, (tp, 1), 0)
        d = (win / _window_count(tok, w, n) - cur).astype(_BF16)
        raw = jnp.dot(d, w_ref[...].reshape(pg, pg), preferred_element_type=_F32)
        d_ref[...] = d
        raw_ref[...] = raw
        y_ref[...] = ((raw * sc_ref[...]) * _silu(g_ref[...])).astype(_BF16)

    blk = pl.BlockSpec((tp, pg), lambda i: (i, 0))
    return pl.pallas_call(
        body, name=name, grid=(n // tp,),
        in_specs=_halo_specs(tp, pg, u_col, 0, t) + [
            pl.BlockSpec((tp, pg), lambda i: (i, gate_col)),
            pl.BlockSpec((N_DEV, 1, pg // N_DEV, pg), lambda i: (0, gi, 0, 0)),
            pl.BlockSpec((1, pg), lambda i: (0, gi)), _ANY],
        out_specs=[pl.BlockSpec((tp, pg), lambda i: (i, aw // pg + gi)), blk, blk],
        out_shape=[jax.ShapeDtypeStruct(y.shape, y.dtype), jax.ShapeDtypeStruct((n, pg), _F32),
                   jax.ShapeDtypeStruct((n, pg), _BF16)],
        input_output_aliases={6: 0},
        compiler_params=pltpu.CompilerParams(dimension_semantics=("arbitrary",)),
    )(proj, proj, proj, proj, pool_w, pool_scale, y)


def _pool_backward_gate(gi, dy, proj, raw, pool_w, pool_scale, dproj, l, heads, kv_heads, tp, name):
    n, pg = raw.shape
    aw, kw = heads * HEAD_DIM, kv_heads * HEAD_DIM
    gate_col = (2 * aw + 2 * kw + len(POOL_WINDOWS) * pg) // pg + gi

    def body(dy_ref, g_ref, raw_ref, w_ref, sc_ref, dproj_ref, dg_ref, dr_ref, dd_ref, ds_ref):
        del dproj_ref

        @pl.when(pl.program_id(0) == 0)
        def _():
            ds_ref[...] = jnp.zeros_like(ds_ref)

        gate, rawv, dyv, scale = g_ref[...], raw_ref[...], dy_ref[...], sc_ref[...]
        dpool = dyv * _silu(gate)
        dg_ref[...] = (dyv * (rawv * scale) * _silu_grad(gate)).astype(_BF16)
        ds_ref[0:1, :] += jnp.sum(dpool * rawv, axis=0, keepdims=True)
        draw = (dpool * scale).astype(_BF16)
        dr_ref[...] = draw
        dd_ref[...] = lax.dot_general(
            draw, w_ref[...].reshape(pg, pg), (((1,), (1,)), ((), ())), preferred_element_type=_F32)

    blk = pl.BlockSpec((tp, pg), lambda i: (i, 0))
    gate = pl.BlockSpec((tp, pg), lambda i: (i, gate_col))
    return pl.pallas_call(
        body, name=name, grid=(n // tp,),
        in_specs=[pl.BlockSpec((tp, pg), lambda i: (i, aw // pg + gi)), gate, blk,
                  pl.BlockSpec((N_DEV, 1, pg // N_DEV, pg), lambda i: (0, gi, 0, 0)),
                  pl.BlockSpec((1, pg), lambda i: (0, gi)), _ANY],
        out_specs=[gate, blk, blk, pl.BlockSpec((SUBLANES, pg), lambda i: (0, 0))],
        out_shape=[jax.ShapeDtypeStruct(dproj.shape, dproj.dtype), jax.ShapeDtypeStruct((n, pg), _BF16),
                   jax.ShapeDtypeStruct((n, pg), _F32), jax.ShapeDtypeStruct((SUBLANES, pg), _F32)],
        input_output_aliases={5: 0},
        compiler_params=pltpu.CompilerParams(dimension_semantics=("arbitrary",)),
    )(dy, proj, raw, pool_w, pool_scale, dproj)


def _pool_backward_window(gi, dd, dproj, l, col, tp, name):
    n, pg = dd.shape
    w = POOL_WINDOWS[gi]

    def body(c_ref, p_ref, n_ref, dproj_ref, du_ref):
        del dproj_ref
        t0 = pl.program_id(0) * tp
        cur = c_ref[...]
        e = _with_halo(cur, p_ref[...], n_ref[...], t0, n)
        tok = t0 - POOL_HALO + lax.broadcasted_iota(jnp.int32, (tp + 2 * POOL_HALO, 1), 0)
        e = e / jnp.maximum(_window_count(tok, w, n), 1.0)
        du_ref[...] = (_window_sum(e, w, True)[POOL_HALO:POOL_HALO + tp] - cur).astype(_BF16)

    return pl.pallas_call(
        body, name=name, grid=(n // tp,),
        in_specs=_halo_specs(tp, pg, 0, 0, n) + [_ANY],
        out_specs=pl.BlockSpec((tp, pg), lambda i: (i, col)),
        out_shape=jax.ShapeDtypeStruct(dproj.shape, dproj.dtype), input_output_aliases={3: 0},
        compiler_params=pltpu.CompilerParams(dimension_semantics=("arbitrary",)),
    )(dd, dd, dd, dproj)


def _post(out, x, target, gate, g_post, tr, name):
    n, d = out.shape

    def body(o_ref, x_ref, t_ref, gate_ref, g_ref, dxn_ref, do_ref, dgate_ref, dg_ref, loss_ref):
        @pl.when(pl.program_id(0) == 0)
        def _():
            dgate_ref[...] = jnp.zeros_like(dgate_ref)
            dg_ref[...] = jnp.zeros_like(dg_ref)
            loss_ref[...] = jnp.zeros_like(loss_ref)

        ov = o_ref[...]
        r = _rms(ov)
        on = ov * r
        normed = on * g_ref[...]
        err = (x_ref[...] + gate_ref[...] * normed) - t_ref[...]
        loss_ref[...] += jnp.sum(err * err)
        dxn = err / d
        dxn_ref[...] = dxn
        dgate_ref[0:1, :] += jnp.sum(dxn * normed, axis=0, keepdims=True)
        dr = dxn * gate_ref[...]
        dg_ref[0:1, :] += jnp.sum(dr * on, axis=0, keepdims=True)
        dn = dr * g_ref[...]
        do_ref[...] = (r * (dn - on * jnp.mean(dn * on, axis=-1, keepdims=True))).astype(_BF16)

    blk = pl.BlockSpec((tr, d), lambda i: (i, 0))
    vec = pl.BlockSpec((1, d), lambda i: (0, 0))
    acc = pl.BlockSpec((SUBLANES, d), lambda i: (0, 0))
    return pl.pallas_call(
        body, name=name, grid=(n // tr,),
        in_specs=[blk, blk, blk, vec, vec],
        out_specs=[blk, blk, acc, acc, pl.BlockSpec((SUBLANES, LANES), lambda i: (0, 0))],
        out_shape=[jax.ShapeDtypeStruct((n, d), _F32), jax.ShapeDtypeStruct((n, d), _BF16),
                   jax.ShapeDtypeStruct((SUBLANES, d), _F32), jax.ShapeDtypeStruct((SUBLANES, d), _F32),
                   jax.ShapeDtypeStruct((SUBLANES, LANES), _F32)],
        compiler_params=pltpu.CompilerParams(dimension_semantics=("arbitrary",)),
    )(out, x, target, gate, g_post)


def _adam_sharded(slab_ids, grad, got, far, w, m, v, name):
    r, c = w.shape
    tr = _tile(r, max(BF16_ROWS, min(256, (1 << 18) // c)), BF16_ROWS)

    def body(ids_ref, own_ref, got_ref, far_ref, w_ref, m_ref, v_ref, g_ref, dl_ref, nm_ref, nv_ref):
        del ids_ref
        g = own_ref[0] + got_ref[0]
        for k in range(N_CHIPS - 1):
            g = g + far_ref[k].astype(_F32)
        delta, nm, nv = _adamw(w_ref[...], g, m_ref[...], v_ref[...])
        g_ref[...] = g
        dl_ref[...] = delta
        nm_ref[...] = nm
        nv_ref[...] = nv

    blk = pl.BlockSpec((tr, c), lambda i, ids: (i, 0))
    return pl.pallas_call(
        body, name=name,
        grid_spec=pltpu.PrefetchScalarGridSpec(
            num_scalar_prefetch=1, grid=(r // tr,),
            in_specs=[pl.BlockSpec((1, tr, c), lambda i, ids: (ids[0], i, 0)),
                      pl.BlockSpec((1, tr, c), lambda i, ids: (0, i, 0)),
                      pl.BlockSpec((N_CHIPS - 1, tr, c), lambda i, ids: (0, i, 0)), blk, blk, blk],
            out_specs=[blk] * 4),
        out_shape=[jax.ShapeDtypeStruct((r, c), _F32)] * 4,
        compiler_params=pltpu.CompilerParams(dimension_semantics=("parallel",)),
    )(slab_ids, grad, got, far, w, m, v)


def _adam_replicated(parts, extra, through_silu, w, m, v, name):
    def body(p_ref, e_ref, s_ref, w_ref, m_ref, v_ref, g_ref, dl_ref, nm_ref, nv_ref):
        total = p_ref[0] + e_ref[0]
        for dev in range(1, N_DEV):
            total = total + (p_ref[dev] + e_ref[dev])
        g = jnp.where(s_ref[...] > 0.5, total * _silu_grad(w_ref[...]), total)
        delta, nm, nv = _adamw(w_ref[...], g, m_ref[...], v_ref[...])
        g_ref[...] = g
        dl_ref[...] = delta
        nm_ref[...] = nm
        nv_ref[...] = nv

    return pl.pallas_call(
        body, name=name, in_specs=[_VMEM] * 6, out_specs=[_VMEM] * 4,
        out_shape=[jax.ShapeDtypeStruct(w.shape, _F32)] * 4,
    )(parts, extra, through_silu, w, m, v)


def _as_rows(vec):
    size = vec.shape[0]
    padded = -(-size // (SUBLANES * LANES)) * SUBLANES * LANES
    return jnp.pad(vec, (0, padded - size)).reshape(padded // LANES, LANES)


def kernel(x, c, ctx, c_ctx, w_ada, b_ada, norm_pre, norm_post, w_in, q_norm, k_norm, pool_w, pool_scale, w_out, loss_target, m_c_ctx, m_w_ada, m_b_ada, m_norm_pre, m_norm_post, m_w_in, m_q_norm, m_k_norm, m_pool_w, m_pool_scale, m_w_out, v_c_ctx, v_w_ada, v_b_ada, v_norm_pre, v_norm_post, v_w_in, v_q_norm, v_k_norm, v_pool_w, v_pool_scale, v_w_out):
    me = _dev_index(*_position())
    x2, ctx2, target = x[0], ctx[0], loss_target[0]
    n, d = x2.shape
    l = ctx2.shape[0]
    t = l + n
    aw = d // 2
    heads = aw // HEAD_DIM
    kv_heads = heads // GQA_GROUP
    kw = kv_heads * HEAD_DIM
    n_groups = len(POOL_WINDOWS)
    pg = (d - aw) // n_groups
    mix = d
    tr = _tile(l, 128, BF16_ROWS)
    tr2 = _tile(l, 256, BF16_ROWS)
    tq = _tile(l, 128, BF16_ROWS)
    tp = _tile(n, 1024, POOL_HALO)

    xi, yi, ci = _position()
    slab_ids = jnp.stack([_dev_index(*chip, ci) for chip in _chip_order(xi, yi)]).astype(jnp.int32)

    cw = w_in.shape[-1]
    wg = _cast_into_columns(slab_ids, w_in[0], N_DEV, "cast_w_in")
    late = [_cast_into_slab(slab_ids, w_out[0], "cast_w_out"),
            _cast_into_slab(slab_ids, pool_w[0].reshape(-1, pg), "cast_pool_w")]

    c_all = _all_gather_small(_as_rows(c[0]), "gather_c").reshape(N_DEV, -1)[:, :d]
    craw = jnp.concatenate([c_all, jnp.broadcast_to(c_ctx[None], (N_DEV, d))], axis=0)
    ada = _ada_forward(craw, w_ada[0], "ada_forward")
    ada_all = _all_gather_small(ada, "gather_ada")
    mod_all = ada_all.transpose(1, 0, 2).reshape(ada.shape[0], -1) + b_ada[0]
    mod = lax.dynamic_index_in_dim(mod_all, me, 0, keepdims=False)
    mod_c = mod_all[N_DEV]
    shift, scale, gate = mod[:d], mod[d:2 * d], mod[2 * d:]
    zeros6 = jnp.zeros((SUBLANES - 2, d), _F32)
    mods = jnp.stack([jnp.concatenate([mod_c[None, d:2 * d], mod_c[None, :d], zeros6], axis=0),
                      jnp.concatenate([scale[None], shift[None], zeros6], axis=0)])

    a_s, a_r, wg, tok = _w_in_hop(wg, cw, [], "a", ada_all, "gather_w_in_a")
    h_all = _prenorm(ctx2, x2, norm_pre, mods + tok[0, 0], tr2, "prenorm")
    order_ids = jnp.stack([_dev_index(*dev) for dev in _w_in_order(xi, yi, ci)]).astype(jnp.int32)
    proj = lax.empty((t, N_DEV * cw), _F32)
    proj = _proj_blocks(h_all, wg, proj, order_ids, 0, 1, cw, tok, "proj_0")
    b_s, b_r, wg, tok = _w_in_hop(wg, cw, [("a", a_s, a_r, [0, 1], [])], "b", proj, "gather_w_in_b")
    proj = _proj_blocks(h_all, wg, proj, order_ids, 1, 2, cw, tok, "proj_1")
    c_s, c_r, wg, tok = _w_in_hop(wg, cw, [("b", b_s, b_r, [2, 0], [])], "c", proj, "gather_w_in_c")
    proj = _proj_blocks(h_all, wg, proj, order_ids, 3, 2, cw, tok, "proj_2")
    d_s, d_r, wg, tok = _w_in_hop(
        wg, cw, [("c", c_s, c_r, [0], []), ("b", b_s, b_r, [1], [])], "d", proj, "gather_w_in_d")
    proj = _proj_blocks(h_all, wg, proj, order_ids, 5, 2, cw, tok, "proj_3")
    w_in_g, tok = _w_in_hop(
        wg, cw, [("d", d_s, d_r, [0], [0]), ("a", a_s, a_r, [], [0, 1]), ("b", b_s, b_r, [], [0, 1, 2]),
                 ("c", c_s, c_r, [], [0])], None, proj, "gather_w_in_end")
    flight_w = _gather_slabs_start(late, w_in_g, "gather_late_start")
    proj = _proj_blocks(h_all, w_in_g, proj, order_ids, 7, 1, cw, flight_w[-1], "proj_4")
    tables = _rope_tables(l, n)
    q, k, v = _qkv_post(proj, tables, q_norm, k_norm, heads, kv_heads, tr2, "qkv_post")
    attn_o, y, lse = _attention(q, k, v, proj, l, mix, _tile(l, 256, BF16_ROWS), "attention")
    w_out_g8, pool_g8 = _gather_slabs_wait(*flight_w[:3], attn_o, "gather_late_wait")
    w_out_g = w_out_g8.reshape(mix, d)
    pool_g = pool_g8.reshape(N_DEV, n_groups, pg // N_DEV, pg)
    raws, ds = [], []
    for gi in range(n_groups):
        y, raw, dsave = _pool_forward(gi, proj, y, pool_g, pool_scale, l, heads, kv_heads, tp, f"pool_forward_{gi}")
        raws.append(raw)
        ds.append(dsave)
    out = _matmul(y, w_out_g, name="out_proj")
    dxn, dout, dgate8, dgpost8, loss8 = _post(out, x2, target, gate[None], norm_post, tr2, "post")

    gw_out = _matmul(y, dout, ta=True, name="grad_w_out").reshape(N_DEV, mix // N_DEV, d)
    flight_so = _exchange_start(_sibling_copies_by_device, [gw_out], "exchange_sibling_start_w_out", land_slabs=N_CHIPS)
    dy = _matmul(dout, w_out_g, tb=True, after=flight_so[-1], name="d_y")
    gw_out, got_out = _exchange_wait(
        _sibling_copies_by_device, *flight_so[:4], dy, "exchange_sibling_wait_w_out", with_sources=True)
    sum_out = _pre_add(slab_ids, gw_out, got_out, "pre_add_w_out")
    flight_out = _exchange_start(_chip_copies, [sum_out], "exchange_chips_start_w_out")
    w3 = aw + 2 * kw
    dq, dproj, dk, dv = _attention_backward(
        q, k, v, attn_o, dy, proj, lse, flight_out[-1], lax.empty(proj.shape, _BF16), l, tq, "attention_backward")
    dproj, dgq8, dgk8 = _qkv_post_backward(proj, dq, dk, dv, tables, q_norm, k_norm, dproj, l, tr2, "qkv_post_backward")
    dproj = _zero_context_rows(dproj, l, w3, tr, "zero_context_rows")
    gpw, dps8 = [], []
    for gi in range(n_groups):
        dproj, draw, dd, dps = _pool_backward_gate(
            gi, dy, proj, raws[gi], pool_g, pool_scale, dproj, l, heads, kv_heads, tp, f"pool_backward_gate_{gi}")
        dproj = _pool_backward_window(gi, dd, dproj, l, (w3 + aw) // pg + gi, tp, f"pool_backward_window_{gi}")
        dps8.append(dps)
        gpw.append(_matmul(ds[gi], draw, ta=True, name=f"grad_pool_w_{gi}"))
    cw = w_in.shape[-1]
    other_ids = jnp.stack([_dev_index(*chip, 1 - ci) for chip in _chip_order(xi, yi)]).astype(jnp.int32)
    chip_slabs = jnp.arange(N_CHIPS, dtype=jnp.int32)
    pr = pool_w.shape[2]
    gpw8 = jnp.stack(gpw).reshape(n_groups, N_DEV, pr, pg).transpose(1, 0, 2, 3).reshape(N_DEV, n_groups * pr, pg)
    give_in = _matmul_slabs(h_all, dproj, other_ids, cw, "grad_w_in_sibling")
    flight_sib = _exchange_start(_sibling_copies, [give_in, jnp.take(gpw8, other_ids, axis=0)], "exchange_sibling_start")
    gw_in = _matmul_slabs(h_all, dproj, slab_ids, cw, "grad_w_in_own", after=flight_sib[-1])
    gpw_own = jnp.take(gpw8, slab_ids, axis=0)
    got_in, got_pw = _exchange_wait(_sibling_copies, *flight_sib[:4], gw_in, "exchange_sibling_wait")
    sums_in = [_pre_add(chip_slabs, gw_in, got_in, "pre_add_w_in"), _pre_add(chip_slabs, gpw_own, got_pw, "pre_add_pool_w")]
    flight_in = _exchange_start(_chip_copies, sums_in, "exchange_chips_start_w_in")
    dh = _matmul(dproj, w_in_g, tb=True, tm=1088, tk=3072, after=flight_in[-1], name="d_h")
    grad_x, dmods, dgpre8 = _prenorm_backward(dh, ctx2, x2, dxn, norm_pre, mods, tr, "prenorm_backward")

    dmod_lat = jnp.concatenate([dmods[1, 1], dmods[1, 0], dgate8[0]])
    dmod_ctx = jnp.concatenate([dmods[0, 1], dmods[0, 0], jnp.zeros((d,), _F32)])
    small = jnp.concatenate([dmod_lat, dmod_ctx, dgpre8[0], dgpost8[0], dgq8[0], dgk8[0]] + [p[0] for p in dps8]
                            + [loss8[0, :1]])
    gathered = _all_gather_small(_as_rows(small), "gather_small").reshape(N_DEV, -1)
    o = 0
    take = lambda size: (gathered[:, o:o + size], o + size)
    g_mod, o = take(3 * d)
    g_modc, o = take(3 * d)
    g_pre, o = take(d)
    g_post, o = take(d)
    g_q, o = take(HEAD_DIM)
    g_k, o = take(HEAD_DIM)
    g_ps, o = take(n_groups * pg)
    g_loss, o = take(1)
    cols = w_ada.shape[-1]
    mine = lambda a: lax.dynamic_slice_in_dim(a, me * cols, cols, axis=1)
    dmod_rows = jnp.concatenate([mine(g_mod), mine(g_modc)], axis=0)
    g_wada, dl_wada, nm_wada, nv_wada, dcact = _ada_backward(craw, dmod_rows, w_ada[0], m_w_ada[0], v_w_ada[0], "ada_backward")
    dcc = _all_gather_small(_as_rows(dcact[0]), "gather_dcc").reshape(N_DEV, -1)[:, :d]

    sizes = [d, 3 * d, d, d, HEAD_DIM, HEAD_DIM, n_groups * pg]
    def pack(parts):
        rows = jnp.concatenate(parts, axis=1)
        padded = -(-rows.shape[1] // (SUBLANES * LANES)) * SUBLANES * LANES
        return jnp.pad(rows, ((0, 0), (0, padded - rows.shape[1]))).reshape(N_DEV, padded // LANES, LANES)

    zero = lambda size: jnp.zeros((N_DEV, size), _F32)
    parts = pack([dcc, g_mod, g_pre, g_post, g_q, g_k, g_ps])
    extra = pack([zero(d), g_modc, zero(d), zero(d), zero(HEAD_DIM), zero(HEAD_DIM), zero(n_groups * pg)])
    through_silu = _as_rows(jnp.concatenate([jnp.ones((d,), _F32), jnp.zeros((sum(sizes[1:]),), _F32)]))
    cat = lambda items: _as_rows(jnp.concatenate([a.reshape(-1) for a in items]))
    ws = [c_ctx, b_ada, norm_pre, norm_post, q_norm, k_norm, pool_scale]
    ms = [m_c_ctx, m_b_ada, m_norm_pre, m_norm_post, m_q_norm, m_k_norm, m_pool_scale]
    vs = [v_c_ctx, v_b_ada, v_norm_pre, v_norm_post, v_q_norm, v_k_norm, v_pool_scale]
    rep = _adam_replicated(parts, extra, through_silu, cat(ws), cat(ms), cat(vs), "adam_replicated")

    def split(packed):
        flat_, outs, at = packed.reshape(-1), [], 0
        for w, size in zip(ws, sizes):
            outs.append(flat_[at:at + size].reshape(w.shape))
            at += size
        return outs

    g_rep, dl_rep, nm_rep, nv_rep = [split(r) for r in rep]

    far_out = _exchange_wait(_chip_copies, *flight_out[:4], grad_x, "exchange_chips_wait_w_out")[0]
    far_in, far_pw = _exchange_wait(_chip_copies, *flight_in[:4], rep[0], "exchange_chips_wait_w_in")
    two = lambda a: a.reshape(-1, a.shape[-1])
    sharded = []
    for ids, g, got, far, w, m, v_, name in zip(
            (chip_slabs, slab_ids, chip_slabs), (gw_in, gw_out, gpw_own), (got_in, got_out, got_pw),
            (far_in, far_out, far_pw), (w_in, w_out, pool_w), (m_w_in, m_w_out, m_pool_w),
            (v_w_in, v_w_out, v_pool_w), ("adam_w_in", "adam_w_out", "adam_pool_w")):
        res = _adam_sharded(ids, g, got, far, two(w), two(m), two(v_), name)
        sharded.append([r.reshape(w.shape) for r in res])
    (g_w_in, dl_w_in, nm_w_in, nv_w_in), (g_w_out, dl_w_out, nm_w_out, nv_w_out), (g_pw, dl_pw, nm_pw, nv_pw) = sharded

    loss_sum = g_loss[0, 0]
    for dev in range(1, N_DEV):
        loss_sum = loss_sum + g_loss[dev, 0]
    loss = (0.5 / d) * loss_sum

    def ordered(rep_list, ada_, w_in_, pw_, w_out_):
        return [rep_list[0], ada_[None], rep_list[1], rep_list[2], rep_list[3], w_in_, rep_list[4], rep_list[5],
                pw_, rep_list[6], w_out_]

    return (loss, grad_x[None],
            *ordered(g_rep, g_wada, g_w_in, g_pw, g_w_out),
            *ordered(dl_rep, dl_wada, dl_w_in, dl_pw, dl_w_out),
            *ordered(nm_rep, nm_wada, nm_w_in, nm_pw, nm_w_out),
            *ordered(nv_rep, nv_wada, nv_w_in, nv_pw, nv_w_out))
```

```python
import jax
import jax.numpy as jnp
from jax import lax
from jax.experimental import pallas as pl
from jax.experimental.pallas import tpu as pltpu

HEAD_DIM = 128
GQA_GROUP = 4
ATTN_SUB_HEADS = 1
ATTN_BWD_SUB_HEADS = 1
ATTN_KEY_PARTS = 2
LOG2_E = 1.4426950408889634
GRID_W = 64
ROPE_PAIRS = HEAD_DIM // 4
ROPE_THETA = 10000.0
ATTN_SCALE = HEAD_DIM ** -0.5
EPS = 1e-6
POOL_WINDOWS = (2, 4, 8, 16)
POOL_HALO = 8
N_DEV = 8
N_CHIPS = 4
ADAM_LR = 0.001
ADAM_B1 = 0.9
ADAM_B2 = 0.999
ADAM_EPS = 1e-08
ADAM_WD = 0.01
ADAM_STEP = 10

LANES = 128
SUBLANES = 8
BF16_ROWS = 16

_MESH = pl.DeviceIdType.MESH
_ANY = pl.BlockSpec(memory_space=pl.ANY)
_VMEM = pl.BlockSpec(memory_space=pltpu.VMEM)
_HBM = pl.BlockSpec(memory_space=pltpu.HBM)
_SEM = pl.BlockSpec(memory_space=pltpu.SEMAPHORE)
_EFFECT = pltpu.SideEffectType.DATAFLOW_SIDE_EFFECTING
_F32 = jnp.float32
_BF16 = jnp.bfloat16


def _tile(dim, pref, align):
    t = min(pref, dim)
    t -= t % align
    while t >= align:
        if dim % t == 0:
            return t
        t -= align
    return dim


def _position():
    return lax.axis_index("x"), lax.axis_index("y"), lax.axis_index("c")


def _flip(v, bit):
    return 1 - v if bit else v


def _dev_index(x, y, c):
    return 4 * x + 2 * y + c


def _silu(g):
    return g * jax.nn.sigmoid(g)


def _silu_grad(g):
    s = jax.nn.sigmoid(g)
    return s * (1.0 + g * (1.0 - s))


def _adamw(w, g, m, v):
    m = ADAM_B1 * m + (1.0 - ADAM_B1) * g
    v = ADAM_B2 * v + (1.0 - ADAM_B2) * (g * g)
    m_hat = m / (1.0 - ADAM_B1 ** ADAM_STEP)
    v_hat = v / (1.0 - ADAM_B2 ** ADAM_STEP)
    delta = -ADAM_LR * (m_hat / (jnp.sqrt(v_hat) + ADAM_EPS) + ADAM_WD * w)
    return delta, m, v


def _all_gather_small(v, name):
    rows, cols = v.shape

    def body(v_ref, out_ref, send_sems, recv_sems):
        x, y, c = _position()
        me = _dev_index(x, y, c)
        out_ref[me] = v_ref[...]
        peers = [(_flip(x, k & 4), _flip(y, k & 2), _flip(c, k & 1)) for k in range(1, N_DEV)]

        def copy(k, block, to):
            return pltpu.make_async_remote_copy(
                src_ref=v_ref, dst_ref=out_ref.at[block], send_sem=send_sems.at[k], recv_sem=recv_sems.at[k],
                device_id=to, device_id_type=_MESH)

        sends = [copy(k, me, p) for k, p in enumerate(peers)]
        for s in sends:
            s.start()
        for k, p in enumerate(peers):
            copy(k, _dev_index(*p), p).wait_recv()
        for s in sends:
            s.wait_send()

    return pl.pallas_call(
        body, name=name,
        out_shape=jax.ShapeDtypeStruct((N_DEV, rows, cols), v.dtype),
        in_specs=[_VMEM], out_specs=_VMEM,
        scratch_shapes=[pltpu.SemaphoreType.DMA((N_DEV - 1,)), pltpu.SemaphoreType.DMA((N_DEV - 1,))],
    )(v)


def _route(x, y, c):
    first = (x + (1 - c) * (1 - 2 * x), y + c * (1 - 2 * y))
    second = (x + c * (1 - 2 * x), y + (1 - c) * (1 - 2 * y))
    return first, second, (1 - x, 1 - y)


def _w_in_order(x, y, c):
    first, second, diagonal = _route(x, y, c)
    return [(x, y, c), (x, y, 1 - c), (*first, c), (*second, 1 - c), (*second, c), (*first, 1 - c),
            (*diagonal, c), (*diagonal, 1 - c)]


W_IN_HOPS = {"a": 2, "b": 3, "c": 1, "d": 1}


def _w_in_hop_copies(group, wg, width, send_sems, recv_sems):
    x, y, c = _position()
    me, sibling = (x, y, c), (x, y, 1 - c)
    first, second, diagonal = _route(x, y, c)

    def cp(k, block, to):
        cols = wg.at[:, pl.ds(pl.multiple_of(_dev_index(*block) * width, width), width)]
        return pltpu.make_async_remote_copy(
            src_ref=cols, dst_ref=cols, send_sem=send_sems.at[k], recv_sem=recv_sems.at[k],
            device_id=to, device_id_type=_MESH)

    if group == "a":
        return [cp(0, me, sibling), cp(1, me, (*first, c))]
    if group == "b":
        return [cp(0, me, (*second, c)), cp(1, (*first, c), (*second, c)), cp(2, (*first, c), sibling)]
    if group == "c":
        return [cp(0, (*second, c), sibling)]
    return [cp(0, (*diagonal, c), sibling)]


def _w_in_hop(wg, width, waits, start, after, name):
    n_sem = 2 * len(waits)

    def body(*refs):
        wg_ref = refs[0]
        for i, (group, _, _, arrivals, sends) in enumerate(waits):
            cps = _w_in_hop_copies(group, wg_ref, width, refs[1 + 2 * i], refs[2 + 2 * i])
            for k in arrivals:
                cps[k].wait_recv()
            for k in sends:
                cps[k].wait_send()
        if start:
            for cp in _w_in_hop_copies(start, wg_ref, width, refs[n_sem + 2], refs[n_sem + 3]):
                cp.start()
        refs[-1][...] = jnp.zeros_like(refs[-1])

    sems = [s for w in waits for s in w[1:3]]
    new = [pltpu.SemaphoreType.DMA((W_IN_HOPS[start],))] * 2 if start else []
    outs = pl.pallas_call(
        body, name=name,
        out_shape=(*new, pltpu.HBM(wg.shape, wg.dtype), jax.ShapeDtypeStruct((SUBLANES, LANES), _F32)),
        in_specs=[_HBM] + [_SEM] * n_sem + [_ANY], out_specs=(*[_SEM] * len(new), _HBM, _VMEM),
        input_output_aliases={0: len(new)},
        compiler_params=pltpu.CompilerParams(has_side_effects=_EFFECT),
    )(pltpu.with_memory_space_constraint(wg, pltpu.HBM), *sems, after)
    return outs


def _slab_copies(bufs, send_sems, recv_sems):
    x, y, c = _position()
    me = _dev_index(x, y, c)
    peers = [(_flip(x, k & 4), _flip(y, k & 2), _flip(c, k & 1)) for k in range(1, N_DEV)]
    return [pltpu.make_async_remote_copy(
        src_ref=buf.at[me], dst_ref=buf.at[me],
        send_sem=send_sems.at[(N_DEV - 1) * a + k], recv_sem=recv_sems.at[(N_DEV - 1) * a + k],
        device_id=peer, device_id_type=_MESH)
        for a, buf in enumerate(bufs) for k, peer in enumerate(peers)]


def _gather_slabs_start(bufs, after, name):
    n = len(bufs)
    n_copies = (N_DEV - 1) * n

    def body(*refs):
        send_sems, recv_sems, token = refs[n + 1], refs[n + 2], refs[-1]
        for cp in _slab_copies(refs[:n], send_sems, recv_sems):
            cp.start()
        token[...] = jnp.zeros_like(token)

    outs = pl.pallas_call(
        body, name=name,
        out_shape=(pltpu.SemaphoreType.DMA((n_copies,)), pltpu.SemaphoreType.DMA((n_copies,)),
                   *[pltpu.HBM(b.shape, b.dtype) for b in bufs], jax.ShapeDtypeStruct((SUBLANES, LANES), _F32)),
        in_specs=[_HBM] * n + [_ANY], out_specs=(_SEM, _SEM, *[_HBM] * n, _VMEM),
        input_output_aliases={i: 2 + i for i in range(n)},
        compiler_params=pltpu.CompilerParams(has_side_effects=_EFFECT),
    )(*[pltpu.with_memory_space_constraint(b, pltpu.HBM) for b in bufs], after)
    return outs[0], outs[1], list(outs[2:2 + n]), outs[-1]


def _gather_slabs_wait(send_sems, recv_sems, bufs, after, name):
    n = len(bufs)

    def body(*refs):
        for cp in _slab_copies(refs[:n], refs[n], refs[n + 1]):
            cp.wait_send()
            cp.wait_recv()

    outs = pl.pallas_call(
        body, name=name, out_shape=tuple(pltpu.HBM(b.shape, b.dtype) for b in bufs),
        in_specs=[_HBM] * n + [_SEM, _SEM, _ANY], out_specs=[_HBM] * n,
        input_output_aliases={i: i for i in range(n)},
        compiler_params=pltpu.CompilerParams(has_side_effects=_EFFECT),
    )(*bufs, send_sems, recv_sems, after)
    return list(outs)


def _chip_order(x, y):
    return [(x, y), (1 - x, y), (x, 1 - y), (1 - x, 1 - y)]


def _chip_copies(srcs, lands, send_sems, recv_sems):
    x, y, c = _position()
    return [pltpu.make_async_remote_copy(
        src_ref=srcs[a].at[k], dst_ref=lands[a].at[k],
        send_sem=send_sems.at[(N_CHIPS - 1) * a + k], recv_sem=recv_sems.at[(N_CHIPS - 1) * a + k],
        device_id=(*chip, c), device_id_type=_MESH)
        for a in range(len(srcs)) for k, chip in enumerate(_chip_order(x, y)[1:])]


def _sibling_copies(srcs, lands, send_sems, recv_sems):
    x, y, c = _position()
    return [pltpu.make_async_remote_copy(
        src_ref=srcs[a].at[s], dst_ref=lands[a].at[s],
        send_sem=send_sems.at[N_CHIPS * a + s], recv_sem=recv_sems.at[N_CHIPS * a + s],
        device_id=(x, y, 1 - c), device_id_type=_MESH)
        for a in range(len(srcs)) for s in range(N_CHIPS)]


def _sibling_copies_by_device(srcs, lands, send_sems, recv_sems):
    x, y, c = _position()
    return [pltpu.make_async_remote_copy(
        src_ref=srcs[a].at[_dev_index(*chip, 1 - c)], dst_ref=lands[a].at[s],
        send_sem=send_sems.at[N_CHIPS * a + s], recv_sem=recv_sems.at[N_CHIPS * a + s],
        device_id=(x, y, 1 - c), device_id_type=_MESH)
        for a in range(len(srcs)) for s, chip in enumerate(_chip_order(x, y))]


def _exchange_start(copies, sums, name, land_slabs=None):
    n = len(sums)
    land_shapes = [((land_slabs or s.shape[0]),) + s.shape[1:] for s in sums]
    n_copies = sum(shape[0] for shape in land_shapes)

    def body(*refs):
        srcs, lands = refs[:n], refs[n:2 * n]
        send_sems, recv_sems, token = refs[2 * n], refs[2 * n + 1], refs[-1]
        for cp in copies(srcs, lands, send_sems, recv_sems):
            cp.start()
        token[...] = jnp.zeros_like(token)

    hbm = [pltpu.HBM(s.shape, s.dtype) for s in sums] + [pltpu.HBM(shape, s.dtype) for shape, s in zip(land_shapes, sums)]
    outs = pl.pallas_call(
        body, name=name,
        out_shape=(pltpu.SemaphoreType.DMA((n_copies,)), pltpu.SemaphoreType.DMA((n_copies,)), *hbm,
                   jax.ShapeDtypeStruct((SUBLANES, LANES), _F32)),
        in_specs=[_HBM] * (2 * n), out_specs=(_SEM, _SEM, *[_HBM] * (2 * n), _VMEM),
        input_output_aliases={i: 2 + i for i in range(2 * n)},
        compiler_params=pltpu.CompilerParams(has_side_effects=_EFFECT),
    )(*[pltpu.with_memory_space_constraint(s, pltpu.HBM) for s in sums],
      *[pltpu.with_memory_space_constraint(lax.empty(shape, s.dtype), pltpu.HBM) for shape, s in zip(land_shapes, sums)])
    return outs[0], outs[1], list(outs[2:2 + n]), list(outs[2 + n:2 + 2 * n]), outs[-1]


def _exchange_wait(copies, send_sems, recv_sems, srcs, lands, after, name, with_sources=False):
    n = len(srcs)

    def body(*refs):
        for cp in copies(refs[:n], refs[n:2 * n], refs[2 * n], refs[2 * n + 1]):
            cp.wait_send()
            cp.wait_recv()

    hbm = [pltpu.HBM(s.shape, s.dtype) for s in (*srcs, *lands)]
    outs = pl.pallas_call(
        body, name=name, out_shape=tuple(hbm),
        in_specs=[_HBM] * (2 * n) + [_SEM, _SEM, _ANY], out_specs=[_HBM] * (2 * n),
        input_output_aliases={i: i for i in range(2 * n)},
        compiler_params=pltpu.CompilerParams(has_side_effects=_EFFECT),
    )(*srcs, *lands, send_sems, recv_sems, after)
    return list(outs) if with_sources else list(outs[n:])


def _matmul(a, b, *, ta=False, tb=False, out_dtype=_F32, tm=1024, tn=1024, tk=4608, after=None, name):
    kdim, m = a.shape if ta else a.shape[::-1]
    n = b.shape[0] if tb else b.shape[1]
    tm = _tile(m, tm, LANES if ta else BF16_ROWS)
    tn = _tile(n, tn, LANES)
    tk = _tile(kdim, tk, BF16_ROWS if ta else LANES)
    nk = kdim // tk
    dims = (((0 if ta else 1,), (1 if tb else 0,)), ((), ()))

    def body_whole_k(a_ref, b_ref, *rest):
        rest[-1][...] = lax.dot_general(a_ref[...], b_ref[...], dims, preferred_element_type=_F32).astype(out_dtype)

    def body_split_k(a_ref, b_ref, *rest):
        o_ref, acc_ref = rest[-2:]
        k = pl.program_id(2)

        @pl.when(k == 0)
        def _():
            acc_ref[...] = jnp.zeros_like(acc_ref)

        acc_ref[...] += lax.dot_general(a_ref[...], b_ref[...], dims, preferred_element_type=_F32)

        @pl.when(k == nk - 1)
        def _():
            o_ref[...] = acc_ref[...].astype(out_dtype)

    a_spec = pl.BlockSpec((tk, tm), lambda i, j, k: (k, i)) if ta else pl.BlockSpec((tm, tk), lambda i, j, k: (i, k))
    b_spec = pl.BlockSpec((tn, tk), lambda i, j, k: (j, k)) if tb else pl.BlockSpec((tk, tn), lambda i, j, k: (k, j))
    extra = [] if after is None else [after]
    return pl.pallas_call(
        body_whole_k if nk == 1 else body_split_k, name=name, grid=(m // tm, n // tn, nk),
        in_specs=[a_spec, b_spec] + [pl.BlockSpec(t.shape, lambda i, j, k: (0, 0)) for t in extra],
        out_specs=pl.BlockSpec((tm, tn), lambda i, j, k: (i, j)), out_shape=jax.ShapeDtypeStruct((m, n), out_dtype),
        scratch_shapes=[] if nk == 1 else [pltpu.VMEM((tm, tn), _F32)],
        compiler_params=pltpu.CompilerParams(dimension_semantics=("parallel", "parallel", "arbitrary")),
    )(a, b, *extra)


def _proj_blocks(a, wg, dst, order_ids, first, count, width, after, name):
    m, kdim = a.shape
    tm = _tile(m, 1088, BF16_ROWS)

    def body(ids_ref, a_ref, w_ref, after_ref, dst_ref, o_ref):
        del ids_ref, after_ref, dst_ref
        o_ref[...] = jnp.dot(a_ref[...], w_ref[...], preferred_element_type=_F32)

    return pl.pallas_call(
        body, name=name,
        grid_spec=pltpu.PrefetchScalarGridSpec(
            num_scalar_prefetch=1, grid=(count, m // tm),
            in_specs=[pl.BlockSpec((tm, kdim), lambda j, i, ids: (i, 0)),
                      pl.BlockSpec((kdim, width), lambda j, i, ids: (0, ids[first + j])),
                      pl.BlockSpec(after.shape, lambda j, i, ids: (0, 0)), _ANY],
            out_specs=pl.BlockSpec((tm, width), lambda j, i, ids: (i, ids[first + j]))),
        out_shape=jax.ShapeDtypeStruct(dst.shape, dst.dtype),
        input_output_aliases={4: 0},
        compiler_params=pltpu.CompilerParams(dimension_semantics=("arbitrary", "arbitrary")),
    )(order_ids, a, wg, after, dst)


def _cast_into_columns(slab_ids, a, n_blocks, name):
    r, c = a.shape
    tr = _row_tile(r, c)

    def body(ids_ref, a_ref, o_ref):
        del ids_ref
        o_ref[...] = a_ref[...].astype(_BF16)

    return pl.pallas_call(
        body, name=name,
        grid_spec=pltpu.PrefetchScalarGridSpec(
            num_scalar_prefetch=1, grid=(r // tr,),
            in_specs=[pl.BlockSpec((tr, c), lambda i, ids: (i, 0))],
            out_specs=pl.BlockSpec((tr, c), lambda i, ids: (i, ids[0]))),
        out_shape=jax.ShapeDtypeStruct((r, n_blocks * c), _BF16),
        compiler_params=pltpu.CompilerParams(dimension_semantics=("parallel",)),
    )(slab_ids, a)


def _matmul_slabs(a, b, ids, width, name, after=None):
    kdim, m = a.shape
    n_slabs = ids.shape[0]
    tm = _tile(m, 1024, LANES)

    def body(ids_ref, a_ref, b_ref, *rest):
        del ids_ref
        rest[-1][0] = lax.dot_general(a_ref[...], b_ref[...], (((0,), (0,)), ((), ())), preferred_element_type=_F32)

    extra = [] if after is None else [after]
    return pl.pallas_call(
        body, name=name,
        grid_spec=pltpu.PrefetchScalarGridSpec(
            num_scalar_prefetch=1, grid=(m // tm, n_slabs),
            in_specs=[pl.BlockSpec((kdim, tm), lambda i, j, ids: (0, i)),
                      pl.BlockSpec((kdim, width), lambda i, j, ids: (0, ids[j]))]
            + [pl.BlockSpec(t.shape, lambda i, j, ids: (0, 0)) for t in extra],
            out_specs=pl.BlockSpec((1, tm, width), lambda i, j, ids: (j, i, 0))),
        out_shape=jax.ShapeDtypeStruct((n_slabs, m, width), _F32),
        compiler_params=pltpu.CompilerParams(dimension_semantics=("parallel", "parallel")),
    )(ids, a, b, *extra)


def _row_tile(rows, cols):
    return _tile(rows, max(BF16_ROWS, min(512, (1 << 19) // cols)), BF16_ROWS)


def _cast_into_slab(slab_ids, a, name):
    r, c = a.shape
    tr = _row_tile(r, c)

    def body(ids_ref, a_ref, o_ref):
        del ids_ref
        o_ref[0] = a_ref[...].astype(_BF16)

    return pl.pallas_call(
        body, name=name,
        grid_spec=pltpu.PrefetchScalarGridSpec(
            num_scalar_prefetch=1, grid=(r // tr,),
            in_specs=[pl.BlockSpec((tr, c), lambda i, ids: (i, 0))],
            out_specs=pl.BlockSpec((1, tr, c), lambda i, ids: (ids[0], i, 0))),
        out_shape=jax.ShapeDtypeStruct((N_DEV, r, c), _BF16),
        compiler_params=pltpu.CompilerParams(dimension_semantics=("parallel",)),
    )(slab_ids, a)


def _pre_add(slab_ids, grad, got, name):
    _, r, c = grad.shape
    tr = _row_tile(r, c)

    def body(ids_ref, a_ref, b_ref, o_ref):
        del ids_ref
        o_ref[...] = (a_ref[...] + b_ref[...]).astype(_BF16)

    return pl.pallas_call(
        body, name=name,
        grid_spec=pltpu.PrefetchScalarGridSpec(
            num_scalar_prefetch=1, grid=(N_CHIPS - 1, r // tr),
            in_specs=[pl.BlockSpec((1, tr, c), lambda s, i, ids: (ids[s + 1], i, 0)),
                      pl.BlockSpec((1, tr, c), lambda s, i, ids: (s + 1, i, 0))],
            out_specs=pl.BlockSpec((1, tr, c), lambda s, i, ids: (s, i, 0))),
        out_shape=jax.ShapeDtypeStruct((N_CHIPS - 1, r, c), _BF16),
        compiler_params=pltpu.CompilerParams(dimension_semantics=("parallel", "parallel")),
    )(slab_ids, grad, got)


def _ada_forward(craw, w_shard, name):
    d, cols = w_shard.shape
    tk = _tile(d, 512, LANES)

    def body(c_ref, w_ref, o_ref):
        @pl.when(pl.program_id(0) == 0)
        def _():
            o_ref[...] = jnp.zeros_like(o_ref)

        o_ref[...] += jnp.dot(_silu(c_ref[...]).astype(_BF16), w_ref[...].astype(_BF16), preferred_element_type=_F32)

    return pl.pallas_call(
        body, name=name, grid=(d // tk,),
        in_specs=[pl.BlockSpec((craw.shape[0], tk), lambda k: (0, k)), pl.BlockSpec((tk, cols), lambda k: (k, 0))],
        out_specs=pl.BlockSpec((craw.shape[0], cols), lambda k: (0, 0)),
        out_shape=jax.ShapeDtypeStruct((craw.shape[0], cols), _F32),
        compiler_params=pltpu.CompilerParams(dimension_semantics=("arbitrary",)),
    )(craw, w_shard)


def _ada_backward(craw, dmod, w, m, v, name):
    d, cols = w.shape
    rows = craw.shape[0]
    tr = _tile(d, 256, LANES)

    def body(c_ref, dm_ref, w_ref, m_ref, v_ref, g_ref, dl_ref, nm_ref, nv_ref, dc_ref):
        act = _silu(c_ref[...]).astype(_BF16)
        dmb = dm_ref[...].astype(_BF16)
        wv = w_ref[...]
        g = lax.dot_general(act, dmb, (((0,), (0,)), ((), ())), preferred_element_type=_F32)
        delta, nm, nv = _adamw(wv, g, m_ref[...], v_ref[...])
        g_ref[...] = g
        dl_ref[...] = delta
        nm_ref[...] = nm
        nv_ref[...] = nv
        dc = lax.dot_general(dmb, wv.astype(_BF16), (((1,), (1,)), ((), ())), preferred_element_type=_F32)
        dc_ref[...] = jnp.broadcast_to(jnp.sum(dc[N_DEV:], axis=0, keepdims=True), dc_ref.shape)

    blk = pl.BlockSpec((tr, cols), lambda i: (i, 0))
    return pl.pallas_call(
        body, name=name, grid=(d // tr,),
        in_specs=[pl.BlockSpec((rows, tr), lambda i: (0, i)), pl.BlockSpec((rows, cols), lambda i: (0, 0)), blk, blk, blk],
        out_specs=[blk, blk, blk, blk, pl.BlockSpec((SUBLANES, tr), lambda i: (0, i))],
        out_shape=[jax.ShapeDtypeStruct((d, cols), _F32)] * 4 + [jax.ShapeDtypeStruct((SUBLANES, d), _F32)],
        compiler_params=pltpu.CompilerParams(dimension_semantics=("parallel",)),
    )(craw, dmod, w, m, v)


def _rms(xf):
    return lax.rsqrt(jnp.mean(xf * xf, axis=-1, keepdims=True) + EPS)


def _head_mean(v):
    hi = v.astype(_BF16)
    lo = (v - hi.astype(_F32)).astype(_BF16)
    ones = jnp.full((2 * HEAD_DIM, HEAD_DIM), 1.0 / HEAD_DIM, _BF16)
    return jnp.dot(jnp.concatenate([hi, lo], axis=1), ones, preferred_element_type=_F32)


def _prenorm(ctx, x, g_pre, mods, tr, name):
    l, d = ctx.shape
    n = x.shape[0]
    nbl = n // tr

    def body(ctx_ref, x_ref, g_ref, mod_ref, h_ref):
        def emit(src_ref):
            xf = src_ref[...]
            y = (xf * _rms(xf)) * g_ref[...]
            h_ref[...] = (y * (1.0 + mod_ref[0, 0:1, :]) + mod_ref[0, 1:2, :]).astype(_BF16)

        is_latent = pl.program_id(0) < nbl
        pl.when(is_latent)(lambda: emit(x_ref))
        pl.when(jnp.logical_not(is_latent))(lambda: emit(ctx_ref))

    return pl.pallas_call(
        body, name=name, grid=((l + n) // tr,),
        in_specs=[pl.BlockSpec((tr, d), lambda i: (jnp.maximum(i - nbl, 0), 0)),
                  pl.BlockSpec((tr, d), lambda i: (jnp.minimum(i, nbl - 1), 0)),
                  pl.BlockSpec((1, d), lambda i: (0, 0)),
                  pl.BlockSpec((1, SUBLANES, d), lambda i: ((i < nbl).astype(jnp.int32), 0, 0))],
        out_specs=pl.BlockSpec((tr, d), lambda i: (i, 0)),
        out_shape=jax.ShapeDtypeStruct((l + n, d), _BF16),
        compiler_params=pltpu.CompilerParams(dimension_semantics=("arbitrary",)),
    )(ctx, x, g_pre, mods)


def _prenorm_backward(dh, ctx, x, dxn, g_pre, mods, tr, name):
    l, d = ctx.shape
    n = x.shape[0]
    nbl = n // tr

    def body(dh_ref, ctx_ref, x_ref, dxn_ref, g_ref, mod_ref, gx_ref, dmod_ref, dg_ref):
        i = pl.program_id(0)

        @pl.when(i == 0)
        def _():
            dg_ref[...] = jnp.zeros_like(dg_ref)

        @pl.when(jnp.logical_or(i == 0, i == nbl))
        def _():
            dmod_ref[...] = jnp.zeros_like(dmod_ref)

        def emit(src_ref, latent):
            xf = src_ref[...]
            r = _rms(xf)
            xn = xf * r
            dhv = dh_ref[...]
            one_scale = 1.0 + mod_ref[0, 0:1, :]
            dmod_ref[0, 0:1, :] += jnp.sum(dhv * (xn * g_ref[...]), axis=0, keepdims=True)
            dmod_ref[0, 1:2, :] += jnp.sum(dhv, axis=0, keepdims=True)
            dyg = dhv * one_scale
            dg_ref[0:1, :] += jnp.sum(dyg * xn, axis=0, keepdims=True)
            if latent:
                dn = dyg * g_ref[...]
                gx_ref[...] = dxn_ref[...] + r * (dn - xn * jnp.mean(dn * xn, axis=-1, keepdims=True))

        pl.when(i < nbl)(lambda: emit(x_ref, True))
        pl.when(i >= nbl)(lambda: emit(ctx_ref, False))

    lat = pl.BlockSpec((tr, d), lambda i: (jnp.minimum(i, nbl - 1), 0))
    sel = pl.BlockSpec((1, SUBLANES, d), lambda i: ((i < nbl).astype(jnp.int32), 0, 0))
    return pl.pallas_call(
        body, name=name, grid=((l + n) // tr,),
        in_specs=[pl.BlockSpec((tr, d), lambda i: (i, 0)),
                  pl.BlockSpec((tr, d), lambda i: (jnp.maximum(i - nbl, 0), 0)),
                  lat, lat, pl.BlockSpec((1, d), lambda i: (0, 0)), sel],
        out_specs=[lat, sel, pl.BlockSpec((SUBLANES, d), lambda i: (0, 0))],
        out_shape=[jax.ShapeDtypeStruct((n, d), _F32), jax.ShapeDtypeStruct((2, SUBLANES, d), _F32),
                   jax.ShapeDtypeStruct((SUBLANES, d), _F32)],
        compiler_params=pltpu.CompilerParams(dimension_semantics=("arbitrary",)),
    )(dh, ctx, x, dxn, g_pre, mods)


def _rope_tables(l, n):
    rows = n // GRID_W
    inv = ROPE_THETA ** (-jnp.arange(ROPE_PAIRS, dtype=_F32) / ROPE_PAIRS)
    ang_r = jnp.arange(rows, dtype=_F32)[:, None] * inv
    ang_c = jnp.arange(GRID_W, dtype=_F32)[:, None] * inv
    per_row = lambda a: jnp.repeat(a, GRID_W, axis=0)
    per_col = lambda a: jnp.tile(a, (rows, 1))
    cr, sr, cc, sc = per_row(jnp.cos(ang_r)), per_row(jnp.sin(ang_r)), per_col(jnp.cos(ang_c)), per_col(jnp.sin(ang_c))
    zero = jnp.zeros_like(sr)
    tc = jnp.concatenate([cr, cr, cc, cc], axis=-1)
    ta = jnp.concatenate([-sr, zero, -sc, zero], axis=-1)
    tb = jnp.concatenate([zero, sr, zero, sc], axis=-1)
    pad = lambda t, fill: jnp.concatenate([t, jnp.full((l, HEAD_DIM), fill, _F32)], axis=0)
    return pad(tc, 1.0), pad(ta, 0.0), pad(tb, 0.0)


def _rope(y, tc, ta, tb):
    return y * tc + pltpu.roll(y, HEAD_DIM - ROPE_PAIRS, 1) * ta + pltpu.roll(y, ROPE_PAIRS, 1) * tb


def _rope_transposed(dy, tc, ta, tb):
    return dy * tc + pltpu.roll(dy * ta, ROPE_PAIRS, 1) + pltpu.roll(dy * tb, HEAD_DIM - ROPE_PAIRS, 1)


def _qkv_post(proj, tables, g_q, g_k, heads, kv_heads, tr, name):
    t = proj.shape[0]
    aw, kw = heads * HEAD_DIM, kv_heads * HEAD_DIM
    w3 = aw + 2 * kw

    def body(p_ref, tc_ref, ta_ref, tb_ref, gq_ref, gk_ref, q_ref, k_ref, v_ref):
        tabs = (tc_ref[...], ta_ref[...], tb_ref[...])

        def norm_rope(col, gain):
            xh = p_ref[:, col:col + HEAD_DIM]
            return _rope((xh * lax.rsqrt(_head_mean(xh * xh) + EPS)) * gain, *tabs).astype(_BF16)

        for h in range(heads):
            q_ref[h] = norm_rope(h * HEAD_DIM, gq_ref[...])
        for h in range(kv_heads):
            k_ref[h] = norm_rope(aw + h * HEAD_DIM, gk_ref[...])
            v_ref[h] = p_ref[:, aw + kw + h * HEAD_DIM:aw + kw + (h + 1) * HEAD_DIM].astype(_BF16)

    tab = pl.BlockSpec((tr, HEAD_DIM), lambda i: (i, 0))
    gain = pl.BlockSpec((1, HEAD_DIM), lambda i: (0, 0))
    return pl.pallas_call(
        body, name=name, grid=(t // tr,),
        in_specs=[pl.BlockSpec((tr, w3), lambda i: (i, 0)), tab, tab, tab, gain, gain],
        out_specs=[pl.BlockSpec((heads, tr, HEAD_DIM), lambda i: (0, i, 0)),
                   pl.BlockSpec((kv_heads, tr, HEAD_DIM), lambda i: (0, i, 0)),
                   pl.BlockSpec((kv_heads, tr, HEAD_DIM), lambda i: (0, i, 0))],
        out_shape=[jax.ShapeDtypeStruct((heads, t, HEAD_DIM), _BF16),
                   jax.ShapeDtypeStruct((kv_heads, t, HEAD_DIM), _BF16),
                   jax.ShapeDtypeStruct((kv_heads, t, HEAD_DIM), _BF16)],
        compiler_params=pltpu.CompilerParams(dimension_semantics=("parallel",)),
    )(proj, *tables, g_q, g_k)


def _qkv_post_backward(proj, dq, dk, dv, tables, g_q, g_k, dproj, l, tr, name):
    t = proj.shape[0]
    heads, kv_heads = dq.shape[0], dk.shape[0]
    aw, kw = heads * HEAD_DIM, kv_heads * HEAD_DIM
    w3 = aw + 2 * kw
    nbl = (t - l) // tr

    def body(p_ref, dq_ref, dk_ref, dv_ref, tc_ref, ta_ref, tb_ref, gq_ref, gk_ref, dproj_ref, o_ref, dgq_ref, dgk_ref):
        del dproj_ref
        i = pl.program_id(0)

        @pl.when(i == 0)
        def _():
            dgq_ref[...] = jnp.zeros_like(dgq_ref)
            dgk_ref[...] = jnp.zeros_like(dgk_ref)

        tabs = (tc_ref[...], ta_ref[...], tb_ref[...])
        latent = i < nbl

        def back(col, dout, gain, dg_ref):
            xh = p_ref[:, col:col + HEAD_DIM]
            r = lax.rsqrt(_head_mean(xh * xh) + EPS)
            xn = xh * r
            dy = _rope_transposed(dout, *tabs)
            dg_ref[0:1, :] += jnp.sum(dy * xn, axis=0, keepdims=True)
            dn = dy * gain
            o_ref[:, col:col + HEAD_DIM] = (r * (dn - xn * _head_mean(dn * xn))).astype(_BF16)

        for h in range(heads):
            back(h * HEAD_DIM, jnp.where(latent, dq_ref[h], 0.0), gq_ref[...], dgq_ref)
        for h in range(kv_heads):
            back(aw + h * HEAD_DIM, dk_ref[h], gk_ref[...], dgk_ref)
            o_ref[:, aw + kw + h * HEAD_DIM:aw + kw + (h + 1) * HEAD_DIM] = dv_ref[h].astype(_BF16)

    tab = pl.BlockSpec((tr, HEAD_DIM), lambda i: (i, 0))
    gain = pl.BlockSpec((1, HEAD_DIM), lambda i: (0, 0))
    acc = pl.BlockSpec((SUBLANES, HEAD_DIM), lambda i: (0, 0))
    return pl.pallas_call(
        body, name=name, grid=(t // tr,),
        in_specs=[pl.BlockSpec((tr, w3), lambda i: (i, 0)),
                  pl.BlockSpec((heads, tr, HEAD_DIM), lambda i: (0, jnp.minimum(i, nbl - 1), 0)),
                  pl.BlockSpec((kv_heads, tr, HEAD_DIM), lambda i: (0, i, 0)),
                  pl.BlockSpec((kv_heads, tr, HEAD_DIM), lambda i: (0, i, 0)),
                  tab, tab, tab, gain, gain, _ANY],
        out_specs=[pl.BlockSpec((tr, w3), lambda i: (i, 0)), acc, acc],
        out_shape=[jax.ShapeDtypeStruct(dproj.shape, dproj.dtype), jax.ShapeDtypeStruct((SUBLANES, HEAD_DIM), _F32),
                   jax.ShapeDtypeStruct((SUBLANES, HEAD_DIM), _F32)],
        input_output_aliases={9: 0},
        compiler_params=pltpu.CompilerParams(dimension_semantics=("arbitrary",)),
    )(proj, dq, dk, dv, *tables, g_q, g_k, dproj)


def _zero_context_rows(dproj, l, w3, tr, name):
    t, iw = dproj.shape
    first = (t - l) // tr

    def body(dproj_ref, o_ref):
        del dproj_ref
        o_ref[...] = jnp.zeros_like(o_ref)

    return pl.pallas_call(
        body, name=name, grid=(l // tr, iw // w3 - 1),
        in_specs=[_ANY], out_specs=pl.BlockSpec((tr, w3), lambda i, j: (first + i, j + 1)),
        out_shape=jax.ShapeDtypeStruct(dproj.shape, dproj.dtype), input_output_aliases={0: 0},
        compiler_params=pltpu.CompilerParams(dimension_semantics=("parallel", "parallel")),
    )(dproj)


def _attention(q, k, v, proj, l, mix, tq, name):
    heads, t, _ = q.shape
    kv_heads = k.shape[0]
    n = t - l
    gw = GQA_GROUP * HEAD_DIM
    aw = heads * HEAD_DIM
    gate_col = (aw + 2 * kv_heads * HEAD_DIM) // gw

    def body(q_ref, k_ref, v_ref, g_ref, o_ref, y_ref, lse_ref):
        lane = lax.broadcasted_iota(jnp.int32, (tq, LANES), 1)
        lse_blk = jnp.zeros((tq, LANES), _F32)
        firsts = list(range(0, GQA_GROUP, ATTN_SUB_HEADS))

        def scores(first):
            qs = q_ref[first:first + ATTN_SUB_HEADS].reshape(ATTN_SUB_HEADS * tq, HEAD_DIM)
            return lax.dot_general(qs, k_ref[0], (((1,), (1,)), ((), ())), preferred_element_type=_F32)

        raw_next = scores(firsts[0])
        for idx, first in enumerate(firsts):
            raw = raw_next
            if idx + 1 < len(firsts):
                raw_next = scores(firsts[idx + 1])
            m = jnp.max(raw, axis=-1, keepdims=True)
            p = jnp.exp2((raw - m) * (ATTN_SCALE * LOG2_E))
            denom = jnp.sum(p, axis=-1, keepdims=True)
            os_ = jnp.dot(p.astype(_BF16), v_ref[0], preferred_element_type=_F32) / denom
            lse_s = m * ATTN_SCALE + jnp.log(denom)
            for j in range(ATTN_SUB_HEADS):
                g = first + j
                og = os_[j * tq:(j + 1) * tq]
                cols = slice(g * HEAD_DIM, (g + 1) * HEAD_DIM)
                o_ref[:, cols] = og
                y_ref[:, cols] = (og * _silu(g_ref[:, cols])).astype(_BF16)
                lse_blk = jnp.where(lane == g, lse_s[j * tq:(j + 1) * tq], lse_blk)
        lse_ref[0] = lse_blk

    return pl.pallas_call(
        body, name=name, grid=(kv_heads, n // tq),
        in_specs=[pl.BlockSpec((GQA_GROUP, tq, HEAD_DIM), lambda h, i: (h, i, 0)),
                  pl.BlockSpec((1, t, HEAD_DIM), lambda h, i: (h, 0, 0)),
                  pl.BlockSpec((1, t, HEAD_DIM), lambda h, i: (h, 0, 0)),
                  pl.BlockSpec((tq, gw), lambda h, i: (i, gate_col + h))],
        out_specs=[pl.BlockSpec((tq, gw), lambda h, i: (i, h)),
                   pl.BlockSpec((tq, gw), lambda h, i: (i, h)),
                   pl.BlockSpec((1, tq, LANES), lambda h, i: (h, i, 0))],
        out_shape=[jax.ShapeDtypeStruct((n, aw), _F32), jax.ShapeDtypeStruct((n, mix), _BF16),
                   jax.ShapeDtypeStruct((kv_heads, n, LANES), _F32)],
        compiler_params=pltpu.CompilerParams(dimension_semantics=("parallel", "parallel")),
    )(q, k, v, proj)


def _attention_backward(q, k, v, attn_o, dy, proj, lse, after, dproj, l, tq, name):
    heads, t, _ = q.shape
    kv_heads = k.shape[0]
    n = t - l
    gw = GQA_GROUP * HEAD_DIM
    aw = heads * HEAD_DIM
    gate_col = (aw + 2 * kv_heads * HEAD_DIM) // gw
    n_parts = next(p for p in (ATTN_KEY_PARTS, 2, 1) if t % (p * BF16_ROWS) == 0)
    part = t // n_parts

    def body(q_ref, k_ref, v_ref, o_ref, dy_ref, g_ref, lse_ref, after_ref, dproj_ref, dq_ref, dg_ref, dk_ref, dv_ref):
        del after_ref, dproj_ref

        @pl.when(pl.program_id(1) == 0)
        def _():
            dk_ref[...] = jnp.zeros_like(dk_ref)
            dv_ref[...] = jnp.zeros_like(dv_ref)

        lse_blk = lse_ref[0]
        for first in range(0, GQA_GROUP, ATTN_BWD_SUB_HEADS):
            qs = q_ref[first:first + ATTN_BWD_SUB_HEADS].reshape(ATTN_BWD_SUB_HEADS * tq, HEAD_DIM)
            do_parts, delta_parts, lse_parts = [], [], []
            for g in range(first, first + ATTN_BWD_SUB_HEADS):
                cols = slice(g * HEAD_DIM, (g + 1) * HEAD_DIM)
                gate, og, dyg = g_ref[:, cols], o_ref[:, cols], dy_ref[:, cols]
                dog = dyg * _silu(gate)
                dg_ref[:, cols] = (dyg * og * _silu_grad(gate)).astype(_BF16)
                do_parts.append(dog)
                delta_parts.append(jnp.sum(dog * og, axis=-1, keepdims=True))
                lse_parts.append(lse_blk[:, g:g + 1])
            dos = jnp.concatenate(do_parts, axis=0).astype(_BF16)
            delta = jnp.concatenate(delta_parts, axis=0)
            lse2 = jnp.concatenate(lse_parts, axis=0) * LOG2_E
            dqs = jnp.zeros((ATTN_BWD_SUB_HEADS * tq, HEAD_DIM), _F32)
            for part_i in range(n_parts):
                keys = slice(part_i * part, (part_i + 1) * part)
                ks, vs = k_ref[0, keys, :], v_ref[0, keys, :]
                raw = lax.dot_general(qs, ks, (((1,), (1,)), ((), ())), preferred_element_type=_F32)
                p = jnp.exp2(raw * (ATTN_SCALE * LOG2_E) - lse2)
                dp = lax.dot_general(dos, vs, (((1,), (1,)), ((), ())), preferred_element_type=_F32)
                ds = (p * (dp - delta)).astype(_BF16)
                dqs = dqs + jnp.dot(ds, ks, preferred_element_type=_F32)
                dk_ref[0, keys, :] += ATTN_SCALE * lax.dot_general(
                    ds, qs, (((0,), (0,)), ((), ())), preferred_element_type=_F32)
                dv_ref[0, keys, :] += lax.dot_general(
                    p.astype(_BF16), dos, (((0,), (0,)), ((), ())), preferred_element_type=_F32)
            dq_ref[first:first + ATTN_BWD_SUB_HEADS] = (ATTN_SCALE * dqs).reshape(ATTN_BWD_SUB_HEADS, tq, HEAD_DIM)

    kv_spec = pl.BlockSpec((1, t, HEAD_DIM), lambda h, i: (h, 0, 0))
    tok = pl.BlockSpec((tq, gw), lambda h, i: (i, h))
    gate = pl.BlockSpec((tq, gw), lambda h, i: (i, gate_col + h))
    return pl.pallas_call(
        body, name=name, grid=(kv_heads, n // tq),
        in_specs=[pl.BlockSpec((GQA_GROUP, tq, HEAD_DIM), lambda h, i: (h, i, 0)), kv_spec, kv_spec,
                  tok, tok, gate, pl.BlockSpec((1, tq, LANES), lambda h, i: (h, i, 0)),
                  pl.BlockSpec(after.shape, lambda h, i: (0, 0)), _ANY],
        out_specs=[pl.BlockSpec((GQA_GROUP, tq, HEAD_DIM), lambda h, i: (h, i, 0)), gate, kv_spec, kv_spec],
        out_shape=[jax.ShapeDtypeStruct((heads, n, HEAD_DIM), _F32), jax.ShapeDtypeStruct(dproj.shape, dproj.dtype),
                   jax.ShapeDtypeStruct((kv_heads, t, HEAD_DIM), _F32), jax.ShapeDtypeStruct((kv_heads, t, HEAD_DIM), _F32)],
        input_output_aliases={8: 1},
        compiler_params=pltpu.CompilerParams(dimension_semantics=("parallel", "arbitrary")),
    )(q, k, v, attn_o, dy, proj, lse, after, dproj)


def _halo_specs(tp, width, col, row_off, total_rows):
    per = tp // POOL_HALO
    first = row_off // POOL_HALO
    last = total_rows // POOL_HALO - 1
    return [pl.BlockSpec((tp, width), lambda i: (i + row_off // tp, col)),
            pl.BlockSpec((POOL_HALO, width), lambda i: (jnp.maximum(first + i * per - 1, 0), col)),
            pl.BlockSpec((POOL_HALO, width), lambda i: (jnp.minimum(first + (i + 1) * per, last), col))]


def _with_halo(cur, prev, nxt, t0, n):
    tp = cur.shape[0]
    r8 = lax.broadcasted_iota(jnp.int32, (POOL_HALO, 1), 0)
    prev = jnp.where(t0 - POOL_HALO + r8 >= 0, prev, 0.0)
    nxt = jnp.where(t0 + tp + r8 < n, nxt, 0.0)
    return jnp.concatenate([prev, cur, nxt], axis=0)


def _shift_rows(a, s):
    return pltpu.roll(a, s % a.shape[0], 0)


def _window_sum(e, w, mirrored):
    a = e + _shift_rows(e, -1 if mirrored else 1)
    s = 1
    while 2 * s < w:
        a = _shift_rows(a, s) + _shift_rows(a, -s)
        s *= 2
    return a


def _window_count(t, w, n):
    half = w // 2
    return (jnp.minimum(t + half, n) - jnp.maximum(t - half, 0)).astype(_F32)


def _pool_forward(gi, proj, y, pool_w, pool_scale, l, heads, kv_heads, tp, name):
    t = proj.shape[0]
    n = t - l
    pg = pool_w.shape[-1]
    w = POOL_WINDOWS[gi]
    aw, kw = heads * HEAD_DIM, kv_heads * HEAD_DIM
    u_col = (2 * aw + 2 * kw) // pg + gi
    gate_col = (2 * aw + 2 * kw + len(POOL_WINDOWS) * pg) // pg + gi

    def body(u_ref, up_ref, un_ref, g_ref, w_ref, sc_ref, y_in_ref, y_ref, raw_ref, d_ref):
        del y_in_ref
        t0 = pl.program_id(0) * tp
        cur = u_ref[...]
        win = _window_sum(_with_halo(cur, up_ref[...], un_ref[...], t0, n), w, False)[POOL_HALO:POOL_HALO + tp]
        tok = t0 + lax.broadcasted_iota(jnp.int32, (tp, 1), 0)
        d = (win / _window_count(tok, w, n) - cur).astype(_BF16)
        raw = jnp.dot(d, w_ref[...].reshape(pg, pg), preferred_element_type=_F32)
        d_ref[...] = d
        raw_ref[...] = raw
        y_ref[...] = ((raw * sc_ref[...]) * _silu(g_ref[...])).astype(_BF16)

    blk = pl.BlockSpec((tp, pg), lambda i: (i, 0))
    return pl.pallas_call(
        body, name=name, grid=(n // tp,),
        in_specs=_halo_specs(tp, pg, u_col, 0, t) + [
            pl.BlockSpec((tp, pg), lambda i: (i, gate_col)),
            pl.BlockSpec((N_DEV, 1, pg // N_DEV, pg), lambda i: (0, gi, 0, 0)),
            pl.BlockSpec((1, pg), lambda i: (0, gi)), _ANY],
        out_specs=[pl.BlockSpec((tp, pg), lambda i: (i, aw // pg + gi)), blk, blk],
        out_shape=[jax.ShapeDtypeStruct(y.shape, y.dtype), jax.ShapeDtypeStruct((n, pg), _F32),
                   jax.ShapeDtypeStruct((n, pg), _BF16)],
        input_output_aliases={6: 0},
        compiler_params=pltpu.CompilerParams(dimension_semantics=("arbitrary",)),
    )(proj, proj, proj, proj, pool_w, pool_scale, y)


def _pool_backward_gate(gi, dy, proj, raw, pool_w, pool_scale, dproj, l, heads, kv_heads, tp, name):
    n, pg = raw.shape
    aw, kw = heads * HEAD_DIM, kv_heads * HEAD_DIM
    gate_col = (2 * aw + 2 * kw + len(POOL_WINDOWS) * pg) // pg + gi

    def body(dy_ref, g_ref, raw_ref, w_ref, sc_ref, dproj_ref, dg_ref, dr_ref, dd_ref, ds_ref):
        del dproj_ref

        @pl.when(pl.program_id(0) == 0)
        def _():
            ds_ref[...] = jnp.zeros_like(ds_ref)

        gate, rawv, dyv, scale = g_ref[...], raw_ref[...], dy_ref[...], sc_ref[...]
        dpool = dyv * _silu(gate)
        dg_ref[...] = (dyv * (rawv * scale) * _silu_grad(gate)).astype(_BF16)
        ds_ref[0:1, :] += jnp.sum(dpool * rawv, axis=0, keepdims=True)
        draw = (dpool * scale).astype(_BF16)
        dr_ref[...] = draw
        dd_ref[...] = lax.dot_general(
            draw, w_ref[...].reshape(pg, pg), (((1,), (1,)), ((), ())), preferred_element_type=_F32)

    blk = pl.BlockSpec((tp, pg), lambda i: (i, 0))
    gate = pl.BlockSpec((tp, pg), lambda i: (i, gate_col))
    return pl.pallas_call(
        body, name=name, grid=(n // tp,),
        in_specs=[pl.BlockSpec((tp, pg), lambda i: (i, aw // pg + gi)), gate, blk,
                  pl.BlockSpec((N_DEV, 1, pg // N_DEV, pg), lambda i: (0, gi, 0, 0)),
                  pl.BlockSpec((1, pg), lambda i: (0, gi)), _ANY],
        out_specs=[gate, blk, blk, pl.BlockSpec((SUBLANES, pg), lambda i: (0, 0))],
        out_shape=[jax.ShapeDtypeStruct(dproj.shape, dproj.dtype), jax.ShapeDtypeStruct((n, pg), _BF16),
                   jax.ShapeDtypeStruct((n, pg), _F32), jax.ShapeDtypeStruct((SUBLANES, pg), _F32)],
        input_output_aliases={5: 0},
        compiler_params=pltpu.CompilerParams(dimension_semantics=("arbitrary",)),
    )(dy, proj, raw, pool_w, pool_scale, dproj)


def _pool_backward_window(gi, dd, dproj, l, col, tp, name):
    n, pg = dd.shape
    w = POOL_WINDOWS[gi]

    def body(c_ref, p_ref, n_ref, dproj_ref, du_ref):
        del dproj_ref
        t0 = pl.program_id(0) * tp
        cur = c_ref[...]
        e = _with_halo(cur, p_ref[...], n_ref[...], t0, n)
        tok = t0 - POOL_HALO + lax.broadcasted_iota(jnp.int32, (tp + 2 * POOL_HALO, 1), 0)
        e = e / jnp.maximum(_window_count(tok, w, n), 1.0)
        du_ref[...] = (_window_sum(e, w, True)[POOL_HALO:POOL_HALO + tp] - cur).astype(_BF16)

    return pl.pallas_call(
        body, name=name, grid=(n // tp,),
        in_specs=_halo_specs(tp, pg, 0, 0, n) + [_ANY],
        out_specs=pl.BlockSpec((tp, pg), lambda i: (i, col)),
        out_shape=jax.ShapeDtypeStruct(dproj.shape, dproj.dtype), input_output_aliases={3: 0},
        compiler_params=pltpu.CompilerParams(dimension_semantics=("arbitrary",)),
    )(dd, dd, dd, dproj)


def _post(out, x, target, gate, g_post, tr, name):
    n, d = out.shape

    def body(o_ref, x_ref, t_ref, gate_ref, g_ref, dxn_ref, do_ref, dgate_ref, dg_ref, loss_ref):
        @pl.when(pl.program_id(0) == 0)
        def _():
            dgate_ref[...] = jnp.zeros_like(dgate_ref)
            dg_ref[...] = jnp.zeros_like(dg_ref)
            loss_ref[...] = jnp.zeros_like(loss_ref)

        ov = o_ref[...]
        r = _rms(ov)
        on = ov * r
        normed = on * g_ref[...]
        err = (x_ref[...] + gate_ref[...] * normed) - t_ref[...]
        loss_ref[...] += jnp.sum(err * err)
        dxn = err / d
        dxn_ref[...] = dxn
        dgate_ref[0:1, :] += jnp.sum(dxn * normed, axis=0, keepdims=True)
        dr = dxn * gate_ref[...]
        dg_ref[0:1, :] += jnp.sum(dr * on, axis=0, keepdims=True)
        dn = dr * g_ref[...]
        do_ref[...] = (r * (dn - on * jnp.mean(dn * on, axis=-1, keepdims=True))).astype(_BF16)

    blk = pl.BlockSpec((tr, d), lambda i: (i, 0))
    vec = pl.BlockSpec((1, d), lambda i: (0, 0))
    acc = pl.BlockSpec((SUBLANES, d), lambda i: (0, 0))
    return pl.pallas_call(
        body, name=name, grid=(n // tr,),
        in_specs=[blk, blk, blk, vec, vec],
        out_specs=[blk, blk, acc, acc, pl.BlockSpec((SUBLANES, LANES), lambda i: (0, 0))],
        out_shape=[jax.ShapeDtypeStruct((n, d), _F32), jax.ShapeDtypeStruct((n, d), _BF16),
                   jax.ShapeDtypeStruct((SUBLANES, d), _F32), jax.ShapeDtypeStruct((SUBLANES, d), _F32),
                   jax.ShapeDtypeStruct((SUBLANES, LANES), _F32)],
        compiler_params=pltpu.CompilerParams(dimension_semantics=("arbitrary",)),
    )(out, x, target, gate, g_post)


def _adam_sharded(slab_ids, grad, got, far, w, m, v, name):
    r, c = w.shape
    tr = _tile(r, max(BF16_ROWS, min(256, (3 << 17) // c)), BF16_ROWS)

    def body(ids_ref, own_ref, got_ref, far_ref, w_ref, m_ref, v_ref, g_ref, dl_ref, nm_ref, nv_ref):
        del ids_ref
        g = own_ref[0] + got_ref[0]
        for k in range(N_CHIPS - 1):
            g = g + far_ref[k].astype(_F32)
        delta, nm, nv = _adamw(w_ref[...], g, m_ref[...], v_ref[...])
        g_ref[...] = g
        dl_ref[...] = delta
        nm_ref[...] = nm
        nv_ref[...] = nv

    blk = pl.BlockSpec((tr, c), lambda i, ids: (i, 0))
    return pl.pallas_call(
        body, name=name,
        grid_spec=pltpu.PrefetchScalarGridSpec(
            num_scalar_prefetch=1, grid=(r // tr,),
            in_specs=[pl.BlockSpec((1, tr, c), lambda i, ids: (ids[0], i, 0)),
                      pl.BlockSpec((1, tr, c), lambda i, ids: (0, i, 0)),
                      pl.BlockSpec((N_CHIPS - 1, tr, c), lambda i, ids: (0, i, 0)), blk, blk, blk],
            out_specs=[blk] * 4),
        out_shape=[jax.ShapeDtypeStruct((r, c), _F32)] * 4,
        compiler_params=pltpu.CompilerParams(dimension_semantics=("parallel",)),
    )(slab_ids, grad, got, far, w, m, v)


def _adam_replicated(parts, extra, through_silu, w, m, v, name):
    def body(p_ref, e_ref, s_ref, w_ref, m_ref, v_ref, g_ref, dl_ref, nm_ref, nv_ref):
        total = p_ref[0] + e_ref[0]
        for dev in range(1, N_DEV):
            total = total + (p_ref[dev] + e_ref[dev])
        g = jnp.where(s_ref[...] > 0.5, total * _silu_grad(w_ref[...]), total)
        delta, nm, nv = _adamw(w_ref[...], g, m_ref[...], v_ref[...])
        g_ref[...] = g
        dl_ref[...] = delta
        nm_ref[...] = nm
        nv_ref[...] = nv

    return pl.pallas_call(
        body, name=name, in_specs=[_VMEM] * 6, out_specs=[_VMEM] * 4,
        out_shape=[jax.ShapeDtypeStruct(w.shape, _F32)] * 4,
    )(parts, extra, through_silu, w, m, v)


def _as_rows(vec):
    size = vec.shape[0]
    padded = -(-size // (SUBLANES * LANES)) * SUBLANES * LANES
    return jnp.pad(vec, (0, padded - size)).reshape(padded // LANES, LANES)


def kernel(x, c, ctx, c_ctx, w_ada, b_ada, norm_pre, norm_post, w_in, q_norm, k_norm, pool_w, pool_scale, w_out, loss_target, m_c_ctx, m_w_ada, m_b_ada, m_norm_pre, m_norm_post, m_w_in, m_q_norm, m_k_norm, m_pool_w, m_pool_scale, m_w_out, v_c_ctx, v_w_ada, v_b_ada, v_norm_pre, v_norm_post, v_w_in, v_q_norm, v_k_norm, v_pool_w, v_pool_scale, v_w_out):
    me = _dev_index(*_position())
    x2, ctx2, target = x[0], ctx[0], loss_target[0]
    n, d = x2.shape
    l = ctx2.shape[0]
    t = l + n
    aw = d // 2
    heads = aw // HEAD_DIM
    kv_heads = heads // GQA_GROUP
    kw = kv_heads * HEAD_DIM
    n_groups = len(POOL_WINDOWS)
    pg = (d - aw) // n_groups
    mix = d
    tr = _tile(l, 128, BF16_ROWS)
    tr2 = _tile(l, 256, BF16_ROWS)
    tq2 = _tile(l, 256, BF16_ROWS)
    tp = _tile(n, 1024, POOL_HALO)

    xi, yi, ci = _position()
    slab_ids = jnp.stack([_dev_index(*chip, ci) for chip in _chip_order(xi, yi)]).astype(jnp.int32)

    cw = w_in.shape[-1]
    wg = _cast_into_columns(slab_ids, w_in[0], N_DEV, "cast_w_in")
    late = [_cast_into_slab(slab_ids, w_out[0], "cast_w_out"),
            _cast_into_slab(slab_ids, pool_w[0].reshape(-1, pg), "cast_pool_w")]

    c_all = _all_gather_small(_as_rows(c[0]), "gather_c").reshape(N_DEV, -1)[:, :d]
    craw = jnp.concatenate([c_all, jnp.broadcast_to(c_ctx[None], (N_DEV, d))], axis=0)
    ada = _ada_forward(craw, w_ada[0], "ada_forward")
    ada_all = _all_gather_small(ada, "gather_ada")
    mod_all = ada_all.transpose(1, 0, 2).reshape(ada.shape[0], -1) + b_ada[0]
    mod = lax.dynamic_index_in_dim(mod_all, me, 0, keepdims=False)
    mod_c = mod_all[N_DEV]
    shift, scale, gate = mod[:d], mod[d:2 * d], mod[2 * d:]
    zeros6 = jnp.zeros((SUBLANES - 2, d), _F32)
    mods = jnp.stack([jnp.concatenate([mod_c[None, d:2 * d], mod_c[None, :d], zeros6], axis=0),
                      jnp.concatenate([scale[None], shift[None], zeros6], axis=0)])

    a_s, a_r, wg, tok = _w_in_hop(wg, cw, [], "a", ada_all, "gather_w_in_a")
    h_all = _prenorm(ctx2, x2, norm_pre, mods + tok[0, 0], tr2, "prenorm")
    order_ids = jnp.stack([_dev_index(*dev) for dev in _w_in_order(xi, yi, ci)]).astype(jnp.int32)
    proj = lax.empty((t, N_DEV * cw), _F32)
    proj = _proj_blocks(h_all, wg, proj, order_ids, 0, 1, cw, tok, "proj_0")
    b_s, b_r, wg, tok = _w_in_hop(wg, cw, [("a", a_s, a_r, [0, 1], [])], "b", proj, "gather_w_in_b")
    proj = _proj_blocks(h_all, wg, proj, order_ids, 1, 2, cw, tok, "proj_1")
    c_s, c_r, wg, tok = _w_in_hop(wg, cw, [("b", b_s, b_r, [2, 0], [])], "c", proj, "gather_w_in_c")
    proj = _proj_blocks(h_all, wg, proj, order_ids, 3, 2, cw, tok, "proj_2")
    d_s, d_r, wg, tok = _w_in_hop(
        wg, cw, [("c", c_s, c_r, [0], []), ("b", b_s, b_r, [1], [])], "d", proj, "gather_w_in_d")
    proj = _proj_blocks(h_all, wg, proj, order_ids, 5, 2, cw, tok, "proj_3")
    w_in_g, tok = _w_in_hop(
        wg, cw, [("d", d_s, d_r, [0], [0]), ("a", a_s, a_r, [], [0, 1]), ("b", b_s, b_r, [], [0, 1, 2]),
                 ("c", c_s, c_r, [], [0])], None, proj, "gather_w_in_end")
    flight_w = _gather_slabs_start(late, w_in_g, "gather_late_start")
    proj = _proj_blocks(h_all, w_in_g, proj, order_ids, 7, 1, cw, flight_w[-1], "proj_4")
    tables = _rope_tables(l, n)
    q, k, v = _qkv_post(proj, tables, q_norm, k_norm, heads, kv_heads, tr2, "qkv_post")
    attn_o, y, lse = _attention(q, k, v, proj, l, mix, tq2, "attention")
    w_out_g8, pool_g8 = _gather_slabs_wait(*flight_w[:3], attn_o, "gather_late_wait")
    w_out_g = w_out_g8.reshape(mix, d)
    pool_g = pool_g8.reshape(N_DEV, n_groups, pg // N_DEV, pg)
    raws, ds = [], []
    for gi in range(n_groups):
        y, raw, dsave = _pool_forward(gi, proj, y, pool_g, pool_scale, l, heads, kv_heads, tp, f"pool_forward_{gi}")
        raws.append(raw)
        ds.append(dsave)
    out = _matmul(y, w_out_g, name="out_proj")
    dxn, dout, dgate8, dgpost8, loss8 = _post(out, x2, target, gate[None], norm_post, tr2, "post")

    gw_out = _matmul(y, dout, ta=True, name="grad_w_out").reshape(N_DEV, mix // N_DEV, d)
    flight_so = _exchange_start(_sibling_copies_by_device, [gw_out], "exchange_sibling_start_w_out", land_slabs=N_CHIPS)
    dy = _matmul(dout, w_out_g, tb=True, after=flight_so[-1], name="d_y")
    gw_out, got_out = _exchange_wait(
        _sibling_copies_by_device, *flight_so[:4], dy, "exchange_sibling_wait_w_out", with_sources=True)
    sum_out = _pre_add(slab_ids, gw_out, got_out, "pre_add_w_out")
    flight_out = _exchange_start(_chip_copies, [sum_out], "exchange_chips_start_w_out")
    w3 = aw + 2 * kw
    dq, dproj, dk, dv = _attention_backward(
        q, k, v, attn_o, dy, proj, lse, flight_out[-1], lax.empty(proj.shape, _BF16), l, tq2, "attention_backward")
    dproj, dgq8, dgk8 = _qkv_post_backward(proj, dq, dk, dv, tables, q_norm, k_norm, dproj, l, tr2, "qkv_post_backward")
    dproj = _zero_context_rows(dproj, l, w3, tr, "zero_context_rows")
    gpw, dps8 = [], []
    for gi in range(n_groups):
        dproj, draw, dd, dps = _pool_backward_gate(
            gi, dy, proj, raws[gi], pool_g, pool_scale, dproj, l, heads, kv_heads, tp, f"pool_backward_gate_{gi}")
        dproj = _pool_backward_window(gi, dd, dproj, l, (w3 + aw) // pg + gi, tp, f"pool_backward_window_{gi}")
        dps8.append(dps)
        gpw.append(_matmul(ds[gi], draw, ta=True, name=f"grad_pool_w_{gi}"))
    cw = w_in.shape[-1]
    other_ids = jnp.stack([_dev_index(*chip, 1 - ci) for chip in _chip_order(xi, yi)]).astype(jnp.int32)
    chip_slabs = jnp.arange(N_CHIPS, dtype=jnp.int32)
    pr = pool_w.shape[2]
    gpw8 = jnp.stack(gpw).reshape(n_groups, N_DEV, pr, pg).transpose(1, 0, 2, 3).reshape(N_DEV, n_groups * pr, pg)
    give_in = _matmul_slabs(h_all, dproj, other_ids, cw, "grad_w_in_sibling")
    flight_sib = _exchange_start(_sibling_copies, [give_in, jnp.take(gpw8, other_ids, axis=0)], "exchange_sibling_start")
    gw_in = _matmul_slabs(h_all, dproj, slab_ids, cw, "grad_w_in_own", after=flight_sib[-1])
    gpw_own = jnp.take(gpw8, slab_ids, axis=0)
    got_in, got_pw = _exchange_wait(_sibling_copies, *flight_sib[:4], gw_in, "exchange_sibling_wait")
    sums_in = [_pre_add(chip_slabs, gw_in, got_in, "pre_add_w_in"), _pre_add(chip_slabs, gpw_own, got_pw, "pre_add_pool_w")]
    flight_in = _exchange_start(_chip_copies, sums_in, "exchange_chips_start_w_in")
    dh = _matmul(dproj, w_in_g, tb=True, tm=1088, tk=3072, after=flight_in[-1], name="d_h")
    grad_x, dmods, dgpre8 = _prenorm_backward(dh, ctx2, x2, dxn, norm_pre, mods, tr, "prenorm_backward")

    dmod_lat = jnp.concatenate([dmods[1, 1], dmods[1, 0], dgate8[0]])
    dmod_ctx = jnp.concatenate([dmods[0, 1], dmods[0, 0], jnp.zeros((d,), _F32)])
    small = jnp.concatenate([dmod_lat, dmod_ctx, dgpre8[0], dgpost8[0], dgq8[0], dgk8[0]] + [p[0] for p in dps8]
                            + [loss8[0, :1]])
    gathered = _all_gather_small(_as_rows(small), "gather_small").reshape(N_DEV, -1)
    o = 0
    take = lambda size: (gathered[:, o:o + size], o + size)
    g_mod, o = take(3 * d)
    g_modc, o = take(3 * d)
    g_pre, o = take(d)
    g_post, o = take(d)
    g_q, o = take(HEAD_DIM)
    g_k, o = take(HEAD_DIM)
    g_ps, o = take(n_groups * pg)
    g_loss, o = take(1)
    cols = w_ada.shape[-1]
    mine = lambda a: lax.dynamic_slice_in_dim(a, me * cols, cols, axis=1)
    dmod_rows = jnp.concatenate([mine(g_mod), mine(g_modc)], axis=0)
    g_wada, dl_wada, nm_wada, nv_wada, dcact = _ada_backward(craw, dmod_rows, w_ada[0], m_w_ada[0], v_w_ada[0], "ada_backward")
    dcc = _all_gather_small(_as_rows(dcact[0]), "gather_dcc").reshape(N_DEV, -1)[:, :d]

    sizes = [d, 3 * d, d, d, HEAD_DIM, HEAD_DIM, n_groups * pg]
    def pack(parts):
        rows = jnp.concatenate(parts, axis=1)
        padded = -(-rows.shape[1] // (SUBLANES * LANES)) * SUBLANES * LANES
        return jnp.pad(rows, ((0, 0), (0, padded - rows.shape[1]))).reshape(N_DEV, padded // LANES, LANES)

    zero = lambda size: jnp.zeros((N_DEV, size), _F32)
    parts = pack([dcc, g_mod, g_pre, g_post, g_q, g_k, g_ps])
    extra = pack([zero(d), g_modc, zero(d), zero(d), zero(HEAD_DIM), zero(HEAD_DIM), zero(n_groups * pg)])
    through_silu = _as_rows(jnp.concatenate([jnp.ones((d,), _F32), jnp.zeros((sum(sizes[1:]),), _F32)]))
    cat = lambda items: _as_rows(jnp.concatenate([a.reshape(-1) for a in items]))
    ws = [c_ctx, b_ada, norm_pre, norm_post, q_norm, k_norm, pool_scale]
    ms = [m_c_ctx, m_b_ada, m_norm_pre, m_norm_post, m_q_norm, m_k_norm, m_pool_scale]
    vs = [v_c_ctx, v_b_ada, v_norm_pre, v_norm_post, v_q_norm, v_k_norm, v_pool_scale]
    rep = _adam_replicated(parts, extra, through_silu, cat(ws), cat(ms), cat(vs), "adam_replicated")

    def split(packed):
        flat_, outs, at = packed.reshape(-1), [], 0
        for w, size in zip(ws, sizes):
            outs.append(flat_[at:at + size].reshape(w.shape))
            at += size
        return outs

    g_rep, dl_rep, nm_rep, nv_rep = [split(r) for r in rep]

    far_out = _exchange_wait(_chip_copies, *flight_out[:4], grad_x, "exchange_chips_wait_w_out")[0]
    far_in, far_pw = _exchange_wait(_chip_copies, *flight_in[:4], rep[0], "exchange_chips_wait_w_in")
    two = lambda a: a.reshape(-1, a.shape[-1])
    sharded = []
    for ids, g, got, far, w, m, v_, name in zip(
            (chip_slabs, slab_ids, chip_slabs), (gw_in, gw_out, gpw_own), (got_in, got_out, got_pw),
            (far_in, far_out, far_pw), (w_in, w_out, pool_w), (m_w_in, m_w_out, m_pool_w),
            (v_w_in, v_w_out, v_pool_w), ("adam_w_in", "adam_w_out", "adam_pool_w")):
        res = _adam_sharded(ids, g, got, far, two(w), two(m), two(v_), name)
        sharded.append([r.reshape(w.shape) for r in res])
    (g_w_in, dl_w_in, nm_w_in, nv_w_in), (g_w_out, dl_w_out, nm_w_out, nv_w_out), (g_pw, dl_pw, nm_pw, nv_pw) = sharded

    loss_sum = g_loss[0, 0]
    for dev in range(1, N_DEV):
        loss_sum = loss_sum + g_loss[dev, 0]
    loss = (0.5 / d) * loss_sum

    def ordered(rep_list, ada_, w_in_, pw_, w_out_):
        return [rep_list[0], ada_[None], rep_list[1], rep_list[2], rep_list[3], w_in_, rep_list[4], rep_list[5],
                pw_, rep_list[6], w_out_]

    return (loss, grad_x[None],
            *ordered(g_rep, g_wada, g_w_in, g_pw, g_w_out),
            *ordered(dl_rep, dl_wada, dl_w_in, dl_pw, dl_w_out),
            *ordered(nm_rep, nm_wada, nm_w_in, nm_pw, nm_w_out),
            *ordered(nv_rep, nv_wada, nv_w_in, nv_pw, nv_w_out))
```

```python
import jax
import jax.numpy as jnp
from jax import lax
from jax.experimental import pallas as pl
from jax.experimental.pallas import tpu as pltpu

HEAD_DIM = 128
GQA_GROUP = 4
ATTN_SUB_HEADS = 1
ATTN_BWD_SUB_HEADS = 1
ATTN_KEY_PARTS = 2
LOG2_E = 1.4426950408889634
GRID_W = 64
ROPE_PAIRS = HEAD_DIM // 4
ROPE_THETA = 10000.0
ATTN_SCALE = HEAD_DIM ** -0.5
EPS = 1e-6
POOL_WINDOWS = (2, 4, 8, 16)
POOL_HALO = 8
N_DEV = 8
N_CHIPS = 4
ADAM_LR = 0.001
ADAM_B1 = 0.9
ADAM_B2 = 0.999
ADAM_EPS = 1e-08
ADAM_WD = 0.01
ADAM_STEP = 10

LANES = 128
SUBLANES = 8
BF16_ROWS = 16

_MESH = pl.DeviceIdType.MESH
_ANY = pl.BlockSpec(memory_space=pl.ANY)
_VMEM = pl.BlockSpec(memory_space=pltpu.VMEM)
_HBM = pl.BlockSpec(memory_space=pltpu.HBM)
_SEM = pl.BlockSpec(memory_space=pltpu.SEMAPHORE)
_EFFECT = pltpu.SideEffectType.DATAFLOW_SIDE_EFFECTING
_F32 = jnp.float32
_BF16 = jnp.bfloat16


def _tile(dim, pref, align):
    t = min(pref, dim)
    t -= t % align
    while t >= align:
        if dim % t == 0:
            return t
        t -= align
    return dim


def _position():
    return lax.axis_index("x"), lax.axis_index("y"), lax.axis_index("c")


def _flip(v, bit):
    return 1 - v if bit else v


def _dev_index(x, y, c):
    return 4 * x + 2 * y + c


def _silu(g):
    return g * jax.nn.sigmoid(g)


def _silu_grad(g):
    s = jax.nn.sigmoid(g)
    return s * (1.0 + g * (1.0 - s))


def _adamw(w, g, m, v):
    m = ADAM_B1 * m + (1.0 - ADAM_B1) * g
    v = ADAM_B2 * v + (1.0 - ADAM_B2) * (g * g)
    m_hat = m / (1.0 - ADAM_B1 ** ADAM_STEP)
    v_hat = v / (1.0 - ADAM_B2 ** ADAM_STEP)
    delta = -ADAM_LR * (m_hat / (jnp.sqrt(v_hat) + ADAM_EPS) + ADAM_WD * w)
    return delta, m, v


def _all_gather_small(v, name):
    rows, cols = v.shape

    def body(v_ref, out_ref, send_sems, recv_sems):
        x, y, c = _position()
        me = _dev_index(x, y, c)
        out_ref[me] = v_ref[...]
        peers = [(_flip(x, k & 4), _flip(y, k & 2), _flip(c, k & 1)) for k in range(1, N_DEV)]

        def copy(k, block, to):
            return pltpu.make_async_remote_copy(
                src_ref=v_ref, dst_ref=out_ref.at[block], send_sem=send_sems.at[k], recv_sem=recv_sems.at[k],
                device_id=to, device_id_type=_MESH)

        sends = [copy(k, me, p) for k, p in enumerate(peers)]
        for s in sends:
            s.start()
        for k, p in enumerate(peers):
            copy(k, _dev_index(*p), p).wait_recv()
        for s in sends:
            s.wait_send()

    return pl.pallas_call(
        body, name=name,
        out_shape=jax.ShapeDtypeStruct((N_DEV, rows, cols), v.dtype),
        in_specs=[_VMEM], out_specs=_VMEM,
        scratch_shapes=[pltpu.SemaphoreType.DMA((N_DEV - 1,)), pltpu.SemaphoreType.DMA((N_DEV - 1,))],
    )(v)


def _route(x, y, c):
    first = (x + (1 - c) * (1 - 2 * x), y + c * (1 - 2 * y))
    second = (x + c * (1 - 2 * x), y + (1 - c) * (1 - 2 * y))
    return first, second, (1 - x, 1 - y)


def _w_in_order(x, y, c):
    first, second, diagonal = _route(x, y, c)
    return [(x, y, c), (x, y, 1 - c), (*first, c), (*second, 1 - c), (*second, c), (*first, 1 - c),
            (*diagonal, c), (*diagonal, 1 - c)]


W_IN_HOPS = {"a": 2, "b": 3, "c": 1, "d": 1}


def _w_in_hop_copies(group, wg, width, send_sems, recv_sems):
    x, y, c = _position()
    me, sibling = (x, y, c), (x, y, 1 - c)
    first, second, diagonal = _route(x, y, c)

    def cp(k, block, to):
        cols = wg.at[:, pl.ds(pl.multiple_of(_dev_index(*block) * width, width), width)]
        return pltpu.make_async_remote_copy(
            src_ref=cols, dst_ref=cols, send_sem=send_sems.at[k], recv_sem=recv_sems.at[k],
            device_id=to, device_id_type=_MESH)

    if group == "a":
        return [cp(0, me, sibling), cp(1, me, (*first, c))]
    if group == "b":
        return [cp(0, me, (*second, c)), cp(1, (*first, c), (*second, c)), cp(2, (*first, c), sibling)]
    if group == "c":
        return [cp(0, (*second, c), sibling)]
    return [cp(0, (*diagonal, c), sibling)]


def _w_in_hop(wg, width, waits, start, after, name):
    n_sem = 2 * len(waits)

    def body(*refs):
        wg_ref = refs[0]
        for i, (group, _, _, arrivals, sends) in enumerate(waits):
            cps = _w_in_hop_copies(group, wg_ref, width, refs[1 + 2 * i], refs[2 + 2 * i])
            for k in arrivals:
                cps[k].wait_recv()
            for k in sends:
                cps[k].wait_send()
        if start:
            for cp in _w_in_hop_copies(start, wg_ref, width, refs[n_sem + 2], refs[n_sem + 3]):
                cp.start()
        refs[-1][...] = jnp.zeros_like(refs[-1])

    sems = [s for w in waits for s in w[1:3]]
    new = [pltpu.SemaphoreType.DMA((W_IN_HOPS[start],))] * 2 if start else []
    outs = pl.pallas_call(
        body, name=name,
        out_shape=(*new, pltpu.HBM(wg.shape, wg.dtype), jax.ShapeDtypeStruct((SUBLANES, LANES), _F32)),
        in_specs=[_HBM] + [_SEM] * n_sem + [_ANY], out_specs=(*[_SEM] * len(new), _HBM, _VMEM),
        input_output_aliases={0: len(new)},
        compiler_params=pltpu.CompilerParams(has_side_effects=_EFFECT),
    )(pltpu.with_memory_space_constraint(wg, pltpu.HBM), *sems, after)
    return outs


def _slab_copies(bufs, send_sems, recv_sems):
    x, y, c = _position()
    me = _dev_index(x, y, c)
    peers = [(_flip(x, k & 4), _flip(y, k & 2), _flip(c, k & 1)) for k in range(1, N_DEV)]
    return [pltpu.make_async_remote_copy(
        src_ref=buf.at[me], dst_ref=buf.at[me],
        send_sem=send_sems.at[(N_DEV - 1) * a + k], recv_sem=recv_sems.at[(N_DEV - 1) * a + k],
        device_id=peer, device_id_type=_MESH)
        for a, buf in enumerate(bufs) for k, peer in enumerate(peers)]


def _gather_slabs_start(bufs, after, name):
    n = len(bufs)
    n_copies = (N_DEV - 1) * n

    def body(*refs):
        send_sems, recv_sems, token = refs[n + 1], refs[n + 2], refs[-1]
        for cp in _slab_copies(refs[:n], send_sems, recv_sems):
            cp.start()
        token[...] = jnp.zeros_like(token)

    outs = pl.pallas_call(
        body, name=name,
        out_shape=(pltpu.SemaphoreType.DMA((n_copies,)), pltpu.SemaphoreType.DMA((n_copies,)),
                   *[pltpu.HBM(b.shape, b.dtype) for b in bufs], jax.ShapeDtypeStruct((SUBLANES, LANES), _F32)),
        in_specs=[_HBM] * n + [_ANY], out_specs=(_SEM, _SEM, *[_HBM] * n, _VMEM),
        input_output_aliases={i: 2 + i for i in range(n)},
        compiler_params=pltpu.CompilerParams(has_side_effects=_EFFECT),
    )(*[pltpu.with_memory_space_constraint(b, pltpu.HBM) for b in bufs], after)
    return outs[0], outs[1], list(outs[2:2 + n]), outs[-1]


def _gather_slabs_wait(send_sems, recv_sems, bufs, after, name):
    n = len(bufs)

    def body(*refs):
        for cp in _slab_copies(refs[:n], refs[n], refs[n + 1]):
            cp.wait_send()
            cp.wait_recv()

    outs = pl.pallas_call(
        body, name=name, out_shape=tuple(pltpu.HBM(b.shape, b.dtype) for b in bufs),
        in_specs=[_HBM] * n + [_SEM, _SEM, _ANY], out_specs=[_HBM] * n,
        input_output_aliases={i: i for i in range(n)},
        compiler_params=pltpu.CompilerParams(has_side_effects=_EFFECT),
    )(*bufs, send_sems, recv_sems, after)
    return list(outs)


def _chip_order(x, y):
    return [(x, y), (1 - x, y), (x, 1 - y), (1 - x, 1 - y)]


def _chip_copies(srcs, lands, send_sems, recv_sems):
    x, y, c = _position()
    return [pltpu.make_async_remote_copy(
        src_ref=srcs[a].at[k], dst_ref=lands[a].at[k],
        send_sem=send_sems.at[(N_CHIPS - 1) * a + k], recv_sem=recv_sems.at[(N_CHIPS - 1) * a + k],
        device_id=(*chip, c), device_id_type=_MESH)
        for a in range(len(srcs)) for k, chip in enumerate(_chip_order(x, y)[1:])]


def _sibling_copies(srcs, lands, send_sems, recv_sems):
    x, y, c = _position()
    return [pltpu.make_async_remote_copy(
        src_ref=srcs[a].at[s], dst_ref=lands[a].at[s],
        send_sem=send_sems.at[N_CHIPS * a + s], recv_sem=recv_sems.at[N_CHIPS * a + s],
        device_id=(x, y, 1 - c), device_id_type=_MESH)
        for a in range(len(srcs)) for s in range(N_CHIPS)]


def _sibling_copies_by_device(srcs, lands, send_sems, recv_sems):
    x, y, c = _position()
    return [pltpu.make_async_remote_copy(
        src_ref=srcs[a].at[_dev_index(*chip, 1 - c)], dst_ref=lands[a].at[s],
        send_sem=send_sems.at[N_CHIPS * a + s], recv_sem=recv_sems.at[N_CHIPS * a + s],
        device_id=(x, y, 1 - c), device_id_type=_MESH)
        for a in range(len(srcs)) for s, chip in enumerate(_chip_order(x, y))]


def _exchange_start(copies, sums, name, land_slabs=None):
    n = len(sums)
    land_shapes = [((land_slabs or s.shape[0]),) + s.shape[1:] for s in sums]
    n_copies = sum(shape[0] for shape in land_shapes)

    def body(*refs):
        srcs, lands = refs[:n], refs[n:2 * n]
        send_sems, recv_sems, token = refs[2 * n], refs[2 * n + 1], refs[-1]
        for cp in copies(srcs, lands, send_sems, recv_sems):
            cp.start()
        token[...] = jnp.zeros_like(token)

    hbm = [pltpu.HBM(s.shape, s.dtype) for s in sums] + [pltpu.HBM(shape, s.dtype) for shape, s in zip(land_shapes, sums)]
    outs = pl.pallas_call(
        body, name=name,
        out_shape=(pltpu.SemaphoreType.DMA((n_copies,)), pltpu.SemaphoreType.DMA((n_copies,)), *hbm,
                   jax.ShapeDtypeStruct((SUBLANES, LANES), _F32)),
        in_specs=[_HBM] * (2 * n), out_specs=(_SEM, _SEM, *[_HBM] * (2 * n), _VMEM),
        input_output_aliases={i: 2 + i for i in range(2 * n)},
        compiler_params=pltpu.CompilerParams(has_side_effects=_EFFECT),
    )(*[pltpu.with_memory_space_constraint(s, pltpu.HBM) for s in sums],
      *[pltpu.with_memory_space_constraint(lax.empty(shape, s.dtype), pltpu.HBM) for shape, s in zip(land_shapes, sums)])
    return outs[0], outs[1], list(outs[2:2 + n]), list(outs[2 + n:2 + 2 * n]), outs[-1]


def _exchange_wait(copies, send_sems, recv_sems, srcs, lands, after, name, with_sources=False):
    n = len(srcs)

    def body(*refs):
        for cp in copies(refs[:n], refs[n:2 * n], refs[2 * n], refs[2 * n + 1]):
            cp.wait_send()
            cp.wait_recv()

    hbm = [pltpu.HBM(s.shape, s.dtype) for s in (*srcs, *lands)]
    outs = pl.pallas_call(
        body, name=name, out_shape=tuple(hbm),
        in_specs=[_HBM] * (2 * n) + [_SEM, _SEM, _ANY], out_specs=[_HBM] * (2 * n),
        input_output_aliases={i: i for i in range(2 * n)},
        compiler_params=pltpu.CompilerParams(has_side_effects=_EFFECT),
    )(*srcs, *lands, send_sems, recv_sems, after)
    return list(outs) if with_sources else list(outs[n:])


def _matmul(a, b, *, ta=False, tb=False, out_dtype=_F32, tm=1024, tn=1024, tk=4608, after=None, name):
    kdim, m = a.shape if ta else a.shape[::-1]
    n = b.shape[0] if tb else b.shape[1]
    tm = _tile(m, tm, LANES if ta else BF16_ROWS)
    tn = _tile(n, tn, LANES)
    tk = _tile(kdim, tk, BF16_ROWS if ta else LANES)
    nk = kdim // tk
    dims = (((0 if ta else 1,), (1 if tb else 0,)), ((), ()))

    def body_whole_k(a_ref, b_ref, *rest):
        rest[-1][...] = lax.dot_general(a_ref[...], b_ref[...], dims, preferred_element_type=_F32).astype(out_dtype)

    def body_split_k(a_ref, b_ref, *rest):
        o_ref, acc_ref = rest[-2:]
        k = pl.program_id(2)

        @pl.when(k == 0)
        def _():
            acc_ref[...] = jnp.zeros_like(acc_ref)

        acc_ref[...] += lax.dot_general(a_ref[...], b_ref[...], dims, preferred_element_type=_F32)

        @pl.when(k == nk - 1)
        def _():
            o_ref[...] = acc_ref[...].astype(out_dtype)

    a_spec = pl.BlockSpec((tk, tm), lambda i, j, k: (k, i)) if ta else pl.BlockSpec((tm, tk), lambda i, j, k: (i, k))
    b_spec = pl.BlockSpec((tn, tk), lambda i, j, k: (j, k)) if tb else pl.BlockSpec((tk, tn), lambda i, j, k: (k, j))
    extra = [] if after is None else [after]
    return pl.pallas_call(
        body_whole_k if nk == 1 else body_split_k, name=name, grid=(m // tm, n // tn, nk),
        in_specs=[a_spec, b_spec] + [pl.BlockSpec(t.shape, lambda i, j, k: (0, 0)) for t in extra],
        out_specs=pl.BlockSpec((tm, tn), lambda i, j, k: (i, j)), out_shape=jax.ShapeDtypeStruct((m, n), out_dtype),
        scratch_shapes=[] if nk == 1 else [pltpu.VMEM((tm, tn), _F32)],
        compiler_params=pltpu.CompilerParams(dimension_semantics=("parallel", "parallel", "arbitrary")),
    )(a, b, *extra)


def _proj_blocks(a, wg, dst, order_ids, first, count, width, after, name):
    m, kdim = a.shape
    tm = _tile(m, 1088, BF16_ROWS)

    def body(ids_ref, a_ref, w_ref, after_ref, dst_ref, o_ref):
        del ids_ref, after_ref, dst_ref
        o_ref[...] = jnp.dot(a_ref[...], w_ref[...], preferred_element_type=_F32)

    return pl.pallas_call(
        body, name=name,
        grid_spec=pltpu.PrefetchScalarGridSpec(
            num_scalar_prefetch=1, grid=(count, m // tm),
            in_specs=[pl.BlockSpec((tm, kdim), lambda j, i, ids: (i, 0)),
                      pl.BlockSpec((kdim, width), lambda j, i, ids: (0, ids[first + j])),
                      pl.BlockSpec(after.shape, lambda j, i, ids: (0, 0)), _ANY],
            out_specs=pl.BlockSpec((tm, width), lambda j, i, ids: (i, ids[first + j]))),
        out_shape=jax.ShapeDtypeStruct(dst.shape, dst.dtype),
        input_output_aliases={4: 0},
        compiler_params=pltpu.CompilerParams(dimension_semantics=("arbitrary", "arbitrary")),
    )(order_ids, a, wg, after, dst)


def _cast_into_columns(slab_ids, a, n_blocks, name):
    r, c = a.shape
    tr = _row_tile(r, c)

    def body(ids_ref, a_ref, o_ref):
        del ids_ref
        o_ref[...] = a_ref[...].astype(_BF16)

    return pl.pallas_call(
        body, name=name,
        grid_spec=pltpu.PrefetchScalarGridSpec(
            num_scalar_prefetch=1, grid=(r // tr,),
            in_specs=[pl.BlockSpec((tr, c), lambda i, ids: (i, 0))],
            out_specs=pl.BlockSpec((tr, c), lambda i, ids: (i, ids[0]))),
        out_shape=jax.ShapeDtypeStruct((r, n_blocks * c), _BF16),
        compiler_params=pltpu.CompilerParams(dimension_semantics=("parallel",)),
    )(slab_ids, a)


def _matmul_slabs(a, b, ids, width, name, after=None):
    kdim, m = a.shape
    n_slabs = ids.shape[0]
    tm = _tile(m, 1024, LANES)

    def body(ids_ref, a_ref, b_ref, *rest):
        del ids_ref
        rest[-1][0] = lax.dot_general(a_ref[...], b_ref[...], (((0,), (0,)), ((), ())), preferred_element_type=_F32)

    extra = [] if after is None else [after]
    return pl.pallas_call(
        body, name=name,
        grid_spec=pltpu.PrefetchScalarGridSpec(
            num_scalar_prefetch=1, grid=(m // tm, n_slabs),
            in_specs=[pl.BlockSpec((kdim, tm), lambda i, j, ids: (0, i)),
                      pl.BlockSpec((kdim, width), lambda i, j, ids: (0, ids[j]))]
            + [pl.BlockSpec(t.shape, lambda i, j, ids: (0, 0)) for t in extra],
            out_specs=pl.BlockSpec((1, tm, width), lambda i, j, ids: (j, i, 0))),
        out_shape=jax.ShapeDtypeStruct((n_slabs, m, width), _F32),
        compiler_params=pltpu.CompilerParams(dimension_semantics=("parallel", "parallel")),
    )(ids, a, b, *extra)


def _row_tile(rows, cols):
    return _tile(rows, max(BF16_ROWS, min(512, (1 << 19) // cols)), BF16_ROWS)


def _cast_into_slab(slab_ids, a, name):
    r, c = a.shape
    tr = _row_tile(r, c)

    def body(ids_ref, a_ref, o_ref):
        del ids_ref
        o_ref[0] = a_ref[...].astype(_BF16)

    return pl.pallas_call(
        body, name=name,
        grid_spec=pltpu.PrefetchScalarGridSpec(
            num_scalar_prefetch=1, grid=(r // tr,),
            in_specs=[pl.BlockSpec((tr, c), lambda i, ids: (i, 0))],
            out_specs=pl.BlockSpec((1, tr, c), lambda i, ids: (ids[0], i, 0))),
        out_shape=jax.ShapeDtypeStruct((N_DEV, r, c), _BF16),
        compiler_params=pltpu.CompilerParams(dimension_semantics=("parallel",)),
    )(slab_ids, a)


def _pre_add(slab_ids, grad, got, name):
    _, r, c = grad.shape
    tr = _row_tile(r, c)

    def body(ids_ref, a_ref, b_ref, o_ref):
        del ids_ref
        o_ref[...] = (a_ref[...] + b_ref[...]).astype(_BF16)

    return pl.pallas_call(
        body, name=name,
        grid_spec=pltpu.PrefetchScalarGridSpec(
            num_scalar_prefetch=1, grid=(N_CHIPS - 1, r // tr),
            in_specs=[pl.BlockSpec((1, tr, c), lambda s, i, ids: (ids[s + 1], i, 0)),
                      pl.BlockSpec((1, tr, c), lambda s, i, ids: (s + 1, i, 0))],
            out_specs=pl.BlockSpec((1, tr, c), lambda s, i, ids: (s, i, 0))),
        out_shape=jax.ShapeDtypeStruct((N_CHIPS - 1, r, c), _BF16),
        compiler_params=pltpu.CompilerParams(dimension_semantics=("parallel", "parallel")),
    )(slab_ids, grad, got)


def _ada_forward(craw, w_shard, name):
    d, cols = w_shard.shape
    tk = _tile(d, 512, LANES)

    def body(c_ref, w_ref, o_ref):
        @pl.when(pl.program_id(0) == 0)
        def _():
            o_ref[...] = jnp.zeros_like(o_ref)

        o_ref[...] += jnp.dot(_silu(c_ref[...]).astype(_BF16), w_ref[...].astype(_BF16), preferred_element_type=_F32)

    return pl.pallas_call(
        body, name=name, grid=(d // tk,),
        in_specs=[pl.BlockSpec((craw.shape[0], tk), lambda k: (0, k)), pl.BlockSpec((tk, cols), lambda k: (k, 0))],
        out_specs=pl.BlockSpec((craw.shape[0], cols), lambda k: (0, 0)),
        out_shape=jax.ShapeDtypeStruct((craw.shape[0], cols), _F32),
        compiler_params=pltpu.CompilerParams(dimension_semantics=("arbitrary",)),
    )(craw, w_shard)


def _ada_backward(craw, dmod, w, m, v, name):
    d, cols = w.shape
    rows = craw.shape[0]
    tr = _tile(d, 256, LANES)

    def body(c_ref, dm_ref, w_ref, m_ref, v_ref, g_ref, dl_ref, nm_ref, nv_ref, dc_ref):
        act = _silu(c_ref[...]).astype(_BF16)
        dmb = dm_ref[...].astype(_BF16)
        wv = w_ref[...]
        g = lax.dot_general(act, dmb, (((0,), (0,)), ((), ())), preferred_element_type=_F32)
        delta, nm, nv = _adamw(wv, g, m_ref[...], v_ref[...])
        g_ref[...] = g
        dl_ref[...] = delta
        nm_ref[...] = nm
        nv_ref[...] = nv
        dc = lax.dot_general(dmb, wv.astype(_BF16), (((1,), (1,)), ((), ())), preferred_element_type=_F32)
        dc_ref[...] = jnp.broadcast_to(jnp.sum(dc[N_DEV:], axis=0, keepdims=True), dc_ref.shape)

    blk = pl.BlockSpec((tr, cols), lambda i: (i, 0))
    return pl.pallas_call(
        body, name=name, grid=(d // tr,),
        in_specs=[pl.BlockSpec((rows, tr), lambda i: (0, i)), pl.BlockSpec((rows, cols), lambda i: (0, 0)), blk, blk, blk],
        out_specs=[blk, blk, blk, blk, pl.BlockSpec((SUBLANES, tr), lambda i: (0, i))],
        out_shape=[jax.ShapeDtypeStruct((d, cols), _F32)] * 4 + [jax.ShapeDtypeStruct((SUBLANES, d), _F32)],
        compiler_params=pltpu.CompilerParams(dimension_semantics=("parallel",)),
    )(craw, dmod, w, m, v)


def _rms(xf):
    return lax.rsqrt(jnp.mean(xf * xf, axis=-1, keepdims=True) + EPS)


def _head_mean(v):
    hi = v.astype(_BF16)
    lo = (v - hi.astype(_F32)).astype(_BF16)
    ones = jnp.full((2 * HEAD_DIM, HEAD_DIM), 1.0 / HEAD_DIM, _BF16)
    return jnp.dot(jnp.concatenate([hi, lo], axis=1), ones, preferred_element_type=_F32)


def _prenorm(ctx, x, g_pre, mods, tr, name):
    l, d = ctx.shape
    n = x.shape[0]
    nbl = n // tr

    def body(ctx_ref, x_ref, g_ref, mod_ref, h_ref):
        def emit(src_ref):
            xf = src_ref[...]
            y = (xf * _rms(xf)) * g_ref[...]
            h_ref[...] = (y * (1.0 + mod_ref[0, 0:1, :]) + mod_ref[0, 1:2, :]).astype(_BF16)

        is_latent = pl.program_id(0) < nbl
        pl.when(is_latent)(lambda: emit(x_ref))
        pl.when(jnp.logical_not(is_latent))(lambda: emit(ctx_ref))

    return pl.pallas_call(
        body, name=name, grid=((l + n) // tr,),
        in_specs=[pl.BlockSpec((tr, d), lambda i: (jnp.maximum(i - nbl, 0), 0)),
                  pl.BlockSpec((tr, d), lambda i: (jnp.minimum(i, nbl - 1), 0)),
                  pl.BlockSpec((1, d), lambda i: (0, 0)),
                  pl.BlockSpec((1, SUBLANES, d), lambda i: ((i < nbl).astype(jnp.int32), 0, 0))],
        out_specs=pl.BlockSpec((tr, d), lambda i: (i, 0)),
        out_shape=jax.ShapeDtypeStruct((l + n, d), _BF16),
        compiler_params=pltpu.CompilerParams(dimension_semantics=("arbitrary",)),
    )(ctx, x, g_pre, mods)


def _prenorm_backward(dh, ctx, x, dxn, g_pre, mods, tr, name):
    l, d = ctx.shape
    n = x.shape[0]
    nbl = n // tr

    def body(dh_ref, ctx_ref, x_ref, dxn_ref, g_ref, mod_ref, gx_ref, dmod_ref, dg_ref):
        i = pl.program_id(0)

        @pl.when(i == 0)
        def _():
            dg_ref[...] = jnp.zeros_like(dg_ref)

        @pl.when(jnp.logical_or(i == 0, i == nbl))
        def _():
            dmod_ref[...] = jnp.zeros_like(dmod_ref)

        def emit(src_ref, latent):
            xf = src_ref[...]
            r = _rms(xf)
            xn = xf * r
            dhv = dh_ref[...]
            one_scale = 1.0 + mod_ref[0, 0:1, :]
            dmod_ref[0, 0:1, :] += jnp.sum(dhv * (xn * g_ref[...]), axis=0, keepdims=True)
            dmod_ref[0, 1:2, :] += jnp.sum(dhv, axis=0, keepdims=True)
            dyg = dhv * one_scale
            dg_ref[0:1, :] += jnp.sum(dyg * xn, axis=0, keepdims=True)
            if latent:
                dn = dyg * g_ref[...]
                gx_ref[...] = dxn_ref[...] + r * (dn - xn * jnp.mean(dn * xn, axis=-1, keepdims=True))

        pl.when(i < nbl)(lambda: emit(x_ref, True))
        pl.when(i >= nbl)(lambda: emit(ctx_ref, False))

    lat = pl.BlockSpec((tr, d), lambda i: (jnp.minimum(i, nbl - 1), 0))
    sel = pl.BlockSpec((1, SUBLANES, d), lambda i: ((i < nbl).astype(jnp.int32), 0, 0))
    return pl.pallas_call(
        body, name=name, grid=((l + n) // tr,),
        in_specs=[pl.BlockSpec((tr, d), lambda i: (i, 0)),
                  pl.BlockSpec((tr, d), lambda i: (jnp.maximum(i - nbl, 0), 0)),
                  lat, lat, pl.BlockSpec((1, d), lambda i: (0, 0)), sel],
        out_specs=[lat, sel, pl.BlockSpec((SUBLANES, d), lambda i: (0, 0))],
        out_shape=[jax.ShapeDtypeStruct((n, d), _F32), jax.ShapeDtypeStruct((2, SUBLANES, d), _F32),
                   jax.ShapeDtypeStruct((SUBLANES, d), _F32)],
        compiler_params=pltpu.CompilerParams(dimension_semantics=("arbitrary",)),
    )(dh, ctx, x, dxn, g_pre, mods)


def _rope_tables(l, n):
    rows = n // GRID_W
    inv = ROPE_THETA ** (-jnp.arange(ROPE_PAIRS, dtype=_F32) / ROPE_PAIRS)
    ang_r = jnp.arange(rows, dtype=_F32)[:, None] * inv
    ang_c = jnp.arange(GRID_W, dtype=_F32)[:, None] * inv
    per_row = lambda a: jnp.repeat(a, GRID_W, axis=0)
    per_col = lambda a: jnp.tile(a, (rows, 1))
    cr, sr, cc, sc = per_row(jnp.cos(ang_r)), per_row(jnp.sin(ang_r)), per_col(jnp.cos(ang_c)), per_col(jnp.sin(ang_c))
    zero = jnp.zeros_like(sr)
    tc = jnp.concatenate([cr, cr, cc, cc], axis=-1)
    ta = jnp.concatenate([-sr, zero, -sc, zero], axis=-1)
    tb = jnp.concatenate([zero, sr, zero, sc], axis=-1)
    pad = lambda t, fill: jnp.concatenate([t, jnp.full((l, HEAD_DIM), fill, _F32)], axis=0)
    return pad(tc, 1.0), pad(ta, 0.0), pad(tb, 0.0)


def _rope(y, tc, ta, tb):
    return y * tc + pltpu.roll(y, HEAD_DIM - ROPE_PAIRS, 1) * ta + pltpu.roll(y, ROPE_PAIRS, 1) * tb


def _rope_transposed(dy, tc, ta, tb):
    return dy * tc + pltpu.roll(dy * ta, ROPE_PAIRS, 1) + pltpu.roll(dy * tb, HEAD_DIM - ROPE_PAIRS, 1)


def _qkv_post(proj, tables, g_q, g_k, heads, kv_heads, tr, name):
    t = proj.shape[0]
    aw, kw = heads * HEAD_DIM, kv_heads * HEAD_DIM
    w3 = aw + 2 * kw

    def body(p_ref, tc_ref, ta_ref, tb_ref, gq_ref, gk_ref, q_ref, k_ref, v_ref):
        tabs = (tc_ref[...], ta_ref[...], tb_ref[...])

        def norm_rope(col, gain):
            xh = p_ref[:, col:col + HEAD_DIM]
            return _rope((xh * lax.rsqrt(_head_mean(xh * xh) + EPS)) * gain, *tabs).astype(_BF16)

        for h in range(heads):
            q_ref[h] = norm_rope(h * HEAD_DIM, gq_ref[...])
        for h in range(kv_heads):
            k_ref[h] = norm_rope(aw + h * HEAD_DIM, gk_ref[...])
            v_ref[h] = p_ref[:, aw + kw + h * HEAD_DIM:aw + kw + (h + 1) * HEAD_DIM].astype(_BF16)

    tab = pl.BlockSpec((tr, HEAD_DIM), lambda i: (i, 0))
    gain = pl.BlockSpec((1, HEAD_DIM), lambda i: (0, 0))
    return pl.pallas_call(
        body, name=name, grid=(t // tr,),
        in_specs=[pl.BlockSpec((tr, w3), lambda i: (i, 0)), tab, tab, tab, gain, gain],
        out_specs=[pl.BlockSpec((heads, tr, HEAD_DIM), lambda i: (0, i, 0)),
                   pl.BlockSpec((kv_heads, tr, HEAD_DIM), lambda i: (0, i, 0)),
                   pl.BlockSpec((kv_heads, tr, HEAD_DIM), lambda i: (0, i, 0))],
        out_shape=[jax.ShapeDtypeStruct((heads, t, HEAD_DIM), _BF16),
                   jax.ShapeDtypeStruct((kv_heads, t, HEAD_DIM), _BF16),
                   jax.ShapeDtypeStruct((kv_heads, t, HEAD_DIM), _BF16)],
        compiler_params=pltpu.CompilerParams(dimension_semantics=("parallel",)),
    )(proj, *tables, g_q, g_k)


def _qkv_post_backward(proj, dq, dk, dv, tables, g_q, g_k, dproj, l, tr, name):
    t = proj.shape[0]
    heads, kv_heads = dq.shape[0], dk.shape[0]
    aw, kw = heads * HEAD_DIM, kv_heads * HEAD_DIM
    w3 = aw + 2 * kw
    nbl = (t - l) // tr

    def body(p_ref, dq_ref, dk_ref, dv_ref, tc_ref, ta_ref, tb_ref, gq_ref, gk_ref, dproj_ref, o_ref, dgq_ref, dgk_ref):
        del dproj_ref
        i = pl.program_id(0)

        @pl.when(i == 0)
        def _():
            dgq_ref[...] = jnp.zeros_like(dgq_ref)
            dgk_ref[...] = jnp.zeros_like(dgk_ref)

        tabs = (tc_ref[...], ta_ref[...], tb_ref[...])
        latent = i < nbl

        def back(col, dout, gain, dg_ref):
            xh = p_ref[:, col:col + HEAD_DIM]
            r = lax.rsqrt(_head_mean(xh * xh) + EPS)
            xn = xh * r
            dy = _rope_transposed(dout, *tabs)
            dg_ref[0:1, :] += jnp.sum(dy * xn, axis=0, keepdims=True)
            dn = dy * gain
            o_ref[:, col:col + HEAD_DIM] = (r * (dn - xn * _head_mean(dn * xn))).astype(_BF16)

        for h in range(heads):
            back(h * HEAD_DIM, jnp.where(latent, dq_ref[h], 0.0), gq_ref[...], dgq_ref)
        for h in range(kv_heads):
            back(aw + h * HEAD_DIM, dk_ref[h], gk_ref[...], dgk_ref)
            o_ref[:, aw + kw + h * HEAD_DIM:aw + kw + (h + 1) * HEAD_DIM] = dv_ref[h].astype(_BF16)

    tab = pl.BlockSpec((tr, HEAD_DIM), lambda i: (i, 0))
    gain = pl.BlockSpec((1, HEAD_DIM), lambda i: (0, 0))
    acc = pl.BlockSpec((SUBLANES, HEAD_DIM), lambda i: (0, 0))
    return pl.pallas_call(
        body, name=name, grid=(t // tr,),
        in_specs=[pl.BlockSpec((tr, w3), lambda i: (i, 0)),
                  pl.BlockSpec((heads, tr, HEAD_DIM), lambda i: (0, jnp.minimum(i, nbl - 1), 0)),
                  pl.BlockSpec((kv_heads, tr, HEAD_DIM), lambda i: (0, i, 0)),
                  pl.BlockSpec((kv_heads, tr, HEAD_DIM), lambda i: (0, i, 0)),
                  tab, tab, tab, gain, gain, _ANY],
        out_specs=[pl.BlockSpec((tr, w3), lambda i: (i, 0)), acc, acc],
        out_shape=[jax.ShapeDtypeStruct(dproj.shape, dproj.dtype), jax.ShapeDtypeStruct((SUBLANES, HEAD_DIM), _F32),
                   jax.ShapeDtypeStruct((SUBLANES, HEAD_DIM), _F32)],
        input_output_aliases={9: 0},
        compiler_params=pltpu.CompilerParams(dimension_semantics=("arbitrary",)),
    )(proj, dq, dk, dv, *tables, g_q, g_k, dproj)


def _zero_context_rows(dproj, l, w3, tr, name):
    t, iw = dproj.shape
    first = (t - l) // tr

    def body(dproj_ref, o_ref):
        del dproj_ref
        o_ref[...] = jnp.zeros_like(o_ref)

    return pl.pallas_call(
        body, name=name, grid=(l // tr, iw // w3 - 1),
        in_specs=[_ANY], out_specs=pl.BlockSpec((tr, w3), lambda i, j: (first + i, j + 1)),
        out_shape=jax.ShapeDtypeStruct(dproj.shape, dproj.dtype), input_output_aliases={0: 0},
        compiler_params=pltpu.CompilerParams(dimension_semantics=("parallel", "parallel")),
    )(dproj)


def _attention(q, k, v, proj, l, mix, tq, name):
    heads, t, _ = q.shape
    kv_heads = k.shape[0]
    n = t - l
    gw = GQA_GROUP * HEAD_DIM
    aw = heads * HEAD_DIM
    gate_col = (aw + 2 * kv_heads * HEAD_DIM) // gw

    def body(q_ref, k_ref, v_ref, g_ref, o_ref, y_ref, lse_ref):
        lane = lax.broadcasted_iota(jnp.int32, (tq, LANES), 1)
        lse_blk = jnp.zeros((tq, LANES), _F32)
        firsts = list(range(0, GQA_GROUP, ATTN_SUB_HEADS))

        def scores(first):
            qs = q_ref[first:first + ATTN_SUB_HEADS].reshape(ATTN_SUB_HEADS * tq, HEAD_DIM)
            return lax.dot_general(qs, k_ref[0], (((1,), (1,)), ((), ())), preferred_element_type=_F32)

        raw_next = scores(firsts[0])
        for idx, first in enumerate(firsts):
            raw = raw_next
            if idx + 1 < len(firsts):
                raw_next = scores(firsts[idx + 1])
            m = jnp.max(raw, axis=-1, keepdims=True)
            p = jnp.exp2((raw - m) * (ATTN_SCALE * LOG2_E))
            denom = jnp.sum(p, axis=-1, keepdims=True)
            os_ = jnp.dot(p.astype(_BF16), v_ref[0], preferred_element_type=_F32) / denom
            lse_s = m * ATTN_SCALE + jnp.log(denom)
            for j in range(ATTN_SUB_HEADS):
                g = first + j
                og = os_[j * tq:(j + 1) * tq]
                cols = slice(g * HEAD_DIM, (g + 1) * HEAD_DIM)
                o_ref[:, cols] = og
                y_ref[:, cols] = (og * _silu(g_ref[:, cols])).astype(_BF16)
                lse_blk = jnp.where(lane == g, lse_s[j * tq:(j + 1) * tq], lse_blk)
        lse_ref[0] = lse_blk

    return pl.pallas_call(
        body, name=name, grid=(kv_heads, n // tq),
        in_specs=[pl.BlockSpec((GQA_GROUP, tq, HEAD_DIM), lambda h, i: (h, i, 0)),
                  pl.BlockSpec((1, t, HEAD_DIM), lambda h, i: (h, 0, 0)),
                  pl.BlockSpec((1, t, HEAD_DIM), lambda h, i: (h, 0, 0)),
                  pl.BlockSpec((tq, gw), lambda h, i: (i, gate_col + h))],
        out_specs=[pl.BlockSpec((tq, gw), lambda h, i: (i, h)),
                   pl.BlockSpec((tq, gw), lambda h, i: (i, h)),
                   pl.BlockSpec((1, tq, LANES), lambda h, i: (h, i, 0))],
        out_shape=[jax.ShapeDtypeStruct((n, aw), _F32), jax.ShapeDtypeStruct((n, mix), _BF16),
                   jax.ShapeDtypeStruct((kv_heads, n, LANES), _F32)],
        compiler_params=pltpu.CompilerParams(dimension_semantics=("parallel", "parallel")),
    )(q, k, v, proj)


def _attention_backward(q, k, v, attn_o, dy, proj, lse, after, dproj, l, tq, name):
    heads, t, _ = q.shape
    kv_heads = k.shape[0]
    n = t - l
    gw = GQA_GROUP * HEAD_DIM
    aw = heads * HEAD_DIM
    gate_col = (aw + 2 * kv_heads * HEAD_DIM) // gw
    n_parts = next(p for p in (ATTN_KEY_PARTS, 2, 1) if t % (p * BF16_ROWS) == 0)
    part = t // n_parts

    def body(q_ref, k_ref, v_ref, o_ref, dy_ref, g_ref, lse_ref, after_ref, dproj_ref, dq_ref, dg_ref, dk_ref, dv_ref):
        del after_ref, dproj_ref

        @pl.when(pl.program_id(1) == 0)
        def _():
            dk_ref[...] = jnp.zeros_like(dk_ref)
            dv_ref[...] = jnp.zeros_like(dv_ref)

        lse_blk = lse_ref[0]
        for first in range(0, GQA_GROUP, ATTN_BWD_SUB_HEADS):
            qs = q_ref[first:first + ATTN_BWD_SUB_HEADS].reshape(ATTN_BWD_SUB_HEADS * tq, HEAD_DIM)
            do_parts, delta_parts, lse_parts = [], [], []
            for g in range(first, first + ATTN_BWD_SUB_HEADS):
                cols = slice(g * HEAD_DIM, (g + 1) * HEAD_DIM)
                gate, og, dyg = g_ref[:, cols], o_ref[:, cols], dy_ref[:, cols]
                dog = dyg * _silu(gate)
                dg_ref[:, cols] = (dyg * og * _silu_grad(gate)).astype(_BF16)
                do_parts.append(dog)
                delta_parts.append(jnp.sum(dog * og, axis=-1, keepdims=True))
                lse_parts.append(lse_blk[:, g:g + 1])
            dos = jnp.concatenate(do_parts, axis=0).astype(_BF16)
            delta = jnp.concatenate(delta_parts, axis=0)
            lse2 = jnp.concatenate(lse_parts, axis=0) * LOG2_E
            dqs = jnp.zeros((ATTN_BWD_SUB_HEADS * tq, HEAD_DIM), _F32)
            for part_i in range(n_parts):
                keys = slice(part_i * part, (part_i + 1) * part)
                ks, vs = k_ref[0, keys, :], v_ref[0, keys, :]
                raw = lax.dot_general(qs, ks, (((1,), (1,)), ((), ())), preferred_element_type=_F32)
                p = jnp.exp2(raw * (ATTN_SCALE * LOG2_E) - lse2)
                dp = lax.dot_general(dos, vs, (((1,), (1,)), ((), ())), preferred_element_type=_F32)
                ds = (p * (dp - delta)).astype(_BF16)
                dqs = dqs + jnp.dot(ds, ks, preferred_element_type=_F32)
                dk_ref[0, keys, :] += ATTN_SCALE * lax.dot_general(
                    ds, qs, (((0,), (0,)), ((), ())), preferred_element_type=_F32)
                dv_ref[0, keys, :] += lax.dot_general(
                    p.astype(_BF16), dos, (((0,), (0,)), ((), ())), preferred_element_type=_F32)
            dq_ref[first:first + ATTN_BWD_SUB_HEADS] = (ATTN_SCALE * dqs).reshape(ATTN_BWD_SUB_HEADS, tq, HEAD_DIM)

    kv_spec = pl.BlockSpec((1, t, HEAD_DIM), lambda h, i: (h, 0, 0))
    tok = pl.BlockSpec((tq, gw), lambda h, i: (i, h))
    gate = pl.BlockSpec((tq, gw), lambda h, i: (i, gate_col + h))
    return pl.pallas_call(
        body, name=name, grid=(kv_heads, n // tq),
        in_specs=[pl.BlockSpec((GQA_GROUP, tq, HEAD_DIM), lambda h, i: (h, i, 0)), kv_spec, kv_spec,
                  tok, tok, gate, pl.BlockSpec((1, tq, LANES), lambda h, i: (h, i, 0)),
                  pl.BlockSpec(after.shape, lambda h, i: (0, 0)), _ANY],
        out_specs=[pl.BlockSpec((GQA_GROUP, tq, HEAD_DIM), lambda h, i: (h, i, 0)), gate, kv_spec, kv_spec],
        out_shape=[jax.ShapeDtypeStruct((heads, n, HEAD_DIM), _F32), jax.ShapeDtypeStruct(dproj.shape, dproj.dtype),
                   jax.ShapeDtypeStruct((kv_heads, t, HEAD_DIM), _F32), jax.ShapeDtypeStruct((kv_heads, t, HEAD_DIM), _F32)],
        input_output_aliases={8: 1},
        compiler_params=pltpu.CompilerParams(dimension_semantics=("parallel", "arbitrary")),
    )(q, k, v, attn_o, dy, proj, lse, after, dproj)


def _halo_specs(tp, width, col, row_off, total_rows):
    per = tp // POOL_HALO
    first = row_off // POOL_HALO
    last = total_rows // POOL_HALO - 1
    return [pl.BlockSpec((tp, width), lambda i: (i + row_off // tp, col)),
            pl.BlockSpec((POOL_HALO, width), lambda i: (jnp.maximum(first + i * per - 1, 0), col)),
            pl.BlockSpec((POOL_HALO, width), lambda i: (jnp.minimum(first + (i + 1) * per, last), col))]


def _with_halo(cur, prev, nxt, t0, n):
    tp = cur.shape[0]
    r8 = lax.broadcasted_iota(jnp.int32, (POOL_HALO, 1), 0)
    prev = jnp.where(t0 - POOL_HALO + r8 >= 0, prev, 0.0)
    nxt = jnp.where(t0 + tp + r8 < n, nxt, 0.0)
    return jnp.concatenate([prev, cur, nxt], axis=0)


def _shift_rows(a, s):
    return pltpu.roll(a, s % a.shape[0], 0)


def _window_sum(e, w, mirrored):
    a = e + _shift_rows(e, -1 if mirrored else 1)
    s = 1
    while 2 * s < w:
        a = _shift_rows(a, s) + _shift_rows(a, -s)
        s *= 2
    return a


def _window_count(t, w, n):
    half = w // 2
    return (jnp.minimum(t + half, n) - jnp.maximum(t - half, 0)).astype(_F32)


def _pool_forward(gi, proj, y, pool_w, pool_scale, l, heads, kv_heads, tp, name):
    t = proj.shape[0]
    n = t - l
    pg = pool_w.shape[-1]
    w = POOL_WINDOWS[gi]
    aw, kw = heads * HEAD_DIM, kv_heads * HEAD_DIM
    u_col = (2 * aw + 2 * kw) // pg + gi
    gate_col = (2 * aw + 2 * kw + len(POOL_WINDOWS) * pg) // pg + gi

    def body(u_ref, up_ref, un_ref, g_ref, w_ref, sc_ref, y_in_ref, y_ref, raw_ref, d_ref):
        del y_in_ref
        t0 = pl.program_id(0) * tp
        cur = u_ref[...]
        win = _window_sum(_with_halo(cur, up_ref[...], un_ref[...], t0, n), w, False)[POOL_HALO:POOL_HALO + tp]
        tok = t0 + lax.broadcasted_iota(jnp.int32, (tp, 1), 0)
        d = (win / _window_count(tok, w, n) - cur).astype(_BF16)
        raw = jnp.dot(d, w_ref[...].reshape(pg, pg), preferred_element_type=_F32)
        d_ref[...] = d
        raw_ref[...] = raw
        y_ref[...] = ((raw * sc_ref[...]) * _silu(g_ref[...])).astype(_BF16)

    blk = pl.BlockSpec((tp, pg), lambda i: (i, 0))
    return pl.pallas_call(
        body, name=name, grid=(n // tp,),
        in_specs=_halo_specs(tp, pg, u_col, 0, t) + [
            pl.BlockSpec((tp, pg), lambda i: (i, gate_col)),
            pl.BlockSpec((N_DEV, 1, pg // N_DEV, pg), lambda i: (0, gi, 0, 0)),
            pl.BlockSpec((1, pg), lambda i: (0, gi)), _ANY],
        out_specs=[pl.BlockSpec((tp, pg), lambda i: (i, aw // pg + gi)), blk, blk],
        out_shape=[jax.ShapeDtypeStruct(y.shape, y.dtype), jax.ShapeDtypeStruct((n, pg), _F32),
                   jax.ShapeDtypeStruct((n, pg), _BF16)],
        input_output_aliases={6: 0},
        compiler_params=pltpu.CompilerParams(dimension_semantics=("arbitrary",)),
    )(proj, proj, proj, proj, pool_w, pool_scale, y)


def _pool_backward_gate(gi, dy, proj, raw, pool_w, pool_scale, dproj, l, heads, kv_heads, tp, name):
    n, pg = raw.shape
    aw, kw = heads * HEAD_DIM, kv_heads * HEAD_DIM
    gate_col = (2 * aw + 2 * kw + len(POOL_WINDOWS) * pg) // pg + gi

    def body(dy_ref, g_ref, raw_ref, w_ref, sc_ref, dproj_ref, dg_ref, dr_ref, dd_ref, ds_ref):
        del dproj_ref

        @pl.when(pl.program_id(0) == 0)
        def _():
            ds_ref[...] = jnp.zeros_like(ds_ref)

        gate, rawv, dyv, scale = g_ref[...], raw_ref[...], dy_ref[...], sc_ref[...]
        dpool = dyv * _silu(gate)
        dg_ref[...] = (dyv * (rawv * scale) * _silu_grad(gate)).astype(_BF16)
        ds_ref[0:1, :] += jnp.sum(dpool * rawv, axis=0, keepdims=True)
        draw = (dpool * scale).astype(_BF16)
        dr_ref[...] = draw
        dd_ref[...] = lax.dot_general(
            draw, w_ref[...].reshape(pg, pg), (((1,), (1,)), ((), ())), preferred_element_type=_F32)

    blk = pl.BlockSpec((tp, pg), lambda i: (i, 0))
    gate = pl.BlockSpec((tp, pg), lambda i: (i, gate_col))
    return pl.pallas_call(
        body, name=name, grid=(n // tp,),
        in_specs=[pl.BlockSpec((tp, pg), lambda i: (i, aw // pg + gi)), gate, blk,
                  pl.BlockSpec((N_DEV, 1, pg // N_DEV, pg), lambda i: (0, gi, 0, 0)),
                  pl.BlockSpec((1, pg), lambda i: (0, gi)), _ANY],
        out_specs=[gate, blk, blk, pl.BlockSpec((SUBLANES, pg), lambda i: (0, 0))],
        out_shape=[jax.ShapeDtypeStruct(dproj.shape, dproj.dtype), jax.ShapeDtypeStruct((n, pg), _BF16),
                   jax.ShapeDtypeStruct((n, pg), _F32), jax.ShapeDtypeStruct((SUBLANES, pg), _F32)],
        input_output_aliases={5: 0},
        compiler_params=pltpu.CompilerParams(dimension_semantics=("arbitrary",)),
    )(dy, proj, raw, pool_w, pool_scale, dproj)


def _pool_backward_window(gi, dd, dproj, l, col, tp, name):
    n, pg = dd.shape
    w = POOL_WINDOWS[gi]

    def body(c_ref, p_ref, n_ref, dproj_ref, du_ref):
        del dproj_ref
        t0 = pl.program_id(0) * tp
        cur = c_ref[...]
        e = _with_halo(cur, p_ref[...], n_ref[...], t0, n)
        tok = t0 - POOL_HALO + lax.broadcasted_iota(jnp.int32, (tp + 2 * POOL_HALO, 1), 0)
        e = e / jnp.maximum(_window_count(tok, w, n), 1.0)
        du_ref[...] = (_window_sum(e, w, True)[POOL_HALO:POOL_HALO + tp] - cur).astype(_BF16)

    return pl.pallas_call(
        body, name=name, grid=(n // tp,),
        in_specs=_halo_specs(tp, pg, 0, 0, n) + [_ANY],
        out_specs=pl.BlockSpec((tp, pg), lambda i: (i, col)),
        out_shape=jax.ShapeDtypeStruct(dproj.shape, dproj.dtype), input_output_aliases={3: 0},
        compiler_params=pltpu.CompilerParams(dimension_semantics=("arbitrary",)),
    )(dd, dd, dd, dproj)


def _post(out, x, target, gate, g_post, tr, name):
    n, d = out.shape

    def body(o_ref, x_ref, t_ref, gate_ref, g_ref, dxn_ref, do_ref, dgate_ref, dg_ref, loss_ref):
        @pl.when(pl.program_id(0) == 0)
        def _():
            dgate_ref[...] = jnp.zeros_like(dgate_ref)
            dg_ref[...] = jnp.zeros_like(dg_ref)
            loss_ref[...] = jnp.zeros_like(loss_ref)

        ov = o_ref[...]
        r = _rms(ov)
        on = ov * r
        normed = on * g_ref[...]
        err = (x_ref[...] + gate_ref[...] * normed) - t_ref[...]
        loss_ref[...] += jnp.sum(err * err)
        dxn = err / d
        dxn_ref[...] = dxn
        dgate_ref[0:1, :] += jnp.sum(dxn * normed, axis=0, keepdims=True)
        dr = dxn * gate_ref[...]
        dg_ref[0:1, :] += jnp.sum(dr * on, axis=0, keepdims=True)
        dn = dr * g_ref[...]
        do_ref[...] = (r * (dn - on * jnp.mean(dn * on, axis=-1, keepdims=True))).astype(_BF16)

    blk = pl.BlockSpec((tr, d), lambda i: (i, 0))
    vec = pl.BlockSpec((1, d), lambda i: (0, 0))
    acc = pl.BlockSpec((SUBLANES, d), lambda i: (0, 0))
    return pl.pallas_call(
        body, name=name, grid=(n // tr,),
        in_specs=[blk, blk, blk, vec, vec],
        out_specs=[blk, blk, acc, acc, pl.BlockSpec((SUBLANES, LANES), lambda i: (0, 0))],
        out_shape=[jax.ShapeDtypeStruct((n, d), _F32), jax.ShapeDtypeStruct((n, d), _BF16),
                   jax.ShapeDtypeStruct((SUBLANES, d), _F32), jax.ShapeDtypeStruct((SUBLANES, d), _F32),
                   jax.ShapeDtypeStruct((SUBLANES, LANES), _F32)],
        compiler_params=pltpu.CompilerParams(dimension_semantics=("arbitrary",)),
    )(out, x, target, gate, g_post)


def _adam_sharded(slab_ids, grad, got, far, w, m, v, name):
    r, c = w.shape
    tr = _tile(r, max(BF16_ROWS, min(256, (1 << 18) // c)), BF16_ROWS)

    def body(ids_ref, own_ref, got_ref, far_ref, w_ref, m_ref, v_ref, g_ref, dl_ref, nm_ref, nv_ref):
        del ids_ref
        g = own_ref[0] + got_ref[0]
        for k in range(N_CHIPS - 1):
            g = g + far_ref[k].astype(_F32)
        delta, nm, nv = _adamw(w_ref[...], g, m_ref[...], v_ref[...])
        g_ref[...] = g
        dl_ref[...] = delta
        nm_ref[...] = nm
        nv_ref[...] = nv

    blk = pl.BlockSpec((tr, c), lambda i, ids: (i, 0))
    return pl.pallas_call(
        body, name=name,
        grid_spec=pltpu.PrefetchScalarGridSpec(
            num_scalar_prefetch=1, grid=(r // tr,),
            in_specs=[pl.BlockSpec((1, tr, c), lambda i, ids: (ids[0], i, 0)),
                      pl.BlockSpec((1, tr, c), lambda i, ids: (0, i, 0)),
                      pl.BlockSpec((N_CHIPS - 1, tr, c), lambda i, ids: (0, i, 0)), blk, blk, blk],
            out_specs=[blk] * 4),
        out_shape=[jax.ShapeDtypeStruct((r, c), _F32)] * 4,
        compiler_params=pltpu.CompilerParams(dimension_semantics=("parallel",)),
    )(slab_ids, grad, got, far, w, m, v)


def _adam_replicated(parts, extra, through_silu, w, m, v, name):
    def body(p_ref, e_ref, s_ref, w_ref, m_ref, v_ref, g_ref, dl_ref, nm_ref, nv_ref):
        total = p_ref[0] + e_ref[0]
        for dev in range(1, N_DEV):
            total = total + (p_ref[dev] + e_ref[dev])
        g = jnp.where(s_ref[...] > 0.5, total * _silu_grad(w_ref[...]), total)
        delta, nm, nv = _adamw(w_ref[...], g, m_ref[...], v_ref[...])
        g_ref[...] = g
        dl_ref[...] = delta
        nm_ref[...] = nm
        nv_ref[...] = nv

    return pl.pallas_call(
        body, name=name, in_specs=[_VMEM] * 6, out_specs=[_VMEM] * 4,
        out_shape=[jax.ShapeDtypeStruct(w.shape, _F32)] * 4,
    )(parts, extra, through_silu, w, m, v)


def _as_rows(vec):
    size = vec.shape[0]
    padded = -(-size // (SUBLANES * LANES)) * SUBLANES * LANES
    return jnp.pad(vec, (0, padded - size)).reshape(padded // LANES, LANES)


def kernel(x, c, ctx, c_ctx, w_ada, b_ada, norm_pre, norm_post, w_in, q_norm, k_norm, pool_w, pool_scale, w_out, loss_target, m_c_ctx, m_w_ada, m_b_ada, m_norm_pre, m_norm_post, m_w_in, m_q_norm, m_k_norm, m_pool_w, m_pool_scale, m_w_out, v_c_ctx, v_w_ada, v_b_ada, v_norm_pre, v_norm_post, v_w_in, v_q_norm, v_k_norm, v_pool_w, v_pool_scale, v_w_out):
    me = _dev_index(*_position())
    x2, ctx2, target = x[0], ctx[0], loss_target[0]
    n, d = x2.shape
    l = ctx2.shape[0]
    t = l + n
    aw = d // 2
    heads = aw // HEAD_DIM
    kv_heads = heads // GQA_GROUP
    kw = kv_heads * HEAD_DIM
    n_groups = len(POOL_WINDOWS)
    pg = (d - aw) // n_groups
    mix = d
    tr = _tile(l, 128, BF16_ROWS)
    tr2 = _tile(l, 256, BF16_ROWS)
    tq = _tile(l, 128, BF16_ROWS)
    tp = _tile(n, 1024, POOL_HALO)

    xi, yi, ci = _position()
    slab_ids = jnp.stack([_dev_index(*chip, ci) for chip in _chip_order(xi, yi)]).astype(jnp.int32)

    cw = w_in.shape[-1]
    wg = _cast_into_columns(slab_ids, w_in[0], N_DEV, "cast_w_in")
    late = [_cast_into_slab(slab_ids, w_out[0], "cast_w_out"),
            _cast_into_slab(slab_ids, pool_w[0].reshape(-1, pg), "cast_pool_w")]

    c_all = _all_gather_small(_as_rows(c[0]), "gather_c").reshape(N_DEV, -1)[:, :d]
    craw = jnp.concatenate([c_all, jnp.broadcast_to(c_ctx[None], (N_DEV, d))], axis=0)
    ada = _ada_forward(craw, w_ada[0], "ada_forward")
    ada_all = _all_gather_small(ada, "gather_ada")
    mod_all = ada_all.transpose(1, 0, 2).reshape(ada.shape[0], -1) + b_ada[0]
    mod = lax.dynamic_index_in_dim(mod_all, me, 0, keepdims=False)
    mod_c = mod_all[N_DEV]
    shift, scale, gate = mod[:d], mod[d:2 * d], mod[2 * d:]
    zeros6 = jnp.zeros((SUBLANES - 2, d), _F32)
    mods = jnp.stack([jnp.concatenate([mod_c[None, d:2 * d], mod_c[None, :d], zeros6], axis=0),
                      jnp.concatenate([scale[None], shift[None], zeros6], axis=0)])

    a_s, a_r, wg, tok = _w_in_hop(wg, cw, [], "a", ada_all, "gather_w_in_a")
    h_all = _prenorm(ctx2, x2, norm_pre, mods + tok[0, 0], tr2, "prenorm")
    order_ids = jnp.stack([_dev_index(*dev) for dev in _w_in_order(xi, yi, ci)]).astype(jnp.int32)
    proj = lax.empty((t, N_DEV * cw), _F32)
    proj = _proj_blocks(h_all, wg, proj, order_ids, 0, 1, cw, tok, "proj_0")
    b_s, b_r, wg, tok = _w_in_hop(wg, cw, [("a", a_s, a_r, [0, 1], [])], "b", proj, "gather_w_in_b")
    proj = _proj_blocks(h_all, wg, proj, order_ids, 1, 2, cw, tok, "proj_1")
    c_s, c_r, wg, tok = _w_in_hop(wg, cw, [("b", b_s, b_r, [2, 0], [])], "c", proj, "gather_w_in_c")
    proj = _proj_blocks(h_all, wg, proj, order_ids, 3, 2, cw, tok, "proj_2")
    d_s, d_r, wg, tok = _w_in_hop(
        wg, cw, [("c", c_s, c_r, [0], []), ("b", b_s, b_r, [1], [])], "d", proj, "gather_w_in_d")
    proj = _proj_blocks(h_all, wg, proj, order_ids, 5, 2, cw, tok, "proj_3")
    w_in_g, tok = _w_in_hop(
        wg, cw, [("d", d_s, d_r, [0], [0]), ("a", a_s, a_r, [], [0, 1]), ("b", b_s, b_r, [], [0, 1, 2]),
                 ("c", c_s, c_r, [], [0])], None, proj, "gather_w_in_end")
    flight_w = _gather_slabs_start(late, w_in_g, "gather_late_start")
    proj = _proj_blocks(h_all, w_in_g, proj, order_ids, 7, 1, cw, flight_w[-1], "proj_4")
    tables = _rope_tables(l, n)
    q, k, v = _qkv_post(proj, tables, q_norm, k_norm, heads, kv_heads, tr2, "qkv_post")
    attn_o, y, lse = _attention(q, k, v, proj, l, mix, _tile(l, 256, BF16_ROWS), "attention")
    w_out_g8, pool_g8 = _gather_slabs_wait(*flight_w[:3], attn_o, "gather_late_wait")
    w_out_g = w_out_g8.reshape(mix, d)
    pool_g = pool_g8.reshape(N_DEV, n_groups, pg // N_DEV, pg)
    raws, ds = [], []
    for gi in range(n_groups):
        y, raw, dsave = _pool_forward(gi, proj, y, pool_g, pool_scale, l, heads, kv_heads, tp, f"pool_forward_{gi}")
        raws.append(raw)
        ds.append(dsave)
    out = _matmul(y, w_out_g, name="out_proj")
    dxn, dout, dgate8, dgpost8, loss8 = _post(out, x2, target, gate[None], norm_post, tr2, "post")

    gw_out = _matmul(y, dout, ta=True, name="grad_w_out").reshape(N_DEV, mix // N_DEV, d)
    flight_so = _exchange_start(_sibling_copies_by_device, [gw_out], "exchange_sibling_start_w_out", land_slabs=N_CHIPS)
    dy = _matmul(dout, w_out_g, tb=True, after=flight_so[-1], name="d_y")
    gw_out, got_out = _exchange_wait(
        _sibling_copies_by_device, *flight_so[:4], dy, "exchange_sibling_wait_w_out", with_sources=True)
    sum_out = _pre_add(slab_ids, gw_out, got_out, "pre_add_w_out")
    flight_out = _exchange_start(_chip_copies, [sum_out], "exchange_chips_start_w_out")
    w3 = aw + 2 * kw
    dq, dproj, dk, dv = _attention_backward(
        q, k, v, attn_o, dy, proj, lse, flight_out[-1], lax.empty(proj.shape, _BF16), l, tq, "attention_backward")
    dproj, dgq8, dgk8 = _qkv_post_backward(proj, dq, dk, dv, tables, q_norm, k_norm, dproj, l, tr2, "qkv_post_backward")
    dproj = _zero_context_rows(dproj, l, w3, tr, "zero_context_rows")
    gpw, dps8 = [], []
    for gi in range(n_groups):
        dproj, draw, dd, dps = _pool_backward_gate(
            gi, dy, proj, raws[gi], pool_g, pool_scale, dproj, l, heads, kv_heads, tp, f"pool_backward_gate_{gi}")
        dproj = _pool_backward_window(gi, dd, dproj, l, (w3 + aw) // pg + gi, tp, f"pool_backward_window_{gi}")
        dps8.append(dps)
        gpw.append(_matmul(ds[gi], draw, ta=True, name=f"grad_pool_w_{gi}"))
    cw = w_in.shape[-1]
    other_ids = jnp.stack([_dev_index(*chip, 1 - ci) for chip in _chip_order(xi, yi)]).astype(jnp.int32)
    chip_slabs = jnp.arange(N_CHIPS, dtype=jnp.int32)
    pr = pool_w.shape[2]
    gpw8 = jnp.stack(gpw).reshape(n_groups, N_DEV, pr, pg).transpose(1, 0, 2, 3).reshape(N_DEV, n_groups * pr, pg)
    give_in = _matmul_slabs(h_all, dproj, other_ids, cw, "grad_w_in_sibling")
    flight_sib = _exchange_start(_sibling_copies, [give_in, jnp.take(gpw8, other_ids, axis=0)], "exchange_sibling_start")
    gw_in = _matmul_slabs(h_all, dproj, slab_ids, cw, "grad_w_in_own", after=flight_sib[-1])
    gpw_own = jnp.take(gpw8, slab_ids, axis=0)
    got_in, got_pw = _exchange_wait(_sibling_copies, *flight_sib[:4], gw_in, "exchange_sibling_wait")
    sums_in = [_pre_add(chip_slabs, gw_in, got_in, "pre_add_w_in"), _pre_add(chip_slabs, gpw_own, got_pw, "pre_add_pool_w")]
    flight_in = _exchange_start(_chip_copies, sums_in, "exchange_chips_start_w_in")
    dh = _matmul(dproj, w_in_g, tb=True, tm=1088, tk=3072, after=flight_in[-1], name="d_h")
    grad_x, dmods, dgpre8 = _prenorm_backward(dh, ctx2, x2, dxn, norm_pre, mods, tr, "prenorm_backward")

    dmod_lat = jnp.concatenate([dmods[1, 1], dmods[1, 0], dgate8[0]])
    dmod_ctx = jnp.concatenate([dmods[0, 1], dmods[0, 0], jnp.zeros((d,), _F32)])
    small = jnp.concatenate([dmod_lat, dmod_ctx, dgpre8[0], dgpost8[0], dgq8[0], dgk8[0]] + [p[0] for p in dps8]
                            + [loss8[0, :1]])
    small_rows = _as_rows(small)
    small_buf = lax.dynamic_update_slice(jnp.zeros((N_DEV,) + small_rows.shape, _F32), small_rows[None], (me, 0, 0))
    flight_small = _gather_slabs_start([small_buf], grad_x, "gather_small_start")

    far_out = _exchange_wait(_chip_copies, *flight_out[:4], flight_small[-1], "exchange_chips_wait_w_out")[0]
    far_in, far_pw = _exchange_wait(_chip_copies, *flight_in[:4], far_out, "exchange_chips_wait_w_in")
    two = lambda a: a.reshape(-1, a.shape[-1])
    sharded = []
    for ids, g, got, far, w, m, v_, name in zip(
            (chip_slabs, slab_ids, chip_slabs), (gw_in, gw_out, gpw_own), (got_in, got_out, got_pw),
            (far_in, far_out, far_pw), (w_in, w_out, pool_w), (m_w_in, m_w_out, m_pool_w),
            (v_w_in, v_w_out, v_pool_w), ("adam_w_in", "adam_w_out", "adam_pool_w")):
        res = _adam_sharded(ids, g, got, far, two(w), two(m), two(v_), name)
        sharded.append([r.reshape(w.shape) for r in res])
    (g_w_in, dl_w_in, nm_w_in, nv_w_in), (g_w_out, dl_w_out, nm_w_out, nv_w_out), (g_pw, dl_pw, nm_pw, nv_pw) = sharded

    gathered = _gather_slabs_wait(*flight_small[:3], nv_pw, "gather_small_wait")[0].reshape(N_DEV, -1)
    o = 0
    take = lambda size: (gathered[:, o:o + size], o + size)
    g_mod, o = take(3 * d)
    g_modc, o = take(3 * d)
    g_pre, o = take(d)
    g_post, o = take(d)
    g_q, o = take(HEAD_DIM)
    g_k, o = take(HEAD_DIM)
    g_ps, o = take(n_groups * pg)
    g_loss, o = take(1)
    cols = w_ada.shape[-1]
    mine = lambda a: lax.dynamic_slice_in_dim(a, me * cols, cols, axis=1)
    dmod_rows = jnp.concatenate([mine(g_mod), mine(g_modc)], axis=0)
    g_wada, dl_wada, nm_wada, nv_wada, dcact = _ada_backward(craw, dmod_rows, w_ada[0], m_w_ada[0], v_w_ada[0], "ada_backward")
    dcc = _all_gather_small(_as_rows(dcact[0]), "gather_dcc").reshape(N_DEV, -1)[:, :d]

    sizes = [d, 3 * d, d, d, HEAD_DIM, HEAD_DIM, n_groups * pg]
    def pack(parts):
        rows = jnp.concatenate(parts, axis=1)
        padded = -(-rows.shape[1] // (SUBLANES * LANES)) * SUBLANES * LANES
        return jnp.pad(rows, ((0, 0), (0, padded - rows.shape[1]))).reshape(N_DEV, padded // LANES, LANES)

    zero = lambda size: jnp.zeros((N_DEV, size), _F32)
    parts = pack([dcc, g_mod, g_pre, g_post, g_q, g_k, g_ps])
    extra = pack([zero(d), g_modc, zero(d), zero(d), zero(HEAD_DIM), zero(HEAD_DIM), zero(n_groups * pg)])
    through_silu = _as_rows(jnp.concatenate([jnp.ones((d,), _F32), jnp.zeros((sum(sizes[1:]),), _F32)]))
    cat = lambda items: _as_rows(jnp.concatenate([a.reshape(-1) for a in items]))
    ws = [c_ctx, b_ada, norm_pre, norm_post, q_norm, k_norm, pool_scale]
    ms = [m_c_ctx, m_b_ada, m_norm_pre, m_norm_post, m_q_norm, m_k_norm, m_pool_scale]
    vs = [v_c_ctx, v_b_ada, v_norm_pre, v_norm_post, v_q_norm, v_k_norm, v_pool_scale]
    rep = _adam_replicated(parts, extra, through_silu, cat(ws), cat(ms), cat(vs), "adam_replicated")

    def split(packed):
        flat_, outs, at = packed.reshape(-1), [], 0
        for w, size in zip(ws, sizes):
            outs.append(flat_[at:at + size].reshape(w.shape))
            at += size
        return outs

    g_rep, dl_rep, nm_rep, nv_rep = [split(r) for r in rep]

    loss_sum = g_loss[0, 0]
    for dev in range(1, N_DEV):
        loss_sum = loss_sum + g_loss[dev, 0]
    loss = (0.5 / d) * loss_sum

    def ordered(rep_list, ada_, w_in_, pw_, w_out_):
        return [rep_list[0], ada_[None], rep_list[1], rep_list[2], rep_list[3], w_in_, rep_list[4], rep_list[5],
                pw_, rep_list[6], w_out_]

    return (loss, grad_x[None],
            *ordered(g_rep, g_wada, g_w_in, g_pw, g_w_out),
            *ordered(dl_rep, dl_wada, dl_w_in, dl_pw, dl_w_out),
            *ordered(nm_rep, nm_wada, nm_w_in, nm_pw, nm_w_out),
            *ordered(nv_rep, nv_wada, nv_w_in, nv_pw, nv_w_out))
```

```python
import jax
import jax.numpy as jnp
from jax import lax
from jax.experimental import pallas as pl
from jax.experimental.pallas import tpu as pltpu

HEAD_DIM = 128
GQA_GROUP = 4
ATTN_SUB_HEADS = 1
ATTN_BWD_SUB_HEADS = 1
ATTN_KEY_PARTS = 2
LOG2_E = 1.4426950408889634
GRID_W = 64
ROPE_PAIRS = HEAD_DIM // 4
ROPE_THETA = 10000.0
ATTN_SCALE = HEAD_DIM ** -0.5
EPS = 1e-6
POOL_WINDOWS = (2, 4, 8, 16)
POOL_HALO = 8
N_DEV = 8
N_CHIPS = 4
ADAM_LR = 0.001
ADAM_B1 = 0.9
ADAM_B2 = 0.999
ADAM_EPS = 1e-08
ADAM_WD = 0.01
ADAM_STEP = 10

LANES = 128
SUBLANES = 8
BF16_ROWS = 16

_MESH = pl.DeviceIdType.MESH
_ANY = pl.BlockSpec(memory_space=pl.ANY)
_VMEM = pl.BlockSpec(memory_space=pltpu.VMEM)
_HBM = pl.BlockSpec(memory_space=pltpu.HBM)
_SEM = pl.BlockSpec(memory_space=pltpu.SEMAPHORE)
_EFFECT = pltpu.SideEffectType.DATAFLOW_SIDE_EFFECTING
_F32 = jnp.float32
_BF16 = jnp.bfloat16


def _tile(dim, pref, align):
    t = min(pref, dim)
    t -= t % align
    while t >= align:
        if dim % t == 0:
            return t
        t -= align
    return dim


def _position():
    return lax.axis_index("x"), lax.axis_index("y"), lax.axis_index("c")


def _flip(v, bit):
    return 1 - v if bit else v


def _dev_index(x, y, c):
    return 4 * x + 2 * y + c


def _silu(g):
    return g * jax.nn.sigmoid(g)


def _silu_grad(g):
    s = jax.nn.sigmoid(g)
    return s * (1.0 + g * (1.0 - s))


def _adamw(w, g, m, v):
    m = ADAM_B1 * m + (1.0 - ADAM_B1) * g
    v = ADAM_B2 * v + (1.0 - ADAM_B2) * (g * g)
    m_hat = m / (1.0 - ADAM_B1 ** ADAM_STEP)
    v_hat = v / (1.0 - ADAM_B2 ** ADAM_STEP)
    delta = -ADAM_LR * (m_hat / (jnp.sqrt(v_hat) + ADAM_EPS) + ADAM_WD * w)
    return delta, m, v


def _all_gather_small(v, name):
    rows, cols = v.shape

    def body(v_ref, out_ref, send_sems, recv_sems):
        x, y, c = _position()
        me = _dev_index(x, y, c)
        out_ref[me] = v_ref[...]
        peers = [(_flip(x, k & 4), _flip(y, k & 2), _flip(c, k & 1)) for k in range(1, N_DEV)]

        def copy(k, block, to):
            return pltpu.make_async_remote_copy(
                src_ref=v_ref, dst_ref=out_ref.at[block], send_sem=send_sems.at[k], recv_sem=recv_sems.at[k],
                device_id=to, device_id_type=_MESH)

        sends = [copy(k, me, p) for k, p in enumerate(peers)]
        for s in sends:
            s.start()
        for k, p in enumerate(peers):
            copy(k, _dev_index(*p), p).wait_recv()
        for s in sends:
            s.wait_send()

    return pl.pallas_call(
        body, name=name,
        out_shape=jax.ShapeDtypeStruct((N_DEV, rows, cols), v.dtype),
        in_specs=[_VMEM], out_specs=_VMEM,
        scratch_shapes=[pltpu.SemaphoreType.DMA((N_DEV - 1,)), pltpu.SemaphoreType.DMA((N_DEV - 1,))],
    )(v)


def _route(x, y, c):
    first = (x + (1 - c) * (1 - 2 * x), y + c * (1 - 2 * y))
    second = (x + c * (1 - 2 * x), y + (1 - c) * (1 - 2 * y))
    return first, second, (1 - x, 1 - y)


def _w_in_order(x, y, c):
    first, second, diagonal = _route(x, y, c)
    return [(x, y, c), (x, y, 1 - c), (*first, c), (*second, 1 - c), (*second, c), (*first, 1 - c),
            (*diagonal, c), (*diagonal, 1 - c)]


W_IN_HOPS = {"a": 2, "b": 3, "c": 1, "d": 1}


def _w_in_hop_copies(group, wg, width, send_sems, recv_sems):
    x, y, c = _position()
    me, sibling = (x, y, c), (x, y, 1 - c)
    first, second, diagonal = _route(x, y, c)

    def cp(k, block, to):
        cols = wg.at[:, pl.ds(pl.multiple_of(_dev_index(*block) * width, width), width)]
        return pltpu.make_async_remote_copy(
            src_ref=cols, dst_ref=cols, send_sem=send_sems.at[k], recv_sem=recv_sems.at[k],
            device_id=to, device_id_type=_MESH)

    if group == "a":
        return [cp(0, me, sibling), cp(1, me, (*first, c))]
    if group == "b":
        return [cp(0, me, (*second, c)), cp(1, (*first, c), (*second, c)), cp(2, (*first, c), sibling)]
    if group == "c":
        return [cp(0, (*second, c), sibling)]
    return [cp(0, (*diagonal, c), sibling)]


def _w_in_hop(wg, width, waits, start, after, name):
    n_sem = 2 * len(waits)

    def body(*refs):
        wg_ref = refs[0]
        for i, (group, _, _, arrivals, sends) in enumerate(waits):
            cps = _w_in_hop_copies(group, wg_ref, width, refs[1 + 2 * i], refs[2 + 2 * i])
            for k in arrivals:
                cps[k].wait_recv()
            for k in sends:
                cps[k].wait_send()
        if start:
            for cp in _w_in_hop_copies(start, wg_ref, width, refs[n_sem + 2], refs[n_sem + 3]):
                cp.start()
        refs[-1][...] = jnp.zeros_like(refs[-1])

    sems = [s for w in waits for s in w[1:3]]
    new = [pltpu.SemaphoreType.DMA((W_IN_HOPS[start],))] * 2 if start else []
    outs = pl.pallas_call(
        body, name=name,
        out_shape=(*new, pltpu.HBM(wg.shape, wg.dtype), jax.ShapeDtypeStruct((SUBLANES, LANES), _F32)),
        in_specs=[_HBM] + [_SEM] * n_sem + [_ANY], out_specs=(*[_SEM] * len(new), _HBM, _VMEM),
        input_output_aliases={0: len(new)},
        compiler_params=pltpu.CompilerParams(has_side_effects=_EFFECT),
    )(pltpu.with_memory_space_constraint(wg, pltpu.HBM), *sems, after)
    return outs


def _slab_copies(bufs, send_sems, recv_sems):
    x, y, c = _position()
    me = _dev_index(x, y, c)
    peers = [(_flip(x, k & 4), _flip(y, k & 2), _flip(c, k & 1)) for k in range(1, N_DEV)]
    return [pltpu.make_async_remote_copy(
        src_ref=buf.at[me], dst_ref=buf.at[me],
        send_sem=send_sems.at[(N_DEV - 1) * a + k], recv_sem=recv_sems.at[(N_DEV - 1) * a + k],
        device_id=peer, device_id_type=_MESH)
        for a, buf in enumerate(bufs) for k, peer in enumerate(peers)]


def _gather_slabs_start(bufs, after, name):
    n = len(bufs)
    n_copies = (N_DEV - 1) * n

    def body(*refs):
        send_sems, recv_sems, token = refs[n + 1], refs[n + 2], refs[-1]
        for cp in _slab_copies(refs[:n], send_sems, recv_sems):
            cp.start()
        token[...] = jnp.zeros_like(token)

    outs = pl.pallas_call(
        body, name=name,
        out_shape=(pltpu.SemaphoreType.DMA((n_copies,)), pltpu.SemaphoreType.DMA((n_copies,)),
                   *[pltpu.HBM(b.shape, b.dtype) for b in bufs], jax.ShapeDtypeStruct((SUBLANES, LANES), _F32)),
        in_specs=[_HBM] * n + [_ANY], out_specs=(_SEM, _SEM, *[_HBM] * n, _VMEM),
        input_output_aliases={i: 2 + i for i in range(n)},
        compiler_params=pltpu.CompilerParams(has_side_effects=_EFFECT),
    )(*[pltpu.with_memory_space_constraint(b, pltpu.HBM) for b in bufs], after)
    return outs[0], outs[1], list(outs[2:2 + n]), outs[-1]


def _gather_slabs_wait(send_sems, recv_sems, bufs, after, name):
    n = len(bufs)

    def body(*refs):
        for cp in _slab_copies(refs[:n], refs[n], refs[n + 1]):
            cp.wait_send()
            cp.wait_recv()

    outs = pl.pallas_call(
        body, name=name, out_shape=tuple(pltpu.HBM(b.shape, b.dtype) for b in bufs),
        in_specs=[_HBM] * n + [_SEM, _SEM, _ANY], out_specs=[_HBM] * n,
        input_output_aliases={i: i for i in range(n)},
        compiler_params=pltpu.CompilerParams(has_side_effects=_EFFECT),
    )(*bufs, send_sems, recv_sems, after)
    return list(outs)


def _chip_order(x, y):
    return [(x, y), (1 - x, y), (x, 1 - y), (1 - x, 1 - y)]


def _chip_copies(srcs, lands, send_sems, recv_sems):
    x, y, c = _position()
    return [pltpu.make_async_remote_copy(
        src_ref=srcs[a].at[k], dst_ref=lands[a].at[k],
        send_sem=send_sems.at[(N_CHIPS - 1) * a + k], recv_sem=recv_sems.at[(N_CHIPS - 1) * a + k],
        device_id=(*chip, c), device_id_type=_MESH)
        for a in range(len(srcs)) for k, chip in enumerate(_chip_order(x, y)[1:])]


def _sibling_copies(srcs, lands, send_sems, recv_sems):
    x, y, c = _position()
    return [pltpu.make_async_remote_copy(
        src_ref=srcs[a].at[s], dst_ref=lands[a].at[s],
        send_sem=send_sems.at[N_CHIPS * a + s], recv_sem=recv_sems.at[N_CHIPS * a + s],
        device_id=(x, y, 1 - c), device_id_type=_MESH)
        for a in range(len(srcs)) for s in range(N_CHIPS)]


def _sibling_copies_by_device(srcs, lands, send_sems, recv_sems):
    x, y, c = _position()
    return [pltpu.make_async_remote_copy(
        src_ref=srcs[a].at[_dev_index(*chip, 1 - c)], dst_ref=lands[a].at[s],
        send_sem=send_sems.at[N_CHIPS * a + s], recv_sem=recv_sems.at[N_CHIPS * a + s],
        device_id=(x, y, 1 - c), device_id_type=_MESH)
        for a in range(len(srcs)) for s, chip in enumerate(_chip_order(x, y))]


def _exchange_start(copies, sums, name, land_slabs=None):
    n = len(sums)
    land_shapes = [((land_slabs or s.shape[0]),) + s.shape[1:] for s in sums]
    n_copies = sum(shape[0] for shape in land_shapes)

    def body(*refs):
        srcs, lands = refs[:n], refs[n:2 * n]
        send_sems, recv_sems, token = refs[2 * n], refs[2 * n + 1], refs[-1]
        for cp in copies(srcs, lands, send_sems, recv_sems):
            cp.start()
        token[...] = jnp.zeros_like(token)

    hbm = [pltpu.HBM(s.shape, s.dtype) for s in sums] + [pltpu.HBM(shape, s.dtype) for shape, s in zip(land_shapes, sums)]
    outs = pl.pallas_call(
        body, name=name,
        out_shape=(pltpu.SemaphoreType.DMA((n_copies,)), pltpu.SemaphoreType.DMA((n_copies,)), *hbm,
                   jax.ShapeDtypeStruct((SUBLANES, LANES), _F32)),
        in_specs=[_HBM] * (2 * n), out_specs=(_SEM, _SEM, *[_HBM] * (2 * n), _VMEM),
        input_output_aliases={i: 2 + i for i in range(2 * n)},
        compiler_params=pltpu.CompilerParams(has_side_effects=_EFFECT),
    )(*[pltpu.with_memory_space_constraint(s, pltpu.HBM) for s in sums],
      *[pltpu.with_memory_space_constraint(lax.empty(shape, s.dtype), pltpu.HBM) for shape, s in zip(land_shapes, sums)])
    return outs[0], outs[1], list(outs[2:2 + n]), list(outs[2 + n:2 + 2 * n]), outs[-1]


def _exchange_wait(copies, send_sems, recv_sems, srcs, lands, after, name, with_sources=False):
    n = len(srcs)

    def body(*refs):
        for cp in copies(refs[:n], refs[n:2 * n], refs[2 * n], refs[2 * n + 1]):
            cp.wait_send()
            cp.wait_recv()

    hbm = [pltpu.HBM(s.shape, s.dtype) for s in (*srcs, *lands)]
    outs = pl.pallas_call(
        body, name=name, out_shape=tuple(hbm),
        in_specs=[_HBM] * (2 * n) + [_SEM, _SEM, _ANY], out_specs=[_HBM] * (2 * n),
        input_output_aliases={i: i for i in range(2 * n)},
        compiler_params=pltpu.CompilerParams(has_side_effects=_EFFECT),
    )(*srcs, *lands, send_sems, recv_sems, after)
    return list(outs) if with_sources else list(outs[n:])


def _matmul(a, b, *, ta=False, tb=False, out_dtype=_F32, tm=1024, tn=1024, tk=4608, after=None, name):
    kdim, m = a.shape if ta else a.shape[::-1]
    n = b.shape[0] if tb else b.shape[1]
    tm = _tile(m, tm, LANES if ta else BF16_ROWS)
    tn = _tile(n, tn, LANES)
    tk = _tile(kdim, tk, BF16_ROWS if ta else LANES)
    nk = kdim // tk
    dims = (((0 if ta else 1,), (1 if tb else 0,)), ((), ()))

    def body_whole_k(a_ref, b_ref, *rest):
        rest[-1][...] = lax.dot_general(a_ref[...], b_ref[...], dims, preferred_element_type=_F32).astype(out_dtype)

    def body_split_k(a_ref, b_ref, *rest):
        o_ref, acc_ref = rest[-2:]
        k = pl.program_id(2)

        @pl.when(k == 0)
        def _():
            acc_ref[...] = jnp.zeros_like(acc_ref)

        acc_ref[...] += lax.dot_general(a_ref[...], b_ref[...], dims, preferred_element_type=_F32)

        @pl.when(k == nk - 1)
        def _():
            o_ref[...] = acc_ref[...].astype(out_dtype)

    a_spec = pl.BlockSpec((tk, tm), lambda i, j, k: (k, i)) if ta else pl.BlockSpec((tm, tk), lambda i, j, k: (i, k))
    b_spec = pl.BlockSpec((tn, tk), lambda i, j, k: (j, k)) if tb else pl.BlockSpec((tk, tn), lambda i, j, k: (k, j))
    extra = [] if after is None else [after]
    return pl.pallas_call(
        body_whole_k if nk == 1 else body_split_k, name=name, grid=(m // tm, n // tn, nk),
        in_specs=[a_spec, b_spec] + [pl.BlockSpec(t.shape, lambda i, j, k: (0, 0)) for t in extra],
        out_specs=pl.BlockSpec((tm, tn), lambda i, j, k: (i, j)), out_shape=jax.ShapeDtypeStruct((m, n), out_dtype),
        scratch_shapes=[] if nk == 1 else [pltpu.VMEM((tm, tn), _F32)],
        compiler_params=pltpu.CompilerParams(dimension_semantics=("parallel", "parallel", "arbitrary")),
    )(a, b, *extra)


def _proj_blocks(a, wg, dst, order_ids, first, count, width, after, name):
    m, kdim = a.shape
    tm = _tile(m, 1088, BF16_ROWS)

    def body(ids_ref, a_ref, w_ref, after_ref, dst_ref, o_ref):
        del ids_ref, after_ref, dst_ref
        o_ref[...] = jnp.dot(a_ref[...], w_ref[...], preferred_element_type=_F32)

    return pl.pallas_call(
        body, name=name,
        grid_spec=pltpu.PrefetchScalarGridSpec(
            num_scalar_prefetch=1, grid=(count, m // tm),
            in_specs=[pl.BlockSpec((tm, kdim), lambda j, i, ids: (i, 0)),
                      pl.BlockSpec((kdim, width), lambda j, i, ids: (0, ids[first + j])),
                      pl.BlockSpec(after.shape, lambda j, i, ids: (0, 0)), _ANY],
            out_specs=pl.BlockSpec((tm, width), lambda j, i, ids: (i, ids[first + j]))),
        out_shape=jax.ShapeDtypeStruct(dst.shape, dst.dtype),
        input_output_aliases={4: 0},
        compiler_params=pltpu.CompilerParams(dimension_semantics=("arbitrary", "arbitrary")),
    )(order_ids, a, wg, after, dst)


def _cast_into_columns(slab_ids, a, n_blocks, name):
    r, c = a.shape
    tr = _row_tile(r, c)

    def body(ids_ref, a_ref, o_ref):
        del ids_ref
        o_ref[...] = a_ref[...].astype(_BF16)

    return pl.pallas_call(
        body, name=name,
        grid_spec=pltpu.PrefetchScalarGridSpec(
            num_scalar_prefetch=1, grid=(r // tr,),
            in_specs=[pl.BlockSpec((tr, c), lambda i, ids: (i, 0))],
            out_specs=pl.BlockSpec((tr, c), lambda i, ids: (i, ids[0]))),
        out_shape=jax.ShapeDtypeStruct((r, n_blocks * c), _BF16),
        compiler_params=pltpu.CompilerParams(dimension_semantics=("parallel",)),
    )(slab_ids, a)


def _matmul_slabs(a, b, ids, width, name, after=None):
    kdim, m = a.shape
    n_slabs = ids.shape[0]
    tm = _tile(m, 1024, LANES)

    def body(ids_ref, a_ref, b_ref, *rest):
        del ids_ref
        rest[-1][0] = lax.dot_general(a_ref[...], b_ref[...], (((0,), (0,)), ((), ())), preferred_element_type=_F32)

    extra = [] if after is None else [after]
    return pl.pallas_call(
        body, name=name,
        grid_spec=pltpu.PrefetchScalarGridSpec(
            num_scalar_prefetch=1, grid=(m // tm, n_slabs),
            in_specs=[pl.BlockSpec((kdim, tm), lambda i, j, ids: (0, i)),
                      pl.BlockSpec((kdim, width), lambda i, j, ids: (0, ids[j]))]
            + [pl.BlockSpec(t.shape, lambda i, j, ids: (0, 0)) for t in extra],
            out_specs=pl.BlockSpec((1, tm, width), lambda i, j, ids: (j, i, 0))),
        out_shape=jax.ShapeDtypeStruct((n_slabs, m, width), _F32),
        compiler_params=pltpu.CompilerParams(dimension_semantics=("parallel", "parallel")),
    )(ids, a, b, *extra)


def _row_tile(rows, cols):
    return _tile(rows, max(BF16_ROWS, min(512, (1 << 19) // cols)), BF16_ROWS)


def _cast_into_slab(slab_ids, a, name):
    r, c = a.shape
    tr = _row_tile(r, c)

    def body(ids_ref, a_ref, o_ref):
        del ids_ref
        o_ref[0] = a_ref[...].astype(_BF16)

    return pl.pallas_call(
        body, name=name,
        grid_spec=pltpu.PrefetchScalarGridSpec(
            num_scalar_prefetch=1, grid=(r // tr,),
            in_specs=[pl.BlockSpec((tr, c), lambda i, ids: (i, 0))],
            out_specs=pl.BlockSpec((1, tr, c), lambda i, ids: (ids[0], i, 0))),
        out_shape=jax.ShapeDtypeStruct((N_DEV, r, c), _BF16),
        compiler_params=pltpu.CompilerParams(dimension_semantics=("parallel",)),
    )(slab_ids, a)


def _pre_add(slab_ids, grad, got, name):
    _, r, c = grad.shape
    tr = _row_tile(r, c)

    def body(ids_ref, a_ref, b_ref, o_ref):
        del ids_ref
        o_ref[...] = (a_ref[...] + b_ref[...]).astype(_BF16)

    return pl.pallas_call(
        body, name=name,
        grid_spec=pltpu.PrefetchScalarGridSpec(
            num_scalar_prefetch=1, grid=(N_CHIPS - 1, r // tr),
            in_specs=[pl.BlockSpec((1, tr, c), lambda s, i, ids: (ids[s + 1], i, 0)),
                      pl.BlockSpec((1, tr, c), lambda s, i, ids: (s + 1, i, 0))],
            out_specs=pl.BlockSpec((1, tr, c), lambda s, i, ids: (s, i, 0))),
        out_shape=jax.ShapeDtypeStruct((N_CHIPS - 1, r, c), _BF16),
        compiler_params=pltpu.CompilerParams(dimension_semantics=("parallel", "parallel")),
    )(slab_ids, grad, got)


def _ada_forward(craw, w_shard, name):
    d, cols = w_shard.shape
    tk = _tile(d, 512, LANES)

    def body(c_ref, w_ref, o_ref):
        @pl.when(pl.program_id(0) == 0)
        def _():
            o_ref[...] = jnp.zeros_like(o_ref)

        o_ref[...] += jnp.dot(_silu(c_ref[...]).astype(_BF16), w_ref[...].astype(_BF16), preferred_element_type=_F32)

    return pl.pallas_call(
        body, name=name, grid=(d // tk,),
        in_specs=[pl.BlockSpec((craw.shape[0], tk), lambda k: (0, k)), pl.BlockSpec((tk, cols), lambda k: (k, 0))],
        out_specs=pl.BlockSpec((craw.shape[0], cols), lambda k: (0, 0)),
        out_shape=jax.ShapeDtypeStruct((craw.shape[0], cols), _F32),
        compiler_params=pltpu.CompilerParams(dimension_semantics=("arbitrary",)),
    )(craw, w_shard)


def _ada_backward(craw, dmod, w, m, v, name):
    d, cols = w.shape
    rows = craw.shape[0]
    tr = _tile(d, 256, LANES)

    def body(c_ref, dm_ref, w_ref, m_ref, v_ref, g_ref, dl_ref, nm_ref, nv_ref, dc_ref):
        act = _silu(c_ref[...]).astype(_BF16)
        dmb = dm_ref[...].astype(_BF16)
        wv = w_ref[...]
        g = lax.dot_general(act, dmb, (((0,), (0,)), ((), ())), preferred_element_type=_F32)
        delta, nm, nv = _adamw(wv, g, m_ref[...], v_ref[...])
        g_ref[...] = g
        dl_ref[...] = delta
        nm_ref[...] = nm
        nv_ref[...] = nv
        dc = lax.dot_general(dmb, wv.astype(_BF16), (((1,), (1,)), ((), ())), preferred_element_type=_F32)
        dc_ref[...] = jnp.broadcast_to(jnp.sum(dc[N_DEV:], axis=0, keepdims=True), dc_ref.shape)

    blk = pl.BlockSpec((tr, cols), lambda i: (i, 0))
    return pl.pallas_call(
        body, name=name, grid=(d // tr,),
        in_specs=[pl.BlockSpec((rows, tr), lambda i: (0, i)), pl.BlockSpec((rows, cols), lambda i: (0, 0)), blk, blk, blk],
        out_specs=[blk, blk, blk, blk, pl.BlockSpec((SUBLANES, tr), lambda i: (0, i))],
        out_shape=[jax.ShapeDtypeStruct((d, cols), _F32)] * 4 + [jax.ShapeDtypeStruct((SUBLANES, d), _F32)],
        compiler_params=pltpu.CompilerParams(dimension_semantics=("parallel",)),
    )(craw, dmod, w, m, v)


def _rms(xf):
    return lax.rsqrt(jnp.mean(xf * xf, axis=-1, keepdims=True) + EPS)


def _head_mean(v):
    hi = v.astype(_BF16)
    lo = (v - hi.astype(_F32)).astype(_BF16)
    ones = jnp.full((2 * HEAD_DIM, HEAD_DIM), 1.0 / HEAD_DIM, _BF16)
    return jnp.dot(jnp.concatenate([hi, lo], axis=1), ones, preferred_element_type=_F32)


def _prenorm(ctx, x, g_pre, mods, tr, name):
    l, d = ctx.shape
    n = x.shape[0]
    nbl = n // tr

    def body(ctx_ref, x_ref, g_ref, mod_ref, h_ref):
        def emit(src_ref):
            xf = src_ref[...]
            y = (xf * _rms(xf)) * g_ref[...]
            h_ref[...] = (y * (1.0 + mod_ref[0, 0:1, :]) + mod_ref[0, 1:2, :]).astype(_BF16)

        is_latent = pl.program_id(0) < nbl
        pl.when(is_latent)(lambda: emit(x_ref))
        pl.when(jnp.logical_not(is_latent))(lambda: emit(ctx_ref))

    return pl.pallas_call(
        body, name=name, grid=((l + n) // tr,),
        in_specs=[pl.BlockSpec((tr, d), lambda i: (jnp.maximum(i - nbl, 0), 0)),
                  pl.BlockSpec((tr, d), lambda i: (jnp.minimum(i, nbl - 1), 0)),
                  pl.BlockSpec((1, d), lambda i: (0, 0)),
                  pl.BlockSpec((1, SUBLANES, d), lambda i: ((i < nbl).astype(jnp.int32), 0, 0))],
        out_specs=pl.BlockSpec((tr, d), lambda i: (i, 0)),
        out_shape=jax.ShapeDtypeStruct((l + n, d), _BF16),
        compiler_params=pltpu.CompilerParams(dimension_semantics=("arbitrary",)),
    )(ctx, x, g_pre, mods)


def _prenorm_backward(dh, ctx, x, dxn, g_pre, mods, tr, name):
    l, d = ctx.shape
    n = x.shape[0]
    nbl = n // tr

    def body(dh_ref, ctx_ref, x_ref, dxn_ref, g_ref, mod_ref, gx_ref, dmod_ref, dg_ref):
        i = pl.program_id(0)

        @pl.when(i == 0)
        def _():
            dg_ref[...] = jnp.zeros_like(dg_ref)

        @pl.when(jnp.logical_or(i == 0, i == nbl))
        def _():
            dmod_ref[...] = jnp.zeros_like(dmod_ref)

        def emit(src_ref, latent):
            xf = src_ref[...]
            r = _rms(xf)
            xn = xf * r
            dhv = dh_ref[...]
            one_scale = 1.0 + mod_ref[0, 0:1, :]
            dmod_ref[0, 0:1, :] += jnp.sum(dhv * (xn * g_ref[...]), axis=0, keepdims=True)
            dmod_ref[0, 1:2, :] += jnp.sum(dhv, axis=0, keepdims=True)
            dyg = dhv * one_scale
            dg_ref[0:1, :] += jnp.sum(dyg * xn, axis=0, keepdims=True)
            if latent:
                dn = dyg * g_ref[...]
                gx_ref[...] = dxn_ref[...] + r * (dn - xn * jnp.mean(dn * xn, axis=-1, keepdims=True))

        pl.when(i < nbl)(lambda: emit(x_ref, True))
        pl.when(i >= nbl)(lambda: emit(ctx_ref, False))

    lat = pl.BlockSpec((tr, d), lambda i: (jnp.minimum(i, nbl - 1), 0))
    sel = pl.BlockSpec((1, SUBLANES, d), lambda i: ((i < nbl).astype(jnp.int32), 0, 0))
    return pl.pallas_call(
        body, name=name, grid=((l + n) // tr,),
        in_specs=[pl.BlockSpec((tr, d), lambda i: (i, 0)),
                  pl.BlockSpec((tr, d), lambda i: (jnp.maximum(i - nbl, 0), 0)),
                  lat, lat, pl.BlockSpec((1, d), lambda i: (0, 0)), sel],
        out_specs=[lat, sel, pl.BlockSpec((SUBLANES, d), lambda i: (0, 0))],
        out_shape=[jax.ShapeDtypeStruct((n, d), _F32), jax.ShapeDtypeStruct((2, SUBLANES, d), _F32),
                   jax.ShapeDtypeStruct((SUBLANES, d), _F32)],
        compiler_params=pltpu.CompilerParams(dimension_semantics=("arbitrary",)),
    )(dh, ctx, x, dxn, g_pre, mods)


def _rope_tables(l, n):
    rows = n // GRID_W
    inv = ROPE_THETA ** (-jnp.arange(ROPE_PAIRS, dtype=_F32) / ROPE_PAIRS)
    ang_r = jnp.arange(rows, dtype=_F32)[:, None] * inv
    ang_c = jnp.arange(GRID_W, dtype=_F32)[:, None] * inv
    per_row = lambda a: jnp.repeat(a, GRID_W, axis=0)
    per_col = lambda a: jnp.tile(a, (rows, 1))
    cr, sr, cc, sc = per_row(jnp.cos(ang_r)), per_row(jnp.sin(ang_r)), per_col(jnp.cos(ang_c)), per_col(jnp.sin(ang_c))
    zero = jnp.zeros_like(sr)
    tc = jnp.concatenate([cr, cr, cc, cc], axis=-1)
    ta = jnp.concatenate([-sr, zero, -sc, zero], axis=-1)
    tb = jnp.concatenate([zero, sr, zero, sc], axis=-1)
    pad = lambda t, fill: jnp.concatenate([t, jnp.full((l, HEAD_DIM), fill, _F32)], axis=0)
    return pad(tc, 1.0), pad(ta, 0.0), pad(tb, 0.0)


def _rope(y, tc, ta, tb):
    return y * tc + pltpu.roll(y, HEAD_DIM - ROPE_PAIRS, 1) * ta + pltpu.roll(y, ROPE_PAIRS, 1) * tb


def _rope_transposed(dy, tc, ta, tb):
    return dy * tc + pltpu.roll(dy * ta, ROPE_PAIRS, 1) + pltpu.roll(dy * tb, HEAD_DIM - ROPE_PAIRS, 1)


def _qkv_post(proj, tables, g_q, g_k, heads, kv_heads, tr, name):
    t = proj.shape[0]
    aw, kw = heads * HEAD_DIM, kv_heads * HEAD_DIM
    w3 = aw + 2 * kw

    def body(p_ref, tc_ref, ta_ref, tb_ref, gq_ref, gk_ref, q_ref, k_ref, v_ref):
        tabs = (tc_ref[...], ta_ref[...], tb_ref[...])

        def norm_rope(col, gain):
            xh = p_ref[:, col:col + HEAD_DIM]
            return _rope((xh * lax.rsqrt(_head_mean(xh * xh) + EPS)) * gain, *tabs).astype(_BF16)

        for h in range(heads):
            q_ref[h] = norm_rope(h * HEAD_DIM, gq_ref[...])
        for h in range(kv_heads):
            k_ref[h] = norm_rope(aw + h * HEAD_DIM, gk_ref[...])
            v_ref[h] = p_ref[:, aw + kw + h * HEAD_DIM:aw + kw + (h + 1) * HEAD_DIM].astype(_BF16)

    tab = pl.BlockSpec((tr, HEAD_DIM), lambda i: (i, 0))
    gain = pl.BlockSpec((1, HEAD_DIM), lambda i: (0, 0))
    return pl.pallas_call(
        body, name=name, grid=(t // tr,),
        in_specs=[pl.BlockSpec((tr, w3), lambda i: (i, 0)), tab, tab, tab, gain, gain],
        out_specs=[pl.BlockSpec((heads, tr, HEAD_DIM), lambda i: (0, i, 0)),
                   pl.BlockSpec((kv_heads, tr, HEAD_DIM), lambda i: (0, i, 0)),
                   pl.BlockSpec((kv_heads, tr, HEAD_DIM), lambda i: (0, i, 0))],
        out_shape=[jax.ShapeDtypeStruct((heads, t, HEAD_DIM), _BF16),
                   jax.ShapeDtypeStruct((kv_heads, t, HEAD_DIM), _BF16),
                   jax.ShapeDtypeStruct((kv_heads, t, HEAD_DIM), _BF16)],
        compiler_params=pltpu.CompilerParams(dimension_semantics=("parallel",)),
    )(proj, *tables, g_q, g_k)


def _qkv_post_backward(proj, dq, dk, dv, tables, g_q, g_k, dproj, l, tr, name):
    t = proj.shape[0]
    heads, kv_heads = dq.shape[0], dk.shape[0]
    aw, kw = heads * HEAD_DIM, kv_heads * HEAD_DIM
    w3 = aw + 2 * kw
    nbl = (t - l) // tr

    def body(p_ref, dq_ref, dk_ref, dv_ref, tc_ref, ta_ref, tb_ref, gq_ref, gk_ref, dproj_ref, o_ref, dgq_ref, dgk_ref):
        del dproj_ref
        i = pl.program_id(0)

        @pl.when(i == 0)
        def _():
            dgq_ref[...] = jnp.zeros_like(dgq_ref)
            dgk_ref[...] = jnp.zeros_like(dgk_ref)

        tabs = (tc_ref[...], ta_ref[...], tb_ref[...])
        latent = i < nbl

        def back(col, dout, gain, dg_ref):
            xh = p_ref[:, col:col + HEAD_DIM]
            r = lax.rsqrt(_head_mean(xh * xh) + EPS)
            xn = xh * r
            dy = _rope_transposed(dout, *tabs)
            dg_ref[0:1, :] += jnp.sum(dy * xn, axis=0, keepdims=True)
            dn = dy * gain
            o_ref[:, col:col + HEAD_DIM] = (r * (dn - xn * _head_mean(dn * xn))).astype(_BF16)

        for h in range(heads):
            back(h * HEAD_DIM, jnp.where(latent, dq_ref[h], 0.0), gq_ref[...], dgq_ref)
        for h in range(kv_heads):
            back(aw + h * HEAD_DIM, dk_ref[h], gk_ref[...], dgk_ref)
            o_ref[:, aw + kw + h * HEAD_DIM:aw + kw + (h + 1) * HEAD_DIM] = dv_ref[h].astype(_BF16)

    tab = pl.BlockSpec((tr, HEAD_DIM), lambda i: (i, 0))
    gain = pl.BlockSpec((1, HEAD_DIM), lambda i: (0, 0))
    acc = pl.BlockSpec((SUBLANES, HEAD_DIM), lambda i: (0, 0))
    return pl.pallas_call(
        body, name=name, grid=(t // tr,),
        in_specs=[pl.BlockSpec((tr, w3), lambda i: (i, 0)),
                  pl.BlockSpec((heads, tr, HEAD_DIM), lambda i: (0, jnp.minimum(i, nbl - 1), 0)),
                  pl.BlockSpec((kv_heads, tr, HEAD_DIM), lambda i: (0, i, 0)),
                  pl.BlockSpec((kv_heads, tr, HEAD_DIM), lambda i: (0, i, 0)),
                  tab, tab, tab, gain, gain, _ANY],
        out_specs=[pl.BlockSpec((tr, w3), lambda i: (i, 0)), acc, acc],
        out_shape=[jax.ShapeDtypeStruct(dproj.shape, dproj.dtype), jax.ShapeDtypeStruct((SUBLANES, HEAD_DIM), _F32),
                   jax.ShapeDtypeStruct((SUBLANES, HEAD_DIM), _F32)],
        input_output_aliases={9: 0},
        compiler_params=pltpu.CompilerParams(dimension_semantics=("arbitrary",)),
    )(proj, dq, dk, dv, *tables, g_q, g_k, dproj)


def _zero_context_rows(dproj, l, w3, tr, name):
    t, iw = dproj.shape
    first = (t - l) // tr

    def body(dproj_ref, o_ref):
        del dproj_ref
        o_ref[...] = jnp.zeros_like(o_ref)

    return pl.pallas_call(
        body, name=name, grid=(l // tr, iw // w3 - 1),
        in_specs=[_ANY], out_specs=pl.BlockSpec((tr, w3), lambda i, j: (first + i, j + 1)),
        out_shape=jax.ShapeDtypeStruct(dproj.shape, dproj.dtype), input_output_aliases={0: 0},
        compiler_params=pltpu.CompilerParams(dimension_semantics=("parallel", "parallel")),
    )(dproj)


def _attention(q, k, v, proj, l, mix, tq, name):
    heads, t, _ = q.shape
    kv_heads = k.shape[0]
    n = t - l
    gw = GQA_GROUP * HEAD_DIM
    aw = heads * HEAD_DIM
    gate_col = (aw + 2 * kv_heads * HEAD_DIM) // gw

    def body(q_ref, k_ref, v_ref, g_ref, o_ref, y_ref, lse_ref):
        lane = lax.broadcasted_iota(jnp.int32, (tq, LANES), 1)
        lse_blk = jnp.zeros((tq, LANES), _F32)
        firsts = list(range(0, GQA_GROUP, ATTN_SUB_HEADS))

        def scores(first):
            qs = q_ref[first:first + ATTN_SUB_HEADS].reshape(ATTN_SUB_HEADS * tq, HEAD_DIM)
            return lax.dot_general(qs, k_ref[0], (((1,), (1,)), ((), ())), preferred_element_type=_F32)

        raw_next = scores(firsts[0])
        for idx, first in enumerate(firsts):
            raw = raw_next
            if idx + 1 < len(firsts):
                raw_next = scores(firsts[idx + 1])
            m = jnp.max(raw, axis=-1, keepdims=True)
            p = jnp.exp2((raw - m) * (ATTN_SCALE * LOG2_E))
            denom = jnp.sum(p, axis=-1, keepdims=True)
            os_ = jnp.dot(p.astype(_BF16), v_ref[0], preferred_element_type=_F32) / denom
            lse_s = m * ATTN_SCALE + jnp.log(denom)
            for j in range(ATTN_SUB_HEADS):
                g = first + j
                og = os_[j * tq:(j + 1) * tq]
                cols = slice(g * HEAD_DIM, (g + 1) * HEAD_DIM)
                o_ref[:, cols] = og
                y_ref[:, cols] = (og * _silu(g_ref[:, cols])).astype(_BF16)
                lse_blk = jnp.where(lane == g, lse_s[j * tq:(j + 1) * tq], lse_blk)
        lse_ref[0] = lse_blk

    return pl.pallas_call(
        body, name=name, grid=(kv_heads, n // tq),
        in_specs=[pl.BlockSpec((GQA_GROUP, tq, HEAD_DIM), lambda h, i: (h, i, 0)),
                  pl.BlockSpec((1, t, HEAD_DIM), lambda h, i: (h, 0, 0)),
                  pl.BlockSpec((1, t, HEAD_DIM), lambda h, i: (h, 0, 0)),
                  pl.BlockSpec((tq, gw), lambda h, i: (i, gate_col + h))],
        out_specs=[pl.BlockSpec((tq, gw), lambda h, i: (i, h)),
                   pl.BlockSpec((tq, gw), lambda h, i: (i, h)),
                   pl.BlockSpec((1, tq, LANES), lambda h, i: (h, i, 0))],
        out_shape=[jax.ShapeDtypeStruct((n, aw), _F32), jax.ShapeDtypeStruct((n, mix), _BF16),
                   jax.ShapeDtypeStruct((kv_heads, n, LANES), _F32)],
        compiler_params=pltpu.CompilerParams(dimension_semantics=("parallel", "parallel")),
    )(q, k, v, proj)


def _attention_backward(q, k, v, attn_o, dy, proj, lse, after, dproj, l, tq, name):
    heads, t, _ = q.shape
    kv_heads = k.shape[0]
    n = t - l
    gw = GQA_GROUP * HEAD_DIM
    aw = heads * HEAD_DIM
    gate_col = (aw + 2 * kv_heads * HEAD_DIM) // gw
    n_parts = next(p for p in (ATTN_KEY_PARTS, 2, 1) if t % (p * BF16_ROWS) == 0)
    part = t // n_parts

    def body(q_ref, k_ref, v_ref, o_ref, dy_ref, g_ref, lse_ref, after_ref, dproj_ref, dq_ref, dg_ref, dk_ref, dv_ref):
        del after_ref, dproj_ref

        @pl.when(pl.program_id(1) == 0)
        def _():
            dk_ref[...] = jnp.zeros_like(dk_ref)
            dv_ref[...] = jnp.zeros_like(dv_ref)

        lse_blk = lse_ref[0]
        for first in range(0, GQA_GROUP, ATTN_BWD_SUB_HEADS):
            qs = q_ref[first:first + ATTN_BWD_SUB_HEADS].reshape(ATTN_BWD_SUB_HEADS * tq, HEAD_DIM)
            do_parts, delta_parts, lse_parts = [], [], []
            for g in range(first, first + ATTN_BWD_SUB_HEADS):
                cols = slice(g * HEAD_DIM, (g + 1) * HEAD_DIM)
                gate, og, dyg = g_ref[:, cols], o_ref[:, cols], dy_ref[:, cols]
                dog = dyg * _silu(gate)
                dg_ref[:, cols] = (dyg * og * _silu_grad(gate)).astype(_BF16)
                do_parts.append(dog)
                delta_parts.append(jnp.sum(dog * og, axis=-1, keepdims=True))
                lse_parts.append(lse_blk[:, g:g + 1])
            dos = jnp.concatenate(do_parts, axis=0).astype(_BF16)
            delta = jnp.concatenate(delta_parts, axis=0)
            lse2 = jnp.concatenate(lse_parts, axis=0) * LOG2_E
            dqs = jnp.zeros((ATTN_BWD_SUB_HEADS * tq, HEAD_DIM), _F32)
            for part_i in range(n_parts):
                keys = slice(part_i * part, (part_i + 1) * part)
                ks, vs = k_ref[0, keys, :], v_ref[0, keys, :]
                raw = lax.dot_general(qs, ks, (((1,), (1,)), ((), ())), preferred_element_type=_F32)
                p = jnp.exp2(raw * (ATTN_SCALE * LOG2_E) - lse2)
                dp = lax.dot_general(dos, vs, (((1,), (1,)), ((), ())), preferred_element_type=_F32)
                ds = (p * (dp - delta)).astype(_BF16)
                dqs = dqs + jnp.dot(ds, ks, preferred_element_type=_F32)
                dk_ref[0, keys, :] += ATTN_SCALE * lax.dot_general(
                    ds, qs, (((0,), (0,)), ((), ())), preferred_element_type=_F32)
                dv_ref[0, keys, :] += lax.dot_general(
                    p.astype(_BF16), dos, (((0,), (0,)), ((), ())), preferred_element_type=_F32)
            dq_ref[first:first + ATTN_BWD_SUB_HEADS] = (ATTN_SCALE * dqs).reshape(ATTN_BWD_SUB_HEADS, tq, HEAD_DIM)

    kv_spec = pl.BlockSpec((1, t, HEAD_DIM), lambda h, i: (h, 0, 0))
    tok = pl.BlockSpec((tq, gw), lambda h, i: (i, h))
    gate = pl.BlockSpec((tq, gw), lambda h, i: (i, gate_col + h))
    return pl.pallas_call(
        body, name=name, grid=(kv_heads, n // tq),
        in_specs=[pl.BlockSpec((GQA_GROUP, tq, HEAD_DIM), lambda h, i: (h, i, 0)), kv_spec, kv_spec,
                  tok, tok, gate, pl.BlockSpec((1, tq, LANES), lambda h, i: (h, i, 0)),
                  pl.BlockSpec(after.shape, lambda h, i: (0, 0)), _ANY],
        out_specs=[pl.BlockSpec((GQA_GROUP, tq, HEAD_DIM), lambda h, i: (h, i, 0)), gate, kv_spec, kv_spec],
        out_shape=[jax.ShapeDtypeStruct((heads, n, HEAD_DIM), _F32), jax.ShapeDtypeStruct(dproj.shape, dproj.dtype),
                   jax.ShapeDtypeStruct((kv_heads, t, HEAD_DIM), _F32), jax.ShapeDtypeStruct((kv_heads, t, HEAD_DIM), _F32)],
        input_output_aliases={8: 1},
        compiler_params=pltpu.CompilerParams(dimension_semantics=("parallel", "arbitrary")),
    )(q, k, v, attn_o, dy, proj, lse, after, dproj)


def _halo_specs(tp, width, col, row_off, total_rows):
    per = tp // POOL_HALO
    first = row_off // POOL_HALO
    last = total_rows // POOL_HALO - 1
    return [pl.BlockSpec((tp, width), lambda i: (i + row_off // tp, col)),
            pl.BlockSpec((POOL_HALO, width), lambda i: (jnp.maximum(first + i * per - 1, 0), col)),
            pl.BlockSpec((POOL_HALO, width), lambda i: (jnp.minimum(first + (i + 1) * per, last), col))]


def _with_halo(cur, prev, nxt, t0, n):
    tp = cur.shape[0]
    r8 = lax.broadcasted_iota(jnp.int32, (POOL_HALO, 1), 0)
    prev = jnp.where(t0 - POOL_HALO + r8 >= 0, prev, 0.0)
    nxt = jnp.where(t0 + tp + r8 < n, nxt, 0.0)
    return jnp.concatenate([prev, cur, nxt], axis=0)


def _shift_rows(a, s):
    return pltpu.roll(a, s % a.shape[0], 0)


def _window_sum(e, w, mirrored):
    a = e + _shift_rows(e, -1 if mirrored else 1)
    s = 1
    while 2 * s < w:
        a = _shift_rows(a, s) + _shift_rows(a, -s)
        s *= 2
    return a


def _window_count(t, w, n):
    half = w // 2
    return (jnp.minimum(t + half, n) - jnp.maximum(t - half, 0)).astype(_F32)


def _pool_forward(gi, proj, y, pool_w, pool_scale, l, heads, kv_heads, tp, name):
    t = proj.shape[0]
    n = t - l
    pg = pool_w.shape[-1]
    w = POOL_WINDOWS[gi]
    aw, kw = heads * HEAD_DIM, kv_heads * HEAD_DIM
    u_col = (2 * aw + 2 * kw) // pg + gi
    gate_col = (2 * aw + 2 * kw + len(POOL_WINDOWS) * pg) // pg + gi

    def body(u_ref, up_ref, un_ref, g_ref, w_ref, sc_ref, y_in_ref, y_ref, raw_ref, d_ref):
        del y_in_ref
        t0 = pl.program_id(0) * tp
        cur = u_ref[...]
        win = _window_sum(_with_halo(cur, up_ref[...], un_ref[...], t0, n), w, False)[POOL_HALO:POOL_HALO + tp]
        tok = t0 + lax.broadcasted_iota(jnp.int32, (tp, 1), 0)
        d = (win / _window_count(tok, w, n) - cur).astype(_BF16)
        raw = jnp.dot(d, w_ref[...].reshape(pg, pg), preferred_element_type=_F32)
        d_ref[...] = d
        raw_ref[...] = raw
        y_ref[...] = ((raw * sc_ref[...]) * _silu(g_ref[...])).astype(_BF16)

    blk = pl.BlockSpec((tp, pg), lambda i: (i, 0))
    return pl.pallas_call(
        body, name=name, grid=(n // tp,),
        in_specs=_halo_specs(tp, pg, u_col, 0, t) + [
            pl.BlockSpec((tp, pg), lambda i: (i, gate_col)),
            pl.BlockSpec((N_DEV, 1, pg // N_DEV, pg), lambda i: (0, gi, 0, 0)),
            pl.BlockSpec((1, pg), lambda i: (0, gi)), _ANY],
        out_specs=[pl.BlockSpec((tp, pg), lambda i: (i, aw // pg + gi)), blk, blk],
        out_shape=[jax.ShapeDtypeStruct(y.shape, y.dtype), jax.ShapeDtypeStruct((n, pg), _F32),
                   jax.ShapeDtypeStruct((n, pg), _BF16)],
        input_output_aliases={6: 0},
        compiler_params=pltpu.CompilerParams(dimension_semantics=("arbitrary",)),
    )(proj, proj, proj, proj, pool_w, pool_scale, y)


def _pool_backward_gate(gi, dy, proj, raw, pool_w, pool_scale, dproj, l, heads, kv_heads, tp, name):
    n, pg = raw.shape
    aw, kw = heads * HEAD_DIM, kv_heads * HEAD_DIM
    gate_col = (2 * aw + 2 * kw + len(POOL_WINDOWS) * pg) // pg + gi

    def body(dy_ref, g_ref, raw_ref, w_ref, sc_ref, dproj_ref, dg_ref, dr_ref, dd_ref, ds_ref):
        del dproj_ref

        @pl.when(pl.program_id(0) == 0)
        def _():
            ds_ref[...] = jnp.zeros_like(ds_ref)

        gate, rawv, dyv, scale = g_ref[...], raw_ref[...], dy_ref[...], sc_ref[...]
        dpool = dyv * _silu(gate)
        dg_ref[...] = (dyv * (rawv * scale) * _silu_grad(gate)).astype(_BF16)
        ds_ref[0:1, :] += jnp.sum(dpool * rawv, axis=0, keepdims=True)
        draw = (dpool * scale).astype(_BF16)
        dr_ref[...] = draw
        dd_ref[...] = lax.dot_general(
            draw, w_ref[...].reshape(pg, pg), (((1,), (1,)), ((), ())), preferred_element_type=_F32)

    blk = pl.BlockSpec((tp, pg), lambda i: (i, 0))
    gate = pl.BlockSpec((tp, pg), lambda i: (i, gate_col))
    return pl.pallas_call(
        body, name=name, grid=(n // tp,),
        in_specs=[pl.BlockSpec((tp, pg), lambda i: (i, aw // pg + gi)), gate, blk,
                  pl.BlockSpec((N_DEV, 1, pg // N_DEV, pg), lambda i: (0, gi, 0, 0)),
                  pl.BlockSpec((1, pg), lambda i: (0, gi)), _ANY],
        out_specs=[gate, blk, blk, pl.BlockSpec((SUBLANES, pg), lambda i: (0, 0))],
        out_shape=[jax.ShapeDtypeStruct(dproj.shape, dproj.dtype), jax.ShapeDtypeStruct((n, pg), _BF16),
                   jax.ShapeDtypeStruct((n, pg), _F32), jax.ShapeDtypeStruct((SUBLANES, pg), _F32)],
        input_output_aliases={5: 0},
        compiler_params=pltpu.CompilerParams(dimension_semantics=("arbitrary",)),
    )(dy, proj, raw, pool_w, pool_scale, dproj)


def _pool_backward_window(gi, dd, dproj, l, col, tp, name):
    n, pg = dd.shape
    w = POOL_WINDOWS[gi]

    def body(c_ref, p_ref, n_ref, dproj_ref, du_ref):
        del dproj_ref
        t0 = pl.program_id(0) * tp
        cur = c_ref[...]
        e = _with_halo(cur, p_ref[...], n_ref[...], t0, n)
        tok = t0 - POOL_HALO + lax.broadcasted_iota(jnp.int32, (tp + 2 * POOL_HALO, 1), 0)
        e = e / jnp.maximum(_window_count(tok, w, n), 1.0)
        du_ref[...] = (_window_sum(e, w, True)[POOL_HALO:POOL_HALO + tp] - cur).astype(_BF16)

    return pl.pallas_call(
        body, name=name, grid=(n // tp,),
        in_specs=_halo_specs(tp, pg, 0, 0, n) + [_ANY],
        out_specs=pl.BlockSpec((tp, pg), lambda i: (i, col)),
        out_shape=jax.ShapeDtypeStruct(dproj.shape, dproj.dtype), input_output_aliases={3: 0},
        compiler_params=pltpu.CompilerParams(dimension_semantics=("arbitrary",)),
    )(dd, dd, dd, dproj)


def _post(out, x, target, gate, g_post, tr, name):
    n, d = out.shape

    def body(o_ref, x_ref, t_ref, gate_ref, g_ref, dxn_ref, do_ref, dgate_ref, dg_ref, loss_ref):
        @pl.when(pl.program_id(0) == 0)
        def _():
            dgate_ref[...] = jnp.zeros_like(dgate_ref)
            dg_ref[...] = jnp.zeros_like(dg_ref)
            loss_ref[...] = jnp.zeros_like(loss_ref)

        ov = o_ref[...]
        r = _rms(ov)
        on = ov * r
        normed = on * g_ref[...]
        err = (x_ref[...] + gate_ref[...] * normed) - t_ref[...]
        loss_ref[...] += jnp.sum(err * err)
        dxn = err / d
        dxn_ref[...] = dxn
        dgate_ref[0:1, :] += jnp.sum(dxn * normed, axis=0, keepdims=True)
        dr = dxn * gate_ref[...]
        dg_ref[0:1, :] += jnp.sum(dr * on, axis=0, keepdims=True)
        dn = dr * g_ref[...]
        do_ref[...] = (r * (dn - on * jnp.mean(dn * on, axis=-1, keepdims=True))).astype(_BF16)

    blk = pl.BlockSpec((tr, d), lambda i: (i, 0))
    vec = pl.BlockSpec((1, d), lambda i: (0, 0))
    acc = pl.BlockSpec((SUBLANES, d), lambda i: (0, 0))
    return pl.pallas_call(
        body, name=name, grid=(n // tr,),
        in_specs=[blk, blk, blk, vec, vec],
        out_specs=[blk, blk, acc, acc, pl.BlockSpec((SUBLANES, LANES), lambda i: (0, 0))],
        out_shape=[jax.ShapeDtypeStruct((n, d), _F32), jax.ShapeDtypeStruct((n, d), _BF16),
                   jax.ShapeDtypeStruct((SUBLANES, d), _F32), jax.ShapeDtypeStruct((SUBLANES, d), _F32),
                   jax.ShapeDtypeStruct((SUBLANES, LANES), _F32)],
        compiler_params=pltpu.CompilerParams(dimension_semantics=("arbitrary",)),
    )(out, x, target, gate, g_post)


def _adam_sharded(slab_ids, grad, got, far, w, m, v, name):
    r, c = w.shape
    tr = _tile(r, max(BF16_ROWS, min(256, (1 << 18) // c)), BF16_ROWS)

    def body(ids_ref, own_ref, got_ref, far_ref, w_ref, m_ref, v_ref, g_ref, dl_ref, nm_ref, nv_ref):
        del ids_ref
        g = own_ref[0] + got_ref[0]
        for k in range(N_CHIPS - 1):
            g = g + far_ref[k].astype(_F32)
        delta, nm, nv = _adamw(w_ref[...], g, m_ref[...], v_ref[...])
        g_ref[...] = g
        dl_ref[...] = delta
        nm_ref[...] = nm
        nv_ref[...] = nv

    blk = pl.BlockSpec((tr, c), lambda i, ids: (i, 0))
    return pl.pallas_call(
        body, name=name,
        grid_spec=pltpu.PrefetchScalarGridSpec(
            num_scalar_prefetch=1, grid=(r // tr,),
            in_specs=[pl.BlockSpec((1, tr, c), lambda i, ids: (ids[0], i, 0)),
                      pl.BlockSpec((1, tr, c), lambda i, ids: (0, i, 0)),
                      pl.BlockSpec((N_CHIPS - 1, tr, c), lambda i, ids: (0, i, 0)), blk, blk, blk],
            out_specs=[blk] * 4),
        out_shape=[jax.ShapeDtypeStruct((r, c), _F32)] * 4,
        compiler_params=pltpu.CompilerParams(dimension_semantics=("parallel",)),
    )(slab_ids, grad, got, far, w, m, v)


def _adam_replicated(parts, extra, through_silu, w, m, v, name):
    def body(p_ref, e_ref, s_ref, w_ref, m_ref, v_ref, g_ref, dl_ref, nm_ref, nv_ref):
        total = p_ref[0] + e_ref[0]
        for dev in range(1, N_DEV):
            total = total + (p_ref[dev] + e_ref[dev])
        g = jnp.where(s_ref[...] > 0.5, total * _silu_grad(w_ref[...]), total)
        delta, nm, nv = _adamw(w_ref[...], g, m_ref[...], v_ref[...])
        g_ref[...] = g
        dl_ref[...] = delta
        nm_ref[...] = nm
        nv_ref[...] = nv

    return pl.pallas_call(
        body, name=name, in_specs=[_VMEM] * 6, out_specs=[_VMEM] * 4,
        out_shape=[jax.ShapeDtypeStruct(w.shape, _F32)] * 4,
    )(parts, extra, through_silu, w, m, v)


def _as_rows(vec):
    size = vec.shape[0]
    padded = -(-size // (SUBLANES * LANES)) * SUBLANES * LANES
    return jnp.pad(vec, (0, padded - size)).reshape(padded // LANES, LANES)


def kernel(x, c, ctx, c_ctx, w_ada, b_ada, norm_pre, norm_post, w_in, q_norm, k_norm, pool_w, pool_scale, w_out, loss_target, m_c_ctx, m_w_ada, m_b_ada, m_norm_pre, m_norm_post, m_w_in, m_q_norm, m_k_norm, m_pool_w, m_pool_scale, m_w_out, v_c_ctx, v_w_ada, v_b_ada, v_norm_pre, v_norm_post, v_w_in, v_q_norm, v_k_norm, v_pool_w, v_pool_scale, v_w_out):
    me = _dev_index(*_position())
    x2, ctx2, target = x[0], ctx[0], loss_target[0]
    n, d = x2.shape
    l = ctx2.shape[0]
    t = l + n
    aw = d // 2
    heads = aw // HEAD_DIM
    kv_heads = heads // GQA_GROUP
    kw = kv_heads * HEAD_DIM
    n_groups = len(POOL_WINDOWS)
    pg = (d - aw) // n_groups
    mix = d
    tr = _tile(l, 128, BF16_ROWS)
    tr2 = _tile(l, 256, BF16_ROWS)
    tq = _tile(l, 128, BF16_ROWS)
    tp = _tile(n, 1024, POOL_HALO)

    xi, yi, ci = _position()
    slab_ids = jnp.stack([_dev_index(*chip, ci) for chip in _chip_order(xi, yi)]).astype(jnp.int32)

    cw = w_in.shape[-1]
    wg = _cast_into_columns(slab_ids, w_in[0], N_DEV, "cast_w_in")
    late = [_cast_into_slab(slab_ids, w_out[0], "cast_w_out"),
            _cast_into_slab(slab_ids, pool_w[0].reshape(-1, pg), "cast_pool_w")]

    c_all = _all_gather_small(_as_rows(c[0]), "gather_c").reshape(N_DEV, -1)[:, :d]
    craw = jnp.concatenate([c_all, jnp.broadcast_to(c_ctx[None], (N_DEV, d))], axis=0)
    ada = _ada_forward(craw, w_ada[0], "ada_forward")
    ada_all = _all_gather_small(ada, "gather_ada")
    mod_all = ada_all.transpose(1, 0, 2).reshape(ada.shape[0], -1) + b_ada[0]
    mod = lax.dynamic_index_in_dim(mod_all, me, 0, keepdims=False)
    mod_c = mod_all[N_DEV]
    shift, scale, gate = mod[:d], mod[d:2 * d], mod[2 * d:]
    zeros6 = jnp.zeros((SUBLANES - 2, d), _F32)
    mods = jnp.stack([jnp.concatenate([mod_c[None, d:2 * d], mod_c[None, :d], zeros6], axis=0),
                      jnp.concatenate([scale[None], shift[None], zeros6], axis=0)])

    a_s, a_r, wg, tok = _w_in_hop(wg, cw, [], "a", ada_all, "gather_w_in_a")
    h_all = _prenorm(ctx2, x2, norm_pre, mods + tok[0, 0], tr2, "prenorm")
    order_ids = jnp.stack([_dev_index(*dev) for dev in _w_in_order(xi, yi, ci)]).astype(jnp.int32)
    proj = lax.empty((t, N_DEV * cw), _F32)
    proj = _proj_blocks(h_all, wg, proj, order_ids, 0, 1, cw, tok, "proj_0")
    b_s, b_r, wg, tok = _w_in_hop(wg, cw, [("a", a_s, a_r, [0, 1], [])], "b", proj, "gather_w_in_b")
    proj = _proj_blocks(h_all, wg, proj, order_ids, 1, 2, cw, tok, "proj_1")
    c_s, c_r, wg, tok = _w_in_hop(wg, cw, [("b", b_s, b_r, [2, 0], [])], "c", proj, "gather_w_in_c")
    proj = _proj_blocks(h_all, wg, proj, order_ids, 3, 2, cw, tok, "proj_2")
    d_s, d_r, wg, tok = _w_in_hop(
        wg, cw, [("c", c_s, c_r, [0], []), ("b", b_s, b_r, [1], [])], "d", proj, "gather_w_in_d")
    proj = _proj_blocks(h_all, wg, proj, order_ids, 5, 2, cw, tok, "proj_3")
    w_in_g, tok = _w_in_hop(
        wg, cw, [("d", d_s, d_r, [0], [0]), ("a", a_s, a_r, [], [0, 1]), ("b", b_s, b_r, [], [0, 1, 2]),
                 ("c", c_s, c_r, [], [0])], None, proj, "gather_w_in_end")
    flight_w = _gather_slabs_start(late, w_in_g, "gather_late_start")
    proj = _proj_blocks(h_all, w_in_g, proj, order_ids, 7, 1, cw, flight_w[-1], "proj_4")
    tables = _rope_tables(l, n)
    q, k, v = _qkv_post(proj, tables, q_norm, k_norm, heads, kv_heads, tr2, "qkv_post")
    attn_o, y, lse = _attention(q, k, v, proj, l, mix, _tile(l, 256, BF16_ROWS), "attention")
    w_out_g8, pool_g8 = _gather_slabs_wait(*flight_w[:3], attn_o, "gather_late_wait")
    w_out_g = w_out_g8.reshape(mix, d)
    pool_g = pool_g8.reshape(N_DEV, n_groups, pg // N_DEV, pg)
    raws, ds = [], []
    for gi in range(n_groups):
        y, raw, dsave = _pool_forward(gi, proj, y, pool_g, pool_scale, l, heads, kv_heads, tp, f"pool_forward_{gi}")
        raws.append(raw)
        ds.append(dsave)
    out = _matmul(y, w_out_g, name="out_proj")
    dxn, dout, dgate8, dgpost8, loss8 = _post(out, x2, target, gate[None], norm_post, tr2, "post")

    gw_out = _matmul(y, dout, ta=True, name="grad_w_out").reshape(N_DEV, mix // N_DEV, d)
    flight_so = _exchange_start(_sibling_copies_by_device, [gw_out], "exchange_sibling_start_w_out", land_slabs=N_CHIPS)
    dy = _matmul(dout, w_out_g, tb=True, after=flight_so[-1], name="d_y")
    gw_out, got_out = _exchange_wait(
        _sibling_copies_by_device, *flight_so[:4], dy, "exchange_sibling_wait_w_out", with_sources=True)
    sum_out = _pre_add(slab_ids, gw_out, got_out, "pre_add_w_out")
    flight_out = _exchange_start(_chip_copies, [sum_out], "exchange_chips_start_w_out")
    w3 = aw + 2 * kw
    dq, dproj, dk, dv = _attention_backward(
        q, k, v, attn_o, dy, proj, lse, flight_out[-1], lax.empty(proj.shape, _BF16), l, tq, "attention_backward")
    dproj, dgq8, dgk8 = _qkv_post_backward(proj, dq, dk, dv, tables, q_norm, k_norm, dproj, l, tr2, "qkv_post_backward")
    dproj = _zero_context_rows(dproj, l, w3, tr, "zero_context_rows")
    gpw, dps8 = [], []
    for gi in range(n_groups):
        dproj, draw, dd, dps = _pool_backward_gate(
            gi, dy, proj, raws[gi], pool_g, pool_scale, dproj, l, heads, kv_heads, tp, f"pool_backward_gate_{gi}")
        dproj = _pool_backward_window(gi, dd, dproj, l, (w3 + aw) // pg + gi, tp, f"pool_backward_window_{gi}")
        dps8.append(dps)
        gpw.append(_matmul(ds[gi], draw, ta=True, name=f"grad_pool_w_{gi}"))
    cw = w_in.shape[-1]
    other_ids = jnp.stack([_dev_index(*chip, 1 - ci) for chip in _chip_order(xi, yi)]).astype(jnp.int32)
    chip_slabs = jnp.arange(N_CHIPS, dtype=jnp.int32)
    pr = pool_w.shape[2]
    gpw8 = jnp.stack(gpw).reshape(n_groups, N_DEV, pr, pg).transpose(1, 0, 2, 3).reshape(N_DEV, n_groups * pr, pg)
    give_in = _matmul_slabs(h_all, dproj, other_ids, cw, "grad_w_in_sibling")
    flight_sib = _exchange_start(_sibling_copies, [give_in, jnp.take(gpw8, other_ids, axis=0)], "exchange_sibling_start")
    gw_in = _matmul_slabs(h_all, dproj, slab_ids, cw, "grad_w_in_own", after=flight_sib[-1])
    gpw_own = jnp.take(gpw8, slab_ids, axis=0)
    got_in, got_pw = _exchange_wait(_sibling_copies, *flight_sib[:4], gw_in, "exchange_sibling_wait")
    sums_in = [_pre_add(chip_slabs, gw_in, got_in, "pre_add_w_in"), _pre_add(chip_slabs, gpw_own, got_pw, "pre_add_pool_w")]
    flight_in = _exchange_start(_chip_copies, sums_in, "exchange_chips_start_w_in")
    dh = _matmul(dproj, w_in_g, tb=True, tm=1088, tk=3072, after=flight_in[-1], name="d_h")
    grad_x, dmods, dgpre8 = _prenorm_backward(dh, ctx2, x2, dxn, norm_pre, mods, tr2, "prenorm_backward")

    dmod_lat = jnp.concatenate([dmods[1, 1], dmods[1, 0], dgate8[0]])
    dmod_ctx = jnp.concatenate([dmods[0, 1], dmods[0, 0], jnp.zeros((d,), _F32)])
    small = jnp.concatenate([dmod_lat, dmod_ctx, dgpre8[0], dgpost8[0], dgq8[0], dgk8[0]] + [p[0] for p in dps8]
                            + [loss8[0, :1]])
    small_rows = _as_rows(small)
    small_buf = lax.dynamic_update_slice(jnp.zeros((N_DEV,) + small_rows.shape, _F32), small_rows[None], (me, 0, 0))
    flight_small = _gather_slabs_start([small_buf], grad_x, "gather_small_start")

    far_out = _exchange_wait(_chip_copies, *flight_out[:4], flight_small[-1], "exchange_chips_wait_w_out")[0]
    far_in, far_pw = _exchange_wait(_chip_copies, *flight_in[:4], far_out, "exchange_chips_wait_w_in")
    two = lambda a: a.reshape(-1, a.shape[-1])
    sharded = []
    for ids, g, got, far, w, m, v_, name in zip(
            (chip_slabs, slab_ids, chip_slabs), (gw_in, gw_out, gpw_own), (got_in, got_out, got_pw),
            (far_in, far_out, far_pw), (w_in, w_out, pool_w), (m_w_in, m_w_out, m_pool_w),
            (v_w_in, v_w_out, v_pool_w), ("adam_w_in", "adam_w_out", "adam_pool_w")):
        res = _adam_sharded(ids, g, got, far, two(w), two(m), two(v_), name)
        sharded.append([r.reshape(w.shape) for r in res])
    (g_w_in, dl_w_in, nm_w_in, nv_w_in), (g_w_out, dl_w_out, nm_w_out, nv_w_out), (g_pw, dl_pw, nm_pw, nv_pw) = sharded

    gathered = _gather_slabs_wait(*flight_small[:3], nv_pw, "gather_small_wait")[0].reshape(N_DEV, -1)
    o = 0
    take = lambda size: (gathered[:, o:o + size], o + size)
    g_mod, o = take(3 * d)
    g_modc, o = take(3 * d)
    g_pre, o = take(d)
    g_post, o = take(d)
    g_q, o = take(HEAD_DIM)
    g_k, o = take(HEAD_DIM)
    g_ps, o = take(n_groups * pg)
    g_loss, o = take(1)
    cols = w_ada.shape[-1]
    mine = lambda a: lax.dynamic_slice_in_dim(a, me * cols, cols, axis=1)
    dmod_rows = jnp.concatenate([mine(g_mod), mine(g_modc)], axis=0)
    g_wada, dl_wada, nm_wada, nv_wada, dcact = _ada_backward(craw, dmod_rows, w_ada[0], m_w_ada[0], v_w_ada[0], "ada_backward")
    dcc = _all_gather_small(_as_rows(dcact[0]), "gather_dcc").reshape(N_DEV, -1)[:, :d]

    sizes = [d, 3 * d, d, d, HEAD_DIM, HEAD_DIM, n_groups * pg]
    def pack(parts):
        rows = jnp.concatenate(parts, axis=1)
        padded = -(-rows.shape[1] // (SUBLANES * LANES)) * SUBLANES * LANES
        return jnp.pad(rows, ((0, 0), (0, padded - rows.shape[1]))).reshape(N_DEV, padded // LANES, LANES)

    zero = lambda size: jnp.zeros((N_DEV, size), _F32)
    parts = pack([dcc, g_mod, g_pre, g_post, g_q, g_k, g_ps])
    extra = pack([zero(d), g_modc, zero(d), zero(d), zero(HEAD_DIM), zero(HEAD_DIM), zero(n_groups * pg)])
    through_silu = _as_rows(jnp.concatenate([jnp.ones((d,), _F32), jnp.zeros((sum(sizes[1:]),), _F32)]))
    cat = lambda items: _as_rows(jnp.concatenate([a.reshape(-1) for a in items]))
    ws = [c_ctx, b_ada, norm_pre, norm_post, q_norm, k_norm, pool_scale]
    ms = [m_c_ctx, m_b_ada, m_norm_pre, m_norm_post, m_q_norm, m_k_norm, m_pool_scale]
    vs = [v_c_ctx, v_b_ada, v_norm_pre, v_norm_post, v_q_norm, v_k_norm, v_pool_scale]
    rep = _adam_replicated(parts, extra, through_silu, cat(ws), cat(ms), cat(vs), "adam_replicated")

    def split(packed):
        flat_, outs, at = packed.reshape(-1), [], 0
        for w, size in zip(ws, sizes):
            outs.append(flat_[at:at + size].reshape(w.shape))
            at += size
        return outs

    g_rep, dl_rep, nm_rep, nv_rep = [split(r) for r in rep]

    loss_sum = g_loss[0, 0]
    for dev in range(1, N_DEV):
        loss_sum = loss_sum + g_loss[dev, 0]
    loss = (0.5 / d) * loss_sum

    def ordered(rep_list, ada_, w_in_, pw_, w_out_):
        return [rep_list[0], ada_[None], rep_list[1], rep_list[2], rep_list[3], w_in_, rep_list[4], rep_list[5],
                pw_, rep_list[6], w_out_]

    return (loss, grad_x[None],
            *ordered(g_rep, g_wada, g_w_in, g_pw, g_w_out),
            *ordered(dl_rep, dl_wada, dl_w_in, dl_pw, dl_w_out),
            *ordered(nm_rep, nm_wada, nm_w_in, nm_pw, nm_w_out),
            *ordered(nv_rep, nv_wada, nv_w_in, nv_pw, nv_w_out))
```

```python
import jax
import jax.numpy as jnp
from jax import lax
from jax.experimental import pallas as pl
from jax.experimental.pallas import tpu as pltpu

HEAD_DIM = 128
GQA_GROUP = 4
ATTN_SUB_HEADS = 1
ATTN_BWD_SUB_HEADS = 1
ATTN_KEY_PARTS = 2
LOG2_E = 1.4426950408889634
GRID_W = 64
ROPE_PAIRS = HEAD_DIM // 4
ROPE_THETA = 10000.0
ATTN_SCALE = HEAD_DIM ** -0.5
EPS = 1e-6
POOL_WINDOWS = (2, 4, 8, 16)
POOL_HALO = 8
N_DEV = 8
N_CHIPS = 4
ADAM_LR = 0.001
ADAM_B1 = 0.9
ADAM_B2 = 0.999
ADAM_EPS = 1e-08
ADAM_WD = 0.01
ADAM_STEP = 10

LANES = 128
SUBLANES = 8
BF16_ROWS = 16

_MESH = pl.DeviceIdType.MESH
_ANY = pl.BlockSpec(memory_space=pl.ANY)
_VMEM = pl.BlockSpec(memory_space=pltpu.VMEM)
_HBM = pl.BlockSpec(memory_space=pltpu.HBM)
_SEM = pl.BlockSpec(memory_space=pltpu.SEMAPHORE)
_EFFECT = pltpu.SideEffectType.DATAFLOW_SIDE_EFFECTING
_F32 = jnp.float32
_BF16 = jnp.bfloat16


def _tile(dim, pref, align):
    t = min(pref, dim)
    t -= t % align
    while t >= align:
        if dim % t == 0:
            return t
        t -= align
    return dim


def _position():
    return lax.axis_index("x"), lax.axis_index("y"), lax.axis_index("c")


def _flip(v, bit):
    return 1 - v if bit else v


def _dev_index(x, y, c):
    return 4 * x + 2 * y + c


def _silu(g):
    return g * jax.nn.sigmoid(g)


def _silu_grad(g):
    s = jax.nn.sigmoid(g)
    return s * (1.0 + g * (1.0 - s))


def _adamw(w, g, m, v):
    m = ADAM_B1 * m + (1.0 - ADAM_B1) * g
    v = ADAM_B2 * v + (1.0 - ADAM_B2) * (g * g)
    m_hat = m / (1.0 - ADAM_B1 ** ADAM_STEP)
    v_hat = v / (1.0 - ADAM_B2 ** ADAM_STEP)
    delta = -ADAM_LR * (m_hat / (jnp.sqrt(v_hat) + ADAM_EPS) + ADAM_WD * w)
    return delta, m, v


def _all_gather_small(v, name):
    rows, cols = v.shape

    def body(v_ref, out_ref, send_sems, recv_sems):
        x, y, c = _position()
        me = _dev_index(x, y, c)
        out_ref[me] = v_ref[...]
        peers = [(_flip(x, k & 4), _flip(y, k & 2), _flip(c, k & 1)) for k in range(1, N_DEV)]

        def copy(k, block, to):
            return pltpu.make_async_remote_copy(
                src_ref=v_ref, dst_ref=out_ref.at[block], send_sem=send_sems.at[k], recv_sem=recv_sems.at[k],
                device_id=to, device_id_type=_MESH)

        sends = [copy(k, me, p) for k, p in enumerate(peers)]
        for s in sends:
            s.start()
        for k, p in enumerate(peers):
            copy(k, _dev_index(*p), p).wait_recv()
        for s in sends:
            s.wait_send()

    return pl.pallas_call(
        body, name=name,
        out_shape=jax.ShapeDtypeStruct((N_DEV, rows, cols), v.dtype),
        in_specs=[_VMEM], out_specs=_VMEM,
        scratch_shapes=[pltpu.SemaphoreType.DMA((N_DEV - 1,)), pltpu.SemaphoreType.DMA((N_DEV - 1,))],
    )(v)


def _route(x, y, c):
    first = (x + (1 - c) * (1 - 2 * x), y + c * (1 - 2 * y))
    second = (x + c * (1 - 2 * x), y + (1 - c) * (1 - 2 * y))
    return first, second, (1 - x, 1 - y)


def _w_in_order(x, y, c):
    first, second, diagonal = _route(x, y, c)
    return [(x, y, c), (x, y, 1 - c), (*first, c), (*second, 1 - c), (*second, c), (*first, 1 - c),
            (*diagonal, c), (*diagonal, 1 - c)]


W_IN_HOPS = {"a": 2, "b": 3, "c": 1, "d": 1}


def _w_in_hop_copies(group, wg, width, send_sems, recv_sems):
    x, y, c = _position()
    me, sibling = (x, y, c), (x, y, 1 - c)
    first, second, diagonal = _route(x, y, c)

    def cp(k, block, to):
        cols = wg.at[:, pl.ds(pl.multiple_of(_dev_index(*block) * width, width), width)]
        return pltpu.make_async_remote_copy(
            src_ref=cols, dst_ref=cols, send_sem=send_sems.at[k], recv_sem=recv_sems.at[k],
            device_id=to, device_id_type=_MESH)

    if group == "a":
        return [cp(0, me, sibling), cp(1, me, (*first, c))]
    if group == "b":
        return [cp(0, me, (*second, c)), cp(1, (*first, c), (*second, c)), cp(2, (*first, c), sibling)]
    if group == "c":
        return [cp(0, (*second, c), sibling)]
    return [cp(0, (*diagonal, c), sibling)]


def _w_in_hop(wg, width, waits, start, after, name):
    n_sem = 2 * len(waits)

    def body(*refs):
        wg_ref = refs[0]
        for i, (group, _, _, arrivals, sends) in enumerate(waits):
            cps = _w_in_hop_copies(group, wg_ref, width, refs[1 + 2 * i], refs[2 + 2 * i])
            for k in arrivals:
                cps[k].wait_recv()
            for k in sends:
                cps[k].wait_send()
        if start:
            for cp in _w_in_hop_copies(start, wg_ref, width, refs[n_sem + 2], refs[n_sem + 3]):
                cp.start()
        refs[-1][...] = jnp.zeros_like(refs[-1])

    sems = [s for w in waits for s in w[1:3]]
    new = [pltpu.SemaphoreType.DMA((W_IN_HOPS[start],))] * 2 if start else []
    outs = pl.pallas_call(
        body, name=name,
        out_shape=(*new, pltpu.HBM(wg.shape, wg.dtype), jax.ShapeDtypeStruct((SUBLANES, LANES), _F32)),
        in_specs=[_HBM] + [_SEM] * n_sem + [_ANY], out_specs=(*[_SEM] * len(new), _HBM, _VMEM),
        input_output_aliases={0: len(new)},
        compiler_params=pltpu.CompilerParams(has_side_effects=_EFFECT),
    )(pltpu.with_memory_space_constraint(wg, pltpu.HBM), *sems, after)
    return outs


def _slab_copies(bufs, send_sems, recv_sems):
    x, y, c = _position()
    me = _dev_index(x, y, c)
    peers = [(_flip(x, k & 4), _flip(y, k & 2), _flip(c, k & 1)) for k in range(1, N_DEV)]
    return [pltpu.make_async_remote_copy(
        src_ref=buf.at[me], dst_ref=buf.at[me],
        send_sem=send_sems.at[(N_DEV - 1) * a + k], recv_sem=recv_sems.at[(N_DEV - 1) * a + k],
        device_id=peer, device_id_type=_MESH)
        for a, buf in enumerate(bufs) for k, peer in enumerate(peers)]


def _gather_slabs_start(bufs, after, name):
    n = len(bufs)
    n_copies = (N_DEV - 1) * n

    def body(*refs):
        send_sems, recv_sems, token = refs[n + 1], refs[n + 2], refs[-1]
        for cp in _slab_copies(refs[:n], send_sems, recv_sems):
            cp.start()
        token[...] = jnp.zeros_like(token)

    outs = pl.pallas_call(
        body, name=name,
        out_shape=(pltpu.SemaphoreType.DMA((n_copies,)), pltpu.SemaphoreType.DMA((n_copies,)),
                   *[pltpu.HBM(b.shape, b.dtype) for b in bufs], jax.ShapeDtypeStruct((SUBLANES, LANES), _F32)),
        in_specs=[_HBM] * n + [_ANY], out_specs=(_SEM, _SEM, *[_HBM] * n, _VMEM),
        input_output_aliases={i: 2 + i for i in range(n)},
        compiler_params=pltpu.CompilerParams(has_side_effects=_EFFECT),
    )(*[pltpu.with_memory_space_constraint(b, pltpu.HBM) for b in bufs], after)
    return outs[0], outs[1], list(outs[2:2 + n]), outs[-1]


def _gather_slabs_wait(send_sems, recv_sems, bufs, after, name):
    n = len(bufs)

    def body(*refs):
        for cp in _slab_copies(refs[:n], refs[n], refs[n + 1]):
            cp.wait_send()
            cp.wait_recv()

    outs = pl.pallas_call(
        body, name=name, out_shape=tuple(pltpu.HBM(b.shape, b.dtype) for b in bufs),
        in_specs=[_HBM] * n + [_SEM, _SEM, _ANY], out_specs=[_HBM] * n,
        input_output_aliases={i: i for i in range(n)},
        compiler_params=pltpu.CompilerParams(has_side_effects=_EFFECT),
    )(*bufs, send_sems, recv_sems, after)
    return list(outs)


def _chip_order(x, y):
    return [(x, y), (1 - x, y), (x, 1 - y), (1 - x, 1 - y)]


def _chip_copies(srcs, lands, send_sems, recv_sems):
    x, y, c = _position()
    return [pltpu.make_async_remote_copy(
        src_ref=srcs[a].at[k], dst_ref=lands[a].at[k],
        send_sem=send_sems.at[(N_CHIPS - 1) * a + k], recv_sem=recv_sems.at[(N_CHIPS - 1) * a + k],
        device_id=(*chip, c), device_id_type=_MESH)
        for a in range(len(srcs)) for k, chip in enumerate(_chip_order(x, y)[1:])]


def _sibling_copies(srcs, lands, send_sems, recv_sems):
    x, y, c = _position()
    return [pltpu.make_async_remote_copy(
        src_ref=srcs[a].at[s], dst_ref=lands[a].at[s],
        send_sem=send_sems.at[N_CHIPS * a + s], recv_sem=recv_sems.at[N_CHIPS * a + s],
        device_id=(x, y, 1 - c), device_id_type=_MESH)
        for a in range(len(srcs)) for s in range(N_CHIPS)]


def _sibling_copies_by_device(srcs, lands, send_sems, recv_sems):
    x, y, c = _position()
    return [pltpu.make_async_remote_copy(
        src_ref=srcs[a].at[_dev_index(*chip, 1 - c)], dst_ref=lands[a].at[s],
        send_sem=send_sems.at[N_CHIPS * a + s], recv_sem=recv_sems.at[N_CHIPS * a + s],
        device_id=(x, y, 1 - c), device_id_type=_MESH)
        for a in range(len(srcs)) for s, chip in enumerate(_chip_order(x, y))]


def _exchange_start(copies, sums, name, land_slabs=None):
    n = len(sums)
    land_shapes = [((land_slabs or s.shape[0]),) + s.shape[1:] for s in sums]
    n_copies = sum(shape[0] for shape in land_shapes)

    def body(*refs):
        srcs, lands = refs[:n], refs[n:2 * n]
        send_sems, recv_sems, token = refs[2 * n], refs[2 * n + 1], refs[-1]
        for cp in copies(srcs, lands, send_sems, recv_sems):
            cp.start()
        token[...] = jnp.zeros_like(token)

    hbm = [pltpu.HBM(s.shape, s.dtype) for s in sums] + [pltpu.HBM(shape, s.dtype) for shape, s in zip(land_shapes, sums)]
    outs = pl.pallas_call(
        body, name=name,
        out_shape=(pltpu.SemaphoreType.DMA((n_copies,)), pltpu.SemaphoreType.DMA((n_copies,)), *hbm,
                   jax.ShapeDtypeStruct((SUBLANES, LANES), _F32)),
        in_specs=[_HBM] * (2 * n), out_specs=(_SEM, _SEM, *[_HBM] * (2 * n), _VMEM),
        input_output_aliases={i: 2 + i for i in range(2 * n)},
        compiler_params=pltpu.CompilerParams(has_side_effects=_EFFECT),
    )(*[pltpu.with_memory_space_constraint(s, pltpu.HBM) for s in sums],
      *[pltpu.with_memory_space_constraint(lax.empty(shape, s.dtype), pltpu.HBM) for shape, s in zip(land_shapes, sums)])
    return outs[0], outs[1], list(outs[2:2 + n]), list(outs[2 + n:2 + 2 * n]), outs[-1]


def _exchange_wait(copies, send_sems, recv_sems, srcs, lands, after, name, with_sources=False):
    n = len(srcs)

    def body(*refs):
        for cp in copies(refs[:n], refs[n:2 * n], refs[2 * n], refs[2 * n + 1]):
            cp.wait_send()
            cp.wait_recv()

    hbm = [pltpu.HBM(s.shape, s.dtype) for s in (*srcs, *lands)]
    outs = pl.pallas_call(
        body, name=name, out_shape=tuple(hbm),
        in_specs=[_HBM] * (2 * n) + [_SEM, _SEM, _ANY], out_specs=[_HBM] * (2 * n),
        input_output_aliases={i: i for i in range(2 * n)},
        compiler_params=pltpu.CompilerParams(has_side_effects=_EFFECT),
    )(*srcs, *lands, send_sems, recv_sems, after)
    return list(outs) if with_sources else list(outs[n:])


def _matmul(a, b, *, ta=False, tb=False, out_dtype=_F32, tm=1024, tn=1024, tk=4608, after=None, name):
    kdim, m = a.shape if ta else a.shape[::-1]
    n = b.shape[0] if tb else b.shape[1]
    tm = _tile(m, tm, LANES if ta else BF16_ROWS)
    tn = _tile(n, tn, LANES)
    tk = _tile(kdim, tk, BF16_ROWS if ta else LANES)
    nk = kdim // tk
    dims = (((0 if ta else 1,), (1 if tb else 0,)), ((), ()))

    def body_whole_k(a_ref, b_ref, *rest):
        rest[-1][...] = lax.dot_general(a_ref[...], b_ref[...], dims, preferred_element_type=_F32).astype(out_dtype)

    def body_split_k(a_ref, b_ref, *rest):
        o_ref, acc_ref = rest[-2:]
        k = pl.program_id(2)

        @pl.when(k == 0)
        def _():
            acc_ref[...] = jnp.zeros_like(acc_ref)

        acc_ref[...] += lax.dot_general(a_ref[...], b_ref[...], dims, preferred_element_type=_F32)

        @pl.when(k == nk - 1)
        def _():
            o_ref[...] = acc_ref[...].astype(out_dtype)

    a_spec = pl.BlockSpec((tk, tm), lambda i, j, k: (k, i)) if ta else pl.BlockSpec((tm, tk), lambda i, j, k: (i, k))
    b_spec = pl.BlockSpec((tn, tk), lambda i, j, k: (j, k)) if tb else pl.BlockSpec((tk, tn), lambda i, j, k: (k, j))
    extra = [] if after is None else [after]
    return pl.pallas_call(
        body_whole_k if nk == 1 else body_split_k, name=name, grid=(m // tm, n // tn, nk),
        in_specs=[a_spec, b_spec] + [pl.BlockSpec(t.shape, lambda i, j, k: (0, 0)) for t in extra],
        out_specs=pl.BlockSpec((tm, tn), lambda i, j, k: (i, j)), out_shape=jax.ShapeDtypeStruct((m, n), out_dtype),
        scratch_shapes=[] if nk == 1 else [pltpu.VMEM((tm, tn), _F32)],
        compiler_params=pltpu.CompilerParams(dimension_semantics=("parallel", "parallel", "arbitrary")),
    )(a, b, *extra)


def _proj_blocks(a, wg, dst, order_ids, first, count, width, after, name):
    m, kdim = a.shape
    tm = _tile(m, 1088, BF16_ROWS)

    def body(ids_ref, a_ref, w_ref, after_ref, dst_ref, o_ref):
        del ids_ref, after_ref, dst_ref
        o_ref[...] = jnp.dot(a_ref[...], w_ref[...], preferred_element_type=_F32)

    return pl.pallas_call(
        body, name=name,
        grid_spec=pltpu.PrefetchScalarGridSpec(
            num_scalar_prefetch=1, grid=(count, m // tm),
            in_specs=[pl.BlockSpec((tm, kdim), lambda j, i, ids: (i, 0)),
                      pl.BlockSpec((kdim, width), lambda j, i, ids: (0, ids[first + j])),
                      pl.BlockSpec(after.shape, lambda j, i, ids: (0, 0)), _ANY],
            out_specs=pl.BlockSpec((tm, width), lambda j, i, ids: (i, ids[first + j]))),
        out_shape=jax.ShapeDtypeStruct(dst.shape, dst.dtype),
        input_output_aliases={4: 0},
        compiler_params=pltpu.CompilerParams(dimension_semantics=("arbitrary", "arbitrary")),
    )(order_ids, a, wg, after, dst)


def _cast_into_columns(slab_ids, a, n_blocks, name):
    r, c = a.shape
    tr = _row_tile(r, c)

    def body(ids_ref, a_ref, o_ref):
        del ids_ref
        o_ref[...] = a_ref[...].astype(_BF16)

    return pl.pallas_call(
        body, name=name,
        grid_spec=pltpu.PrefetchScalarGridSpec(
            num_scalar_prefetch=1, grid=(r // tr,),
            in_specs=[pl.BlockSpec((tr, c), lambda i, ids: (i, 0))],
            out_specs=pl.BlockSpec((tr, c), lambda i, ids: (i, ids[0]))),
        out_shape=jax.ShapeDtypeStruct((r, n_blocks * c), _BF16),
        compiler_params=pltpu.CompilerParams(dimension_semantics=("parallel",)),
    )(slab_ids, a)


def _matmul_slabs(a, b, ids, width, name, after=None):
    kdim, m = a.shape
    n_slabs = ids.shape[0]
    tm = _tile(m, 1024, LANES)

    def body(ids_ref, a_ref, b_ref, *rest):
        del ids_ref
        rest[-1][0] = lax.dot_general(a_ref[...], b_ref[...], (((0,), (0,)), ((), ())), preferred_element_type=_F32)

    extra = [] if after is None else [after]
    return pl.pallas_call(
        body, name=name,
        grid_spec=pltpu.PrefetchScalarGridSpec(
            num_scalar_prefetch=1, grid=(m // tm, n_slabs),
            in_specs=[pl.BlockSpec((kdim, tm), lambda i, j, ids: (0, i)),
                      pl.BlockSpec((kdim, width), lambda i, j, ids: (0, ids[j]))]
            + [pl.BlockSpec(t.shape, lambda i, j, ids: (0, 0)) for t in extra],
            out_specs=pl.BlockSpec((1, tm, width), lambda i, j, ids: (j, i, 0))),
        out_shape=jax.ShapeDtypeStruct((n_slabs, m, width), _F32),
        compiler_params=pltpu.CompilerParams(dimension_semantics=("parallel", "parallel")),
    )(ids, a, b, *extra)


def _row_tile(rows, cols):
    return _tile(rows, max(BF16_ROWS, min(512, (1 << 19) // cols)), BF16_ROWS)


def _cast_into_slab(slab_ids, a, name):
    r, c = a.shape
    tr = _row_tile(r, c)

    def body(ids_ref, a_ref, o_ref):
        del ids_ref
        o_ref[0] = a_ref[...].astype(_BF16)

    return pl.pallas_call(
        body, name=name,
        grid_spec=pltpu.PrefetchScalarGridSpec(
            num_scalar_prefetch=1, grid=(r // tr,),
            in_specs=[pl.BlockSpec((tr, c), lambda i, ids: (i, 0))],
            out_specs=pl.BlockSpec((1, tr, c), lambda i, ids: (ids[0], i, 0))),
        out_shape=jax.ShapeDtypeStruct((N_DEV, r, c), _BF16),
        compiler_params=pltpu.CompilerParams(dimension_semantics=("parallel",)),
    )(slab_ids, a)


def _pre_add(slab_ids, grad, got, name):
    _, r, c = grad.shape
    tr = _row_tile(r, c)

    def body(ids_ref, a_ref, b_ref, o_ref):
        del ids_ref
        o_ref[...] = (a_ref[...] + b_ref[...]).astype(_BF16)

    return pl.pallas_call(
        body, name=name,
        grid_spec=pltpu.PrefetchScalarGridSpec(
            num_scalar_prefetch=1, grid=(N_CHIPS - 1, r // tr),
            in_specs=[pl.BlockSpec((1, tr, c), lambda s, i, ids: (ids[s + 1], i, 0)),
                      pl.BlockSpec((1, tr, c), lambda s, i, ids: (s + 1, i, 0))],
            out_specs=pl.BlockSpec((1, tr, c), lambda s, i, ids: (s, i, 0))),
        out_shape=jax.ShapeDtypeStruct((N_CHIPS - 1, r, c), _BF16),
        compiler_params=pltpu.CompilerParams(dimension_semantics=("parallel", "parallel")),
    )(slab_ids, grad, got)


def _ada_forward(craw, w_shard, name):
    d, cols = w_shard.shape
    tk = _tile(d, 512, LANES)

    def body(c_ref, w_ref, o_ref):
        @pl.when(pl.program_id(0) == 0)
        def _():
            o_ref[...] = jnp.zeros_like(o_ref)

        o_ref[...] += jnp.dot(_silu(c_ref[...]).astype(_BF16), w_ref[...].astype(_BF16), preferred_element_type=_F32)

    return pl.pallas_call(
        body, name=name, grid=(d // tk,),
        in_specs=[pl.BlockSpec((craw.shape[0], tk), lambda k: (0, k)), pl.BlockSpec((tk, cols), lambda k: (k, 0))],
        out_specs=pl.BlockSpec((craw.shape[0], cols), lambda k: (0, 0)),
        out_shape=jax.ShapeDtypeStruct((craw.shape[0], cols), _F32),
        compiler_params=pltpu.CompilerParams(dimension_semantics=("arbitrary",)),
    )(craw, w_shard)


def _ada_backward(craw, dmod, w, m, v, name):
    d, cols = w.shape
    rows = craw.shape[0]
    tr = _tile(d, 256, LANES)

    def body(c_ref, dm_ref, w_ref, m_ref, v_ref, g_ref, dl_ref, nm_ref, nv_ref, dc_ref):
        act = _silu(c_ref[...]).astype(_BF16)
        dmb = dm_ref[...].astype(_BF16)
        wv = w_ref[...]
        g = lax.dot_general(act, dmb, (((0,), (0,)), ((), ())), preferred_element_type=_F32)
        delta, nm, nv = _adamw(wv, g, m_ref[...], v_ref[...])
        g_ref[...] = g
        dl_ref[...] = delta
        nm_ref[...] = nm
        nv_ref[...] = nv
        dc = lax.dot_general(dmb, wv.astype(_BF16), (((1,), (1,)), ((), ())), preferred_element_type=_F32)
        dc_ref[...] = jnp.broadcast_to(jnp.sum(dc[N_DEV:], axis=0, keepdims=True), dc_ref.shape)

    blk = pl.BlockSpec((tr, cols), lambda i: (i, 0))
    return pl.pallas_call(
        body, name=name, grid=(d // tr,),
        in_specs=[pl.BlockSpec((rows, tr), lambda i: (0, i)), pl.BlockSpec((rows, cols), lambda i: (0, 0)), blk, blk, blk],
        out_specs=[blk, blk, blk, blk, pl.BlockSpec((SUBLANES, tr), lambda i: (0, i))],
        out_shape=[jax.ShapeDtypeStruct((d, cols), _F32)] * 4 + [jax.ShapeDtypeStruct((SUBLANES, d), _F32)],
        compiler_params=pltpu.CompilerParams(dimension_semantics=("parallel",)),
    )(craw, dmod, w, m, v)


def _rms(xf):
    return lax.rsqrt(jnp.mean(xf * xf, axis=-1, keepdims=True) + EPS)


def _head_mean(v):
    hi = v.astype(_BF16)
    lo = (v - hi.astype(_F32)).astype(_BF16)
    ones = jnp.full((2 * HEAD_DIM, HEAD_DIM), 1.0 / HEAD_DIM, _BF16)
    return jnp.dot(jnp.concatenate([hi, lo], axis=1), ones, preferred_element_type=_F32)


def _prenorm(ctx, x, g_pre, mods, tr, name):
    l, d = ctx.shape
    n = x.shape[0]
    nbl = n // tr

    def body(ctx_ref, x_ref, g_ref, mod_ref, h_ref):
        def emit(src_ref):
            xf = src_ref[...]
            y = (xf * _rms(xf)) * g_ref[...]
            h_ref[...] = (y * (1.0 + mod_ref[0, 0:1, :]) + mod_ref[0, 1:2, :]).astype(_BF16)

        is_latent = pl.program_id(0) < nbl
        pl.when(is_latent)(lambda: emit(x_ref))
        pl.when(jnp.logical_not(is_latent))(lambda: emit(ctx_ref))

    return pl.pallas_call(
        body, name=name, grid=((l + n) // tr,),
        in_specs=[pl.BlockSpec((tr, d), lambda i: (jnp.maximum(i - nbl, 0), 0)),
                  pl.BlockSpec((tr, d), lambda i: (jnp.minimum(i, nbl - 1), 0)),
                  pl.BlockSpec((1, d), lambda i: (0, 0)),
                  pl.BlockSpec((1, SUBLANES, d), lambda i: ((i < nbl).astype(jnp.int32), 0, 0))],
        out_specs=pl.BlockSpec((tr, d), lambda i: (i, 0)),
        out_shape=jax.ShapeDtypeStruct((l + n, d), _BF16),
        compiler_params=pltpu.CompilerParams(dimension_semantics=("arbitrary",)),
    )(ctx, x, g_pre, mods)


def _prenorm_backward(dh, ctx, x, dxn, g_pre, mods, tr, name):
    l, d = ctx.shape
    n = x.shape[0]
    nbl = n // tr

    def body(dh_ref, ctx_ref, x_ref, dxn_ref, g_ref, mod_ref, gx_ref, dmod_ref, dg_ref):
        i = pl.program_id(0)

        @pl.when(i == 0)
        def _():
            dg_ref[...] = jnp.zeros_like(dg_ref)

        @pl.when(jnp.logical_or(i == 0, i == nbl))
        def _():
            dmod_ref[...] = jnp.zeros_like(dmod_ref)

        def emit(src_ref, latent):
            xf = src_ref[...]
            r = _rms(xf)
            xn = xf * r
            dhv = dh_ref[...]
            one_scale = 1.0 + mod_ref[0, 0:1, :]
            dmod_ref[0, 0:1, :] += jnp.sum(dhv * (xn * g_ref[...]), axis=0, keepdims=True)
            dmod_ref[0, 1:2, :] += jnp.sum(dhv, axis=0, keepdims=True)
            dyg = dhv * one_scale
            dg_ref[0:1, :] += jnp.sum(dyg * xn, axis=0, keepdims=True)
            if latent:
                dn = dyg * g_ref[...]
                gx_ref[...] = dxn_ref[...] + r * (dn - xn * jnp.mean(dn * xn, axis=-1, keepdims=True))

        pl.when(i < nbl)(lambda: emit(x_ref, True))
        pl.when(i >= nbl)(lambda: emit(ctx_ref, False))

    lat = pl.BlockSpec((tr, d), lambda i: (jnp.minimum(i, nbl - 1), 0))
    sel = pl.BlockSpec((1, SUBLANES, d), lambda i: ((i < nbl).astype(jnp.int32), 0, 0))
    return pl.pallas_call(
        body, name=name, grid=((l + n) // tr,),
        in_specs=[pl.BlockSpec((tr, d), lambda i: (i, 0)),
                  pl.BlockSpec((tr, d), lambda i: (jnp.maximum(i - nbl, 0), 0)),
                  lat, lat, pl.BlockSpec((1, d), lambda i: (0, 0)), sel],
        out_specs=[lat, sel, pl.BlockSpec((SUBLANES, d), lambda i: (0, 0))],
        out_shape=[jax.ShapeDtypeStruct((n, d), _F32), jax.ShapeDtypeStruct((2, SUBLANES, d), _F32),
                   jax.ShapeDtypeStruct((SUBLANES, d), _F32)],
        compiler_params=pltpu.CompilerParams(dimension_semantics=("arbitrary",)),
    )(dh, ctx, x, dxn, g_pre, mods)


def _rope_tables(l, n):
    rows = n // GRID_W
    inv = ROPE_THETA ** (-jnp.arange(ROPE_PAIRS, dtype=_F32) / ROPE_PAIRS)
    ang_r = jnp.arange(rows, dtype=_F32)[:, None] * inv
    ang_c = jnp.arange(GRID_W, dtype=_F32)[:, None] * inv
    per_row = lambda a: jnp.repeat(a, GRID_W, axis=0)
    per_col = lambda a: jnp.tile(a, (rows, 1))
    cr, sr, cc, sc = per_row(jnp.cos(ang_r)), per_row(jnp.sin(ang_r)), per_col(jnp.cos(ang_c)), per_col(jnp.sin(ang_c))
    zero = jnp.zeros_like(sr)
    tc = jnp.concatenate([cr, cr, cc, cc], axis=-1)
    ta = jnp.concatenate([-sr, zero, -sc, zero], axis=-1)
    tb = jnp.concatenate([zero, sr, zero, sc], axis=-1)
    pad = lambda t, fill: jnp.concatenate([t, jnp.full((l, HEAD_DIM), fill, _F32)], axis=0)
    return pad(tc, 1.0), pad(ta, 0.0), pad(tb, 0.0)


def _rope(y, tc, ta, tb):
    return y * tc + pltpu.roll(y, HEAD_DIM - ROPE_PAIRS, 1) * ta + pltpu.roll(y, ROPE_PAIRS, 1) * tb


def _rope_transposed(dy, tc, ta, tb):
    return dy * tc + pltpu.roll(dy * ta, ROPE_PAIRS, 1) + pltpu.roll(dy * tb, HEAD_DIM - ROPE_PAIRS, 1)


def _qkv_post(proj, tables, g_q, g_k, heads, kv_heads, tr, name):
    t = proj.shape[0]
    aw, kw = heads * HEAD_DIM, kv_heads * HEAD_DIM
    w3 = aw + 2 * kw

    def body(p_ref, tc_ref, ta_ref, tb_ref, gq_ref, gk_ref, q_ref, k_ref, v_ref):
        tabs = (tc_ref[...], ta_ref[...], tb_ref[...])

        def norm_rope(col, gain):
            xh = p_ref[:, col:col + HEAD_DIM]
            return _rope((xh * lax.rsqrt(_head_mean(xh * xh) + EPS)) * gain, *tabs).astype(_BF16)

        for h in range(heads):
            q_ref[h] = norm_rope(h * HEAD_DIM, gq_ref[...])
        for h in range(kv_heads):
            k_ref[h] = norm_rope(aw + h * HEAD_DIM, gk_ref[...])
            v_ref[h] = p_ref[:, aw + kw + h * HEAD_DIM:aw + kw + (h + 1) * HEAD_DIM].astype(_BF16)

    tab = pl.BlockSpec((tr, HEAD_DIM), lambda i: (i, 0))
    gain = pl.BlockSpec((1, HEAD_DIM), lambda i: (0, 0))
    return pl.pallas_call(
        body, name=name, grid=(t // tr,),
        in_specs=[pl.BlockSpec((tr, w3), lambda i: (i, 0)), tab, tab, tab, gain, gain],
        out_specs=[pl.BlockSpec((heads, tr, HEAD_DIM), lambda i: (0, i, 0)),
                   pl.BlockSpec((kv_heads, tr, HEAD_DIM), lambda i: (0, i, 0)),
                   pl.BlockSpec((kv_heads, tr, HEAD_DIM), lambda i: (0, i, 0))],
        out_shape=[jax.ShapeDtypeStruct((heads, t, HEAD_DIM), _BF16),
                   jax.ShapeDtypeStruct((kv_heads, t, HEAD_DIM), _BF16),
                   jax.ShapeDtypeStruct((kv_heads, t, HEAD_DIM), _BF16)],
        compiler_params=pltpu.CompilerParams(dimension_semantics=("parallel",)),
    )(proj, *tables, g_q, g_k)


def _qkv_post_backward(proj, dq, dk, dv, tables, g_q, g_k, dproj, l, tr, name):
    t = proj.shape[0]
    heads, kv_heads = dq.shape[0], dk.shape[0]
    aw, kw = heads * HEAD_DIM, kv_heads * HEAD_DIM
    w3 = aw + 2 * kw
    nbl = (t - l) // tr

    def body(p_ref, dq_ref, dk_ref, dv_ref, tc_ref, ta_ref, tb_ref, gq_ref, gk_ref, dproj_ref, o_ref, dgq_ref, dgk_ref):
        del dproj_ref
        i = pl.program_id(0)

        @pl.when(i == 0)
        def _():
            dgq_ref[...] = jnp.zeros_like(dgq_ref)
            dgk_ref[...] = jnp.zeros_like(dgk_ref)

        tabs = (tc_ref[...], ta_ref[...], tb_ref[...])
        latent = i < nbl

        def back(col, dout, gain, dg_ref):
            xh = p_ref[:, col:col + HEAD_DIM]
            r = lax.rsqrt(_head_mean(xh * xh) + EPS)
            xn = xh * r
            dy = _rope_transposed(dout, *tabs)
            dg_ref[0:1, :] += jnp.sum(dy * xn, axis=0, keepdims=True)
            dn = dy * gain
            o_ref[:, col:col + HEAD_DIM] = (r * (dn - xn * _head_mean(dn * xn))).astype(_BF16)

        for h in range(heads):
            back(h * HEAD_DIM, jnp.where(latent, dq_ref[h], 0.0), gq_ref[...], dgq_ref)
        for h in range(kv_heads):
            back(aw + h * HEAD_DIM, dk_ref[h], gk_ref[...], dgk_ref)
            o_ref[:, aw + kw + h * HEAD_DIM:aw + kw + (h + 1) * HEAD_DIM] = dv_ref[h].astype(_BF16)

    tab = pl.BlockSpec((tr, HEAD_DIM), lambda i: (i, 0))
    gain = pl.BlockSpec((1, HEAD_DIM), lambda i: (0, 0))
    acc = pl.BlockSpec((SUBLANES, HEAD_DIM), lambda i: (0, 0))
    return pl.pallas_call(
        body, name=name, grid=(t // tr,),
        in_specs=[pl.BlockSpec((tr, w3), lambda i: (i, 0)),
                  pl.BlockSpec((heads, tr, HEAD_DIM), lambda i: (0, jnp.minimum(i, nbl - 1), 0)),
                  pl.BlockSpec((kv_heads, tr, HEAD_DIM), lambda i: (0, i, 0)),
                  pl.BlockSpec((kv_heads, tr, HEAD_DIM), lambda i: (0, i, 0)),
                  tab, tab, tab, gain, gain, _ANY],
        out_specs=[pl.BlockSpec((tr, w3), lambda i: (i, 0)), acc, acc],
        out_shape=[jax.ShapeDtypeStruct(dproj.shape, dproj.dtype), jax.ShapeDtypeStruct((SUBLANES, HEAD_DIM), _F32),
                   jax.ShapeDtypeStruct((SUBLANES, HEAD_DIM), _F32)],
        input_output_aliases={9: 0},
        compiler_params=pltpu.CompilerParams(dimension_semantics=("arbitrary",)),
    )(proj, dq, dk, dv, *tables, g_q, g_k, dproj)


def _zero_context_rows(dproj, l, w3, tr, name):
    t, iw = dproj.shape
    first = (t - l) // tr

    def body(dproj_ref, o_ref):
        del dproj_ref
        o_ref[...] = jnp.zeros_like(o_ref)

    return pl.pallas_call(
        body, name=name, grid=(l // tr, iw // w3 - 1),
        in_specs=[_ANY], out_specs=pl.BlockSpec((tr, w3), lambda i, j: (first + i, j + 1)),
        out_shape=jax.ShapeDtypeStruct(dproj.shape, dproj.dtype), input_output_aliases={0: 0},
        compiler_params=pltpu.CompilerParams(dimension_semantics=("parallel", "parallel")),
    )(dproj)


def _attention(q, k, v, proj, l, mix, tq, name):
    heads, t, _ = q.shape
    kv_heads = k.shape[0]
    n = t - l
    gw = GQA_GROUP * HEAD_DIM
    aw = heads * HEAD_DIM
    gate_col = (aw + 2 * kv_heads * HEAD_DIM) // gw

    def body(q_ref, k_ref, v_ref, g_ref, o_ref, y_ref, lse_ref):
        lane = lax.broadcasted_iota(jnp.int32, (tq, LANES), 1)
        lse_blk = jnp.zeros((tq, LANES), _F32)
        firsts = list(range(0, GQA_GROUP, ATTN_SUB_HEADS))

        def scores(first):
            qs = q_ref[first:first + ATTN_SUB_HEADS].reshape(ATTN_SUB_HEADS * tq, HEAD_DIM)
            return lax.dot_general(qs, k_ref[0], (((1,), (1,)), ((), ())), preferred_element_type=_F32)

        raw_next = scores(firsts[0])
        for idx, first in enumerate(firsts):
            raw = raw_next
            if idx + 1 < len(firsts):
                raw_next = scores(firsts[idx + 1])
            m = jnp.max(raw, axis=-1, keepdims=True)
            p = jnp.exp2((raw - m) * (ATTN_SCALE * LOG2_E))
            denom = jnp.sum(p, axis=-1, keepdims=True)
            os_ = jnp.dot(p.astype(_BF16), v_ref[0], preferred_element_type=_F32) / denom
            lse_s = m * ATTN_SCALE + jnp.log(denom)
            for j in range(ATTN_SUB_HEADS):
                g = first + j
                og = os_[j * tq:(j + 1) * tq]
                cols = slice(g * HEAD_DIM, (g + 1) * HEAD_DIM)
                o_ref[:, cols] = og
                y_ref[:, cols] = (og * _silu(g_ref[:, cols])).astype(_BF16)
                lse_blk = jnp.where(lane == g, lse_s[j * tq:(j + 1) * tq], lse_blk)
        lse_ref[0] = lse_blk

    return pl.pallas_call(
        body, name=name, grid=(kv_heads, n // tq),
        in_specs=[pl.BlockSpec((GQA_GROUP, tq, HEAD_DIM), lambda h, i: (h, i, 0)),
                  pl.BlockSpec((1, t, HEAD_DIM), lambda h, i: (h, 0, 0)),
                  pl.BlockSpec((1, t, HEAD_DIM), lambda h, i: (h, 0, 0)),
                  pl.BlockSpec((tq, gw), lambda h, i: (i, gate_col + h))],
        out_specs=[pl.BlockSpec((tq, gw), lambda h, i: (i, h)),
                   pl.BlockSpec((tq, gw), lambda h, i: (i, h)),
                   pl.BlockSpec((1, tq, LANES), lambda h, i: (h, i, 0))],
        out_shape=[jax.ShapeDtypeStruct((n, aw), _F32), jax.ShapeDtypeStruct((n, mix), _BF16),
                   jax.ShapeDtypeStruct((kv_heads, n, LANES), _F32)],
        compiler_params=pltpu.CompilerParams(dimension_semantics=("parallel", "parallel")),
    )(q, k, v, proj)


def _attention_backward(q, k, v, attn_o, dy, proj, lse, after, dproj, l, tq, name):
    heads, t, _ = q.shape
    kv_heads = k.shape[0]
    n = t - l
    gw = GQA_GROUP * HEAD_DIM
    aw = heads * HEAD_DIM
    gate_col = (aw + 2 * kv_heads * HEAD_DIM) // gw
    n_parts = next(p for p in (ATTN_KEY_PARTS, 2, 1) if t % (p * BF16_ROWS) == 0)
    part = t // n_parts

    def body(q_ref, k_ref, v_ref, o_ref, dy_ref, g_ref, lse_ref, after_ref, dproj_ref, dq_ref, dg_ref, dk_ref, dv_ref):
        del after_ref, dproj_ref

        @pl.when(pl.program_id(1) == 0)
        def _():
            dk_ref[...] = jnp.zeros_like(dk_ref)
            dv_ref[...] = jnp.zeros_like(dv_ref)

        lse_blk = lse_ref[0]
        for first in range(0, GQA_GROUP, ATTN_BWD_SUB_HEADS):
            qs = q_ref[first:first + ATTN_BWD_SUB_HEADS].reshape(ATTN_BWD_SUB_HEADS * tq, HEAD_DIM)
            do_parts, delta_parts, lse_parts = [], [], []
            for g in range(first, first + ATTN_BWD_SUB_HEADS):
                cols = slice(g * HEAD_DIM, (g + 1) * HEAD_DIM)
                gate, og, dyg = g_ref[:, cols], o_ref[:, cols], dy_ref[:, cols]
                dog = dyg * _silu(gate)
                dg_ref[:, cols] = (dyg * og * _silu_grad(gate)).astype(_BF16)
                do_parts.append(dog)
                delta_parts.append(jnp.sum(dog * og, axis=-1, keepdims=True))
                lse_parts.append(lse_blk[:, g:g + 1])
            dos = jnp.concatenate(do_parts, axis=0).astype(_BF16)
            delta = jnp.concatenate(delta_parts, axis=0)
            lse2 = jnp.concatenate(lse_parts, axis=0) * LOG2_E
            dqs = jnp.zeros((ATTN_BWD_SUB_HEADS * tq, HEAD_DIM), _F32)
            for part_i in range(n_parts):
                keys = slice(part_i * part, (part_i + 1) * part)
                ks, vs = k_ref[0, keys, :], v_ref[0, keys, :]
                raw = lax.dot_general(qs, ks, (((1,), (1,)), ((), ())), preferred_element_type=_F32)
                p = jnp.exp2(raw * (ATTN_SCALE * LOG2_E) - lse2)
                dp = lax.dot_general(dos, vs, (((1,), (1,)), ((), ())), preferred_element_type=_F32)
                ds = (p * (dp - delta)).astype(_BF16)
                dqs = dqs + jnp.dot(ds, ks, preferred_element_type=_F32)
                dk_ref[0, keys, :] += ATTN_SCALE * lax.dot_general(
                    ds, qs, (((0,), (0,)), ((), ())), preferred_element_type=_F32)
                dv_ref[0, keys, :] += lax.dot_general(
                    p.astype(_BF16), dos, (((0,), (0,)), ((), ())), preferred_element_type=_F32)
            dq_ref[first:first + ATTN_BWD_SUB_HEADS] = (ATTN_SCALE * dqs).reshape(ATTN_BWD_SUB_HEADS, tq, HEAD_DIM)

    kv_spec = pl.BlockSpec((1, t, HEAD_DIM), lambda h, i: (h, 0, 0))
    tok = pl.BlockSpec((tq, gw), lambda h, i: (i, h))
    gate = pl.BlockSpec((tq, gw), lambda h, i: (i, gate_col + h))
    return pl.pallas_call(
        body, name=name, grid=(kv_heads, n // tq),
        in_specs=[pl.BlockSpec((GQA_GROUP, tq, HEAD_DIM), lambda h, i: (h, i, 0)), kv_spec, kv_spec,
                  tok, tok, gate, pl.BlockSpec((1, tq, LANES), lambda h, i: (h, i, 0)),
                  pl.BlockSpec(after.shape, lambda h, i: (0, 0)), _ANY],
        out_specs=[pl.BlockSpec((GQA_GROUP, tq, HEAD_DIM), lambda h, i: (h, i, 0)), gate, kv_spec, kv_spec],
        out_shape=[jax.ShapeDtypeStruct((heads, n, HEAD_DIM), _F32), jax.ShapeDtypeStruct(dproj.shape, dproj.dtype),
                   jax.ShapeDtypeStruct((kv_heads, t, HEAD_DIM), _F32), jax.ShapeDtypeStruct((kv_heads, t, HEAD_DIM), _F32)],
        input_output_aliases={8: 1},
        compiler_params=pltpu.CompilerParams(dimension_semantics=("parallel", "arbitrary")),
    )(q, k, v, attn_o, dy, proj, lse, after, dproj)


def _halo_specs(tp, width, col, row_off, total_rows):
    per = tp // POOL_HALO
    first = row_off // POOL_HALO
    last = total_rows // POOL_HALO - 1
    return [pl.BlockSpec((tp, width), lambda i: (i + row_off // tp, col)),
            pl.BlockSpec((POOL_HALO, width), lambda i: (jnp.maximum(first + i * per - 1, 0), col)),
            pl.BlockSpec((POOL_HALO, width), lambda i: (jnp.minimum(first + (i + 1) * per, last), col))]


def _with_halo(cur, prev, nxt, t0, n):
    tp = cur.shape[0]
    r8 = lax.broadcasted_iota(jnp.int32, (POOL_HALO, 1), 0)
    prev = jnp.where(t0 - POOL_HALO + r8 >= 0, prev, 0.0)
    nxt = jnp.where(t0 + tp + r8 < n, nxt, 0.0)
    return jnp.concatenate([prev, cur, nxt], axis=0)


def _shift_rows(a, s):
    return pltpu.roll(a, s % a.shape[0], 0)


def _window_sum(e, w, mirrored):
    a = e + _shift_rows(e, -1 if mirrored else 1)
    s = 1
    while 2 * s < w:
        a = _shift_rows(a, s) + _shift_rows(a, -s)
        s *= 2
    return a


def _window_count(t, w, n):
    half = w // 2
    return (jnp.minimum(t + half, n) - jnp.maximum(t - half, 0)).astype(_F32)


def _pool_forward(gi, proj, y, pool_w, pool_scale, l, heads, kv_heads, tp, name):
    t = proj.shape[0]
    n = t - l
    pg = pool_w.shape[-1]
    w = POOL_WINDOWS[gi]
    aw, kw = heads * HEAD_DIM, kv_heads * HEAD_DIM
    u_col = (2 * aw + 2 * kw) // pg + gi
    gate_col = (2 * aw + 2 * kw + len(POOL_WINDOWS) * pg) // pg + gi

    def body(u_ref, up_ref, un_ref, g_ref, w_ref, sc_ref, y_in_ref, y_ref, raw_ref, d_ref):
        del y_in_ref
        t0 = pl.program_id(0) * tp
        cur = u_ref[...]
        win = _window_sum(_with_halo(cur, up_ref[...], un_ref[...], t0, n), w, False)[POOL_HALO:POOL_HALO + tp]
        tok = t0 + lax.broadcasted_iota(jnp.int32, (tp, 1), 0)
        d = (win / _window_count(tok, w, n) - cur).astype(_BF16)
        raw = jnp.dot(d, w_ref[...].reshape(pg, pg), preferred_element_type=_F32)
        d_ref[...] = d
        raw_ref[...] = raw
        y_ref[...] = ((raw * sc_ref[...]) * _silu(g_ref[...])).astype(_BF16)

    blk = pl.BlockSpec((tp, pg), lambda i: (i, 0))
    return pl.pallas_call(
        body, name=name, grid=(n // tp,),
        in_specs=_halo_specs(tp, pg, u_col, 0, t) + [
            pl.BlockSpec((tp, pg), lambda i: (i, gate_col)),
            pl.BlockSpec((N_DEV, 1, pg // N_DEV, pg), lambda i: (0, gi, 0, 0)),
            pl.BlockSpec((1, pg), lambda i: (0, gi)), _ANY],
        out_specs=[pl.BlockSpec((tp, pg), lambda i: (i, aw // pg + gi)), blk, blk],
        out_shape=[jax.ShapeDtypeStruct(y.shape, y.dtype), jax.ShapeDtypeStruct((n, pg), _F32),
                   jax.ShapeDtypeStruct((n, pg), _BF16)],
        input_output_aliases={6: 0},
        compiler_params=pltpu.CompilerParams(dimension_semantics=("arbitrary",)),
    )(proj, proj, proj, proj, pool_w, pool_scale, y)


def _pool_backward_gate(gi, dy, proj, raw, pool_w, pool_scale, dproj, l, heads, kv_heads, tp, name):
    n, pg = raw.shape
    aw, kw = heads * HEAD_DIM, kv_heads * HEAD_DIM
    gate_col = (2 * aw + 2 * kw + len(POOL_WINDOWS) * pg) // pg + gi

    def body(dy_ref, g_ref, raw_ref, w_ref, sc_ref, dproj_ref, dg_ref, dr_ref, dd_ref, ds_ref):
        del dproj_ref

        @pl.when(pl.program_id(0) == 0)
        def _():
            ds_ref[...] = jnp.zeros_like(ds_ref)

        gate, rawv, dyv, scale = g_ref[...], raw_ref[...], dy_ref[...], sc_ref[...]
        dpool = dyv * _silu(gate)
        dg_ref[...] = (dyv * (rawv * scale) * _silu_grad(gate)).astype(_BF16)
        ds_ref[0:1, :] += jnp.sum(dpool * rawv, axis=0, keepdims=True)
        draw = (dpool * scale).astype(_BF16)
        dr_ref[...] = draw
        dd_ref[...] = lax.dot_general(
            draw, w_ref[...].reshape(pg, pg), (((1,), (1,)), ((), ())), preferred_element_type=_F32)

    blk = pl.BlockSpec((tp, pg), lambda i: (i, 0))
    gate = pl.BlockSpec((tp, pg), lambda i: (i, gate_col))
    return pl.pallas_call(
        body, name=name, grid=(n // tp,),
        in_specs=[pl.BlockSpec((tp, pg), lambda i: (i, aw // pg + gi)), gate, blk,
                  pl.BlockSpec((N_DEV, 1, pg // N_DEV, pg), lambda i: (0, gi, 0, 0)),
                  pl.BlockSpec((1, pg), lambda i: (0, gi)), _ANY],
        out_specs=[gate, blk, blk, pl.BlockSpec((SUBLANES, pg), lambda i: (0, 0))],
        out_shape=[jax.ShapeDtypeStruct(dproj.shape, dproj.dtype), jax.ShapeDtypeStruct((n, pg), _BF16),
                   jax.ShapeDtypeStruct((n, pg), _F32), jax.ShapeDtypeStruct((SUBLANES, pg), _F32)],
        input_output_aliases={5: 0},
        compiler_params=pltpu.CompilerParams(dimension_semantics=("arbitrary",)),
    )(dy, proj, raw, pool_w, pool_scale, dproj)


def _pool_backward_window(gi, dd, dproj, l, col, tp, name):
    n, pg = dd.shape
    w = POOL_WINDOWS[gi]

    def body(c_ref, p_ref, n_ref, dproj_ref, du_ref):
        del dproj_ref
        t0 = pl.program_id(0) * tp
        cur = c_ref[...]
        e = _with_halo(cur, p_ref[...], n_ref[...], t0, n)
        tok = t0 - POOL_HALO + lax.broadcasted_iota(jnp.int32, (tp + 2 * POOL_HALO, 1), 0)
        e = e / jnp.maximum(_window_count(tok, w, n), 1.0)
        du_ref[...] = (_window_sum(e, w, True)[POOL_HALO:POOL_HALO + tp] - cur).astype(_BF16)

    return pl.pallas_call(
        body, name=name, grid=(n // tp,),
        in_specs=_halo_specs(tp, pg, 0, 0, n) + [_ANY],
        out_specs=pl.BlockSpec((tp, pg), lambda i: (i, col)),
        out_shape=jax.ShapeDtypeStruct(dproj.shape, dproj.dtype), input_output_aliases={3: 0},
        compiler_params=pltpu.CompilerParams(dimension_semantics=("arbitrary",)),
    )(dd, dd, dd, dproj)


def _post(out, x, target, gate, g_post, tr, name):
    n, d = out.shape

    def body(o_ref, x_ref, t_ref, gate_ref, g_ref, dxn_ref, do_ref, dgate_ref, dg_ref, loss_ref):
        @pl.when(pl.program_id(0) == 0)
        def _():
            dgate_ref[...] = jnp.zeros_like(dgate_ref)
            dg_ref[...] = jnp.zeros_like(dg_ref)
            loss_ref[...] = jnp.zeros_like(loss_ref)

        ov = o_ref[...]
        r = _rms(ov)
        on = ov * r
        normed = on * g_ref[...]
        err = (x_ref[...] + gate_ref[...] * normed) - t_ref[...]
        loss_ref[...] += jnp.sum(err * err)
        dxn = err / d
        dxn_ref[...] = dxn
        dgate_ref[0:1, :] += jnp.sum(dxn * normed, axis=0, keepdims=True)
        dr = dxn * gate_ref[...]
        dg_ref[0:1, :] += jnp.sum(dr * on, axis=0, keepdims=True)
        dn = dr * g_ref[...]
        do_ref[...] = (r * (dn - on * jnp.mean(dn * on, axis=-1, keepdims=True))).astype(_BF16)

    blk = pl.BlockSpec((tr, d), lambda i: (i, 0))
    vec = pl.BlockSpec((1, d), lambda i: (0, 0))
    acc = pl.BlockSpec((SUBLANES, d), lambda i: (0, 0))
    return pl.pallas_call(
        body, name=name, grid=(n // tr,),
        in_specs=[blk, blk, blk, vec, vec],
        out_specs=[blk, blk, acc, acc, pl.BlockSpec((SUBLANES, LANES), lambda i: (0, 0))],
        out_shape=[jax.ShapeDtypeStruct((n, d), _F32), jax.ShapeDtypeStruct((n, d), _BF16),
                   jax.ShapeDtypeStruct((SUBLANES, d), _F32), jax.ShapeDtypeStruct((SUBLANES, d), _F32),
                   jax.ShapeDtypeStruct((SUBLANES, LANES), _F32)],
        compiler_params=pltpu.CompilerParams(dimension_semantics=("arbitrary",)),
    )(out, x, target, gate, g_post)


def _adam_sharded(slab_ids, grad, got, far, w, m, v, name):
    r, c = w.shape
    tr = _tile(r, max(BF16_ROWS, min(256, (1 << 18) // c)), BF16_ROWS)

    def body(ids_ref, own_ref, got_ref, far_ref, w_ref, m_ref, v_ref, g_ref, dl_ref, nm_ref, nv_ref):
        del ids_ref
        g = own_ref[0] + got_ref[0]
        for k in range(N_CHIPS - 1):
            g = g + far_ref[k].astype(_F32)
        delta, nm, nv = _adamw(w_ref[...], g, m_ref[...], v_ref[...])
        g_ref[...] = g
        dl_ref[...] = delta
        nm_ref[...] = nm
        nv_ref[...] = nv

    blk = pl.BlockSpec((tr, c), lambda i, ids: (i, 0))
    return pl.pallas_call(
        body, name=name,
        grid_spec=pltpu.PrefetchScalarGridSpec(
            num_scalar_prefetch=1, grid=(r // tr,),
            in_specs=[pl.BlockSpec((1, tr, c), lambda i, ids: (ids[0], i, 0)),
                      pl.BlockSpec((1, tr, c), lambda i, ids: (0, i, 0)),
                      pl.BlockSpec((N_CHIPS - 1, tr, c), lambda i, ids: (0, i, 0)), blk, blk, blk],
            out_specs=[blk] * 4),
        out_shape=[jax.ShapeDtypeStruct((r, c), _F32)] * 4,
        compiler_params=pltpu.CompilerParams(dimension_semantics=("parallel",)),
    )(slab_ids, grad, got, far, w, m, v)


def _adam_replicated(parts, extra, through_silu, w, m, v, name):
    def body(p_ref, e_ref, s_ref, w_ref, m_ref, v_ref, g_ref, dl_ref, nm_ref, nv_ref):
        total = p_ref[0] + e_ref[0]
        for dev in range(1, N_DEV):
            total = total + (p_ref[dev] + e_ref[dev])
        g = jnp.where(s_ref[...] > 0.5, total * _silu_grad(w_ref[...]), total)
        delta, nm, nv = _adamw(w_ref[...], g, m_ref[...], v_ref[...])
        g_ref[...] = g
        dl_ref[...] = delta
        nm_ref[...] = nm
        nv_ref[...] = nv

    return pl.pallas_call(
        body, name=name, in_specs=[_VMEM] * 6, out_specs=[_VMEM] * 4,
        out_shape=[jax.ShapeDtypeStruct(w.shape, _F32)] * 4,
    )(parts, extra, through_silu, w, m, v)


def _as_rows(vec):
    size = vec.shape[0]
    padded = -(-size // (SUBLANES * LANES)) * SUBLANES * LANES
    return jnp.pad(vec, (0, padded - size)).reshape(padded // LANES, LANES)


def kernel(x, c, ctx, c_ctx, w_ada, b_ada, norm_pre, norm_post, w_in, q_norm, k_norm, pool_w, pool_scale, w_out, loss_target, m_c_ctx, m_w_ada, m_b_ada, m_norm_pre, m_norm_post, m_w_in, m_q_norm, m_k_norm, m_pool_w, m_pool_scale, m_w_out, v_c_ctx, v_w_ada, v_b_ada, v_norm_pre, v_norm_post, v_w_in, v_q_norm, v_k_norm, v_pool_w, v_pool_scale, v_w_out):
    me = _dev_index(*_position())
    x2, ctx2, target = x[0], ctx[0], loss_target[0]
    n, d = x2.shape
    l = ctx2.shape[0]
    t = l + n
    aw = d // 2
    heads = aw // HEAD_DIM
    kv_heads = heads // GQA_GROUP
    kw = kv_heads * HEAD_DIM
    n_groups = len(POOL_WINDOWS)
    pg = (d - aw) // n_groups
    mix = d
    tr = _tile(l, 128, BF16_ROWS)
    tr2 = _tile(l, 256, BF16_ROWS)
    tq = _tile(l, 128, BF16_ROWS)
    tp = _tile(n, 2048, POOL_HALO)

    xi, yi, ci = _position()
    slab_ids = jnp.stack([_dev_index(*chip, ci) for chip in _chip_order(xi, yi)]).astype(jnp.int32)

    cw = w_in.shape[-1]
    wg = _cast_into_columns(slab_ids, w_in[0], N_DEV, "cast_w_in")
    late = [_cast_into_slab(slab_ids, w_out[0], "cast_w_out"),
            _cast_into_slab(slab_ids, pool_w[0].reshape(-1, pg), "cast_pool_w")]

    c_all = _all_gather_small(_as_rows(c[0]), "gather_c").reshape(N_DEV, -1)[:, :d]
    craw = jnp.concatenate([c_all, jnp.broadcast_to(c_ctx[None], (N_DEV, d))], axis=0)
    ada = _ada_forward(craw, w_ada[0], "ada_forward")
    ada_all = _all_gather_small(ada, "gather_ada")
    mod_all = ada_all.transpose(1, 0, 2).reshape(ada.shape[0], -1) + b_ada[0]
    mod = lax.dynamic_index_in_dim(mod_all, me, 0, keepdims=False)
    mod_c = mod_all[N_DEV]
    shift, scale, gate = mod[:d], mod[d:2 * d], mod[2 * d:]
    zeros6 = jnp.zeros((SUBLANES - 2, d), _F32)
    mods = jnp.stack([jnp.concatenate([mod_c[None, d:2 * d], mod_c[None, :d], zeros6], axis=0),
                      jnp.concatenate([scale[None], shift[None], zeros6], axis=0)])

    a_s, a_r, wg, tok = _w_in_hop(wg, cw, [], "a", ada_all, "gather_w_in_a")
    h_all = _prenorm(ctx2, x2, norm_pre, mods + tok[0, 0], tr2, "prenorm")
    order_ids = jnp.stack([_dev_index(*dev) for dev in _w_in_order(xi, yi, ci)]).astype(jnp.int32)
    proj = lax.empty((t, N_DEV * cw), _F32)
    proj = _proj_blocks(h_all, wg, proj, order_ids, 0, 1, cw, tok, "proj_0")
    b_s, b_r, wg, tok = _w_in_hop(wg, cw, [("a", a_s, a_r, [0, 1], [])], "b", proj, "gather_w_in_b")
    proj = _proj_blocks(h_all, wg, proj, order_ids, 1, 2, cw, tok, "proj_1")
    c_s, c_r, wg, tok = _w_in_hop(wg, cw, [("b", b_s, b_r, [2, 0], [])], "c", proj, "gather_w_in_c")
    proj = _proj_blocks(h_all, wg, proj, order_ids, 3, 2, cw, tok, "proj_2")
    d_s, d_r, wg, tok = _w_in_hop(
        wg, cw, [("c", c_s, c_r, [0], []), ("b", b_s, b_r, [1], [])], "d", proj, "gather_w_in_d")
    proj = _proj_blocks(h_all, wg, proj, order_ids, 5, 2, cw, tok, "proj_3")
    w_in_g, tok = _w_in_hop(
        wg, cw, [("d", d_s, d_r, [0], [0]), ("a", a_s, a_r, [], [0, 1]), ("b", b_s, b_r, [], [0, 1, 2]),
                 ("c", c_s, c_r, [], [0])], None, proj, "gather_w_in_end")
    flight_w = _gather_slabs_start(late, w_in_g, "gather_late_start")
    proj = _proj_blocks(h_all, w_in_g, proj, order_ids, 7, 1, cw, flight_w[-1], "proj_4")
    tables = _rope_tables(l, n)
    q, k, v = _qkv_post(proj, tables, q_norm, k_norm, heads, kv_heads, tr2, "qkv_post")
    attn_o, y, lse = _attention(q, k, v, proj, l, mix, _tile(l, 256, BF16_ROWS), "attention")
    w_out_g8, pool_g8 = _gather_slabs_wait(*flight_w[:3], attn_o, "gather_late_wait")
    w_out_g = w_out_g8.reshape(mix, d)
    pool_g = pool_g8.reshape(N_DEV, n_groups, pg // N_DEV, pg)
    raws, ds = [], []
    for gi in range(n_groups):
        y, raw, dsave = _pool_forward(gi, proj, y, pool_g, pool_scale, l, heads, kv_heads, tp, f"pool_forward_{gi}")
        raws.append(raw)
        ds.append(dsave)
    out = _matmul(y, w_out_g, name="out_proj")
    dxn, dout, dgate8, dgpost8, loss8 = _post(out, x2, target, gate[None], norm_post, tr2, "post")

    gw_out = _matmul(y, dout, ta=True, name="grad_w_out").reshape(N_DEV, mix // N_DEV, d)
    flight_so = _exchange_start(_sibling_copies_by_device, [gw_out], "exchange_sibling_start_w_out", land_slabs=N_CHIPS)
    dy = _matmul(dout, w_out_g, tb=True, after=flight_so[-1], name="d_y")
    gw_out, got_out = _exchange_wait(
        _sibling_copies_by_device, *flight_so[:4], dy, "exchange_sibling_wait_w_out", with_sources=True)
    sum_out = _pre_add(slab_ids, gw_out, got_out, "pre_add_w_out")
    flight_out = _exchange_start(_chip_copies, [sum_out], "exchange_chips_start_w_out")
    w3 = aw + 2 * kw
    dq, dproj, dk, dv = _attention_backward(
        q, k, v, attn_o, dy, proj, lse, flight_out[-1], lax.empty(proj.shape, _BF16), l, tq, "attention_backward")
    dproj, dgq8, dgk8 = _qkv_post_backward(proj, dq, dk, dv, tables, q_norm, k_norm, dproj, l, tr2, "qkv_post_backward")
    dproj = _zero_context_rows(dproj, l, w3, tr, "zero_context_rows")
    gpw, dps8 = [], []
    for gi in range(n_groups):
        dproj, draw, dd, dps = _pool_backward_gate(
            gi, dy, proj, raws[gi], pool_g, pool_scale, dproj, l, heads, kv_heads, tp, f"pool_backward_gate_{gi}")
        dproj = _pool_backward_window(gi, dd, dproj, l, (w3 + aw) // pg + gi, tp, f"pool_backward_window_{gi}")
        dps8.append(dps)
        gpw.append(_matmul(ds[gi], draw, ta=True, name=f"grad_pool_w_{gi}"))
    cw = w_in.shape[-1]
    other_ids = jnp.stack([_dev_index(*chip, 1 - ci) for chip in _chip_order(xi, yi)]).astype(jnp.int32)
    chip_slabs = jnp.arange(N_CHIPS, dtype=jnp.int32)
    pr = pool_w.shape[2]
    gpw8 = jnp.stack(gpw).reshape(n_groups, N_DEV, pr, pg).transpose(1, 0, 2, 3).reshape(N_DEV, n_groups * pr, pg)
    give_in = _matmul_slabs(h_all, dproj, other_ids, cw, "grad_w_in_sibling")
    flight_sib = _exchange_start(_sibling_copies, [give_in, jnp.take(gpw8, other_ids, axis=0)], "exchange_sibling_start")
    gw_in = _matmul_slabs(h_all, dproj, slab_ids, cw, "grad_w_in_own", after=flight_sib[-1])
    gpw_own = jnp.take(gpw8, slab_ids, axis=0)
    got_in, got_pw = _exchange_wait(_sibling_copies, *flight_sib[:4], gw_in, "exchange_sibling_wait")
    sums_in = [_pre_add(chip_slabs, gw_in, got_in, "pre_add_w_in"), _pre_add(chip_slabs, gpw_own, got_pw, "pre_add_pool_w")]
    flight_in = _exchange_start(_chip_copies, sums_in, "exchange_chips_start_w_in")
    dh = _matmul(dproj, w_in_g, tb=True, tm=1088, tk=3072, after=flight_in[-1], name="d_h")
    grad_x, dmods, dgpre8 = _prenorm_backward(dh, ctx2, x2, dxn, norm_pre, mods, tr2, "prenorm_backward")

    dmod_lat = jnp.concatenate([dmods[1, 1], dmods[1, 0], dgate8[0]])
    dmod_ctx = jnp.concatenate([dmods[0, 1], dmods[0, 0], jnp.zeros((d,), _F32)])
    small = jnp.concatenate([dmod_lat, dmod_ctx, dgpre8[0], dgpost8[0], dgq8[0], dgk8[0]] + [p[0] for p in dps8]
                            + [loss8[0, :1]])
    small_rows = _as_rows(small)
    small_buf = lax.dynamic_update_slice(jnp.zeros((N_DEV,) + small_rows.shape, _F32), small_rows[None], (me, 0, 0))
    flight_small = _gather_slabs_start([small_buf], grad_x, "gather_small_start")

    far_out = _exchange_wait(_chip_copies, *flight_out[:4], flight_small[-1], "exchange_chips_wait_w_out")[0]
    far_in, far_pw = _exchange_wait(_chip_copies, *flight_in[:4], far_out, "exchange_chips_wait_w_in")
    two = lambda a: a.reshape(-1, a.shape[-1])
    sharded = []
    for ids, g, got, far, w, m, v_, name in zip(
            (chip_slabs, slab_ids, chip_slabs), (gw_in, gw_out, gpw_own), (got_in, got_out, got_pw),
            (far_in, far_out, far_pw), (w_in, w_out, pool_w), (m_w_in, m_w_out, m_pool_w),
            (v_w_in, v_w_out, v_pool_w), ("adam_w_in", "adam_w_out", "adam_pool_w")):
        res = _adam_sharded(ids, g, got, far, two(w), two(m), two(v_), name)
        sharded.append([r.reshape(w.shape) for r in res])
    (g_w_in, dl_w_in, nm_w_in, nv_w_in), (g_w_out, dl_w_out, nm_w_out, nv_w_out), (g_pw, dl_pw, nm_pw, nv_pw) = sharded

    gathered = _gather_slabs_wait(*flight_small[:3], nv_pw, "gather_small_wait")[0].reshape(N_DEV, -1)
    o = 0
    take = lambda size: (gathered[:, o:o + size], o + size)
    g_mod, o = take(3 * d)
    g_modc, o = take(3 * d)
    g_pre, o = take(d)
    g_post, o = take(d)
    g_q, o = take(HEAD_DIM)
    g_k, o = take(HEAD_DIM)
    g_ps, o = take(n_groups * pg)
    g_loss, o = take(1)
    cols = w_ada.shape[-1]
    mine = lambda a: lax.dynamic_slice_in_dim(a, me * cols, cols, axis=1)
    dmod_rows = jnp.concatenate([mine(g_mod), mine(g_modc)], axis=0)
    g_wada, dl_wada, nm_wada, nv_wada, dcact = _ada_backward(craw, dmod_rows, w_ada[0], m_w_ada[0], v_w_ada[0], "ada_backward")
    dcc = _all_gather_small(_as_rows(dcact[0]), "gather_dcc").reshape(N_DEV, -1)[:, :d]

    sizes = [d, 3 * d, d, d, HEAD_DIM, HEAD_DIM, n_groups * pg]
    def pack(parts):
        rows = jnp.concatenate(parts, axis=1)
        padded = -(-rows.shape[1] // (SUBLANES * LANES)) * SUBLANES * LANES
        return jnp.pad(rows, ((0, 0), (0, padded - rows.shape[1]))).reshape(N_DEV, padded // LANES, LANES)

    zero = lambda size: jnp.zeros((N_DEV, size), _F32)
    parts = pack([dcc, g_mod, g_pre, g_post, g_q, g_k, g_ps])
    extra = pack([zero(d), g_modc, zero(d), zero(d), zero(HEAD_DIM), zero(HEAD_DIM), zero(n_groups * pg)])
    through_silu = _as_rows(jnp.concatenate([jnp.ones((d,), _F32), jnp.zeros((sum(sizes[1:]),), _F32)]))
    cat = lambda items: _as_rows(jnp.concatenate([a.reshape(-1) for a in items]))
    ws = [c_ctx, b_ada, norm_pre, norm_post, q_norm, k_norm, pool_scale]
    ms = [m_c_ctx, m_b_ada, m_norm_pre, m_norm_post, m_q_norm, m_k_norm, m_pool_scale]
    vs = [v_c_ctx, v_b_ada, v_norm_pre, v_norm_post, v_q_norm, v_k_norm, v_pool_scale]
    rep = _adam_replicated(parts, extra, through_silu, cat(ws), cat(ms), cat(vs), "adam_replicated")

    def split(packed):
        flat_, outs, at = packed.reshape(-1), [], 0
        for w, size in zip(ws, sizes):
            outs.append(flat_[at:at + size].reshape(w.shape))
            at += size
        return outs

    g_rep, dl_rep, nm_rep, nv_rep = [split(r) for r in rep]

    loss_sum = g_loss[0, 0]
    for dev in range(1, N_DEV):
        loss_sum = loss_sum + g_loss[dev, 0]
    loss = (0.5 / d) * loss_sum

    def ordered(rep_list, ada_, w_in_, pw_, w_out_):
        return [rep_list[0], ada_[None], rep_list[1], rep_list[2], rep_list[3], w_in_, rep_list[4], rep_list[5],
                pw_, rep_list[6], w_out_]

    return (loss, grad_x[None],
            *ordered(g_rep, g_wada, g_w_in, g_pw, g_w_out),
            *ordered(dl_rep, dl_wada, dl_w_in, dl_pw, dl_w_out),
            *ordered(nm_rep, nm_wada, nm_w_in, nm_pw, nm_w_out),
            *ordered(nv_rep, nv_wada, nv_w_in, nv_pw, nv_w_out))
```
